```python
import jax, jax.numpy as jnp
from jax import lax
import numpy as np

D_MODEL = 1024
BATCH = 16
SEQ = 2048
DEPTH = 1

HGRN_EXPAND = 128
HGRN_WIDTH = D_MODEL
HGRN_HEADS = HGRN_WIDTH // HGRN_EXPAND
HGRN_HEAD_V = HGRN_WIDTH // HGRN_HEADS
HGRN_CHUNK = 16
POOL_WINDOWS = (2, 4, 8, 16)
POOL_GROUPS = len(POOL_WINDOWS)
POOL_WIDTH = D_MODEL
POOL_GROUP_DIM = POOL_WIDTH // POOL_GROUPS
D_FF = 4 * D_MODEL
IN_COLS = 4 * HGRN_WIDTH + POOL_WIDTH + 2 * D_MODEL
DEEPNORM_ALPHA = (2.0 * DEPTH) ** 0.25
DEEPNORM_BETA = (8.0 * DEPTH) ** -0.25
LN_EPS = 1e-5
RMS_EPS = 1e-6

kernel_name = "hgrn2_multiscale_pool_gated_hybrid_deepnorm"


def layer_norm(x, g, b):
    xf = x.astype(jnp.float32)
    mu = jnp.mean(xf, axis=-1, keepdims=True)
    var = jnp.mean(jnp.square(xf - mu), axis=-1, keepdims=True)
    y = (xf - mu) * lax.rsqrt(var + LN_EPS)
    return (y * g.astype(jnp.float32) + b.astype(jnp.float32)).astype(x.dtype)


def hgrn2_chunked(q, k, v, log_f):
    B, S, H, dk = q.shape
    dv = v.shape[-1]
    n = S // HGRN_CHUNK

    def to_chunks(t):
        return t.reshape(B, n, HGRN_CHUNK, H, t.shape[-1]).transpose(1, 0, 3, 2, 4)

    q, k, v, log_f = to_chunks(q), to_chunks(k), to_chunks(v), to_chunks(log_f)
    G = jnp.cumsum(log_f, axis=3)
    G_last = G[:, :, :, -1:, :]
    q_dec = q * jnp.exp(G)
    k_inv = k * jnp.exp(-G)
    k_to_end = k * jnp.exp(G_last - G)
    causal = jnp.tril(jnp.ones((HGRN_CHUNK, HGRN_CHUNK), dtype=bool))
    scores = jnp.einsum('nbhtd,nbhsd->nbhts', q_dec, k_inv)
    scores = jnp.where(causal, scores, 0.0)
    o_intra = jnp.einsum('nbhts,nbhsv->nbhtv', scores, v)

    def step(state, xs):
        q_c, k_c, v_c, decay_c = xs
        o_inter = jnp.einsum('bhtd,bhdv->bhtv', q_c, state)
        new_state = jnp.swapaxes(decay_c, -1, -2) * state + jnp.einsum('bhsd,bhsv->bhdv', k_c, v_c)
        return new_state, o_inter

    state0 = jnp.zeros((B, H, dk, dv), jnp.float32)
    _, o_inter = lax.scan(step, state0, (q_dec, k_to_end, v, jnp.exp(G_last)))
    o = o_intra + o_inter
    return o.transpose(1, 0, 3, 2, 4).reshape(B, S, H, dv)


def causal_multiscale_pool(v):
    B, S, _ = v.shape
    vg = v.reshape(B, S, POOL_GROUPS, POOL_GROUP_DIM).astype(jnp.float32)
    csum = jnp.cumsum(vg, axis=1)
    pos = jnp.arange(S)
    outs = []
    for g, w in enumerate(POOL_WINDOWS):
        c = csum[:, :, g]
        lagged = jnp.pad(c, ((0, 0), (w, 0), (0, 0)))[:, :S]
        count = jnp.minimum(pos + 1, w).astype(jnp.float32)[:, None]
        outs.append((c - lagged) / count - vg[:, :, g])
    return jnp.stack(outs, axis=2).astype(v.dtype)


def hybrid_mixer(x, w_in, lower_bound, hgrn_norm_g, w_a, w_pool, pool_scale, w_out):
    B, S, _ = x.shape
    proj = x @ w_in
    hw = HGRN_WIDTH
    splits = [hw, 2 * hw, 3 * hw, 4 * hw, 4 * hw + POOL_WIDTH, 4 * hw + POOL_WIDTH + D_MODEL]
    q, f_pre, i_val, o_gate, pool_v, gate_a, gate_b = jnp.split(proj, splits, axis=-1)

    qf = jax.nn.silu(q.astype(jnp.float32)) * (HGRN_EXPAND ** -0.5)
    lb = lower_bound.astype(jnp.float32)
    f = lb + (1.0 - lb) * jax.nn.sigmoid(f_pre.astype(jnp.float32))
    k = 1.0 - f
    log_f = jnp.log(f)
    shp = (B, S, HGRN_HEADS, HGRN_EXPAND)
    o = hgrn2_chunked(qf.reshape(shp), k.reshape(shp),
                      i_val.astype(jnp.float32).reshape(B, S, HGRN_HEADS, HGRN_HEAD_V), log_f.reshape(shp))
    o = o * lax.rsqrt(jnp.mean(jnp.square(o), axis=-1, keepdims=True) + RMS_EPS)
    o = o.reshape(B, S, hw) * hgrn_norm_g.astype(jnp.float32) * jax.nn.sigmoid(o_gate.astype(jnp.float32))
    a = o.astype(x.dtype) @ w_a

    pooled = causal_multiscale_pool(pool_v)
    b = jnp.einsum('bsgc,gcd->bsgd', pooled, w_pool).reshape(B, S, POOL_WIDTH) * pool_scale

    merged = jax.nn.sigmoid(gate_a) * a + jax.nn.sigmoid(gate_b) * b
    return merged @ w_out


def sq_relu_mlp(x, w_up, w_down):
    return jnp.square(jax.nn.relu(x @ w_up)) @ w_down


def _fwd_setup_inputs(seed: int = 0) -> dict:
    key = jax.random.key(seed)
    ks = jax.random.split(key, 15)
    f32 = jnp.float32
    nrm = lambda k, s: jax.random.normal(k, s, f32)
    return {
        "x": nrm(ks[0], (BATCH, SEQ, D_MODEL)),
        "w_in": nrm(ks[1], (DEPTH, D_MODEL, IN_COLS)) * D_MODEL ** -0.5,
        "lb_logits": nrm(ks[2], (DEPTH + 1, HGRN_WIDTH)) * 0.5,
        "hgrn_norm_g": 1.0 + 0.05 * nrm(ks[3], (DEPTH, HGRN_WIDTH)),
        "w_a": nrm(ks[4], (DEPTH, HGRN_WIDTH, D_MODEL)) * HGRN_WIDTH ** -0.5,
        "w_pool": nrm(ks[5], (DEPTH, POOL_GROUPS, POOL_GROUP_DIM, POOL_GROUP_DIM)) * POOL_GROUP_DIM ** -0.5,
        "pool_scale": 1.0 + 0.05 * nrm(ks[6], (DEPTH, POOL_WIDTH)),
        "w_out": nrm(ks[7], (DEPTH, D_MODEL, D_MODEL)) * (D_MODEL ** -0.5) * DEEPNORM_BETA,
        "ln1_g": 1.0 + 0.05 * nrm(ks[8], (DEPTH, D_MODEL)),
        "ln1_b": 0.02 * nrm(ks[9], (DEPTH, D_MODEL)),
        "w_up": nrm(ks[10], (DEPTH, D_MODEL, D_FF)) * D_MODEL ** -0.5,
        "w_down": nrm(ks[11], (DEPTH, D_FF, D_MODEL)) * (D_FF ** -0.5) * DEEPNORM_BETA,
        "ln2_g": 1.0 + 0.05 * nrm(ks[12], (DEPTH, D_MODEL)),
        "ln2_b": 0.02 * nrm(ks[13], (DEPTH, D_MODEL)),
    }


def _fwd_reference(x, w_in, lb_logits, hgrn_norm_g, w_a, w_pool, pool_scale, w_out,
              ln1_g, ln1_b, w_up, w_down, ln2_g, ln2_b):
    lower_bounds = jnp.cumsum(jax.nn.softmax(lb_logits.astype(jnp.float32), axis=0), axis=0)
    for l in range(DEPTH):
        mix = hybrid_mixer(x, w_in[l], lower_bounds[l], hgrn_norm_g[l], w_a[l], w_pool[l],
                           pool_scale[l], w_out[l])
        x = layer_norm(DEEPNORM_ALPHA * x + mix, ln1_g[l], ln1_b[l])
        x = layer_norm(DEEPNORM_ALPHA * x + sq_relu_mlp(x, w_up[l], w_down[l]), ln2_g[l], ln2_b[l])
    return x


import jax as _jax
import jax.numpy as _jnp

TWIN_FORMAT = 'train_step'
FWD_PARAMS = ['x', 'w_in', 'lb_logits', 'hgrn_norm_g', 'w_a', 'w_pool', 'pool_scale', 'w_out', 'ln1_g', 'ln1_b', 'w_up', 'w_down', 'ln2_g', 'ln2_b']
TWIN_WEIGHTS = ['w_in', 'lb_logits', 'hgrn_norm_g', 'w_a', 'w_pool', 'pool_scale', 'w_out', 'ln1_g', 'ln1_b', 'w_up', 'w_down', 'ln2_g', 'ln2_b']
TWIN_DIFF_INPUT = 'x'
TWIN_INPUTS = ['x', 'w_in', 'lb_logits', 'hgrn_norm_g', 'w_a', 'w_pool', 'pool_scale', 'w_out', 'ln1_g', 'ln1_b', 'w_up', 'w_down', 'ln2_g', 'ln2_b', 'loss_target', 'm_w_in', 'm_lb_logits', 'm_hgrn_norm_g', 'm_w_a', 'm_w_pool', 'm_pool_scale', 'm_w_out', 'm_ln1_g', 'm_ln1_b', 'm_w_up', 'm_w_down', 'm_ln2_g', 'm_ln2_b', 'v_w_in', 'v_lb_logits', 'v_hgrn_norm_g', 'v_w_a', 'v_w_pool', 'v_pool_scale', 'v_w_out', 'v_ln1_g', 'v_ln1_b', 'v_w_up', 'v_w_down', 'v_ln2_g', 'v_ln2_b']
TWIN_OUTPUTS = ['loss', 'grad_x', 'grad_w_in', 'grad_lb_logits', 'grad_hgrn_norm_g', 'grad_w_a', 'grad_w_pool', 'grad_pool_scale', 'grad_w_out', 'grad_ln1_g', 'grad_ln1_b', 'grad_w_up', 'grad_w_down', 'grad_ln2_g', 'grad_ln2_b', 'delta_w_in', 'delta_lb_logits', 'delta_hgrn_norm_g', 'delta_w_a', 'delta_w_pool', 'delta_pool_scale', 'delta_w_out', 'delta_ln1_g', 'delta_ln1_b', 'delta_w_up', 'delta_w_down', 'delta_ln2_g', 'delta_ln2_b', 'new_m_w_in', 'new_m_lb_logits', 'new_m_hgrn_norm_g', 'new_m_w_a', 'new_m_w_pool', 'new_m_pool_scale', 'new_m_w_out', 'new_m_ln1_g', 'new_m_ln1_b', 'new_m_w_up', 'new_m_w_down', 'new_m_ln2_g', 'new_m_ln2_b', 'new_v_w_in', 'new_v_lb_logits', 'new_v_hgrn_norm_g', 'new_v_w_a', 'new_v_w_pool', 'new_v_pool_scale', 'new_v_w_out', 'new_v_ln1_g', 'new_v_ln1_b', 'new_v_w_up', 'new_v_w_down', 'new_v_ln2_g', 'new_v_ln2_b']
TWIN_LEAF_KINDS = {'loss': 'loss', 'grad_x': 'grad_x', 'grad_w_in': 'grad_w', 'grad_lb_logits': 'grad_w', 'grad_hgrn_norm_g': 'grad_w', 'grad_w_a': 'grad_w', 'grad_w_pool': 'grad_w', 'grad_pool_scale': 'grad_w', 'grad_w_out': 'grad_w', 'grad_ln1_g': 'grad_w', 'grad_ln1_b': 'grad_w', 'grad_w_up': 'grad_w', 'grad_w_down': 'grad_w', 'grad_ln2_g': 'grad_w', 'grad_ln2_b': 'grad_w', 'delta_w_in': 'delta_w', 'delta_lb_logits': 'delta_w', 'delta_hgrn_norm_g': 'delta_w', 'delta_w_a': 'delta_w', 'delta_w_pool': 'delta_w', 'delta_pool_scale': 'delta_w', 'delta_w_out': 'delta_w', 'delta_ln1_g': 'delta_w', 'delta_ln1_b': 'delta_w', 'delta_w_up': 'delta_w', 'delta_w_down': 'delta_w', 'delta_ln2_g': 'delta_w', 'delta_ln2_b': 'delta_w', 'new_m_w_in': 'new_m', 'new_m_lb_logits': 'new_m', 'new_m_hgrn_norm_g': 'new_m', 'new_m_w_a': 'new_m', 'new_m_w_pool': 'new_m', 'new_m_pool_scale': 'new_m', 'new_m_w_out': 'new_m', 'new_m_ln1_g': 'new_m', 'new_m_ln1_b': 'new_m', 'new_m_w_up': 'new_m', 'new_m_w_down': 'new_m', 'new_m_ln2_g': 'new_m', 'new_m_ln2_b': 'new_m', 'new_v_w_in': 'new_v', 'new_v_lb_logits': 'new_v', 'new_v_hgrn_norm_g': 'new_v', 'new_v_w_a': 'new_v', 'new_v_w_pool': 'new_v', 'new_v_pool_scale': 'new_v', 'new_v_w_out': 'new_v', 'new_v_ln1_g': 'new_v', 'new_v_ln1_b': 'new_v', 'new_v_w_up': 'new_v', 'new_v_w_down': 'new_v', 'new_v_ln2_g': 'new_v', 'new_v_ln2_b': 'new_v'}


def _forward(args):
    return _fwd_reference(*[args[k] for k in FWD_PARAMS])


def _output_shape():
    out = _jax.eval_shape(lambda: _forward(_fwd_setup_inputs(0)))
    return out.shape, out.dtype

N_MICROBATCH = 1
ADAM_LR = 0.001
ADAM_B1 = 0.9
ADAM_B2 = 0.999
ADAM_EPS = 1e-08
ADAM_WD = 0.01
ADAM_STEP = 10
PER_EXAMPLE_BATCH_AXIS = {'x': 0, 'loss_target': 0}
SHARED_INPUTS = []
_WEIGHT_DTYPES = {'w_in': _jnp.float32, 'lb_logits': _jnp.float32, 'hgrn_norm_g': _jnp.float32, 'w_a': _jnp.float32, 'w_pool': _jnp.float32, 'pool_scale': _jnp.float32, 'w_out': _jnp.float32, 'ln1_g': _jnp.float32, 'ln1_b': _jnp.float32, 'w_up': _jnp.float32, 'w_down': _jnp.float32, 'ln2_g': _jnp.float32, 'ln2_b': _jnp.float32}
MOMENT_SCALE = {'w_in': 2.104041e-02, 'lb_logits': 2.396926e-03, 'hgrn_norm_g': 2.658874e-02, 'w_a': 2.668867e-02, 'w_pool': 4.354807e-02, 'pool_scale': 4.212213e-02, 'w_out': 8.572265e-02, 'ln1_g': 2.109205e+00, 'ln1_b': 6.191599e-01, 'w_up': 5.443926e-02, 'w_down': 2.462484e-01, 'ln2_g': 3.226480e+01, 'ln2_b': 6.716886e+00}


def _to_microbatches(a, axis):
    t = _jnp.moveaxis(a, axis, 0)
    t = t.reshape((N_MICROBATCH, t.shape[0] // N_MICROBATCH) + t.shape[1:])
    return _jnp.moveaxis(t, 1, axis + 1)


def setup_inputs(seed: int = 0) -> dict:
    inp = _fwd_setup_inputs(seed)
    key = _jax.random.fold_in(_jax.random.key(seed), 7919)
    shape, _ = _output_shape()
    out = dict(inp)
    out["loss_target"] = _jax.random.normal(_jax.random.fold_in(key, 0), shape, _jnp.float32)
    for i, name in enumerate(TWIN_WEIGHTS):
        w = inp[name].astype(_jnp.float32)
        if MOMENT_SCALE is None:
            s = _jnp.sqrt(_jnp.mean(_jnp.square(w)) + 1e-30)
        else:
            s = MOMENT_SCALE[name]
        km, kv = _jax.random.split(_jax.random.fold_in(key, i + 1))
        out[name] = w
        out["m_" + name] = s * _jax.random.normal(km, w.shape, _jnp.float32)
        out["v_" + name] = (s * s) * _jax.random.uniform(kv, w.shape, _jnp.float32, 0.5, 1.5)
    if N_MICROBATCH > 1:
        for name, axis in PER_EXAMPLE_BATCH_AXIS.items():
            out[name] = _to_microbatches(out[name], axis)
    return {'x': out['x'], 'w_in': out['w_in'], 'lb_logits': out['lb_logits'], 'hgrn_norm_g': out['hgrn_norm_g'], 'w_a': out['w_a'], 'w_pool': out['w_pool'], 'pool_scale': out['pool_scale'], 'w_out': out['w_out'], 'ln1_g': out['ln1_g'], 'ln1_b': out['ln1_b'], 'w_up': out['w_up'], 'w_down': out['w_down'], 'ln2_g': out['ln2_g'], 'ln2_b': out['ln2_b'], 'loss_target': out['loss_target'], 'm_w_in': out['m_w_in'], 'm_lb_logits': out['m_lb_logits'], 'm_hgrn_norm_g': out['m_hgrn_norm_g'], 'm_w_a': out['m_w_a'], 'm_w_pool': out['m_w_pool'], 'm_pool_scale': out['m_pool_scale'], 'm_w_out': out['m_w_out'], 'm_ln1_g': out['m_ln1_g'], 'm_ln1_b': out['m_ln1_b'], 'm_w_up': out['m_w_up'], 'm_w_down': out['m_w_down'], 'm_ln2_g': out['m_ln2_g'], 'm_ln2_b': out['m_ln2_b'], 'v_w_in': out['v_w_in'], 'v_lb_logits': out['v_lb_logits'], 'v_hgrn_norm_g': out['v_hgrn_norm_g'], 'v_w_a': out['v_w_a'], 'v_w_pool': out['v_w_pool'], 'v_pool_scale': out['v_pool_scale'], 'v_w_out': out['v_w_out'], 'v_ln1_g': out['v_ln1_g'], 'v_ln1_b': out['v_ln1_b'], 'v_w_up': out['v_w_up'], 'v_w_down': out['v_w_down'], 'v_ln2_g': out['v_ln2_g'], 'v_ln2_b': out['v_ln2_b']}


def _loss(weights, diff, rest, loss_target):
    with _jax.named_scope("forward"):
        args = {**rest, TWIN_DIFF_INPUT: diff, **{k: w.astype(_WEIGHT_DTYPES[k]) for k, w in weights.items()}}
        y = _forward(args)
    with _jax.named_scope("loss_head"):
        err = _jnp.square(y.astype(_jnp.float32) - loss_target)
        return 0.5 * _jnp.sum(_jnp.mean(err, axis=-1)) if err.ndim else 0.5 * err


def _adamw(w, g, m, v):
    m = ADAM_B1 * m + (1.0 - ADAM_B1) * g
    v = ADAM_B2 * v + (1.0 - ADAM_B2) * _jnp.square(g)
    m_hat = m / (1.0 - ADAM_B1 ** ADAM_STEP)
    v_hat = v / (1.0 - ADAM_B2 ** ADAM_STEP)
    delta = -ADAM_LR * (m_hat / (_jnp.sqrt(v_hat) + ADAM_EPS) + ADAM_WD * w)
    return delta, m, v


def reference(x, w_in, lb_logits, hgrn_norm_g, w_a, w_pool, pool_scale, w_out, ln1_g, ln1_b, w_up, w_down, ln2_g, ln2_b, loss_target, m_w_in, m_lb_logits, m_hgrn_norm_g, m_w_a, m_w_pool, m_pool_scale, m_w_out, m_ln1_g, m_ln1_b, m_w_up, m_w_down, m_ln2_g, m_ln2_b, v_w_in, v_lb_logits, v_hgrn_norm_g, v_w_a, v_w_pool, v_pool_scale, v_w_out, v_ln1_g, v_ln1_b, v_w_up, v_w_down, v_ln2_g, v_ln2_b):
    given = dict(x=x, w_in=w_in, lb_logits=lb_logits, hgrn_norm_g=hgrn_norm_g, w_a=w_a, w_pool=w_pool, pool_scale=pool_scale, w_out=w_out, ln1_g=ln1_g, ln1_b=ln1_b, w_up=w_up, w_down=w_down, ln2_g=ln2_g, ln2_b=ln2_b, loss_target=loss_target, m_w_in=m_w_in, m_lb_logits=m_lb_logits, m_hgrn_norm_g=m_hgrn_norm_g, m_w_a=m_w_a, m_w_pool=m_w_pool, m_pool_scale=m_pool_scale, m_w_out=m_w_out, m_ln1_g=m_ln1_g, m_ln1_b=m_ln1_b, m_w_up=m_w_up, m_w_down=m_w_down, m_ln2_g=m_ln2_g, m_ln2_b=m_ln2_b, v_w_in=v_w_in, v_lb_logits=v_lb_logits, v_hgrn_norm_g=v_hgrn_norm_g, v_w_a=v_w_a, v_w_pool=v_w_pool, v_pool_scale=v_pool_scale, v_w_out=v_w_out, v_ln1_g=v_ln1_g, v_ln1_b=v_ln1_b, v_w_up=v_w_up, v_w_down=v_w_down, v_ln2_g=v_ln2_g, v_ln2_b=v_ln2_b)
    weights = {n: given[n] for n in TWIN_WEIGHTS}
    shared = {n: given[n] for n in SHARED_INPUTS}
    per_example = {n: given[n] for n in ['x']}
    grad_fn = _jax.value_and_grad(_loss, argnums=(0, 1))

    def one_microbatch(ex, loss_target):
        ex = dict(ex)
        diff = ex.pop(TWIN_DIFF_INPUT)
        return grad_fn(weights, diff, {**shared, **ex}, loss_target)

    if N_MICROBATCH == 1:
        loss, (grad_w, grad_x) = one_microbatch(per_example, given["loss_target"])
    else:
        def body(carry, xs):
            loss_sum, grad_sum = carry
            l_k, (gw_k, gx_k) = one_microbatch(xs[0], xs[1])
            with _jax.named_scope("update"):
                return (loss_sum + l_k, _jax.tree.map(_jnp.add, grad_sum, gw_k)), gx_k

        init = (_jnp.zeros((), _jnp.float32), _jax.tree.map(_jnp.zeros_like, weights))
        (loss, grad_w), grad_x = _jax.lax.scan(body, init, (per_example, given["loss_target"]))
    with _jax.named_scope("update"):
        delta_w, new_m, new_v = {}, {}, {}
        for n in TWIN_WEIGHTS:
            delta_w[n], new_m[n], new_v[n] = _adamw(weights[n], grad_w[n], given["m_" + n], given["v_" + n])
    return (loss, grad_x, *[grad_w[n] for n in TWIN_WEIGHTS], *[delta_w[n] for n in TWIN_WEIGHTS],
            *[new_m[n] for n in TWIN_WEIGHTS], *[new_v[n] for n in TWIN_WEIGHTS])
```

```python
import jax
import jax.numpy as jnp
from jax import lax
from jax.experimental import pallas as pl
from jax.experimental.pallas import tpu as pltpu

F32 = jnp.float32
BF16 = jnp.bfloat16
MESH = pl.DeviceIdType.MESH

N_DEV = 8
N_CHIP = 4
HEAD = 128
CHUNK = 16
GROUP = 128
CH_PER_GROUP = GROUP // CHUNK
N_SEC = 7
POOL_GROUPS = 4
ALPHA = (2.0 * 1) ** 0.25
LN_EPS = 1e-5
RMS_EPS = 1e-6
Q_SCALE = HEAD ** -0.5
ADAM_LR = 0.001
ADAM_B1 = 0.9
ADAM_B2 = 0.999
ADAM_EPS = 1e-08
ADAM_WD = 0.01
ADAM_STEP = 10
VMEM_LIMIT = 56 << 20

NT_DIMS = (((1,), (1,)), ((), ()))
TN_DIMS = (((0,), (0,)), ((), ()))


def _params(sem=None):
    kw = dict(vmem_limit_bytes=VMEM_LIMIT)
    if sem is not None:
        kw["dimension_semantics"] = sem
    return pltpu.CompilerParams(**kw)


def _me():
    return lax.axis_index("x"), lax.axis_index("y"), lax.axis_index("c")


def _sigmoid(v):
    return jax.nn.sigmoid(v)


def _adamw(w, g, m, v):
    m = ADAM_B1 * m + (1.0 - ADAM_B1) * g
    v = ADAM_B2 * v + (1.0 - ADAM_B2) * jnp.square(g)
    m_hat = m / (1.0 - ADAM_B1 ** ADAM_STEP)
    v_hat = v / (1.0 - ADAM_B2 ** ADAM_STEP)
    delta = -ADAM_LR * (m_hat / (jnp.sqrt(v_hat) + ADAM_EPS) + ADAM_WD * w)
    return delta, m, v


def _all_gather(shards):
    nw = len(shards)

    def body(*refs):
        ins, outs = refs[:nw], refs[nw:2 * nw]
        send_sems, recv_sems, local_sems = refs[2 * nw:]
        x, y, c = _me()
        me = (x, y, c)
        sibling = (x, y, 1 - c)
        chips = [(1 - x, y), (x, 1 - y), (1 - x, 1 - y)]

        def slot(px, py, pc):
            return 4 * px + 2 * py + pc

        def copy(w, k, block, to, src=None):
            dst = outs[w].at[slot(*block)]
            return pltpu.make_async_remote_copy(
                src_ref=dst if src is None else src, dst_ref=dst,
                send_sem=send_sems.at[w, k], recv_sem=recv_sems.at[w, k],
                device_id=to, device_id_type=MESH)

        local = []
        for w in range(nw):
            mine = pltpu.make_async_copy(ins[w], outs[w].at[slot(*me)], local_sems.at[w])
            mine.start()
            local.append(mine)
        first = []
        for w in range(nw):
            first.append(copy(w, 0, me, sibling, src=ins[w]))
            first += [copy(w, 1 + j, me, (*chip, c), src=ins[w]) for j, chip in enumerate(chips)]
        for cp in first:
            cp.start()
        passed = []
        for w in range(nw):
            for j, chip in enumerate(chips):
                copy(w, 1 + j, (*chip, c), me).wait_recv()
                fwd = copy(w, 4 + j, (*chip, c), sibling)
                fwd.start()
                passed.append(fwd)
        for w in range(nw):
            copy(w, 0, sibling, me).wait_recv()
            for j, chip in enumerate(chips):
                copy(w, 4 + j, (*chip, 1 - c), me).wait_recv()
        for cp in first + passed:
            cp.wait_send()
        for cp in local:
            cp.wait()

    any_spec = pl.BlockSpec(memory_space=pl.ANY)
    return pl.pallas_call(
        body, name="ag_weights",
        out_shape=[jax.ShapeDtypeStruct((N_DEV,) + s.shape, s.dtype) for s in shards],
        in_specs=[any_spec] * nw, out_specs=[any_spec] * nw,
        scratch_shapes=[pltpu.SemaphoreType.DMA((nw, 7)), pltpu.SemaphoreType.DMA((nw, 7)),
                        pltpu.SemaphoreType.DMA((nw,))],
    )(*shards)


def _proj(x2, w_in):
    T, D = x2.shape
    tm = min(512, T)

    def body(x_ref, w_ref, o_ref):
        o_ref[0] = jnp.dot(x_ref[...].astype(BF16), w_ref[...], preferred_element_type=F32)

    return pl.pallas_call(
        body, name="proj", grid=(N_SEC, T // tm),
        in_specs=[pl.BlockSpec((tm, D), lambda j, i: (i, 0)),
                  pl.BlockSpec((D, D), lambda j, i: (0, j))],
        out_specs=pl.BlockSpec((1, tm, D), lambda j, i: (j, i, 0)),
        out_shape=jax.ShapeDtypeStruct((N_SEC, T, D), F32),
        compiler_params=_params(("parallel", "parallel")),
    )(x2, w_in)


def _chunk_cumsum(v, reverse=False):
    rows = v.shape[0]
    pos = lax.broadcasted_iota(jnp.int32, v.shape, 0) % CHUNK
    for sh in (1, 2, 4, 8):
        if reverse:
            v = v + jnp.where(pos < CHUNK - sh, pltpu.roll(v, rows - sh, 0), 0.0)
        else:
            v = v + jnp.where(pos >= sh, pltpu.roll(v, sh, 0), 0.0)
    return v


def _hgrn_gates(q, f_pre, lb_logits):
    l0, l1 = lb_logits[0:1, :], lb_logits[1:2, :]
    mx = jnp.maximum(l0, l1)
    e0, e1 = jnp.exp(l0 - mx), jnp.exp(l1 - mx)
    lb = e0 / (e0 + e1)
    sq = _sigmoid(q)
    qf = q * sq * Q_SCALE
    sg = _sigmoid(f_pre)
    f = lb + (1.0 - lb) * sg
    k = 1.0 - f
    log_f = jnp.log(f)
    G = _chunk_cumsum(log_f)
    g_to_end = _chunk_cumsum(log_f, reverse=True) - log_f
    e_g = jnp.exp(G)
    e_ng = jnp.exp(-G)
    e_ge = jnp.exp(g_to_end)
    return dict(lb=lb, sq=sq, qf=qf, sg=sg, f=f, k=k, G=G, e_g=e_g, e_ng=e_ng, e_ge=e_ge,
                qd=qf * e_g, ki=k * e_ng, ke=k * e_ge, dec=jnp.exp(G + g_to_end))


def _intra_mask():
    r = lax.broadcasted_iota(jnp.int32, (GROUP, GROUP), 0)
    c = lax.broadcasted_iota(jnp.int32, (GROUP, GROUP), 1)
    return (r // CHUNK == c // CHUNK) & (c <= r)


def _chunk_outer(lhs_rows, rhs_b, out_scr, sb):
    lane = lax.broadcasted_iota(jnp.int32, (GROUP, GROUP), 1) // CHUNK
    for g in range(sb // GROUP):
        sl = slice(g * GROUP, (g + 1) * GROUP)
        lhs_t = lhs_rows[sl].T
        for cc in range(CH_PER_GROUP):
            masked = jnp.where(lane == cc, lhs_t, 0.0).astype(BF16)
            out_scr[g * CH_PER_GROUP + cc] = jnp.dot(masked, rhs_b[sl], preferred_element_type=F32)


def _hgrn_forward_block(c, v, st0, sb, o_scr, kv_scr, st_scr, dec_scr, qd_scr):
    nc = sb // CHUNK
    qd_b, ki_b, ke_b, v_b = (c["qd"].astype(BF16), c["ki"].astype(BF16), c["ke"].astype(BF16),
                             v.astype(BF16))
    mask = _intra_mask()
    for g in range(sb // GROUP):
        sl = slice(g * GROUP, (g + 1) * GROUP)
        sc = lax.dot_general(qd_b[sl], ki_b[sl], NT_DIMS, preferred_element_type=F32)
        a = jnp.where(mask, sc, 0.0).astype(BF16)
        o_scr[sl, :] = jnp.dot(a, v_b[sl], preferred_element_type=F32)
    _chunk_outer(v, ke_b, kv_scr, sb)
    dec_scr[...] = c["dec"]
    qd_scr[...] = qd_b

    def rec(n, st):
        st_scr[n] = st
        d = dec_scr[pl.ds(pl.multiple_of(n * CHUNK, CHUNK), 1), :]
        return st * d + kv_scr[n]

    st_end = lax.fori_loop(0, nc, rec, st0)

    def inter(n, carry):
        rows = pl.ds(pl.multiple_of(n * CHUNK, CHUNK), CHUNK)
        o_scr[rows, :] += lax.dot_general(qd_scr[rows, :], st_scr[n].astype(BF16), NT_DIMS,
                                          preferred_element_type=F32)
        return carry

    lax.fori_loop(0, nc, inter, 0)
    return st_end


def _hgrn_fwd(proj5, lb_logits, gn):
    _, Bl, S, D = proj5.shape
    H = D // HEAD
    sb = min(512, S)
    nsb = S // sb
    nc = sb // CHUNK

    def body(p_ref, lbl_ref, gn_ref, ain_ref, st0_ref, carry, o_scr, kv_scr, st_scr, dec_scr, qd_scr):
        s = pl.program_id(2)

        @pl.when(s == 0)
        def _():
            carry[...] = jnp.zeros_like(carry)

        st0 = carry[...]
        st0_ref[0, 0, 0] = st0
        c = _hgrn_gates(p_ref[0, 0], p_ref[1, 0], lbl_ref[...])
        carry[...] = _hgrn_forward_block(c, p_ref[2, 0], st0, sb, o_scr, kv_scr, st_scr, dec_scr, qd_scr)
        o = o_scr[...]
        rinv = lax.rsqrt(jnp.mean(o * o, axis=-1, keepdims=True) + RMS_EPS)
        ain_ref[0] = (o * rinv * gn_ref[...] * _sigmoid(p_ref[3, 0])).astype(BF16)

    return pl.pallas_call(
        body, name="hgrn_fwd", grid=(H, Bl, nsb),
        in_specs=[pl.BlockSpec((4, 1, sb, HEAD), lambda h, b, s: (0, b, s, h)),
                  pl.BlockSpec((2, HEAD), lambda h, b, s: (0, h)),
                  pl.BlockSpec((1, HEAD), lambda h, b, s: (0, h))],
        out_specs=[pl.BlockSpec((1, sb, HEAD), lambda h, b, s: (b, s, h)),
                   pl.BlockSpec((1, 1, 1, HEAD, HEAD), lambda h, b, s: (b, h, s, 0, 0))],
        out_shape=[jax.ShapeDtypeStruct((Bl, S, D), BF16),
                   jax.ShapeDtypeStruct((Bl, H, nsb, HEAD, HEAD), F32)],
        scratch_shapes=[pltpu.VMEM((HEAD, HEAD), F32), pltpu.VMEM((sb, HEAD), F32),
                        pltpu.VMEM((nc, HEAD, HEAD), F32), pltpu.VMEM((nc, HEAD, HEAD), F32),
                        pltpu.VMEM((sb, HEAD), F32), pltpu.VMEM((sb, HEAD), BF16)],
        compiler_params=_params(("parallel", "parallel", "arbitrary")),
    )(proj5, lb_logits, gn)


def _window_count(shape, g):
    pos = lax.broadcasted_iota(jnp.int32, shape, 0)
    return pos, jnp.minimum(pos + 1, jnp.left_shift(2, g)).astype(F32)


def _select_window(g, sums):
    return jnp.where(g == 0, sums[0], jnp.where(g == 1, sums[1], jnp.where(g == 2, sums[2], sums[3])))


def _pool_fwd(proj5, w_pool):
    _, Bl, S, D = proj5.shape
    pg = D // POOL_GROUPS

    def body(v_ref, w_ref, pooled_ref, bp_ref):
        g = pl.program_id(1)
        v = v_ref[0, 0]
        pos, cnt = _window_count(v.shape, g)
        cur, sums = v, []
        for sh in (1, 2, 4, 8):
            cur = cur + jnp.where(pos >= sh, pltpu.roll(cur, sh, 0), 0.0)
            sums.append(cur)
        pooled = (_select_window(g, sums) / cnt - v).astype(BF16)
        pooled_ref[0] = pooled
        bp_ref[0] = jnp.dot(pooled, w_ref[0], preferred_element_type=F32)

    return pl.pallas_call(
        body, name="pool_fwd", grid=(Bl, POOL_GROUPS),
        in_specs=[pl.BlockSpec((1, 1, S, pg), lambda b, g: (4, b, 0, g)),
                  pl.BlockSpec((1, pg, pg), lambda b, g: (g, 0, 0))],
        out_specs=[pl.BlockSpec((1, S, pg), lambda b, g: (b, 0, g)),
                   pl.BlockSpec((1, S, pg), lambda b, g: (b, 0, g))],
        out_shape=[jax.ShapeDtypeStruct((Bl, S, D), BF16), jax.ShapeDtypeStruct((Bl, S, D), F32)],
        compiler_params=_params(("parallel", "parallel")),
    )(proj5, w_pool)


def _layer_norm_fwd(r):
    mu = jnp.mean(r, axis=-1, keepdims=True)
    d = r - mu
    rs = lax.rsqrt(jnp.mean(d * d, axis=-1, keepdims=True) + LN_EPS)
    return d * rs, rs


def _layer_norm_bwd(dy_g, xhat, rs):
    return rs * (dy_g - jnp.mean(dy_g, axis=-1, keepdims=True)
                 - xhat * jnp.mean(dy_g * xhat, axis=-1, keepdims=True))


def _mix_fwd(ain, proj, bp, x2, w_a, w_out, ps, g1, b1):
    T, D = x2.shape
    tm = min(256, T)

    def body(ain_ref, ga_ref, gb_ref, bp_ref, x_ref, wa_ref, wo_ref, ps_ref, g1_ref, b1_ref,
             a_ref, mg_ref, xh_ref, rs_ref, x1b_ref):
        a = jnp.dot(ain_ref[...], wa_ref[...], preferred_element_type=F32)
        a_ref[...] = a
        merged = (_sigmoid(ga_ref[0]) * a + _sigmoid(gb_ref[0]) * (bp_ref[...] * ps_ref[...])).astype(BF16)
        mg_ref[...] = merged
        r1 = ALPHA * x_ref[...] + jnp.dot(merged, wo_ref[...], preferred_element_type=F32)
        xhat, rs = _layer_norm_fwd(r1)
        xh_ref[...] = xhat
        rs_ref[...] = rs
        x1b_ref[...] = (xhat * g1_ref[...] + b1_ref[...]).astype(BF16)

    row = lambda i: (i, 0)
    full = lambda i: (0, 0)
    return pl.pallas_call(
        body, name="mix_fwd", grid=(T // tm,),
        in_specs=[pl.BlockSpec((tm, D), row),
                  pl.BlockSpec((1, tm, D), lambda i: (5, i, 0)),
                  pl.BlockSpec((1, tm, D), lambda i: (6, i, 0)),
                  pl.BlockSpec((tm, D), row), pl.BlockSpec((tm, D), row),
                  pl.BlockSpec((D, D), full), pl.BlockSpec((D, D), full),
                  pl.BlockSpec((1, D), full), pl.BlockSpec((1, D), full), pl.BlockSpec((1, D), full)],
        out_specs=[pl.BlockSpec((tm, D), row), pl.BlockSpec((tm, D), row), pl.BlockSpec((tm, D), row),
                   pl.BlockSpec((tm, 1), row), pl.BlockSpec((tm, D), row)],
        out_shape=[jax.ShapeDtypeStruct((T, D), F32), jax.ShapeDtypeStruct((T, D), BF16),
                   jax.ShapeDtypeStruct((T, D), F32), jax.ShapeDtypeStruct((T, 1), F32),
                   jax.ShapeDtypeStruct((T, D), BF16)],
        compiler_params=_params(("parallel",)),
    )(ain, proj, proj, bp, x2, w_a, w_out, ps, g1, b1)


def _mlp_fwd(x1b, w_up, w_down, xhat1, tgt, g1, b1, g2, b2):
    T, D = xhat1.shape
    FF = w_up.shape[1]
    tm, tf = min(512, T), 512
    nf = FF // tf

    def body(x_ref, wu_ref, wd_ref, xh_ref, t_ref, g1_ref, b1_ref, g2_ref, b2_ref,
             hp_ref, dr_ref, drb_ref, vec_ref, acc):
        i, j = pl.program_id(0), pl.program_id(1)

        @pl.when((i == 0) & (j == 0))
        def _():
            vec_ref[...] = jnp.zeros_like(vec_ref)

        hp = jnp.dot(x_ref[...], wu_ref[...], preferred_element_type=F32)
        hp_ref[...] = hp
        part = jnp.dot(jnp.square(jnp.maximum(hp, 0.0)).astype(BF16), wd_ref[...],
                       preferred_element_type=F32)

        @pl.when(j == 0)
        def _():
            acc[...] = part

        @pl.when(j > 0)
        def _():
            acc[...] += part

        @pl.when(j == nf - 1)
        def _():
            x1 = xh_ref[...] * g1_ref[...] + b1_ref[...]
            xhat2, rs2 = _layer_norm_fwd(ALPHA * x1 + acc[...])
            err = xhat2 * g2_ref[...] + b2_ref[...] - t_ref[...]
            dy = err / D
            vec_ref[5:6, :] += jnp.sum(dy * xhat2, axis=0, keepdims=True)
            vec_ref[6:7, :] += jnp.sum(dy, axis=0, keepdims=True)
            vec_ref[7:8, :] += jnp.sum(0.5 * err * err / D, axis=0, keepdims=True)
            dr = _layer_norm_bwd(dy * g2_ref[...], xhat2, rs2)
            dr_ref[...] = dr
            drb_ref[...] = dr.astype(BF16)

    row = lambda i, j: (i, 0)
    full = lambda i, j: (0, 0)
    return pl.pallas_call(
        body, name="mlp_fwd", grid=(T // tm, nf),
        in_specs=[pl.BlockSpec((tm, D), row),
                  pl.BlockSpec((D, tf), lambda i, j: (0, j)),
                  pl.BlockSpec((tf, D), lambda i, j: (j, 0)),
                  pl.BlockSpec((tm, D), row), pl.BlockSpec((tm, D), row),
                  pl.BlockSpec((1, D), full), pl.BlockSpec((1, D), full),
                  pl.BlockSpec((1, D), full), pl.BlockSpec((1, D), full)],
        out_specs=[pl.BlockSpec((tm, tf), lambda i, j: (i, j)),
                   pl.BlockSpec((tm, D), row), pl.BlockSpec((tm, D), row),
                   pl.BlockSpec((8, D), full)],
        out_shape=[jax.ShapeDtypeStruct((T, FF), F32), jax.ShapeDtypeStruct((T, D), F32),
                   jax.ShapeDtypeStruct((T, D), BF16), jax.ShapeDtypeStruct((8, D), F32)],
        scratch_shapes=[pltpu.VMEM((tm, D), F32)],
        compiler_params=_params(("arbitrary", "arbitrary")),
    )(x1b, w_up, w_down, xhat1, tgt, g1, b1, g2, b2)


def _mlp_bwd(drb, dr, hp, w_up, w_down, xhat1, rs1, g1):
    T, D = dr.shape
    FF = hp.shape[1]
    tm, tf = min(512, T), 512
    nf = FF // tf

    def body(drb_ref, dr_ref, hp_ref, wu_ref, wd_ref, xh_ref, rs_ref, g1_ref,
             dhp_ref, d1_ref, d1b_ref, vec_ref, acc):
        i, j = pl.program_id(0), pl.program_id(1)

        @pl.when((i == 0) & (j == 0))
        def _():
            vec_ref[...] = jnp.zeros_like(vec_ref)

        dh = lax.dot_general(drb_ref[...], wd_ref[...], NT_DIMS, preferred_element_type=F32)
        dhp = (dh * (2.0 * jnp.maximum(hp_ref[...], 0.0))).astype(BF16)
        dhp_ref[...] = dhp
        part = lax.dot_general(dhp, wu_ref[...], NT_DIMS, preferred_element_type=F32)

        @pl.when(j == 0)
        def _():
            acc[...] = part

        @pl.when(j > 0)
        def _():
            acc[...] += part

        @pl.when(j == nf - 1)
        def _():
            dx1 = ALPHA * dr_ref[...] + acc[...]
            xhat = xh_ref[...]
            vec_ref[3:4, :] += jnp.sum(dx1 * xhat, axis=0, keepdims=True)
            vec_ref[4:5, :] += jnp.sum(dx1, axis=0, keepdims=True)
            d1 = _layer_norm_bwd(dx1 * g1_ref[...], xhat, rs_ref[...])
            d1_ref[...] = d1
            d1b_ref[...] = d1.astype(BF16)

    row = lambda i, j: (i, 0)
    full = lambda i, j: (0, 0)
    return pl.pallas_call(
        body, name="mlp_bwd", grid=(T // tm, nf),
        in_specs=[pl.BlockSpec((tm, D), row), pl.BlockSpec((tm, D), row),
                  pl.BlockSpec((tm, tf), lambda i, j: (i, j)),
                  pl.BlockSpec((D, tf), lambda i, j: (0, j)),
                  pl.BlockSpec((tf, D), lambda i, j: (j, 0)),
                  pl.BlockSpec((tm, D), row), pl.BlockSpec((tm, 1), row), pl.BlockSpec((1, D), full)],
        out_specs=[pl.BlockSpec((tm, tf), lambda i, j: (i, j)),
                   pl.BlockSpec((tm, D), row), pl.BlockSpec((tm, D), row),
                   pl.BlockSpec((8, D), full)],
        out_shape=[jax.ShapeDtypeStruct((T, FF), BF16), jax.ShapeDtypeStruct((T, D), F32),
                   jax.ShapeDtypeStruct((T, D), BF16), jax.ShapeDtypeStruct((8, D), F32)],
        scratch_shapes=[pltpu.VMEM((tm, D), F32)],
        compiler_params=_params(("arbitrary", "arbitrary")),
    )(drb, dr, hp, w_up, w_down, xhat1, rs1, g1)


def _mm_tn(name, a, b, n_j, a_block, a_map, b_block, b_map, o_shape, o_block, o_map, a_fn=None):
    T = a.shape[0]
    nt = T // a_block[0]
    lead = len(o_block) == 3

    def body(a_ref, b_ref, o_ref):
        t = pl.program_id(1)
        av = a_ref[...]
        if a_fn is not None:
            av = a_fn(av)
        p = lax.dot_general(av.astype(BF16), b_ref[...].astype(BF16), TN_DIMS,
                            preferred_element_type=F32)
        out = o_ref.at[0] if lead else o_ref

        @pl.when(t == 0)
        def _():
            out[...] = p

        @pl.when(t > 0)
        def _():
            out[...] += p

    return pl.pallas_call(
        body, name=name, grid=(n_j, nt),
        in_specs=[pl.BlockSpec(a_block, a_map), pl.BlockSpec(b_block, b_map)],
        out_specs=pl.BlockSpec(o_block, o_map),
        out_shape=jax.ShapeDtypeStruct(o_shape, F32),
        compiler_params=_params(("parallel", "arbitrary")),
    )(a, b)


def _mix_bwd(d1b, proj, a, bp, w_a, w_out, w_pool, ps):
    T, D = a.shape
    tm = min(256, T)
    pg = D // POOL_GROUPS

    def body(d1b_ref, ga_ref, gb_ref, a_ref, bp_ref, wa_ref, wo_ref, wp_ref, ps_ref,
             da_ref, dbp_ref, dain_ref, dpl_ref, dg_ref, vec_ref):
        @pl.when(pl.program_id(0) == 0)
        def _():
            vec_ref[...] = jnp.zeros_like(vec_ref)

        dm = lax.dot_general(d1b_ref[...], wo_ref[...], NT_DIMS, preferred_element_type=F32)
        sa, sg = _sigmoid(ga_ref[0]), _sigmoid(gb_ref[0])
        bp_v, ps_v = bp_ref[...], ps_ref[...]
        da = (dm * sa).astype(BF16)
        db = dm * sg
        dg_ref[0] = (dm * a_ref[...] * sa * (1.0 - sa)).astype(BF16)
        dg_ref[1] = (dm * (bp_v * ps_v) * sg * (1.0 - sg)).astype(BF16)
        vec_ref[2:3, :] += jnp.sum(db * bp_v, axis=0, keepdims=True)
        dbp = (db * ps_v).astype(BF16)
        da_ref[...] = da
        dbp_ref[...] = dbp
        dain_ref[...] = lax.dot_general(da, wa_ref[...], NT_DIMS, preferred_element_type=F32)
        for g in range(POOL_GROUPS):
            cols = slice(g * pg, (g + 1) * pg)
            dpl_ref[:, cols] = lax.dot_general(dbp[:, cols], wp_ref[g], NT_DIMS,
                                               preferred_element_type=F32)

    row = lambda i: (i, 0)
    full = lambda i: (0, 0)
    return pl.pallas_call(
        body, name="mix_bwd", grid=(T // tm,),
        in_specs=[pl.BlockSpec((tm, D), row),
                  pl.BlockSpec((1, tm, D), lambda i: (5, i, 0)),
                  pl.BlockSpec((1, tm, D), lambda i: (6, i, 0)),
                  pl.BlockSpec((tm, D), row), pl.BlockSpec((tm, D), row),
                  pl.BlockSpec((D, D), full), pl.BlockSpec((D, D), full),
                  pl.BlockSpec((POOL_GROUPS, pg, pg), lambda i: (0, 0, 0)),
                  pl.BlockSpec((1, D), full)],
        out_specs=[pl.BlockSpec((tm, D), row), pl.BlockSpec((tm, D), row),
                   pl.BlockSpec((tm, D), row), pl.BlockSpec((tm, D), row),
                   pl.BlockSpec((2, tm, D), lambda i: (0, i, 0)),
                   pl.BlockSpec((8, D), full)],
        out_shape=[jax.ShapeDtypeStruct((T, D), BF16), jax.ShapeDtypeStruct((T, D), BF16),
                   jax.ShapeDtypeStruct((T, D), F32), jax.ShapeDtypeStruct((T, D), F32),
                   jax.ShapeDtypeStruct((2, T, D), BF16), jax.ShapeDtypeStruct((8, D), F32)],
        compiler_params=_params(("arbitrary",)),
    )(d1b, proj, proj, a, bp, w_a, w_out, w_pool, ps)


def _pool_bwd(dpooled3):
    Bl, S, D = dpooled3.shape
    pg = D // POOL_GROUPS

    def body(dp_ref, dv_ref):
        g = pl.program_id(1)
        dp = dp_ref[0]
        pos, cnt = _window_count(dp.shape, g)
        cur, sums = dp / cnt, []
        for sh in (1, 2, 4, 8):
            cur = cur + jnp.where(pos < S - sh, pltpu.roll(cur, S - sh, 0), 0.0)
            sums.append(cur)
        dv_ref[0] = (_select_window(g, sums) - dp).astype(BF16)

    spec = pl.BlockSpec((1, S, pg), lambda b, g: (b, 0, g))
    return pl.pallas_call(
        body, name="pool_bwd", grid=(Bl, POOL_GROUPS), in_specs=[spec], out_specs=spec,
        out_shape=jax.ShapeDtypeStruct((Bl, S, D), BF16),
        compiler_params=_params(("parallel", "parallel")),
    )(dpooled3)


def _hgrn_bwd(proj5, lb_logits, gn, dain3, st0_all):
    _, Bl, S, D = proj5.shape
    H = D // HEAD
    sb = min(512, S)
    nsb = S // sb
    nc = sb // CHUNK

    def body(p_ref, lbl_ref, gn_ref, dain_ref, st0_ref, d_ref, vec_ref,
             dcarry, o_scr, kv_scr, st_scr, dst_scr, dec_scr, qd_scr, ke_scr, v_scr, do_scr,
             dvi_scr, dke_scr, dqi_scr):
        b, s = pl.program_id(1), pl.program_id(2)

        @pl.when(s == 0)
        def _():
            dcarry[...] = jnp.zeros_like(dcarry)

        @pl.when((b == 0) & (s == 0))
        def _():
            vec_ref[...] = jnp.zeros_like(vec_ref)

        q, f_pre, v, og = p_ref[0, 0], p_ref[1, 0], p_ref[2, 0], p_ref[3, 0]
        c = _hgrn_gates(q, f_pre, lbl_ref[...])
        _hgrn_forward_block(c, v, st0_ref[0, 0, 0], sb, o_scr, kv_scr, st_scr, dec_scr, qd_scr)
        qd_b, ki_b, ke_b, v_b = (c["qd"].astype(BF16), c["ki"].astype(BF16),
                                 c["ke"].astype(BF16), v.astype(BF16))

        o = o_scr[...]
        rinv = lax.rsqrt(jnp.mean(o * o, axis=-1, keepdims=True) + RMS_EPS)
        on = o * rinv
        so = _sigmoid(og)
        gn_v = gn_ref[...]
        dain = dain_ref[0]
        vec_ref[1:2, :] += jnp.sum(dain * on * so, axis=0, keepdims=True)
        d_og = dain * on * gn_v * so * (1.0 - so)
        d_on = dain * gn_v * so
        do = rinv * (d_on - on * jnp.mean(d_on * on, axis=-1, keepdims=True))
        do_b = do.astype(BF16)

        mask = _intra_mask()
        dv_parts, dqd_parts, dki_parts = [], [], []
        for g in range(sb // GROUP):
            sl = slice(g * GROUP, (g + 1) * GROUP)
            sc = lax.dot_general(qd_b[sl], ki_b[sl], NT_DIMS, preferred_element_type=F32)
            a = jnp.where(mask, sc, 0.0).astype(BF16)
            da = lax.dot_general(do_b[sl], v_b[sl], NT_DIMS, preferred_element_type=F32)
            da = jnp.where(mask, da, 0.0).astype(BF16)
            dv_parts.append(lax.dot_general(a, do_b[sl], TN_DIMS, preferred_element_type=F32))
            dqd_parts.append(jnp.dot(da, ki_b[sl], preferred_element_type=F32))
            dki_parts.append(lax.dot_general(da, qd_b[sl], TN_DIMS, preferred_element_type=F32))
        dv_intra = jnp.concatenate(dv_parts, axis=0)
        dqd_intra = jnp.concatenate(dqd_parts, axis=0)
        dki = jnp.concatenate(dki_parts, axis=0)

        _chunk_outer(do, qd_b, kv_scr, sb)

        def rrec(i, dst):
            n = nc - 1 - i
            dst_scr[n] = dst
            d = dec_scr[pl.ds(pl.multiple_of(n * CHUNK, CHUNK), 1), :]
            return dst * d + kv_scr[n]

        dcarry[...] = lax.fori_loop(0, nc, rrec, dcarry[...])
        ke_scr[...] = ke_b
        v_scr[...] = v_b
        do_scr[...] = do_b

        def inter(n, carry):
            rows = pl.ds(pl.multiple_of(n * CHUNK, CHUNK), CHUNK)
            dst_b = dst_scr[n].astype(BF16)
            dvi_scr[rows, :] = lax.dot_general(ke_scr[rows, :], dst_b, NT_DIMS, preferred_element_type=F32)
            dke_scr[rows, :] = jnp.dot(v_scr[rows, :], dst_b, preferred_element_type=F32)
            dqi_scr[rows, :] = jnp.dot(do_scr[rows, :], st_scr[n].astype(BF16), preferred_element_type=F32)
            return carry

        lax.fori_loop(0, nc, inter, 0)
        ddec = jnp.sum(dst_scr[...] * st_scr[...], axis=1)
        dgl = jnp.broadcast_to(ddec[:, None, :], (nc, CHUNK, HEAD)).reshape(sb, HEAD) * c["dec"]

        dqd = dqd_intra + dqi_scr[...]
        dke = dke_scr[...]
        t_ke = dke * c["ke"]
        dG = dqd * c["qd"] - dki * c["ki"] - t_ke
        dgl = dgl + _chunk_cumsum(t_ke) + _chunk_cumsum(t_ke, reverse=True) - t_ke
        dlogf = _chunk_cumsum(dG, reverse=True) + dgl
        dk = dki * c["e_ng"] + dke * c["e_ge"]
        df = dlogf / c["f"] - dk
        sg, sq, lb = c["sg"], c["sq"], c["lb"]
        vec_ref[0:1, :] += jnp.sum(df * (1.0 - sg), axis=0, keepdims=True)
        d_ref[0, 0] = (dqd * c["e_g"] * Q_SCALE * (sq + q * sq * (1.0 - sq))).astype(BF16)
        d_ref[1, 0] = (df * (1.0 - lb) * sg * (1.0 - sg)).astype(BF16)
        d_ref[2, 0] = (dv_intra + dvi_scr[...]).astype(BF16)
        d_ref[3, 0] = d_og.astype(BF16)

    rev = lambda s: nsb - 1 - s
    big = pltpu.VMEM((nc, HEAD, HEAD), F32)
    rows_f32 = pltpu.VMEM((sb, HEAD), F32)
    rows_b16 = pltpu.VMEM((sb, HEAD), BF16)
    return pl.pallas_call(
        body, name="hgrn_bwd", grid=(H, Bl, nsb),
        in_specs=[pl.BlockSpec((4, 1, sb, HEAD), lambda h, b, s: (0, b, rev(s), h)),
                  pl.BlockSpec((2, HEAD), lambda h, b, s: (0, h)),
                  pl.BlockSpec((1, HEAD), lambda h, b, s: (0, h)),
                  pl.BlockSpec((1, sb, HEAD), lambda h, b, s: (b, rev(s), h)),
                  pl.BlockSpec((1, 1, 1, HEAD, HEAD), lambda h, b, s: (b, h, rev(s), 0, 0))],
        out_specs=[pl.BlockSpec((4, 1, sb, HEAD), lambda h, b, s: (0, b, rev(s), h)),
                   pl.BlockSpec((8, HEAD), lambda h, b, s: (0, h))],
        out_shape=[jax.ShapeDtypeStruct((4, Bl, S, D), BF16), jax.ShapeDtypeStruct((8, D), F32)],
        scratch_shapes=[pltpu.VMEM((HEAD, HEAD), F32), rows_f32, big, big, big, rows_f32,
                        rows_b16, rows_b16, rows_b16, rows_b16, rows_f32, rows_f32, rows_f32],
        compiler_params=_params(("parallel", "arbitrary", "arbitrary")),
    )(proj5, lb_logits, gn, dain3, st0_all)


def _section_sources(j, refs, fn):
    dh_ref, dp_ref, dg_ref = refs

    @pl.when(j < 4)
    def _():
        fn(dh_ref[0])

    @pl.when(j == 4)
    def _():
        fn(dp_ref[...])

    @pl.when(j > 4)
    def _():
        fn(dg_ref[0])


def _section_specs(tm, D, order):
    pick = (lambda a, b: (b, a)) if order == "ji" else (lambda a, b: (a, b))

    def at(fn):
        return lambda a, b: fn(*pick(a, b))

    return [pl.BlockSpec((1, tm, D), at(lambda i, j: (jnp.minimum(j, 3), i, 0))),
            pl.BlockSpec((tm, D), at(lambda i, j: (i, 0))),
            pl.BlockSpec((1, tm, D), at(lambda i, j: (jnp.clip(j - 5, 0, 1), i, 0)))]


def _dx(d1, dh4, dpv, dg2, w_in):
    T, D = d1.shape
    tm = min(512, T)

    def body(d1_ref, dh_ref, dp_ref, dg_ref, w_ref, o_ref):
        j = pl.program_id(1)

        @pl.when(j == 0)
        def _():
            o_ref[...] = ALPHA * d1_ref[...]

        def add(blk):
            o_ref[...] += lax.dot_general(blk, w_ref[...], NT_DIMS, preferred_element_type=F32)

        _section_sources(j, (dh_ref, dp_ref, dg_ref), add)

    return pl.pallas_call(
        body, name="dx", grid=(T // tm, N_SEC),
        in_specs=[pl.BlockSpec((tm, D), lambda i, j: (i, 0))] + _section_specs(tm, D, "ij")
        + [pl.BlockSpec((D, D), lambda i, j: (0, j))],
        out_specs=pl.BlockSpec((tm, D), lambda i, j: (i, 0)),
        out_shape=jax.ShapeDtypeStruct((T, D), F32),
        compiler_params=_params(("parallel", "arbitrary")),
    )(d1, dh4, dpv, dg2, w_in)


def _dw_in(x2, dh4, dpv, dg2):
    T, D = x2.shape
    tk = min(512, T)

    def body(x_ref, dh_ref, dp_ref, dg_ref, o_ref):
        j, t = pl.program_id(0), pl.program_id(1)
        xb = x_ref[...].astype(BF16)

        def acc(blk):
            p = lax.dot_general(xb, blk, TN_DIMS, preferred_element_type=F32)

            @pl.when(t == 0)
            def _():
                o_ref[...] = p

            @pl.when(t > 0)
            def _():
                o_ref[...] += p

        _section_sources(j, (dh_ref, dp_ref, dg_ref), acc)

    return pl.pallas_call(
        body, name="dw_in", grid=(N_SEC, T // tk),
        in_specs=[pl.BlockSpec((tk, D), lambda j, t: (t, 0))] + _section_specs(tk, D, "ji"),
        out_specs=pl.BlockSpec((D, D), lambda j, t: (0, j)),
        out_shape=jax.ShapeDtypeStruct((D, N_SEC * D), F32),
        compiler_params=_params(("parallel", "arbitrary")),
    )(x2, dh4, dpv, dg2)


class _Sharded:
    def __init__(self, kind, shard_shape):
        self.kind, self.shape = kind, tuple(shard_shape)

    def at(self, ref, d):
        if self.kind == "col":
            n = self.shape[1]
            return ref.at[:, pl.ds(pl.multiple_of(d * n, 128), n)]
        if self.kind == "row":
            n = self.shape[0]
            return ref.at[pl.ds(pl.multiple_of(d * n, 8), n), :]
        n = self.shape[1]
        return ref.at[:, pl.ds(pl.multiple_of(d * n, 8), n), :]

    def block_index(self, d):
        return {"col": (0, d), "row": (d, 0), "pool": (0, d, 0)}[self.kind]


def _rs_stage_a(grads, layouts):
    nw = len(grads)

    def body(*refs):
        ins, outs = refs[:nw], refs[nw:2 * nw]
        send_sems, recv_sems = refs[2 * nw:]
        x, y, c = _me()
        copies = []
        for w in range(nw):
            for chip in range(N_CHIP):
                cp = pltpu.make_async_remote_copy(
                    src_ref=layouts[w].at(ins[w], 2 * chip + (1 - c)), dst_ref=outs[w].at[chip],
                    send_sem=send_sems.at[w, chip], recv_sem=recv_sems.at[w, chip],
                    device_id=(x, y, 1 - c), device_id_type=MESH)
                cp.start()
                copies.append(cp)
        for cp in copies:
            cp.wait()

    any_spec = pl.BlockSpec(memory_space=pl.ANY)
    return pl.pallas_call(
        body, name="rs_d2d",
        out_shape=[jax.ShapeDtypeStruct((N_CHIP,) + l.shape, F32) for l in layouts],
        in_specs=[any_spec] * nw, out_specs=[any_spec] * nw,
        scratch_shapes=[pltpu.SemaphoreType.DMA((nw, N_CHIP)), pltpu.SemaphoreType.DMA((nw, N_CHIP))],
    )(*grads)


def _rs_sum_a(name, c_arr, grad, recv, layout):
    blk = layout.shape

    def body(c_ref, g_ref, r_ref, pf_ref, pb_ref):
        s = g_ref[...] + r_ref[0]
        pf_ref[0] = s
        pb_ref[0] = s.astype(BF16)

    lead = (lambda i, c_ref: (i,) + (0,) * len(blk))
    grid_spec = pltpu.PrefetchScalarGridSpec(
        num_scalar_prefetch=1, grid=(N_CHIP,),
        in_specs=[pl.BlockSpec(blk, lambda i, c_ref: layout.block_index(2 * i + c_ref[0])),
                  pl.BlockSpec((1,) + blk, lead)],
        out_specs=[pl.BlockSpec((1,) + blk, lead), pl.BlockSpec((1,) + blk, lead)])
    return pl.pallas_call(
        body, name=name, grid_spec=grid_spec,
        out_shape=[jax.ShapeDtypeStruct((N_CHIP,) + blk, F32),
                   jax.ShapeDtypeStruct((N_CHIP,) + blk, BF16)],
        compiler_params=_params(("parallel",)),
    )(c_arr, grad, recv)


def _rs_stage_b(partials):
    nw = len(partials)

    def body(*refs):
        ins, outs = refs[:nw], refs[nw:2 * nw]
        send_sems, recv_sems = refs[2 * nw:]
        x, y, c = _me()
        chips = [(1 - x, y), (x, 1 - y), (1 - x, 1 - y)]
        copies = []
        for w in range(nw):
            for k, (px, py) in enumerate(chips):
                cp = pltpu.make_async_remote_copy(
                    src_ref=ins[w].at[2 * px + py], dst_ref=outs[w].at[k],
                    send_sem=send_sems.at[w, k], recv_sem=recv_sems.at[w, k],
                    device_id=(px, py, c), device_id_type=MESH)
                cp.start()
                copies.append(cp)
        for cp in copies:
            cp.wait()

    any_spec = pl.BlockSpec(memory_space=pl.ANY)
    return pl.pallas_call(
        body, name="rs_ici",
        out_shape=[jax.ShapeDtypeStruct((3,) + p.shape[1:], BF16) for p in partials],
        in_specs=[any_spec] * nw, out_specs=[any_spec] * nw,
        scratch_shapes=[pltpu.SemaphoreType.DMA((nw, 3)), pltpu.SemaphoreType.DMA((nw, 3))],
    )(*partials)


def _adam_shard(name, chip_arr, pf, recv, w, m, v, n_split):
    shape = w.shape
    blk = (shape[0] // n_split,) + tuple(shape[1:])
    zeros = (0,) * (len(shape) - 1)

    def body(chip_ref, pf_ref, r_ref, w_ref, m_ref, v_ref, g_out, d_out, m_out, v_out):
        g = pf_ref[0] + r_ref[0].astype(F32) + r_ref[1].astype(F32) + r_ref[2].astype(F32)
        d, m2, v2 = _adamw(w_ref[...], g, m_ref[...], v_ref[...])
        g_out[...] = g
        d_out[...] = d
        m_out[...] = m2
        v_out[...] = v2

    plain = pl.BlockSpec(blk, lambda i, chip_ref: (i,) + zeros)
    grid_spec = pltpu.PrefetchScalarGridSpec(
        num_scalar_prefetch=1, grid=(n_split,),
        in_specs=[pl.BlockSpec((1,) + blk, lambda i, chip_ref: (chip_ref[0], i) + zeros),
                  pl.BlockSpec((3,) + blk, lambda i, chip_ref: (0, i) + zeros),
                  plain, plain, plain],
        out_specs=[plain] * 4)
    return pl.pallas_call(
        body, name=name, grid_spec=grid_spec,
        out_shape=[jax.ShapeDtypeStruct(shape, F32)] * 4,
        compiler_params=_params(("parallel",)),
    )(chip_arr, pf, recv, w, m, v)


def _vec_allreduce_adam(vec, small_w, small_m, small_v):
    n = len(small_w)
    D = vec.shape[1]

    def body(*refs):
        vec_ref = refs[0]
        ws, ms, vs = refs[1:1 + n], refs[1 + n:1 + 2 * n], refs[1 + 2 * n:1 + 3 * n]
        outs = refs[1 + 3 * n:2 + 7 * n]
        gat, send_sems, recv_sems = refs[2 + 7 * n:]
        loss_ref, g_out, d_out = outs[0], outs[1:1 + n], outs[1 + n:1 + 2 * n]
        m_out, v_out = outs[1 + 2 * n:1 + 3 * n], outs[1 + 3 * n:1 + 4 * n]
        x, y, c = _me()
        me = 4 * x + 2 * y + c
        gat[me] = vec_ref[...]
        copies = []
        for k in range(1, N_DEV):
            fx, fy, fc = (k >> 2) & 1, (k >> 1) & 1, k & 1
            to = (1 - x if fx else x, 1 - y if fy else y, 1 - c if fc else c)
            cp = pltpu.make_async_remote_copy(
                src_ref=vec_ref, dst_ref=gat.at[me], send_sem=send_sems.at[k - 1],
                recv_sem=recv_sems.at[k - 1], device_id=to, device_id_type=MESH)
            cp.start()
            copies.append(cp)
        for cp in copies:
            cp.wait()
        tot = gat[0]
        for d in range(1, N_DEV):
            tot = tot + gat[d]
        loss_ref[...] = jnp.broadcast_to(jnp.sum(tot[7:8, :], axis=1, keepdims=True), loss_ref.shape)
        lbl = ws[0][...]
        mx = jnp.maximum(lbl[0:1, :], lbl[1:2, :])
        e0, e1 = jnp.exp(lbl[0:1, :] - mx), jnp.exp(lbl[1:2, :] - mx)
        p0 = e0 / (e0 + e1)
        dl0 = tot[0:1, :] * p0 * (1.0 - p0)
        grads = [jnp.concatenate([dl0, -dl0], axis=0)] + [tot[r:r + 1, :] for r in range(1, n)]
        for i in range(n):
            d, m2, v2 = _adamw(ws[i][...], grads[i], ms[i][...], vs[i][...])
            g_out[i][...] = grads[i]
            d_out[i][...] = d
            m_out[i][...] = m2
            v_out[i][...] = v2

    vm = pl.BlockSpec(memory_space=pltpu.VMEM)
    shapes = [jax.ShapeDtypeStruct(w.shape, F32) for w in small_w]
    return pl.pallas_call(
        body, name="vec_allreduce_adam",
        out_shape=[jax.ShapeDtypeStruct((1, 128), F32)] + shapes * 4,
        in_specs=[vm] * (1 + 3 * n), out_specs=[vm] * (1 + 4 * n),
        scratch_shapes=[pltpu.VMEM((N_DEV, 8, D), F32), pltpu.SemaphoreType.DMA((N_DEV - 1,)),
                        pltpu.SemaphoreType.DMA((N_DEV - 1,))],
    )(vec, *small_w, *small_m, *small_v)


def kernel(x, w_in, lb_logits, hgrn_norm_g, w_a, w_pool, pool_scale, w_out, ln1_g, ln1_b, w_up, w_down, ln2_g, ln2_b, loss_target, m_w_in, m_lb_logits, m_hgrn_norm_g, m_w_a, m_w_pool, m_pool_scale, m_w_out, m_ln1_g, m_ln1_b, m_w_up, m_w_down, m_ln2_g, m_ln2_b, v_w_in, v_lb_logits, v_hgrn_norm_g, v_w_a, v_w_pool, v_pool_scale, v_w_out, v_ln1_g, v_ln1_b, v_w_up, v_w_down, v_ln2_g, v_ln2_b):
    Bl, S, D = x.shape
    T = Bl * S
    FF = 4 * D
    pg = D // POOL_GROUPS
    x2 = x.reshape(T, D)
    tgt = loss_target.reshape(T, D)

    big_w = [w_in[0], w_a[0], w_pool[0], w_out[0], w_up[0], w_down[0]]
    g_in, g_a, g_pool, g_out, g_up, g_down = _all_gather([w.astype(BF16) for w in big_w])
    w_in_f = jnp.transpose(g_in, (1, 0, 2)).reshape(D, N_SEC * D)
    w_up_f = jnp.transpose(g_up, (1, 0, 2)).reshape(D, FF)
    w_a_f = g_a.reshape(D, D)
    w_out_f = g_out.reshape(D, D)
    w_down_f = g_down.reshape(FF, D)
    w_pool_f = jnp.transpose(g_pool, (1, 0, 2, 3)).reshape(POOL_GROUPS, pg, pg)

    proj = _proj(x2, w_in_f)
    proj5 = proj.reshape(N_SEC, Bl, S, D)
    ain3, st0_all = _hgrn_fwd(proj5, lb_logits, hgrn_norm_g)
    pooled3, bp3 = _pool_fwd(proj5, w_pool_f)
    ain, pooled, bp = ain3.reshape(T, D), pooled3.reshape(T, D), bp3.reshape(T, D)
    a, merged, xhat1, rs1, x1b = _mix_fwd(ain, proj, bp, x2, w_a_f, w_out_f, pool_scale, ln1_g, ln1_b)
    hp, dr2, dr2b, vec_mlp = _mlp_fwd(x1b, w_up_f, w_down_f, xhat1, tgt, ln1_g, ln1_b, ln2_g, ln2_b)

    dhp, dr1, dr1b, vec_ln1 = _mlp_bwd(dr2b, dr2, hp, w_up_f, w_down_f, xhat1, rs1, ln1_g)
    tk = min(512, T)
    gw_down = _mm_tn("dw_down", hp, dr2b, N_DEV, (tk, 512), lambda j, t: (t, j), (tk, D), lambda j, t: (t, 0),
                     (FF, D), (512, D), lambda j, t: (j, 0),
                     a_fn=lambda h: jnp.square(jnp.maximum(h, 0.0)))
    gw_up = _mm_tn("dw_up", x1b, dhp, N_DEV, (tk, D), lambda j, t: (t, 0), (tk, 512), lambda j, t: (t, j),
                   (D, FF), (D, 512), lambda j, t: (0, j))
    da_b, dbp_b, dain, dpooled, dg2, vec_mix = _mix_bwd(dr1b, proj, a, bp, w_a_f, w_out_f, w_pool_f, pool_scale)
    gw_out = _mm_tn("dw_out", merged, dr1b, 2, (tk, D), lambda j, t: (t, 0), (tk, D // 2), lambda j, t: (t, j),
                    (D, D), (D, D // 2), lambda j, t: (0, j))
    gw_a = _mm_tn("dw_a", ain, da_b, 2, (tk, D), lambda j, t: (t, 0), (tk, D // 2), lambda j, t: (t, j),
                  (D, D), (D, D // 2), lambda j, t: (0, j))
    gw_pool = _mm_tn("dw_pool", pooled, dbp_b, POOL_GROUPS, (tk, pg), lambda j, t: (t, j), (tk, pg),
                     lambda j, t: (t, j), (POOL_GROUPS, pg, pg), (1, pg, pg), lambda j, t: (j, 0, 0))
    dpv = _pool_bwd(dpooled.reshape(Bl, S, D)).reshape(T, D)
    dh4, vec_hgrn = _hgrn_bwd(proj5, lb_logits, hgrn_norm_g, dain.reshape(Bl, S, D), st0_all)
    dh4 = dh4.reshape(4, T, D)
    grad_x = _dx(dr1, dh4, dpv, dg2, w_in_f).reshape(Bl, S, D)
    gw_in = _dw_in(x2, dh4, dpv, dg2)

    names = ["w_in", "w_a", "w_pool", "w_out", "w_up", "w_down"]
    grads = [gw_in, gw_a, gw_pool, gw_out, gw_up, gw_down]
    layouts = [_Sharded("col", w_in.shape[1:]), _Sharded("row", w_a.shape[1:]),
               _Sharded("pool", w_pool.shape[1:]), _Sharded("row", w_out.shape[1:]),
               _Sharded("col", w_up.shape[1:]), _Sharded("row", w_down.shape[1:])]
    ms = [m_w_in, m_w_a, m_w_pool, m_w_out, m_w_up, m_w_down]
    vs = [v_w_in, v_w_a, v_w_pool, v_w_out, v_w_up, v_w_down]
    c_arr = jnp.reshape(lax.axis_index("c"), (1,)).astype(jnp.int32)
    chip_arr = jnp.reshape(2 * lax.axis_index("x") + lax.axis_index("y"), (1,)).astype(jnp.int32)
    recv_a = _rs_stage_a(grads, layouts)
    sums = [_rs_sum_a("rs_sum_" + nm, c_arr, g, r, l) for nm, g, r, l in zip(names, grads, recv_a, layouts)]
    recv_b = _rs_stage_b([s[1] for s in sums])
    big = {}
    for nm, s, r, w, m, v in zip(names, sums, recv_b, big_w, ms, vs):
        g, d, m2, v2 = _adam_shard("adam_" + nm, chip_arr, s[0], r, w, m[0], v[0], 4)
        big[nm] = tuple(t[None] for t in (g, d, m2, v2))

    vec = vec_mlp + vec_ln1 + vec_mix + vec_hgrn
    small_names = ["lb_logits", "hgrn_norm_g", "pool_scale", "ln1_g", "ln1_b", "ln2_g", "ln2_b"]
    small_w = [lb_logits, hgrn_norm_g, pool_scale, ln1_g, ln1_b, ln2_g, ln2_b]
    small_m = [m_lb_logits, m_hgrn_norm_g, m_pool_scale, m_ln1_g, m_ln1_b, m_ln2_g, m_ln2_b]
    small_v = [v_lb_logits, v_hgrn_norm_g, v_pool_scale, v_ln1_g, v_ln1_b, v_ln2_g, v_ln2_b]
    res = _vec_allreduce_adam(vec, small_w, small_m, small_v)
    loss = res[0][0, 0]
    n = len(small_w)
    small = {nm: (res[1 + i], res[1 + n + i], res[1 + 2 * n + i], res[1 + 3 * n + i])
             for i, nm in enumerate(small_names)}

    order = ["w_in", "lb_logits", "hgrn_norm_g", "w_a", "w_pool", "pool_scale", "w_out", "ln1_g", "ln1_b",
             "w_up", "w_down", "ln2_g", "ln2_b"]
    allp = {**big, **small}
    out = [loss, grad_x]
    for part in range(4):
        out += [allp[nm][part] for nm in order]
    return tuple(out)
```

```python
import jax
import jax.numpy as jnp
from jax import lax
from jax.experimental import pallas as pl
from jax.experimental.pallas import tpu as pltpu

F32 = jnp.float32
BF16 = jnp.bfloat16
MESH = pl.DeviceIdType.MESH

N_DEV = 8
N_CHIP = 4
HEAD = 128
CHUNK = 16
GROUP = 128
CH_PER_GROUP = GROUP // CHUNK
N_SEC = 7
POOL_GROUPS = 4
ALPHA = (2.0 * 1) ** 0.25
LN_EPS = 1e-5
RMS_EPS = 1e-6
Q_SCALE = HEAD ** -0.5
ADAM_LR = 0.001
ADAM_B1 = 0.9
ADAM_B2 = 0.999
ADAM_EPS = 1e-08
ADAM_WD = 0.01
ADAM_STEP = 10
VMEM_LIMIT = 56 << 20

NT_DIMS = (((1,), (1,)), ((), ()))
TN_DIMS = (((0,), (0,)), ((), ()))


def _params(sem=None):
    kw = dict(vmem_limit_bytes=VMEM_LIMIT)
    if sem is not None:
        kw["dimension_semantics"] = sem
    return pltpu.CompilerParams(**kw)


def _me():
    return lax.axis_index("x"), lax.axis_index("y"), lax.axis_index("c")


def _sigmoid(v):
    return jax.nn.sigmoid(v)


def _adamw(w, g, m, v):
    m = ADAM_B1 * m + (1.0 - ADAM_B1) * g
    v = ADAM_B2 * v + (1.0 - ADAM_B2) * jnp.square(g)
    m_hat = m / (1.0 - ADAM_B1 ** ADAM_STEP)
    v_hat = v / (1.0 - ADAM_B2 ** ADAM_STEP)
    delta = -ADAM_LR * (m_hat / (jnp.sqrt(v_hat) + ADAM_EPS) + ADAM_WD * w)
    return delta, m, v


def _all_gather(shards):
    nw = len(shards)

    def body(*refs):
        ins, outs = refs[:nw], refs[nw:2 * nw]
        send_sems, recv_sems, local_sems = refs[2 * nw:]
        x, y, c = _me()
        me = (x, y, c)
        sibling = (x, y, 1 - c)
        chips = [(1 - x, y), (x, 1 - y), (1 - x, 1 - y)]

        def slot(px, py, pc):
            return 4 * px + 2 * py + pc

        def copy(w, k, block, to, src=None):
            dst = outs[w].at[slot(*block)]
            return pltpu.make_async_remote_copy(
                src_ref=dst if src is None else src, dst_ref=dst,
                send_sem=send_sems.at[w, k], recv_sem=recv_sems.at[w, k],
                device_id=to, device_id_type=MESH)

        local = []
        for w in range(nw):
            mine = pltpu.make_async_copy(ins[w], outs[w].at[slot(*me)], local_sems.at[w])
            mine.start()
            local.append(mine)
        first = []
        for w in range(nw):
            first.append(copy(w, 0, me, sibling, src=ins[w]))
            first += [copy(w, 1 + j, me, (*chip, c), src=ins[w]) for j, chip in enumerate(chips)]
        for cp in first:
            cp.start()
        passed = []
        for w in range(nw):
            for j, chip in enumerate(chips):
                copy(w, 1 + j, (*chip, c), me).wait_recv()
                fwd = copy(w, 4 + j, (*chip, c), sibling)
                fwd.start()
                passed.append(fwd)
        for w in range(nw):
            copy(w, 0, sibling, me).wait_recv()
            for j, chip in enumerate(chips):
                copy(w, 4 + j, (*chip, 1 - c), me).wait_recv()
        for cp in first + passed:
            cp.wait_send()
        for cp in local:
            cp.wait()

    any_spec = pl.BlockSpec(memory_space=pl.ANY)
    return pl.pallas_call(
        body, name="ag_weights",
        out_shape=[jax.ShapeDtypeStruct((N_DEV,) + s.shape, s.dtype) for s in shards],
        in_specs=[any_spec] * nw, out_specs=[any_spec] * nw,
        scratch_shapes=[pltpu.SemaphoreType.DMA((nw, 7)), pltpu.SemaphoreType.DMA((nw, 7)),
                        pltpu.SemaphoreType.DMA((nw,))],
    )(*shards)


def _proj(x2, w_in):
    T, D = x2.shape
    tm = min(512, T)

    def body(x_ref, w_ref, o_ref):
        o_ref[0] = jnp.dot(x_ref[...].astype(BF16), w_ref[...], preferred_element_type=F32)

    return pl.pallas_call(
        body, name="proj", grid=(N_SEC, T // tm),
        in_specs=[pl.BlockSpec((tm, D), lambda j, i: (i, 0)),
                  pl.BlockSpec((D, D), lambda j, i: (0, j))],
        out_specs=pl.BlockSpec((1, tm, D), lambda j, i: (j, i, 0)),
        out_shape=jax.ShapeDtypeStruct((N_SEC, T, D), F32),
        compiler_params=_params(("parallel", "parallel")),
    )(x2, w_in)


def _chunk_cumsum(v, reverse=False):
    rows = v.shape[0]
    pos = lax.broadcasted_iota(jnp.int32, v.shape, 0) % CHUNK
    for sh in (1, 2, 4, 8):
        if reverse:
            v = v + jnp.where(pos < CHUNK - sh, pltpu.roll(v, rows - sh, 0), 0.0)
        else:
            v = v + jnp.where(pos >= sh, pltpu.roll(v, sh, 0), 0.0)
    return v


def _hgrn_gates(q, f_pre, lb_logits):
    l0, l1 = lb_logits[0:1, :], lb_logits[1:2, :]
    mx = jnp.maximum(l0, l1)
    e0, e1 = jnp.exp(l0 - mx), jnp.exp(l1 - mx)
    lb = e0 / (e0 + e1)
    sq = _sigmoid(q)
    qf = q * sq * Q_SCALE
    sg = _sigmoid(f_pre)
    f = lb + (1.0 - lb) * sg
    k = 1.0 - f
    log_f = jnp.log(f)
    G = _chunk_cumsum(log_f)
    g_to_end = _chunk_cumsum(log_f, reverse=True) - log_f
    e_g = jnp.exp(G)
    e_ng = jnp.exp(-G)
    e_ge = jnp.exp(g_to_end)
    return dict(lb=lb, sq=sq, qf=qf, sg=sg, f=f, k=k, G=G, e_g=e_g, e_ng=e_ng, e_ge=e_ge,
                qd=qf * e_g, ki=k * e_ng, ke=k * e_ge, dec=jnp.exp(G + g_to_end))


def _intra_mask():
    r = lax.broadcasted_iota(jnp.int32, (GROUP, GROUP), 0)
    c = lax.broadcasted_iota(jnp.int32, (GROUP, GROUP), 1)
    return (r // CHUNK == c // CHUNK) & (c <= r)


def _chunk_outer(lhs_rows, rhs_b, out_scr, sb):
    lane = lax.broadcasted_iota(jnp.int32, (GROUP, GROUP), 1) // CHUNK
    for g in range(sb // GROUP):
        sl = slice(g * GROUP, (g + 1) * GROUP)
        lhs_t = lhs_rows[sl].T
        for cc in range(CH_PER_GROUP):
            masked = jnp.where(lane == cc, lhs_t, 0.0).astype(BF16)
            out_scr[g * CH_PER_GROUP + cc] = jnp.dot(masked, rhs_b[sl], preferred_element_type=F32)


def _hgrn_forward_block(c, v, st0, sb, o_scr, kv_scr, st_scr, dec_scr):
    nc = sb // CHUNK
    qd_b, ki_b, ke_b, v_b = (c["qd"].astype(BF16), c["ki"].astype(BF16), c["ke"].astype(BF16),
                             v.astype(BF16))
    mask = _intra_mask()
    for g in range(sb // GROUP):
        sl = slice(g * GROUP, (g + 1) * GROUP)
        sc = lax.dot_general(qd_b[sl], ki_b[sl], NT_DIMS, preferred_element_type=F32)
        a = jnp.where(mask, sc, 0.0).astype(BF16)
        o_scr[sl, :] = jnp.dot(a, v_b[sl], preferred_element_type=F32)
    _chunk_outer(v, ke_b, kv_scr, sb)
    dec_scr[...] = c["dec"]

    def rec(n, st):
        st_scr[n] = st
        d = dec_scr[pl.ds(pl.multiple_of(n * CHUNK, CHUNK), 1), :]
        return st * d + kv_scr[n]

    st_end = lax.fori_loop(0, nc, rec, st0)

    for n in range(nc):
        rows = slice(n * CHUNK, (n + 1) * CHUNK)
        o_scr[rows, :] += lax.dot_general(qd_b[rows], st_scr[n].astype(BF16), NT_DIMS,
                                          preferred_element_type=F32)
    return st_end


def _hgrn_fwd(proj5, lb_logits, gn):
    _, Bl, S, D = proj5.shape
    H = D // HEAD
    sb = min(512, S)
    nsb = S // sb
    nc = sb // CHUNK

    def body(p_ref, lbl_ref, gn_ref, ain_ref, st0_ref, carry, o_scr, kv_scr, st_scr, dec_scr):
        s = pl.program_id(2)

        @pl.when(s == 0)
        def _():
            carry[...] = jnp.zeros_like(carry)

        st0 = carry[...]
        st0_ref[0, 0, 0] = st0
        c = _hgrn_gates(p_ref[0, 0], p_ref[1, 0], lbl_ref[...])
        carry[...] = _hgrn_forward_block(c, p_ref[2, 0], st0, sb, o_scr, kv_scr, st_scr, dec_scr)
        o = o_scr[...]
        rinv = lax.rsqrt(jnp.mean(o * o, axis=-1, keepdims=True) + RMS_EPS)
        ain_ref[0] = (o * rinv * gn_ref[...] * _sigmoid(p_ref[3, 0])).astype(BF16)

    return pl.pallas_call(
        body, name="hgrn_fwd", grid=(H, Bl, nsb),
        in_specs=[pl.BlockSpec((4, 1, sb, HEAD), lambda h, b, s: (0, b, s, h)),
                  pl.BlockSpec((2, HEAD), lambda h, b, s: (0, h)),
                  pl.BlockSpec((1, HEAD), lambda h, b, s: (0, h))],
        out_specs=[pl.BlockSpec((1, sb, HEAD), lambda h, b, s: (b, s, h)),
                   pl.BlockSpec((1, 1, 1, HEAD, HEAD), lambda h, b, s: (b, h, s, 0, 0))],
        out_shape=[jax.ShapeDtypeStruct((Bl, S, D), BF16),
                   jax.ShapeDtypeStruct((Bl, H, nsb, HEAD, HEAD), F32)],
        scratch_shapes=[pltpu.VMEM((HEAD, HEAD), F32), pltpu.VMEM((sb, HEAD), F32),
                        pltpu.VMEM((nc, HEAD, HEAD), F32), pltpu.VMEM((nc, HEAD, HEAD), F32),
                        pltpu.VMEM((sb, HEAD), F32)],
        compiler_params=_params(("parallel", "parallel", "arbitrary")),
    )(proj5, lb_logits, gn)


def _window_count(shape, g):
    pos = lax.broadcasted_iota(jnp.int32, shape, 0)
    return pos, jnp.minimum(pos + 1, jnp.left_shift(2, g)).astype(F32)


def _select_window(g, sums):
    return jnp.where(g == 0, sums[0], jnp.where(g == 1, sums[1], jnp.where(g == 2, sums[2], sums[3])))


def _pool_fwd(proj5, w_pool):
    _, Bl, S, D = proj5.shape
    pg = D // POOL_GROUPS

    def body(v_ref, w_ref, pooled_ref, bp_ref):
        g = pl.program_id(1)
        v = v_ref[0, 0]
        pos, cnt = _window_count(v.shape, g)
        cur, sums = v, []
        for sh in (1, 2, 4, 8):
            cur = cur + jnp.where(pos >= sh, pltpu.roll(cur, sh, 0), 0.0)
            sums.append(cur)
        pooled = (_select_window(g, sums) / cnt - v).astype(BF16)
        pooled_ref[0] = pooled
        bp_ref[0] = jnp.dot(pooled, w_ref[0], preferred_element_type=F32)

    return pl.pallas_call(
        body, name="pool_fwd", grid=(Bl, POOL_GROUPS),
        in_specs=[pl.BlockSpec((1, 1, S, pg), lambda b, g: (4, b, 0, g)),
                  pl.BlockSpec((1, pg, pg), lambda b, g: (g, 0, 0))],
        out_specs=[pl.BlockSpec((1, S, pg), lambda b, g: (b, 0, g)),
                   pl.BlockSpec((1, S, pg), lambda b, g: (b, 0, g))],
        out_shape=[jax.ShapeDtypeStruct((Bl, S, D), BF16), jax.ShapeDtypeStruct((Bl, S, D), F32)],
        compiler_params=_params(("parallel", "parallel")),
    )(proj5, w_pool)


def _layer_norm_fwd(r):
    mu = jnp.mean(r, axis=-1, keepdims=True)
    d = r - mu
    rs = lax.rsqrt(jnp.mean(d * d, axis=-1, keepdims=True) + LN_EPS)
    return d * rs, rs


def _layer_norm_bwd(dy_g, xhat, rs):
    return rs * (dy_g - jnp.mean(dy_g, axis=-1, keepdims=True)
                 - xhat * jnp.mean(dy_g * xhat, axis=-1, keepdims=True))


def _mix_fwd(ain, proj, bp, x2, w_a, w_out, ps, g1, b1):
    T, D = x2.shape
    tm = min(256, T)

    def body(ain_ref, ga_ref, gb_ref, bp_ref, x_ref, wa_ref, wo_ref, ps_ref, g1_ref, b1_ref,
             a_ref, mg_ref, xh_ref, rs_ref, x1b_ref):
        a = jnp.dot(ain_ref[...], wa_ref[...], preferred_element_type=F32)
        a_ref[...] = a
        merged = (_sigmoid(ga_ref[0]) * a + _sigmoid(gb_ref[0]) * (bp_ref[...] * ps_ref[...])).astype(BF16)
        mg_ref[...] = merged
        r1 = ALPHA * x_ref[...] + jnp.dot(merged, wo_ref[...], preferred_element_type=F32)
        xhat, rs = _layer_norm_fwd(r1)
        xh_ref[...] = xhat
        rs_ref[...] = rs
        x1b_ref[...] = (xhat * g1_ref[...] + b1_ref[...]).astype(BF16)

    row = lambda i: (i, 0)
    full = lambda i: (0, 0)
    return pl.pallas_call(
        body, name="mix_fwd", grid=(T // tm,),
        in_specs=[pl.BlockSpec((tm, D), row),
                  pl.BlockSpec((1, tm, D), lambda i: (5, i, 0)),
                  pl.BlockSpec((1, tm, D), lambda i: (6, i, 0)),
                  pl.BlockSpec((tm, D), row), pl.BlockSpec((tm, D), row),
                  pl.BlockSpec((D, D), full), pl.BlockSpec((D, D), full),
                  pl.BlockSpec((1, D), full), pl.BlockSpec((1, D), full), pl.BlockSpec((1, D), full)],
        out_specs=[pl.BlockSpec((tm, D), row), pl.BlockSpec((tm, D), row), pl.BlockSpec((tm, D), row),
                   pl.BlockSpec((tm, 1), row), pl.BlockSpec((tm, D), row)],
        out_shape=[jax.ShapeDtypeStruct((T, D), F32), jax.ShapeDtypeStruct((T, D), BF16),
                   jax.ShapeDtypeStruct((T, D), F32), jax.ShapeDtypeStruct((T, 1), F32),
                   jax.ShapeDtypeStruct((T, D), BF16)],
        compiler_params=_params(("parallel",)),
    )(ain, proj, proj, bp, x2, w_a, w_out, ps, g1, b1)


def _mlp_fwd(x1b, w_up, w_down, xhat1, tgt, g1, b1, g2, b2):
    T, D = xhat1.shape
    FF = w_up.shape[1]
    tm, tf = min(512, T), 512
    nf = FF // tf

    def body(x_ref, wu_ref, wd_ref, xh_ref, t_ref, g1_ref, b1_ref, g2_ref, b2_ref,
             hp_ref, dr_ref, drb_ref, vec_ref, acc):
        i, j = pl.program_id(0), pl.program_id(1)

        @pl.when((i == 0) & (j == 0))
        def _():
            vec_ref[...] = jnp.zeros_like(vec_ref)

        hp = jnp.dot(x_ref[...], wu_ref[...], preferred_element_type=F32)
        hp_ref[...] = hp
        part = jnp.dot(jnp.square(jnp.maximum(hp, 0.0)).astype(BF16), wd_ref[...],
                       preferred_element_type=F32)

        @pl.when(j == 0)
        def _():
            acc[...] = part

        @pl.when(j > 0)
        def _():
            acc[...] += part

        @pl.when(j == nf - 1)
        def _():
            x1 = xh_ref[...] * g1_ref[...] + b1_ref[...]
            xhat2, rs2 = _layer_norm_fwd(ALPHA * x1 + acc[...])
            err = xhat2 * g2_ref[...] + b2_ref[...] - t_ref[...]
            dy = err / D
            vec_ref[5:6, :] += jnp.sum(dy * xhat2, axis=0, keepdims=True)
            vec_ref[6:7, :] += jnp.sum(dy, axis=0, keepdims=True)
            vec_ref[7:8, :] += jnp.sum(0.5 * err * err / D, axis=0, keepdims=True)
            dr = _layer_norm_bwd(dy * g2_ref[...], xhat2, rs2)
            dr_ref[...] = dr
            drb_ref[...] = dr.astype(BF16)

    row = lambda i, j: (i, 0)
    full = lambda i, j: (0, 0)
    return pl.pallas_call(
        body, name="mlp_fwd", grid=(T // tm, nf),
        in_specs=[pl.BlockSpec((tm, D), row),
                  pl.BlockSpec((D, tf), lambda i, j: (0, j)),
                  pl.BlockSpec((tf, D), lambda i, j: (j, 0)),
                  pl.BlockSpec((tm, D), row), pl.BlockSpec((tm, D), row),
                  pl.BlockSpec((1, D), full), pl.BlockSpec((1, D), full),
                  pl.BlockSpec((1, D), full), pl.BlockSpec((1, D), full)],
        out_specs=[pl.BlockSpec((tm, tf), lambda i, j: (i, j)),
                   pl.BlockSpec((tm, D), row), pl.BlockSpec((tm, D), row),
                   pl.BlockSpec((8, D), full)],
        out_shape=[jax.ShapeDtypeStruct((T, FF), F32), jax.ShapeDtypeStruct((T, D), F32),
                   jax.ShapeDtypeStruct((T, D), BF16), jax.ShapeDtypeStruct((8, D), F32)],
        scratch_shapes=[pltpu.VMEM((tm, D), F32)],
        compiler_params=_params(("arbitrary", "arbitrary")),
    )(x1b, w_up, w_down, xhat1, tgt, g1, b1, g2, b2)


def _mlp_bwd(drb, dr, hp, w_up, w_down, xhat1, rs1, g1):
    T, D = dr.shape
    FF = hp.shape[1]
    tm, tf = min(512, T), 512
    nf = FF // tf

    def body(drb_ref, dr_ref, hp_ref, wu_ref, wd_ref, xh_ref, rs_ref, g1_ref,
             dhp_ref, d1_ref, d1b_ref, vec_ref, acc):
        i, j = pl.program_id(0), pl.program_id(1)

        @pl.when((i == 0) & (j == 0))
        def _():
            vec_ref[...] = jnp.zeros_like(vec_ref)

        dh = lax.dot_general(drb_ref[...], wd_ref[...], NT_DIMS, preferred_element_type=F32)
        dhp = (dh * (2.0 * jnp.maximum(hp_ref[...], 0.0))).astype(BF16)
        dhp_ref[...] = dhp
        part = lax.dot_general(dhp, wu_ref[...], NT_DIMS, preferred_element_type=F32)

        @pl.when(j == 0)
        def _():
            acc[...] = part

        @pl.when(j > 0)
        def _():
            acc[...] += part

        @pl.when(j == nf - 1)
        def _():
            dx1 = ALPHA * dr_ref[...] + acc[...]
            xhat = xh_ref[...]
            vec_ref[3:4, :] += jnp.sum(dx1 * xhat, axis=0, keepdims=True)
            vec_ref[4:5, :] += jnp.sum(dx1, axis=0, keepdims=True)
            d1 = _layer_norm_bwd(dx1 * g1_ref[...], xhat, rs_ref[...])
            d1_ref[...] = d1
            d1b_ref[...] = d1.astype(BF16)

    row = lambda i, j: (i, 0)
    full = lambda i, j: (0, 0)
    return pl.pallas_call(
        body, name="mlp_bwd", grid=(T // tm, nf),
        in_specs=[pl.BlockSpec((tm, D), row), pl.BlockSpec((tm, D), row),
                  pl.BlockSpec((tm, tf), lambda i, j: (i, j)),
                  pl.BlockSpec((D, tf), lambda i, j: (0, j)),
                  pl.BlockSpec((tf, D), lambda i, j: (j, 0)),
                  pl.BlockSpec((tm, D), row), pl.BlockSpec((tm, 1), row), pl.BlockSpec((1, D), full)],
        out_specs=[pl.BlockSpec((tm, tf), lambda i, j: (i, j)),
                   pl.BlockSpec((tm, D), row), pl.BlockSpec((tm, D), row),
                   pl.BlockSpec((8, D), full)],
        out_shape=[jax.ShapeDtypeStruct((T, FF), BF16), jax.ShapeDtypeStruct((T, D), F32),
                   jax.ShapeDtypeStruct((T, D), BF16), jax.ShapeDtypeStruct((8, D), F32)],
        scratch_shapes=[pltpu.VMEM((tm, D), F32)],
        compiler_params=_params(("arbitrary", "arbitrary")),
    )(drb, dr, hp, w_up, w_down, xhat1, rs1, g1)


def _mm_tn(name, a, b, n_j, a_block, a_map, b_block, b_map, o_shape, o_block, o_map, a_fn=None):
    T = a.shape[0]
    nt = T // a_block[0]
    lead = len(o_block) == 3

    def body(a_ref, b_ref, o_ref):
        t = pl.program_id(1)
        av = a_ref[...]
        if a_fn is not None:
            av = a_fn(av)
        p = lax.dot_general(av.astype(BF16), b_ref[...].astype(BF16), TN_DIMS,
                            preferred_element_type=F32)
        out = o_ref.at[0] if lead else o_ref

        @pl.when(t == 0)
        def _():
            out[...] = p

        @pl.when(t > 0)
        def _():
            out[...] += p

    return pl.pallas_call(
        body, name=name, grid=(n_j, nt),
        in_specs=[pl.BlockSpec(a_block, a_map), pl.BlockSpec(b_block, b_map)],
        out_specs=pl.BlockSpec(o_block, o_map),
        out_shape=jax.ShapeDtypeStruct(o_shape, F32),
        compiler_params=_params(("parallel", "arbitrary")),
    )(a, b)


def _mix_bwd(d1b, proj, a, bp, w_a, w_out, w_pool, ps):
    T, D = a.shape
    tm = min(256, T)
    pg = D // POOL_GROUPS

    def body(d1b_ref, ga_ref, gb_ref, a_ref, bp_ref, wa_ref, wo_ref, wp_ref, ps_ref,
             da_ref, dbp_ref, dain_ref, dpl_ref, dg_ref, vec_ref):
        @pl.when(pl.program_id(0) == 0)
        def _():
            vec_ref[...] = jnp.zeros_like(vec_ref)

        dm = lax.dot_general(d1b_ref[...], wo_ref[...], NT_DIMS, preferred_element_type=F32)
        sa, sg = _sigmoid(ga_ref[0]), _sigmoid(gb_ref[0])
        bp_v, ps_v = bp_ref[...], ps_ref[...]
        da = (dm * sa).astype(BF16)
        db = dm * sg
        dg_ref[0] = (dm * a_ref[...] * sa * (1.0 - sa)).astype(BF16)
        dg_ref[1] = (dm * (bp_v * ps_v) * sg * (1.0 - sg)).astype(BF16)
        vec_ref[2:3, :] += jnp.sum(db * bp_v, axis=0, keepdims=True)
        dbp = (db * ps_v).astype(BF16)
        da_ref[...] = da
        dbp_ref[...] = dbp
        dain_ref[...] = lax.dot_general(da, wa_ref[...], NT_DIMS, preferred_element_type=F32)
        for g in range(POOL_GROUPS):
            cols = slice(g * pg, (g + 1) * pg)
            dpl_ref[:, cols] = lax.dot_general(dbp[:, cols], wp_ref[g], NT_DIMS,
                                               preferred_element_type=F32)

    row = lambda i: (i, 0)
    full = lambda i: (0, 0)
    return pl.pallas_call(
        body, name="mix_bwd", grid=(T // tm,),
        in_specs=[pl.BlockSpec((tm, D), row),
                  pl.BlockSpec((1, tm, D), lambda i: (5, i, 0)),
                  pl.BlockSpec((1, tm, D), lambda i: (6, i, 0)),
                  pl.BlockSpec((tm, D), row), pl.BlockSpec((tm, D), row),
                  pl.BlockSpec((D, D), full), pl.BlockSpec((D, D), full),
                  pl.BlockSpec((POOL_GROUPS, pg, pg), lambda i: (0, 0, 0)),
                  pl.BlockSpec((1, D), full)],
        out_specs=[pl.BlockSpec((tm, D), row), pl.BlockSpec((tm, D), row),
                   pl.BlockSpec((tm, D), row), pl.BlockSpec((tm, D), row),
                   pl.BlockSpec((2, tm, D), lambda i: (0, i, 0)),
                   pl.BlockSpec((8, D), full)],
        out_shape=[jax.ShapeDtypeStruct((T, D), BF16), jax.ShapeDtypeStruct((T, D), BF16),
                   jax.ShapeDtypeStruct((T, D), F32), jax.ShapeDtypeStruct((T, D), F32),
                   jax.ShapeDtypeStruct((2, T, D), BF16), jax.ShapeDtypeStruct((8, D), F32)],
        compiler_params=_params(("arbitrary",)),
    )(d1b, proj, proj, a, bp, w_a, w_out, w_pool, ps)


def _pool_bwd(dpooled3):
    Bl, S, D = dpooled3.shape
    pg = D // POOL_GROUPS

    def body(dp_ref, dv_ref):
        g = pl.program_id(1)
        dp = dp_ref[0]
        pos, cnt = _window_count(dp.shape, g)
        cur, sums = dp / cnt, []
        for sh in (1, 2, 4, 8):
            cur = cur + jnp.where(pos < S - sh, pltpu.roll(cur, S - sh, 0), 0.0)
            sums.append(cur)
        dv_ref[0] = (_select_window(g, sums) - dp).astype(BF16)

    spec = pl.BlockSpec((1, S, pg), lambda b, g: (b, 0, g))
    return pl.pallas_call(
        body, name="pool_bwd", grid=(Bl, POOL_GROUPS), in_specs=[spec], out_specs=spec,
        out_shape=jax.ShapeDtypeStruct((Bl, S, D), BF16),
        compiler_params=_params(("parallel", "parallel")),
    )(dpooled3)


def _hgrn_bwd(proj5, lb_logits, gn, dain3, st0_all):
    _, Bl, S, D = proj5.shape
    H = D // HEAD
    sb = min(512, S)
    nsb = S // sb
    nc = sb // CHUNK

    def body(p_ref, lbl_ref, gn_ref, dain_ref, st0_ref, d_ref, vec_ref,
             dcarry, o_scr, kv_scr, st_scr, dst_scr, dec_scr, dvi_scr, dke_scr, dqi_scr):
        b, s = pl.program_id(1), pl.program_id(2)

        @pl.when(s == 0)
        def _():
            dcarry[...] = jnp.zeros_like(dcarry)

        @pl.when((b == 0) & (s == 0))
        def _():
            vec_ref[...] = jnp.zeros_like(vec_ref)

        q, f_pre, v, og = p_ref[0, 0], p_ref[1, 0], p_ref[2, 0], p_ref[3, 0]
        c = _hgrn_gates(q, f_pre, lbl_ref[...])
        _hgrn_forward_block(c, v, st0_ref[0, 0, 0], sb, o_scr, kv_scr, st_scr, dec_scr)
        qd_b, ki_b, ke_b, v_b = (c["qd"].astype(BF16), c["ki"].astype(BF16),
                                 c["ke"].astype(BF16), v.astype(BF16))

        o = o_scr[...]
        rinv = lax.rsqrt(jnp.mean(o * o, axis=-1, keepdims=True) + RMS_EPS)
        on = o * rinv
        so = _sigmoid(og)
        gn_v = gn_ref[...]
        dain = dain_ref[0]
        vec_ref[1:2, :] += jnp.sum(dain * on * so, axis=0, keepdims=True)
        d_og = dain * on * gn_v * so * (1.0 - so)
        d_on = dain * gn_v * so
        do = rinv * (d_on - on * jnp.mean(d_on * on, axis=-1, keepdims=True))
        do_b = do.astype(BF16)

        mask = _intra_mask()
        dv_parts, dqd_parts, dki_parts = [], [], []
        for g in range(sb // GROUP):
            sl = slice(g * GROUP, (g + 1) * GROUP)
            sc = lax.dot_general(qd_b[sl], ki_b[sl], NT_DIMS, preferred_element_type=F32)
            a = jnp.where(mask, sc, 0.0).astype(BF16)
            da = lax.dot_general(do_b[sl], v_b[sl], NT_DIMS, preferred_element_type=F32)
            da = jnp.where(mask, da, 0.0).astype(BF16)
            dv_parts.append(lax.dot_general(a, do_b[sl], TN_DIMS, preferred_element_type=F32))
            dqd_parts.append(jnp.dot(da, ki_b[sl], preferred_element_type=F32))
            dki_parts.append(lax.dot_general(da, qd_b[sl], TN_DIMS, preferred_element_type=F32))
        dv_intra = jnp.concatenate(dv_parts, axis=0)
        dqd_intra = jnp.concatenate(dqd_parts, axis=0)
        dki = jnp.concatenate(dki_parts, axis=0)

        _chunk_outer(do, qd_b, kv_scr, sb)

        def rrec(i, dst):
            n = nc - 1 - i
            dst_scr[n] = dst
            d = dec_scr[pl.ds(pl.multiple_of(n * CHUNK, CHUNK), 1), :]
            return dst * d + kv_scr[n]

        dcarry[...] = lax.fori_loop(0, nc, rrec, dcarry[...])
        for n in range(nc):
            rows = slice(n * CHUNK, (n + 1) * CHUNK)
            dst_b = dst_scr[n].astype(BF16)
            dvi_scr[rows, :] = lax.dot_general(ke_b[rows], dst_b, NT_DIMS, preferred_element_type=F32)
            dke_scr[rows, :] = jnp.dot(v_b[rows], dst_b, preferred_element_type=F32)
            dqi_scr[rows, :] = jnp.dot(do_b[rows], st_scr[n].astype(BF16), preferred_element_type=F32)
        ddec = jnp.sum(dst_scr[...] * st_scr[...], axis=1)
        dgl = jnp.broadcast_to(ddec[:, None, :], (nc, CHUNK, HEAD)).reshape(sb, HEAD) * c["dec"]

        dqd = dqd_intra + dqi_scr[...]
        dke = dke_scr[...]
        t_ke = dke * c["ke"]
        dG = dqd * c["qd"] - dki * c["ki"] - t_ke
        dgl = dgl + _chunk_cumsum(t_ke) + _chunk_cumsum(t_ke, reverse=True) - t_ke
        dlogf = _chunk_cumsum(dG, reverse=True) + dgl
        dk = dki * c["e_ng"] + dke * c["e_ge"]
        df = dlogf / c["f"] - dk
        sg, sq, lb = c["sg"], c["sq"], c["lb"]
        vec_ref[0:1, :] += jnp.sum(df * (1.0 - sg), axis=0, keepdims=True)
        d_ref[0, 0] = (dqd * c["e_g"] * Q_SCALE * (sq + q * sq * (1.0 - sq))).astype(BF16)
        d_ref[1, 0] = (df * (1.0 - lb) * sg * (1.0 - sg)).astype(BF16)
        d_ref[2, 0] = (dv_intra + dvi_scr[...]).astype(BF16)
        d_ref[3, 0] = d_og.astype(BF16)

    rev = lambda s: nsb - 1 - s
    big = pltpu.VMEM((nc, HEAD, HEAD), F32)
    rows_f32 = pltpu.VMEM((sb, HEAD), F32)
    return pl.pallas_call(
        body, name="hgrn_bwd", grid=(H, Bl, nsb),
        in_specs=[pl.BlockSpec((4, 1, sb, HEAD), lambda h, b, s: (0, b, rev(s), h)),
                  pl.BlockSpec((2, HEAD), lambda h, b, s: (0, h)),
                  pl.BlockSpec((1, HEAD), lambda h, b, s: (0, h)),
                  pl.BlockSpec((1, sb, HEAD), lambda h, b, s: (b, rev(s), h)),
                  pl.BlockSpec((1, 1, 1, HEAD, HEAD), lambda h, b, s: (b, h, rev(s), 0, 0))],
        out_specs=[pl.BlockSpec((4, 1, sb, HEAD), lambda h, b, s: (0, b, rev(s), h)),
                   pl.BlockSpec((8, HEAD), lambda h, b, s: (0, h))],
        out_shape=[jax.ShapeDtypeStruct((4, Bl, S, D), BF16), jax.ShapeDtypeStruct((8, D), F32)],
        scratch_shapes=[pltpu.VMEM((HEAD, HEAD), F32), rows_f32, big, big, big, rows_f32,
                        rows_f32, rows_f32, rows_f32],
        compiler_params=_params(("parallel", "arbitrary", "arbitrary")),
    )(proj5, lb_logits, gn, dain3, st0_all)


def _section_sources(j, refs, fn):
    dh_ref, dp_ref, dg_ref = refs

    @pl.when(j < 4)
    def _():
        fn(dh_ref[0])

    @pl.when(j == 4)
    def _():
        fn(dp_ref[...])

    @pl.when(j > 4)
    def _():
        fn(dg_ref[0])


def _section_specs(tm, D, order):
    pick = (lambda a, b: (b, a)) if order == "ji" else (lambda a, b: (a, b))

    def at(fn):
        return lambda a, b: fn(*pick(a, b))

    return [pl.BlockSpec((1, tm, D), at(lambda i, j: (jnp.minimum(j, 3), i, 0))),
            pl.BlockSpec((tm, D), at(lambda i, j: (i, 0))),
            pl.BlockSpec((1, tm, D), at(lambda i, j: (jnp.clip(j - 5, 0, 1), i, 0)))]


def _dx(d1, dh4, dpv, dg2, w_in):
    T, D = d1.shape
    tm = min(512, T)

    def body(d1_ref, dh_ref, dp_ref, dg_ref, w_ref, o_ref):
        j = pl.program_id(1)

        @pl.when(j == 0)
        def _():
            o_ref[...] = ALPHA * d1_ref[...]

        def add(blk):
            o_ref[...] += lax.dot_general(blk, w_ref[...], NT_DIMS, preferred_element_type=F32)

        _section_sources(j, (dh_ref, dp_ref, dg_ref), add)

    return pl.pallas_call(
        body, name="dx", grid=(T // tm, N_SEC),
        in_specs=[pl.BlockSpec((tm, D), lambda i, j: (i, 0))] + _section_specs(tm, D, "ij")
        + [pl.BlockSpec((D, D), lambda i, j: (0, j))],
        out_specs=pl.BlockSpec((tm, D), lambda i, j: (i, 0)),
        out_shape=jax.ShapeDtypeStruct((T, D), F32),
        compiler_params=_params(("parallel", "arbitrary")),
    )(d1, dh4, dpv, dg2, w_in)


def _dw_in(x2, dh4, dpv, dg2):
    T, D = x2.shape
    tk = min(512, T)

    def body(x_ref, dh_ref, dp_ref, dg_ref, o_ref):
        j, t = pl.program_id(0), pl.program_id(1)
        xb = x_ref[...].astype(BF16)

        def acc(blk):
            p = lax.dot_general(xb, blk, TN_DIMS, preferred_element_type=F32)

            @pl.when(t == 0)
            def _():
                o_ref[...] = p

            @pl.when(t > 0)
            def _():
                o_ref[...] += p

        _section_sources(j, (dh_ref, dp_ref, dg_ref), acc)

    return pl.pallas_call(
        body, name="dw_in", grid=(N_SEC, T // tk),
        in_specs=[pl.BlockSpec((tk, D), lambda j, t: (t, 0))] + _section_specs(tk, D, "ji"),
        out_specs=pl.BlockSpec((D, D), lambda j, t: (0, j)),
        out_shape=jax.ShapeDtypeStruct((D, N_SEC * D), F32),
        compiler_params=_params(("parallel", "arbitrary")),
    )(x2, dh4, dpv, dg2)


class _Sharded:
    def __init__(self, kind, shard_shape):
        self.kind, self.shape = kind, tuple(shard_shape)

    def at(self, ref, d):
        if self.kind == "col":
            n = self.shape[1]
            return ref.at[:, pl.ds(pl.multiple_of(d * n, 128), n)]
        if self.kind == "row":
            n = self.shape[0]
            return ref.at[pl.ds(pl.multiple_of(d * n, 8), n), :]
        n = self.shape[1]
        return ref.at[:, pl.ds(pl.multiple_of(d * n, 8), n), :]

    def block_index(self, d):
        return {"col": (0, d), "row": (d, 0), "pool": (0, d, 0)}[self.kind]


def _rs_stage_a(grads, layouts):
    nw = len(grads)

    def body(*refs):
        ins, outs = refs[:nw], refs[nw:2 * nw]
        send_sems, recv_sems = refs[2 * nw:]
        x, y, c = _me()
        copies = []
        for w in range(nw):
            for chip in range(N_CHIP):
                cp = pltpu.make_async_remote_copy(
                    src_ref=layouts[w].at(ins[w], 2 * chip + (1 - c)), dst_ref=outs[w].at[chip],
                    send_sem=send_sems.at[w, chip], recv_sem=recv_sems.at[w, chip],
                    device_id=(x, y, 1 - c), device_id_type=MESH)
                cp.start()
                copies.append(cp)
        for cp in copies:
            cp.wait()

    any_spec = pl.BlockSpec(memory_space=pl.ANY)
    return pl.pallas_call(
        body, name="rs_d2d",
        out_shape=[jax.ShapeDtypeStruct((N_CHIP,) + l.shape, F32) for l in layouts],
        in_specs=[any_spec] * nw, out_specs=[any_spec] * nw,
        scratch_shapes=[pltpu.SemaphoreType.DMA((nw, N_CHIP)), pltpu.SemaphoreType.DMA((nw, N_CHIP))],
    )(*grads)


def _rs_sum_a(name, c_arr, grad, recv, layout):
    blk = layout.shape

    def body(c_ref, g_ref, r_ref, pf_ref, pb_ref):
        s = g_ref[...] + r_ref[0]
        pf_ref[0] = s
        pb_ref[0] = s.astype(BF16)

    lead = (lambda i, c_ref: (i,) + (0,) * len(blk))
    grid_spec = pltpu.PrefetchScalarGridSpec(
        num_scalar_prefetch=1, grid=(N_CHIP,),
        in_specs=[pl.BlockSpec(blk, lambda i, c_ref: layout.block_index(2 * i + c_ref[0])),
                  pl.BlockSpec((1,) + blk, lead)],
        out_specs=[pl.BlockSpec((1,) + blk, lead), pl.BlockSpec((1,) + blk, lead)])
    return pl.pallas_call(
        body, name=name, grid_spec=grid_spec,
        out_shape=[jax.ShapeDtypeStruct((N_CHIP,) + blk, F32),
                   jax.ShapeDtypeStruct((N_CHIP,) + blk, BF16)],
        compiler_params=_params(("parallel",)),
    )(c_arr, grad, recv)


def _rs_stage_b(partials):
    nw = len(partials)

    def body(*refs):
        ins, outs = refs[:nw], refs[nw:2 * nw]
        send_sems, recv_sems = refs[2 * nw:]
        x, y, c = _me()
        chips = [(1 - x, y), (x, 1 - y), (1 - x, 1 - y)]
        copies = []
        for w in range(nw):
            for k, (px, py) in enumerate(chips):
                cp = pltpu.make_async_remote_copy(
                    src_ref=ins[w].at[2 * px + py], dst_ref=outs[w].at[k],
                    send_sem=send_sems.at[w, k], recv_sem=recv_sems.at[w, k],
                    device_id=(px, py, c), device_id_type=MESH)
                cp.start()
                copies.append(cp)
        for cp in copies:
            cp.wait()

    any_spec = pl.BlockSpec(memory_space=pl.ANY)
    return pl.pallas_call(
        body, name="rs_ici",
        out_shape=[jax.ShapeDtypeStruct((3,) + p.shape[1:], BF16) for p in partials],
        in_specs=[any_spec] * nw, out_specs=[any_spec] * nw,
        scratch_shapes=[pltpu.SemaphoreType.DMA((nw, 3)), pltpu.SemaphoreType.DMA((nw, 3))],
    )(*partials)


def _adam_shard(name, chip_arr, pf, recv, w, m, v, n_split):
    shape = w.shape
    blk = (shape[0] // n_split,) + tuple(shape[1:])
    zeros = (0,) * (len(shape) - 1)

    def body(chip_ref, pf_ref, r_ref, w_ref, m_ref, v_ref, g_out, d_out, m_out, v_out):
        g = pf_ref[0] + r_ref[0].astype(F32) + r_ref[1].astype(F32) + r_ref[2].astype(F32)
        d, m2, v2 = _adamw(w_ref[...], g, m_ref[...], v_ref[...])
        g_out[...] = g
        d_out[...] = d
        m_out[...] = m2
        v_out[...] = v2

    plain = pl.BlockSpec(blk, lambda i, chip_ref: (i,) + zeros)
    grid_spec = pltpu.PrefetchScalarGridSpec(
        num_scalar_prefetch=1, grid=(n_split,),
        in_specs=[pl.BlockSpec((1,) + blk, lambda i, chip_ref: (chip_ref[0], i) + zeros),
                  pl.BlockSpec((3,) + blk, lambda i, chip_ref: (0, i) + zeros),
                  plain, plain, plain],
        out_specs=[plain] * 4)
    return pl.pallas_call(
        body, name=name, grid_spec=grid_spec,
        out_shape=[jax.ShapeDtypeStruct(shape, F32)] * 4,
        compiler_params=_params(("parallel",)),
    )(chip_arr, pf, recv, w, m, v)


def _vec_allreduce_adam(vec, small_w, small_m, small_v):
    n = len(small_w)
    D = vec.shape[1]

    def body(*refs):
        vec_ref = refs[0]
        ws, ms, vs = refs[1:1 + n], refs[1 + n:1 + 2 * n], refs[1 + 2 * n:1 + 3 * n]
        outs = refs[1 + 3 * n:2 + 7 * n]
        gat, send_sems, recv_sems = refs[2 + 7 * n:]
        loss_ref, g_out, d_out = outs[0], outs[1:1 + n], outs[1 + n:1 + 2 * n]
        m_out, v_out = outs[1 + 2 * n:1 + 3 * n], outs[1 + 3 * n:1 + 4 * n]
        x, y, c = _me()
        me = 4 * x + 2 * y + c
        gat[me] = vec_ref[...]
        copies = []
        for k in range(1, N_DEV):
            fx, fy, fc = (k >> 2) & 1, (k >> 1) & 1, k & 1
            to = (1 - x if fx else x, 1 - y if fy else y, 1 - c if fc else c)
            cp = pltpu.make_async_remote_copy(
                src_ref=vec_ref, dst_ref=gat.at[me], send_sem=send_sems.at[k - 1],
                recv_sem=recv_sems.at[k - 1], device_id=to, device_id_type=MESH)
            cp.start()
            copies.append(cp)
        for cp in copies:
            cp.wait()
        tot = gat[0]
        for d in range(1, N_DEV):
            tot = tot + gat[d]
        loss_ref[...] = jnp.broadcast_to(jnp.sum(tot[7:8, :], axis=1, keepdims=True), loss_ref.shape)
        lbl = ws[0][...]
        mx = jnp.maximum(lbl[0:1, :], lbl[1:2, :])
        e0, e1 = jnp.exp(lbl[0:1, :] - mx), jnp.exp(lbl[1:2, :] - mx)
        p0 = e0 / (e0 + e1)
        dl0 = tot[0:1, :] * p0 * (1.0 - p0)
        grads = [jnp.concatenate([dl0, -dl0], axis=0)] + [tot[r:r + 1, :] for r in range(1, n)]
        for i in range(n):
            d, m2, v2 = _adamw(ws[i][...], grads[i], ms[i][...], vs[i][...])
            g_out[i][...] = grads[i]
            d_out[i][...] = d
            m_out[i][...] = m2
            v_out[i][...] = v2

    vm = pl.BlockSpec(memory_space=pltpu.VMEM)
    shapes = [jax.ShapeDtypeStruct(w.shape, F32) for w in small_w]
    return pl.pallas_call(
        body, name="vec_allreduce_adam",
        out_shape=[jax.ShapeDtypeStruct((1, 128), F32)] + shapes * 4,
        in_specs=[vm] * (1 + 3 * n), out_specs=[vm] * (1 + 4 * n),
        scratch_shapes=[pltpu.VMEM((N_DEV, 8, D), F32), pltpu.SemaphoreType.DMA((N_DEV - 1,)),
                        pltpu.SemaphoreType.DMA((N_DEV - 1,))],
    )(vec, *small_w, *small_m, *small_v)


def kernel(x, w_in, lb_logits, hgrn_norm_g, w_a, w_pool, pool_scale, w_out, ln1_g, ln1_b, w_up, w_down, ln2_g, ln2_b, loss_target, m_w_in, m_lb_logits, m_hgrn_norm_g, m_w_a, m_w_pool, m_pool_scale, m_w_out, m_ln1_g, m_ln1_b, m_w_up, m_w_down, m_ln2_g, m_ln2_b, v_w_in, v_lb_logits, v_hgrn_norm_g, v_w_a, v_w_pool, v_pool_scale, v_w_out, v_ln1_g, v_ln1_b, v_w_up, v_w_down, v_ln2_g, v_ln2_b):
    Bl, S, D = x.shape
    T = Bl * S
    FF = 4 * D
    pg = D // POOL_GROUPS
    x2 = x.reshape(T, D)
    tgt = loss_target.reshape(T, D)

    big_w = [w_in[0], w_a[0], w_pool[0], w_out[0], w_up[0], w_down[0]]
    g_in, g_a, g_pool, g_out, g_up, g_down = _all_gather([w.astype(BF16) for w in big_w])
    w_in_f = jnp.transpose(g_in, (1, 0, 2)).reshape(D, N_SEC * D)
    w_up_f = jnp.transpose(g_up, (1, 0, 2)).reshape(D, FF)
    w_a_f = g_a.reshape(D, D)
    w_out_f = g_out.reshape(D, D)
    w_down_f = g_down.reshape(FF, D)
    w_pool_f = jnp.transpose(g_pool, (1, 0, 2, 3)).reshape(POOL_GROUPS, pg, pg)

    proj = _proj(x2, w_in_f)
    proj5 = proj.reshape(N_SEC, Bl, S, D)
    ain3, st0_all = _hgrn_fwd(proj5, lb_logits, hgrn_norm_g)
    pooled3, bp3 = _pool_fwd(proj5, w_pool_f)
    ain, pooled, bp = ain3.reshape(T, D), pooled3.reshape(T, D), bp3.reshape(T, D)
    a, merged, xhat1, rs1, x1b = _mix_fwd(ain, proj, bp, x2, w_a_f, w_out_f, pool_scale, ln1_g, ln1_b)
    hp, dr2, dr2b, vec_mlp = _mlp_fwd(x1b, w_up_f, w_down_f, xhat1, tgt, ln1_g, ln1_b, ln2_g, ln2_b)

    dhp, dr1, dr1b, vec_ln1 = _mlp_bwd(dr2b, dr2, hp, w_up_f, w_down_f, xhat1, rs1, ln1_g)
    tk = min(512, T)
    gw_down = _mm_tn("dw_down", hp, dr2b, N_DEV, (tk, 512), lambda j, t: (t, j), (tk, D), lambda j, t: (t, 0),
                     (FF, D), (512, D), lambda j, t: (j, 0),
                     a_fn=lambda h: jnp.square(jnp.maximum(h, 0.0)))
    gw_up = _mm_tn("dw_up", x1b, dhp, N_DEV, (tk, D), lambda j, t: (t, 0), (tk, 512), lambda j, t: (t, j),
                   (D, FF), (D, 512), lambda j, t: (0, j))
    da_b, dbp_b, dain, dpooled, dg2, vec_mix = _mix_bwd(dr1b, proj, a, bp, w_a_f, w_out_f, w_pool_f, pool_scale)
    gw_out = _mm_tn("dw_out", merged, dr1b, 2, (tk, D), lambda j, t: (t, 0), (tk, D // 2), lambda j, t: (t, j),
                    (D, D), (D, D // 2), lambda j, t: (0, j))
    gw_a = _mm_tn("dw_a", ain, da_b, 2, (tk, D), lambda j, t: (t, 0), (tk, D // 2), lambda j, t: (t, j),
                  (D, D), (D, D // 2), lambda j, t: (0, j))
    gw_pool = _mm_tn("dw_pool", pooled, dbp_b, POOL_GROUPS, (tk, pg), lambda j, t: (t, j), (tk, pg),
                     lambda j, t: (t, j), (POOL_GROUPS, pg, pg), (1, pg, pg), lambda j, t: (j, 0, 0))
    dpv = _pool_bwd(dpooled.reshape(Bl, S, D)).reshape(T, D)
    dh4, vec_hgrn = _hgrn_bwd(proj5, lb_logits, hgrn_norm_g, dain.reshape(Bl, S, D), st0_all)
    dh4 = dh4.reshape(4, T, D)
    grad_x = _dx(dr1, dh4, dpv, dg2, w_in_f).reshape(Bl, S, D)
    gw_in = _dw_in(x2, dh4, dpv, dg2)

    names = ["w_in", "w_a", "w_pool", "w_out", "w_up", "w_down"]
    grads = [gw_in, gw_a, gw_pool, gw_out, gw_up, gw_down]
    layouts = [_Sharded("col", w_in.shape[1:]), _Sharded("row", w_a.shape[1:]),
               _Sharded("pool", w_pool.shape[1:]), _Sharded("row", w_out.shape[1:]),
               _Sharded("col", w_up.shape[1:]), _Sharded("row", w_down.shape[1:])]
    ms = [m_w_in, m_w_a, m_w_pool, m_w_out, m_w_up, m_w_down]
    vs = [v_w_in, v_w_a, v_w_pool, v_w_out, v_w_up, v_w_down]
    c_arr = jnp.reshape(lax.axis_index("c"), (1,)).astype(jnp.int32)
    chip_arr = jnp.reshape(2 * lax.axis_index("x") + lax.axis_index("y"), (1,)).astype(jnp.int32)
    recv_a = _rs_stage_a(grads, layouts)
    sums = [_rs_sum_a("rs_sum_" + nm, c_arr, g, r, l) for nm, g, r, l in zip(names, grads, recv_a, layouts)]
    recv_b = _rs_stage_b([s[1] for s in sums])
    big = {}
    for nm, s, r, w, m, v in zip(names, sums, recv_b, big_w, ms, vs):
        g, d, m2, v2 = _adam_shard("adam_" + nm, chip_arr, s[0], r, w, m[0], v[0], 4)
        big[nm] = tuple(t[None] for t in (g, d, m2, v2))

    vec = vec_mlp + vec_ln1 + vec_mix + vec_hgrn
    small_names = ["lb_logits", "hgrn_norm_g", "pool_scale", "ln1_g", "ln1_b", "ln2_g", "ln2_b"]
    small_w = [lb_logits, hgrn_norm_g, pool_scale, ln1_g, ln1_b, ln2_g, ln2_b]
    small_m = [m_lb_logits, m_hgrn_norm_g, m_pool_scale, m_ln1_g, m_ln1_b, m_ln2_g, m_ln2_b]
    small_v = [v_lb_logits, v_hgrn_norm_g, v_pool_scale, v_ln1_g, v_ln1_b, v_ln2_g, v_ln2_b]
    res = _vec_allreduce_adam(vec, small_w, small_m, small_v)
    loss = res[0][0, 0]
    n = len(small_w)
    small = {nm: (res[1 + i], res[1 + n + i], res[1 + 2 * n + i], res[1 + 3 * n + i])
             for i, nm in enumerate(small_names)}

    order = ["w_in", "lb_logits", "hgrn_norm_g", "w_a", "w_pool", "pool_scale", "w_out", "ln1_g", "ln1_b",
             "w_up", "w_down", "ln2_g", "ln2_b"]
    allp = {**big, **small}
    out = [loss, grad_x]
    for part in range(4):
        out += [allp[nm][part] for nm in order]
    return tuple(out)
```

```python
import jax
import jax.numpy as jnp
from jax import lax
from jax.experimental import pallas as pl
from jax.experimental.pallas import tpu as pltpu

F32 = jnp.float32
BF16 = jnp.bfloat16
MESH = pl.DeviceIdType.MESH

N_DEV = 8
N_CHIP = 4
HEAD = 128
CHUNK = 16
GROUP = 128
CH_PER_GROUP = GROUP // CHUNK
N_SEC = 7
POOL_GROUPS = 4
ALPHA = (2.0 * 1) ** 0.25
LN_EPS = 1e-5
RMS_EPS = 1e-6
Q_SCALE = HEAD ** -0.5
ADAM_LR = 0.001
ADAM_B1 = 0.9
ADAM_B2 = 0.999
ADAM_EPS = 1e-08
ADAM_WD = 0.01
ADAM_STEP = 10
VMEM_LIMIT = 56 << 20

NT_DIMS = (((1,), (1,)), ((), ()))
TN_DIMS = (((0,), (0,)), ((), ()))


def _params(sem=None):
    kw = dict(vmem_limit_bytes=VMEM_LIMIT)
    if sem is not None:
        kw["dimension_semantics"] = sem
    return pltpu.CompilerParams(**kw)


def _me():
    return lax.axis_index("x"), lax.axis_index("y"), lax.axis_index("c")


def _sigmoid(v):
    return jax.nn.sigmoid(v)


def _adamw(w, g, m, v):
    m = ADAM_B1 * m + (1.0 - ADAM_B1) * g
    v = ADAM_B2 * v + (1.0 - ADAM_B2) * jnp.square(g)
    m_hat = m / (1.0 - ADAM_B1 ** ADAM_STEP)
    v_hat = v / (1.0 - ADAM_B2 ** ADAM_STEP)
    delta = -ADAM_LR * (m_hat / (jnp.sqrt(v_hat) + ADAM_EPS) + ADAM_WD * w)
    return delta, m, v


class _Sharded:
    def __init__(self, kind, shard_shape):
        self.kind, self.shape = kind, tuple(shard_shape)

    @property
    def full_shape(self):
        r = self.shape
        if self.kind == "row":
            return (N_DEV * r[0],) + r[1:]
        return (r[0], N_DEV * r[1]) + r[2:]

    def at(self, ref, d):
        if self.kind == "col":
            n = self.shape[1]
            return ref.at[:, pl.ds(pl.multiple_of(d * n, 128), n)]
        if self.kind == "row":
            n = self.shape[0]
            return ref.at[pl.ds(pl.multiple_of(d * n, 16), n), :]
        n = self.shape[1]
        return ref.at[:, pl.ds(pl.multiple_of(d * n, 16), n), :]

    def block_index(self, d):
        return {"col": (0, d), "row": (d, 0), "pool": (0, d, 0)}[self.kind]

    def start_index(self, d):
        r = self.shape
        if self.kind == "row":
            return (d * r[0],) + (0,) * (len(r) - 1)
        return (0, d * r[1]) + (0,) * (len(r) - 2)


def _peer(k, x, y, c):
    return (1 - x if k & 4 else x, 1 - y if k & 2 else y, 1 - c if k & 1 else c)


def _all_gather(name, shards, layouts):
    nw = len(shards)

    def body(*refs):
        ins, outs = refs[:nw], refs[nw:2 * nw]
        send_sems, recv_sems, local_sems = refs[2 * nw:]
        x, y, c = _me()
        me = (x, y, c)
        sibling = (x, y, 1 - c)
        chips = [(1 - x, y), (x, 1 - y), (1 - x, 1 - y)]

        def copy(w, k, block, to, src=None):
            px, py, pc = block
            dst = layouts[w].at(outs[w], 4 * px + 2 * py + pc)
            return pltpu.make_async_remote_copy(
                src_ref=dst if src is None else src, dst_ref=dst,
                send_sem=send_sems.at[w, k], recv_sem=recv_sems.at[w, k],
                device_id=to, device_id_type=MESH)

        local = []
        for w in range(nw):
            mine = pltpu.make_async_copy(ins[w], layouts[w].at(outs[w], 4 * x + 2 * y + c), local_sems.at[w])
            mine.start()
            local.append(mine)
        first = []
        for w in range(nw):
            first.append(copy(w, 0, me, sibling, src=ins[w]))
            first += [copy(w, 1 + j, me, (*chip, c), src=ins[w]) for j, chip in enumerate(chips)]
        for cp in first:
            cp.start()
        passed = []
        for w in range(nw):
            for j, chip in enumerate(chips):
                copy(w, 1 + j, (*chip, c), me).wait_recv()
                fwd = copy(w, 4 + j, (*chip, c), sibling)
                fwd.start()
                passed.append(fwd)
        for w in range(nw):
            copy(w, 0, sibling, me).wait_recv()
            for j, chip in enumerate(chips):
                copy(w, 4 + j, (*chip, 1 - c), me).wait_recv()
        for cp in first + passed:
            cp.wait_send()
        for cp in local:
            cp.wait()

    any_spec = pl.BlockSpec(memory_space=pl.ANY)
    return pl.pallas_call(
        body, name=name,
        out_shape=[jax.ShapeDtypeStruct(l.full_shape, s.dtype) for s, l in zip(shards, layouts)],
        in_specs=[any_spec] * nw, out_specs=[any_spec] * nw,
        scratch_shapes=[pltpu.SemaphoreType.DMA((nw, 7)), pltpu.SemaphoreType.DMA((nw, 7)),
                        pltpu.SemaphoreType.DMA((nw,))],
    )(*shards)


HBM_SPEC = pl.BlockSpec(memory_space=pltpu.HBM)
SEM_SPEC = pl.BlockSpec(memory_space=pltpu.SEMAPHORE)
DATAFLOW = pltpu.SideEffectType.DATAFLOW_SIDE_EFFECTING


def _exchange_copies(srcs, lands, send_sems, recv_sems, src_at, dst_at):
    x, y, c = _me()
    me = 4 * x + 2 * y + c
    copies = []
    for w in range(len(srcs)):
        for k in range(1, N_DEV):
            px, py, pc = _peer(k, x, y, c)
            copies.append(pltpu.make_async_remote_copy(
                src_ref=src_at(w, srcs[w], 4 * px + 2 * py + pc), dst_ref=dst_at(w, lands[w], me, k),
                send_sem=send_sems.at[w * (N_DEV - 1) + k - 1], recv_sem=recv_sems.at[w * (N_DEV - 1) + k - 1],
                device_id=(px, py, pc), device_id_type=MESH))
    return copies


def _exchange_start(name, srcs, lands, src_at, dst_at, after):
    nw = len(srcs)

    def body(*refs):
        src_refs, land_refs = refs[:nw], refs[nw:2 * nw]
        send_sems, recv_sems = refs[2 * nw + 1], refs[2 * nw + 2]
        token = refs[-1]
        for cp in _exchange_copies(src_refs, land_refs, send_sems, recv_sems, src_at, dst_at):
            cp.start()
        token[...] = jnp.zeros_like(token)

    hbm = lambda a: pltpu.HBM(a.shape, a.dtype)
    outs = pl.pallas_call(
        body, name=name,
        out_shape=(pltpu.SemaphoreType.DMA((nw * (N_DEV - 1),)), pltpu.SemaphoreType.DMA((nw * (N_DEV - 1),)),
                   *[hbm(a) for a in srcs], *[hbm(a) for a in lands],
                   jax.ShapeDtypeStruct((8, 128), F32)),
        in_specs=[HBM_SPEC] * (2 * nw) + [pl.BlockSpec(memory_space=pl.ANY)],
        out_specs=(SEM_SPEC, SEM_SPEC, *[HBM_SPEC] * (2 * nw), pl.BlockSpec(memory_space=pltpu.VMEM)),
        input_output_aliases={i: 2 + i for i in range(2 * nw)},
        compiler_params=pltpu.CompilerParams(has_side_effects=DATAFLOW),
    )(*[pltpu.with_memory_space_constraint(a, pltpu.HBM) for a in list(srcs) + list(lands)], after)
    return dict(send=outs[0], recv=outs[1], srcs=outs[2:2 + nw], lands=outs[2 + nw:2 + 2 * nw],
                token=outs[-1], src_at=src_at, dst_at=dst_at, name=name)


def _exchange_wait(pending, after):
    nw = len(pending["srcs"])

    def body(*refs):
        src_refs, land_refs = refs[:nw], refs[nw:2 * nw]
        send_sems, recv_sems = refs[2 * nw], refs[2 * nw + 1]
        for cp in _exchange_copies(src_refs, land_refs, send_sems, recv_sems,
                                   pending["src_at"], pending["dst_at"]):
            cp.wait_send()
            cp.wait_recv()

    hbm = lambda a: pltpu.HBM(a.shape, a.dtype)
    outs = pl.pallas_call(
        body, name=pending["name"] + "_wait",
        out_shape=(*[hbm(a) for a in pending["srcs"]], *[hbm(a) for a in pending["lands"]]),
        in_specs=[HBM_SPEC] * (2 * nw) + [SEM_SPEC, SEM_SPEC, pl.BlockSpec(memory_space=pl.ANY)],
        out_specs=tuple([HBM_SPEC] * (2 * nw)),
        input_output_aliases={i: i for i in range(2 * nw)},
        compiler_params=pltpu.CompilerParams(has_side_effects=DATAFLOW),
    )(*pending["srcs"], *pending["lands"], pending["send"], pending["recv"], after)
    return outs[nw:]


def _call_after(dep, body, args, *, in_specs, **kw):
    n_in = len(args)

    def wrapped(*refs):
        body(*refs[:n_in], *refs[n_in + 1:])

    dep_spec = pl.BlockSpec(dep.shape, lambda *_: (0,) * dep.ndim)
    return pl.pallas_call(wrapped, in_specs=list(in_specs) + [dep_spec], **kw)(*args, dep)


def _proj(x2, w_in, dep):
    T, D = x2.shape
    tm = min(512, T)

    def body(x_ref, w_ref, o_ref):
        o_ref[0] = jnp.dot(x_ref[...].astype(BF16), w_ref[...], preferred_element_type=F32)

    return _call_after(
        dep, body, (x2, w_in), name="proj", grid=(N_SEC, T // tm),
        in_specs=[pl.BlockSpec((tm, D), lambda j, i: (i, 0)),
                  pl.BlockSpec((D, D), lambda j, i: (0, j))],
        out_specs=pl.BlockSpec((1, tm, D), lambda j, i: (j, i, 0)),
        out_shape=jax.ShapeDtypeStruct((N_SEC, T, D), F32),
        compiler_params=_params(("parallel", "parallel")))


def _chunk_cumsum(v, reverse=False):
    rows = v.shape[0]
    pos = lax.broadcasted_iota(jnp.int32, v.shape, 0) % CHUNK
    for sh in (1, 2, 4, 8):
        if reverse:
            v = v + jnp.where(pos < CHUNK - sh, pltpu.roll(v, rows - sh, 0), 0.0)
        else:
            v = v + jnp.where(pos >= sh, pltpu.roll(v, sh, 0), 0.0)
    return v


def _hgrn_gates(q, f_pre, lb_logits):
    l0, l1 = lb_logits[0:1, :], lb_logits[1:2, :]
    mx = jnp.maximum(l0, l1)
    e0, e1 = jnp.exp(l0 - mx), jnp.exp(l1 - mx)
    lb = e0 / (e0 + e1)
    sq = _sigmoid(q)
    qf = q * sq * Q_SCALE
    sg = _sigmoid(f_pre)
    f = lb + (1.0 - lb) * sg
    k = 1.0 - f
    log_f = jnp.log(f)
    G = _chunk_cumsum(log_f)
    g_to_end = _chunk_cumsum(log_f, reverse=True) - log_f
    e_g = jnp.exp(G)
    e_ng = jnp.exp(-G)
    e_ge = jnp.exp(g_to_end)
    return dict(lb=lb, sq=sq, qf=qf, sg=sg, f=f, k=k, G=G, e_g=e_g, e_ng=e_ng, e_ge=e_ge,
                qd=qf * e_g, ki=k * e_ng, ke=k * e_ge, dec=jnp.exp(G + g_to_end))


def _intra_mask():
    r = lax.broadcasted_iota(jnp.int32, (GROUP, GROUP), 0)
    c = lax.broadcasted_iota(jnp.int32, (GROUP, GROUP), 1)
    return (r // CHUNK == c // CHUNK) & (c <= r)


def _chunk_outer(lhs_rows, rhs_b, out_scr, sb):
    lane = lax.broadcasted_iota(jnp.int32, (GROUP, GROUP), 1) // CHUNK
    for g in range(sb // GROUP):
        sl = slice(g * GROUP, (g + 1) * GROUP)
        lhs_t = lhs_rows[sl].T
        for cc in range(CH_PER_GROUP):
            masked = jnp.where(lane == cc, lhs_t, 0.0).astype(BF16)
            out_scr[g * CH_PER_GROUP + cc] = jnp.dot(masked, rhs_b[sl], preferred_element_type=F32)


def _hgrn_forward_block(c, v, st0, sb, o_scr, kv_scr, st_scr, dec_scr):
    nc = sb // CHUNK
    qd_b, ki_b, ke_b, v_b = (c["qd"].astype(BF16), c["ki"].astype(BF16), c["ke"].astype(BF16),
                             v.astype(BF16))
    mask = _intra_mask()
    for g in range(sb // GROUP):
        sl = slice(g * GROUP, (g + 1) * GROUP)
        sc = lax.dot_general(qd_b[sl], ki_b[sl], NT_DIMS, preferred_element_type=F32)
        a = jnp.where(mask, sc, 0.0).astype(BF16)
        o_scr[sl, :] = jnp.dot(a, v_b[sl], preferred_element_type=F32)
    _chunk_outer(v, ke_b, kv_scr, sb)
    dec_scr[...] = c["dec"]

    def rec(n, st):
        st_scr[n] = st
        d = dec_scr[pl.ds(pl.multiple_of(n * CHUNK, CHUNK), 1), :]
        return st * d + kv_scr[n]

    st_end = lax.fori_loop(0, nc, rec, st0)

    for n in range(nc):
        rows = slice(n * CHUNK, (n + 1) * CHUNK)
        o_scr[rows, :] += lax.dot_general(qd_b[rows], st_scr[n].astype(BF16), NT_DIMS,
                                          preferred_element_type=F32)
    return st_end


def _hgrn_fwd(proj5, lb_logits, gn):
    _, Bl, S, D = proj5.shape
    H = D // HEAD
    sb = min(512, S)
    nsb = S // sb
    nc = sb // CHUNK

    def body(p_ref, lbl_ref, gn_ref, ain_ref, st0_ref, carry, o_scr, kv_scr, st_scr, dec_scr):
        s = pl.program_id(2)

        @pl.when(s == 0)
        def _():
            carry[...] = jnp.zeros_like(carry)

        st0 = carry[...]
        st0_ref[0, 0, 0] = st0
        c = _hgrn_gates(p_ref[0, 0], p_ref[1, 0], lbl_ref[...])
        carry[...] = _hgrn_forward_block(c, p_ref[2, 0], st0, sb, o_scr, kv_scr, st_scr, dec_scr)
        o = o_scr[...]
        rinv = lax.rsqrt(jnp.mean(o * o, axis=-1, keepdims=True) + RMS_EPS)
        ain_ref[0] = (o * rinv * gn_ref[...] * _sigmoid(p_ref[3, 0])).astype(BF16)

    return pl.pallas_call(
        body, name="hgrn_fwd", grid=(H, Bl, nsb),
        in_specs=[pl.BlockSpec((4, 1, sb, HEAD), lambda h, b, s: (0, b, s, h)),
                  pl.BlockSpec((2, HEAD), lambda h, b, s: (0, h)),
                  pl.BlockSpec((1, HEAD), lambda h, b, s: (0, h))],
        out_specs=[pl.BlockSpec((1, sb, HEAD), lambda h, b, s: (b, s, h)),
                   pl.BlockSpec((1, 1, 1, HEAD, HEAD), lambda h, b, s: (b, h, s, 0, 0))],
        out_shape=[jax.ShapeDtypeStruct((Bl, S, D), BF16),
                   jax.ShapeDtypeStruct((Bl, H, nsb, HEAD, HEAD), F32)],
        scratch_shapes=[pltpu.VMEM((HEAD, HEAD), F32), pltpu.VMEM((sb, HEAD), F32),
                        pltpu.VMEM((nc, HEAD, HEAD), F32), pltpu.VMEM((nc, HEAD, HEAD), F32),
                        pltpu.VMEM((sb, HEAD), F32)],
        compiler_params=_params(("parallel", "parallel", "arbitrary")),
    )(proj5, lb_logits, gn)


def _window_count(shape, g):
    pos = lax.broadcasted_iota(jnp.int32, shape, 0)
    return pos, jnp.minimum(pos + 1, jnp.left_shift(2, g)).astype(F32)


def _select_window(g, sums):
    return jnp.where(g == 0, sums[0], jnp.where(g == 1, sums[1], jnp.where(g == 2, sums[2], sums[3])))


def _pool_fwd(proj5, w_pool):
    _, Bl, S, D = proj5.shape
    pg = D // POOL_GROUPS

    def body(v_ref, w_ref, pooled_ref, bp_ref):
        g = pl.program_id(1)
        v = v_ref[0, 0]
        pos, cnt = _window_count(v.shape, g)
        cur, sums = v, []
        for sh in (1, 2, 4, 8):
            cur = cur + jnp.where(pos >= sh, pltpu.roll(cur, sh, 0), 0.0)
            sums.append(cur)
        pooled = (_select_window(g, sums) / cnt - v).astype(BF16)
        pooled_ref[0] = pooled
        bp_ref[0] = jnp.dot(pooled, w_ref[0], preferred_element_type=F32)

    return pl.pallas_call(
        body, name="pool_fwd", grid=(Bl, POOL_GROUPS),
        in_specs=[pl.BlockSpec((1, 1, S, pg), lambda b, g: (4, b, 0, g)),
                  pl.BlockSpec((1, pg, pg), lambda b, g: (g, 0, 0))],
        out_specs=[pl.BlockSpec((1, S, pg), lambda b, g: (b, 0, g)),
                   pl.BlockSpec((1, S, pg), lambda b, g: (b, 0, g))],
        out_shape=[jax.ShapeDtypeStruct((Bl, S, D), BF16), jax.ShapeDtypeStruct((Bl, S, D), F32)],
        compiler_params=_params(("parallel", "parallel")),
    )(proj5, w_pool)


def _layer_norm_fwd(r):
    mu = jnp.mean(r, axis=-1, keepdims=True)
    d = r - mu
    rs = lax.rsqrt(jnp.mean(d * d, axis=-1, keepdims=True) + LN_EPS)
    return d * rs, rs


def _layer_norm_bwd(dy_g, xhat, rs):
    return rs * (dy_g - jnp.mean(dy_g, axis=-1, keepdims=True)
                 - xhat * jnp.mean(dy_g * xhat, axis=-1, keepdims=True))


def _mix_fwd(ain, proj, bp, x2, w_a, w_out, ps, g1, b1):
    T, D = x2.shape
    tm = min(256, T)

    def body(ain_ref, ga_ref, gb_ref, bp_ref, x_ref, wa_ref, wo_ref, ps_ref, g1_ref, b1_ref,
             a_ref, mg_ref, xh_ref, rs_ref, x1b_ref):
        a = jnp.dot(ain_ref[...], wa_ref[...], preferred_element_type=F32)
        a_ref[...] = a
        merged = (_sigmoid(ga_ref[0]) * a + _sigmoid(gb_ref[0]) * (bp_ref[...] * ps_ref[...])).astype(BF16)
        mg_ref[...] = merged
        r1 = ALPHA * x_ref[...] + jnp.dot(merged, wo_ref[...], preferred_element_type=F32)
        xhat, rs = _layer_norm_fwd(r1)
        xh_ref[...] = xhat
        rs_ref[...] = rs
        x1b_ref[...] = (xhat * g1_ref[...] + b1_ref[...]).astype(BF16)

    row = lambda i: (i, 0)
    full = lambda i: (0, 0)
    return pl.pallas_call(
        body, name="mix_fwd", grid=(T // tm,),
        in_specs=[pl.BlockSpec((tm, D), row),
                  pl.BlockSpec((1, tm, D), lambda i: (5, i, 0)),
                  pl.BlockSpec((1, tm, D), lambda i: (6, i, 0)),
                  pl.BlockSpec((tm, D), row), pl.BlockSpec((tm, D), row),
                  pl.BlockSpec((D, D), full), pl.BlockSpec((D, D), full),
                  pl.BlockSpec((1, D), full), pl.BlockSpec((1, D), full), pl.BlockSpec((1, D), full)],
        out_specs=[pl.BlockSpec((tm, D), row), pl.BlockSpec((tm, D), row), pl.BlockSpec((tm, D), row),
                   pl.BlockSpec((tm, 1), row), pl.BlockSpec((tm, D), row)],
        out_shape=[jax.ShapeDtypeStruct((T, D), F32), jax.ShapeDtypeStruct((T, D), BF16),
                   jax.ShapeDtypeStruct((T, D), F32), jax.ShapeDtypeStruct((T, 1), F32),
                   jax.ShapeDtypeStruct((T, D), BF16)],
        compiler_params=_params(("parallel",)),
    )(ain, proj, proj, bp, x2, w_a, w_out, ps, g1, b1)


def _mlp_fwd(x1b, w_up, w_down, xhat1, tgt, g1, b1, g2, b2):
    T, D = xhat1.shape
    FF = w_up.shape[1]
    tm, tf = min(512, T), 512
    nf = FF // tf

    def body(x_ref, wu_ref, wd_ref, xh_ref, t_ref, g1_ref, b1_ref, g2_ref, b2_ref,
             hp_ref, dr_ref, drb_ref, vec_ref, acc):
        i, j = pl.program_id(0), pl.program_id(1)

        @pl.when((i == 0) & (j == 0))
        def _():
            vec_ref[...] = jnp.zeros_like(vec_ref)

        hp = jnp.dot(x_ref[...], wu_ref[...], preferred_element_type=F32)
        hp_ref[...] = hp
        part = jnp.dot(jnp.square(jnp.maximum(hp, 0.0)).astype(BF16), wd_ref[...],
                       preferred_element_type=F32)

        @pl.when(j == 0)
        def _():
            acc[...] = part

        @pl.when(j > 0)
        def _():
            acc[...] += part

        @pl.when(j == nf - 1)
        def _():
            x1 = xh_ref[...] * g1_ref[...] + b1_ref[...]
            xhat2, rs2 = _layer_norm_fwd(ALPHA * x1 + acc[...])
            err = xhat2 * g2_ref[...] + b2_ref[...] - t_ref[...]
            dy = err / D
            vec_ref[5:6, :] += jnp.sum(dy * xhat2, axis=0, keepdims=True)
            vec_ref[6:7, :] += jnp.sum(dy, axis=0, keepdims=True)
            vec_ref[7:8, :] += jnp.sum(0.5 * err * err / D, axis=0, keepdims=True)
            dr = _layer_norm_bwd(dy * g2_ref[...], xhat2, rs2)
            dr_ref[...] = dr
            drb_ref[...] = dr.astype(BF16)

    row = lambda i, j: (i, 0)
    full = lambda i, j: (0, 0)
    return pl.pallas_call(
        body, name="mlp_fwd", grid=(T // tm, nf),
        in_specs=[pl.BlockSpec((tm, D), row),
                  pl.BlockSpec((D, tf), lambda i, j: (0, j)),
                  pl.BlockSpec((tf, D), lambda i, j: (j, 0)),
                  pl.BlockSpec((tm, D), row), pl.BlockSpec((tm, D), row),
                  pl.BlockSpec((1, D), full), pl.BlockSpec((1, D), full),
                  pl.BlockSpec((1, D), full), pl.BlockSpec((1, D), full)],
        out_specs=[pl.BlockSpec((tm, tf), lambda i, j: (i, j)),
                   pl.BlockSpec((tm, D), row), pl.BlockSpec((tm, D), row),
                   pl.BlockSpec((8, D), full)],
        out_shape=[jax.ShapeDtypeStruct((T, FF), F32), jax.ShapeDtypeStruct((T, D), F32),
                   jax.ShapeDtypeStruct((T, D), BF16), jax.ShapeDtypeStruct((8, D), F32)],
        scratch_shapes=[pltpu.VMEM((tm, D), F32)],
        compiler_params=_params(("arbitrary", "arbitrary")),
    )(x1b, w_up, w_down, xhat1, tgt, g1, b1, g2, b2)


def _mlp_bwd(drb, dr, hp, w_up, w_down, xhat1, rs1, g1):
    T, D = dr.shape
    FF = hp.shape[1]
    tm, tf = min(512, T), 512
    nf = FF // tf

    def body(drb_ref, dr_ref, hp_ref, wu_ref, wd_ref, xh_ref, rs_ref, g1_ref,
             dhp_ref, d1_ref, d1b_ref, vec_ref, acc):
        i, j = pl.program_id(0), pl.program_id(1)

        @pl.when((i == 0) & (j == 0))
        def _():
            vec_ref[...] = jnp.zeros_like(vec_ref)

        dh = lax.dot_general(drb_ref[...], wd_ref[...], NT_DIMS, preferred_element_type=F32)
        dhp = (dh * (2.0 * jnp.maximum(hp_ref[...], 0.0))).astype(BF16)
        dhp_ref[...] = dhp
        part = lax.dot_general(dhp, wu_ref[...], NT_DIMS, preferred_element_type=F32)

        @pl.when(j == 0)
        def _():
            acc[...] = part

        @pl.when(j > 0)
        def _():
            acc[...] += part

        @pl.when(j == nf - 1)
        def _():
            dx1 = ALPHA * dr_ref[...] + acc[...]
            xhat = xh_ref[...]
            vec_ref[3:4, :] += jnp.sum(dx1 * xhat, axis=0, keepdims=True)
            vec_ref[4:5, :] += jnp.sum(dx1, axis=0, keepdims=True)
            d1 = _layer_norm_bwd(dx1 * g1_ref[...], xhat, rs_ref[...])
            d1_ref[...] = d1
            d1b_ref[...] = d1.astype(BF16)

    row = lambda i, j: (i, 0)
    full = lambda i, j: (0, 0)
    return pl.pallas_call(
        body, name="mlp_bwd", grid=(T // tm, nf),
        in_specs=[pl.BlockSpec((tm, D), row), pl.BlockSpec((tm, D), row),
                  pl.BlockSpec((tm, tf), lambda i, j: (i, j)),
                  pl.BlockSpec((D, tf), lambda i, j: (0, j)),
                  pl.BlockSpec((tf, D), lambda i, j: (j, 0)),
                  pl.BlockSpec((tm, D), row), pl.BlockSpec((tm, 1), row), pl.BlockSpec((1, D), full)],
        out_specs=[pl.BlockSpec((tm, tf), lambda i, j: (i, j)),
                   pl.BlockSpec((tm, D), row), pl.BlockSpec((tm, D), row),
                   pl.BlockSpec((8, D), full)],
        out_shape=[jax.ShapeDtypeStruct((T, FF), BF16), jax.ShapeDtypeStruct((T, D), F32),
                   jax.ShapeDtypeStruct((T, D), BF16), jax.ShapeDtypeStruct((8, D), F32)],
        scratch_shapes=[pltpu.VMEM((tm, D), F32)],
        compiler_params=_params(("arbitrary", "arbitrary")),
    )(drb, dr, hp, w_up, w_down, xhat1, rs1, g1)


def _mm_tn(name, a, b, n_j, a_block, a_map, b_block, b_map, o_shape, o_block, o_map, a_fn=None, dep=None):
    T = a.shape[0]
    nt = T // a_block[0]
    lead = len(o_block) == 3

    def body(a_ref, b_ref, o_ref, ob_ref):
        t = pl.program_id(1)
        av = a_ref[...]
        if a_fn is not None:
            av = a_fn(av)
        p = lax.dot_general(av.astype(BF16), b_ref[...].astype(BF16), TN_DIMS,
                            preferred_element_type=F32)
        out = o_ref.at[0] if lead else o_ref

        @pl.when(t == 0)
        def _():
            out[...] = p

        @pl.when(t > 0)
        def _():
            out[...] += p

        @pl.when(t == nt - 1)
        def _():
            ob_ref[...] = o_ref[...].astype(BF16)

    kw = dict(name=name, grid=(n_j, nt),
              in_specs=[pl.BlockSpec(a_block, a_map), pl.BlockSpec(b_block, b_map)],
              out_specs=[pl.BlockSpec(o_block, o_map), pl.BlockSpec(o_block, o_map)],
              out_shape=[jax.ShapeDtypeStruct(o_shape, F32), jax.ShapeDtypeStruct(o_shape, BF16)],
              compiler_params=_params(("parallel", "arbitrary")))
    if dep is None:
        return pl.pallas_call(body, **kw)(a, b)
    return _call_after(dep, body, (a, b), **kw)


def _mix_bwd(d1b, proj, a, bp, w_a, w_out, w_pool, ps, dep):
    T, D = a.shape
    tm = min(256, T)
    pg = D // POOL_GROUPS

    def body(d1b_ref, ga_ref, gb_ref, a_ref, bp_ref, wa_ref, wo_ref, wp_ref, ps_ref,
             da_ref, dbp_ref, dain_ref, dpl_ref, dg_ref, vec_ref):
        @pl.when(pl.program_id(0) == 0)
        def _():
            vec_ref[...] = jnp.zeros_like(vec_ref)

        dm = lax.dot_general(d1b_ref[...], wo_ref[...], NT_DIMS, preferred_element_type=F32)
        sa, sg = _sigmoid(ga_ref[0]), _sigmoid(gb_ref[0])
        bp_v, ps_v = bp_ref[...], ps_ref[...]
        da = (dm * sa).astype(BF16)
        db = dm * sg
        dg_ref[0] = (dm * a_ref[...] * sa * (1.0 - sa)).astype(BF16)
        dg_ref[1] = (dm * (bp_v * ps_v) * sg * (1.0 - sg)).astype(BF16)
        vec_ref[2:3, :] += jnp.sum(db * bp_v, axis=0, keepdims=True)
        dbp = (db * ps_v).astype(BF16)
        da_ref[...] = da
        dbp_ref[...] = dbp
        dain_ref[...] = lax.dot_general(da, wa_ref[...], NT_DIMS, preferred_element_type=F32)
        for g in range(POOL_GROUPS):
            cols = slice(g * pg, (g + 1) * pg)
            dpl_ref[:, cols] = lax.dot_general(dbp[:, cols], wp_ref[g], NT_DIMS,
                                               preferred_element_type=F32)

    row = lambda i: (i, 0)
    full = lambda i: (0, 0)
    return _call_after(
        dep, body, (d1b, proj, proj, a, bp, w_a, w_out, w_pool, ps), name="mix_bwd", grid=(T // tm,),
        in_specs=[pl.BlockSpec((tm, D), row),
                  pl.BlockSpec((1, tm, D), lambda i: (5, i, 0)),
                  pl.BlockSpec((1, tm, D), lambda i: (6, i, 0)),
                  pl.BlockSpec((tm, D), row), pl.BlockSpec((tm, D), row),
                  pl.BlockSpec((D, D), full), pl.BlockSpec((D, D), full),
                  pl.BlockSpec((POOL_GROUPS, pg, pg), lambda i: (0, 0, 0)),
                  pl.BlockSpec((1, D), full)],
        out_specs=[pl.BlockSpec((tm, D), row), pl.BlockSpec((tm, D), row),
                   pl.BlockSpec((tm, D), row), pl.BlockSpec((tm, D), row),
                   pl.BlockSpec((2, tm, D), lambda i: (0, i, 0)),
                   pl.BlockSpec((8, D), full)],
        out_shape=[jax.ShapeDtypeStruct((T, D), BF16), jax.ShapeDtypeStruct((T, D), BF16),
                   jax.ShapeDtypeStruct((T, D), F32), jax.ShapeDtypeStruct((T, D), F32),
                   jax.ShapeDtypeStruct((2, T, D), BF16), jax.ShapeDtypeStruct((8, D), F32)],
        compiler_params=_params(("arbitrary",)))


def _pool_bwd(dpooled3, dep):
    Bl, S, D = dpooled3.shape
    pg = D // POOL_GROUPS

    def body(dp_ref, dv_ref):
        g = pl.program_id(1)
        dp = dp_ref[0]
        pos, cnt = _window_count(dp.shape, g)
        cur, sums = dp / cnt, []
        for sh in (1, 2, 4, 8):
            cur = cur + jnp.where(pos < S - sh, pltpu.roll(cur, S - sh, 0), 0.0)
            sums.append(cur)
        dv_ref[0] = (_select_window(g, sums) - dp).astype(BF16)

    spec = pl.BlockSpec((1, S, pg), lambda b, g: (b, 0, g))
    return _call_after(
        dep, body, (dpooled3,), name="pool_bwd", grid=(Bl, POOL_GROUPS), in_specs=[spec], out_specs=spec,
        out_shape=jax.ShapeDtypeStruct((Bl, S, D), BF16),
        compiler_params=_params(("parallel", "parallel")))


def _hgrn_bwd(proj5, lb_logits, gn, dain3, st0_all):
    _, Bl, S, D = proj5.shape
    H = D // HEAD
    sb = min(512, S)
    nsb = S // sb
    nc = sb // CHUNK

    def body(p_ref, lbl_ref, gn_ref, dain_ref, st0_ref, d_ref, vec_ref,
             dcarry, o_scr, kv_scr, st_scr, dst_scr, dec_scr, dvi_scr, dke_scr, dqi_scr):
        b, s = pl.program_id(1), pl.program_id(2)

        @pl.when(s == 0)
        def _():
            dcarry[...] = jnp.zeros_like(dcarry)

        @pl.when((b == 0) & (s == 0))
        def _():
            vec_ref[...] = jnp.zeros_like(vec_ref)

        q, f_pre, v, og = p_ref[0, 0], p_ref[1, 0], p_ref[2, 0], p_ref[3, 0]
        c = _hgrn_gates(q, f_pre, lbl_ref[...])
        _hgrn_forward_block(c, v, st0_ref[0, 0, 0], sb, o_scr, kv_scr, st_scr, dec_scr)
        qd_b, ki_b, ke_b, v_b = (c["qd"].astype(BF16), c["ki"].astype(BF16),
                                 c["ke"].astype(BF16), v.astype(BF16))

        o = o_scr[...]
        rinv = lax.rsqrt(jnp.mean(o * o, axis=-1, keepdims=True) + RMS_EPS)
        on = o * rinv
        so = _sigmoid(og)
        gn_v = gn_ref[...]
        dain = dain_ref[0]
        vec_ref[1:2, :] += jnp.sum(dain * on * so, axis=0, keepdims=True)
        d_og = dain * on * gn_v * so * (1.0 - so)
        d_on = dain * gn_v * so
        do = rinv * (d_on - on * jnp.mean(d_on * on, axis=-1, keepdims=True))
        do_b = do.astype(BF16)

        mask = _intra_mask()
        dv_parts, dqd_parts, dki_parts = [], [], []
        for g in range(sb // GROUP):
            sl = slice(g * GROUP, (g + 1) * GROUP)
            sc = lax.dot_general(qd_b[sl], ki_b[sl], NT_DIMS, preferred_element_type=F32)
            a = jnp.where(mask, sc, 0.0).astype(BF16)
            da = lax.dot_general(do_b[sl], v_b[sl], NT_DIMS, preferred_element_type=F32)
            da = jnp.where(mask, da, 0.0).astype(BF16)
            dv_parts.append(lax.dot_general(a, do_b[sl], TN_DIMS, preferred_element_type=F32))
            dqd_parts.append(jnp.dot(da, ki_b[sl], preferred_element_type=F32))
            dki_parts.append(lax.dot_general(da, qd_b[sl], TN_DIMS, preferred_element_type=F32))
        dv_intra = jnp.concatenate(dv_parts, axis=0)
        dqd_intra = jnp.concatenate(dqd_parts, axis=0)
        dki = jnp.concatenate(dki_parts, axis=0)

        _chunk_outer(do, qd_b, kv_scr, sb)

        def rrec(i, dst):
            n = nc - 1 - i
            dst_scr[n] = dst
            d = dec_scr[pl.ds(pl.multiple_of(n * CHUNK, CHUNK), 1), :]
            return dst * d + kv_scr[n]

        dcarry[...] = lax.fori_loop(0, nc, rrec, dcarry[...])
        for n in range(nc):
            rows = slice(n * CHUNK, (n + 1) * CHUNK)
            dst_b = dst_scr[n].astype(BF16)
            dvi_scr[rows, :] = lax.dot_general(ke_b[rows], dst_b, NT_DIMS, preferred_element_type=F32)
            dke_scr[rows, :] = jnp.dot(v_b[rows], dst_b, preferred_element_type=F32)
            dqi_scr[rows, :] = jnp.dot(do_b[rows], st_scr[n].astype(BF16), preferred_element_type=F32)
        ddec = jnp.sum(dst_scr[...] * st_scr[...], axis=1)
        dgl = jnp.broadcast_to(ddec[:, None, :], (nc, CHUNK, HEAD)).reshape(sb, HEAD) * c["dec"]

        dqd = dqd_intra + dqi_scr[...]
        dke = dke_scr[...]
        t_ke = dke * c["ke"]
        dG = dqd * c["qd"] - dki * c["ki"] - t_ke
        dgl = dgl + _chunk_cumsum(t_ke) + _chunk_cumsum(t_ke, reverse=True) - t_ke
        dlogf = _chunk_cumsum(dG, reverse=True) + dgl
        dk = dki * c["e_ng"] + dke * c["e_ge"]
        df = dlogf / c["f"] - dk
        sg, sq, lb = c["sg"], c["sq"], c["lb"]
        vec_ref[0:1, :] += jnp.sum(df * (1.0 - sg), axis=0, keepdims=True)
        d_ref[0, 0] = (dqd * c["e_g"] * Q_SCALE * (sq + q * sq * (1.0 - sq))).astype(BF16)
        d_ref[1, 0] = (df * (1.0 - lb) * sg * (1.0 - sg)).astype(BF16)
        d_ref[2, 0] = (dv_intra + dvi_scr[...]).astype(BF16)
        d_ref[3, 0] = d_og.astype(BF16)

    rev = lambda s: nsb - 1 - s
    big = pltpu.VMEM((nc, HEAD, HEAD), F32)
    rows_f32 = pltpu.VMEM((sb, HEAD), F32)
    return pl.pallas_call(
        body, name="hgrn_bwd", grid=(H, Bl, nsb),
        in_specs=[pl.BlockSpec((4, 1, sb, HEAD), lambda h, b, s: (0, b, rev(s), h)),
                  pl.BlockSpec((2, HEAD), lambda h, b, s: (0, h)),
                  pl.BlockSpec((1, HEAD), lambda h, b, s: (0, h)),
                  pl.BlockSpec((1, sb, HEAD), lambda h, b, s: (b, rev(s), h)),
                  pl.BlockSpec((1, 1, 1, HEAD, HEAD), lambda h, b, s: (b, h, rev(s), 0, 0))],
        out_specs=[pl.BlockSpec((4, 1, sb, HEAD), lambda h, b, s: (0, b, rev(s), h)),
                   pl.BlockSpec((8, HEAD), lambda h, b, s: (0, h))],
        out_shape=[jax.ShapeDtypeStruct((4, Bl, S, D), BF16), jax.ShapeDtypeStruct((8, D), F32)],
        scratch_shapes=[pltpu.VMEM((HEAD, HEAD), F32), rows_f32, big, big, big, rows_f32,
                        rows_f32, rows_f32, rows_f32],
        compiler_params=_params(("parallel", "arbitrary", "arbitrary")),
    )(proj5, lb_logits, gn, dain3, st0_all)


def _section_sources(j, refs, fn):
    dh_ref, dp_ref, dg_ref = refs

    @pl.when(j < 4)
    def _():
        fn(dh_ref[0])

    @pl.when(j == 4)
    def _():
        fn(dp_ref[...])

    @pl.when(j > 4)
    def _():
        fn(dg_ref[0])


def _section_specs(tm, D, order):
    pick = (lambda a, b: (b, a)) if order == "ji" else (lambda a, b: (a, b))

    def at(fn):
        return lambda a, b: fn(*pick(a, b))

    return [pl.BlockSpec((1, tm, D), at(lambda i, j: (jnp.minimum(j, 3), i, 0))),
            pl.BlockSpec((tm, D), at(lambda i, j: (i, 0))),
            pl.BlockSpec((1, tm, D), at(lambda i, j: (jnp.clip(j - 5, 0, 1), i, 0)))]


def _dx(d1, dh4, dpv, dg2, w_in, dep):
    T, D = d1.shape
    tm = min(512, T)

    def body(d1_ref, dh_ref, dp_ref, dg_ref, w_ref, o_ref):
        j = pl.program_id(1)

        @pl.when(j == 0)
        def _():
            o_ref[...] = ALPHA * d1_ref[...]

        def add(blk):
            o_ref[...] += lax.dot_general(blk, w_ref[...], NT_DIMS, preferred_element_type=F32)

        _section_sources(j, (dh_ref, dp_ref, dg_ref), add)

    return _call_after(
        dep, body, (d1, dh4, dpv, dg2, w_in), name="dx", grid=(T // tm, N_SEC),
        in_specs=[pl.BlockSpec((tm, D), lambda i, j: (i, 0))] + _section_specs(tm, D, "ij")
        + [pl.BlockSpec((D, D), lambda i, j: (0, j))],
        out_specs=pl.BlockSpec((tm, D), lambda i, j: (i, 0)),
        out_shape=jax.ShapeDtypeStruct((T, D), F32),
        compiler_params=_params(("parallel", "arbitrary")))


def _dw_in(x2, dh4, dpv, dg2):
    T, D = x2.shape
    tk = min(512, T)

    nt = T // tk

    def body(x_ref, dh_ref, dp_ref, dg_ref, o_ref, ob_ref):
        j, t = pl.program_id(0), pl.program_id(1)
        xb = x_ref[...].astype(BF16)

        def acc(blk):
            p = lax.dot_general(xb, blk, TN_DIMS, preferred_element_type=F32)

            @pl.when(t == 0)
            def _():
                o_ref[...] = p

            @pl.when(t > 0)
            def _():
                o_ref[...] += p

        _section_sources(j, (dh_ref, dp_ref, dg_ref), acc)

        @pl.when(t == nt - 1)
        def _():
            ob_ref[...] = o_ref[...].astype(BF16)

    return pl.pallas_call(
        body, name="dw_in", grid=(N_SEC, nt),
        in_specs=[pl.BlockSpec((tk, D), lambda j, t: (t, 0))] + _section_specs(tk, D, "ji"),
        out_specs=[pl.BlockSpec((D, D), lambda j, t: (0, j)), pl.BlockSpec((D, D), lambda j, t: (0, j))],
        out_shape=[jax.ShapeDtypeStruct((D, N_SEC * D), F32), jax.ShapeDtypeStruct((D, N_SEC * D), BF16)],
        compiler_params=_params(("parallel", "arbitrary")),
    )(x2, dh4, dpv, dg2)


def _adam_shard(name, me_arr, grad, land, layout, w, m, v):
    shape = layout.shape
    n_split = 4
    blk = (shape[0] // n_split,) + shape[1:]
    zeros = (0,) * (len(shape) - 1)

    def body(me_ref, g_ref, r_ref, w_ref, m_ref, v_ref, g_out, d_out, m_out, v_out):
        g = g_ref[...]
        for k in range(N_DEV - 1):
            g = g + r_ref[k].astype(F32)
        d, m2, v2 = _adamw(w_ref[...], g, m_ref[...], v_ref[...])
        g_out[...] = g
        d_out[...] = d
        m_out[...] = m2
        v_out[...] = v2

    def own(i, me_ref):
        bi = layout.block_index(me_ref[0])
        return (bi[0] * n_split + i,) + tuple(bi[1:]) if layout.kind == "row" else (i,) + tuple(bi[1:])

    plain = pl.BlockSpec(blk, lambda i, me_ref: (i,) + zeros)
    grid_spec = pltpu.PrefetchScalarGridSpec(
        num_scalar_prefetch=1, grid=(n_split,),
        in_specs=[pl.BlockSpec(blk, own),
                  pl.BlockSpec((N_DEV - 1,) + blk, lambda i, me_ref: (0, i) + zeros),
                  plain, plain, plain],
        out_specs=[plain] * 4)
    return pl.pallas_call(
        body, name=name, grid_spec=grid_spec,
        out_shape=[jax.ShapeDtypeStruct(shape, F32)] * 4,
        compiler_params=_params(("parallel",)),
    )(me_arr, grad, land, w, m, v)


def _vec_allreduce_adam(vec, small_w, small_m, small_v):
    n = len(small_w)
    D = vec.shape[1]

    def body(*refs):
        vec_ref = refs[0]
        ws, ms, vs = refs[1:1 + n], refs[1 + n:1 + 2 * n], refs[1 + 2 * n:1 + 3 * n]
        outs = refs[1 + 3 * n:2 + 7 * n]
        gat, send_sems, recv_sems = refs[2 + 7 * n:]
        loss_ref, g_out, d_out = outs[0], outs[1:1 + n], outs[1 + n:1 + 2 * n]
        m_out, v_out = outs[1 + 2 * n:1 + 3 * n], outs[1 + 3 * n:1 + 4 * n]
        x, y, c = _me()
        me = 4 * x + 2 * y + c
        gat[me] = vec_ref[...]
        copies = []
        for k in range(1, N_DEV):
            fx, fy, fc = (k >> 2) & 1, (k >> 1) & 1, k & 1
            to = (1 - x if fx else x, 1 - y if fy else y, 1 - c if fc else c)
            cp = pltpu.make_async_remote_copy(
                src_ref=vec_ref, dst_ref=gat.at[me], send_sem=send_sems.at[k - 1],
                recv_sem=recv_sems.at[k - 1], device_id=to, device_id_type=MESH)
            cp.start()
            copies.append(cp)
        for cp in copies:
            cp.wait()
        tot = gat[0]
        for d in range(1, N_DEV):
            tot = tot + gat[d]
        loss_ref[...] = jnp.broadcast_to(jnp.sum(tot[7:8, :], axis=1, keepdims=True), loss_ref.shape)
        lbl = ws[0][...]
        mx = jnp.maximum(lbl[0:1, :], lbl[1:2, :])
        e0, e1 = jnp.exp(lbl[0:1, :] - mx), jnp.exp(lbl[1:2, :] - mx)
        p0 = e0 / (e0 + e1)
        dl0 = tot[0:1, :] * p0 * (1.0 - p0)
        grads = [jnp.concatenate([dl0, -dl0], axis=0)] + [tot[r:r + 1, :] for r in range(1, n)]
        for i in range(n):
            d, m2, v2 = _adamw(ws[i][...], grads[i], ms[i][...], vs[i][...])
            g_out[i][...] = grads[i]
            d_out[i][...] = d
            m_out[i][...] = m2
            v_out[i][...] = v2

    vm = pl.BlockSpec(memory_space=pltpu.VMEM)
    shapes = [jax.ShapeDtypeStruct(w.shape, F32) for w in small_w]
    return pl.pallas_call(
        body, name="vec_allreduce_adam",
        out_shape=[jax.ShapeDtypeStruct((1, 128), F32)] + shapes * 4,
        in_specs=[vm] * (1 + 3 * n), out_specs=[vm] * (1 + 4 * n),
        scratch_shapes=[pltpu.VMEM((N_DEV, 8, D), F32), pltpu.SemaphoreType.DMA((N_DEV - 1,)),
                        pltpu.SemaphoreType.DMA((N_DEV - 1,))],
    )(vec, *small_w, *small_m, *small_v)


def kernel(x, w_in, lb_logits, hgrn_norm_g, w_a, w_pool, pool_scale, w_out, ln1_g, ln1_b, w_up, w_down, ln2_g, ln2_b, loss_target, m_w_in, m_lb_logits, m_hgrn_norm_g, m_w_a, m_w_pool, m_pool_scale, m_w_out, m_ln1_g, m_ln1_b, m_w_up, m_w_down, m_ln2_g, m_ln2_b, v_w_in, v_lb_logits, v_hgrn_norm_g, v_w_a, v_w_pool, v_pool_scale, v_w_out, v_ln1_g, v_ln1_b, v_w_up, v_w_down, v_ln2_g, v_ln2_b):
    Bl, S, D = x.shape
    T = Bl * S
    pg = D // POOL_GROUPS
    x2 = x.reshape(T, D)
    tgt = loss_target.reshape(T, D)
    me = 4 * lax.axis_index("x") + 2 * lax.axis_index("y") + lax.axis_index("c")
    me_arr = jnp.reshape(me, (1,)).astype(jnp.int32)

    names = ["w_in", "w_a", "w_pool", "w_out", "w_up", "w_down"]
    big_w = dict(zip(names, [w_in[0], w_a[0], w_pool[0], w_out[0], w_up[0], w_down[0]]))
    big_m = dict(zip(names, [m_w_in[0], m_w_a[0], m_w_pool[0], m_w_out[0], m_w_up[0], m_w_down[0]]))
    big_v = dict(zip(names, [v_w_in[0], v_w_a[0], v_w_pool[0], v_w_out[0], v_w_up[0], v_w_down[0]]))
    kinds = dict(w_in="col", w_a="row", w_pool="pool", w_out="row", w_up="col", w_down="row")
    lay = {nm: _Sharded(kinds[nm], big_w[nm].shape) for nm in names}
    wb = {nm: big_w[nm].astype(BF16) for nm in names}

    (w_in_f,) = _all_gather("ag_w_in", [wb["w_in"]], [lay["w_in"]])
    rest = names[1:]
    own_placed = [lax.dynamic_update_slice(lax.empty(lay[nm].full_shape, BF16), wb[nm], lay[nm].start_index(me))
                  for nm in rest]
    ag_rest = _exchange_start("ag_rest", [wb[nm] for nm in rest], own_placed,
                              src_at=lambda w, ref, peer: ref,
                              dst_at=lambda w, ref, mine, k: lay[rest[w]].at(ref, mine), after=w_in_f)

    proj = _proj(x2, w_in_f, ag_rest["token"])
    proj5 = proj.reshape(N_SEC, Bl, S, D)
    ain3, st0_all = _hgrn_fwd(proj5, lb_logits, hgrn_norm_g)
    w_a_f, w_pool_f, w_out_f, w_up_f, w_down_f = _exchange_wait(ag_rest, ain3)
    pooled3, bp3 = _pool_fwd(proj5, w_pool_f)
    ain, pooled, bp = ain3.reshape(T, D), pooled3.reshape(T, D), bp3.reshape(T, D)
    a, merged, xhat1, rs1, x1b = _mix_fwd(ain, proj, bp, x2, w_a_f, w_out_f, pool_scale, ln1_g, ln1_b)
    hp, dr2, dr2b, vec_mlp = _mlp_fwd(x1b, w_up_f, w_down_f, xhat1, tgt, ln1_g, ln1_b, ln2_g, ln2_b)

    def scatter_start(name, nms, grads_b, after):
        lands = [lax.empty((N_DEV - 1,) + lay[nm].shape, BF16) for nm in nms]
        return _exchange_start(name, grads_b, lands,
                               src_at=lambda w, ref, peer: lay[nms[w]].at(ref, peer),
                               dst_at=lambda w, ref, mine, k: ref.at[k - 1], after=after)

    dhp, dr1, dr1b, vec_ln1 = _mlp_bwd(dr2b, dr2, hp, w_up_f, w_down_f, xhat1, rs1, ln1_g)
    tk = min(512, T)
    FF = 4 * D
    gw, gwb = {}, {}
    gw["w_down"], gwb["w_down"] = _mm_tn(
        "dw_down", hp, dr2b, N_DEV, (tk, 512), lambda j, t: (t, j), (tk, D), lambda j, t: (t, 0),
        (FF, D), (512, D), lambda j, t: (j, 0), a_fn=lambda h: jnp.square(jnp.maximum(h, 0.0)))
    rs_down = scatter_start("rs_w_down", ["w_down"], [gwb["w_down"]], gw["w_down"])
    gw["w_up"], gwb["w_up"] = _mm_tn(
        "dw_up", x1b, dhp, N_DEV, (tk, D), lambda j, t: (t, 0), (tk, 512), lambda j, t: (t, j),
        (D, FF), (D, 512), lambda j, t: (0, j), dep=rs_down["token"])
    rs_up = scatter_start("rs_w_up", ["w_up"], [gwb["w_up"]], gw["w_up"])
    da_b, dbp_b, dain, dpooled, dg2, vec_mix = _mix_bwd(dr1b, proj, a, bp, w_a_f, w_out_f, w_pool_f, pool_scale,
                                                        rs_up["token"])
    gw["w_out"], gwb["w_out"] = _mm_tn(
        "dw_out", merged, dr1b, 2, (tk, D), lambda j, t: (t, 0), (tk, D // 2), lambda j, t: (t, j),
        (D, D), (D, D // 2), lambda j, t: (0, j))
    gw["w_a"], gwb["w_a"] = _mm_tn(
        "dw_a", ain, da_b, 2, (tk, D), lambda j, t: (t, 0), (tk, D // 2), lambda j, t: (t, j),
        (D, D), (D, D // 2), lambda j, t: (0, j))
    gw["w_pool"], gwb["w_pool"] = _mm_tn(
        "dw_pool", pooled, dbp_b, POOL_GROUPS, (tk, pg), lambda j, t: (t, j), (tk, pg), lambda j, t: (t, j),
        (POOL_GROUPS, pg, pg), (1, pg, pg), lambda j, t: (j, 0, 0))
    mid = ["w_out", "w_a", "w_pool"]
    rs_mid = scatter_start("rs_w_mid", mid, [gwb[nm] for nm in mid], gw["w_pool"])
    dpv = _pool_bwd(dpooled.reshape(Bl, S, D), rs_mid["token"]).reshape(T, D)
    dh4, vec_hgrn = _hgrn_bwd(proj5, lb_logits, hgrn_norm_g, dain.reshape(Bl, S, D), st0_all)
    dh4 = dh4.reshape(4, T, D)
    gw["w_in"], gwb["w_in"] = _dw_in(x2, dh4, dpv, dg2)
    rs_in = scatter_start("rs_w_in", ["w_in"], [gwb["w_in"]], gw["w_in"])
    grad_x2 = _dx(dr1, dh4, dpv, dg2, w_in_f, rs_in["token"])
    grad_x = grad_x2.reshape(Bl, S, D)

    vec = vec_mlp + vec_ln1 + vec_mix + vec_hgrn
    small_names = ["lb_logits", "hgrn_norm_g", "pool_scale", "ln1_g", "ln1_b", "ln2_g", "ln2_b"]
    small_w = [lb_logits, hgrn_norm_g, pool_scale, ln1_g, ln1_b, ln2_g, ln2_b]
    small_m = [m_lb_logits, m_hgrn_norm_g, m_pool_scale, m_ln1_g, m_ln1_b, m_ln2_g, m_ln2_b]
    small_v = [v_lb_logits, v_hgrn_norm_g, v_pool_scale, v_ln1_g, v_ln1_b, v_ln2_g, v_ln2_b]
    res = _vec_allreduce_adam(vec, small_w, small_m, small_v)
    loss = res[0][0, 0]
    n = len(small_w)
    small = {nm: (res[1 + i], res[1 + n + i], res[1 + 2 * n + i], res[1 + 3 * n + i])
             for i, nm in enumerate(small_names)}

    landed = {}
    for pend, nms in ((rs_down, ["w_down"]), (rs_up, ["w_up"]), (rs_mid, mid), (rs_in, ["w_in"])):
        landed.update(zip(nms, _exchange_wait(pend, grad_x2)))
    big = {}
    for nm in names:
        outs = _adam_shard("adam_" + nm, me_arr, gw[nm], landed[nm], lay[nm], big_w[nm], big_m[nm], big_v[nm])
        big[nm] = tuple(t[None] for t in outs)

    order = ["w_in", "lb_logits", "hgrn_norm_g", "w_a", "w_pool", "pool_scale", "w_out", "ln1_g", "ln1_b",
             "w_up", "w_down", "ln2_g", "ln2_b"]
    allp = {**big, **small}
    out = [loss, grad_x]
    for part in range(4):
        out += [allp[nm][part] for nm in order]
    return tuple(out)
```

```python
import jax
import jax.numpy as jnp
from jax import lax
from jax.experimental import pallas as pl
from jax.experimental.pallas import tpu as pltpu

F32 = jnp.float32
BF16 = jnp.bfloat16
MESH = pl.DeviceIdType.MESH

N_DEV = 8
N_CHIP = 4
HEAD = 128
CHUNK = 16
GROUP = 128
CH_PER_GROUP = GROUP // CHUNK
N_SEC = 7
POOL_GROUPS = 4
ALPHA = (2.0 * 1) ** 0.25
LN_EPS = 1e-5
RMS_EPS = 1e-6
Q_SCALE = HEAD ** -0.5
ADAM_LR = 0.001
ADAM_B1 = 0.9
ADAM_B2 = 0.999
ADAM_EPS = 1e-08
ADAM_WD = 0.01
ADAM_STEP = 10
VMEM_LIMIT = 56 << 20

NT_DIMS = (((1,), (1,)), ((), ()))
TN_DIMS = (((0,), (0,)), ((), ()))


def _params(sem=None):
    kw = dict(vmem_limit_bytes=VMEM_LIMIT)
    if sem is not None:
        kw["dimension_semantics"] = sem
    return pltpu.CompilerParams(**kw)


def _me():
    return lax.axis_index("x"), lax.axis_index("y"), lax.axis_index("c")


def _sigmoid(v):
    return jax.nn.sigmoid(v)


def _adamw(w, g, m, v):
    m = ADAM_B1 * m + (1.0 - ADAM_B1) * g
    v = ADAM_B2 * v + (1.0 - ADAM_B2) * jnp.square(g)
    m_hat = m / (1.0 - ADAM_B1 ** ADAM_STEP)
    v_hat = v / (1.0 - ADAM_B2 ** ADAM_STEP)
    delta = -ADAM_LR * (m_hat / (jnp.sqrt(v_hat) + ADAM_EPS) + ADAM_WD * w)
    return delta, m, v


class _Sharded:
    def __init__(self, kind, shard_shape):
        self.kind, self.shape = kind, tuple(shard_shape)

    @property
    def full_shape(self):
        r = self.shape
        if self.kind == "row":
            return (N_DEV * r[0],) + r[1:]
        return (r[0], N_DEV * r[1]) + r[2:]

    def at(self, ref, d):
        if self.kind == "col":
            n = self.shape[1]
            return ref.at[:, pl.ds(pl.multiple_of(d * n, 128), n)]
        if self.kind == "row":
            n = self.shape[0]
            return ref.at[pl.ds(pl.multiple_of(d * n, 16), n), :]
        n = self.shape[1]
        return ref.at[:, pl.ds(pl.multiple_of(d * n, 16), n), :]

    def block_index(self, d):
        return {"col": (0, d), "row": (d, 0), "pool": (0, d, 0)}[self.kind]

    def start_index(self, d):
        r = self.shape
        if self.kind == "row":
            return (d * r[0],) + (0,) * (len(r) - 1)
        return (0, d * r[1]) + (0,) * (len(r) - 2)


def _peer(k, x, y, c):
    return (1 - x if k & 4 else x, 1 - y if k & 2 else y, 1 - c if k & 1 else c)


def _all_gather(name, shards, layouts):
    nw = len(shards)

    def body(*refs):
        ins, outs = refs[:nw], refs[nw:2 * nw]
        send_sems, recv_sems, local_sems = refs[2 * nw:]
        x, y, c = _me()
        me = (x, y, c)
        sibling = (x, y, 1 - c)
        chips = [(1 - x, y), (x, 1 - y), (1 - x, 1 - y)]

        def copy(w, k, block, to, src=None):
            px, py, pc = block
            dst = layouts[w].at(outs[w], 4 * px + 2 * py + pc)
            return pltpu.make_async_remote_copy(
                src_ref=dst if src is None else src, dst_ref=dst,
                send_sem=send_sems.at[w, k], recv_sem=recv_sems.at[w, k],
                device_id=to, device_id_type=MESH)

        local = []
        for w in range(nw):
            mine = pltpu.make_async_copy(ins[w], layouts[w].at(outs[w], 4 * x + 2 * y + c), local_sems.at[w])
            mine.start()
            local.append(mine)
        first = []
        for w in range(nw):
            first.append(copy(w, 0, me, sibling, src=ins[w]))
            first += [copy(w, 1 + j, me, (*chip, c), src=ins[w]) for j, chip in enumerate(chips)]
        for cp in first:
            cp.start()
        passed = []
        for w in range(nw):
            for j, chip in enumerate(chips):
                copy(w, 1 + j, (*chip, c), me).wait_recv()
                fwd = copy(w, 4 + j, (*chip, c), sibling)
                fwd.start()
                passed.append(fwd)
        for w in range(nw):
            copy(w, 0, sibling, me).wait_recv()
            for j, chip in enumerate(chips):
                copy(w, 4 + j, (*chip, 1 - c), me).wait_recv()
        for cp in first + passed:
            cp.wait_send()
        for cp in local:
            cp.wait()

    any_spec = pl.BlockSpec(memory_space=pl.ANY)
    return pl.pallas_call(
        body, name=name,
        out_shape=[jax.ShapeDtypeStruct(l.full_shape, s.dtype) for s, l in zip(shards, layouts)],
        in_specs=[any_spec] * nw, out_specs=[any_spec] * nw,
        scratch_shapes=[pltpu.SemaphoreType.DMA((nw, 7)), pltpu.SemaphoreType.DMA((nw, 7)),
                        pltpu.SemaphoreType.DMA((nw,))],
    )(*shards)


HBM_SPEC = pl.BlockSpec(memory_space=pltpu.HBM)
SEM_SPEC = pl.BlockSpec(memory_space=pltpu.SEMAPHORE)
DATAFLOW = pltpu.SideEffectType.DATAFLOW_SIDE_EFFECTING


def _exchange_copies(srcs, lands, send_sems, recv_sems, src_at, dst_at):
    x, y, c = _me()
    me = 4 * x + 2 * y + c
    copies = []
    for w in range(len(srcs)):
        for k in range(1, N_DEV):
            px, py, pc = _peer(k, x, y, c)
            copies.append(pltpu.make_async_remote_copy(
                src_ref=src_at(w, srcs[w], 4 * px + 2 * py + pc), dst_ref=dst_at(w, lands[w], me, k),
                send_sem=send_sems.at[w * (N_DEV - 1) + k - 1], recv_sem=recv_sems.at[w * (N_DEV - 1) + k - 1],
                device_id=(px, py, pc), device_id_type=MESH))
    return copies


def _exchange_start(name, srcs, lands, src_at, dst_at, after):
    nw = len(srcs)

    def body(*refs):
        src_refs, land_refs = refs[:nw], refs[nw:2 * nw]
        send_sems, recv_sems = refs[2 * nw + 1], refs[2 * nw + 2]
        token = refs[-1]
        for cp in _exchange_copies(src_refs, land_refs, send_sems, recv_sems, src_at, dst_at):
            cp.start()
        token[...] = jnp.zeros_like(token)

    hbm = lambda a: pltpu.HBM(a.shape, a.dtype)
    outs = pl.pallas_call(
        body, name=name,
        out_shape=(pltpu.SemaphoreType.DMA((nw * (N_DEV - 1),)), pltpu.SemaphoreType.DMA((nw * (N_DEV - 1),)),
                   *[hbm(a) for a in srcs], *[hbm(a) for a in lands],
                   jax.ShapeDtypeStruct((8, 128), F32)),
        in_specs=[HBM_SPEC] * (2 * nw) + [pl.BlockSpec(memory_space=pl.ANY)],
        out_specs=(SEM_SPEC, SEM_SPEC, *[HBM_SPEC] * (2 * nw), pl.BlockSpec(memory_space=pltpu.VMEM)),
        input_output_aliases={i: 2 + i for i in range(2 * nw)},
        compiler_params=pltpu.CompilerParams(has_side_effects=DATAFLOW),
    )(*[pltpu.with_memory_space_constraint(a, pltpu.HBM) for a in list(srcs) + list(lands)], after)
    return dict(send=outs[0], recv=outs[1], srcs=outs[2:2 + nw], lands=outs[2 + nw:2 + 2 * nw],
                token=outs[-1], src_at=src_at, dst_at=dst_at, name=name)


def _exchange_wait(pending, after):
    nw = len(pending["srcs"])

    def body(*refs):
        src_refs, land_refs = refs[:nw], refs[nw:2 * nw]
        send_sems, recv_sems = refs[2 * nw], refs[2 * nw + 1]
        for cp in _exchange_copies(src_refs, land_refs, send_sems, recv_sems,
                                   pending["src_at"], pending["dst_at"]):
            cp.wait_send()
            cp.wait_recv()

    hbm = lambda a: pltpu.HBM(a.shape, a.dtype)
    outs = pl.pallas_call(
        body, name=pending["name"] + "_wait",
        out_shape=(*[hbm(a) for a in pending["srcs"]], *[hbm(a) for a in pending["lands"]]),
        in_specs=[HBM_SPEC] * (2 * nw) + [SEM_SPEC, SEM_SPEC, pl.BlockSpec(memory_space=pl.ANY)],
        out_specs=tuple([HBM_SPEC] * (2 * nw)),
        input_output_aliases={i: i for i in range(2 * nw)},
        compiler_params=pltpu.CompilerParams(has_side_effects=DATAFLOW),
    )(*pending["srcs"], *pending["lands"], pending["send"], pending["recv"], after)
    return outs[nw:]


def _call_after(dep, body, args, *, in_specs, **kw):
    n_in = len(args)

    def wrapped(*refs):
        body(*refs[:n_in], *refs[n_in + 1:])

    dep_spec = pl.BlockSpec(dep.shape, lambda *_: (0,) * dep.ndim)
    return pl.pallas_call(wrapped, in_specs=list(in_specs) + [dep_spec], **kw)(*args, dep)


def _resident(shape):
    return pl.BlockSpec(shape, lambda *_: (0,) * len(shape), pipeline_mode=pl.Buffered(1))


def _proj(x2, w_in, dep):
    T, D = x2.shape
    tm = min(256, T)

    def body(x_ref, w_ref, o_ref, xb_ref):
        xb = x_ref[...].astype(BF16)
        xb_ref[...] = xb
        for j in range(N_SEC):
            o_ref[j] = jnp.dot(xb, w_ref[:, j * D:(j + 1) * D], preferred_element_type=F32)

    return _call_after(
        dep, body, (x2, w_in), name="proj", grid=(T // tm,),
        in_specs=[pl.BlockSpec((tm, D), lambda i: (i, 0)), _resident((D, N_SEC * D))],
        out_specs=[pl.BlockSpec((N_SEC, tm, D), lambda i: (0, i, 0)), pl.BlockSpec((tm, D), lambda i: (i, 0))],
        out_shape=[jax.ShapeDtypeStruct((N_SEC, T, D), F32), jax.ShapeDtypeStruct((T, D), BF16)],
        compiler_params=_params(("parallel",)))


def _chunk_cumsum(v, reverse=False):
    rows = v.shape[0]
    pos = lax.broadcasted_iota(jnp.int32, v.shape, 0) % CHUNK
    for sh in (1, 2, 4, 8):
        if reverse:
            v = v + jnp.where(pos < CHUNK - sh, pltpu.roll(v, rows - sh, 0), 0.0)
        else:
            v = v + jnp.where(pos >= sh, pltpu.roll(v, sh, 0), 0.0)
    return v


def _hgrn_gates(q, f_pre, lb_logits):
    l0, l1 = lb_logits[0:1, :], lb_logits[1:2, :]
    mx = jnp.maximum(l0, l1)
    e0, e1 = jnp.exp(l0 - mx), jnp.exp(l1 - mx)
    lb = e0 / (e0 + e1)
    sq = _sigmoid(q)
    qf = q * sq * Q_SCALE
    sg = _sigmoid(f_pre)
    f = lb + (1.0 - lb) * sg
    k = 1.0 - f
    log_f = jnp.log(f)
    G = _chunk_cumsum(log_f)
    g_to_end = _chunk_cumsum(log_f, reverse=True) - log_f
    e_g = jnp.exp(G)
    e_ng = jnp.exp(-G)
    e_ge = jnp.exp(g_to_end)
    return dict(lb=lb, sq=sq, qf=qf, sg=sg, f=f, k=k, G=G, e_g=e_g, e_ng=e_ng, e_ge=e_ge,
                qd=qf * e_g, ki=k * e_ng, ke=k * e_ge, dec=jnp.exp(G + g_to_end))


def _intra_mask():
    r = lax.broadcasted_iota(jnp.int32, (GROUP, GROUP), 0)
    c = lax.broadcasted_iota(jnp.int32, (GROUP, GROUP), 1)
    return (r // CHUNK == c // CHUNK) & (c <= r)


def _chunk_outer(lhs_rows, rhs_b, out_scr, sb):
    lane = lax.broadcasted_iota(jnp.int32, (GROUP, GROUP), 1) // CHUNK
    for g in range(sb // GROUP):
        sl = slice(g * GROUP, (g + 1) * GROUP)
        lhs_t = lhs_rows[sl].T
        for cc in range(CH_PER_GROUP):
            masked = jnp.where(lane == cc, lhs_t, 0.0).astype(BF16)
            out_scr[g * CH_PER_GROUP + cc] = jnp.dot(masked, rhs_b[sl], preferred_element_type=F32)


def _hgrn_forward_block(c, v, st0, sb, o_scr, kv_scr, st_scr, dec_scr):
    nc = sb // CHUNK
    qd_b, ki_b, ke_b, v_b = (c["qd"].astype(BF16), c["ki"].astype(BF16), c["ke"].astype(BF16),
                             v.astype(BF16))
    mask = _intra_mask()
    for g in range(sb // GROUP):
        sl = slice(g * GROUP, (g + 1) * GROUP)
        sc = lax.dot_general(qd_b[sl], ki_b[sl], NT_DIMS, preferred_element_type=F32)
        a = jnp.where(mask, sc, 0.0).astype(BF16)
        o_scr[sl, :] = jnp.dot(a, v_b[sl], preferred_element_type=F32)
    _chunk_outer(v, ke_b, kv_scr, sb)
    dec_scr[...] = c["dec"]

    def rec(n, st):
        st_scr[n] = st
        d = dec_scr[pl.ds(pl.multiple_of(n * CHUNK, CHUNK), 1), :]
        return st * d + kv_scr[n]

    st_end = lax.fori_loop(0, nc, rec, st0)

    for n in range(nc):
        rows = slice(n * CHUNK, (n + 1) * CHUNK)
        o_scr[rows, :] += lax.dot_general(qd_b[rows], st_scr[n].astype(BF16), NT_DIMS,
                                          preferred_element_type=F32)
    return st_end


def _hgrn_fwd(proj5, lb_logits, gn):
    _, Bl, S, D = proj5.shape
    H = D // HEAD
    sb = min(512, S)
    nsb = S // sb
    nc = sb // CHUNK

    def body(p_ref, lbl_ref, gn_ref, ain_ref, st0_ref, carry, o_scr, kv_scr, st_scr, dec_scr):
        s = pl.program_id(2)

        @pl.when(s == 0)
        def _():
            carry[...] = jnp.zeros_like(carry)

        st0 = carry[...]
        st0_ref[0, 0, 0] = st0
        c = _hgrn_gates(p_ref[0, 0], p_ref[1, 0], lbl_ref[...])
        carry[...] = _hgrn_forward_block(c, p_ref[2, 0], st0, sb, o_scr, kv_scr, st_scr, dec_scr)
        o = o_scr[...]
        rinv = lax.rsqrt(jnp.mean(o * o, axis=-1, keepdims=True) + RMS_EPS)
        ain_ref[0] = (o * rinv * gn_ref[...] * _sigmoid(p_ref[3, 0])).astype(BF16)

    return pl.pallas_call(
        body, name="hgrn_fwd", grid=(H, Bl, nsb),
        in_specs=[pl.BlockSpec((4, 1, sb, HEAD), lambda h, b, s: (0, b, s, h)),
                  pl.BlockSpec((2, HEAD), lambda h, b, s: (0, h)),
                  pl.BlockSpec((1, HEAD), lambda h, b, s: (0, h))],
        out_specs=[pl.BlockSpec((1, sb, HEAD), lambda h, b, s: (b, s, h)),
                   pl.BlockSpec((1, 1, 1, HEAD, HEAD), lambda h, b, s: (b, h, s, 0, 0))],
        out_shape=[jax.ShapeDtypeStruct((Bl, S, D), BF16),
                   jax.ShapeDtypeStruct((Bl, H, nsb, HEAD, HEAD), F32)],
        scratch_shapes=[pltpu.VMEM((HEAD, HEAD), F32), pltpu.VMEM((sb, HEAD), F32),
                        pltpu.VMEM((nc, HEAD, HEAD), F32), pltpu.VMEM((nc, HEAD, HEAD), F32),
                        pltpu.VMEM((sb, HEAD), F32)],
        compiler_params=_params(("parallel", "parallel", "arbitrary")),
    )(proj5, lb_logits, gn)


def _window_count(shape, g):
    pos = lax.broadcasted_iota(jnp.int32, shape, 0)
    return pos, jnp.minimum(pos + 1, jnp.left_shift(2, g)).astype(F32)


def _select_window(g, sums):
    return jnp.where(g == 0, sums[0], jnp.where(g == 1, sums[1], jnp.where(g == 2, sums[2], sums[3])))


def _pool_fwd(proj5, w_pool):
    _, Bl, S, D = proj5.shape
    pg = D // POOL_GROUPS

    def body(v_ref, w_ref, pooled_ref, bp_ref):
        g = pl.program_id(1)
        v = v_ref[0, 0]
        pos, cnt = _window_count(v.shape, g)
        cur, sums = v, []
        for sh in (1, 2, 4, 8):
            cur = cur + jnp.where(pos >= sh, pltpu.roll(cur, sh, 0), 0.0)
            sums.append(cur)
        pooled = (_select_window(g, sums) / cnt - v).astype(BF16)
        pooled_ref[0] = pooled
        bp_ref[0] = jnp.dot(pooled, w_ref[0], preferred_element_type=F32)

    return pl.pallas_call(
        body, name="pool_fwd", grid=(Bl, POOL_GROUPS),
        in_specs=[pl.BlockSpec((1, 1, S, pg), lambda b, g: (4, b, 0, g)),
                  pl.BlockSpec((1, pg, pg), lambda b, g: (g, 0, 0))],
        out_specs=[pl.BlockSpec((1, S, pg), lambda b, g: (b, 0, g)),
                   pl.BlockSpec((1, S, pg), lambda b, g: (b, 0, g))],
        out_shape=[jax.ShapeDtypeStruct((Bl, S, D), BF16), jax.ShapeDtypeStruct((Bl, S, D), F32)],
        compiler_params=_params(("parallel", "parallel")),
    )(proj5, w_pool)


def _layer_norm_fwd(r):
    mu = jnp.mean(r, axis=-1, keepdims=True)
    d = r - mu
    rs = lax.rsqrt(jnp.mean(d * d, axis=-1, keepdims=True) + LN_EPS)
    return d * rs, rs


def _layer_norm_bwd(dy_g, xhat, rs):
    return rs * (dy_g - jnp.mean(dy_g, axis=-1, keepdims=True)
                 - xhat * jnp.mean(dy_g * xhat, axis=-1, keepdims=True))


def _mix_fwd(ain, proj, bp, x2, w_a, w_out, ps, g1, b1):
    T, D = x2.shape
    tm = min(256, T)

    def body(ain_ref, ga_ref, gb_ref, bp_ref, x_ref, wa_ref, wo_ref, ps_ref, g1_ref, b1_ref,
             a_ref, mg_ref, xh_ref, rs_ref, x1b_ref):
        a = jnp.dot(ain_ref[...], wa_ref[...], preferred_element_type=F32)
        a_ref[...] = a
        merged = (_sigmoid(ga_ref[0]) * a + _sigmoid(gb_ref[0]) * (bp_ref[...] * ps_ref[...])).astype(BF16)
        mg_ref[...] = merged
        r1 = ALPHA * x_ref[...] + jnp.dot(merged, wo_ref[...], preferred_element_type=F32)
        xhat, rs = _layer_norm_fwd(r1)
        xh_ref[...] = xhat
        rs_ref[...] = rs
        x1b_ref[...] = (xhat * g1_ref[...] + b1_ref[...]).astype(BF16)

    row = lambda i: (i, 0)
    full = lambda i: (0, 0)
    return pl.pallas_call(
        body, name="mix_fwd", grid=(T // tm,),
        in_specs=[pl.BlockSpec((tm, D), row),
                  pl.BlockSpec((1, tm, D), lambda i: (5, i, 0)),
                  pl.BlockSpec((1, tm, D), lambda i: (6, i, 0)),
                  pl.BlockSpec((tm, D), row), pl.BlockSpec((tm, D), row),
                  pl.BlockSpec((D, D), full), pl.BlockSpec((D, D), full),
                  pl.BlockSpec((1, D), full), pl.BlockSpec((1, D), full), pl.BlockSpec((1, D), full)],
        out_specs=[pl.BlockSpec((tm, D), row), pl.BlockSpec((tm, D), row), pl.BlockSpec((tm, D), row),
                   pl.BlockSpec((tm, 1), row), pl.BlockSpec((tm, D), row)],
        out_shape=[jax.ShapeDtypeStruct((T, D), F32), jax.ShapeDtypeStruct((T, D), BF16),
                   jax.ShapeDtypeStruct((T, D), F32), jax.ShapeDtypeStruct((T, 1), F32),
                   jax.ShapeDtypeStruct((T, D), BF16)],
        compiler_params=_params(("parallel",)),
    )(ain, proj, proj, bp, x2, w_a, w_out, ps, g1, b1)


def _mlp_fwd(x1b, w_up, w_down, xhat1, tgt, g1, b1, g2, b2):
    T, D = xhat1.shape
    FF = w_up.shape[1]
    tm = min(256, T)

    def body(x_ref, wu_ref, wd_ref, xh_ref, t_ref, g1_ref, b1_ref, g2_ref, b2_ref,
             hp_ref, dr_ref, drb_ref, vec_ref):
        @pl.when(pl.program_id(0) == 0)
        def _():
            vec_ref[...] = jnp.zeros_like(vec_ref)

        xb = x_ref[...]
        x1 = xh_ref[...] * g1_ref[...] + b1_ref[...]
        r2 = ALPHA * x1
        for f in range(FF // D):
            cols = slice(f * D, (f + 1) * D)
            hp = jnp.dot(xb, wu_ref[:, cols], preferred_element_type=F32)
            hp_ref[:, cols] = hp
            r2 = r2 + jnp.dot(jnp.square(jnp.maximum(hp, 0.0)).astype(BF16), wd_ref[cols, :],
                              preferred_element_type=F32)
        xhat2, rs2 = _layer_norm_fwd(r2)
        err = xhat2 * g2_ref[...] + b2_ref[...] - t_ref[...]
        dy = err / D
        vec_ref[5:6, :] += jnp.sum(dy * xhat2, axis=0, keepdims=True)
        vec_ref[6:7, :] += jnp.sum(dy, axis=0, keepdims=True)
        vec_ref[7:8, :] += jnp.sum(0.5 * err * err / D, axis=0, keepdims=True)
        dr = _layer_norm_bwd(dy * g2_ref[...], xhat2, rs2)
        dr_ref[...] = dr
        drb_ref[...] = dr.astype(BF16)

    row = lambda i: (i, 0)
    full = lambda i: (0, 0)
    return pl.pallas_call(
        body, name="mlp_fwd", grid=(T // tm,),
        in_specs=[pl.BlockSpec((tm, D), row), _resident((D, FF)), _resident((FF, D)),
                  pl.BlockSpec((tm, D), row), pl.BlockSpec((tm, D), row),
                  pl.BlockSpec((1, D), full), pl.BlockSpec((1, D), full),
                  pl.BlockSpec((1, D), full), pl.BlockSpec((1, D), full)],
        out_specs=[pl.BlockSpec((tm, FF), row), pl.BlockSpec((tm, D), row), pl.BlockSpec((tm, D), row),
                   pl.BlockSpec((8, D), full)],
        out_shape=[jax.ShapeDtypeStruct((T, FF), F32), jax.ShapeDtypeStruct((T, D), F32),
                   jax.ShapeDtypeStruct((T, D), BF16), jax.ShapeDtypeStruct((8, D), F32)],
        compiler_params=_params(("arbitrary",)),
    )(x1b, w_up, w_down, xhat1, tgt, g1, b1, g2, b2)


def _mlp_bwd(drb, dr, hp, w_up, w_down, xhat1, rs1, g1):
    T, D = dr.shape
    FF = hp.shape[1]
    tm = min(256, T)

    def body(drb_ref, dr_ref, hp_ref, wu_ref, wd_ref, xh_ref, rs_ref, g1_ref,
             dhp_ref, d1_ref, d1b_ref, vec_ref):
        @pl.when(pl.program_id(0) == 0)
        def _():
            vec_ref[...] = jnp.zeros_like(vec_ref)

        drb = drb_ref[...]
        dx1 = ALPHA * dr_ref[...]
        for f in range(FF // D):
            cols = slice(f * D, (f + 1) * D)
            dh = lax.dot_general(drb, wd_ref[cols, :], NT_DIMS, preferred_element_type=F32)
            dhp = (dh * (2.0 * jnp.maximum(hp_ref[:, cols], 0.0))).astype(BF16)
            dhp_ref[:, cols] = dhp
            dx1 = dx1 + lax.dot_general(dhp, wu_ref[:, cols], NT_DIMS, preferred_element_type=F32)
        xhat = xh_ref[...]
        vec_ref[3:4, :] += jnp.sum(dx1 * xhat, axis=0, keepdims=True)
        vec_ref[4:5, :] += jnp.sum(dx1, axis=0, keepdims=True)
        d1 = _layer_norm_bwd(dx1 * g1_ref[...], xhat, rs_ref[...])
        d1_ref[...] = d1
        d1b_ref[...] = d1.astype(BF16)

    row = lambda i: (i, 0)
    full = lambda i: (0, 0)
    return pl.pallas_call(
        body, name="mlp_bwd", grid=(T // tm,),
        in_specs=[pl.BlockSpec((tm, D), row), pl.BlockSpec((tm, D), row), pl.BlockSpec((tm, FF), row),
                  _resident((D, FF)), _resident((FF, D)),
                  pl.BlockSpec((tm, D), row), pl.BlockSpec((tm, 1), row), pl.BlockSpec((1, D), full)],
        out_specs=[pl.BlockSpec((tm, FF), row), pl.BlockSpec((tm, D), row), pl.BlockSpec((tm, D), row),
                   pl.BlockSpec((8, D), full)],
        out_shape=[jax.ShapeDtypeStruct((T, FF), BF16), jax.ShapeDtypeStruct((T, D), F32),
                   jax.ShapeDtypeStruct((T, D), BF16), jax.ShapeDtypeStruct((8, D), F32)],
        compiler_params=_params(("arbitrary",)),
    )(drb, dr, hp, w_up, w_down, xhat1, rs1, g1)


def _mm_tn(name, a, b, n_j, a_block, a_map, b_block, b_map, o_shape, o_block, o_map, a_fn=None, dep=None):
    T = a.shape[0]
    nt = T // a_block[0]
    lead = len(o_block) == 3

    def body(a_ref, b_ref, o_ref, ob_ref):
        t = pl.program_id(1)
        av = a_ref[...]
        if a_fn is not None:
            av = a_fn(av)
        p = lax.dot_general(av.astype(BF16), b_ref[...].astype(BF16), TN_DIMS,
                            preferred_element_type=F32)
        out = o_ref.at[0] if lead else o_ref

        @pl.when(t == 0)
        def _():
            out[...] = p

        @pl.when(t > 0)
        def _():
            out[...] += p

        @pl.when(t == nt - 1)
        def _():
            ob_ref[...] = o_ref[...].astype(BF16)

    kw = dict(name=name, grid=(n_j, nt),
              in_specs=[pl.BlockSpec(a_block, a_map), pl.BlockSpec(b_block, b_map)],
              out_specs=[pl.BlockSpec(o_block, o_map), pl.BlockSpec(o_block, o_map)],
              out_shape=[jax.ShapeDtypeStruct(o_shape, F32), jax.ShapeDtypeStruct(o_shape, BF16)],
              compiler_params=_params(("parallel", "arbitrary")))
    if dep is None:
        return pl.pallas_call(body, **kw)(a, b)
    return _call_after(dep, body, (a, b), **kw)


def _mix_bwd(d1b, proj, a, bp, w_a, w_out, w_pool, ps, dep):
    T, D = a.shape
    tm = min(256, T)
    pg = D // POOL_GROUPS

    def body(d1b_ref, ga_ref, gb_ref, a_ref, bp_ref, wa_ref, wo_ref, wp_ref, ps_ref,
             da_ref, dbp_ref, dain_ref, dpl_ref, dg_ref, vec_ref):
        @pl.when(pl.program_id(0) == 0)
        def _():
            vec_ref[...] = jnp.zeros_like(vec_ref)

        dm = lax.dot_general(d1b_ref[...], wo_ref[...], NT_DIMS, preferred_element_type=F32)
        sa, sg = _sigmoid(ga_ref[0]), _sigmoid(gb_ref[0])
        bp_v, ps_v = bp_ref[...], ps_ref[...]
        da = (dm * sa).astype(BF16)
        db = dm * sg
        dg_ref[0] = (dm * a_ref[...] * sa * (1.0 - sa)).astype(BF16)
        dg_ref[1] = (dm * (bp_v * ps_v) * sg * (1.0 - sg)).astype(BF16)
        vec_ref[2:3, :] += jnp.sum(db * bp_v, axis=0, keepdims=True)
        dbp = (db * ps_v).astype(BF16)
        da_ref[...] = da
        dbp_ref[...] = dbp
        dain_ref[...] = lax.dot_general(da, wa_ref[...], NT_DIMS, preferred_element_type=F32)
        for g in range(POOL_GROUPS):
            cols = slice(g * pg, (g + 1) * pg)
            dpl_ref[:, cols] = lax.dot_general(dbp[:, cols], wp_ref[g], NT_DIMS,
                                               preferred_element_type=F32)

    row = lambda i: (i, 0)
    full = lambda i: (0, 0)
    return _call_after(
        dep, body, (d1b, proj, proj, a, bp, w_a, w_out, w_pool, ps), name="mix_bwd", grid=(T // tm,),
        in_specs=[pl.BlockSpec((tm, D), row),
                  pl.BlockSpec((1, tm, D), lambda i: (5, i, 0)),
                  pl.BlockSpec((1, tm, D), lambda i: (6, i, 0)),
                  pl.BlockSpec((tm, D), row), pl.BlockSpec((tm, D), row),
                  pl.BlockSpec((D, D), full), pl.BlockSpec((D, D), full),
                  pl.BlockSpec((POOL_GROUPS, pg, pg), lambda i: (0, 0, 0)),
                  pl.BlockSpec((1, D), full)],
        out_specs=[pl.BlockSpec((tm, D), row), pl.BlockSpec((tm, D), row),
                   pl.BlockSpec((tm, D), row), pl.BlockSpec((tm, D), row),
                   pl.BlockSpec((2, tm, D), lambda i: (0, i, 0)),
                   pl.BlockSpec((8, D), full)],
        out_shape=[jax.ShapeDtypeStruct((T, D), BF16), jax.ShapeDtypeStruct((T, D), BF16),
                   jax.ShapeDtypeStruct((T, D), F32), jax.ShapeDtypeStruct((T, D), F32),
                   jax.ShapeDtypeStruct((2, T, D), BF16), jax.ShapeDtypeStruct((8, D), F32)],
        compiler_params=_params(("arbitrary",)))


def _pool_bwd(dpooled3, dep):
    Bl, S, D = dpooled3.shape
    pg = D // POOL_GROUPS

    def body(dp_ref, dv_ref):
        g = pl.program_id(1)
        dp = dp_ref[0]
        pos, cnt = _window_count(dp.shape, g)
        cur, sums = dp / cnt, []
        for sh in (1, 2, 4, 8):
            cur = cur + jnp.where(pos < S - sh, pltpu.roll(cur, S - sh, 0), 0.0)
            sums.append(cur)
        dv_ref[0] = (_select_window(g, sums) - dp).astype(BF16)

    spec = pl.BlockSpec((1, S, pg), lambda b, g: (b, 0, g))
    return _call_after(
        dep, body, (dpooled3,), name="pool_bwd", grid=(Bl, POOL_GROUPS), in_specs=[spec], out_specs=spec,
        out_shape=jax.ShapeDtypeStruct((Bl, S, D), BF16),
        compiler_params=_params(("parallel", "parallel")))


def _hgrn_bwd(proj5, lb_logits, gn, dain3, st0_all):
    _, Bl, S, D = proj5.shape
    H = D // HEAD
    sb = min(512, S)
    nsb = S // sb
    nc = sb // CHUNK

    def body(p_ref, lbl_ref, gn_ref, dain_ref, st0_ref, d_ref, vec_ref,
             dcarry, o_scr, kv_scr, st_scr, dst_scr, dec_scr, dvi_scr, dke_scr, dqi_scr):
        b, s = pl.program_id(1), pl.program_id(2)

        @pl.when(s == 0)
        def _():
            dcarry[...] = jnp.zeros_like(dcarry)

        @pl.when((b == 0) & (s == 0))
        def _():
            vec_ref[...] = jnp.zeros_like(vec_ref)

        q, f_pre, v, og = p_ref[0, 0], p_ref[1, 0], p_ref[2, 0], p_ref[3, 0]
        c = _hgrn_gates(q, f_pre, lbl_ref[...])
        _hgrn_forward_block(c, v, st0_ref[0, 0, 0], sb, o_scr, kv_scr, st_scr, dec_scr)
        qd_b, ki_b, ke_b, v_b = (c["qd"].astype(BF16), c["ki"].astype(BF16),
                                 c["ke"].astype(BF16), v.astype(BF16))

        o = o_scr[...]
        rinv = lax.rsqrt(jnp.mean(o * o, axis=-1, keepdims=True) + RMS_EPS)
        on = o * rinv
        so = _sigmoid(og)
        gn_v = gn_ref[...]
        dain = dain_ref[0]
        vec_ref[1:2, :] += jnp.sum(dain * on * so, axis=0, keepdims=True)
        d_og = dain * on * gn_v * so * (1.0 - so)
        d_on = dain * gn_v * so
        do = rinv * (d_on - on * jnp.mean(d_on * on, axis=-1, keepdims=True))
        do_b = do.astype(BF16)

        mask = _intra_mask()
        dv_parts, dqd_parts, dki_parts = [], [], []
        for g in range(sb // GROUP):
            sl = slice(g * GROUP, (g + 1) * GROUP)
            sc = lax.dot_general(qd_b[sl], ki_b[sl], NT_DIMS, preferred_element_type=F32)
            a = jnp.where(mask, sc, 0.0).astype(BF16)
            da = lax.dot_general(do_b[sl], v_b[sl], NT_DIMS, preferred_element_type=F32)
            da = jnp.where(mask, da, 0.0).astype(BF16)
            dv_parts.append(lax.dot_general(a, do_b[sl], TN_DIMS, preferred_element_type=F32))
            dqd_parts.append(jnp.dot(da, ki_b[sl], preferred_element_type=F32))
            dki_parts.append(lax.dot_general(da, qd_b[sl], TN_DIMS, preferred_element_type=F32))
        dv_intra = jnp.concatenate(dv_parts, axis=0)
        dqd_intra = jnp.concatenate(dqd_parts, axis=0)
        dki = jnp.concatenate(dki_parts, axis=0)

        _chunk_outer(do, qd_b, kv_scr, sb)

        def rrec(i, dst):
            n = nc - 1 - i
            dst_scr[n] = dst
            d = dec_scr[pl.ds(pl.multiple_of(n * CHUNK, CHUNK), 1), :]
            return dst * d + kv_scr[n]

        dcarry[...] = lax.fori_loop(0, nc, rrec, dcarry[...])
        for n in range(nc):
            rows = slice(n * CHUNK, (n + 1) * CHUNK)
            dst_b = dst_scr[n].astype(BF16)
            dvi_scr[rows, :] = lax.dot_general(ke_b[rows], dst_b, NT_DIMS, preferred_element_type=F32)
            dke_scr[rows, :] = jnp.dot(v_b[rows], dst_b, preferred_element_type=F32)
            dqi_scr[rows, :] = jnp.dot(do_b[rows], st_scr[n].astype(BF16), preferred_element_type=F32)
        ddec = jnp.sum(dst_scr[...] * st_scr[...], axis=1)
        dgl = jnp.broadcast_to(ddec[:, None, :], (nc, CHUNK, HEAD)).reshape(sb, HEAD) * c["dec"]

        dqd = dqd_intra + dqi_scr[...]
        dke = dke_scr[...]
        t_ke = dke * c["ke"]
        dG = dqd * c["qd"] - dki * c["ki"] - t_ke
        dgl = dgl + _chunk_cumsum(t_ke) + _chunk_cumsum(t_ke, reverse=True) - t_ke
        dlogf = _chunk_cumsum(dG, reverse=True) + dgl
        dk = dki * c["e_ng"] + dke * c["e_ge"]
        df = dlogf / c["f"] - dk
        sg, sq, lb = c["sg"], c["sq"], c["lb"]
        vec_ref[0:1, :] += jnp.sum(df * (1.0 - sg), axis=0, keepdims=True)
        d_ref[0, 0] = (dqd * c["e_g"] * Q_SCALE * (sq + q * sq * (1.0 - sq))).astype(BF16)
        d_ref[1, 0] = (df * (1.0 - lb) * sg * (1.0 - sg)).astype(BF16)
        d_ref[2, 0] = (dv_intra + dvi_scr[...]).astype(BF16)
        d_ref[3, 0] = d_og.astype(BF16)

    rev = lambda s: nsb - 1 - s
    big = pltpu.VMEM((nc, HEAD, HEAD), F32)
    rows_f32 = pltpu.VMEM((sb, HEAD), F32)
    return pl.pallas_call(
        body, name="hgrn_bwd", grid=(H, Bl, nsb),
        in_specs=[pl.BlockSpec((4, 1, sb, HEAD), lambda h, b, s: (0, b, rev(s), h)),
                  pl.BlockSpec((2, HEAD), lambda h, b, s: (0, h)),
                  pl.BlockSpec((1, HEAD), lambda h, b, s: (0, h)),
                  pl.BlockSpec((1, sb, HEAD), lambda h, b, s: (b, rev(s), h)),
                  pl.BlockSpec((1, 1, 1, HEAD, HEAD), lambda h, b, s: (b, h, rev(s), 0, 0))],
        out_specs=[pl.BlockSpec((4, 1, sb, HEAD), lambda h, b, s: (0, b, rev(s), h)),
                   pl.BlockSpec((8, HEAD), lambda h, b, s: (0, h))],
        out_shape=[jax.ShapeDtypeStruct((4, Bl, S, D), BF16), jax.ShapeDtypeStruct((8, D), F32)],
        scratch_shapes=[pltpu.VMEM((HEAD, HEAD), F32), rows_f32, big, big, big, rows_f32,
                        rows_f32, rows_f32, rows_f32],
        compiler_params=_params(("parallel", "arbitrary", "arbitrary")),
    )(proj5, lb_logits, gn, dain3, st0_all)


def _section_sources(j, refs, fn):
    dh_ref, dp_ref, dg_ref = refs

    @pl.when(j < 4)
    def _():
        fn(dh_ref[0])

    @pl.when(j == 4)
    def _():
        fn(dp_ref[...])

    @pl.when(j > 4)
    def _():
        fn(dg_ref[0])


def _section_specs(tm, D, order):
    pick = (lambda a, b: (b, a)) if order == "ji" else (lambda a, b: (a, b))

    def at(fn):
        return lambda a, b: fn(*pick(a, b))

    return [pl.BlockSpec((1, tm, D), at(lambda i, j: (jnp.minimum(j, 3), i, 0))),
            pl.BlockSpec((tm, D), at(lambda i, j: (i, 0))),
            pl.BlockSpec((1, tm, D), at(lambda i, j: (jnp.clip(j - 5, 0, 1), i, 0)))]


def _dx(d1, dh4, dpv, dg2, w_in, dep):
    T, D = d1.shape
    tm = min(256, T)

    def body(d1_ref, dh_ref, dp_ref, dg_ref, w_ref, o_ref):
        blocks = [dh_ref[0], dh_ref[1], dh_ref[2], dh_ref[3], dp_ref[...], dg_ref[0], dg_ref[1]]
        acc = ALPHA * d1_ref[...]
        for j, blk in enumerate(blocks):
            acc = acc + lax.dot_general(blk, w_ref[:, j * D:(j + 1) * D], NT_DIMS, preferred_element_type=F32)
        o_ref[...] = acc

    row = lambda i: (i, 0)
    return _call_after(
        dep, body, (d1, dh4, dpv, dg2, w_in), name="dx", grid=(T // tm,),
        in_specs=[pl.BlockSpec((tm, D), row), pl.BlockSpec((4, tm, D), lambda i: (0, i, 0)),
                  pl.BlockSpec((tm, D), row), pl.BlockSpec((2, tm, D), lambda i: (0, i, 0)),
                  _resident((D, N_SEC * D))],
        out_specs=pl.BlockSpec((tm, D), row),
        out_shape=jax.ShapeDtypeStruct((T, D), F32),
        compiler_params=_params(("parallel",)))


def _dw_in(xb2, dh4, dpv, dg2):
    T, D = xb2.shape
    tk = min(512, T)
    nt = T // tk

    def body(x_ref, dh_ref, dp_ref, dg_ref, o_ref, ob_ref):
        j, t = pl.program_id(0), pl.program_id(1)
        xb = x_ref[...]

        def acc(blk):
            p = lax.dot_general(xb, blk, TN_DIMS, preferred_element_type=F32)

            @pl.when(t == 0)
            def _():
                o_ref[...] = p

            @pl.when(t > 0)
            def _():
                o_ref[...] += p

        _section_sources(j, (dh_ref, dp_ref, dg_ref), acc)

        @pl.when(t == nt - 1)
        def _():
            ob_ref[...] = o_ref[...].astype(BF16)

    return pl.pallas_call(
        body, name="dw_in", grid=(N_SEC, nt),
        in_specs=[pl.BlockSpec((tk, D), lambda j, t: (t, 0))] + _section_specs(tk, D, "ji"),
        out_specs=[pl.BlockSpec((D, D), lambda j, t: (0, j)), pl.BlockSpec((D, D), lambda j, t: (0, j))],
        out_shape=[jax.ShapeDtypeStruct((D, N_SEC * D), F32), jax.ShapeDtypeStruct((D, N_SEC * D), BF16)],
        compiler_params=_params(("parallel", "arbitrary")),
    )(xb2, dh4, dpv, dg2)


def _adam_shard(name, me_arr, grad, land, layout, w, m, v):
    shape = layout.shape
    n_split = 4
    blk = (shape[0] // n_split,) + shape[1:]
    zeros = (0,) * (len(shape) - 1)

    def body(me_ref, g_ref, r_ref, w_ref, m_ref, v_ref, g_out, d_out, m_out, v_out):
        g = g_ref[...]
        for k in range(N_DEV - 1):
            g = g + r_ref[k].astype(F32)
        d, m2, v2 = _adamw(w_ref[...], g, m_ref[...], v_ref[...])
        g_out[...] = g
        d_out[...] = d
        m_out[...] = m2
        v_out[...] = v2

    def own(i, me_ref):
        bi = layout.block_index(me_ref[0])
        return (bi[0] * n_split + i,) + tuple(bi[1:]) if layout.kind == "row" else (i,) + tuple(bi[1:])

    plain = pl.BlockSpec(blk, lambda i, me_ref: (i,) + zeros)
    grid_spec = pltpu.PrefetchScalarGridSpec(
        num_scalar_prefetch=1, grid=(n_split,),
        in_specs=[pl.BlockSpec(blk, own),
                  pl.BlockSpec((N_DEV - 1,) + blk, lambda i, me_ref: (0, i) + zeros),
                  plain, plain, plain],
        out_specs=[plain] * 4)
    return pl.pallas_call(
        body, name=name, grid_spec=grid_spec,
        out_shape=[jax.ShapeDtypeStruct(shape, F32)] * 4,
        compiler_params=_params(("parallel",)),
    )(me_arr, grad, land, w, m, v)


def _vec_allreduce_adam(vec, small_w, small_m, small_v):
    n = len(small_w)
    D = vec.shape[1]

    def body(*refs):
        vec_ref = refs[0]
        ws, ms, vs = refs[1:1 + n], refs[1 + n:1 + 2 * n], refs[1 + 2 * n:1 + 3 * n]
        outs = refs[1 + 3 * n:2 + 7 * n]
        gat, send_sems, recv_sems = refs[2 + 7 * n:]
        loss_ref, g_out, d_out = outs[0], outs[1:1 + n], outs[1 + n:1 + 2 * n]
        m_out, v_out = outs[1 + 2 * n:1 + 3 * n], outs[1 + 3 * n:1 + 4 * n]
        x, y, c = _me()
        me = 4 * x + 2 * y + c
        gat[me] = vec_ref[...]
        copies = []
        for k in range(1, N_DEV):
            fx, fy, fc = (k >> 2) & 1, (k >> 1) & 1, k & 1
            to = (1 - x if fx else x, 1 - y if fy else y, 1 - c if fc else c)
            cp = pltpu.make_async_remote_copy(
                src_ref=vec_ref, dst_ref=gat.at[me], send_sem=send_sems.at[k - 1],
                recv_sem=recv_sems.at[k - 1], device_id=to, device_id_type=MESH)
            cp.start()
            copies.append(cp)
        for cp in copies:
            cp.wait()
        tot = gat[0]
        for d in range(1, N_DEV):
            tot = tot + gat[d]
        loss_ref[...] = jnp.broadcast_to(jnp.sum(tot[7:8, :], axis=1, keepdims=True), loss_ref.shape)
        lbl = ws[0][...]
        mx = jnp.maximum(lbl[0:1, :], lbl[1:2, :])
        e0, e1 = jnp.exp(lbl[0:1, :] - mx), jnp.exp(lbl[1:2, :] - mx)
        p0 = e0 / (e0 + e1)
        dl0 = tot[0:1, :] * p0 * (1.0 - p0)
        grads = [jnp.concatenate([dl0, -dl0], axis=0)] + [tot[r:r + 1, :] for r in range(1, n)]
        for i in range(n):
            d, m2, v2 = _adamw(ws[i][...], grads[i], ms[i][...], vs[i][...])
            g_out[i][...] = grads[i]
            d_out[i][...] = d
            m_out[i][...] = m2
            v_out[i][...] = v2

    vm = pl.BlockSpec(memory_space=pltpu.VMEM)
    shapes = [jax.ShapeDtypeStruct(w.shape, F32) for w in small_w]
    return pl.pallas_call(
        body, name="vec_allreduce_adam",
        out_shape=[jax.ShapeDtypeStruct((1, 128), F32)] + shapes * 4,
        in_specs=[vm] * (1 + 3 * n), out_specs=[vm] * (1 + 4 * n),
        scratch_shapes=[pltpu.VMEM((N_DEV, 8, D), F32), pltpu.SemaphoreType.DMA((N_DEV - 1,)),
                        pltpu.SemaphoreType.DMA((N_DEV - 1,))],
    )(vec, *small_w, *small_m, *small_v)


def kernel(x, w_in, lb_logits, hgrn_norm_g, w_a, w_pool, pool_scale, w_out, ln1_g, ln1_b, w_up, w_down, ln2_g, ln2_b, loss_target, m_w_in, m_lb_logits, m_hgrn_norm_g, m_w_a, m_w_pool, m_pool_scale, m_w_out, m_ln1_g, m_ln1_b, m_w_up, m_w_down, m_ln2_g, m_ln2_b, v_w_in, v_lb_logits, v_hgrn_norm_g, v_w_a, v_w_pool, v_pool_scale, v_w_out, v_ln1_g, v_ln1_b, v_w_up, v_w_down, v_ln2_g, v_ln2_b):
    Bl, S, D = x.shape
    T = Bl * S
    pg = D // POOL_GROUPS
    x2 = x.reshape(T, D)
    tgt = loss_target.reshape(T, D)
    me = 4 * lax.axis_index("x") + 2 * lax.axis_index("y") + lax.axis_index("c")
    me_arr = jnp.reshape(me, (1,)).astype(jnp.int32)

    names = ["w_in", "w_a", "w_pool", "w_out", "w_up", "w_down"]
    big_w = dict(zip(names, [w_in[0], w_a[0], w_pool[0], w_out[0], w_up[0], w_down[0]]))
    big_m = dict(zip(names, [m_w_in[0], m_w_a[0], m_w_pool[0], m_w_out[0], m_w_up[0], m_w_down[0]]))
    big_v = dict(zip(names, [v_w_in[0], v_w_a[0], v_w_pool[0], v_w_out[0], v_w_up[0], v_w_down[0]]))
    kinds = dict(w_in="col", w_a="row", w_pool="pool", w_out="row", w_up="col", w_down="row")
    lay = {nm: _Sharded(kinds[nm], big_w[nm].shape) for nm in names}
    wb = {nm: big_w[nm].astype(BF16) for nm in names}

    (w_in_f,) = _all_gather("ag_w_in", [wb["w_in"]], [lay["w_in"]])
    rest = names[1:]
    own_placed = [lax.dynamic_update_slice(lax.empty(lay[nm].full_shape, BF16), wb[nm], lay[nm].start_index(me))
                  for nm in rest]
    ag_rest = _exchange_start("ag_rest", [wb[nm] for nm in rest], own_placed,
                              src_at=lambda w, ref, peer: ref,
                              dst_at=lambda w, ref, mine, k: lay[rest[w]].at(ref, mine), after=w_in_f)

    proj, xb2 = _proj(x2, w_in_f, ag_rest["token"])
    proj5 = proj.reshape(N_SEC, Bl, S, D)
    ain3, st0_all = _hgrn_fwd(proj5, lb_logits, hgrn_norm_g)
    w_a_f, w_pool_f, w_out_f, w_up_f, w_down_f = _exchange_wait(ag_rest, ain3)
    pooled3, bp3 = _pool_fwd(proj5, w_pool_f)
    ain, pooled, bp = ain3.reshape(T, D), pooled3.reshape(T, D), bp3.reshape(T, D)
    a, merged, xhat1, rs1, x1b = _mix_fwd(ain, proj, bp, x2, w_a_f, w_out_f, pool_scale, ln1_g, ln1_b)
    hp, dr2, dr2b, vec_mlp = _mlp_fwd(x1b, w_up_f, w_down_f, xhat1, tgt, ln1_g, ln1_b, ln2_g, ln2_b)

    def scatter_start(name, nms, grads_b, after):
        lands = [lax.empty((N_DEV - 1,) + lay[nm].shape, BF16) for nm in nms]
        return _exchange_start(name, grads_b, lands,
                               src_at=lambda w, ref, peer: lay[nms[w]].at(ref, peer),
                               dst_at=lambda w, ref, mine, k: ref.at[k - 1], after=after)

    dhp, dr1, dr1b, vec_ln1 = _mlp_bwd(dr2b, dr2, hp, w_up_f, w_down_f, xhat1, rs1, ln1_g)
    tk = min(512, T)
    FF = 4 * D
    gw, gwb = {}, {}
    gw["w_down"], gwb["w_down"] = _mm_tn(
        "dw_down", hp, dr2b, FF // D, (tk, D), lambda j, t: (t, j), (tk, D), lambda j, t: (t, 0),
        (FF, D), (D, D), lambda j, t: (j, 0), a_fn=lambda h: jnp.square(jnp.maximum(h, 0.0)))
    rs_down = scatter_start("rs_w_down", ["w_down"], [gwb["w_down"]], gw["w_down"])
    gw["w_up"], gwb["w_up"] = _mm_tn(
        "dw_up", x1b, dhp, FF // D, (tk, D), lambda j, t: (t, 0), (tk, D), lambda j, t: (t, j),
        (D, FF), (D, D), lambda j, t: (0, j), dep=rs_down["token"])
    rs_up = scatter_start("rs_w_up", ["w_up"], [gwb["w_up"]], gw["w_up"])
    da_b, dbp_b, dain, dpooled, dg2, vec_mix = _mix_bwd(dr1b, proj, a, bp, w_a_f, w_out_f, w_pool_f, pool_scale,
                                                        rs_up["token"])
    gw["w_out"], gwb["w_out"] = _mm_tn(
        "dw_out", merged, dr1b, 2, (tk, D), lambda j, t: (t, 0), (tk, D // 2), lambda j, t: (t, j),
        (D, D), (D, D // 2), lambda j, t: (0, j))
    gw["w_a"], gwb["w_a"] = _mm_tn(
        "dw_a", ain, da_b, 2, (tk, D), lambda j, t: (t, 0), (tk, D // 2), lambda j, t: (t, j),
        (D, D), (D, D // 2), lambda j, t: (0, j))
    gw["w_pool"], gwb["w_pool"] = _mm_tn(
        "dw_pool", pooled, dbp_b, POOL_GROUPS, (tk, pg), lambda j, t: (t, j), (tk, pg), lambda j, t: (t, j),
        (POOL_GROUPS, pg, pg), (1, pg, pg), lambda j, t: (j, 0, 0))
    mid = ["w_out", "w_a", "w_pool"]
    rs_mid = scatter_start("rs_w_mid", mid, [gwb[nm] for nm in mid], gw["w_pool"])
    dpv = _pool_bwd(dpooled.reshape(Bl, S, D), rs_mid["token"]).reshape(T, D)
    dh4, vec_hgrn = _hgrn_bwd(proj5, lb_logits, hgrn_norm_g, dain.reshape(Bl, S, D), st0_all)
    dh4 = dh4.reshape(4, T, D)
    gw["w_in"], gwb["w_in"] = _dw_in(xb2, dh4, dpv, dg2)
    rs_in = scatter_start("rs_w_in", ["w_in"], [gwb["w_in"]], gw["w_in"])
    grad_x2 = _dx(dr1, dh4, dpv, dg2, w_in_f, rs_in["token"])
    grad_x = grad_x2.reshape(Bl, S, D)

    vec = vec_mlp + vec_ln1 + vec_mix + vec_hgrn
    small_names = ["lb_logits", "hgrn_norm_g", "pool_scale", "ln1_g", "ln1_b", "ln2_g", "ln2_b"]
    small_w = [lb_logits, hgrn_norm_g, pool_scale, ln1_g, ln1_b, ln2_g, ln2_b]
    small_m = [m_lb_logits, m_hgrn_norm_g, m_pool_scale, m_ln1_g, m_ln1_b, m_ln2_g, m_ln2_b]
    small_v = [v_lb_logits, v_hgrn_norm_g, v_pool_scale, v_ln1_g, v_ln1_b, v_ln2_g, v_ln2_b]
    res = _vec_allreduce_adam(vec, small_w, small_m, small_v)
    loss = res[0][0, 0]
    n = len(small_w)
    small = {nm: (res[1 + i], res[1 + n + i], res[1 + 2 * n + i], res[1 + 3 * n + i])
             for i, nm in enumerate(small_names)}

    big, last = {}, grad_x2
    for pend, nms in ((rs_down, ["w_down"]), (rs_up, ["w_up"]), (rs_mid, mid), (rs_in, ["w_in"])):
        for nm, land in zip(nms, _exchange_wait(pend, last)):
            outs = _adam_shard("adam_" + nm, me_arr, gw[nm], land, lay[nm], big_w[nm], big_m[nm], big_v[nm])
            big[nm] = tuple(t[None] for t in outs)
            last = outs[0]

    order = ["w_in", "lb_logits", "hgrn_norm_g", "w_a", "w_pool", "pool_scale", "w_out", "ln1_g", "ln1_b",
             "w_up", "w_down", "ln2_g", "ln2_b"]
    allp = {**big, **small}
    out = [loss, grad_x]
    for part in range(4):
        out += [allp[nm][part] for nm in order]
    return tuple(out)
```

```python
import jax
import jax.numpy as jnp
from jax import lax
from jax.experimental import pallas as pl
from jax.experimental.pallas import tpu as pltpu

F32 = jnp.float32
BF16 = jnp.bfloat16
MESH = pl.DeviceIdType.MESH

N_DEV = 8
N_CHIP = 4
HEAD = 128
CHUNK = 16
GROUP = 128
CH_PER_GROUP = GROUP // CHUNK
N_SEC = 7
POOL_GROUPS = 4
ALPHA = (2.0 * 1) ** 0.25
LN_EPS = 1e-5
RMS_EPS = 1e-6
Q_SCALE = HEAD ** -0.5
ADAM_LR = 0.001
ADAM_B1 = 0.9
ADAM_B2 = 0.999
ADAM_EPS = 1e-08
ADAM_WD = 0.01
ADAM_STEP = 10
VMEM_LIMIT = 56 << 20

NT_DIMS = (((1,), (1,)), ((), ()))
TN_DIMS = (((0,), (0,)), ((), ()))


def _params(sem=None):
    kw = dict(vmem_limit_bytes=VMEM_LIMIT)
    if sem is not None:
        kw["dimension_semantics"] = sem
    return pltpu.CompilerParams(**kw)


def _me():
    return lax.axis_index("x"), lax.axis_index("y"), lax.axis_index("c")


def _sigmoid(v):
    return jax.nn.sigmoid(v)


def _adamw(w, g, m, v):
    m = ADAM_B1 * m + (1.0 - ADAM_B1) * g
    v = ADAM_B2 * v + (1.0 - ADAM_B2) * jnp.square(g)
    m_hat = m / (1.0 - ADAM_B1 ** ADAM_STEP)
    v_hat = v / (1.0 - ADAM_B2 ** ADAM_STEP)
    delta = -ADAM_LR * (m_hat / (jnp.sqrt(v_hat) + ADAM_EPS) + ADAM_WD * w)
    return delta, m, v


class _Sharded:
    def __init__(self, kind, shard_shape):
        self.kind, self.shape = kind, tuple(shard_shape)

    @property
    def full_shape(self):
        r = self.shape
        if self.kind == "row":
            return (N_DEV * r[0],) + r[1:]
        return (r[0], N_DEV * r[1]) + r[2:]

    def at(self, ref, d):
        if self.kind == "col":
            n = self.shape[1]
            return ref.at[:, pl.ds(pl.multiple_of(d * n, 128), n)]
        if self.kind == "row":
            n = self.shape[0]
            return ref.at[pl.ds(pl.multiple_of(d * n, 16), n), :]
        n = self.shape[1]
        return ref.at[:, pl.ds(pl.multiple_of(d * n, 16), n), :]

    def block_index(self, d):
        return {"col": (0, d), "row": (d, 0), "pool": (0, d, 0)}[self.kind]

    def start_index(self, d):
        r = self.shape
        if self.kind == "row":
            return (d * r[0],) + (0,) * (len(r) - 1)
        return (0, d * r[1]) + (0,) * (len(r) - 2)


def _peer(k, x, y, c):
    return (1 - x if k & 4 else x, 1 - y if k & 2 else y, 1 - c if k & 1 else c)


def _all_gather(name, shards, layouts):
    nw = len(shards)

    def body(*refs):
        ins, outs = refs[:nw], refs[nw:2 * nw]
        send_sems, recv_sems, local_sems = refs[2 * nw:]
        x, y, c = _me()
        me = (x, y, c)
        sibling = (x, y, 1 - c)
        chips = [(1 - x, y), (x, 1 - y), (1 - x, 1 - y)]

        def copy(w, k, block, to, src=None):
            px, py, pc = block
            dst = layouts[w].at(outs[w], 4 * px + 2 * py + pc)
            return pltpu.make_async_remote_copy(
                src_ref=dst if src is None else src, dst_ref=dst,
                send_sem=send_sems.at[w, k], recv_sem=recv_sems.at[w, k],
                device_id=to, device_id_type=MESH)

        local = []
        for w in range(nw):
            mine = pltpu.make_async_copy(ins[w], layouts[w].at(outs[w], 4 * x + 2 * y + c), local_sems.at[w])
            mine.start()
            local.append(mine)
        first = []
        for w in range(nw):
            first.append(copy(w, 0, me, sibling, src=ins[w]))
            first += [copy(w, 1 + j, me, (*chip, c), src=ins[w]) for j, chip in enumerate(chips)]
        for cp in first:
            cp.start()
        passed = []
        for w in range(nw):
            for j, chip in enumerate(chips):
                copy(w, 1 + j, (*chip, c), me).wait_recv()
                fwd = copy(w, 4 + j, (*chip, c), sibling)
                fwd.start()
                passed.append(fwd)
        for w in range(nw):
            copy(w, 0, sibling, me).wait_recv()
            for j, chip in enumerate(chips):
                copy(w, 4 + j, (*chip, 1 - c), me).wait_recv()
        for cp in first + passed:
            cp.wait_send()
        for cp in local:
            cp.wait()

    any_spec = pl.BlockSpec(memory_space=pl.ANY)
    return pl.pallas_call(
        body, name=name,
        out_shape=[jax.ShapeDtypeStruct(l.full_shape, s.dtype) for s, l in zip(shards, layouts)],
        in_specs=[any_spec] * nw, out_specs=[any_spec] * nw,
        scratch_shapes=[pltpu.SemaphoreType.DMA((nw, 7)), pltpu.SemaphoreType.DMA((nw, 7)),
                        pltpu.SemaphoreType.DMA((nw,))],
    )(*shards)


HBM_SPEC = pl.BlockSpec(memory_space=pltpu.HBM)
SEM_SPEC = pl.BlockSpec(memory_space=pltpu.SEMAPHORE)
DATAFLOW = pltpu.SideEffectType.DATAFLOW_SIDE_EFFECTING


def _exchange_copies(srcs, lands, send_sems, recv_sems, src_at, dst_at):
    x, y, c = _me()
    me = 4 * x + 2 * y + c
    copies = []
    for w in range(len(srcs)):
        for k in range(1, N_DEV):
            px, py, pc = _peer(k, x, y, c)
            copies.append(pltpu.make_async_remote_copy(
                src_ref=src_at(w, srcs[w], 4 * px + 2 * py + pc), dst_ref=dst_at(w, lands[w], me, k),
                send_sem=send_sems.at[w * (N_DEV - 1) + k - 1], recv_sem=recv_sems.at[w * (N_DEV - 1) + k - 1],
                device_id=(px, py, pc), device_id_type=MESH))
    return copies


def _exchange_start(name, srcs, lands, src_at, dst_at, after):
    nw = len(srcs)

    def body(*refs):
        src_refs, land_refs = refs[:nw], refs[nw:2 * nw]
        send_sems, recv_sems = refs[2 * nw + 1], refs[2 * nw + 2]
        token = refs[-1]
        for cp in _exchange_copies(src_refs, land_refs, send_sems, recv_sems, src_at, dst_at):
            cp.start()
        token[...] = jnp.zeros_like(token)

    hbm = lambda a: pltpu.HBM(a.shape, a.dtype)
    outs = pl.pallas_call(
        body, name=name,
        out_shape=(pltpu.SemaphoreType.DMA((nw * (N_DEV - 1),)), pltpu.SemaphoreType.DMA((nw * (N_DEV - 1),)),
                   *[hbm(a) for a in srcs], *[hbm(a) for a in lands],
                   jax.ShapeDtypeStruct((8, 128), F32)),
        in_specs=[HBM_SPEC] * (2 * nw) + [pl.BlockSpec(memory_space=pl.ANY)],
        out_specs=(SEM_SPEC, SEM_SPEC, *[HBM_SPEC] * (2 * nw), pl.BlockSpec(memory_space=pltpu.VMEM)),
        input_output_aliases={i: 2 + i for i in range(2 * nw)},
        compiler_params=pltpu.CompilerParams(has_side_effects=DATAFLOW),
    )(*[pltpu.with_memory_space_constraint(a, pltpu.HBM) for a in list(srcs) + list(lands)], after)
    return dict(send=outs[0], recv=outs[1], srcs=outs[2:2 + nw], lands=outs[2 + nw:2 + 2 * nw],
                token=outs[-1], src_at=src_at, dst_at=dst_at, name=name)


def _exchange_wait(pending, after):
    nw = len(pending["srcs"])

    def body(*refs):
        src_refs, land_refs = refs[:nw], refs[nw:2 * nw]
        send_sems, recv_sems = refs[2 * nw], refs[2 * nw + 1]
        for cp in _exchange_copies(src_refs, land_refs, send_sems, recv_sems,
                                   pending["src_at"], pending["dst_at"]):
            cp.wait_send()
            cp.wait_recv()

    hbm = lambda a: pltpu.HBM(a.shape, a.dtype)
    outs = pl.pallas_call(
        body, name=pending["name"] + "_wait",
        out_shape=(*[hbm(a) for a in pending["srcs"]], *[hbm(a) for a in pending["lands"]]),
        in_specs=[HBM_SPEC] * (2 * nw) + [SEM_SPEC, SEM_SPEC, pl.BlockSpec(memory_space=pl.ANY)],
        out_specs=tuple([HBM_SPEC] * (2 * nw)),
        input_output_aliases={i: i for i in range(2 * nw)},
        compiler_params=pltpu.CompilerParams(has_side_effects=DATAFLOW),
    )(*pending["srcs"], *pending["lands"], pending["send"], pending["recv"], after)
    return outs[nw:]


def _call_after(dep, body, args, *, in_specs, **kw):
    n_in = len(args)

    def wrapped(*refs):
        body(*refs[:n_in], *refs[n_in + 1:])

    dep_spec = pl.BlockSpec(dep.shape, lambda *_: (0,) * dep.ndim)
    return pl.pallas_call(wrapped, in_specs=list(in_specs) + [dep_spec], **kw)(*args, dep)


def _resident(shape):
    return pl.BlockSpec(shape, lambda *_: (0,) * len(shape), pipeline_mode=pl.Buffered(1))


def _proj(x2, w_in, dep):
    T, D = x2.shape
    tm = min(256, T)

    def body(x_ref, w_ref, o_ref, xt_ref):
        x = x_ref[...]
        xt_ref[...] = x.T.astype(BF16)
        xb = x.astype(BF16)
        for j in range(N_SEC):
            o_ref[j] = jnp.dot(xb, w_ref[:, j * D:(j + 1) * D], preferred_element_type=F32)

    return _call_after(
        dep, body, (x2, w_in), name="proj", grid=(T // tm,),
        in_specs=[pl.BlockSpec((tm, D), lambda i: (i, 0)), _resident((D, N_SEC * D))],
        out_specs=[pl.BlockSpec((N_SEC, tm, D), lambda i: (0, i, 0)), pl.BlockSpec((D, tm), lambda i: (0, i))],
        out_shape=[jax.ShapeDtypeStruct((N_SEC, T, D), F32), jax.ShapeDtypeStruct((D, T), BF16)],
        compiler_params=_params(("parallel",)))


def _chunk_cumsum(v, reverse=False):
    rows = v.shape[0]
    pos = lax.broadcasted_iota(jnp.int32, v.shape, 0) % CHUNK
    for sh in (1, 2, 4, 8):
        if reverse:
            v = v + jnp.where(pos < CHUNK - sh, pltpu.roll(v, rows - sh, 0), 0.0)
        else:
            v = v + jnp.where(pos >= sh, pltpu.roll(v, sh, 0), 0.0)
    return v


def _hgrn_gates(q, f_pre, lb_logits):
    l0, l1 = lb_logits[0:1, :], lb_logits[1:2, :]
    mx = jnp.maximum(l0, l1)
    e0, e1 = jnp.exp(l0 - mx), jnp.exp(l1 - mx)
    lb = e0 / (e0 + e1)
    sq = _sigmoid(q)
    qf = q * sq * Q_SCALE
    sg = _sigmoid(f_pre)
    f = lb + (1.0 - lb) * sg
    k = 1.0 - f
    log_f = jnp.log(f)
    G = _chunk_cumsum(log_f)
    g_to_end = _chunk_cumsum(log_f, reverse=True) - log_f
    e_g = jnp.exp(G)
    e_ng = jnp.exp(-G)
    e_ge = jnp.exp(g_to_end)
    return dict(lb=lb, sq=sq, qf=qf, sg=sg, f=f, k=k, G=G, e_g=e_g, e_ng=e_ng, e_ge=e_ge,
                qd=qf * e_g, ki=k * e_ng, ke=k * e_ge, dec=jnp.exp(G + g_to_end))


def _intra_mask():
    r = lax.broadcasted_iota(jnp.int32, (GROUP, GROUP), 0)
    c = lax.broadcasted_iota(jnp.int32, (GROUP, GROUP), 1)
    return (r // CHUNK == c // CHUNK) & (c <= r)


def _chunk_outer(lhs_rows, rhs_b, out_scr, sb):
    lane = lax.broadcasted_iota(jnp.int32, (GROUP, GROUP), 1) // CHUNK
    for g in range(sb // GROUP):
        sl = slice(g * GROUP, (g + 1) * GROUP)
        lhs_t = lhs_rows[sl].T
        for cc in range(CH_PER_GROUP):
            masked = jnp.where(lane == cc, lhs_t, 0.0).astype(BF16)
            out_scr[g * CH_PER_GROUP + cc] = jnp.dot(masked, rhs_b[sl], preferred_element_type=F32)


def _hgrn_forward_block(c, v, st0, sb, o_scr, kv_scr, st_scr, dec_scr):
    nc = sb // CHUNK
    qd_b, ki_b, ke_b, v_b = (c["qd"].astype(BF16), c["ki"].astype(BF16), c["ke"].astype(BF16),
                             v.astype(BF16))
    mask = _intra_mask()
    for g in range(sb // GROUP):
        sl = slice(g * GROUP, (g + 1) * GROUP)
        sc = lax.dot_general(qd_b[sl], ki_b[sl], NT_DIMS, preferred_element_type=F32)
        a = jnp.where(mask, sc, 0.0).astype(BF16)
        o_scr[sl, :] = jnp.dot(a, v_b[sl], preferred_element_type=F32)
    _chunk_outer(v, ke_b, kv_scr, sb)
    dec_scr[...] = c["dec"]

    def rec(n, st):
        st_scr[n] = st
        d = dec_scr[pl.ds(pl.multiple_of(n * CHUNK, CHUNK), 1), :]
        return st * d + kv_scr[n]

    st_end = lax.fori_loop(0, nc, rec, st0)

    for n in range(nc):
        rows = slice(n * CHUNK, (n + 1) * CHUNK)
        o_scr[rows, :] += lax.dot_general(qd_b[rows], st_scr[n].astype(BF16), NT_DIMS,
                                          preferred_element_type=F32)
    return st_end


def _hgrn_fwd(proj5, lb_logits, gn):
    _, Bl, S, D = proj5.shape
    H = D // HEAD
    sb = min(512, S)
    nsb = S // sb
    nc = sb // CHUNK

    def body(p_ref, lbl_ref, gn_ref, ain_ref, aint_ref, st0_ref, carry, o_scr, kv_scr, st_scr, dec_scr):
        s = pl.program_id(2)

        @pl.when(s == 0)
        def _():
            carry[...] = jnp.zeros_like(carry)

        st0 = carry[...]
        st0_ref[0, 0, 0] = st0
        c = _hgrn_gates(p_ref[0, 0], p_ref[1, 0], lbl_ref[...])
        carry[...] = _hgrn_forward_block(c, p_ref[2, 0], st0, sb, o_scr, kv_scr, st_scr, dec_scr)
        o = o_scr[...]
        rinv = lax.rsqrt(jnp.mean(o * o, axis=-1, keepdims=True) + RMS_EPS)
        ain = o * rinv * gn_ref[...] * _sigmoid(p_ref[3, 0])
        ain_ref[0] = ain.astype(BF16)
        aint_ref[...] = ain.T.astype(BF16)

    return pl.pallas_call(
        body, name="hgrn_fwd", grid=(H, Bl, nsb),
        in_specs=[pl.BlockSpec((4, 1, sb, HEAD), lambda h, b, s: (0, b, s, h)),
                  pl.BlockSpec((2, HEAD), lambda h, b, s: (0, h)),
                  pl.BlockSpec((1, HEAD), lambda h, b, s: (0, h))],
        out_specs=[pl.BlockSpec((1, sb, HEAD), lambda h, b, s: (b, s, h)),
                   pl.BlockSpec((HEAD, sb), lambda h, b, s: (h, b * nsb + s)),
                   pl.BlockSpec((1, 1, 1, HEAD, HEAD), lambda h, b, s: (b, h, s, 0, 0))],
        out_shape=[jax.ShapeDtypeStruct((Bl, S, D), BF16), jax.ShapeDtypeStruct((D, Bl * S), BF16),
                   jax.ShapeDtypeStruct((Bl, H, nsb, HEAD, HEAD), F32)],
        scratch_shapes=[pltpu.VMEM((HEAD, HEAD), F32), pltpu.VMEM((sb, HEAD), F32),
                        pltpu.VMEM((nc, HEAD, HEAD), F32), pltpu.VMEM((nc, HEAD, HEAD), F32),
                        pltpu.VMEM((sb, HEAD), F32)],
        compiler_params=_params(("parallel", "parallel", "arbitrary")),
    )(proj5, lb_logits, gn)


def _window_count(shape, g):
    pos = lax.broadcasted_iota(jnp.int32, shape, 0)
    return pos, jnp.minimum(pos + 1, jnp.left_shift(2, g)).astype(F32)


def _select_window(g, sums):
    return jnp.where(g == 0, sums[0], jnp.where(g == 1, sums[1], jnp.where(g == 2, sums[2], sums[3])))


def _pool_fwd(proj5, w_pool):
    _, Bl, S, D = proj5.shape
    pg = D // POOL_GROUPS

    def body(v_ref, w_ref, pooled_t_ref, bp_ref):
        g = pl.program_id(1)
        v = v_ref[0, 0]
        pos, cnt = _window_count(v.shape, g)
        cur, sums = v, []
        for sh in (1, 2, 4, 8):
            cur = cur + jnp.where(pos >= sh, pltpu.roll(cur, sh, 0), 0.0)
            sums.append(cur)
        pooled = _select_window(g, sums) / cnt - v
        pooled_t_ref[...] = pooled.T.astype(BF16)
        bp_ref[0] = jnp.dot(pooled.astype(BF16), w_ref[0], preferred_element_type=F32)

    return pl.pallas_call(
        body, name="pool_fwd", grid=(Bl, POOL_GROUPS),
        in_specs=[pl.BlockSpec((1, 1, S, pg), lambda b, g: (4, b, 0, g)),
                  pl.BlockSpec((1, pg, pg), lambda b, g: (g, 0, 0))],
        out_specs=[pl.BlockSpec((pg, S), lambda b, g: (g, b)),
                   pl.BlockSpec((1, S, pg), lambda b, g: (b, 0, g))],
        out_shape=[jax.ShapeDtypeStruct((D, Bl * S), BF16), jax.ShapeDtypeStruct((Bl, S, D), F32)],
        compiler_params=_params(("parallel", "parallel")),
    )(proj5, w_pool)


def _layer_norm_fwd(r):
    mu = jnp.mean(r, axis=-1, keepdims=True)
    d = r - mu
    rs = lax.rsqrt(jnp.mean(d * d, axis=-1, keepdims=True) + LN_EPS)
    return d * rs, rs


def _layer_norm_bwd(dy_g, xhat, rs):
    return rs * (dy_g - jnp.mean(dy_g, axis=-1, keepdims=True)
                 - xhat * jnp.mean(dy_g * xhat, axis=-1, keepdims=True))


def _mix_fwd(ain, proj, bp, x2, w_a, w_out, ps, g1, b1):
    T, D = x2.shape
    tm = min(256, T)

    def body(ain_ref, ga_ref, gb_ref, bp_ref, x_ref, wa_ref, wo_ref, ps_ref, g1_ref, b1_ref,
             a_ref, mgt_ref, xh_ref, rs_ref, x1b_ref, x1t_ref):
        a = jnp.dot(ain_ref[...], wa_ref[...], preferred_element_type=F32)
        a_ref[...] = a
        merged = _sigmoid(ga_ref[0]) * a + _sigmoid(gb_ref[0]) * (bp_ref[...] * ps_ref[...])
        mgt_ref[...] = merged.T.astype(BF16)
        r1 = ALPHA * x_ref[...] + jnp.dot(merged.astype(BF16), wo_ref[...], preferred_element_type=F32)
        xhat, rs = _layer_norm_fwd(r1)
        xh_ref[...] = xhat
        rs_ref[...] = rs
        x1 = xhat * g1_ref[...] + b1_ref[...]
        x1b_ref[...] = x1.astype(BF16)
        x1t_ref[...] = x1.T.astype(BF16)

    row = lambda i: (i, 0)
    col = lambda i: (0, i)
    full = lambda i: (0, 0)
    return pl.pallas_call(
        body, name="mix_fwd", grid=(T // tm,),
        in_specs=[pl.BlockSpec((tm, D), row),
                  pl.BlockSpec((1, tm, D), lambda i: (5, i, 0)),
                  pl.BlockSpec((1, tm, D), lambda i: (6, i, 0)),
                  pl.BlockSpec((tm, D), row), pl.BlockSpec((tm, D), row),
                  pl.BlockSpec((D, D), full), pl.BlockSpec((D, D), full),
                  pl.BlockSpec((1, D), full), pl.BlockSpec((1, D), full), pl.BlockSpec((1, D), full)],
        out_specs=[pl.BlockSpec((tm, D), row), pl.BlockSpec((D, tm), col), pl.BlockSpec((tm, D), row),
                   pl.BlockSpec((tm, 1), row), pl.BlockSpec((tm, D), row), pl.BlockSpec((D, tm), col)],
        out_shape=[jax.ShapeDtypeStruct((T, D), F32), jax.ShapeDtypeStruct((D, T), BF16),
                   jax.ShapeDtypeStruct((T, D), F32), jax.ShapeDtypeStruct((T, 1), F32),
                   jax.ShapeDtypeStruct((T, D), BF16), jax.ShapeDtypeStruct((D, T), BF16)],
        compiler_params=_params(("parallel",)),
    )(ain, proj, proj, bp, x2, w_a, w_out, ps, g1, b1)


def _mlp_fwd(x1b, w_up, w_down, xhat1, tgt, g1, b1, g2, b2):
    T, D = xhat1.shape
    FF = w_up.shape[1]
    tm = min(256, T)

    def body(x_ref, wu_ref, wd_ref, xh_ref, t_ref, g1_ref, b1_ref, g2_ref, b2_ref,
             hp_ref, h_ref, dr_ref, drb_ref, drt_ref, vec_ref):
        @pl.when(pl.program_id(0) == 0)
        def _():
            vec_ref[...] = jnp.zeros_like(vec_ref)

        xb = x_ref[...]
        x1 = xh_ref[...] * g1_ref[...] + b1_ref[...]
        r2 = ALPHA * x1
        for f in range(FF // D):
            cols = slice(f * D, (f + 1) * D)
            hp = jnp.dot(xb, wu_ref[:, cols], preferred_element_type=F32)
            hp_ref[:, cols] = hp
            h = jnp.square(jnp.maximum(hp, 0.0)).astype(BF16)
            h_ref[:, cols] = h
            r2 = r2 + jnp.dot(h, wd_ref[cols, :], preferred_element_type=F32)
        xhat2, rs2 = _layer_norm_fwd(r2)
        err = xhat2 * g2_ref[...] + b2_ref[...] - t_ref[...]
        dy = err / D
        vec_ref[5:6, :] += jnp.sum(dy * xhat2, axis=0, keepdims=True)
        vec_ref[6:7, :] += jnp.sum(dy, axis=0, keepdims=True)
        vec_ref[7:8, :] += jnp.sum(0.5 * err * err / D, axis=0, keepdims=True)
        dr = _layer_norm_bwd(dy * g2_ref[...], xhat2, rs2)
        dr_ref[...] = dr
        drb_ref[...] = dr.astype(BF16)
        drt_ref[...] = dr.T.astype(BF16)

    row = lambda i: (i, 0)
    full = lambda i: (0, 0)
    return pl.pallas_call(
        body, name="mlp_fwd", grid=(T // tm,),
        in_specs=[pl.BlockSpec((tm, D), row), _resident((D, FF)), _resident((FF, D)),
                  pl.BlockSpec((tm, D), row), pl.BlockSpec((tm, D), row),
                  pl.BlockSpec((1, D), full), pl.BlockSpec((1, D), full),
                  pl.BlockSpec((1, D), full), pl.BlockSpec((1, D), full)],
        out_specs=[pl.BlockSpec((tm, FF), row), pl.BlockSpec((tm, FF), row), pl.BlockSpec((tm, D), row),
                   pl.BlockSpec((tm, D), row), pl.BlockSpec((D, tm), lambda i: (0, i)),
                   pl.BlockSpec((8, D), full)],
        out_shape=[jax.ShapeDtypeStruct((T, FF), F32), jax.ShapeDtypeStruct((T, FF), BF16),
                   jax.ShapeDtypeStruct((T, D), F32), jax.ShapeDtypeStruct((T, D), BF16),
                   jax.ShapeDtypeStruct((D, T), BF16), jax.ShapeDtypeStruct((8, D), F32)],
        compiler_params=_params(("arbitrary",)),
    )(x1b, w_up, w_down, xhat1, tgt, g1, b1, g2, b2)


def _mlp_bwd(drb, dr, hp, w_up, w_down, xhat1, rs1, g1):
    T, D = dr.shape
    FF = hp.shape[1]
    tm = min(256, T)

    def body(drb_ref, dr_ref, hp_ref, wu_ref, wd_ref, xh_ref, rs_ref, g1_ref,
             dhp_ref, d1_ref, d1b_ref, vec_ref):
        @pl.when(pl.program_id(0) == 0)
        def _():
            vec_ref[...] = jnp.zeros_like(vec_ref)

        drb = drb_ref[...]
        dx1 = ALPHA * dr_ref[...]
        for f in range(FF // D):
            cols = slice(f * D, (f + 1) * D)
            dh = lax.dot_general(drb, wd_ref[cols, :], NT_DIMS, preferred_element_type=F32)
            dhp = (dh * (2.0 * jnp.maximum(hp_ref[:, cols], 0.0))).astype(BF16)
            dhp_ref[:, cols] = dhp
            dx1 = dx1 + lax.dot_general(dhp, wu_ref[:, cols], NT_DIMS, preferred_element_type=F32)
        xhat = xh_ref[...]
        vec_ref[3:4, :] += jnp.sum(dx1 * xhat, axis=0, keepdims=True)
        vec_ref[4:5, :] += jnp.sum(dx1, axis=0, keepdims=True)
        d1 = _layer_norm_bwd(dx1 * g1_ref[...], xhat, rs_ref[...])
        d1_ref[...] = d1
        d1b_ref[...] = d1.astype(BF16)

    row = lambda i: (i, 0)
    full = lambda i: (0, 0)
    return pl.pallas_call(
        body, name="mlp_bwd", grid=(T // tm,),
        in_specs=[pl.BlockSpec((tm, D), row), pl.BlockSpec((tm, D), row), pl.BlockSpec((tm, FF), row),
                  _resident((D, FF)), _resident((FF, D)),
                  pl.BlockSpec((tm, D), row), pl.BlockSpec((tm, 1), row), pl.BlockSpec((1, D), full)],
        out_specs=[pl.BlockSpec((tm, FF), row), pl.BlockSpec((tm, D), row), pl.BlockSpec((tm, D), row),
                   pl.BlockSpec((8, D), full)],
        out_shape=[jax.ShapeDtypeStruct((T, FF), BF16), jax.ShapeDtypeStruct((T, D), F32),
                   jax.ShapeDtypeStruct((T, D), BF16), jax.ShapeDtypeStruct((8, D), F32)],
        compiler_params=_params(("arbitrary",)),
    )(drb, dr, hp, w_up, w_down, xhat1, rs1, g1)


def _dw(name, a_t, b, n_j, a_spec, b_spec, o_shape, o_block, o_map, transpose_out=False, dep=None,
        into=None):
    def body(*refs):
        a_ref, b_ref, o_ref, ob_ref = refs[0], refs[1], refs[-2], refs[-1]
        b_val = b_ref[0] if len(b_ref.shape) == 3 else b_ref[...]
        p = jnp.dot(a_ref[...], b_val, preferred_element_type=F32)
        if transpose_out:
            p = p.T
        p = p.reshape(o_ref.shape)
        o_ref[...] = p
        ob_ref[...] = p.astype(BF16)

    o_spec = pl.BlockSpec(o_block, o_map)
    kw = dict(name=name, grid=(n_j,), in_specs=[a_spec, b_spec], out_specs=[o_spec, o_spec],
              out_shape=[jax.ShapeDtypeStruct(o_shape, F32), jax.ShapeDtypeStruct(o_shape, BF16)],
              compiler_params=_params(("parallel",)))
    args = (a_t, b)
    if into is not None:
        kw["in_specs"] = kw["in_specs"] + [pl.BlockSpec(memory_space=pl.ANY)] * 2
        kw["input_output_aliases"] = {2: 0, 3: 1}
        args = args + tuple(into)
    if dep is None:
        return pl.pallas_call(body, **kw)(*args)
    return _call_after(dep, body, args, **kw)


def _mix_bwd(d1b, proj, a, bp, w_a, w_out, w_pool, ps, dep):
    T, D = a.shape
    tm = min(256, T)
    pg = D // POOL_GROUPS

    def body(d1b_ref, ga_ref, gb_ref, a_ref, bp_ref, wa_ref, wo_ref, wp_ref, ps_ref,
             da_ref, dbp_ref, dain_ref, dpl_ref, dg_ref, vec_ref):
        @pl.when(pl.program_id(0) == 0)
        def _():
            vec_ref[...] = jnp.zeros_like(vec_ref)

        dm = lax.dot_general(d1b_ref[...], wo_ref[...], NT_DIMS, preferred_element_type=F32)
        sa, sg = _sigmoid(ga_ref[0]), _sigmoid(gb_ref[0])
        bp_v, ps_v = bp_ref[...], ps_ref[...]
        da = (dm * sa).astype(BF16)
        db = dm * sg
        dg_ref[0] = (dm * a_ref[...] * sa * (1.0 - sa)).astype(BF16)
        dg_ref[1] = (dm * (bp_v * ps_v) * sg * (1.0 - sg)).astype(BF16)
        vec_ref[2:3, :] += jnp.sum(db * bp_v, axis=0, keepdims=True)
        dbp = (db * ps_v).astype(BF16)
        da_ref[...] = da
        dbp_ref[...] = dbp
        dain_ref[...] = lax.dot_general(da, wa_ref[...], NT_DIMS, preferred_element_type=F32)
        for g in range(POOL_GROUPS):
            cols = slice(g * pg, (g + 1) * pg)
            dpl_ref[:, cols] = lax.dot_general(dbp[:, cols], wp_ref[g], NT_DIMS,
                                               preferred_element_type=F32)

    row = lambda i: (i, 0)
    full = lambda i: (0, 0)
    return _call_after(
        dep, body, (d1b, proj, proj, a, bp, w_a, w_out, w_pool, ps), name="mix_bwd", grid=(T // tm,),
        in_specs=[pl.BlockSpec((tm, D), row),
                  pl.BlockSpec((1, tm, D), lambda i: (5, i, 0)),
                  pl.BlockSpec((1, tm, D), lambda i: (6, i, 0)),
                  pl.BlockSpec((tm, D), row), pl.BlockSpec((tm, D), row),
                  pl.BlockSpec((D, D), full), pl.BlockSpec((D, D), full),
                  pl.BlockSpec((POOL_GROUPS, pg, pg), lambda i: (0, 0, 0)),
                  pl.BlockSpec((1, D), full)],
        out_specs=[pl.BlockSpec((tm, D), row), pl.BlockSpec((tm, D), row),
                   pl.BlockSpec((tm, D), row), pl.BlockSpec((tm, D), row),
                   pl.BlockSpec((2, tm, D), lambda i: (0, i, 0)),
                   pl.BlockSpec((8, D), full)],
        out_shape=[jax.ShapeDtypeStruct((T, D), BF16), jax.ShapeDtypeStruct((T, D), BF16),
                   jax.ShapeDtypeStruct((T, D), F32), jax.ShapeDtypeStruct((T, D), F32),
                   jax.ShapeDtypeStruct((2, T, D), BF16), jax.ShapeDtypeStruct((8, D), F32)],
        compiler_params=_params(("arbitrary",)))


def _pool_bwd(dpooled3, dep):
    Bl, S, D = dpooled3.shape
    pg = D // POOL_GROUPS

    def body(dp_ref, dv_ref):
        g = pl.program_id(1)
        dp = dp_ref[0]
        pos, cnt = _window_count(dp.shape, g)
        cur, sums = dp / cnt, []
        for sh in (1, 2, 4, 8):
            cur = cur + jnp.where(pos < S - sh, pltpu.roll(cur, S - sh, 0), 0.0)
            sums.append(cur)
        dv_ref[0] = (_select_window(g, sums) - dp).astype(BF16)

    spec = pl.BlockSpec((1, S, pg), lambda b, g: (b, 0, g))
    return _call_after(
        dep, body, (dpooled3,), name="pool_bwd", grid=(Bl, POOL_GROUPS), in_specs=[spec], out_specs=spec,
        out_shape=jax.ShapeDtypeStruct((Bl, S, D), BF16),
        compiler_params=_params(("parallel", "parallel")))


def _hgrn_bwd(proj5, lb_logits, gn, dain3, st0_all):
    _, Bl, S, D = proj5.shape
    H = D // HEAD
    sb = min(512, S)
    nsb = S // sb
    nc = sb // CHUNK

    def body(p_ref, lbl_ref, gn_ref, dain_ref, st0_ref, d_ref, vec_ref,
             dcarry, o_scr, kv_scr, st_scr, dst_scr, dec_scr, dvi_scr, dke_scr, dqi_scr):
        b, s = pl.program_id(1), pl.program_id(2)

        @pl.when(s == 0)
        def _():
            dcarry[...] = jnp.zeros_like(dcarry)

        @pl.when((b == 0) & (s == 0))
        def _():
            vec_ref[...] = jnp.zeros_like(vec_ref)

        q, f_pre, v, og = p_ref[0, 0], p_ref[1, 0], p_ref[2, 0], p_ref[3, 0]
        c = _hgrn_gates(q, f_pre, lbl_ref[...])
        _hgrn_forward_block(c, v, st0_ref[0, 0, 0], sb, o_scr, kv_scr, st_scr, dec_scr)
        qd_b, ki_b, ke_b, v_b = (c["qd"].astype(BF16), c["ki"].astype(BF16),
                                 c["ke"].astype(BF16), v.astype(BF16))

        o = o_scr[...]
        rinv = lax.rsqrt(jnp.mean(o * o, axis=-1, keepdims=True) + RMS_EPS)
        on = o * rinv
        so = _sigmoid(og)
        gn_v = gn_ref[...]
        dain = dain_ref[0]
        vec_ref[1:2, :] += jnp.sum(dain * on * so, axis=0, keepdims=True)
        d_og = dain * on * gn_v * so * (1.0 - so)
        d_on = dain * gn_v * so
        do = rinv * (d_on - on * jnp.mean(d_on * on, axis=-1, keepdims=True))
        do_b = do.astype(BF16)

        mask = _intra_mask()
        dv_parts, dqd_parts, dki_parts = [], [], []
        for g in range(sb // GROUP):
            sl = slice(g * GROUP, (g + 1) * GROUP)
            sc = lax.dot_general(qd_b[sl], ki_b[sl], NT_DIMS, preferred_element_type=F32)
            a = jnp.where(mask, sc, 0.0).astype(BF16)
            da = lax.dot_general(do_b[sl], v_b[sl], NT_DIMS, preferred_element_type=F32)
            da = jnp.where(mask, da, 0.0).astype(BF16)
            dv_parts.append(lax.dot_general(a, do_b[sl], TN_DIMS, preferred_element_type=F32))
            dqd_parts.append(jnp.dot(da, ki_b[sl], preferred_element_type=F32))
            dki_parts.append(lax.dot_general(da, qd_b[sl], TN_DIMS, preferred_element_type=F32))
        dv_intra = jnp.concatenate(dv_parts, axis=0)
        dqd_intra = jnp.concatenate(dqd_parts, axis=0)
        dki = jnp.concatenate(dki_parts, axis=0)

        _chunk_outer(do, qd_b, kv_scr, sb)

        def rrec(i, dst):
            n = nc - 1 - i
            dst_scr[n] = dst
            d = dec_scr[pl.ds(pl.multiple_of(n * CHUNK, CHUNK), 1), :]
            return dst * d + kv_scr[n]

        dcarry[...] = lax.fori_loop(0, nc, rrec, dcarry[...])
        for n in range(nc):
            rows = slice(n * CHUNK, (n + 1) * CHUNK)
            dst_b = dst_scr[n].astype(BF16)
            dvi_scr[rows, :] = lax.dot_general(ke_b[rows], dst_b, NT_DIMS, preferred_element_type=F32)
            dke_scr[rows, :] = jnp.dot(v_b[rows], dst_b, preferred_element_type=F32)
            dqi_scr[rows, :] = jnp.dot(do_b[rows], st_scr[n].astype(BF16), preferred_element_type=F32)
        ddec = jnp.sum(dst_scr[...] * st_scr[...], axis=1)
        dgl = jnp.broadcast_to(ddec[:, None, :], (nc, CHUNK, HEAD)).reshape(sb, HEAD) * c["dec"]

        dqd = dqd_intra + dqi_scr[...]
        dke = dke_scr[...]
        t_ke = dke * c["ke"]
        dG = dqd * c["qd"] - dki * c["ki"] - t_ke
        dgl = dgl + _chunk_cumsum(t_ke) + _chunk_cumsum(t_ke, reverse=True) - t_ke
        dlogf = _chunk_cumsum(dG, reverse=True) + dgl
        dk = dki * c["e_ng"] + dke * c["e_ge"]
        df = dlogf / c["f"] - dk
        sg, sq, lb = c["sg"], c["sq"], c["lb"]
        vec_ref[0:1, :] += jnp.sum(df * (1.0 - sg), axis=0, keepdims=True)
        d_ref[0, 0] = (dqd * c["e_g"] * Q_SCALE * (sq + q * sq * (1.0 - sq))).astype(BF16)
        d_ref[1, 0] = (df * (1.0 - lb) * sg * (1.0 - sg)).astype(BF16)
        d_ref[2, 0] = (dv_intra + dvi_scr[...]).astype(BF16)
        d_ref[3, 0] = d_og.astype(BF16)

    rev = lambda s: nsb - 1 - s
    big = pltpu.VMEM((nc, HEAD, HEAD), F32)
    rows_f32 = pltpu.VMEM((sb, HEAD), F32)
    return pl.pallas_call(
        body, name="hgrn_bwd", grid=(H, Bl, nsb),
        in_specs=[pl.BlockSpec((4, 1, sb, HEAD), lambda h, b, s: (0, b, rev(s), h)),
                  pl.BlockSpec((2, HEAD), lambda h, b, s: (0, h)),
                  pl.BlockSpec((1, HEAD), lambda h, b, s: (0, h)),
                  pl.BlockSpec((1, sb, HEAD), lambda h, b, s: (b, rev(s), h)),
                  pl.BlockSpec((1, 1, 1, HEAD, HEAD), lambda h, b, s: (b, h, rev(s), 0, 0))],
        out_specs=[pl.BlockSpec((4, 1, sb, HEAD), lambda h, b, s: (0, b, rev(s), h)),
                   pl.BlockSpec((8, HEAD), lambda h, b, s: (0, h))],
        out_shape=[jax.ShapeDtypeStruct((4, Bl, S, D), BF16), jax.ShapeDtypeStruct((8, D), F32)],
        scratch_shapes=[pltpu.VMEM((HEAD, HEAD), F32), rows_f32, big, big, big, rows_f32,
                        rows_f32, rows_f32, rows_f32],
        compiler_params=_params(("parallel", "arbitrary", "arbitrary")),
    )(proj5, lb_logits, gn, dain3, st0_all)


def _dx(d1, dh4, dpv, dg2, w_in, dep):
    T, D = d1.shape
    tm = min(256, T)

    def body(d1_ref, dh_ref, dp_ref, dg_ref, w_ref, o_ref):
        blocks = [dh_ref[0], dh_ref[1], dh_ref[2], dh_ref[3], dp_ref[...], dg_ref[0], dg_ref[1]]
        acc = ALPHA * d1_ref[...]
        for j, blk in enumerate(blocks):
            acc = acc + lax.dot_general(blk, w_ref[:, j * D:(j + 1) * D], NT_DIMS, preferred_element_type=F32)
        o_ref[...] = acc

    row = lambda i: (i, 0)
    return _call_after(
        dep, body, (d1, dh4, dpv, dg2, w_in), name="dx", grid=(T // tm,),
        in_specs=[pl.BlockSpec((tm, D), row), pl.BlockSpec((4, tm, D), lambda i: (0, i, 0)),
                  pl.BlockSpec((tm, D), row), pl.BlockSpec((2, tm, D), lambda i: (0, i, 0)),
                  _resident((D, N_SEC * D))],
        out_specs=pl.BlockSpec((tm, D), row),
        out_shape=jax.ShapeDtypeStruct((T, D), F32),
        compiler_params=_params(("parallel",)))


def _dw_in(x_t, dh4, dpv, dg2, dep):
    D, T = x_t.shape
    shape = (D, N_SEC * D)
    x_spec = _resident((D, T))

    def part(name, b, n_j, first_sec, into, dep):
        b_spec = (pl.BlockSpec((1, T, D), lambda j: (j, 0, 0)) if b.ndim == 3
                  else pl.BlockSpec((T, D), lambda j: (0, 0)))
        return _dw(name, x_t, b, n_j, x_spec, b_spec, shape, (D, D), lambda j: (0, first_sec + j),
                   dep=dep, into=into)

    gates = part("dw_in_gates", dg2, 2, 5, None, dep)
    pool = part("dw_in_pool", dpv, 1, 4, gates, None)
    return part("dw_in_rec", dh4, 4, 0, pool, None)


def _adam_shard(name, me_arr, grad, land, layout, w, m, v):
    shape = layout.shape
    n_split = 4
    blk = (shape[0] // n_split,) + shape[1:]
    zeros = (0,) * (len(shape) - 1)

    def body(me_ref, g_ref, r_ref, w_ref, m_ref, v_ref, g_out, d_out, m_out, v_out):
        g = g_ref[...]
        for k in range(N_DEV - 1):
            g = g + r_ref[k].astype(F32)
        d, m2, v2 = _adamw(w_ref[...], g, m_ref[...], v_ref[...])
        g_out[...] = g
        d_out[...] = d
        m_out[...] = m2
        v_out[...] = v2

    def own(i, me_ref):
        bi = layout.block_index(me_ref[0])
        return (bi[0] * n_split + i,) + tuple(bi[1:]) if layout.kind == "row" else (i,) + tuple(bi[1:])

    plain = pl.BlockSpec(blk, lambda i, me_ref: (i,) + zeros)
    grid_spec = pltpu.PrefetchScalarGridSpec(
        num_scalar_prefetch=1, grid=(n_split,),
        in_specs=[pl.BlockSpec(blk, own),
                  pl.BlockSpec((N_DEV - 1,) + blk, lambda i, me_ref: (0, i) + zeros),
                  plain, plain, plain],
        out_specs=[plain] * 4)
    return pl.pallas_call(
        body, name=name, grid_spec=grid_spec,
        out_shape=[jax.ShapeDtypeStruct(shape, F32)] * 4,
        compiler_params=_params(("parallel",)),
    )(me_arr, grad, land, w, m, v)


def _vec_allreduce_adam(vec, small_w, small_m, small_v):
    n = len(small_w)
    D = vec.shape[1]

    def body(*refs):
        vec_ref = refs[0]
        ws, ms, vs = refs[1:1 + n], refs[1 + n:1 + 2 * n], refs[1 + 2 * n:1 + 3 * n]
        outs = refs[1 + 3 * n:2 + 7 * n]
        gat, send_sems, recv_sems = refs[2 + 7 * n:]
        loss_ref, g_out, d_out = outs[0], outs[1:1 + n], outs[1 + n:1 + 2 * n]
        m_out, v_out = outs[1 + 2 * n:1 + 3 * n], outs[1 + 3 * n:1 + 4 * n]
        x, y, c = _me()
        me = 4 * x + 2 * y + c
        gat[me] = vec_ref[...]
        copies = []
        for k in range(1, N_DEV):
            fx, fy, fc = (k >> 2) & 1, (k >> 1) & 1, k & 1
            to = (1 - x if fx else x, 1 - y if fy else y, 1 - c if fc else c)
            cp = pltpu.make_async_remote_copy(
                src_ref=vec_ref, dst_ref=gat.at[me], send_sem=send_sems.at[k - 1],
                recv_sem=recv_sems.at[k - 1], device_id=to, device_id_type=MESH)
            cp.start()
            copies.append(cp)
        for cp in copies:
            cp.wait()
        tot = gat[0]
        for d in range(1, N_DEV):
            tot = tot + gat[d]
        loss_ref[...] = jnp.broadcast_to(jnp.sum(tot[7:8, :], axis=1, keepdims=True), loss_ref.shape)
        lbl = ws[0][...]
        mx = jnp.maximum(lbl[0:1, :], lbl[1:2, :])
        e0, e1 = jnp.exp(lbl[0:1, :] - mx), jnp.exp(lbl[1:2, :] - mx)
        p0 = e0 / (e0 + e1)
        dl0 = tot[0:1, :] * p0 * (1.0 - p0)
        grads = [jnp.concatenate([dl0, -dl0], axis=0)] + [tot[r:r + 1, :] for r in range(1, n)]
        for i in range(n):
            d, m2, v2 = _adamw(ws[i][...], grads[i], ms[i][...], vs[i][...])
            g_out[i][...] = grads[i]
            d_out[i][...] = d
            m_out[i][...] = m2
            v_out[i][...] = v2

    vm = pl.BlockSpec(memory_space=pltpu.VMEM)
    shapes = [jax.ShapeDtypeStruct(w.shape, F32) for w in small_w]
    return pl.pallas_call(
        body, name="vec_allreduce_adam",
        out_shape=[jax.ShapeDtypeStruct((1, 128), F32)] + shapes * 4,
        in_specs=[vm] * (1 + 3 * n), out_specs=[vm] * (1 + 4 * n),
        scratch_shapes=[pltpu.VMEM((N_DEV, 8, D), F32), pltpu.SemaphoreType.DMA((N_DEV - 1,)),
                        pltpu.SemaphoreType.DMA((N_DEV - 1,))],
    )(vec, *small_w, *small_m, *small_v)


def kernel(x, w_in, lb_logits, hgrn_norm_g, w_a, w_pool, pool_scale, w_out, ln1_g, ln1_b, w_up, w_down, ln2_g, ln2_b, loss_target, m_w_in, m_lb_logits, m_hgrn_norm_g, m_w_a, m_w_pool, m_pool_scale, m_w_out, m_ln1_g, m_ln1_b, m_w_up, m_w_down, m_ln2_g, m_ln2_b, v_w_in, v_lb_logits, v_hgrn_norm_g, v_w_a, v_w_pool, v_pool_scale, v_w_out, v_ln1_g, v_ln1_b, v_w_up, v_w_down, v_ln2_g, v_ln2_b):
    Bl, S, D = x.shape
    T = Bl * S
    pg = D // POOL_GROUPS
    x2 = x.reshape(T, D)
    tgt = loss_target.reshape(T, D)
    me = 4 * lax.axis_index("x") + 2 * lax.axis_index("y") + lax.axis_index("c")
    me_arr = jnp.reshape(me, (1,)).astype(jnp.int32)

    names = ["w_in", "w_a", "w_pool", "w_out", "w_up", "w_down"]
    big_w = dict(zip(names, [w_in[0], w_a[0], w_pool[0], w_out[0], w_up[0], w_down[0]]))
    big_m = dict(zip(names, [m_w_in[0], m_w_a[0], m_w_pool[0], m_w_out[0], m_w_up[0], m_w_down[0]]))
    big_v = dict(zip(names, [v_w_in[0], v_w_a[0], v_w_pool[0], v_w_out[0], v_w_up[0], v_w_down[0]]))
    kinds = dict(w_in="col", w_a="row", w_pool="pool", w_out="row", w_up="col", w_down="row")
    lay = {nm: _Sharded(kinds[nm], big_w[nm].shape) for nm in names}
    wb = {nm: big_w[nm].astype(BF16) for nm in names}

    (w_in_f,) = _all_gather("ag_w_in", [wb["w_in"]], [lay["w_in"]])
    rest = names[1:]
    own_placed = [lax.dynamic_update_slice(lax.empty(lay[nm].full_shape, BF16), wb[nm], lay[nm].start_index(me))
                  for nm in rest]
    ag_rest = _exchange_start("ag_rest", [wb[nm] for nm in rest], own_placed,
                              src_at=lambda w, ref, peer: ref,
                              dst_at=lambda w, ref, mine, k: lay[rest[w]].at(ref, mine), after=w_in_f)

    proj, x_t = _proj(x2, w_in_f, ag_rest["token"])
    proj5 = proj.reshape(N_SEC, Bl, S, D)
    ain3, ain_t, st0_all = _hgrn_fwd(proj5, lb_logits, hgrn_norm_g)
    w_a_f, w_pool_f, w_out_f, w_up_f, w_down_f = _exchange_wait(ag_rest, ain3)
    pooled_t, bp3 = _pool_fwd(proj5, w_pool_f)
    ain, bp = ain3.reshape(T, D), bp3.reshape(T, D)
    a, merged_t, xhat1, rs1, x1b, x1_t = _mix_fwd(ain, proj, bp, x2, w_a_f, w_out_f, pool_scale, ln1_g, ln1_b)
    hp, h, dr2, dr2b, dr2_t, vec_mlp = _mlp_fwd(x1b, w_up_f, w_down_f, xhat1, tgt, ln1_g, ln1_b, ln2_g, ln2_b)

    def scatter_start(name, nms, grads_b, after):
        lands = [lax.empty((N_DEV - 1,) + lay[nm].shape, BF16) for nm in nms]
        return _exchange_start(name, grads_b, lands,
                               src_at=lambda w, ref, peer: lay[nms[w]].at(ref, peer),
                               dst_at=lambda w, ref, mine, k: ref.at[k - 1], after=after)

    dhp, dr1, dr1b, vec_ln1 = _mlp_bwd(dr2b, dr2, hp, w_up_f, w_down_f, xhat1, rs1, ln1_g)
    FF = 4 * D
    whole_t = _resident((D, T))
    gw, gwb = {}, {}
    gw["w_down"], gwb["w_down"] = _dw(
        "dw_down", dr2_t, h, FF // D, whole_t, pl.BlockSpec((T, D), lambda j: (0, j)),
        (FF, D), (D, D), lambda j: (j, 0), transpose_out=True)
    rs_down = scatter_start("rs_w_down", ["w_down"], [gwb["w_down"]], gw["w_down"])
    gw["w_up"], gwb["w_up"] = _dw(
        "dw_up", x1_t, dhp, FF // D, whole_t, pl.BlockSpec((T, D), lambda j: (0, j)),
        (D, FF), (D, D), lambda j: (0, j), dep=rs_down["token"])
    rs_up = scatter_start("rs_w_up", ["w_up"], [gwb["w_up"]], gw["w_up"])
    da_b, dbp_b, dain, dpooled, dg2, vec_mix = _mix_bwd(dr1b, proj, a, bp, w_a_f, w_out_f, w_pool_f, pool_scale,
                                                        rs_up["token"])
    whole_b = pl.BlockSpec((T, D), lambda j: (0, 0))
    gw["w_out"], gwb["w_out"] = _dw("dw_out", merged_t, dr1b, 1, whole_t, whole_b, (D, D), (D, D), lambda j: (0, 0))
    gw["w_a"], gwb["w_a"] = _dw("dw_a", ain_t, da_b, 1, whole_t, whole_b, (D, D), (D, D), lambda j: (0, 0))
    gw["w_pool"], gwb["w_pool"] = _dw(
        "dw_pool", pooled_t, dbp_b, POOL_GROUPS, pl.BlockSpec((pg, T), lambda j: (j, 0)),
        pl.BlockSpec((T, pg), lambda j: (0, j)), (POOL_GROUPS, pg, pg), (1, pg, pg), lambda j: (j, 0, 0))
    mid = ["w_out", "w_a", "w_pool"]
    rs_mid = scatter_start("rs_w_mid", mid, [gwb[nm] for nm in mid], gw["w_pool"])
    dpv = _pool_bwd(dpooled.reshape(Bl, S, D), rs_mid["token"]).reshape(T, D)
    dh4, vec_hgrn = _hgrn_bwd(proj5, lb_logits, hgrn_norm_g, dain.reshape(Bl, S, D), st0_all)
    dh4 = dh4.reshape(4, T, D)
    gw["w_in"], gwb["w_in"] = _dw_in(x_t, dh4, dpv, dg2, rs_mid["token"])
    rs_in = scatter_start("rs_w_in", ["w_in"], [gwb["w_in"]], gw["w_in"])
    grad_x2 = _dx(dr1, dh4, dpv, dg2, w_in_f, rs_in["token"])
    grad_x = grad_x2.reshape(Bl, S, D)

    vec = vec_mlp + vec_ln1 + vec_mix + vec_hgrn
    small_names = ["lb_logits", "hgrn_norm_g", "pool_scale", "ln1_g", "ln1_b", "ln2_g", "ln2_b"]
    small_w = [lb_logits, hgrn_norm_g, pool_scale, ln1_g, ln1_b, ln2_g, ln2_b]
    small_m = [m_lb_logits, m_hgrn_norm_g, m_pool_scale, m_ln1_g, m_ln1_b, m_ln2_g, m_ln2_b]
    small_v = [v_lb_logits, v_hgrn_norm_g, v_pool_scale, v_ln1_g, v_ln1_b, v_ln2_g, v_ln2_b]
    res = _vec_allreduce_adam(vec, small_w, small_m, small_v)
    loss = res[0][0, 0]
    n = len(small_w)
    small = {nm: (res[1 + i], res[1 + n + i], res[1 + 2 * n + i], res[1 + 3 * n + i])
             for i, nm in enumerate(small_names)}

    big, last = {}, grad_x2
    for pend, nms in ((rs_down, ["w_down"]), (rs_up, ["w_up"]), (rs_mid, mid), (rs_in, ["w_in"])):
        for nm, land in zip(nms, _exchange_wait(pend, last)):
            outs = _adam_shard("adam_" + nm, me_arr, gw[nm], land, lay[nm], big_w[nm], big_m[nm], big_v[nm])
            big[nm] = tuple(t[None] for t in outs)
            last = outs[0]

    order = ["w_in", "lb_logits", "hgrn_norm_g", "w_a", "w_pool", "pool_scale", "w_out", "ln1_g", "ln1_b",
             "w_up", "w_down", "ln2_g", "ln2_b"]
    allp = {**big, **small}
    out = [loss, grad_x]
    for part in range(4):
        out += [allp[nm][part] for nm in order]
    return tuple(out)
```

```python
import jax
import jax.numpy as jnp
from jax import lax
from jax.experimental import pallas as pl
from jax.experimental.pallas import tpu as pltpu

F32 = jnp.float32
BF16 = jnp.bfloat16
MESH = pl.DeviceIdType.MESH

N_DEV = 8
HEAD = 128
CHUNK = 16
SUBLANES = 8
GROUP = 128
CH_PER_GROUP = GROUP // CHUNK
N_SEC = 7
POOL_GROUPS = 4
ALPHA = (2.0 * 1) ** 0.25
LN_EPS = 1e-5
RMS_EPS = 1e-6
Q_SCALE = HEAD ** -0.5
ADAM_LR = 0.001
ADAM_B1 = 0.9
ADAM_B2 = 0.999
ADAM_EPS = 1e-08
ADAM_WD = 0.01
ADAM_STEP = 10
VMEM_LIMIT = 56 << 20

NT_DIMS = (((1,), (1,)), ((), ()))
TN_DIMS = (((0,), (0,)), ((), ()))


def _params(sem=None):
    kw = dict(vmem_limit_bytes=VMEM_LIMIT)
    if sem is not None:
        kw["dimension_semantics"] = sem
    return pltpu.CompilerParams(**kw)


def _me():
    return lax.axis_index("x"), lax.axis_index("y"), lax.axis_index("c")


def _sigmoid(v):
    return jax.nn.sigmoid(v)


def _adamw(w, g, m, v):
    m = ADAM_B1 * m + (1.0 - ADAM_B1) * g
    v = ADAM_B2 * v + (1.0 - ADAM_B2) * jnp.square(g)
    m_hat = m / (1.0 - ADAM_B1 ** ADAM_STEP)
    v_hat = v / (1.0 - ADAM_B2 ** ADAM_STEP)
    delta = -ADAM_LR * (m_hat / (jnp.sqrt(v_hat) + ADAM_EPS) + ADAM_WD * w)
    return delta, m, v


class _Sharded:
    def __init__(self, kind, shard_shape):
        self.kind, self.shape = kind, tuple(shard_shape)

    @property
    def full_shape(self):
        r = self.shape
        if self.kind == "row":
            return (N_DEV * r[0],) + r[1:]
        return (r[0], N_DEV * r[1]) + r[2:]

    def at(self, ref, d):
        if self.kind == "col":
            n = self.shape[1]
            return ref.at[:, pl.ds(pl.multiple_of(d * n, 128), n)]
        if self.kind == "row":
            n = self.shape[0]
            return ref.at[pl.ds(pl.multiple_of(d * n, 16), n), :]
        n = self.shape[1]
        return ref.at[:, pl.ds(pl.multiple_of(d * n, 16), n), :]

    def block_index(self, d):
        return {"col": (0, d), "row": (d, 0), "pool": (0, d, 0)}[self.kind]


def _peer(k, x, y, c):
    return (1 - x if k & 4 else x, 1 - y if k & 2 else y, 1 - c if k & 1 else c)


def _all_gather(name, shards, layouts, placed=(), placed_layouts=()):
    nw, npl = len(shards), len(placed)

    def body(*refs):
        ins, place_ins = refs[:nw], refs[nw:nw + npl]
        outs, place_outs = refs[nw + npl:2 * nw + npl], refs[2 * nw + npl:2 * (nw + npl)]
        send_sems, recv_sems, local_sems = refs[2 * (nw + npl):]
        x, y, c = _me()
        me = (x, y, c)
        sibling = (x, y, 1 - c)
        chips = [(1 - x, y), (x, 1 - y), (1 - x, 1 - y)]

        def copy(w, k, block, to, src=None):
            px, py, pc = block
            dst = layouts[w].at(outs[w], 4 * px + 2 * py + pc)
            return pltpu.make_async_remote_copy(
                src_ref=dst if src is None else src, dst_ref=dst,
                send_sem=send_sems.at[w, k], recv_sem=recv_sems.at[w, k],
                device_id=to, device_id_type=MESH)

        local = []
        for w in range(nw + npl):
            src, dst, lay_w = ((ins[w], outs[w], layouts[w]) if w < nw else
                               (place_ins[w - nw], place_outs[w - nw], placed_layouts[w - nw]))
            mine = pltpu.make_async_copy(src, lay_w.at(dst, 4 * x + 2 * y + c), local_sems.at[w])
            mine.start()
            local.append(mine)
        first = []
        for w in range(nw):
            first.append(copy(w, 0, me, sibling, src=ins[w]))
            first += [copy(w, 1 + j, me, (*chip, c), src=ins[w]) for j, chip in enumerate(chips)]
        for cp in first:
            cp.start()
        passed = []
        for w in range(nw):
            for j, chip in enumerate(chips):
                copy(w, 1 + j, (*chip, c), me).wait_recv()
                fwd = copy(w, 4 + j, (*chip, c), sibling)
                fwd.start()
                passed.append(fwd)
        for w in range(nw):
            copy(w, 0, sibling, me).wait_recv()
            for j, chip in enumerate(chips):
                copy(w, 4 + j, (*chip, 1 - c), me).wait_recv()
        for cp in first + passed:
            cp.wait_send()
        for cp in local:
            cp.wait()

    any_spec = pl.BlockSpec(memory_space=pl.ANY)
    return pl.pallas_call(
        body, name=name,
        out_shape=[jax.ShapeDtypeStruct(l.full_shape, s.dtype)
                   for s, l in zip(list(shards) + list(placed), list(layouts) + list(placed_layouts))],
        in_specs=[any_spec] * (nw + npl), out_specs=[any_spec] * (nw + npl),
        scratch_shapes=[pltpu.SemaphoreType.DMA((nw, 7)), pltpu.SemaphoreType.DMA((nw, 7)),
                        pltpu.SemaphoreType.DMA((nw + npl,))],
    )(*shards, *placed)


HBM_SPEC = pl.BlockSpec(memory_space=pltpu.HBM)
SEM_SPEC = pl.BlockSpec(memory_space=pltpu.SEMAPHORE)
DATAFLOW = pltpu.SideEffectType.DATAFLOW_SIDE_EFFECTING


def _exchange_copies(srcs, lands, send_sems, recv_sems, src_at, dst_at):
    x, y, c = _me()
    me = 4 * x + 2 * y + c
    copies = []
    for w in range(len(srcs)):
        for k in range(1, N_DEV):
            px, py, pc = _peer(k, x, y, c)
            copies.append(pltpu.make_async_remote_copy(
                src_ref=src_at(w, srcs[w], 4 * px + 2 * py + pc), dst_ref=dst_at(w, lands[w], me, k),
                send_sem=send_sems.at[w * (N_DEV - 1) + k - 1], recv_sem=recv_sems.at[w * (N_DEV - 1) + k - 1],
                device_id=(px, py, pc), device_id_type=MESH))
    return copies


def _exchange_start(name, srcs, lands, src_at, dst_at, after):
    nw = len(srcs)

    def body(*refs):
        src_refs, land_refs = refs[:nw], refs[nw:2 * nw]
        send_sems, recv_sems = refs[2 * nw + 1], refs[2 * nw + 2]
        token = refs[-1]
        for cp in _exchange_copies(src_refs, land_refs, send_sems, recv_sems, src_at, dst_at):
            cp.start()
        token[...] = jnp.zeros_like(token)

    hbm = lambda a: pltpu.HBM(a.shape, a.dtype)
    outs = pl.pallas_call(
        body, name=name,
        out_shape=(pltpu.SemaphoreType.DMA((nw * (N_DEV - 1),)), pltpu.SemaphoreType.DMA((nw * (N_DEV - 1),)),
                   *[hbm(a) for a in srcs], *[hbm(a) for a in lands],
                   jax.ShapeDtypeStruct((8, 128), F32)),
        in_specs=[HBM_SPEC] * (2 * nw) + [pl.BlockSpec(memory_space=pl.ANY)],
        out_specs=(SEM_SPEC, SEM_SPEC, *[HBM_SPEC] * (2 * nw), pl.BlockSpec(memory_space=pltpu.VMEM)),
        input_output_aliases={i: 2 + i for i in range(2 * nw)},
        compiler_params=pltpu.CompilerParams(has_side_effects=DATAFLOW),
    )(*[pltpu.with_memory_space_constraint(a, pltpu.HBM) for a in list(srcs) + list(lands)], after)
    return dict(send=outs[0], recv=outs[1], srcs=outs[2:2 + nw], lands=outs[2 + nw:2 + 2 * nw],
                token=outs[-1], src_at=src_at, dst_at=dst_at, name=name)


def _exchange_wait(pending, after):
    nw = len(pending["srcs"])

    def body(*refs):
        src_refs, land_refs = refs[:nw], refs[nw:2 * nw]
        send_sems, recv_sems = refs[2 * nw], refs[2 * nw + 1]
        for cp in _exchange_copies(src_refs, land_refs, send_sems, recv_sems,
                                   pending["src_at"], pending["dst_at"]):
            cp.wait_send()
            cp.wait_recv()

    hbm = lambda a: pltpu.HBM(a.shape, a.dtype)
    outs = pl.pallas_call(
        body, name=pending["name"] + "_wait",
        out_shape=(*[hbm(a) for a in pending["srcs"]], *[hbm(a) for a in pending["lands"]]),
        in_specs=[HBM_SPEC] * (2 * nw) + [SEM_SPEC, SEM_SPEC, pl.BlockSpec(memory_space=pl.ANY)],
        out_specs=tuple([HBM_SPEC] * (2 * nw)),
        input_output_aliases={i: i for i in range(2 * nw)},
        compiler_params=pltpu.CompilerParams(has_side_effects=DATAFLOW),
    )(*pending["srcs"], *pending["lands"], pending["send"], pending["recv"], after)
    return outs[nw:]


def _call_after(dep, body, args, *, in_specs, **kw):
    n_in = len(args)

    def wrapped(*refs):
        body(*refs[:n_in], *refs[n_in + 1:])

    dep_spec = pl.BlockSpec(dep.shape, lambda *_: (0,) * dep.ndim)
    return pl.pallas_call(wrapped, in_specs=list(in_specs) + [dep_spec], **kw)(*args, dep)


def _resident(shape):
    return pl.BlockSpec(shape, lambda *_: (0,) * len(shape), pipeline_mode=pl.Buffered(1))


def _proj(x2, w_in, dep):
    T, D = x2.shape
    tm = min(256, T)

    def body(x_ref, w_ref, o_ref, xt_ref):
        x = x_ref[...]
        xt_ref[...] = x.T.astype(BF16)
        xb = x.astype(BF16)
        for j in range(N_SEC):
            o_ref[j] = jnp.dot(xb, w_ref[:, j * D:(j + 1) * D], preferred_element_type=F32)

    return _call_after(
        dep, body, (x2, w_in), name="proj", grid=(T // tm,),
        in_specs=[pl.BlockSpec((tm, D), lambda i: (i, 0)), _resident((D, N_SEC * D))],
        out_specs=[pl.BlockSpec((N_SEC, tm, D), lambda i: (0, i, 0)), pl.BlockSpec((D, tm), lambda i: (0, i))],
        out_shape=[jax.ShapeDtypeStruct((N_SEC, T, D), F32), jax.ShapeDtypeStruct((D, T), BF16)],
        compiler_params=_params(("parallel",)))


def _chunk_cumsum(v, reverse=False):
    rows, lanes = v.shape
    x = v.reshape(rows // SUBLANES, SUBLANES, lanes)
    pos = lax.broadcasted_iota(jnp.int32, x.shape, 1)
    for sh in (1, 2, 4):
        if reverse:
            x = x + jnp.where(pos < SUBLANES - sh, pltpu.roll(x, SUBLANES - sh, 1), 0.0)
        else:
            x = x + jnp.where(pos >= sh, pltpu.roll(x, sh, 1), 0.0)
    x = x.reshape(rows // CHUNK, CHUNK // SUBLANES, SUBLANES, lanes)
    half = lax.broadcasted_iota(jnp.int32, x.shape, 1)
    if reverse:
        x = x + jnp.where(half == 0, x[:, 1:2, 0:1, :], 0.0)
    else:
        x = x + jnp.where(half == 1, x[:, 0:1, SUBLANES - 1:SUBLANES, :], 0.0)
    return x.reshape(rows, lanes)


def _hgrn_gates(q, f_pre, lb_logits):
    l0, l1 = lb_logits[0:1, :], lb_logits[1:2, :]
    mx = jnp.maximum(l0, l1)
    e0, e1 = jnp.exp(l0 - mx), jnp.exp(l1 - mx)
    lb = e0 / (e0 + e1)
    sq = _sigmoid(q)
    qf = q * sq * Q_SCALE
    sg = _sigmoid(f_pre)
    f = lb + (1.0 - lb) * sg
    k = 1.0 - f
    log_f = jnp.log(f)
    G = _chunk_cumsum(log_f)
    g_to_end = _chunk_cumsum(log_f, reverse=True) - log_f
    e_g = jnp.exp(G)
    e_ng = jnp.exp(-G)
    e_ge = jnp.exp(g_to_end)
    return dict(lb=lb, sq=sq, qf=qf, sg=sg, f=f, k=k, G=G, e_g=e_g, e_ng=e_ng, e_ge=e_ge,
                qd=qf * e_g, ki=k * e_ng, ke=k * e_ge, dec=jnp.exp(G + g_to_end))


def _intra_mask():
    r = lax.broadcasted_iota(jnp.int32, (GROUP, GROUP), 0)
    c = lax.broadcasted_iota(jnp.int32, (GROUP, GROUP), 1)
    return (r // CHUNK == c // CHUNK) & (c <= r)


def _chunk_outer(lhs_rows, rhs_b, out_scr, sb):
    lane = lax.broadcasted_iota(jnp.int32, (GROUP, GROUP), 1) // CHUNK
    for g in range(sb // GROUP):
        sl = slice(g * GROUP, (g + 1) * GROUP)
        lhs_t = lhs_rows[sl].T
        for cc in range(CH_PER_GROUP):
            masked = jnp.where(lane == cc, lhs_t, 0.0).astype(BF16)
            out_scr[g * CH_PER_GROUP + cc] = jnp.dot(masked, rhs_b[sl], preferred_element_type=F32)


def _hgrn_forward_block(c, v, st0, sb, o_scr, kv_scr, st_scr, dec_scr):
    nc = sb // CHUNK
    qd_b, ki_b, ke_b, v_b = (c["qd"].astype(BF16), c["ki"].astype(BF16), c["ke"].astype(BF16),
                             v.astype(BF16))
    mask = _intra_mask()
    for g in range(sb // GROUP):
        sl = slice(g * GROUP, (g + 1) * GROUP)
        sc = lax.dot_general(qd_b[sl], ki_b[sl], NT_DIMS, preferred_element_type=F32)
        a = jnp.where(mask, sc, 0.0).astype(BF16)
        o_scr[sl, :] = jnp.dot(a, v_b[sl], preferred_element_type=F32)
    _chunk_outer(v, ke_b, kv_scr, sb)
    dec_scr[...] = c["dec"]

    def rec(n, st):
        st_scr[n] = st
        d = dec_scr[pl.ds(pl.multiple_of(n * CHUNK, CHUNK), 1), :]
        return st * d + kv_scr[n]

    st_end = lax.fori_loop(0, nc, rec, st0)

    for n in range(nc):
        rows = slice(n * CHUNK, (n + 1) * CHUNK)
        o_scr[rows, :] += lax.dot_general(qd_b[rows], st_scr[n].astype(BF16), NT_DIMS,
                                          preferred_element_type=F32)
    return st_end


def _hgrn_fwd(proj5, lb_logits, gn):
    _, Bl, S, D = proj5.shape
    H = D // HEAD
    sb = min(512, S)
    nsb = S // sb
    nc = sb // CHUNK

    def body(p_ref, lbl_ref, gn_ref, ain_ref, aint_ref, st0_ref, carry, o_scr, kv_scr, st_scr, dec_scr):
        s = pl.program_id(2)

        @pl.when(s == 0)
        def _():
            carry[...] = jnp.zeros_like(carry)

        st0 = carry[...]
        st0_ref[0, 0, 0] = st0
        c = _hgrn_gates(p_ref[0, 0], p_ref[1, 0], lbl_ref[...])
        carry[...] = _hgrn_forward_block(c, p_ref[2, 0], st0, sb, o_scr, kv_scr, st_scr, dec_scr)
        o = o_scr[...]
        rinv = lax.rsqrt(jnp.mean(o * o, axis=-1, keepdims=True) + RMS_EPS)
        ain = o * rinv * gn_ref[...] * _sigmoid(p_ref[3, 0])
        ain_ref[0] = ain.astype(BF16)
        aint_ref[...] = ain.T.astype(BF16)

    return pl.pallas_call(
        body, name="hgrn_fwd", grid=(H, Bl, nsb),
        in_specs=[pl.BlockSpec((4, 1, sb, HEAD), lambda h, b, s: (0, b, s, h)),
                  pl.BlockSpec((2, HEAD), lambda h, b, s: (0, h)),
                  pl.BlockSpec((1, HEAD), lambda h, b, s: (0, h))],
        out_specs=[pl.BlockSpec((1, sb, HEAD), lambda h, b, s: (b, s, h)),
                   pl.BlockSpec((HEAD, sb), lambda h, b, s: (h, b * nsb + s)),
                   pl.BlockSpec((1, 1, 1, HEAD, HEAD), lambda h, b, s: (b, h, s, 0, 0))],
        out_shape=[jax.ShapeDtypeStruct((Bl, S, D), BF16), jax.ShapeDtypeStruct((D, Bl * S), BF16),
                   jax.ShapeDtypeStruct((Bl, H, nsb, HEAD, HEAD), F32)],
        scratch_shapes=[pltpu.VMEM((HEAD, HEAD), F32), pltpu.VMEM((sb, HEAD), F32),
                        pltpu.VMEM((nc, HEAD, HEAD), F32), pltpu.VMEM((nc, HEAD, HEAD), F32),
                        pltpu.VMEM((sb, HEAD), F32)],
        compiler_params=_params(("parallel", "parallel", "arbitrary")),
    )(proj5, lb_logits, gn)


def _window_count(shape, g):
    pos = lax.broadcasted_iota(jnp.int32, shape, 0)
    return pos, jnp.minimum(pos + 1, jnp.left_shift(2, g)).astype(F32)


def _select_window(g, sums):
    return jnp.where(g == 0, sums[0], jnp.where(g == 1, sums[1], jnp.where(g == 2, sums[2], sums[3])))


def _pool_fwd(proj5, w_pool):
    _, Bl, S, D = proj5.shape
    pg = D // POOL_GROUPS

    def body(v_ref, w_ref, pooled_t_ref, bp_ref):
        g = pl.program_id(1)
        v = v_ref[0, 0]
        pos, cnt = _window_count(v.shape, g)
        cur, sums = v, []
        for sh in (1, 2, 4, 8):
            cur = cur + jnp.where(pos >= sh, pltpu.roll(cur, sh, 0), 0.0)
            sums.append(cur)
        pooled = _select_window(g, sums) / cnt - v
        pooled_t_ref[...] = pooled.T.astype(BF16)
        bp_ref[0] = jnp.dot(pooled.astype(BF16), w_ref[0], preferred_element_type=F32)

    return pl.pallas_call(
        body, name="pool_fwd", grid=(Bl, POOL_GROUPS),
        in_specs=[pl.BlockSpec((1, 1, S, pg), lambda b, g: (4, b, 0, g)),
                  pl.BlockSpec((1, pg, pg), lambda b, g: (g, 0, 0))],
        out_specs=[pl.BlockSpec((pg, S), lambda b, g: (g, b)),
                   pl.BlockSpec((1, S, pg), lambda b, g: (b, 0, g))],
        out_shape=[jax.ShapeDtypeStruct((D, Bl * S), BF16), jax.ShapeDtypeStruct((Bl, S, D), F32)],
        compiler_params=_params(("parallel", "parallel")),
    )(proj5, w_pool)


def _layer_norm_fwd(r):
    mu = jnp.mean(r, axis=-1, keepdims=True)
    d = r - mu
    rs = lax.rsqrt(jnp.mean(d * d, axis=-1, keepdims=True) + LN_EPS)
    return d * rs, rs


def _layer_norm_bwd(dy_g, xhat, rs):
    return rs * (dy_g - jnp.mean(dy_g, axis=-1, keepdims=True)
                 - xhat * jnp.mean(dy_g * xhat, axis=-1, keepdims=True))


def _mix_fwd(ain, proj, bp, x2, w_a, w_out, ps, g1, b1):
    T, D = x2.shape
    tm = min(256, T)

    def body(ain_ref, ga_ref, gb_ref, bp_ref, x_ref, wa_ref, wo_ref, ps_ref, g1_ref, b1_ref,
             a_ref, mgt_ref, xh_ref, rs_ref, x1b_ref, x1t_ref):
        a = jnp.dot(ain_ref[...], wa_ref[...], preferred_element_type=F32)
        a_ref[...] = a
        merged = _sigmoid(ga_ref[0]) * a + _sigmoid(gb_ref[0]) * (bp_ref[...] * ps_ref[...])
        mgt_ref[...] = merged.T.astype(BF16)
        r1 = ALPHA * x_ref[...] + jnp.dot(merged.astype(BF16), wo_ref[...], preferred_element_type=F32)
        xhat, rs = _layer_norm_fwd(r1)
        xh_ref[...] = xhat
        rs_ref[...] = rs
        x1 = xhat * g1_ref[...] + b1_ref[...]
        x1b_ref[...] = x1.astype(BF16)
        x1t_ref[...] = x1.T.astype(BF16)

    row = lambda i: (i, 0)
    col = lambda i: (0, i)
    full = lambda i: (0, 0)
    return pl.pallas_call(
        body, name="mix_fwd", grid=(T // tm,),
        in_specs=[pl.BlockSpec((tm, D), row),
                  pl.BlockSpec((1, tm, D), lambda i: (5, i, 0)),
                  pl.BlockSpec((1, tm, D), lambda i: (6, i, 0)),
                  pl.BlockSpec((tm, D), row), pl.BlockSpec((tm, D), row),
                  pl.BlockSpec((D, D), full), pl.BlockSpec((D, D), full),
                  pl.BlockSpec((1, D), full), pl.BlockSpec((1, D), full), pl.BlockSpec((1, D), full)],
        out_specs=[pl.BlockSpec((tm, D), row), pl.BlockSpec((D, tm), col), pl.BlockSpec((tm, D), row),
                   pl.BlockSpec((tm, 1), row), pl.BlockSpec((tm, D), row), pl.BlockSpec((D, tm), col)],
        out_shape=[jax.ShapeDtypeStruct((T, D), F32), jax.ShapeDtypeStruct((D, T), BF16),
                   jax.ShapeDtypeStruct((T, D), F32), jax.ShapeDtypeStruct((T, 1), F32),
                   jax.ShapeDtypeStruct((T, D), BF16), jax.ShapeDtypeStruct((D, T), BF16)],
        compiler_params=_params(("parallel",)),
    )(ain, proj, proj, bp, x2, w_a, w_out, ps, g1, b1)


def _mlp_fwd(x1b, w_up, w_down, xhat1, tgt, g1, b1, g2, b2):
    T, D = xhat1.shape
    FF = w_up.shape[1]
    tm = min(256, T)

    def body(x_ref, wu_ref, wd_ref, xh_ref, t_ref, g1_ref, b1_ref, g2_ref, b2_ref,
             hp_ref, h_ref, dr_ref, drb_ref, drt_ref, vec_ref):
        @pl.when(pl.program_id(0) == 0)
        def _():
            vec_ref[...] = jnp.zeros_like(vec_ref)

        xb = x_ref[...]
        x1 = xh_ref[...] * g1_ref[...] + b1_ref[...]
        r2 = ALPHA * x1
        for f in range(FF // D):
            cols = slice(f * D, (f + 1) * D)
            hp = jnp.dot(xb, wu_ref[:, cols], preferred_element_type=F32)
            hp_ref[:, cols] = hp
            h = jnp.square(jnp.maximum(hp, 0.0)).astype(BF16)
            h_ref[:, cols] = h
            r2 = r2 + jnp.dot(h, wd_ref[cols, :], preferred_element_type=F32)
        xhat2, rs2 = _layer_norm_fwd(r2)
        err = xhat2 * g2_ref[...] + b2_ref[...] - t_ref[...]
        dy = err / D
        vec_ref[5:6, :] += jnp.sum(dy * xhat2, axis=0, keepdims=True)
        vec_ref[6:7, :] += jnp.sum(dy, axis=0, keepdims=True)
        vec_ref[7:8, :] += jnp.sum(0.5 * err * err / D, axis=0, keepdims=True)
        dr = _layer_norm_bwd(dy * g2_ref[...], xhat2, rs2)
        dr_ref[...] = dr
        drb_ref[...] = dr.astype(BF16)
        drt_ref[...] = dr.T.astype(BF16)

    row = lambda i: (i, 0)
    full = lambda i: (0, 0)
    return pl.pallas_call(
        body, name="mlp_fwd", grid=(T // tm,),
        in_specs=[pl.BlockSpec((tm, D), row), _resident((D, FF)), _resident((FF, D)),
                  pl.BlockSpec((tm, D), row), pl.BlockSpec((tm, D), row),
                  pl.BlockSpec((1, D), full), pl.BlockSpec((1, D), full),
                  pl.BlockSpec((1, D), full), pl.BlockSpec((1, D), full)],
        out_specs=[pl.BlockSpec((tm, FF), row), pl.BlockSpec((tm, FF), row), pl.BlockSpec((tm, D), row),
                   pl.BlockSpec((tm, D), row), pl.BlockSpec((D, tm), lambda i: (0, i)),
                   pl.BlockSpec((8, D), full)],
        out_shape=[jax.ShapeDtypeStruct((T, FF), F32), jax.ShapeDtypeStruct((T, FF), BF16),
                   jax.ShapeDtypeStruct((T, D), F32), jax.ShapeDtypeStruct((T, D), BF16),
                   jax.ShapeDtypeStruct((D, T), BF16), jax.ShapeDtypeStruct((8, D), F32)],
        compiler_params=_params(("arbitrary",)),
    )(x1b, w_up, w_down, xhat1, tgt, g1, b1, g2, b2)


def _mlp_bwd(drb, dr, hp, w_up, w_down, xhat1, rs1, g1):
    T, D = dr.shape
    FF = hp.shape[1]
    tm = min(256, T)

    def body(drb_ref, dr_ref, hp_ref, wu_ref, wd_ref, xh_ref, rs_ref, g1_ref,
             dhp_ref, d1_ref, d1b_ref, vec_ref):
        @pl.when(pl.program_id(0) == 0)
        def _():
            vec_ref[...] = jnp.zeros_like(vec_ref)

        drb = drb_ref[...]
        dx1 = ALPHA * dr_ref[...]
        for f in range(FF // D):
            cols = slice(f * D, (f + 1) * D)
            dh = lax.dot_general(drb, wd_ref[cols, :], NT_DIMS, preferred_element_type=F32)
            dhp = (dh * (2.0 * jnp.maximum(hp_ref[:, cols], 0.0))).astype(BF16)
            dhp_ref[:, cols] = dhp
            dx1 = dx1 + lax.dot_general(dhp, wu_ref[:, cols], NT_DIMS, preferred_element_type=F32)
        xhat = xh_ref[...]
        vec_ref[3:4, :] += jnp.sum(dx1 * xhat, axis=0, keepdims=True)
        vec_ref[4:5, :] += jnp.sum(dx1, axis=0, keepdims=True)
        d1 = _layer_norm_bwd(dx1 * g1_ref[...], xhat, rs_ref[...])
        d1_ref[...] = d1
        d1b_ref[...] = d1.astype(BF16)

    row = lambda i: (i, 0)
    full = lambda i: (0, 0)
    return pl.pallas_call(
        body, name="mlp_bwd", grid=(T // tm,),
        in_specs=[pl.BlockSpec((tm, D), row), pl.BlockSpec((tm, D), row), pl.BlockSpec((tm, FF), row),
                  _resident((D, FF)), _resident((FF, D)),
                  pl.BlockSpec((tm, D), row), pl.BlockSpec((tm, 1), row), pl.BlockSpec((1, D), full)],
        out_specs=[pl.BlockSpec((tm, FF), row), pl.BlockSpec((tm, D), row), pl.BlockSpec((tm, D), row),
                   pl.BlockSpec((8, D), full)],
        out_shape=[jax.ShapeDtypeStruct((T, FF), BF16), jax.ShapeDtypeStruct((T, D), F32),
                   jax.ShapeDtypeStruct((T, D), BF16), jax.ShapeDtypeStruct((8, D), F32)],
        compiler_params=_params(("arbitrary",)),
    )(drb, dr, hp, w_up, w_down, xhat1, rs1, g1)


def _dw(name, a_t, b, n_j, a_spec, b_spec, o_shape, o_block, o_map, transpose_out=False, dep=None,
        into=None):
    def body(*refs):
        a_ref, b_ref, o_ref, ob_ref = refs[0], refs[1], refs[-2], refs[-1]
        b_val = b_ref[0] if len(b_ref.shape) == 3 else b_ref[...]
        p = jnp.dot(a_ref[...], b_val, preferred_element_type=F32)
        if transpose_out:
            p = p.T
        p = p.reshape(o_ref.shape)
        o_ref[...] = p
        ob_ref[...] = p.astype(BF16)

    o_spec = pl.BlockSpec(o_block, o_map)
    kw = dict(name=name, grid=(n_j,), in_specs=[a_spec, b_spec], out_specs=[o_spec, o_spec],
              out_shape=[jax.ShapeDtypeStruct(o_shape, F32), jax.ShapeDtypeStruct(o_shape, BF16)],
              compiler_params=_params(("parallel",)))
    args = (a_t, b)
    if into is not None:
        kw["in_specs"] = kw["in_specs"] + [pl.BlockSpec(memory_space=pl.ANY)] * 2
        kw["input_output_aliases"] = {2: 0, 3: 1}
        args = args + tuple(into)
    if dep is None:
        return pl.pallas_call(body, **kw)(*args)
    return _call_after(dep, body, args, **kw)


def _mix_bwd(d1b, proj, a, bp, w_a, w_out, w_pool, ps, dep):
    T, D = a.shape
    tm = min(256, T)
    pg = D // POOL_GROUPS

    def body(d1b_ref, ga_ref, gb_ref, a_ref, bp_ref, wa_ref, wo_ref, wp_ref, ps_ref,
             da_ref, dbp_ref, dain_ref, dpl_ref, dg_ref, vec_ref):
        @pl.when(pl.program_id(0) == 0)
        def _():
            vec_ref[...] = jnp.zeros_like(vec_ref)

        dm = lax.dot_general(d1b_ref[...], wo_ref[...], NT_DIMS, preferred_element_type=F32)
        sa, sg = _sigmoid(ga_ref[0]), _sigmoid(gb_ref[0])
        bp_v, ps_v = bp_ref[...], ps_ref[...]
        da = (dm * sa).astype(BF16)
        db = dm * sg
        dg_ref[0] = (dm * a_ref[...] * sa * (1.0 - sa)).astype(BF16)
        dg_ref[1] = (dm * (bp_v * ps_v) * sg * (1.0 - sg)).astype(BF16)
        vec_ref[2:3, :] += jnp.sum(db * bp_v, axis=0, keepdims=True)
        dbp = (db * ps_v).astype(BF16)
        da_ref[...] = da
        dbp_ref[...] = dbp
        dain_ref[...] = lax.dot_general(da, wa_ref[...], NT_DIMS, preferred_element_type=F32)
        for g in range(POOL_GROUPS):
            cols = slice(g * pg, (g + 1) * pg)
            dpl_ref[:, cols] = lax.dot_general(dbp[:, cols], wp_ref[g], NT_DIMS,
                                               preferred_element_type=F32)

    row = lambda i: (i, 0)
    full = lambda i: (0, 0)
    return _call_after(
        dep, body, (d1b, proj, proj, a, bp, w_a, w_out, w_pool, ps), name="mix_bwd", grid=(T // tm,),
        in_specs=[pl.BlockSpec((tm, D), row),
                  pl.BlockSpec((1, tm, D), lambda i: (5, i, 0)),
                  pl.BlockSpec((1, tm, D), lambda i: (6, i, 0)),
                  pl.BlockSpec((tm, D), row), pl.BlockSpec((tm, D), row),
                  pl.BlockSpec((D, D), full), pl.BlockSpec((D, D), full),
                  pl.BlockSpec((POOL_GROUPS, pg, pg), lambda i: (0, 0, 0)),
                  pl.BlockSpec((1, D), full)],
        out_specs=[pl.BlockSpec((tm, D), row), pl.BlockSpec((tm, D), row),
                   pl.BlockSpec((tm, D), row), pl.BlockSpec((tm, D), row),
                   pl.BlockSpec((2, tm, D), lambda i: (0, i, 0)),
                   pl.BlockSpec((8, D), full)],
        out_shape=[jax.ShapeDtypeStruct((T, D), BF16), jax.ShapeDtypeStruct((T, D), BF16),
                   jax.ShapeDtypeStruct((T, D), F32), jax.ShapeDtypeStruct((T, D), F32),
                   jax.ShapeDtypeStruct((2, T, D), BF16), jax.ShapeDtypeStruct((8, D), F32)],
        compiler_params=_params(("arbitrary",)))


def _pool_bwd(dpooled3, dep):
    Bl, S, D = dpooled3.shape
    pg = D // POOL_GROUPS

    def body(dp_ref, dv_ref):
        g = pl.program_id(1)
        dp = dp_ref[0]
        pos, cnt = _window_count(dp.shape, g)
        cur, sums = dp / cnt, []
        for sh in (1, 2, 4, 8):
            cur = cur + jnp.where(pos < S - sh, pltpu.roll(cur, S - sh, 0), 0.0)
            sums.append(cur)
        dv_ref[0] = (_select_window(g, sums) - dp).astype(BF16)

    spec = pl.BlockSpec((1, S, pg), lambda b, g: (b, 0, g))
    return _call_after(
        dep, body, (dpooled3,), name="pool_bwd", grid=(Bl, POOL_GROUPS), in_specs=[spec], out_specs=spec,
        out_shape=jax.ShapeDtypeStruct((Bl, S, D), BF16),
        compiler_params=_params(("parallel", "parallel")))


def _hgrn_bwd(proj5, lb_logits, gn, dain3, st0_all):
    _, Bl, S, D = proj5.shape
    H = D // HEAD
    sb = min(512, S)
    nsb = S // sb
    nc = sb // CHUNK

    def body(p_ref, lbl_ref, gn_ref, dain_ref, st0_ref, d_ref, vec_ref,
             dcarry, o_scr, kv_scr, st_scr, dst_scr, dec_scr, dvi_scr, dke_scr, dqi_scr):
        b, s = pl.program_id(1), pl.program_id(2)

        @pl.when(s == 0)
        def _():
            dcarry[...] = jnp.zeros_like(dcarry)

        @pl.when((b == 0) & (s == 0))
        def _():
            vec_ref[...] = jnp.zeros_like(vec_ref)

        q, f_pre, v, og = p_ref[0, 0], p_ref[1, 0], p_ref[2, 0], p_ref[3, 0]
        c = _hgrn_gates(q, f_pre, lbl_ref[...])
        _hgrn_forward_block(c, v, st0_ref[0, 0, 0], sb, o_scr, kv_scr, st_scr, dec_scr)
        qd_b, ki_b, ke_b, v_b = (c["qd"].astype(BF16), c["ki"].astype(BF16),
                                 c["ke"].astype(BF16), v.astype(BF16))

        o = o_scr[...]
        rinv = lax.rsqrt(jnp.mean(o * o, axis=-1, keepdims=True) + RMS_EPS)
        on = o * rinv
        so = _sigmoid(og)
        gn_v = gn_ref[...]
        dain = dain_ref[0]
        vec_ref[1:2, :] += jnp.sum(dain * on * so, axis=0, keepdims=True)
        d_og = dain * on * gn_v * so * (1.0 - so)
        d_on = dain * gn_v * so
        do = rinv * (d_on - on * jnp.mean(d_on * on, axis=-1, keepdims=True))
        do_b = do.astype(BF16)

        mask = _intra_mask()
        dv_parts, dqd_parts, dki_parts = [], [], []
        for g in range(sb // GROUP):
            sl = slice(g * GROUP, (g + 1) * GROUP)
            sc = lax.dot_general(qd_b[sl], ki_b[sl], NT_DIMS, preferred_element_type=F32)
            a = jnp.where(mask, sc, 0.0).astype(BF16)
            da = lax.dot_general(do_b[sl], v_b[sl], NT_DIMS, preferred_element_type=F32)
            da = jnp.where(mask, da, 0.0).astype(BF16)
            dv_parts.append(lax.dot_general(a, do_b[sl], TN_DIMS, preferred_element_type=F32))
            dqd_parts.append(jnp.dot(da, ki_b[sl], preferred_element_type=F32))
            dki_parts.append(lax.dot_general(da, qd_b[sl], TN_DIMS, preferred_element_type=F32))
        dv_intra = jnp.concatenate(dv_parts, axis=0)
        dqd_intra = jnp.concatenate(dqd_parts, axis=0)
        dki = jnp.concatenate(dki_parts, axis=0)

        _chunk_outer(do, qd_b, kv_scr, sb)

        def rrec(i, dst):
            n = nc - 1 - i
            dst_scr[n] = dst
            d = dec_scr[pl.ds(pl.multiple_of(n * CHUNK, CHUNK), 1), :]
            return dst * d + kv_scr[n]

        dcarry[...] = lax.fori_loop(0, nc, rrec, dcarry[...])
        for n in range(nc):
            rows = slice(n * CHUNK, (n + 1) * CHUNK)
            dst_b = dst_scr[n].astype(BF16)
            dvi_scr[rows, :] = lax.dot_general(ke_b[rows], dst_b, NT_DIMS, preferred_element_type=F32)
            dke_scr[rows, :] = jnp.dot(v_b[rows], dst_b, preferred_element_type=F32)
            dqi_scr[rows, :] = jnp.dot(do_b[rows], st_scr[n].astype(BF16), preferred_element_type=F32)
        ddec = jnp.sum(dst_scr[...] * st_scr[...], axis=1)
        dgl = jnp.broadcast_to(ddec[:, None, :], (nc, CHUNK, HEAD)).reshape(sb, HEAD) * c["dec"]

        dqd = dqd_intra + dqi_scr[...]
        dke = dke_scr[...]
        t_ke = dke * c["ke"]
        dG = dqd * c["qd"] - dki * c["ki"] - t_ke
        dgl = dgl + _chunk_cumsum(t_ke) + _chunk_cumsum(t_ke, reverse=True) - t_ke
        dlogf = _chunk_cumsum(dG, reverse=True) + dgl
        dk = dki * c["e_ng"] + dke * c["e_ge"]
        df = dlogf / c["f"] - dk
        sg, sq, lb = c["sg"], c["sq"], c["lb"]
        vec_ref[0:1, :] += jnp.sum(df * (1.0 - sg), axis=0, keepdims=True)
        d_ref[0, 0] = (dqd * c["e_g"] * Q_SCALE * (sq + q * sq * (1.0 - sq))).astype(BF16)
        d_ref[1, 0] = (df * (1.0 - lb) * sg * (1.0 - sg)).astype(BF16)
        d_ref[2, 0] = (dv_intra + dvi_scr[...]).astype(BF16)
        d_ref[3, 0] = d_og.astype(BF16)

    rev = lambda s: nsb - 1 - s
    big = pltpu.VMEM((nc, HEAD, HEAD), F32)
    rows_f32 = pltpu.VMEM((sb, HEAD), F32)
    return pl.pallas_call(
        body, name="hgrn_bwd", grid=(H, Bl, nsb),
        in_specs=[pl.BlockSpec((4, 1, sb, HEAD), lambda h, b, s: (0, b, rev(s), h)),
                  pl.BlockSpec((2, HEAD), lambda h, b, s: (0, h)),
                  pl.BlockSpec((1, HEAD), lambda h, b, s: (0, h)),
                  pl.BlockSpec((1, sb, HEAD), lambda h, b, s: (b, rev(s), h)),
                  pl.BlockSpec((1, 1, 1, HEAD, HEAD), lambda h, b, s: (b, h, rev(s), 0, 0))],
        out_specs=[pl.BlockSpec((4, 1, sb, HEAD), lambda h, b, s: (0, b, rev(s), h)),
                   pl.BlockSpec((8, HEAD), lambda h, b, s: (0, h))],
        out_shape=[jax.ShapeDtypeStruct((4, Bl, S, D), BF16), jax.ShapeDtypeStruct((8, D), F32)],
        scratch_shapes=[pltpu.VMEM((HEAD, HEAD), F32), rows_f32, big, big, big, rows_f32,
                        rows_f32, rows_f32, rows_f32],
        compiler_params=_params(("parallel", "arbitrary", "arbitrary")),
    )(proj5, lb_logits, gn, dain3, st0_all)


def _dx(d1, dh4, dpv, dg2, w_in, dep):
    T, D = d1.shape
    tm = min(256, T)

    def body(d1_ref, dh_ref, dp_ref, dg_ref, w_ref, o_ref):
        blocks = [dh_ref[0], dh_ref[1], dh_ref[2], dh_ref[3], dp_ref[...], dg_ref[0], dg_ref[1]]
        acc = ALPHA * d1_ref[...]
        for j, blk in enumerate(blocks):
            acc = acc + lax.dot_general(blk, w_ref[:, j * D:(j + 1) * D], NT_DIMS, preferred_element_type=F32)
        o_ref[...] = acc

    row = lambda i: (i, 0)
    return _call_after(
        dep, body, (d1, dh4, dpv, dg2, w_in), name="dx", grid=(T // tm,),
        in_specs=[pl.BlockSpec((tm, D), row), pl.BlockSpec((4, tm, D), lambda i: (0, i, 0)),
                  pl.BlockSpec((tm, D), row), pl.BlockSpec((2, tm, D), lambda i: (0, i, 0)),
                  _resident((D, N_SEC * D))],
        out_specs=pl.BlockSpec((tm, D), row),
        out_shape=jax.ShapeDtypeStruct((T, D), F32),
        compiler_params=_params(("parallel",)))


def _dw_in(x_t, dh4, dpv, dg2, dep):
    D, T = x_t.shape
    shape = (D, N_SEC * D)
    x_spec = _resident((D, T))

    def part(name, b, n_j, first_sec, into, dep):
        b_spec = (pl.BlockSpec((1, T, D), lambda j: (j, 0, 0)) if b.ndim == 3
                  else pl.BlockSpec((T, D), lambda j: (0, 0)))
        return _dw(name, x_t, b, n_j, x_spec, b_spec, shape, (D, D), lambda j: (0, first_sec + j),
                   dep=dep, into=into)

    gates = part("dw_in_gates", dg2, 2, 5, None, dep)
    pool = part("dw_in_pool", dpv, 1, 4, gates, None)
    return part("dw_in_rec", dh4, 4, 0, pool, None)


def _adam_shard(name, me_arr, grad, land, layout, w, m, v):
    shape = layout.shape
    n_split = 4
    blk = (shape[0] // n_split,) + shape[1:]
    zeros = (0,) * (len(shape) - 1)

    def body(me_ref, g_ref, r_ref, w_ref, m_ref, v_ref, g_out, d_out, m_out, v_out):
        g = g_ref[...]
        for k in range(N_DEV - 1):
            g = g + r_ref[k].astype(F32)
        d, m2, v2 = _adamw(w_ref[...], g, m_ref[...], v_ref[...])
        g_out[...] = g
        d_out[...] = d
        m_out[...] = m2
        v_out[...] = v2

    def own(i, me_ref):
        bi = layout.block_index(me_ref[0])
        return (bi[0] * n_split + i,) + tuple(bi[1:]) if layout.kind == "row" else (i,) + tuple(bi[1:])

    plain = pl.BlockSpec(blk, lambda i, me_ref: (i,) + zeros)
    grid_spec = pltpu.PrefetchScalarGridSpec(
        num_scalar_prefetch=1, grid=(n_split,),
        in_specs=[pl.BlockSpec(blk, own),
                  pl.BlockSpec((N_DEV - 1,) + blk, lambda i, me_ref: (0, i) + zeros),
                  plain, plain, plain],
        out_specs=[plain] * 4)
    return pl.pallas_call(
        body, name=name, grid_spec=grid_spec,
        out_shape=[jax.ShapeDtypeStruct(shape, F32)] * 4,
        compiler_params=_params(("parallel",)),
    )(me_arr, grad, land, w, m, v)


def _vec_allreduce(vec):
    D = vec.shape[1]

    def body(vec_ref, tot_ref, gat, send_sems, recv_sems):
        x, y, c = _me()
        me = 4 * x + 2 * y + c
        gat[me] = vec_ref[...]
        copies = []
        for k in range(1, N_DEV):
            cp = pltpu.make_async_remote_copy(
                src_ref=vec_ref, dst_ref=gat.at[me], send_sem=send_sems.at[k - 1],
                recv_sem=recv_sems.at[k - 1], device_id=_peer(k, x, y, c), device_id_type=MESH)
            cp.start()
            copies.append(cp)
        for cp in copies:
            cp.wait()
        tot = gat[0]
        for d in range(1, N_DEV):
            tot = tot + gat[d]
        tot_ref[...] = tot

    vm = pl.BlockSpec(memory_space=pltpu.VMEM)
    return pl.pallas_call(
        body, name="vec_allreduce", out_shape=jax.ShapeDtypeStruct(vec.shape, F32),
        in_specs=[vm], out_specs=vm,
        scratch_shapes=[pltpu.VMEM((N_DEV, 8, D), F32), pltpu.SemaphoreType.DMA((N_DEV - 1,)),
                        pltpu.SemaphoreType.DMA((N_DEV - 1,))],
    )(vec)


def _vec_adam(tot, small_w, small_m, small_v):
    n = len(small_w)

    def body(*refs):
        tot = refs[0][...]
        ws, ms, vs = refs[1:1 + n], refs[1 + n:1 + 2 * n], refs[1 + 2 * n:1 + 3 * n]
        outs = refs[1 + 3 * n:]
        loss_ref, g_out, d_out = outs[0], outs[1:1 + n], outs[1 + n:1 + 2 * n]
        m_out, v_out = outs[1 + 2 * n:1 + 3 * n], outs[1 + 3 * n:1 + 4 * n]
        loss_ref[...] = jnp.broadcast_to(jnp.sum(tot[7:8, :], axis=1, keepdims=True), loss_ref.shape)
        lbl = ws[0][...]
        mx = jnp.maximum(lbl[0:1, :], lbl[1:2, :])
        e0, e1 = jnp.exp(lbl[0:1, :] - mx), jnp.exp(lbl[1:2, :] - mx)
        p0 = e0 / (e0 + e1)
        dl0 = tot[0:1, :] * p0 * (1.0 - p0)
        grads = [jnp.concatenate([dl0, -dl0], axis=0)] + [tot[r:r + 1, :] for r in range(1, n)]
        for i in range(n):
            d, m2, v2 = _adamw(ws[i][...], grads[i], ms[i][...], vs[i][...])
            g_out[i][...] = grads[i]
            d_out[i][...] = d
            m_out[i][...] = m2
            v_out[i][...] = v2

    vm = pl.BlockSpec(memory_space=pltpu.VMEM)
    shapes = [jax.ShapeDtypeStruct(w.shape, F32) for w in small_w]
    return pl.pallas_call(
        body, name="vec_adam",
        out_shape=[jax.ShapeDtypeStruct((1, 128), F32)] + shapes * 4,
        in_specs=[vm] * (1 + 3 * n), out_specs=[vm] * (1 + 4 * n),
    )(tot, *small_w, *small_m, *small_v)


def kernel(x, w_in, lb_logits, hgrn_norm_g, w_a, w_pool, pool_scale, w_out, ln1_g, ln1_b, w_up, w_down, ln2_g, ln2_b, loss_target, m_w_in, m_lb_logits, m_hgrn_norm_g, m_w_a, m_w_pool, m_pool_scale, m_w_out, m_ln1_g, m_ln1_b, m_w_up, m_w_down, m_ln2_g, m_ln2_b, v_w_in, v_lb_logits, v_hgrn_norm_g, v_w_a, v_w_pool, v_pool_scale, v_w_out, v_ln1_g, v_ln1_b, v_w_up, v_w_down, v_ln2_g, v_ln2_b):
    Bl, S, D = x.shape
    T = Bl * S
    pg = D // POOL_GROUPS
    x2 = x.reshape(T, D)
    tgt = loss_target.reshape(T, D)
    me = 4 * lax.axis_index("x") + 2 * lax.axis_index("y") + lax.axis_index("c")
    me_arr = jnp.reshape(me, (1,)).astype(jnp.int32)

    names = ["w_in", "w_a", "w_pool", "w_out", "w_up", "w_down"]
    big_w = dict(zip(names, [w_in[0], w_a[0], w_pool[0], w_out[0], w_up[0], w_down[0]]))
    big_m = dict(zip(names, [m_w_in[0], m_w_a[0], m_w_pool[0], m_w_out[0], m_w_up[0], m_w_down[0]]))
    big_v = dict(zip(names, [v_w_in[0], v_w_a[0], v_w_pool[0], v_w_out[0], v_w_up[0], v_w_down[0]]))
    kinds = dict(w_in="col", w_a="row", w_pool="pool", w_out="row", w_up="col", w_down="row")
    lay = {nm: _Sharded(kinds[nm], big_w[nm].shape) for nm in names}
    wb = {nm: big_w[nm].astype(BF16) for nm in names}

    rest = names[1:]
    w_in_f, *own_placed = _all_gather("ag_w_in", [wb["w_in"]], [lay["w_in"]],
                                      [wb[nm] for nm in rest], [lay[nm] for nm in rest])
    ag_rest = _exchange_start("ag_rest", [wb[nm] for nm in rest], own_placed,
                              src_at=lambda w, ref, peer: ref,
                              dst_at=lambda w, ref, mine, k: lay[rest[w]].at(ref, mine), after=w_in_f)

    proj, x_t = _proj(x2, w_in_f, ag_rest["token"])
    proj5 = proj.reshape(N_SEC, Bl, S, D)
    ain3, ain_t, st0_all = _hgrn_fwd(proj5, lb_logits, hgrn_norm_g)
    w_a_f, w_pool_f, w_out_f, w_up_f, w_down_f = _exchange_wait(ag_rest, ain3)
    pooled_t, bp3 = _pool_fwd(proj5, w_pool_f)
    ain, bp = ain3.reshape(T, D), bp3.reshape(T, D)
    a, merged_t, xhat1, rs1, x1b, x1_t = _mix_fwd(ain, proj, bp, x2, w_a_f, w_out_f, pool_scale, ln1_g, ln1_b)
    hp, h, dr2, dr2b, dr2_t, vec_mlp = _mlp_fwd(x1b, w_up_f, w_down_f, xhat1, tgt, ln1_g, ln1_b, ln2_g, ln2_b)

    def scatter_start(name, nms, grads_b, after):
        lands = [lax.empty((N_DEV - 1,) + lay[nm].shape, BF16) for nm in nms]
        return _exchange_start(name, grads_b, lands,
                               src_at=lambda w, ref, peer: lay[nms[w]].at(ref, peer),
                               dst_at=lambda w, ref, mine, k: ref.at[k - 1], after=after)

    dhp, dr1, dr1b, vec_ln1 = _mlp_bwd(dr2b, dr2, hp, w_up_f, w_down_f, xhat1, rs1, ln1_g)
    FF = 4 * D
    whole_t = _resident((D, T))
    gw, gwb = {}, {}
    gw["w_down"], gwb["w_down"] = _dw(
        "dw_down", dr2_t, h, FF // D, whole_t, pl.BlockSpec((T, D), lambda j: (0, j)),
        (FF, D), (D, D), lambda j: (j, 0), transpose_out=True)
    rs_down = scatter_start("rs_w_down", ["w_down"], [gwb["w_down"]], gw["w_down"])
    gw["w_up"], gwb["w_up"] = _dw(
        "dw_up", x1_t, dhp, FF // D, whole_t, pl.BlockSpec((T, D), lambda j: (0, j)),
        (D, FF), (D, D), lambda j: (0, j), dep=rs_down["token"])
    rs_up = scatter_start("rs_w_up", ["w_up"], [gwb["w_up"]], gw["w_up"])
    da_b, dbp_b, dain, dpooled, dg2, vec_mix = _mix_bwd(dr1b, proj, a, bp, w_a_f, w_out_f, w_pool_f, pool_scale,
                                                        rs_up["token"])
    whole_b = pl.BlockSpec((T, D), lambda j: (0, 0))
    gw["w_out"], gwb["w_out"] = _dw("dw_out", merged_t, dr1b, 1, whole_t, whole_b, (D, D), (D, D), lambda j: (0, 0))
    gw["w_a"], gwb["w_a"] = _dw("dw_a", ain_t, da_b, 1, whole_t, whole_b, (D, D), (D, D), lambda j: (0, 0))
    gw["w_pool"], gwb["w_pool"] = _dw(
        "dw_pool", pooled_t, dbp_b, POOL_GROUPS, pl.BlockSpec((pg, T), lambda j: (j, 0)),
        pl.BlockSpec((T, pg), lambda j: (0, j)), (POOL_GROUPS, pg, pg), (1, pg, pg), lambda j: (j, 0, 0))
    mid = ["w_out", "w_a", "w_pool"]
    rs_mid = scatter_start("rs_w_mid", mid, [gwb[nm] for nm in mid], gw["w_pool"])
    dpv = _pool_bwd(dpooled.reshape(Bl, S, D), rs_mid["token"]).reshape(T, D)
    dh4, vec_hgrn = _hgrn_bwd(proj5, lb_logits, hgrn_norm_g, dain.reshape(Bl, S, D), st0_all)
    dh4 = dh4.reshape(4, T, D)
    gw["w_in"], gwb["w_in"] = _dw_in(x_t, dh4, dpv, dg2, rs_mid["token"])
    rs_in = scatter_start("rs_w_in", ["w_in"], [gwb["w_in"]], gw["w_in"])
    grad_x2 = _dx(dr1, dh4, dpv, dg2, w_in_f, rs_in["token"])
    grad_x = grad_x2.reshape(Bl, S, D)

    vec = vec_mlp + vec_ln1 + vec_mix + vec_hgrn
    small_names = ["lb_logits", "hgrn_norm_g", "pool_scale", "ln1_g", "ln1_b", "ln2_g", "ln2_b"]
    small_w = [lb_logits, hgrn_norm_g, pool_scale, ln1_g, ln1_b, ln2_g, ln2_b]
    small_m = [m_lb_logits, m_hgrn_norm_g, m_pool_scale, m_ln1_g, m_ln1_b, m_ln2_g, m_ln2_b]
    small_v = [v_lb_logits, v_hgrn_norm_g, v_pool_scale, v_ln1_g, v_ln1_b, v_ln2_g, v_ln2_b]
    res = _vec_adam(_vec_allreduce(vec), small_w, small_m, small_v)
    loss = res[0][0, 0]
    n = len(small_w)
    small = {nm: (res[1 + i], res[1 + n + i], res[1 + 2 * n + i], res[1 + 3 * n + i])
             for i, nm in enumerate(small_names)}

    big, last = {}, grad_x2
    for pend, nms in ((rs_down, ["w_down"]), (rs_up, ["w_up"]), (rs_mid, mid), (rs_in, ["w_in"])):
        for nm, land in zip(nms, _exchange_wait(pend, last)):
            outs = _adam_shard("adam_" + nm, me_arr, gw[nm], land, lay[nm], big_w[nm], big_m[nm], big_v[nm])
            big[nm] = tuple(t[None] for t in outs)
            last = outs[0]

    order = ["w_in", "lb_logits", "hgrn_norm_g", "w_a", "w_pool", "pool_scale", "w_out", "ln1_g", "ln1_b",
             "w_up", "w_down", "ln2_g", "ln2_b"]
    allp = {**big, **small}
    out = [loss, grad_x]
    for part in range(4):
        out += [allp[nm][part] for nm in order]
    return tuple(out)
```

```python
import jax
import jax.numpy as jnp
from jax import lax
from jax.experimental import pallas as pl
from jax.experimental.pallas import tpu as pltpu

F32 = jnp.float32
BF16 = jnp.bfloat16
MESH = pl.DeviceIdType.MESH

N_DEV = 8
HEAD = 128
CHUNK = 16
SUBLANES = 8
GROUP = 128
CH_PER_GROUP = GROUP // CHUNK
N_SEC = 7
POOL_GROUPS = 4
ALPHA = (2.0 * 1) ** 0.25
LN_EPS = 1e-5
RMS_EPS = 1e-6
Q_SCALE = HEAD ** -0.5
ADAM_LR = 0.001
ADAM_B1 = 0.9
ADAM_B2 = 0.999
ADAM_EPS = 1e-08
ADAM_WD = 0.01
ADAM_STEP = 10
VMEM_LIMIT = 56 << 20

NT_DIMS = (((1,), (1,)), ((), ()))
TN_DIMS = (((0,), (0,)), ((), ()))


def _params(sem=None):
    kw = dict(vmem_limit_bytes=VMEM_LIMIT)
    if sem is not None:
        kw["dimension_semantics"] = sem
    return pltpu.CompilerParams(**kw)


def _me():
    return lax.axis_index("x"), lax.axis_index("y"), lax.axis_index("c")


def _sigmoid(v):
    return jax.nn.sigmoid(v)


def _adamw(w, g, m, v):
    m = ADAM_B1 * m + (1.0 - ADAM_B1) * g
    v = ADAM_B2 * v + (1.0 - ADAM_B2) * jnp.square(g)
    m_hat = m / (1.0 - ADAM_B1 ** ADAM_STEP)
    v_hat = v / (1.0 - ADAM_B2 ** ADAM_STEP)
    delta = -ADAM_LR * (m_hat / (jnp.sqrt(v_hat) + ADAM_EPS) + ADAM_WD * w)
    return delta, m, v


class _Sharded:
    def __init__(self, kind, shard_shape):
        self.kind, self.shape = kind, tuple(shard_shape)

    @property
    def full_shape(self):
        r = self.shape
        if self.kind == "row":
            return (N_DEV * r[0],) + r[1:]
        return (r[0], N_DEV * r[1]) + r[2:]

    def at(self, ref, d):
        if self.kind == "col":
            n = self.shape[1]
            return ref.at[:, pl.ds(pl.multiple_of(d * n, 128), n)]
        if self.kind == "row":
            n = self.shape[0]
            return ref.at[pl.ds(pl.multiple_of(d * n, 16), n), :]
        n = self.shape[1]
        return ref.at[:, pl.ds(pl.multiple_of(d * n, 16), n), :]

    def block_index(self, d):
        return {"col": (0, d), "row": (d, 0), "pool": (0, d, 0)}[self.kind]


def _peer(k, x, y, c):
    return (1 - x if k & 4 else x, 1 - y if k & 2 else y, 1 - c if k & 1 else c)


def _all_gather(name, shards, layouts, placed=(), placed_layouts=()):
    nw, npl = len(shards), len(placed)

    def body(*refs):
        ins, place_ins = refs[:nw], refs[nw:nw + npl]
        outs, place_outs = refs[nw + npl:2 * nw + npl], refs[2 * nw + npl:2 * (nw + npl)]
        send_sems, recv_sems, local_sems = refs[2 * (nw + npl):]
        x, y, c = _me()
        me = (x, y, c)
        sibling = (x, y, 1 - c)
        chips = [(1 - x, y), (x, 1 - y), (1 - x, 1 - y)]

        def copy(w, k, block, to, src=None):
            px, py, pc = block
            dst = layouts[w].at(outs[w], 4 * px + 2 * py + pc)
            return pltpu.make_async_remote_copy(
                src_ref=dst if src is None else src, dst_ref=dst,
                send_sem=send_sems.at[w, k], recv_sem=recv_sems.at[w, k],
                device_id=to, device_id_type=MESH)

        def place(w):
            src, dst, lay_w = ((ins[w], outs[w], layouts[w]) if w < nw else
                               (place_ins[w - nw], place_outs[w - nw], placed_layouts[w - nw]))
            mine = pltpu.make_async_copy(src, lay_w.at(dst, 4 * x + 2 * y + c), local_sems.at[w])
            mine.start()
            return mine

        first = []
        for w in range(nw):
            first.append(copy(w, 0, me, sibling, src=ins[w]))
            first += [copy(w, 1 + j, me, (*chip, c), src=ins[w]) for j, chip in enumerate(chips)]
        for cp in first:
            cp.start()
        local = [place(w) for w in range(nw)]
        passed = []
        for w in range(nw):
            for j, chip in enumerate(chips):
                copy(w, 1 + j, (*chip, c), me).wait_recv()
                fwd = copy(w, 4 + j, (*chip, c), sibling)
                fwd.start()
                passed.append(fwd)
        local += [place(w) for w in range(nw, nw + npl)]
        for w in range(nw):
            copy(w, 0, sibling, me).wait_recv()
            for j, chip in enumerate(chips):
                copy(w, 4 + j, (*chip, 1 - c), me).wait_recv()
        for cp in first + passed:
            cp.wait_send()
        for cp in local:
            cp.wait()

    any_spec = pl.BlockSpec(memory_space=pl.ANY)
    return pl.pallas_call(
        body, name=name,
        out_shape=[jax.ShapeDtypeStruct(l.full_shape, s.dtype)
                   for s, l in zip(list(shards) + list(placed), list(layouts) + list(placed_layouts))],
        in_specs=[any_spec] * (nw + npl), out_specs=[any_spec] * (nw + npl),
        scratch_shapes=[pltpu.SemaphoreType.DMA((nw, 7)), pltpu.SemaphoreType.DMA((nw, 7)),
                        pltpu.SemaphoreType.DMA((nw + npl,))],
    )(*shards, *placed)


HBM_SPEC = pl.BlockSpec(memory_space=pltpu.HBM)
SEM_SPEC = pl.BlockSpec(memory_space=pltpu.SEMAPHORE)
DATAFLOW = pltpu.SideEffectType.DATAFLOW_SIDE_EFFECTING


def _exchange_copies(srcs, lands, send_sems, recv_sems, src_at, dst_at):
    x, y, c = _me()
    me = 4 * x + 2 * y + c
    copies = []
    for w in range(len(srcs)):
        for k in range(1, N_DEV):
            px, py, pc = _peer(k, x, y, c)
            copies.append(pltpu.make_async_remote_copy(
                src_ref=src_at(w, srcs[w], 4 * px + 2 * py + pc), dst_ref=dst_at(w, lands[w], me, k),
                send_sem=send_sems.at[w * (N_DEV - 1) + k - 1], recv_sem=recv_sems.at[w * (N_DEV - 1) + k - 1],
                device_id=(px, py, pc), device_id_type=MESH))
    return copies


def _exchange_start(name, srcs, lands, src_at, dst_at, after):
    nw = len(srcs)

    def body(*refs):
        src_refs, land_refs = refs[:nw], refs[nw:2 * nw]
        send_sems, recv_sems = refs[2 * nw + 1], refs[2 * nw + 2]
        token = refs[-1]
        for cp in _exchange_copies(src_refs, land_refs, send_sems, recv_sems, src_at, dst_at):
            cp.start()
        token[...] = jnp.zeros_like(token)

    hbm = lambda a: pltpu.HBM(a.shape, a.dtype)
    outs = pl.pallas_call(
        body, name=name,
        out_shape=(pltpu.SemaphoreType.DMA((nw * (N_DEV - 1),)), pltpu.SemaphoreType.DMA((nw * (N_DEV - 1),)),
                   *[hbm(a) for a in srcs], *[hbm(a) for a in lands],
                   jax.ShapeDtypeStruct((8, 128), F32)),
        in_specs=[HBM_SPEC] * (2 * nw) + [pl.BlockSpec(memory_space=pl.ANY)],
        out_specs=(SEM_SPEC, SEM_SPEC, *[HBM_SPEC] * (2 * nw), pl.BlockSpec(memory_space=pltpu.VMEM)),
        input_output_aliases={i: 2 + i for i in range(2 * nw)},
        compiler_params=pltpu.CompilerParams(has_side_effects=DATAFLOW),
    )(*[pltpu.with_memory_space_constraint(a, pltpu.HBM) for a in list(srcs) + list(lands)], after)
    return dict(send=outs[0], recv=outs[1], srcs=outs[2:2 + nw], lands=outs[2 + nw:2 + 2 * nw],
                token=outs[-1], src_at=src_at, dst_at=dst_at, name=name)


def _exchange_wait(pending, after):
    nw = len(pending["srcs"])

    def body(*refs):
        src_refs, land_refs = refs[:nw], refs[nw:2 * nw]
        send_sems, recv_sems = refs[2 * nw], refs[2 * nw + 1]
        for cp in _exchange_copies(src_refs, land_refs, send_sems, recv_sems,
                                   pending["src_at"], pending["dst_at"]):
            cp.wait_send()
            cp.wait_recv()

    hbm = lambda a: pltpu.HBM(a.shape, a.dtype)
    outs = pl.pallas_call(
        body, name=pending["name"] + "_wait",
        out_shape=(*[hbm(a) for a in pending["srcs"]], *[hbm(a) for a in pending["lands"]]),
        in_specs=[HBM_SPEC] * (2 * nw) + [SEM_SPEC, SEM_SPEC, pl.BlockSpec(memory_space=pl.ANY)],
        out_specs=tuple([HBM_SPEC] * (2 * nw)),
        input_output_aliases={i: i for i in range(2 * nw)},
        compiler_params=pltpu.CompilerParams(has_side_effects=DATAFLOW),
    )(*pending["srcs"], *pending["lands"], pending["send"], pending["recv"], after)
    return outs[nw:]


def _call_after(dep, body, args, *, in_specs, **kw):
    n_in = len(args)

    def wrapped(*refs):
        body(*refs[:n_in], *refs[n_in + 1:])

    dep_spec = pl.BlockSpec(dep.shape, lambda *_: (0,) * dep.ndim)
    return pl.pallas_call(wrapped, in_specs=list(in_specs) + [dep_spec], **kw)(*args, dep)


def _resident(shape):
    return pl.BlockSpec(shape, lambda *_: (0,) * len(shape), pipeline_mode=pl.Buffered(1))


def _proj(x2, w_in, dep):
    T, D = x2.shape
    tm = min(256, T)

    def body(x_ref, w_ref, o_ref, xt_ref):
        x = x_ref[...]
        xt_ref[...] = x.T.astype(BF16)
        xb = x.astype(BF16)
        for j in range(N_SEC):
            o_ref[j] = jnp.dot(xb, w_ref[:, j * D:(j + 1) * D], preferred_element_type=F32)

    return _call_after(
        dep, body, (x2, w_in), name="proj", grid=(T // tm,),
        in_specs=[pl.BlockSpec((tm, D), lambda i: (i, 0)), _resident((D, N_SEC * D))],
        out_specs=[pl.BlockSpec((N_SEC, tm, D), lambda i: (0, i, 0)), pl.BlockSpec((D, tm), lambda i: (0, i))],
        out_shape=[jax.ShapeDtypeStruct((N_SEC, T, D), F32), jax.ShapeDtypeStruct((D, T), BF16)],
        compiler_params=_params(("parallel",)))


def _chunk_cumsum(v, reverse=False):
    rows, lanes = v.shape
    x = v.reshape(rows // SUBLANES, SUBLANES, lanes)
    pos = lax.broadcasted_iota(jnp.int32, x.shape, 1)
    for sh in (1, 2, 4):
        if reverse:
            x = x + jnp.where(pos < SUBLANES - sh, pltpu.roll(x, SUBLANES - sh, 1), 0.0)
        else:
            x = x + jnp.where(pos >= sh, pltpu.roll(x, sh, 1), 0.0)
    x = x.reshape(rows // CHUNK, CHUNK // SUBLANES, SUBLANES, lanes)
    half = lax.broadcasted_iota(jnp.int32, x.shape, 1)
    if reverse:
        x = x + jnp.where(half == 0, x[:, 1:2, 0:1, :], 0.0)
    else:
        x = x + jnp.where(half == 1, x[:, 0:1, SUBLANES - 1:SUBLANES, :], 0.0)
    return x.reshape(rows, lanes)


def _hgrn_gates(q, f_pre, lb_logits):
    l0, l1 = lb_logits[0:1, :], lb_logits[1:2, :]
    mx = jnp.maximum(l0, l1)
    e0, e1 = jnp.exp(l0 - mx), jnp.exp(l1 - mx)
    lb = e0 / (e0 + e1)
    sq = _sigmoid(q)
    qf = q * sq * Q_SCALE
    sg = _sigmoid(f_pre)
    f = lb + (1.0 - lb) * sg
    k = 1.0 - f
    log_f = jnp.log(f)
    G = _chunk_cumsum(log_f)
    g_to_end = _chunk_cumsum(log_f, reverse=True) - log_f
    e_g = jnp.exp(G)
    e_ng = jnp.exp(-G)
    e_ge = jnp.exp(g_to_end)
    return dict(lb=lb, sq=sq, qf=qf, sg=sg, f=f, k=k, G=G, e_g=e_g, e_ng=e_ng, e_ge=e_ge,
                qd=qf * e_g, ki=k * e_ng, ke=k * e_ge, dec=jnp.exp(G + g_to_end))


def _intra_mask():
    r = lax.broadcasted_iota(jnp.int32, (GROUP, GROUP), 0)
    c = lax.broadcasted_iota(jnp.int32, (GROUP, GROUP), 1)
    return (r // CHUNK == c // CHUNK) & (c <= r)


def _chunk_outer(lhs_rows, rhs_b, out_scr, sb):
    lane = lax.broadcasted_iota(jnp.int32, (GROUP, GROUP), 1) // CHUNK
    for g in range(sb // GROUP):
        sl = slice(g * GROUP, (g + 1) * GROUP)
        lhs_t = lhs_rows[sl].T
        for cc in range(CH_PER_GROUP):
            masked = jnp.where(lane == cc, lhs_t, 0.0).astype(BF16)
            out_scr[g * CH_PER_GROUP + cc] = jnp.dot(masked, rhs_b[sl], preferred_element_type=F32)


def _hgrn_forward_blocks(cs, vs, st0s, sb, o_scr, kv_scr, st_scr, dec_scr):
    nc = sb // CHUNK
    n_str = len(cs)
    mask = _intra_mask()
    bf = []
    for i, (c, v) in enumerate(zip(cs, vs)):
        qd_b, ki_b, ke_b, v_b = (c["qd"].astype(BF16), c["ki"].astype(BF16), c["ke"].astype(BF16),
                                 v.astype(BF16))
        bf.append((qd_b, ki_b, ke_b, v_b))
        for g in range(sb // GROUP):
            sl = slice(g * GROUP, (g + 1) * GROUP)
            sc = lax.dot_general(qd_b[sl], ki_b[sl], NT_DIMS, preferred_element_type=F32)
            a = jnp.where(mask, sc, 0.0).astype(BF16)
            o_scr[i, sl, :] = jnp.dot(a, v_b[sl], preferred_element_type=F32)
        _chunk_outer(v, ke_b, kv_scr.at[i], sb)
        dec_scr[i] = c["dec"]

    def rec(n, sts):
        row = pl.ds(pl.multiple_of(n * CHUNK, CHUNK), 1)
        out = []
        for i in range(n_str):
            st_scr[i, n] = sts[i]
            out.append(sts[i] * dec_scr[i, row, :] + kv_scr[i, n])
        return tuple(out)

    ends = lax.fori_loop(0, nc, rec, tuple(st0s))

    for n in range(nc):
        rows = slice(n * CHUNK, (n + 1) * CHUNK)
        for i in range(n_str):
            o_scr[i, rows, :] += lax.dot_general(bf[i][0][rows], st_scr[i, n].astype(BF16), NT_DIMS,
                                                 preferred_element_type=F32)
    return ends, bf


def _hgrn_fwd(proj5, lb_logits, gn):
    _, Bl, S, D = proj5.shape
    H = D // HEAD
    sb = min(512, S)
    nsb = S // sb
    nc = sb // CHUNK

    def body(p_ref, lbl_ref, gn_ref, ain_ref, aint_ref, st0_ref, carry, o_scr, kv_scr, st_scr, dec_scr):
        @pl.when(pl.program_id(1) == 0)
        def _():
            carry[...] = jnp.zeros_like(carry)

        st0s = [carry[b] for b in range(Bl)]
        for b in range(Bl):
            st0_ref[b, 0, 0] = st0s[b]
        cs = [_hgrn_gates(p_ref[0, b], p_ref[1, b], lbl_ref[...]) for b in range(Bl)]
        ends, _ = _hgrn_forward_blocks(cs, [p_ref[2, b] for b in range(Bl)], st0s, sb,
                                       o_scr, kv_scr, st_scr, dec_scr)
        for b in range(Bl):
            carry[b] = ends[b]
            o = o_scr[b]
            rinv = lax.rsqrt(jnp.mean(o * o, axis=-1, keepdims=True) + RMS_EPS)
            ain = o * rinv * gn_ref[...] * _sigmoid(p_ref[3, b])
            ain_ref[b] = ain.astype(BF16)
            aint_ref[b] = ain.T.astype(BF16)

    return pl.pallas_call(
        body, name="hgrn_fwd", grid=(H, nsb),
        in_specs=[pl.BlockSpec((4, Bl, sb, HEAD), lambda h, s: (0, 0, s, h)),
                  pl.BlockSpec((2, HEAD), lambda h, s: (0, h)),
                  pl.BlockSpec((1, HEAD), lambda h, s: (0, h))],
        out_specs=[pl.BlockSpec((Bl, sb, HEAD), lambda h, s: (0, s, h)),
                   pl.BlockSpec((Bl, HEAD, sb), lambda h, s: (0, h, s)),
                   pl.BlockSpec((Bl, 1, 1, HEAD, HEAD), lambda h, s: (0, h, s, 0, 0))],
        out_shape=[jax.ShapeDtypeStruct((Bl, S, D), BF16), jax.ShapeDtypeStruct((Bl, D, S), BF16),
                   jax.ShapeDtypeStruct((Bl, H, nsb, HEAD, HEAD), F32)],
        scratch_shapes=[pltpu.VMEM((Bl, HEAD, HEAD), F32), pltpu.VMEM((Bl, sb, HEAD), F32),
                        pltpu.VMEM((Bl, nc, HEAD, HEAD), F32), pltpu.VMEM((Bl, nc, HEAD, HEAD), F32),
                        pltpu.VMEM((Bl, sb, HEAD), F32)],
        compiler_params=_params(("parallel", "arbitrary")),
    )(proj5, lb_logits, gn)


def _window_count(shape, g):
    pos = lax.broadcasted_iota(jnp.int32, shape, 0)
    return pos, jnp.minimum(pos + 1, jnp.left_shift(2, g)).astype(F32)


def _select_window(g, sums):
    return jnp.where(g == 0, sums[0], jnp.where(g == 1, sums[1], jnp.where(g == 2, sums[2], sums[3])))


def _pool_fwd(proj5, w_pool):
    _, Bl, S, D = proj5.shape
    pg = D // POOL_GROUPS

    def body(v_ref, w_ref, pooled_t_ref, bp_ref):
        g = pl.program_id(1)
        v = v_ref[0, 0]
        pos, cnt = _window_count(v.shape, g)
        cur, sums = v, []
        for sh in (1, 2, 4, 8):
            cur = cur + jnp.where(pos >= sh, pltpu.roll(cur, sh, 0), 0.0)
            sums.append(cur)
        pooled = _select_window(g, sums) / cnt - v
        pooled_t_ref[...] = pooled.T.astype(BF16)
        bp_ref[0] = jnp.dot(pooled.astype(BF16), w_ref[0], preferred_element_type=F32)

    return pl.pallas_call(
        body, name="pool_fwd", grid=(Bl, POOL_GROUPS),
        in_specs=[pl.BlockSpec((1, 1, S, pg), lambda b, g: (4, b, 0, g)),
                  pl.BlockSpec((1, pg, pg), lambda b, g: (g, 0, 0))],
        out_specs=[pl.BlockSpec((pg, S), lambda b, g: (g, b)),
                   pl.BlockSpec((1, S, pg), lambda b, g: (b, 0, g))],
        out_shape=[jax.ShapeDtypeStruct((D, Bl * S), BF16), jax.ShapeDtypeStruct((Bl, S, D), F32)],
        compiler_params=_params(("parallel", "parallel")),
    )(proj5, w_pool)


def _layer_norm_fwd(r):
    mu = jnp.mean(r, axis=-1, keepdims=True)
    d = r - mu
    rs = lax.rsqrt(jnp.mean(d * d, axis=-1, keepdims=True) + LN_EPS)
    return d * rs, rs


def _layer_norm_bwd(dy_g, xhat, rs):
    return rs * (dy_g - jnp.mean(dy_g, axis=-1, keepdims=True)
                 - xhat * jnp.mean(dy_g * xhat, axis=-1, keepdims=True))


def _mix_fwd(ain, proj, bp, x2, w_a, w_out, ps, g1, b1):
    T, D = x2.shape
    tm = min(256, T)

    def body(ain_ref, ga_ref, gb_ref, bp_ref, x_ref, wa_ref, wo_ref, ps_ref, g1_ref, b1_ref,
             a_ref, mgt_ref, xh_ref, rs_ref, x1b_ref, x1t_ref):
        a = jnp.dot(ain_ref[...], wa_ref[...], preferred_element_type=F32)
        a_ref[...] = a
        merged = _sigmoid(ga_ref[0]) * a + _sigmoid(gb_ref[0]) * (bp_ref[...] * ps_ref[...])
        mgt_ref[...] = merged.T.astype(BF16)
        r1 = ALPHA * x_ref[...] + jnp.dot(merged.astype(BF16), wo_ref[...], preferred_element_type=F32)
        xhat, rs = _layer_norm_fwd(r1)
        xh_ref[...] = xhat
        rs_ref[...] = rs
        x1 = xhat * g1_ref[...] + b1_ref[...]
        x1b_ref[...] = x1.astype(BF16)
        x1t_ref[...] = x1.T.astype(BF16)

    row = lambda i: (i, 0)
    col = lambda i: (0, i)
    full = lambda i: (0, 0)
    return pl.pallas_call(
        body, name="mix_fwd", grid=(T // tm,),
        in_specs=[pl.BlockSpec((tm, D), row),
                  pl.BlockSpec((1, tm, D), lambda i: (5, i, 0)),
                  pl.BlockSpec((1, tm, D), lambda i: (6, i, 0)),
                  pl.BlockSpec((tm, D), row), pl.BlockSpec((tm, D), row),
                  pl.BlockSpec((D, D), full), pl.BlockSpec((D, D), full),
                  pl.BlockSpec((1, D), full), pl.BlockSpec((1, D), full), pl.BlockSpec((1, D), full)],
        out_specs=[pl.BlockSpec((tm, D), row), pl.BlockSpec((D, tm), col), pl.BlockSpec((tm, D), row),
                   pl.BlockSpec((tm, 1), row), pl.BlockSpec((tm, D), row), pl.BlockSpec((D, tm), col)],
        out_shape=[jax.ShapeDtypeStruct((T, D), F32), jax.ShapeDtypeStruct((D, T), BF16),
                   jax.ShapeDtypeStruct((T, D), F32), jax.ShapeDtypeStruct((T, 1), F32),
                   jax.ShapeDtypeStruct((T, D), BF16), jax.ShapeDtypeStruct((D, T), BF16)],
        compiler_params=_params(("parallel",)),
    )(ain, proj, proj, bp, x2, w_a, w_out, ps, g1, b1)


def _mlp_fwd(x1b, w_up, w_down, xhat1, tgt, g1, b1, g2, b2):
    T, D = xhat1.shape
    FF = w_up.shape[1]
    tm = min(256, T)

    def body(x_ref, wu_ref, wd_ref, xh_ref, t_ref, g1_ref, b1_ref, g2_ref, b2_ref,
             hp_ref, h_ref, dr_ref, drb_ref, drt_ref, vec_ref):
        @pl.when(pl.program_id(0) == 0)
        def _():
            vec_ref[...] = jnp.zeros_like(vec_ref)

        xb = x_ref[...]
        x1 = xh_ref[...] * g1_ref[...] + b1_ref[...]
        r2 = ALPHA * x1
        for f in range(FF // D):
            cols = slice(f * D, (f + 1) * D)
            hp = jnp.dot(xb, wu_ref[:, cols], preferred_element_type=F32)
            hp_ref[:, cols] = hp
            h = jnp.square(jnp.maximum(hp, 0.0)).astype(BF16)
            h_ref[:, cols] = h
            r2 = r2 + jnp.dot(h, wd_ref[cols, :], preferred_element_type=F32)
        xhat2, rs2 = _layer_norm_fwd(r2)
        err = xhat2 * g2_ref[...] + b2_ref[...] - t_ref[...]
        dy = err / D
        vec_ref[5:6, :] += jnp.sum(dy * xhat2, axis=0, keepdims=True)
        vec_ref[6:7, :] += jnp.sum(dy, axis=0, keepdims=True)
        vec_ref[7:8, :] += jnp.sum(0.5 * err * err / D, axis=0, keepdims=True)
        dr = _layer_norm_bwd(dy * g2_ref[...], xhat2, rs2)
        dr_ref[...] = dr
        drb_ref[...] = dr.astype(BF16)
        drt_ref[...] = dr.T.astype(BF16)

    row = lambda i: (i, 0)
    full = lambda i: (0, 0)
    return pl.pallas_call(
        body, name="mlp_fwd", grid=(T // tm,),
        in_specs=[pl.BlockSpec((tm, D), row), _resident((D, FF)), _resident((FF, D)),
                  pl.BlockSpec((tm, D), row), pl.BlockSpec((tm, D), row),
                  pl.BlockSpec((1, D), full), pl.BlockSpec((1, D), full),
                  pl.BlockSpec((1, D), full), pl.BlockSpec((1, D), full)],
        out_specs=[pl.BlockSpec((tm, FF), row), pl.BlockSpec((tm, FF), row), pl.BlockSpec((tm, D), row),
                   pl.BlockSpec((tm, D), row), pl.BlockSpec((D, tm), lambda i: (0, i)),
                   pl.BlockSpec((8, D), full)],
        out_shape=[jax.ShapeDtypeStruct((T, FF), F32), jax.ShapeDtypeStruct((T, FF), BF16),
                   jax.ShapeDtypeStruct((T, D), F32), jax.ShapeDtypeStruct((T, D), BF16),
                   jax.ShapeDtypeStruct((D, T), BF16), jax.ShapeDtypeStruct((8, D), F32)],
        compiler_params=_params(("arbitrary",)),
    )(x1b, w_up, w_down, xhat1, tgt, g1, b1, g2, b2)


def _mlp_bwd(drb, dr, hp, w_up, w_down, xhat1, rs1, g1):
    T, D = dr.shape
    FF = hp.shape[1]
    tm = min(256, T)

    def body(drb_ref, dr_ref, hp_ref, wu_ref, wd_ref, xh_ref, rs_ref, g1_ref,
             dhp_ref, d1_ref, d1b_ref, vec_ref):
        @pl.when(pl.program_id(0) == 0)
        def _():
            vec_ref[...] = jnp.zeros_like(vec_ref)

        drb = drb_ref[...]
        dx1 = ALPHA * dr_ref[...]
        for f in range(FF // D):
            cols = slice(f * D, (f + 1) * D)
            dh = lax.dot_general(drb, wd_ref[cols, :], NT_DIMS, preferred_element_type=F32)
            dhp = (dh * (2.0 * jnp.maximum(hp_ref[:, cols], 0.0))).astype(BF16)
            dhp_ref[:, cols] = dhp
            dx1 = dx1 + lax.dot_general(dhp, wu_ref[:, cols], NT_DIMS, preferred_element_type=F32)
        xhat = xh_ref[...]
        vec_ref[3:4, :] += jnp.sum(dx1 * xhat, axis=0, keepdims=True)
        vec_ref[4:5, :] += jnp.sum(dx1, axis=0, keepdims=True)
        d1 = _layer_norm_bwd(dx1 * g1_ref[...], xhat, rs_ref[...])
        d1_ref[...] = d1
        d1b_ref[...] = d1.astype(BF16)

    row = lambda i: (i, 0)
    full = lambda i: (0, 0)
    return pl.pallas_call(
        body, name="mlp_bwd", grid=(T // tm,),
        in_specs=[pl.BlockSpec((tm, D), row), pl.BlockSpec((tm, D), row), pl.BlockSpec((tm, FF), row),
                  _resident((D, FF)), _resident((FF, D)),
                  pl.BlockSpec((tm, D), row), pl.BlockSpec((tm, 1), row), pl.BlockSpec((1, D), full)],
        out_specs=[pl.BlockSpec((tm, FF), row), pl.BlockSpec((tm, D), row), pl.BlockSpec((tm, D), row),
                   pl.BlockSpec((8, D), full)],
        out_shape=[jax.ShapeDtypeStruct((T, FF), BF16), jax.ShapeDtypeStruct((T, D), F32),
                   jax.ShapeDtypeStruct((T, D), BF16), jax.ShapeDtypeStruct((8, D), F32)],
        compiler_params=_params(("arbitrary",)),
    )(drb, dr, hp, w_up, w_down, xhat1, rs1, g1)


def _dw(name, a_t, b, n_j, a_spec, b_spec, o_shape, o_block, o_map, transpose_out=False, dep=None,
        into=None):
    def body(*refs):
        a_ref, b_ref, o_ref, ob_ref = refs[0], refs[1], refs[-2], refs[-1]
        b_val = b_ref[0] if len(b_ref.shape) == 3 else b_ref[...]
        if len(a_ref.shape) == 3:
            seq = a_ref.shape[2]
            p = sum(jnp.dot(a_ref[i], b_val[i * seq:(i + 1) * seq], preferred_element_type=F32)
                    for i in range(a_ref.shape[0]))
        else:
            p = jnp.dot(a_ref[...], b_val, preferred_element_type=F32)
        if transpose_out:
            p = p.T
        p = p.reshape(o_ref.shape)
        o_ref[...] = p
        ob_ref[...] = p.astype(BF16)

    o_spec = pl.BlockSpec(o_block, o_map)
    kw = dict(name=name, grid=(n_j,), in_specs=[a_spec, b_spec], out_specs=[o_spec, o_spec],
              out_shape=[jax.ShapeDtypeStruct(o_shape, F32), jax.ShapeDtypeStruct(o_shape, BF16)],
              compiler_params=_params(("parallel",)))
    args = (a_t, b)
    if into is not None:
        kw["in_specs"] = kw["in_specs"] + [pl.BlockSpec(memory_space=pl.ANY)] * 2
        kw["input_output_aliases"] = {2: 0, 3: 1}
        args = args + tuple(into)
    if dep is None:
        return pl.pallas_call(body, **kw)(*args)
    return _call_after(dep, body, args, **kw)


def _mix_bwd(d1b, proj, a, bp, w_a, w_out, w_pool, ps, dep):
    T, D = a.shape
    tm = min(256, T)
    pg = D // POOL_GROUPS

    def body(d1b_ref, ga_ref, gb_ref, a_ref, bp_ref, wa_ref, wo_ref, wp_ref, ps_ref,
             da_ref, dbp_ref, dain_ref, dpl_ref, dg_ref, vec_ref):
        @pl.when(pl.program_id(0) == 0)
        def _():
            vec_ref[...] = jnp.zeros_like(vec_ref)

        dm = lax.dot_general(d1b_ref[...], wo_ref[...], NT_DIMS, preferred_element_type=F32)
        sa, sg = _sigmoid(ga_ref[0]), _sigmoid(gb_ref[0])
        bp_v, ps_v = bp_ref[...], ps_ref[...]
        da = (dm * sa).astype(BF16)
        db = dm * sg
        dg_ref[0] = (dm * a_ref[...] * sa * (1.0 - sa)).astype(BF16)
        dg_ref[1] = (dm * (bp_v * ps_v) * sg * (1.0 - sg)).astype(BF16)
        vec_ref[2:3, :] += jnp.sum(db * bp_v, axis=0, keepdims=True)
        dbp = (db * ps_v).astype(BF16)
        da_ref[...] = da
        dbp_ref[...] = dbp
        dain_ref[...] = lax.dot_general(da, wa_ref[...], NT_DIMS, preferred_element_type=F32)
        for g in range(POOL_GROUPS):
            cols = slice(g * pg, (g + 1) * pg)
            dpl_ref[:, cols] = lax.dot_general(dbp[:, cols], wp_ref[g], NT_DIMS,
                                               preferred_element_type=F32)

    row = lambda i: (i, 0)
    full = lambda i: (0, 0)
    return _call_after(
        dep, body, (d1b, proj, proj, a, bp, w_a, w_out, w_pool, ps), name="mix_bwd", grid=(T // tm,),
        in_specs=[pl.BlockSpec((tm, D), row),
                  pl.BlockSpec((1, tm, D), lambda i: (5, i, 0)),
                  pl.BlockSpec((1, tm, D), lambda i: (6, i, 0)),
                  pl.BlockSpec((tm, D), row), pl.BlockSpec((tm, D), row),
                  pl.BlockSpec((D, D), full), pl.BlockSpec((D, D), full),
                  pl.BlockSpec((POOL_GROUPS, pg, pg), lambda i: (0, 0, 0)),
                  pl.BlockSpec((1, D), full)],
        out_specs=[pl.BlockSpec((tm, D), row), pl.BlockSpec((tm, D), row),
                   pl.BlockSpec((tm, D), row), pl.BlockSpec((tm, D), row),
                   pl.BlockSpec((2, tm, D), lambda i: (0, i, 0)),
                   pl.BlockSpec((8, D), full)],
        out_shape=[jax.ShapeDtypeStruct((T, D), BF16), jax.ShapeDtypeStruct((T, D), BF16),
                   jax.ShapeDtypeStruct((T, D), F32), jax.ShapeDtypeStruct((T, D), F32),
                   jax.ShapeDtypeStruct((2, T, D), BF16), jax.ShapeDtypeStruct((8, D), F32)],
        compiler_params=_params(("arbitrary",)))


def _pool_bwd(dpooled3, dep):
    Bl, S, D = dpooled3.shape
    pg = D // POOL_GROUPS

    def body(dp_ref, dv_ref):
        g = pl.program_id(1)
        dp = dp_ref[0]
        pos, cnt = _window_count(dp.shape, g)
        cur, sums = dp / cnt, []
        for sh in (1, 2, 4, 8):
            cur = cur + jnp.where(pos < S - sh, pltpu.roll(cur, S - sh, 0), 0.0)
            sums.append(cur)
        dv_ref[0] = (_select_window(g, sums) - dp).astype(BF16)

    spec = pl.BlockSpec((1, S, pg), lambda b, g: (b, 0, g))
    return _call_after(
        dep, body, (dpooled3,), name="pool_bwd", grid=(Bl, POOL_GROUPS), in_specs=[spec], out_specs=spec,
        out_shape=jax.ShapeDtypeStruct((Bl, S, D), BF16),
        compiler_params=_params(("parallel", "parallel")))


def _hgrn_bwd(proj5, lb_logits, gn, dain3, st0_all):
    _, Bl, S, D = proj5.shape
    H = D // HEAD
    sb = min(512, S)
    nsb = S // sb
    nc = sb // CHUNK
    streams = range(Bl)

    def body(p_ref, lbl_ref, gn_ref, dain_ref, st0_ref, d_ref, vec_ref,
             dcarry, o_scr, kv_scr, st_scr, dst_scr, dec_scr, dvi_scr, dke_scr, dqi_scr):
        s = pl.program_id(1)

        @pl.when(s == 0)
        def _():
            dcarry[...] = jnp.zeros_like(dcarry)
            vec_ref[...] = jnp.zeros_like(vec_ref)

        qs, vs, ogs = [p_ref[0, b] for b in streams], [p_ref[2, b] for b in streams], [p_ref[3, b] for b in streams]
        cs = [_hgrn_gates(qs[b], p_ref[1, b], lbl_ref[...]) for b in streams]
        _, bf = _hgrn_forward_blocks(cs, vs, [st0_ref[b, 0, 0] for b in streams], sb,
                                     o_scr, kv_scr, st_scr, dec_scr)
        mask = _intra_mask()
        gn_v = gn_ref[...]
        keep = []
        for b in streams:
            qd_b, ki_b, ke_b, v_b = bf[b]
            o = o_scr[b]
            rinv = lax.rsqrt(jnp.mean(o * o, axis=-1, keepdims=True) + RMS_EPS)
            on = o * rinv
            so = _sigmoid(ogs[b])
            dain = dain_ref[b]
            vec_ref[1:2, :] += jnp.sum(dain * on * so, axis=0, keepdims=True)
            d_og = dain * on * gn_v * so * (1.0 - so)
            d_on = dain * gn_v * so
            do = rinv * (d_on - on * jnp.mean(d_on * on, axis=-1, keepdims=True))
            do_b = do.astype(BF16)
            dv_parts, dqd_parts, dki_parts = [], [], []
            for g in range(sb // GROUP):
                sl = slice(g * GROUP, (g + 1) * GROUP)
                sc = lax.dot_general(qd_b[sl], ki_b[sl], NT_DIMS, preferred_element_type=F32)
                a = jnp.where(mask, sc, 0.0).astype(BF16)
                da = lax.dot_general(do_b[sl], v_b[sl], NT_DIMS, preferred_element_type=F32)
                da = jnp.where(mask, da, 0.0).astype(BF16)
                dv_parts.append(lax.dot_general(a, do_b[sl], TN_DIMS, preferred_element_type=F32))
                dqd_parts.append(jnp.dot(da, ki_b[sl], preferred_element_type=F32))
                dki_parts.append(lax.dot_general(da, qd_b[sl], TN_DIMS, preferred_element_type=F32))
            keep.append(dict(d_og=d_og, do_b=do_b, dv_intra=jnp.concatenate(dv_parts, axis=0),
                             dqd_intra=jnp.concatenate(dqd_parts, axis=0),
                             dki=jnp.concatenate(dki_parts, axis=0)))
            _chunk_outer(do, qd_b, kv_scr.at[b], sb)

        def rrec(i, dsts):
            n = nc - 1 - i
            row = pl.ds(pl.multiple_of(n * CHUNK, CHUNK), 1)
            out = []
            for b in streams:
                dst_scr[b, n] = dsts[b]
                out.append(dsts[b] * dec_scr[b, row, :] + kv_scr[b, n])
            return tuple(out)

        ends = lax.fori_loop(0, nc, rrec, tuple(dcarry[b] for b in streams))
        for b in streams:
            dcarry[b] = ends[b]
        for n in range(nc):
            rows = slice(n * CHUNK, (n + 1) * CHUNK)
            for b in streams:
                qd_b, ki_b, ke_b, v_b = bf[b]
                dst_b = dst_scr[b, n].astype(BF16)
                dvi_scr[b, rows, :] = lax.dot_general(ke_b[rows], dst_b, NT_DIMS, preferred_element_type=F32)
                dke_scr[b, rows, :] = jnp.dot(v_b[rows], dst_b, preferred_element_type=F32)
                dqi_scr[b, rows, :] = jnp.dot(keep[b]["do_b"][rows], st_scr[b, n].astype(BF16),
                                              preferred_element_type=F32)
        for b in streams:
            c, k = cs[b], keep[b]
            ddec = jnp.sum(dst_scr[b] * st_scr[b], axis=1)
            dgl = jnp.broadcast_to(ddec[:, None, :], (nc, CHUNK, HEAD)).reshape(sb, HEAD) * c["dec"]
            dqd = k["dqd_intra"] + dqi_scr[b]
            dke = dke_scr[b]
            dki = k["dki"]
            t_ke = dke * c["ke"]
            dG = dqd * c["qd"] - dki * c["ki"] - t_ke
            dgl = dgl + _chunk_cumsum(t_ke) + _chunk_cumsum(t_ke, reverse=True) - t_ke
            dlogf = _chunk_cumsum(dG, reverse=True) + dgl
            dk = dki * c["e_ng"] + dke * c["e_ge"]
            df = dlogf / c["f"] - dk
            sg, sq, lb, q = c["sg"], c["sq"], c["lb"], qs[b]
            vec_ref[0:1, :] += jnp.sum(df * (1.0 - sg), axis=0, keepdims=True)
            d_ref[0, b] = (dqd * c["e_g"] * Q_SCALE * (sq + q * sq * (1.0 - sq))).astype(BF16)
            d_ref[1, b] = (df * (1.0 - lb) * sg * (1.0 - sg)).astype(BF16)
            d_ref[2, b] = (k["dv_intra"] + dvi_scr[b]).astype(BF16)
            d_ref[3, b] = k["d_og"].astype(BF16)

    rev = lambda s: nsb - 1 - s
    big = pltpu.VMEM((Bl, nc, HEAD, HEAD), F32)
    rows_f32 = pltpu.VMEM((Bl, sb, HEAD), F32)
    return pl.pallas_call(
        body, name="hgrn_bwd", grid=(H, nsb),
        in_specs=[pl.BlockSpec((4, Bl, sb, HEAD), lambda h, s: (0, 0, rev(s), h)),
                  pl.BlockSpec((2, HEAD), lambda h, s: (0, h)),
                  pl.BlockSpec((1, HEAD), lambda h, s: (0, h)),
                  pl.BlockSpec((Bl, sb, HEAD), lambda h, s: (0, rev(s), h)),
                  pl.BlockSpec((Bl, 1, 1, HEAD, HEAD), lambda h, s: (0, h, rev(s), 0, 0))],
        out_specs=[pl.BlockSpec((4, Bl, sb, HEAD), lambda h, s: (0, 0, rev(s), h)),
                   pl.BlockSpec((8, HEAD), lambda h, s: (0, h))],
        out_shape=[jax.ShapeDtypeStruct((4, Bl, S, D), BF16), jax.ShapeDtypeStruct((8, D), F32)],
        scratch_shapes=[pltpu.VMEM((Bl, HEAD, HEAD), F32), rows_f32, big, big, big, rows_f32,
                        rows_f32, rows_f32, rows_f32],
        compiler_params=_params(("parallel", "arbitrary")),
    )(proj5, lb_logits, gn, dain3, st0_all)


def _dx(d1, dh4, dpv, dg2, w_in, dep):
    T, D = d1.shape
    tm = min(256, T)

    def body(d1_ref, dh_ref, dp_ref, dg_ref, w_ref, o_ref):
        blocks = [dh_ref[0], dh_ref[1], dh_ref[2], dh_ref[3], dp_ref[...], dg_ref[0], dg_ref[1]]
        acc = ALPHA * d1_ref[...]
        for j, blk in enumerate(blocks):
            acc = acc + lax.dot_general(blk, w_ref[:, j * D:(j + 1) * D], NT_DIMS, preferred_element_type=F32)
        o_ref[...] = acc

    row = lambda i: (i, 0)
    return _call_after(
        dep, body, (d1, dh4, dpv, dg2, w_in), name="dx", grid=(T // tm,),
        in_specs=[pl.BlockSpec((tm, D), row), pl.BlockSpec((4, tm, D), lambda i: (0, i, 0)),
                  pl.BlockSpec((tm, D), row), pl.BlockSpec((2, tm, D), lambda i: (0, i, 0)),
                  _resident((D, N_SEC * D))],
        out_specs=pl.BlockSpec((tm, D), row),
        out_shape=jax.ShapeDtypeStruct((T, D), F32),
        compiler_params=_params(("parallel",)))


def _dw_in(x_t, dh4, dpv, dg2, dep):
    D, T = x_t.shape
    shape = (D, N_SEC * D)
    x_spec = _resident((D, T))

    def part(name, b, n_j, first_sec, into, dep):
        b_spec = (pl.BlockSpec((1, T, D), lambda j: (j, 0, 0)) if b.ndim == 3
                  else pl.BlockSpec((T, D), lambda j: (0, 0)))
        return _dw(name, x_t, b, n_j, x_spec, b_spec, shape, (D, D), lambda j: (0, first_sec + j),
                   dep=dep, into=into)

    gates = part("dw_in_gates", dg2, 2, 5, None, dep)
    pool = part("dw_in_pool", dpv, 1, 4, gates, None)
    return part("dw_in_rec", dh4, 4, 0, pool, None)


def _adam_shard(name, me_arr, grad, land, layout, w, m, v):
    shape = layout.shape
    n_split = 4
    blk = (shape[0] // n_split,) + shape[1:]
    zeros = (0,) * (len(shape) - 1)

    def body(me_ref, g_ref, r_ref, w_ref, m_ref, v_ref, g_out, d_out, m_out, v_out):
        g = g_ref[...]
        for k in range(N_DEV - 1):
            g = g + r_ref[k].astype(F32)
        d, m2, v2 = _adamw(w_ref[...], g, m_ref[...], v_ref[...])
        g_out[...] = g
        d_out[...] = d
        m_out[...] = m2
        v_out[...] = v2

    def own(i, me_ref):
        bi = layout.block_index(me_ref[0])
        return (bi[0] * n_split + i,) + tuple(bi[1:]) if layout.kind == "row" else (i,) + tuple(bi[1:])

    plain = pl.BlockSpec(blk, lambda i, me_ref: (i,) + zeros)
    grid_spec = pltpu.PrefetchScalarGridSpec(
        num_scalar_prefetch=1, grid=(n_split,),
        in_specs=[pl.BlockSpec(blk, own),
                  pl.BlockSpec((N_DEV - 1,) + blk, lambda i, me_ref: (0, i) + zeros),
                  plain, plain, plain],
        out_specs=[plain] * 4)
    return pl.pallas_call(
        body, name=name, grid_spec=grid_spec,
        out_shape=[jax.ShapeDtypeStruct(shape, F32)] * 4,
        compiler_params=_params(("parallel",)),
    )(me_arr, grad, land, w, m, v)


def _vec_allreduce(vec):
    D = vec.shape[1]

    def body(vec_ref, tot_ref, gat, send_sems, recv_sems):
        x, y, c = _me()
        me = 4 * x + 2 * y + c
        gat[me] = vec_ref[...]
        copies = []
        for k in range(1, N_DEV):
            cp = pltpu.make_async_remote_copy(
                src_ref=vec_ref, dst_ref=gat.at[me], send_sem=send_sems.at[k - 1],
                recv_sem=recv_sems.at[k - 1], device_id=_peer(k, x, y, c), device_id_type=MESH)
            cp.start()
            copies.append(cp)
        for cp in copies:
            cp.wait()
        tot = gat[0]
        for d in range(1, N_DEV):
            tot = tot + gat[d]
        tot_ref[...] = tot

    vm = pl.BlockSpec(memory_space=pltpu.VMEM)
    return pl.pallas_call(
        body, name="vec_allreduce", out_shape=jax.ShapeDtypeStruct(vec.shape, F32),
        in_specs=[vm], out_specs=vm,
        scratch_shapes=[pltpu.VMEM((N_DEV, 8, D), F32), pltpu.SemaphoreType.DMA((N_DEV - 1,)),
                        pltpu.SemaphoreType.DMA((N_DEV - 1,))],
    )(vec)


def _vec_adam(tot, small_w, small_m, small_v):
    n = len(small_w)

    def body(*refs):
        tot = refs[0][...]
        ws, ms, vs = refs[1:1 + n], refs[1 + n:1 + 2 * n], refs[1 + 2 * n:1 + 3 * n]
        outs = refs[1 + 3 * n:]
        loss_ref, g_out, d_out = outs[0], outs[1:1 + n], outs[1 + n:1 + 2 * n]
        m_out, v_out = outs[1 + 2 * n:1 + 3 * n], outs[1 + 3 * n:1 + 4 * n]
        loss_ref[...] = jnp.broadcast_to(jnp.sum(tot[7:8, :], axis=1, keepdims=True), loss_ref.shape)
        lbl = ws[0][...]
        mx = jnp.maximum(lbl[0:1, :], lbl[1:2, :])
        e0, e1 = jnp.exp(lbl[0:1, :] - mx), jnp.exp(lbl[1:2, :] - mx)
        p0 = e0 / (e0 + e1)
        dl0 = tot[0:1, :] * p0 * (1.0 - p0)
        grads = [jnp.concatenate([dl0, -dl0], axis=0)] + [tot[r:r + 1, :] for r in range(1, n)]
        for i in range(n):
            d, m2, v2 = _adamw(ws[i][...], grads[i], ms[i][...], vs[i][...])
            g_out[i][...] = grads[i]
            d_out[i][...] = d
            m_out[i][...] = m2
            v_out[i][...] = v2

    vm = pl.BlockSpec(memory_space=pltpu.VMEM)
    shapes = [jax.ShapeDtypeStruct(w.shape, F32) for w in small_w]
    return pl.pallas_call(
        body, name="vec_adam",
        out_shape=[jax.ShapeDtypeStruct((1, 128), F32)] + shapes * 4,
        in_specs=[vm] * (1 + 3 * n), out_specs=[vm] * (1 + 4 * n),
    )(tot, *small_w, *small_m, *small_v)


def kernel(x, w_in, lb_logits, hgrn_norm_g, w_a, w_pool, pool_scale, w_out, ln1_g, ln1_b, w_up, w_down, ln2_g, ln2_b, loss_target, m_w_in, m_lb_logits, m_hgrn_norm_g, m_w_a, m_w_pool, m_pool_scale, m_w_out, m_ln1_g, m_ln1_b, m_w_up, m_w_down, m_ln2_g, m_ln2_b, v_w_in, v_lb_logits, v_hgrn_norm_g, v_w_a, v_w_pool, v_pool_scale, v_w_out, v_ln1_g, v_ln1_b, v_w_up, v_w_down, v_ln2_g, v_ln2_b):
    Bl, S, D = x.shape
    T = Bl * S
    pg = D // POOL_GROUPS
    x2 = x.reshape(T, D)
    tgt = loss_target.reshape(T, D)
    me = 4 * lax.axis_index("x") + 2 * lax.axis_index("y") + lax.axis_index("c")
    me_arr = jnp.reshape(me, (1,)).astype(jnp.int32)

    names = ["w_in", "w_a", "w_pool", "w_out", "w_up", "w_down"]
    big_w = dict(zip(names, [w_in[0], w_a[0], w_pool[0], w_out[0], w_up[0], w_down[0]]))
    big_m = dict(zip(names, [m_w_in[0], m_w_a[0], m_w_pool[0], m_w_out[0], m_w_up[0], m_w_down[0]]))
    big_v = dict(zip(names, [v_w_in[0], v_w_a[0], v_w_pool[0], v_w_out[0], v_w_up[0], v_w_down[0]]))
    kinds = dict(w_in="col", w_a="row", w_pool="pool", w_out="row", w_up="col", w_down="row")
    lay = {nm: _Sharded(kinds[nm], big_w[nm].shape) for nm in names}
    wb = {nm: big_w[nm].astype(BF16) for nm in names}

    rest = names[1:]
    w_in_f, *own_placed = _all_gather("ag_w_in", [wb["w_in"]], [lay["w_in"]],
                                      [wb[nm] for nm in rest], [lay[nm] for nm in rest])
    ag_rest = _exchange_start("ag_rest", [wb[nm] for nm in rest], own_placed,
                              src_at=lambda w, ref, peer: ref,
                              dst_at=lambda w, ref, mine, k: lay[rest[w]].at(ref, mine), after=w_in_f)

    proj, x_t = _proj(x2, w_in_f, ag_rest["token"])
    proj5 = proj.reshape(N_SEC, Bl, S, D)
    ain3, ain_t, st0_all = _hgrn_fwd(proj5, lb_logits, hgrn_norm_g)
    w_a_f, w_pool_f, w_out_f, w_up_f, w_down_f = _exchange_wait(ag_rest, ain3)
    pooled_t, bp3 = _pool_fwd(proj5, w_pool_f)
    ain, bp = ain3.reshape(T, D), bp3.reshape(T, D)
    a, merged_t, xhat1, rs1, x1b, x1_t = _mix_fwd(ain, proj, bp, x2, w_a_f, w_out_f, pool_scale, ln1_g, ln1_b)
    hp, h, dr2, dr2b, dr2_t, vec_mlp = _mlp_fwd(x1b, w_up_f, w_down_f, xhat1, tgt, ln1_g, ln1_b, ln2_g, ln2_b)

    def scatter_start(name, nms, grads_b, after):
        lands = [lax.empty((N_DEV - 1,) + lay[nm].shape, BF16) for nm in nms]
        return _exchange_start(name, grads_b, lands,
                               src_at=lambda w, ref, peer: lay[nms[w]].at(ref, peer),
                               dst_at=lambda w, ref, mine, k: ref.at[k - 1], after=after)

    dhp, dr1, dr1b, vec_ln1 = _mlp_bwd(dr2b, dr2, hp, w_up_f, w_down_f, xhat1, rs1, ln1_g)
    FF = 4 * D
    whole_t = _resident((D, T))
    gw, gwb = {}, {}
    gw["w_down"], gwb["w_down"] = _dw(
        "dw_down", dr2_t, h, FF // D, whole_t, pl.BlockSpec((T, D), lambda j: (0, j)),
        (FF, D), (D, D), lambda j: (j, 0), transpose_out=True)
    rs_down = scatter_start("rs_w_down", ["w_down"], [gwb["w_down"]], gw["w_down"])
    gw["w_up"], gwb["w_up"] = _dw(
        "dw_up", x1_t, dhp, FF // D, whole_t, pl.BlockSpec((T, D), lambda j: (0, j)),
        (D, FF), (D, D), lambda j: (0, j), dep=rs_down["token"])
    rs_up = scatter_start("rs_w_up", ["w_up"], [gwb["w_up"]], gw["w_up"])
    da_b, dbp_b, dain, dpooled, dg2, vec_mix = _mix_bwd(dr1b, proj, a, bp, w_a_f, w_out_f, w_pool_f, pool_scale,
                                                        rs_up["token"])
    whole_b = pl.BlockSpec((T, D), lambda j: (0, 0))
    gw["w_out"], gwb["w_out"] = _dw("dw_out", merged_t, dr1b, 1, whole_t, whole_b, (D, D), (D, D), lambda j: (0, 0))
    gw["w_a"], gwb["w_a"] = _dw("dw_a", ain_t, da_b, 1, _resident((Bl, D, S)), whole_b, (D, D), (D, D),
                                lambda j: (0, 0))
    gw["w_pool"], gwb["w_pool"] = _dw(
        "dw_pool", pooled_t, dbp_b, POOL_GROUPS, pl.BlockSpec((pg, T), lambda j: (j, 0)),
        pl.BlockSpec((T, pg), lambda j: (0, j)), (POOL_GROUPS, pg, pg), (1, pg, pg), lambda j: (j, 0, 0))
    mid = ["w_out", "w_a", "w_pool"]
    rs_mid = scatter_start("rs_w_mid", mid, [gwb[nm] for nm in mid], gw["w_pool"])
    dpv = _pool_bwd(dpooled.reshape(Bl, S, D), rs_mid["token"]).reshape(T, D)
    dh4, vec_hgrn = _hgrn_bwd(proj5, lb_logits, hgrn_norm_g, dain.reshape(Bl, S, D), st0_all)
    dh4 = dh4.reshape(4, T, D)
    gw["w_in"], gwb["w_in"] = _dw_in(x_t, dh4, dpv, dg2, rs_mid["token"])
    rs_in = scatter_start("rs_w_in", ["w_in"], [gwb["w_in"]], gw["w_in"])
    grad_x2 = _dx(dr1, dh4, dpv, dg2, w_in_f, rs_in["token"])
    grad_x = grad_x2.reshape(Bl, S, D)

    vec = vec_mlp + vec_ln1 + vec_mix + vec_hgrn
    small_names = ["lb_logits", "hgrn_norm_g", "pool_scale", "ln1_g", "ln1_b", "ln2_g", "ln2_b"]
    small_w = [lb_logits, hgrn_norm_g, pool_scale, ln1_g, ln1_b, ln2_g, ln2_b]
    small_m = [m_lb_logits, m_hgrn_norm_g, m_pool_scale, m_ln1_g, m_ln1_b, m_ln2_g, m_ln2_b]
    small_v = [v_lb_logits, v_hgrn_norm_g, v_pool_scale, v_ln1_g, v_ln1_b, v_ln2_g, v_ln2_b]
    res = _vec_adam(_vec_allreduce(vec), small_w, small_m, small_v)
    loss = res[0][0, 0]
    n = len(small_w)
    small = {nm: (res[1 + i], res[1 + n + i], res[1 + 2 * n + i], res[1 + 3 * n + i])
             for i, nm in enumerate(small_names)}

    big, last = {}, grad_x2
    for pend, nms in ((rs_down, ["w_down"]), (rs_up, ["w_up"]), (rs_mid, mid), (rs_in, ["w_in"])):
        for nm, land in zip(nms, _exchange_wait(pend, last)):
            outs = _adam_shard("adam_" + nm, me_arr, gw[nm], land, lay[nm], big_w[nm], big_m[nm], big_v[nm])
            big[nm] = tuple(t[None] for t in outs)
            last = outs[0]

    order = ["w_in", "lb_logits", "hgrn_norm_g", "w_a", "w_pool", "pool_scale", "w_out", "ln1_g", "ln1_b",
             "w_up", "w_down", "ln2_g", "ln2_b"]
    allp = {**big, **small}
    out = [loss, grad_x]
    for part in range(4):
        out += [allp[nm][part] for nm in order]
    return tuple(out)
```

```python
import jax
import jax.numpy as jnp
from jax import lax
from jax.experimental import pallas as pl
from jax.experimental.pallas import tpu as pltpu

F32 = jnp.float32
BF16 = jnp.bfloat16
MESH = pl.DeviceIdType.MESH

N_DEV = 8
HEAD = 128
CHUNK = 16
SUBLANES = 8
GROUP = 128
CH_PER_GROUP = GROUP // CHUNK
N_SEC = 7
POOL_GROUPS = 4
ALPHA = (2.0 * 1) ** 0.25
LN_EPS = 1e-5
RMS_EPS = 1e-6
Q_SCALE = HEAD ** -0.5
ADAM_LR = 0.001
ADAM_B1 = 0.9
ADAM_B2 = 0.999
ADAM_EPS = 1e-08
ADAM_WD = 0.01
ADAM_STEP = 10
VMEM_LIMIT = 56 << 20

NT_DIMS = (((1,), (1,)), ((), ()))
TN_DIMS = (((0,), (0,)), ((), ()))


def _params(sem=None):
    kw = dict(vmem_limit_bytes=VMEM_LIMIT)
    if sem is not None:
        kw["dimension_semantics"] = sem
    return pltpu.CompilerParams(**kw)


def _me():
    return lax.axis_index("x"), lax.axis_index("y"), lax.axis_index("c")


def _sigmoid(v):
    return jax.nn.sigmoid(v)


def _adamw(w, g, m, v):
    m = ADAM_B1 * m + (1.0 - ADAM_B1) * g
    v = ADAM_B2 * v + (1.0 - ADAM_B2) * jnp.square(g)
    m_hat = m / (1.0 - ADAM_B1 ** ADAM_STEP)
    v_hat = v / (1.0 - ADAM_B2 ** ADAM_STEP)
    delta = -ADAM_LR * (m_hat / (jnp.sqrt(v_hat) + ADAM_EPS) + ADAM_WD * w)
    return delta, m, v


class _Sharded:
    def __init__(self, kind, shard_shape):
        self.kind, self.shape = kind, tuple(shard_shape)

    @property
    def full_shape(self):
        r = self.shape
        if self.kind == "row":
            return (N_DEV * r[0],) + r[1:]
        return (r[0], N_DEV * r[1]) + r[2:]

    def at(self, ref, d):
        if self.kind == "col":
            n = self.shape[1]
            return ref.at[:, pl.ds(pl.multiple_of(d * n, 128), n)]
        if self.kind == "row":
            n = self.shape[0]
            return ref.at[pl.ds(pl.multiple_of(d * n, 16), n), :]
        n = self.shape[1]
        return ref.at[:, pl.ds(pl.multiple_of(d * n, 16), n), :]

    def block_index(self, d):
        return {"col": (0, d), "row": (d, 0), "pool": (0, d, 0)}[self.kind]


def _peer(k, x, y, c):
    return (1 - x if k & 4 else x, 1 - y if k & 2 else y, 1 - c if k & 1 else c)


def _all_gather(name, shards, layouts):
    nw = len(shards)

    def body(*refs):
        ins, outs = refs[:nw], refs[nw:2 * nw]
        send_sems, recv_sems, local_sems = refs[2 * nw:]
        x, y, c = _me()
        me = (x, y, c)
        sibling = (x, y, 1 - c)
        chips = [(1 - x, y), (x, 1 - y), (1 - x, 1 - y)]

        def copy(w, k, block, to, src=None):
            px, py, pc = block
            dst = layouts[w].at(outs[w], 4 * px + 2 * py + pc)
            return pltpu.make_async_remote_copy(
                src_ref=dst if src is None else src, dst_ref=dst,
                send_sem=send_sems.at[w, k], recv_sem=recv_sems.at[w, k],
                device_id=to, device_id_type=MESH)

        def place(w):
            mine = pltpu.make_async_copy(ins[w], layouts[w].at(outs[w], 4 * x + 2 * y + c), local_sems.at[w])
            mine.start()
            return mine

        first = []
        for w in range(nw):
            first.append(copy(w, 0, me, sibling, src=ins[w]))
            first += [copy(w, 1 + j, me, (*chip, c), src=ins[w]) for j, chip in enumerate(chips)]
        for cp in first:
            cp.start()
        local = [place(w) for w in range(nw)]
        passed = []
        for w in range(nw):
            for j, chip in enumerate(chips):
                copy(w, 1 + j, (*chip, c), me).wait_recv()
                fwd = copy(w, 4 + j, (*chip, c), sibling)
                fwd.start()
                passed.append(fwd)
        for w in range(nw):
            copy(w, 0, sibling, me).wait_recv()
            for j, chip in enumerate(chips):
                copy(w, 4 + j, (*chip, 1 - c), me).wait_recv()
        for cp in first + passed:
            cp.wait_send()
        for cp in local:
            cp.wait()

    any_spec = pl.BlockSpec(memory_space=pl.ANY)
    return pl.pallas_call(
        body, name=name,
        out_shape=[jax.ShapeDtypeStruct(l.full_shape, s.dtype) for s, l in zip(shards, layouts)],
        in_specs=[any_spec] * nw, out_specs=[any_spec] * nw,
        scratch_shapes=[pltpu.SemaphoreType.DMA((nw, 7)), pltpu.SemaphoreType.DMA((nw, 7)),
                        pltpu.SemaphoreType.DMA((nw,))],
    )(*shards)


HBM_SPEC = pl.BlockSpec(memory_space=pltpu.HBM)
SEM_SPEC = pl.BlockSpec(memory_space=pltpu.SEMAPHORE)
DATAFLOW = pltpu.SideEffectType.DATAFLOW_SIDE_EFFECTING


def _exchange_copies(srcs, lands, send_sems, recv_sems, src_at, dst_at):
    x, y, c = _me()
    me = 4 * x + 2 * y + c
    copies = []
    for w in range(len(srcs)):
        for k in range(1, N_DEV):
            px, py, pc = _peer(k, x, y, c)
            copies.append(pltpu.make_async_remote_copy(
                src_ref=src_at(w, srcs[w], 4 * px + 2 * py + pc), dst_ref=dst_at(w, lands[w], me, k),
                send_sem=send_sems.at[w * (N_DEV - 1) + k - 1], recv_sem=recv_sems.at[w * (N_DEV - 1) + k - 1],
                device_id=(px, py, pc), device_id_type=MESH))
    return copies


def _own_copies(srcs, lands, own_sems, src_at, dst_at):
    x, y, c = _me()
    me = 4 * x + 2 * y + c
    return [pltpu.make_async_copy(src_at(w, srcs[w], me), dst_at(w, lands[w], me, 0), own_sems.at[w])
            for w in range(len(srcs))]


def _exchange_start(name, srcs, lands, src_at, dst_at, after, own=False):
    nw = len(srcs)

    def body(*refs):
        src_refs, land_refs = refs[:nw], refs[nw:2 * nw]
        send_sems, recv_sems, own_sems = refs[2 * nw + 1], refs[2 * nw + 2], refs[2 * nw + 3]
        token = refs[-1]
        for cp in _exchange_copies(src_refs, land_refs, send_sems, recv_sems, src_at, dst_at):
            cp.start()
        if own:
            for cp in _own_copies(src_refs, land_refs, own_sems, src_at, dst_at):
                cp.start()
        token[...] = jnp.zeros_like(token)

    hbm = lambda a: pltpu.HBM(a.shape, a.dtype)
    outs = pl.pallas_call(
        body, name=name,
        out_shape=(pltpu.SemaphoreType.DMA((nw * (N_DEV - 1),)), pltpu.SemaphoreType.DMA((nw * (N_DEV - 1),)),
                   pltpu.SemaphoreType.DMA((nw,)), *[hbm(a) for a in srcs], *[hbm(a) for a in lands],
                   jax.ShapeDtypeStruct((8, 128), F32)),
        in_specs=[HBM_SPEC] * (2 * nw) + [pl.BlockSpec(memory_space=pl.ANY)],
        out_specs=(SEM_SPEC, SEM_SPEC, SEM_SPEC, *[HBM_SPEC] * (2 * nw), pl.BlockSpec(memory_space=pltpu.VMEM)),
        input_output_aliases={i: 3 + i for i in range(2 * nw)},
        compiler_params=pltpu.CompilerParams(has_side_effects=DATAFLOW),
    )(*[pltpu.with_memory_space_constraint(a, pltpu.HBM) for a in list(srcs) + list(lands)], after)
    return dict(send=outs[0], recv=outs[1], own_sems=outs[2], srcs=outs[3:3 + nw], lands=outs[3 + nw:3 + 2 * nw],
                token=outs[-1], src_at=src_at, dst_at=dst_at, name=name, own=own)


def _exchange_wait(pending, after):
    nw = len(pending["srcs"])

    def body(*refs):
        src_refs, land_refs = refs[:nw], refs[nw:2 * nw]
        send_sems, recv_sems, own_sems = refs[2 * nw], refs[2 * nw + 1], refs[2 * nw + 2]
        for cp in _exchange_copies(src_refs, land_refs, send_sems, recv_sems,
                                   pending["src_at"], pending["dst_at"]):
            cp.wait_send()
            cp.wait_recv()
        if pending["own"]:
            for cp in _own_copies(src_refs, land_refs, own_sems, pending["src_at"], pending["dst_at"]):
                cp.wait()

    hbm = lambda a: pltpu.HBM(a.shape, a.dtype)
    outs = pl.pallas_call(
        body, name=pending["name"] + "_wait",
        out_shape=(*[hbm(a) for a in pending["srcs"]], *[hbm(a) for a in pending["lands"]]),
        in_specs=[HBM_SPEC] * (2 * nw) + [SEM_SPEC, SEM_SPEC, SEM_SPEC, pl.BlockSpec(memory_space=pl.ANY)],
        out_specs=tuple([HBM_SPEC] * (2 * nw)),
        input_output_aliases={i: i for i in range(2 * nw)},
        compiler_params=pltpu.CompilerParams(has_side_effects=DATAFLOW),
    )(*pending["srcs"], *pending["lands"], pending["send"], pending["recv"], pending["own_sems"], after)
    return outs[nw:]


def _call_after(dep, body, args, *, in_specs, **kw):
    n_in = len(args)

    def wrapped(*refs):
        body(*refs[:n_in], *refs[n_in + 1:])

    dep_spec = pl.BlockSpec(dep.shape, lambda *_: (0,) * dep.ndim)
    return pl.pallas_call(wrapped, in_specs=list(in_specs) + [dep_spec], **kw)(*args, dep)


def _resident(shape):
    return pl.BlockSpec(shape, lambda *_: (0,) * len(shape), pipeline_mode=pl.Buffered(1))


def _proj(x2, w_in, dep):
    T, D = x2.shape
    tm = min(256, T)

    def body(x_ref, w_ref, o_ref, xt_ref):
        x = x_ref[...]
        xt_ref[...] = x.T.astype(BF16)
        xb = x.astype(BF16)
        for j in range(N_SEC):
            o_ref[j] = jnp.dot(xb, w_ref[:, j * D:(j + 1) * D], preferred_element_type=F32)

    return _call_after(
        dep, body, (x2, w_in), name="proj", grid=(T // tm,),
        in_specs=[pl.BlockSpec((tm, D), lambda i: (i, 0)), _resident((D, N_SEC * D))],
        out_specs=[pl.BlockSpec((N_SEC, tm, D), lambda i: (0, i, 0)), pl.BlockSpec((D, tm), lambda i: (0, i))],
        out_shape=[jax.ShapeDtypeStruct((N_SEC, T, D), F32), jax.ShapeDtypeStruct((D, T), BF16)],
        compiler_params=_params(("parallel",)))


def _chunk_cumsum(v, reverse=False):
    rows, lanes = v.shape
    x = v.reshape(rows // SUBLANES, SUBLANES, lanes)
    pos = lax.broadcasted_iota(jnp.int32, x.shape, 1)
    for sh in (1, 2, 4):
        if reverse:
            x = x + jnp.where(pos < SUBLANES - sh, pltpu.roll(x, SUBLANES - sh, 1), 0.0)
        else:
            x = x + jnp.where(pos >= sh, pltpu.roll(x, sh, 1), 0.0)
    x = x.reshape(rows // CHUNK, CHUNK // SUBLANES, SUBLANES, lanes)
    half = lax.broadcasted_iota(jnp.int32, x.shape, 1)
    if reverse:
        x = x + jnp.where(half == 0, x[:, 1:2, 0:1, :], 0.0)
    else:
        x = x + jnp.where(half == 1, x[:, 0:1, SUBLANES - 1:SUBLANES, :], 0.0)
    return x.reshape(rows, lanes)


def _hgrn_gates(q, f_pre, lb_logits):
    l0, l1 = lb_logits[0:1, :], lb_logits[1:2, :]
    mx = jnp.maximum(l0, l1)
    e0, e1 = jnp.exp(l0 - mx), jnp.exp(l1 - mx)
    lb = e0 / (e0 + e1)
    sq = _sigmoid(q)
    qf = q * sq * Q_SCALE
    sg = _sigmoid(f_pre)
    f = lb + (1.0 - lb) * sg
    k = 1.0 - f
    log_f = jnp.log(f)
    G = _chunk_cumsum(log_f)
    g_to_end = _chunk_cumsum(log_f, reverse=True) - log_f
    e_g = jnp.exp(G)
    e_ng = jnp.exp(-G)
    e_ge = jnp.exp(g_to_end)
    return dict(lb=lb, sq=sq, qf=qf, sg=sg, f=f, k=k, G=G, e_g=e_g, e_ng=e_ng, e_ge=e_ge,
                qd=qf * e_g, ki=k * e_ng, ke=k * e_ge, dec=jnp.exp(G + g_to_end))


def _intra_mask():
    r = lax.broadcasted_iota(jnp.int32, (GROUP, GROUP), 0)
    c = lax.broadcasted_iota(jnp.int32, (GROUP, GROUP), 1)
    return (r // CHUNK == c // CHUNK) & (c <= r)


def _chunk_outer(lhs_rows, rhs_b, out_scr, sb):
    lane = lax.broadcasted_iota(jnp.int32, (GROUP, GROUP), 1) // CHUNK
    for g in range(sb // GROUP):
        sl = slice(g * GROUP, (g + 1) * GROUP)
        lhs_t = lhs_rows[sl].T
        for cc in range(CH_PER_GROUP):
            masked = jnp.where(lane == cc, lhs_t, 0.0).astype(BF16)
            out_scr[g * CH_PER_GROUP + cc] = jnp.dot(masked, rhs_b[sl], preferred_element_type=F32)


def _hgrn_forward_blocks(cs, vs, st0s, sb, o_scr, kv_scr, st_scr, dec_scr):
    nc = sb // CHUNK
    n_str = len(cs)
    mask = _intra_mask()
    bf = []
    for i, (c, v) in enumerate(zip(cs, vs)):
        qd_b, ki_b, ke_b, v_b = (c["qd"].astype(BF16), c["ki"].astype(BF16), c["ke"].astype(BF16),
                                 v.astype(BF16))
        bf.append((qd_b, ki_b, ke_b, v_b))
        for g in range(sb // GROUP):
            sl = slice(g * GROUP, (g + 1) * GROUP)
            sc = lax.dot_general(qd_b[sl], ki_b[sl], NT_DIMS, preferred_element_type=F32)
            a = jnp.where(mask, sc, 0.0).astype(BF16)
            o_scr[i, sl, :] = jnp.dot(a, v_b[sl], preferred_element_type=F32)
        _chunk_outer(v, ke_b, kv_scr.at[i], sb)
        dec_scr[i] = c["dec"]

    def rec(n, sts):
        row = pl.ds(pl.multiple_of(n * CHUNK, CHUNK), 1)
        out = []
        for i in range(n_str):
            st_scr[i, n] = sts[i]
            out.append(sts[i] * dec_scr[i, row, :] + kv_scr[i, n])
        return tuple(out)

    ends = lax.fori_loop(0, nc, rec, tuple(st0s))

    for n in range(nc):
        rows = slice(n * CHUNK, (n + 1) * CHUNK)
        for i in range(n_str):
            o_scr[i, rows, :] += lax.dot_general(bf[i][0][rows], st_scr[i, n].astype(BF16), NT_DIMS,
                                                 preferred_element_type=F32)
    return ends, bf


def _hgrn_fwd(proj5, lb_logits, gn):
    _, Bl, S, D = proj5.shape
    H = D // HEAD
    sb = min(512, S)
    nsb = S // sb
    nc = sb // CHUNK

    def body(p_ref, lbl_ref, gn_ref, ain_ref, aint_ref, st0_ref, carry, o_scr, kv_scr, st_scr, dec_scr):
        @pl.when(pl.program_id(1) == 0)
        def _():
            carry[...] = jnp.zeros_like(carry)

        st0s = [carry[b] for b in range(Bl)]
        for b in range(Bl):
            st0_ref[b, 0, 0] = st0s[b]
        cs = [_hgrn_gates(p_ref[0, b], p_ref[1, b], lbl_ref[...]) for b in range(Bl)]
        ends, _ = _hgrn_forward_blocks(cs, [p_ref[2, b] for b in range(Bl)], st0s, sb,
                                       o_scr, kv_scr, st_scr, dec_scr)
        for b in range(Bl):
            carry[b] = ends[b]
            o = o_scr[b]
            rinv = lax.rsqrt(jnp.mean(o * o, axis=-1, keepdims=True) + RMS_EPS)
            ain = o * rinv * gn_ref[...] * _sigmoid(p_ref[3, b])
            ain_ref[b] = ain.astype(BF16)
            aint_ref[b] = ain.T.astype(BF16)

    return pl.pallas_call(
        body, name="hgrn_fwd", grid=(H, nsb),
        in_specs=[pl.BlockSpec((4, Bl, sb, HEAD), lambda h, s: (0, 0, s, h)),
                  pl.BlockSpec((2, HEAD), lambda h, s: (0, h)),
                  pl.BlockSpec((1, HEAD), lambda h, s: (0, h))],
        out_specs=[pl.BlockSpec((Bl, sb, HEAD), lambda h, s: (0, s, h)),
                   pl.BlockSpec((Bl, HEAD, sb), lambda h, s: (0, h, s)),
                   pl.BlockSpec((Bl, 1, 1, HEAD, HEAD), lambda h, s: (0, h, s, 0, 0))],
        out_shape=[jax.ShapeDtypeStruct((Bl, S, D), BF16), jax.ShapeDtypeStruct((Bl, D, S), BF16),
                   jax.ShapeDtypeStruct((Bl, H, nsb, HEAD, HEAD), F32)],
        scratch_shapes=[pltpu.VMEM((Bl, HEAD, HEAD), F32), pltpu.VMEM((Bl, sb, HEAD), F32),
                        pltpu.VMEM((Bl, nc, HEAD, HEAD), F32), pltpu.VMEM((Bl, nc, HEAD, HEAD), F32),
                        pltpu.VMEM((Bl, sb, HEAD), F32)],
        compiler_params=_params(("parallel", "arbitrary")),
    )(proj5, lb_logits, gn)


def _window_count(shape, g):
    pos = lax.broadcasted_iota(jnp.int32, shape, 0)
    return pos, jnp.minimum(pos + 1, jnp.left_shift(2, g)).astype(F32)


def _select_window(g, sums):
    return jnp.where(g == 0, sums[0], jnp.where(g == 1, sums[1], jnp.where(g == 2, sums[2], sums[3])))


def _pool_fwd(proj5, w_pool):
    _, Bl, S, D = proj5.shape
    pg = D // POOL_GROUPS

    def body(v_ref, w_ref, pooled_t_ref, bp_ref):
        g = pl.program_id(1)
        v = v_ref[0, 0]
        pos, cnt = _window_count(v.shape, g)
        cur, sums = v, []
        for sh in (1, 2, 4, 8):
            cur = cur + jnp.where(pos >= sh, pltpu.roll(cur, sh, 0), 0.0)
            sums.append(cur)
        pooled = _select_window(g, sums) / cnt - v
        pooled_t_ref[...] = pooled.T.astype(BF16)
        bp_ref[0] = jnp.dot(pooled.astype(BF16), w_ref[0], preferred_element_type=F32)

    return pl.pallas_call(
        body, name="pool_fwd", grid=(Bl, POOL_GROUPS),
        in_specs=[pl.BlockSpec((1, 1, S, pg), lambda b, g: (4, b, 0, g)),
                  pl.BlockSpec((1, pg, pg), lambda b, g: (g, 0, 0))],
        out_specs=[pl.BlockSpec((pg, S), lambda b, g: (g, b)),
                   pl.BlockSpec((1, S, pg), lambda b, g: (b, 0, g))],
        out_shape=[jax.ShapeDtypeStruct((D, Bl * S), BF16), jax.ShapeDtypeStruct((Bl, S, D), F32)],
        compiler_params=_params(("parallel", "parallel")),
    )(proj5, w_pool)


def _layer_norm_fwd(r):
    mu = jnp.mean(r, axis=-1, keepdims=True)
    d = r - mu
    rs = lax.rsqrt(jnp.mean(d * d, axis=-1, keepdims=True) + LN_EPS)
    return d * rs, rs


def _layer_norm_bwd(dy_g, xhat, rs):
    return rs * (dy_g - jnp.mean(dy_g, axis=-1, keepdims=True)
                 - xhat * jnp.mean(dy_g * xhat, axis=-1, keepdims=True))


def _mix_fwd(ain, proj, bp, x2, w_a, w_out, ps, g1, b1):
    T, D = x2.shape
    tm = min(256, T)

    def body(ain_ref, ga_ref, gb_ref, bp_ref, x_ref, wa_ref, wo_ref, ps_ref, g1_ref, b1_ref,
             a_ref, mgt_ref, xh_ref, rs_ref, x1b_ref, x1t_ref):
        a = jnp.dot(ain_ref[...], wa_ref[...], preferred_element_type=F32)
        a_ref[...] = a
        merged = _sigmoid(ga_ref[0]) * a + _sigmoid(gb_ref[0]) * (bp_ref[...] * ps_ref[...])
        mgt_ref[...] = merged.T.astype(BF16)
        r1 = ALPHA * x_ref[...] + jnp.dot(merged.astype(BF16), wo_ref[...], preferred_element_type=F32)
        xhat, rs = _layer_norm_fwd(r1)
        xh_ref[...] = xhat
        rs_ref[...] = rs
        x1 = xhat * g1_ref[...] + b1_ref[...]
        x1b_ref[...] = x1.astype(BF16)
        x1t_ref[...] = x1.T.astype(BF16)

    row = lambda i: (i, 0)
    col = lambda i: (0, i)
    full = lambda i: (0, 0)
    return pl.pallas_call(
        body, name="mix_fwd", grid=(T // tm,),
        in_specs=[pl.BlockSpec((tm, D), row),
                  pl.BlockSpec((1, tm, D), lambda i: (5, i, 0)),
                  pl.BlockSpec((1, tm, D), lambda i: (6, i, 0)),
                  pl.BlockSpec((tm, D), row), pl.BlockSpec((tm, D), row),
                  pl.BlockSpec((D, D), full), pl.BlockSpec((D, D), full),
                  pl.BlockSpec((1, D), full), pl.BlockSpec((1, D), full), pl.BlockSpec((1, D), full)],
        out_specs=[pl.BlockSpec((tm, D), row), pl.BlockSpec((D, tm), col), pl.BlockSpec((tm, D), row),
                   pl.BlockSpec((tm, 1), row), pl.BlockSpec((tm, D), row), pl.BlockSpec((D, tm), col)],
        out_shape=[jax.ShapeDtypeStruct((T, D), F32), jax.ShapeDtypeStruct((D, T), BF16),
                   jax.ShapeDtypeStruct((T, D), F32), jax.ShapeDtypeStruct((T, 1), F32),
                   jax.ShapeDtypeStruct((T, D), BF16), jax.ShapeDtypeStruct((D, T), BF16)],
        compiler_params=_params(("parallel",)),
    )(ain, proj, proj, bp, x2, w_a, w_out, ps, g1, b1)


def _mlp_fwd(x1b, w_up, w_down, xhat1, tgt, g1, b1, g2, b2):
    T, D = xhat1.shape
    FF = w_up.shape[1]
    tm = min(256, T)

    def body(x_ref, wu_ref, wd_ref, xh_ref, t_ref, g1_ref, b1_ref, g2_ref, b2_ref,
             hp_ref, h_ref, dr_ref, drb_ref, drt_ref, vec_ref):
        @pl.when(pl.program_id(0) == 0)
        def _():
            vec_ref[...] = jnp.zeros_like(vec_ref)

        xb = x_ref[...]
        x1 = xh_ref[...] * g1_ref[...] + b1_ref[...]
        r2 = ALPHA * x1
        for f in range(FF // D):
            cols = slice(f * D, (f + 1) * D)
            hp = jnp.dot(xb, wu_ref[:, cols], preferred_element_type=F32)
            hp_ref[:, cols] = hp
            h = jnp.square(jnp.maximum(hp, 0.0)).astype(BF16)
            h_ref[:, cols] = h
            r2 = r2 + jnp.dot(h, wd_ref[cols, :], preferred_element_type=F32)
        xhat2, rs2 = _layer_norm_fwd(r2)
        err = xhat2 * g2_ref[...] + b2_ref[...] - t_ref[...]
        dy = err / D
        vec_ref[5:6, :] += jnp.sum(dy * xhat2, axis=0, keepdims=True)
        vec_ref[6:7, :] += jnp.sum(dy, axis=0, keepdims=True)
        vec_ref[7:8, :] += jnp.sum(0.5 * err * err / D, axis=0, keepdims=True)
        dr = _layer_norm_bwd(dy * g2_ref[...], xhat2, rs2)
        dr_ref[...] = dr
        drb_ref[...] = dr.astype(BF16)
        drt_ref[...] = dr.T.astype(BF16)

    row = lambda i: (i, 0)
    full = lambda i: (0, 0)
    return pl.pallas_call(
        body, name="mlp_fwd", grid=(T // tm,),
        in_specs=[pl.BlockSpec((tm, D), row), _resident((D, FF)), _resident((FF, D)),
                  pl.BlockSpec((tm, D), row), pl.BlockSpec((tm, D), row),
                  pl.BlockSpec((1, D), full), pl.BlockSpec((1, D), full),
                  pl.BlockSpec((1, D), full), pl.BlockSpec((1, D), full)],
        out_specs=[pl.BlockSpec((tm, FF), row), pl.BlockSpec((tm, FF), row), pl.BlockSpec((tm, D), row),
                   pl.BlockSpec((tm, D), row), pl.BlockSpec((D, tm), lambda i: (0, i)),
                   pl.BlockSpec((8, D), full)],
        out_shape=[jax.ShapeDtypeStruct((T, FF), F32), jax.ShapeDtypeStruct((T, FF), BF16),
                   jax.ShapeDtypeStruct((T, D), F32), jax.ShapeDtypeStruct((T, D), BF16),
                   jax.ShapeDtypeStruct((D, T), BF16), jax.ShapeDtypeStruct((8, D), F32)],
        compiler_params=_params(("arbitrary",)),
    )(x1b, w_up, w_down, xhat1, tgt, g1, b1, g2, b2)


def _mlp_bwd(drb, dr, hp, w_up, w_down, xhat1, rs1, g1):
    T, D = dr.shape
    FF = hp.shape[1]
    tm = min(256, T)

    def body(drb_ref, dr_ref, hp_ref, wu_ref, wd_ref, xh_ref, rs_ref, g1_ref,
             dhp_ref, d1_ref, d1b_ref, vec_ref):
        @pl.when(pl.program_id(0) == 0)
        def _():
            vec_ref[...] = jnp.zeros_like(vec_ref)

        drb = drb_ref[...]
        dx1 = ALPHA * dr_ref[...]
        for f in range(FF // D):
            cols = slice(f * D, (f + 1) * D)
            dh = lax.dot_general(drb, wd_ref[cols, :], NT_DIMS, preferred_element_type=F32)
            dhp = (dh * (2.0 * jnp.maximum(hp_ref[:, cols], 0.0))).astype(BF16)
            dhp_ref[:, cols] = dhp
            dx1 = dx1 + lax.dot_general(dhp, wu_ref[:, cols], NT_DIMS, preferred_element_type=F32)
        xhat = xh_ref[...]
        vec_ref[3:4, :] += jnp.sum(dx1 * xhat, axis=0, keepdims=True)
        vec_ref[4:5, :] += jnp.sum(dx1, axis=0, keepdims=True)
        d1 = _layer_norm_bwd(dx1 * g1_ref[...], xhat, rs_ref[...])
        d1_ref[...] = d1
        d1b_ref[...] = d1.astype(BF16)

    row = lambda i: (i, 0)
    full = lambda i: (0, 0)
    return pl.pallas_call(
        body, name="mlp_bwd", grid=(T // tm,),
        in_specs=[pl.BlockSpec((tm, D), row), pl.BlockSpec((tm, D), row), pl.BlockSpec((tm, FF), row),
                  _resident((D, FF)), _resident((FF, D)),
                  pl.BlockSpec((tm, D), row), pl.BlockSpec((tm, 1), row), pl.BlockSpec((1, D), full)],
        out_specs=[pl.BlockSpec((tm, FF), row), pl.BlockSpec((tm, D), row), pl.BlockSpec((tm, D), row),
                   pl.BlockSpec((8, D), full)],
        out_shape=[jax.ShapeDtypeStruct((T, FF), BF16), jax.ShapeDtypeStruct((T, D), F32),
                   jax.ShapeDtypeStruct((T, D), BF16), jax.ShapeDtypeStruct((8, D), F32)],
        compiler_params=_params(("arbitrary",)),
    )(drb, dr, hp, w_up, w_down, xhat1, rs1, g1)


def _dw(name, a_t, b, n_j, a_spec, b_spec, o_shape, o_block, o_map, transpose_out=False, dep=None,
        into=None):
    def body(*refs):
        a_ref, b_ref, o_ref, ob_ref = refs[0], refs[1], refs[-2], refs[-1]
        b_val = b_ref[0] if len(b_ref.shape) == 3 else b_ref[...]
        if len(a_ref.shape) == 3:
            seq = a_ref.shape[2]
            p = sum(jnp.dot(a_ref[i], b_val[i * seq:(i + 1) * seq], preferred_element_type=F32)
                    for i in range(a_ref.shape[0]))
        else:
            p = jnp.dot(a_ref[...], b_val, preferred_element_type=F32)
        if transpose_out:
            p = p.T
        p = p.reshape(o_ref.shape)
        o_ref[...] = p
        ob_ref[...] = p.astype(BF16)

    o_spec = pl.BlockSpec(o_block, o_map)
    kw = dict(name=name, grid=(n_j,), in_specs=[a_spec, b_spec], out_specs=[o_spec, o_spec],
              out_shape=[jax.ShapeDtypeStruct(o_shape, F32), jax.ShapeDtypeStruct(o_shape, BF16)],
              compiler_params=_params(("parallel",)))
    args = (a_t, b)
    if into is not None:
        kw["in_specs"] = kw["in_specs"] + [pl.BlockSpec(memory_space=pl.ANY)] * 2
        kw["input_output_aliases"] = {2: 0, 3: 1}
        args = args + tuple(into)
    if dep is None:
        return pl.pallas_call(body, **kw)(*args)
    return _call_after(dep, body, args, **kw)


def _mix_bwd(d1b, proj, a, bp, w_a, w_out, w_pool, ps, dep):
    T, D = a.shape
    tm = min(256, T)
    pg = D // POOL_GROUPS

    def body(d1b_ref, ga_ref, gb_ref, a_ref, bp_ref, wa_ref, wo_ref, wp_ref, ps_ref,
             da_ref, dbp_ref, dain_ref, dpl_ref, dg_ref, vec_ref):
        @pl.when(pl.program_id(0) == 0)
        def _():
            vec_ref[...] = jnp.zeros_like(vec_ref)

        dm = lax.dot_general(d1b_ref[...], wo_ref[...], NT_DIMS, preferred_element_type=F32)
        sa, sg = _sigmoid(ga_ref[0]), _sigmoid(gb_ref[0])
        bp_v, ps_v = bp_ref[...], ps_ref[...]
        da = (dm * sa).astype(BF16)
        db = dm * sg
        dg_ref[0] = (dm * a_ref[...] * sa * (1.0 - sa)).astype(BF16)
        dg_ref[1] = (dm * (bp_v * ps_v) * sg * (1.0 - sg)).astype(BF16)
        vec_ref[2:3, :] += jnp.sum(db * bp_v, axis=0, keepdims=True)
        dbp = (db * ps_v).astype(BF16)
        da_ref[...] = da
        dbp_ref[...] = dbp
        dain_ref[...] = lax.dot_general(da, wa_ref[...], NT_DIMS, preferred_element_type=F32)
        for g in range(POOL_GROUPS):
            cols = slice(g * pg, (g + 1) * pg)
            dpl_ref[:, cols] = lax.dot_general(dbp[:, cols], wp_ref[g], NT_DIMS,
                                               preferred_element_type=F32)

    row = lambda i: (i, 0)
    full = lambda i: (0, 0)
    return _call_after(
        dep, body, (d1b, proj, proj, a, bp, w_a, w_out, w_pool, ps), name="mix_bwd", grid=(T // tm,),
        in_specs=[pl.BlockSpec((tm, D), row),
                  pl.BlockSpec((1, tm, D), lambda i: (5, i, 0)),
                  pl.BlockSpec((1, tm, D), lambda i: (6, i, 0)),
                  pl.BlockSpec((tm, D), row), pl.BlockSpec((tm, D), row),
                  pl.BlockSpec((D, D), full), pl.BlockSpec((D, D), full),
                  pl.BlockSpec((POOL_GROUPS, pg, pg), lambda i: (0, 0, 0)),
                  pl.BlockSpec((1, D), full)],
        out_specs=[pl.BlockSpec((tm, D), row), pl.BlockSpec((tm, D), row),
                   pl.BlockSpec((tm, D), row), pl.BlockSpec((tm, D), row),
                   pl.BlockSpec((2, tm, D), lambda i: (0, i, 0)),
                   pl.BlockSpec((8, D), full)],
        out_shape=[jax.ShapeDtypeStruct((T, D), BF16), jax.ShapeDtypeStruct((T, D), BF16),
                   jax.ShapeDtypeStruct((T, D), F32), jax.ShapeDtypeStruct((T, D), F32),
                   jax.ShapeDtypeStruct((2, T, D), BF16), jax.ShapeDtypeStruct((8, D), F32)],
        compiler_params=_params(("arbitrary",)))


def _pool_bwd(dpooled3, dep):
    Bl, S, D = dpooled3.shape
    pg = D // POOL_GROUPS

    def body(dp_ref, dv_ref):
        g = pl.program_id(1)
        dp = dp_ref[0]
        pos, cnt = _window_count(dp.shape, g)
        cur, sums = dp / cnt, []
        for sh in (1, 2, 4, 8):
            cur = cur + jnp.where(pos < S - sh, pltpu.roll(cur, S - sh, 0), 0.0)
            sums.append(cur)
        dv_ref[0] = (_select_window(g, sums) - dp).astype(BF16)

    spec = pl.BlockSpec((1, S, pg), lambda b, g: (b, 0, g))
    return _call_after(
        dep, body, (dpooled3,), name="pool_bwd", grid=(Bl, POOL_GROUPS), in_specs=[spec], out_specs=spec,
        out_shape=jax.ShapeDtypeStruct((Bl, S, D), BF16),
        compiler_params=_params(("parallel", "parallel")))


def _hgrn_bwd(proj5, lb_logits, gn, dain3, st0_all):
    _, Bl, S, D = proj5.shape
    H = D // HEAD
    sb = min(512, S)
    nsb = S // sb
    nc = sb // CHUNK
    streams = range(Bl)

    def body(p_ref, lbl_ref, gn_ref, dain_ref, st0_ref, d_ref, vec_ref,
             dcarry, o_scr, kv_scr, st_scr, dst_scr, dec_scr, dvi_scr, dke_scr, dqi_scr):
        s = pl.program_id(1)

        @pl.when(s == 0)
        def _():
            dcarry[...] = jnp.zeros_like(dcarry)
            vec_ref[...] = jnp.zeros_like(vec_ref)

        qs, vs, ogs = [p_ref[0, b] for b in streams], [p_ref[2, b] for b in streams], [p_ref[3, b] for b in streams]
        cs = [_hgrn_gates(qs[b], p_ref[1, b], lbl_ref[...]) for b in streams]
        _, bf = _hgrn_forward_blocks(cs, vs, [st0_ref[b, 0, 0] for b in streams], sb,
                                     o_scr, kv_scr, st_scr, dec_scr)
        mask = _intra_mask()
        gn_v = gn_ref[...]
        keep = []
        for b in streams:
            qd_b, ki_b, ke_b, v_b = bf[b]
            o = o_scr[b]
            rinv = lax.rsqrt(jnp.mean(o * o, axis=-1, keepdims=True) + RMS_EPS)
            on = o * rinv
            so = _sigmoid(ogs[b])
            dain = dain_ref[b]
            vec_ref[1:2, :] += jnp.sum(dain * on * so, axis=0, keepdims=True)
            d_og = dain * on * gn_v * so * (1.0 - so)
            d_on = dain * gn_v * so
            do = rinv * (d_on - on * jnp.mean(d_on * on, axis=-1, keepdims=True))
            do_b = do.astype(BF16)
            dv_parts, dqd_parts, dki_parts = [], [], []
            for g in range(sb // GROUP):
                sl = slice(g * GROUP, (g + 1) * GROUP)
                sc = lax.dot_general(qd_b[sl], ki_b[sl], NT_DIMS, preferred_element_type=F32)
                a = jnp.where(mask, sc, 0.0).astype(BF16)
                da = lax.dot_general(do_b[sl], v_b[sl], NT_DIMS, preferred_element_type=F32)
                da = jnp.where(mask, da, 0.0).astype(BF16)
                dv_parts.append(lax.dot_general(a, do_b[sl], TN_DIMS, preferred_element_type=F32))
                dqd_parts.append(jnp.dot(da, ki_b[sl], preferred_element_type=F32))
                dki_parts.append(lax.dot_general(da, qd_b[sl], TN_DIMS, preferred_element_type=F32))
            keep.append(dict(d_og=d_og, do_b=do_b, dv_intra=jnp.concatenate(dv_parts, axis=0),
                             dqd_intra=jnp.concatenate(dqd_parts, axis=0),
                             dki=jnp.concatenate(dki_parts, axis=0)))
            _chunk_outer(do, qd_b, kv_scr.at[b], sb)

        def rrec(i, dsts):
            n = nc - 1 - i
            row = pl.ds(pl.multiple_of(n * CHUNK, CHUNK), 1)
            out = []
            for b in streams:
                dst_scr[b, n] = dsts[b]
                out.append(dsts[b] * dec_scr[b, row, :] + kv_scr[b, n])
            return tuple(out)

        ends = lax.fori_loop(0, nc, rrec, tuple(dcarry[b] for b in streams))
        for b in streams:
            dcarry[b] = ends[b]
        for n in range(nc):
            rows = slice(n * CHUNK, (n + 1) * CHUNK)
            for b in streams:
                qd_b, ki_b, ke_b, v_b = bf[b]
                dst_b = dst_scr[b, n].astype(BF16)
                dvi_scr[b, rows, :] = lax.dot_general(ke_b[rows], dst_b, NT_DIMS, preferred_element_type=F32)
                dke_scr[b, rows, :] = jnp.dot(v_b[rows], dst_b, preferred_element_type=F32)
                dqi_scr[b, rows, :] = jnp.dot(keep[b]["do_b"][rows], st_scr[b, n].astype(BF16),
                                              preferred_element_type=F32)
        for b in streams:
            c, k = cs[b], keep[b]
            ddec = jnp.sum(dst_scr[b] * st_scr[b], axis=1)
            dgl = jnp.broadcast_to(ddec[:, None, :], (nc, CHUNK, HEAD)).reshape(sb, HEAD) * c["dec"]
            dqd = k["dqd_intra"] + dqi_scr[b]
            dke = dke_scr[b]
            dki = k["dki"]
            t_ke = dke * c["ke"]
            dG = dqd * c["qd"] - dki * c["ki"] - t_ke
            dgl = dgl + _chunk_cumsum(t_ke) + _chunk_cumsum(t_ke, reverse=True) - t_ke
            dlogf = _chunk_cumsum(dG, reverse=True) + dgl
            dk = dki * c["e_ng"] + dke * c["e_ge"]
            df = dlogf / c["f"] - dk
            sg, sq, lb, q = c["sg"], c["sq"], c["lb"], qs[b]
            vec_ref[0:1, :] += jnp.sum(df * (1.0 - sg), axis=0, keepdims=True)
            d_ref[0, b] = (dqd * c["e_g"] * Q_SCALE * (sq + q * sq * (1.0 - sq))).astype(BF16)
            d_ref[1, b] = (df * (1.0 - lb) * sg * (1.0 - sg)).astype(BF16)
            d_ref[2, b] = (k["dv_intra"] + dvi_scr[b]).astype(BF16)
            d_ref[3, b] = k["d_og"].astype(BF16)

    rev = lambda s: nsb - 1 - s
    big = pltpu.VMEM((Bl, nc, HEAD, HEAD), F32)
    rows_f32 = pltpu.VMEM((Bl, sb, HEAD), F32)
    return pl.pallas_call(
        body, name="hgrn_bwd", grid=(H, nsb),
        in_specs=[pl.BlockSpec((4, Bl, sb, HEAD), lambda h, s: (0, 0, rev(s), h)),
                  pl.BlockSpec((2, HEAD), lambda h, s: (0, h)),
                  pl.BlockSpec((1, HEAD), lambda h, s: (0, h)),
                  pl.BlockSpec((Bl, sb, HEAD), lambda h, s: (0, rev(s), h)),
                  pl.BlockSpec((Bl, 1, 1, HEAD, HEAD), lambda h, s: (0, h, rev(s), 0, 0))],
        out_specs=[pl.BlockSpec((4, Bl, sb, HEAD), lambda h, s: (0, 0, rev(s), h)),
                   pl.BlockSpec((8, HEAD), lambda h, s: (0, h))],
        out_shape=[jax.ShapeDtypeStruct((4, Bl, S, D), BF16), jax.ShapeDtypeStruct((8, D), F32)],
        scratch_shapes=[pltpu.VMEM((Bl, HEAD, HEAD), F32), rows_f32, big, big, big, rows_f32,
                        rows_f32, rows_f32, rows_f32],
        compiler_params=_params(("parallel", "arbitrary")),
    )(proj5, lb_logits, gn, dain3, st0_all)


def _dx(d1, dh4, dpv, dg2, w_in, dep):
    T, D = d1.shape
    tm = min(256, T)

    def body(d1_ref, dh_ref, dp_ref, dg_ref, w_ref, o_ref):
        blocks = [dh_ref[0], dh_ref[1], dh_ref[2], dh_ref[3], dp_ref[...], dg_ref[0], dg_ref[1]]
        acc = ALPHA * d1_ref[...]
        for j, blk in enumerate(blocks):
            acc = acc + lax.dot_general(blk, w_ref[:, j * D:(j + 1) * D], NT_DIMS, preferred_element_type=F32)
        o_ref[...] = acc

    row = lambda i: (i, 0)
    return _call_after(
        dep, body, (d1, dh4, dpv, dg2, w_in), name="dx", grid=(T // tm,),
        in_specs=[pl.BlockSpec((tm, D), row), pl.BlockSpec((4, tm, D), lambda i: (0, i, 0)),
                  pl.BlockSpec((tm, D), row), pl.BlockSpec((2, tm, D), lambda i: (0, i, 0)),
                  _resident((D, N_SEC * D))],
        out_specs=pl.BlockSpec((tm, D), row),
        out_shape=jax.ShapeDtypeStruct((T, D), F32),
        compiler_params=_params(("parallel",)))


def _dw_in(x_t, dh4, dpv, dg2, dep):
    D, T = x_t.shape
    shape = (D, N_SEC * D)
    x_spec = _resident((D, T))

    def part(name, b, n_j, first_sec, into, dep):
        b_spec = (pl.BlockSpec((1, T, D), lambda j: (j, 0, 0)) if b.ndim == 3
                  else pl.BlockSpec((T, D), lambda j: (0, 0)))
        return _dw(name, x_t, b, n_j, x_spec, b_spec, shape, (D, D), lambda j: (0, first_sec + j),
                   dep=dep, into=into)

    gates = part("dw_in_gates", dg2, 2, 5, None, dep)
    pool = part("dw_in_pool", dpv, 1, 4, gates, None)
    return part("dw_in_rec", dh4, 4, 0, pool, None)


def _adam_shard(name, me_arr, grad, land, layout, w, m, v):
    shape = layout.shape
    n_split = 4
    blk = (shape[0] // n_split,) + shape[1:]
    zeros = (0,) * (len(shape) - 1)

    def body(me_ref, g_ref, r_ref, w_ref, m_ref, v_ref, g_out, d_out, m_out, v_out):
        g = g_ref[...]
        for k in range(N_DEV - 1):
            g = g + r_ref[k].astype(F32)
        d, m2, v2 = _adamw(w_ref[...], g, m_ref[...], v_ref[...])
        g_out[...] = g
        d_out[...] = d
        m_out[...] = m2
        v_out[...] = v2

    def own(i, me_ref):
        bi = layout.block_index(me_ref[0])
        return (bi[0] * n_split + i,) + tuple(bi[1:]) if layout.kind == "row" else (i,) + tuple(bi[1:])

    plain = pl.BlockSpec(blk, lambda i, me_ref: (i,) + zeros)
    grid_spec = pltpu.PrefetchScalarGridSpec(
        num_scalar_prefetch=1, grid=(n_split,),
        in_specs=[pl.BlockSpec(blk, own),
                  pl.BlockSpec((N_DEV - 1,) + blk, lambda i, me_ref: (0, i) + zeros),
                  plain, plain, plain],
        out_specs=[plain] * 4)
    return pl.pallas_call(
        body, name=name, grid_spec=grid_spec,
        out_shape=[jax.ShapeDtypeStruct(shape, F32)] * 4,
        compiler_params=_params(("parallel",)),
    )(me_arr, grad, land, w, m, v)


def _vec_allreduce(vec):
    D = vec.shape[1]

    def body(vec_ref, tot_ref, gat, send_sems, recv_sems):
        x, y, c = _me()
        me = 4 * x + 2 * y + c
        gat[me] = vec_ref[...]
        copies = []
        for k in range(1, N_DEV):
            cp = pltpu.make_async_remote_copy(
                src_ref=vec_ref, dst_ref=gat.at[me], send_sem=send_sems.at[k - 1],
                recv_sem=recv_sems.at[k - 1], device_id=_peer(k, x, y, c), device_id_type=MESH)
            cp.start()
            copies.append(cp)
        for cp in copies:
            cp.wait()
        tot = gat[0]
        for d in range(1, N_DEV):
            tot = tot + gat[d]
        tot_ref[...] = tot

    vm = pl.BlockSpec(memory_space=pltpu.VMEM)
    return pl.pallas_call(
        body, name="vec_allreduce", out_shape=jax.ShapeDtypeStruct(vec.shape, F32),
        in_specs=[vm], out_specs=vm,
        scratch_shapes=[pltpu.VMEM((N_DEV, 8, D), F32), pltpu.SemaphoreType.DMA((N_DEV - 1,)),
                        pltpu.SemaphoreType.DMA((N_DEV - 1,))],
    )(vec)


def _vec_adam(tot, small_w, small_m, small_v):
    n = len(small_w)

    def body(*refs):
        tot = refs[0][...]
        ws, ms, vs = refs[1:1 + n], refs[1 + n:1 + 2 * n], refs[1 + 2 * n:1 + 3 * n]
        outs = refs[1 + 3 * n:]
        loss_ref, g_out, d_out = outs[0], outs[1:1 + n], outs[1 + n:1 + 2 * n]
        m_out, v_out = outs[1 + 2 * n:1 + 3 * n], outs[1 + 3 * n:1 + 4 * n]
        loss_ref[...] = jnp.broadcast_to(jnp.sum(tot[7:8, :], axis=1, keepdims=True), loss_ref.shape)
        lbl = ws[0][...]
        mx = jnp.maximum(lbl[0:1, :], lbl[1:2, :])
        e0, e1 = jnp.exp(lbl[0:1, :] - mx), jnp.exp(lbl[1:2, :] - mx)
        p0 = e0 / (e0 + e1)
        dl0 = tot[0:1, :] * p0 * (1.0 - p0)
        grads = [jnp.concatenate([dl0, -dl0], axis=0)] + [tot[r:r + 1, :] for r in range(1, n)]
        for i in range(n):
            d, m2, v2 = _adamw(ws[i][...], grads[i], ms[i][...], vs[i][...])
            g_out[i][...] = grads[i]
            d_out[i][...] = d
            m_out[i][...] = m2
            v_out[i][...] = v2

    vm = pl.BlockSpec(memory_space=pltpu.VMEM)
    shapes = [jax.ShapeDtypeStruct(w.shape, F32) for w in small_w]
    return pl.pallas_call(
        body, name="vec_adam",
        out_shape=[jax.ShapeDtypeStruct((1, 128), F32)] + shapes * 4,
        in_specs=[vm] * (1 + 3 * n), out_specs=[vm] * (1 + 4 * n),
    )(tot, *small_w, *small_m, *small_v)


def kernel(x, w_in, lb_logits, hgrn_norm_g, w_a, w_pool, pool_scale, w_out, ln1_g, ln1_b, w_up, w_down, ln2_g, ln2_b, loss_target, m_w_in, m_lb_logits, m_hgrn_norm_g, m_w_a, m_w_pool, m_pool_scale, m_w_out, m_ln1_g, m_ln1_b, m_w_up, m_w_down, m_ln2_g, m_ln2_b, v_w_in, v_lb_logits, v_hgrn_norm_g, v_w_a, v_w_pool, v_pool_scale, v_w_out, v_ln1_g, v_ln1_b, v_w_up, v_w_down, v_ln2_g, v_ln2_b):
    Bl, S, D = x.shape
    T = Bl * S
    pg = D // POOL_GROUPS
    x2 = x.reshape(T, D)
    tgt = loss_target.reshape(T, D)
    me = 4 * lax.axis_index("x") + 2 * lax.axis_index("y") + lax.axis_index("c")
    me_arr = jnp.reshape(me, (1,)).astype(jnp.int32)

    names = ["w_in", "w_a", "w_pool", "w_out", "w_up", "w_down"]
    big_w = dict(zip(names, [w_in[0], w_a[0], w_pool[0], w_out[0], w_up[0], w_down[0]]))
    big_m = dict(zip(names, [m_w_in[0], m_w_a[0], m_w_pool[0], m_w_out[0], m_w_up[0], m_w_down[0]]))
    big_v = dict(zip(names, [v_w_in[0], v_w_a[0], v_w_pool[0], v_w_out[0], v_w_up[0], v_w_down[0]]))
    kinds = dict(w_in="col", w_a="row", w_pool="pool", w_out="row", w_up="col", w_down="row")
    lay = {nm: _Sharded(kinds[nm], big_w[nm].shape) for nm in names}
    wb = {nm: big_w[nm].astype(BF16) for nm in names}

    (w_in_f,) = _all_gather("ag_w_in", [wb["w_in"]], [lay["w_in"]])
    rest = names[1:]
    ag_rest = _exchange_start("ag_rest", [wb[nm] for nm in rest],
                              [lax.empty(lay[nm].full_shape, BF16) for nm in rest],
                              src_at=lambda w, ref, peer: ref,
                              dst_at=lambda w, ref, mine, k: lay[rest[w]].at(ref, mine), after=w_in_f, own=True)

    proj, x_t = _proj(x2, w_in_f, ag_rest["token"])
    proj5 = proj.reshape(N_SEC, Bl, S, D)
    ain3, ain_t, st0_all = _hgrn_fwd(proj5, lb_logits, hgrn_norm_g)
    w_a_f, w_pool_f, w_out_f, w_up_f, w_down_f = _exchange_wait(ag_rest, ain3)
    pooled_t, bp3 = _pool_fwd(proj5, w_pool_f)
    ain, bp = ain3.reshape(T, D), bp3.reshape(T, D)
    a, merged_t, xhat1, rs1, x1b, x1_t = _mix_fwd(ain, proj, bp, x2, w_a_f, w_out_f, pool_scale, ln1_g, ln1_b)
    hp, h, dr2, dr2b, dr2_t, vec_mlp = _mlp_fwd(x1b, w_up_f, w_down_f, xhat1, tgt, ln1_g, ln1_b, ln2_g, ln2_b)

    def scatter_start(name, nms, grads_b, after):
        lands = [lax.empty((N_DEV - 1,) + lay[nm].shape, BF16) for nm in nms]
        return _exchange_start(name, grads_b, lands,
                               src_at=lambda w, ref, peer: lay[nms[w]].at(ref, peer),
                               dst_at=lambda w, ref, mine, k: ref.at[k - 1], after=after)

    dhp, dr1, dr1b, vec_ln1 = _mlp_bwd(dr2b, dr2, hp, w_up_f, w_down_f, xhat1, rs1, ln1_g)
    FF = 4 * D
    whole_t = _resident((D, T))
    gw, gwb = {}, {}
    gw["w_down"], gwb["w_down"] = _dw(
        "dw_down", dr2_t, h, FF // D, whole_t, pl.BlockSpec((T, D), lambda j: (0, j)),
        (FF, D), (D, D), lambda j: (j, 0), transpose_out=True)
    rs_down = scatter_start("rs_w_down", ["w_down"], [gwb["w_down"]], gw["w_down"])
    gw["w_up"], gwb["w_up"] = _dw(
        "dw_up", x1_t, dhp, FF // D, whole_t, pl.BlockSpec((T, D), lambda j: (0, j)),
        (D, FF), (D, D), lambda j: (0, j), dep=rs_down["token"])
    rs_up = scatter_start("rs_w_up", ["w_up"], [gwb["w_up"]], gw["w_up"])
    da_b, dbp_b, dain, dpooled, dg2, vec_mix = _mix_bwd(dr1b, proj, a, bp, w_a_f, w_out_f, w_pool_f, pool_scale,
                                                        rs_up["token"])
    dpv = _pool_bwd(dpooled.reshape(Bl, S, D), rs_up["token"]).reshape(T, D)
    dh4, vec_hgrn = _hgrn_bwd(proj5, lb_logits, hgrn_norm_g, dain.reshape(Bl, S, D), st0_all)
    dh4 = dh4.reshape(4, T, D)
    gw["w_in"], gwb["w_in"] = _dw_in(x_t, dh4, dpv, dg2, rs_up["token"])
    rs_in = scatter_start("rs_w_in", ["w_in"], [gwb["w_in"]], gw["w_in"])
    whole_b = pl.BlockSpec((T, D), lambda j: (0, 0))
    gw["w_out"], gwb["w_out"] = _dw("dw_out", merged_t, dr1b, 1, whole_t, whole_b, (D, D), (D, D), lambda j: (0, 0),
                                    dep=rs_in["token"])
    gw["w_a"], gwb["w_a"] = _dw("dw_a", ain_t, da_b, 1, _resident((Bl, D, S)), whole_b, (D, D), (D, D),
                                lambda j: (0, 0), dep=rs_in["token"])
    gw["w_pool"], gwb["w_pool"] = _dw(
        "dw_pool", pooled_t, dbp_b, POOL_GROUPS, pl.BlockSpec((pg, T), lambda j: (j, 0)),
        pl.BlockSpec((T, pg), lambda j: (0, j)), (POOL_GROUPS, pg, pg), (1, pg, pg), lambda j: (j, 0, 0),
        dep=rs_in["token"])
    mid = ["w_out", "w_a", "w_pool"]
    rs_mid = scatter_start("rs_w_mid", mid, [gwb[nm] for nm in mid], gw["w_pool"])
    grad_x2 = _dx(dr1, dh4, dpv, dg2, w_in_f, rs_mid["token"])
    grad_x = grad_x2.reshape(Bl, S, D)

    vec = vec_mlp + vec_ln1 + vec_mix + vec_hgrn
    small_names = ["lb_logits", "hgrn_norm_g", "pool_scale", "ln1_g", "ln1_b", "ln2_g", "ln2_b"]
    small_w = [lb_logits, hgrn_norm_g, pool_scale, ln1_g, ln1_b, ln2_g, ln2_b]
    small_m = [m_lb_logits, m_hgrn_norm_g, m_pool_scale, m_ln1_g, m_ln1_b, m_ln2_g, m_ln2_b]
    small_v = [v_lb_logits, v_hgrn_norm_g, v_pool_scale, v_ln1_g, v_ln1_b, v_ln2_g, v_ln2_b]
    res = _vec_adam(_vec_allreduce(vec), small_w, small_m, small_v)
    loss = res[0][0, 0]
    n = len(small_w)
    small = {nm: (res[1 + i], res[1 + n + i], res[1 + 2 * n + i], res[1 + 3 * n + i])
             for i, nm in enumerate(small_names)}

    big, last = {}, grad_x2
    for pend, nms in ((rs_down, ["w_down"]), (rs_up, ["w_up"]), (rs_mid, mid), (rs_in, ["w_in"])):
        for nm, land in zip(nms, _exchange_wait(pend, last)):
            outs = _adam_shard("adam_" + nm, me_arr, gw[nm], land, lay[nm], big_w[nm], big_m[nm], big_v[nm])
            big[nm] = tuple(t[None] for t in outs)
            last = outs[0]

    order = ["w_in", "lb_logits", "hgrn_norm_g", "w_a", "w_pool", "pool_scale", "w_out", "ln1_g", "ln1_b",
             "w_up", "w_down", "ln2_g", "ln2_b"]
    allp = {**big, **small}
    out = [loss, grad_x]
    for part in range(4):
        out += [allp[nm][part] for nm in order]
    return tuple(out)
```

```python
import jax
import jax.numpy as jnp
from jax import lax
from jax.experimental import pallas as pl
from jax.experimental.pallas import tpu as pltpu

F32 = jnp.float32
BF16 = jnp.bfloat16
MESH = pl.DeviceIdType.MESH

N_DEV = 8
HEAD = 128
CHUNK = 16
SUBLANES = 8
GROUP = 128
CH_PER_GROUP = GROUP // CHUNK
N_SEC = 7
POOL_GROUPS = 4
ALPHA = (2.0 * 1) ** 0.25
LN_EPS = 1e-5
RMS_EPS = 1e-6
Q_SCALE = HEAD ** -0.5
ADAM_LR = 0.001
ADAM_B1 = 0.9
ADAM_B2 = 0.999
ADAM_EPS = 1e-08
ADAM_WD = 0.01
ADAM_STEP = 10
VMEM_LIMIT = 56 << 20

NT_DIMS = (((1,), (1,)), ((), ()))
TN_DIMS = (((0,), (0,)), ((), ()))


def _params(sem=None):
    kw = dict(vmem_limit_bytes=VMEM_LIMIT)
    if sem is not None:
        kw["dimension_semantics"] = sem
    return pltpu.CompilerParams(**kw)


def _me():
    return lax.axis_index("x"), lax.axis_index("y"), lax.axis_index("c")


def _sigmoid(v):
    return jax.nn.sigmoid(v)


def _adamw(w, g, m, v):
    m = ADAM_B1 * m + (1.0 - ADAM_B1) * g
    v = ADAM_B2 * v + (1.0 - ADAM_B2) * jnp.square(g)
    m_hat = m / (1.0 - ADAM_B1 ** ADAM_STEP)
    v_hat = v / (1.0 - ADAM_B2 ** ADAM_STEP)
    delta = -ADAM_LR * (m_hat / (jnp.sqrt(v_hat) + ADAM_EPS) + ADAM_WD * w)
    return delta, m, v


class _Sharded:
    def __init__(self, kind, shard_shape):
        self.kind, self.shape = kind, tuple(shard_shape)

    @property
    def full_shape(self):
        r = self.shape
        if self.kind == "row":
            return (N_DEV * r[0],) + r[1:]
        return (r[0], N_DEV * r[1]) + r[2:]

    def at(self, ref, d):
        if self.kind == "col":
            n = self.shape[1]
            return ref.at[:, pl.ds(pl.multiple_of(d * n, 128), n)]
        if self.kind == "row":
            n = self.shape[0]
            return ref.at[pl.ds(pl.multiple_of(d * n, 16), n), :]
        n = self.shape[1]
        return ref.at[:, pl.ds(pl.multiple_of(d * n, 16), n), :]

    def block_index(self, d):
        return {"col": (0, d), "row": (d, 0), "pool": (0, d, 0)}[self.kind]


def _peer(k, x, y, c):
    return (1 - x if k & 4 else x, 1 - y if k & 2 else y, 1 - c if k & 1 else c)


def _all_gather(name, shards, layouts):
    nw = len(shards)

    def body(*refs):
        ins, outs = refs[:nw], refs[nw:2 * nw]
        send_sems, recv_sems, local_sems = refs[2 * nw:]
        x, y, c = _me()
        me = (x, y, c)
        sibling = (x, y, 1 - c)
        chips = [(1 - x, y), (x, 1 - y), (1 - x, 1 - y)]

        def copy(w, k, block, to, src=None):
            px, py, pc = block
            dst = layouts[w].at(outs[w], 4 * px + 2 * py + pc)
            return pltpu.make_async_remote_copy(
                src_ref=dst if src is None else src, dst_ref=dst,
                send_sem=send_sems.at[w, k], recv_sem=recv_sems.at[w, k],
                device_id=to, device_id_type=MESH)

        def place(w):
            mine = pltpu.make_async_copy(ins[w], layouts[w].at(outs[w], 4 * x + 2 * y + c), local_sems.at[w])
            mine.start()
            return mine

        first = []
        for w in range(nw):
            first.append(copy(w, 0, me, sibling, src=ins[w]))
            first += [copy(w, 1 + j, me, (*chip, c), src=ins[w]) for j, chip in enumerate(chips)]
        for cp in first:
            cp.start()
        local = [place(w) for w in range(nw)]
        passed = []
        for w in range(nw):
            for j, chip in enumerate(chips):
                copy(w, 1 + j, (*chip, c), me).wait_recv()
                fwd = copy(w, 4 + j, (*chip, c), sibling)
                fwd.start()
                passed.append(fwd)
        for w in range(nw):
            copy(w, 0, sibling, me).wait_recv()
            for j, chip in enumerate(chips):
                copy(w, 4 + j, (*chip, 1 - c), me).wait_recv()
        for cp in first + passed:
            cp.wait_send()
        for cp in local:
            cp.wait()

    any_spec = pl.BlockSpec(memory_space=pl.ANY)
    return pl.pallas_call(
        body, name=name,
        out_shape=[jax.ShapeDtypeStruct(l.full_shape, s.dtype) for s, l in zip(shards, layouts)],
        in_specs=[any_spec] * nw, out_specs=[any_spec] * nw,
        scratch_shapes=[pltpu.SemaphoreType.DMA((nw, 7)), pltpu.SemaphoreType.DMA((nw, 7)),
                        pltpu.SemaphoreType.DMA((nw,))],
    )(*shards)


HBM_SPEC = pl.BlockSpec(memory_space=pltpu.HBM)
SEM_SPEC = pl.BlockSpec(memory_space=pltpu.SEMAPHORE)
DATAFLOW = pltpu.SideEffectType.DATAFLOW_SIDE_EFFECTING


def _exchange_copies(srcs, lands, send_sems, recv_sems, src_at, dst_at):
    x, y, c = _me()
    me = 4 * x + 2 * y + c
    copies = []
    for w in range(len(srcs)):
        for k in range(1, N_DEV):
            px, py, pc = _peer(k, x, y, c)
            copies.append(pltpu.make_async_remote_copy(
                src_ref=src_at(w, srcs[w], 4 * px + 2 * py + pc), dst_ref=dst_at(w, lands[w], me, k),
                send_sem=send_sems.at[w * (N_DEV - 1) + k - 1], recv_sem=recv_sems.at[w * (N_DEV - 1) + k - 1],
                device_id=(px, py, pc), device_id_type=MESH))
    return copies


def _own_copies(srcs, lands, own_sems, src_at, dst_at):
    x, y, c = _me()
    me = 4 * x + 2 * y + c
    return [pltpu.make_async_copy(src_at(w, srcs[w], me), dst_at(w, lands[w], me, 0), own_sems.at[w])
            for w in range(len(srcs))]


def _exchange_start(name, srcs, lands, src_at, dst_at, after, own=False):
    nw = len(srcs)

    def body(*refs):
        src_refs, land_refs = refs[:nw], refs[nw:2 * nw]
        send_sems, recv_sems, own_sems = refs[2 * nw + 1], refs[2 * nw + 2], refs[2 * nw + 3]
        token = refs[-1]
        for cp in _exchange_copies(src_refs, land_refs, send_sems, recv_sems, src_at, dst_at):
            cp.start()
        if own:
            for cp in _own_copies(src_refs, land_refs, own_sems, src_at, dst_at):
                cp.start()
        token[...] = jnp.zeros_like(token)

    hbm = lambda a: pltpu.HBM(a.shape, a.dtype)
    outs = pl.pallas_call(
        body, name=name,
        out_shape=(pltpu.SemaphoreType.DMA((nw * (N_DEV - 1),)), pltpu.SemaphoreType.DMA((nw * (N_DEV - 1),)),
                   pltpu.SemaphoreType.DMA((nw,)), *[hbm(a) for a in srcs], *[hbm(a) for a in lands],
                   jax.ShapeDtypeStruct((8, 128), F32)),
        in_specs=[HBM_SPEC] * (2 * nw) + [pl.BlockSpec(memory_space=pl.ANY)],
        out_specs=(SEM_SPEC, SEM_SPEC, SEM_SPEC, *[HBM_SPEC] * (2 * nw), pl.BlockSpec(memory_space=pltpu.VMEM)),
        input_output_aliases={i: 3 + i for i in range(2 * nw)},
        compiler_params=pltpu.CompilerParams(has_side_effects=DATAFLOW),
    )(*[pltpu.with_memory_space_constraint(a, pltpu.HBM) for a in list(srcs) + list(lands)], after)
    return dict(send=outs[0], recv=outs[1], own_sems=outs[2], srcs=outs[3:3 + nw], lands=outs[3 + nw:3 + 2 * nw],
                token=outs[-1], src_at=src_at, dst_at=dst_at, name=name, own=own)


def _exchange_wait(pending, after):
    nw = len(pending["srcs"])

    def body(*refs):
        src_refs, land_refs = refs[:nw], refs[nw:2 * nw]
        send_sems, recv_sems, own_sems = refs[2 * nw], refs[2 * nw + 1], refs[2 * nw + 2]
        for cp in _exchange_copies(src_refs, land_refs, send_sems, recv_sems,
                                   pending["src_at"], pending["dst_at"]):
            cp.wait_send()
            cp.wait_recv()
        if pending["own"]:
            for cp in _own_copies(src_refs, land_refs, own_sems, pending["src_at"], pending["dst_at"]):
                cp.wait()

    hbm = lambda a: pltpu.HBM(a.shape, a.dtype)
    outs = pl.pallas_call(
        body, name=pending["name"] + "_wait",
        out_shape=(*[hbm(a) for a in pending["srcs"]], *[hbm(a) for a in pending["lands"]]),
        in_specs=[HBM_SPEC] * (2 * nw) + [SEM_SPEC, SEM_SPEC, SEM_SPEC, pl.BlockSpec(memory_space=pl.ANY)],
        out_specs=tuple([HBM_SPEC] * (2 * nw)),
        input_output_aliases={i: i for i in range(2 * nw)},
        compiler_params=pltpu.CompilerParams(has_side_effects=DATAFLOW),
    )(*pending["srcs"], *pending["lands"], pending["send"], pending["recv"], pending["own_sems"], after)
    return outs[nw:]


def _w_in_scatter_copies(src, land, send_sems, recv_sems, early):
    rows, cols = land.shape[1], land.shape[2]
    bound = EARLY_SEC * rows
    cut_dev = bound // cols
    cut = bound - cut_dev * cols
    x, y, c = _me()
    me = 4 * x + 2 * y + c

    def pieces(t):
        if early:
            return [(t > cut_dev, t * cols - bound, cols, 0), (t == cut_dev, 0, cols - cut, cut)]
        return [(t < cut_dev, t * cols, cols, 0), (t == cut_dev, cut_dev * cols, cut, 0)]

    out = []
    for k in range(1, N_DEV):
        px, py, pc = _peer(k, x, y, c)
        for (to_peer, s0, width, d0), (to_me, _, _, _) in zip(pieces(4 * px + 2 * py + pc), pieces(me)):
            s0 = s0 if isinstance(s0, int) else pl.multiple_of(jnp.maximum(s0, 0), 128)
            out.append((to_peer, to_me, pltpu.make_async_remote_copy(
                src_ref=src.at[:, pl.ds(s0, width)], dst_ref=land.at[k - 1, :, pl.ds(d0, width)],
                send_sem=send_sems.at[k - 1], recv_sem=recv_sems.at[k - 1],
                device_id=(px, py, pc), device_id_type=MESH)))
    return out


def _w_in_scatter_start(name, src, land, early, after):
    def body(src_ref, land_ref, after_ref, send_sems, recv_sems, src_thru, land_thru, token):
        for to_peer, _, cp in _w_in_scatter_copies(src_ref, land_ref, send_sems, recv_sems, early):
            pl.when(to_peer)(cp.start)
        token[...] = jnp.zeros_like(token)

    hbm = lambda a: pltpu.HBM(a.shape, a.dtype)
    outs = pl.pallas_call(
        body, name=name,
        out_shape=(pltpu.SemaphoreType.DMA((N_DEV - 1,)), pltpu.SemaphoreType.DMA((N_DEV - 1,)),
                   hbm(src), hbm(land), jax.ShapeDtypeStruct((8, 128), F32)),
        in_specs=[HBM_SPEC, HBM_SPEC, pl.BlockSpec(memory_space=pl.ANY)],
        out_specs=(SEM_SPEC, SEM_SPEC, HBM_SPEC, HBM_SPEC, pl.BlockSpec(memory_space=pltpu.VMEM)),
        input_output_aliases={0: 2, 1: 3},
        compiler_params=pltpu.CompilerParams(has_side_effects=DATAFLOW),
    )(pltpu.with_memory_space_constraint(src, pltpu.HBM), pltpu.with_memory_space_constraint(land, pltpu.HBM), after)
    return dict(send=outs[0], recv=outs[1], src=outs[2], land=outs[3], token=outs[4], early=early, name=name)


def _w_in_scatter_wait(pending, land, after):
    def body(src_ref, land_ref, send_sems, recv_sems, after_ref, src_dead, land_out):
        for to_peer, to_me, cp in _w_in_scatter_copies(src_ref, land_ref, send_sems, recv_sems, pending["early"]):
            pl.when(to_peer)(cp.wait_send)
            pl.when(to_me)(cp.wait_recv)

    hbm = lambda a: pltpu.HBM(a.shape, a.dtype)
    outs = pl.pallas_call(
        body, name=pending["name"] + "_wait", out_shape=(hbm(pending["src"]), hbm(land)),
        in_specs=[HBM_SPEC, HBM_SPEC, SEM_SPEC, SEM_SPEC, pl.BlockSpec(memory_space=pl.ANY)],
        out_specs=(HBM_SPEC, HBM_SPEC), input_output_aliases={0: 0, 1: 1},
        compiler_params=pltpu.CompilerParams(has_side_effects=DATAFLOW),
    )(pending["src"], land, pending["send"], pending["recv"], after)
    return outs[1]


def _call_after(dep, body, args, *, in_specs, **kw):
    n_in = len(args)

    def wrapped(*refs):
        body(*refs[:n_in], *refs[n_in + 1:])

    dep_spec = pl.BlockSpec(dep.shape, lambda *_: (0,) * dep.ndim)
    return pl.pallas_call(wrapped, in_specs=list(in_specs) + [dep_spec], **kw)(*args, dep)


def _resident(shape):
    return pl.BlockSpec(shape, lambda *_: (0,) * len(shape), pipeline_mode=pl.Buffered(1))


def _proj(x2, w_in, dep):
    T, D = x2.shape
    tm = min(256, T)

    def body(x_ref, w_ref, o_ref, xt_ref):
        x = x_ref[...]
        xt_ref[...] = x.T.astype(BF16)
        xb = x.astype(BF16)
        for j in range(N_SEC):
            o_ref[j] = jnp.dot(xb, w_ref[:, j * D:(j + 1) * D], preferred_element_type=F32)

    return _call_after(
        dep, body, (x2, w_in), name="proj", grid=(T // tm,),
        in_specs=[pl.BlockSpec((tm, D), lambda i: (i, 0)), _resident((D, N_SEC * D))],
        out_specs=[pl.BlockSpec((N_SEC, tm, D), lambda i: (0, i, 0)), pl.BlockSpec((D, tm), lambda i: (0, i))],
        out_shape=[jax.ShapeDtypeStruct((N_SEC, T, D), F32), jax.ShapeDtypeStruct((D, T), BF16)],
        compiler_params=_params(("parallel",)))


def _chunk_cumsum(v, reverse=False):
    rows, lanes = v.shape
    x = v.reshape(rows // SUBLANES, SUBLANES, lanes)
    pos = lax.broadcasted_iota(jnp.int32, x.shape, 1)
    for sh in (1, 2, 4):
        if reverse:
            x = x + jnp.where(pos < SUBLANES - sh, pltpu.roll(x, SUBLANES - sh, 1), 0.0)
        else:
            x = x + jnp.where(pos >= sh, pltpu.roll(x, sh, 1), 0.0)
    x = x.reshape(rows // CHUNK, CHUNK // SUBLANES, SUBLANES, lanes)
    half = lax.broadcasted_iota(jnp.int32, x.shape, 1)
    if reverse:
        x = x + jnp.where(half == 0, x[:, 1:2, 0:1, :], 0.0)
    else:
        x = x + jnp.where(half == 1, x[:, 0:1, SUBLANES - 1:SUBLANES, :], 0.0)
    return x.reshape(rows, lanes)


def _hgrn_gates(q, f_pre, lb_logits):
    l0, l1 = lb_logits[0:1, :], lb_logits[1:2, :]
    mx = jnp.maximum(l0, l1)
    e0, e1 = jnp.exp(l0 - mx), jnp.exp(l1 - mx)
    lb = e0 / (e0 + e1)
    sq = _sigmoid(q)
    qf = q * sq * Q_SCALE
    sg = _sigmoid(f_pre)
    f = lb + (1.0 - lb) * sg
    k = 1.0 - f
    log_f = jnp.log(f)
    G = _chunk_cumsum(log_f)
    g_to_end = _chunk_cumsum(log_f, reverse=True) - log_f
    e_g = jnp.exp(G)
    e_ng = jnp.exp(-G)
    e_ge = jnp.exp(g_to_end)
    return dict(lb=lb, sq=sq, qf=qf, sg=sg, f=f, k=k, G=G, e_g=e_g, e_ng=e_ng, e_ge=e_ge,
                qd=qf * e_g, ki=k * e_ng, ke=k * e_ge, dec=jnp.exp(G + g_to_end))


def _intra_mask():
    r = lax.broadcasted_iota(jnp.int32, (GROUP, GROUP), 0)
    c = lax.broadcasted_iota(jnp.int32, (GROUP, GROUP), 1)
    return (r // CHUNK == c // CHUNK) & (c <= r)


def _chunk_outer(lhs_rows, rhs_b, out_scr, sb):
    lane = lax.broadcasted_iota(jnp.int32, (GROUP, GROUP), 1) // CHUNK
    for g in range(sb // GROUP):
        sl = slice(g * GROUP, (g + 1) * GROUP)
        lhs_t = lhs_rows[sl].T
        for cc in range(CH_PER_GROUP):
            masked = jnp.where(lane == cc, lhs_t, 0.0).astype(BF16)
            out_scr[g * CH_PER_GROUP + cc] = jnp.dot(masked, rhs_b[sl], preferred_element_type=F32)


def _hgrn_forward_blocks(cs, vs, st0s, sb, o_scr, kv_scr, st_scr, dec_scr):
    nc = sb // CHUNK
    n_str = len(cs)
    mask = _intra_mask()
    bf = []
    for i, (c, v) in enumerate(zip(cs, vs)):
        qd_b, ki_b, ke_b, v_b = (c["qd"].astype(BF16), c["ki"].astype(BF16), c["ke"].astype(BF16),
                                 v.astype(BF16))
        bf.append((qd_b, ki_b, ke_b, v_b))
        for g in range(sb // GROUP):
            sl = slice(g * GROUP, (g + 1) * GROUP)
            sc = lax.dot_general(qd_b[sl], ki_b[sl], NT_DIMS, preferred_element_type=F32)
            a = jnp.where(mask, sc, 0.0).astype(BF16)
            o_scr[i, sl, :] = jnp.dot(a, v_b[sl], preferred_element_type=F32)
        _chunk_outer(v, ke_b, kv_scr.at[i], sb)
        dec_scr[i] = c["dec"]

    def rec(n, sts):
        row = pl.ds(pl.multiple_of(n * CHUNK, CHUNK), 1)
        out = []
        for i in range(n_str):
            st_scr[i, n] = sts[i]
            out.append(sts[i] * dec_scr[i, row, :] + kv_scr[i, n])
        return tuple(out)

    ends = lax.fori_loop(0, nc, rec, tuple(st0s))

    for n in range(nc):
        rows = slice(n * CHUNK, (n + 1) * CHUNK)
        for i in range(n_str):
            o_scr[i, rows, :] += lax.dot_general(bf[i][0][rows], st_scr[i, n].astype(BF16), NT_DIMS,
                                                 preferred_element_type=F32)
    return ends, bf


def _hgrn_fwd(proj5, lb_logits, gn):
    _, Bl, S, D = proj5.shape
    H = D // HEAD
    sb = min(512, S)
    nsb = S // sb
    nc = sb // CHUNK

    def body(p_ref, lbl_ref, gn_ref, ain_ref, aint_ref, st0_ref, carry, o_scr, kv_scr, st_scr, dec_scr):
        @pl.when(pl.program_id(1) == 0)
        def _():
            carry[...] = jnp.zeros_like(carry)

        st0s = [carry[b] for b in range(Bl)]
        for b in range(Bl):
            st0_ref[b, 0, 0] = st0s[b]
        cs = [_hgrn_gates(p_ref[0, b], p_ref[1, b], lbl_ref[...]) for b in range(Bl)]
        ends, _ = _hgrn_forward_blocks(cs, [p_ref[2, b] for b in range(Bl)], st0s, sb,
                                       o_scr, kv_scr, st_scr, dec_scr)
        for b in range(Bl):
            carry[b] = ends[b]
            o = o_scr[b]
            rinv = lax.rsqrt(jnp.mean(o * o, axis=-1, keepdims=True) + RMS_EPS)
            ain = o * rinv * gn_ref[...] * _sigmoid(p_ref[3, b])
            ain_ref[b] = ain.astype(BF16)
            aint_ref[b] = ain.T.astype(BF16)

    return pl.pallas_call(
        body, name="hgrn_fwd", grid=(H, nsb),
        in_specs=[pl.BlockSpec((4, Bl, sb, HEAD), lambda h, s: (0, 0, s, h)),
                  pl.BlockSpec((2, HEAD), lambda h, s: (0, h)),
                  pl.BlockSpec((1, HEAD), lambda h, s: (0, h))],
        out_specs=[pl.BlockSpec((Bl, sb, HEAD), lambda h, s: (0, s, h)),
                   pl.BlockSpec((Bl, HEAD, sb), lambda h, s: (0, h, s)),
                   pl.BlockSpec((Bl, 1, 1, HEAD, HEAD), lambda h, s: (0, h, s, 0, 0))],
        out_shape=[jax.ShapeDtypeStruct((Bl, S, D), BF16), jax.ShapeDtypeStruct((Bl, D, S), BF16),
                   jax.ShapeDtypeStruct((Bl, H, nsb, HEAD, HEAD), F32)],
        scratch_shapes=[pltpu.VMEM((Bl, HEAD, HEAD), F32), pltpu.VMEM((Bl, sb, HEAD), F32),
                        pltpu.VMEM((Bl, nc, HEAD, HEAD), F32), pltpu.VMEM((Bl, nc, HEAD, HEAD), F32),
                        pltpu.VMEM((Bl, sb, HEAD), F32)],
        compiler_params=_params(("parallel", "arbitrary")),
    )(proj5, lb_logits, gn)


def _window_count(shape, g):
    pos = lax.broadcasted_iota(jnp.int32, shape, 0)
    return pos, jnp.minimum(pos + 1, jnp.left_shift(2, g)).astype(F32)


def _select_window(g, sums):
    return jnp.where(g == 0, sums[0], jnp.where(g == 1, sums[1], jnp.where(g == 2, sums[2], sums[3])))


def _pool_fwd(proj5, w_pool):
    _, Bl, S, D = proj5.shape
    pg = D // POOL_GROUPS

    def body(v_ref, w_ref, pooled_t_ref, bp_ref):
        g = pl.program_id(1)
        v = v_ref[0, 0]
        pos, cnt = _window_count(v.shape, g)
        cur, sums = v, []
        for sh in (1, 2, 4, 8):
            cur = cur + jnp.where(pos >= sh, pltpu.roll(cur, sh, 0), 0.0)
            sums.append(cur)
        pooled = _select_window(g, sums) / cnt - v
        pooled_t_ref[...] = pooled.T.astype(BF16)
        bp_ref[0] = jnp.dot(pooled.astype(BF16), w_ref[0], preferred_element_type=F32)

    return pl.pallas_call(
        body, name="pool_fwd", grid=(Bl, POOL_GROUPS),
        in_specs=[pl.BlockSpec((1, 1, S, pg), lambda b, g: (4, b, 0, g)),
                  pl.BlockSpec((1, pg, pg), lambda b, g: (g, 0, 0))],
        out_specs=[pl.BlockSpec((pg, S), lambda b, g: (g, b)),
                   pl.BlockSpec((1, S, pg), lambda b, g: (b, 0, g))],
        out_shape=[jax.ShapeDtypeStruct((D, Bl * S), BF16), jax.ShapeDtypeStruct((Bl, S, D), F32)],
        compiler_params=_params(("parallel", "parallel")),
    )(proj5, w_pool)


def _layer_norm_fwd(r):
    mu = jnp.mean(r, axis=-1, keepdims=True)
    d = r - mu
    rs = lax.rsqrt(jnp.mean(d * d, axis=-1, keepdims=True) + LN_EPS)
    return d * rs, rs


def _layer_norm_bwd(dy_g, xhat, rs):
    return rs * (dy_g - jnp.mean(dy_g, axis=-1, keepdims=True)
                 - xhat * jnp.mean(dy_g * xhat, axis=-1, keepdims=True))


def _mix_fwd(ain, proj, bp, x2, w_a, w_out, ps, g1, b1):
    T, D = x2.shape
    tm = min(256, T)

    def body(ain_ref, ga_ref, gb_ref, bp_ref, x_ref, wa_ref, wo_ref, ps_ref, g1_ref, b1_ref,
             a_ref, mgt_ref, xh_ref, rs_ref, x1b_ref, x1t_ref):
        a = jnp.dot(ain_ref[...], wa_ref[...], preferred_element_type=F32)
        a_ref[...] = a
        merged = _sigmoid(ga_ref[0]) * a + _sigmoid(gb_ref[0]) * (bp_ref[...] * ps_ref[...])
        mgt_ref[...] = merged.T.astype(BF16)
        r1 = ALPHA * x_ref[...] + jnp.dot(merged.astype(BF16), wo_ref[...], preferred_element_type=F32)
        xhat, rs = _layer_norm_fwd(r1)
        xh_ref[...] = xhat
        rs_ref[...] = rs
        x1 = xhat * g1_ref[...] + b1_ref[...]
        x1b_ref[...] = x1.astype(BF16)
        x1t_ref[...] = x1.T.astype(BF16)

    row = lambda i: (i, 0)
    col = lambda i: (0, i)
    full = lambda i: (0, 0)
    return pl.pallas_call(
        body, name="mix_fwd", grid=(T // tm,),
        in_specs=[pl.BlockSpec((tm, D), row),
                  pl.BlockSpec((1, tm, D), lambda i: (5, i, 0)),
                  pl.BlockSpec((1, tm, D), lambda i: (6, i, 0)),
                  pl.BlockSpec((tm, D), row), pl.BlockSpec((tm, D), row),
                  pl.BlockSpec((D, D), full), pl.BlockSpec((D, D), full),
                  pl.BlockSpec((1, D), full), pl.BlockSpec((1, D), full), pl.BlockSpec((1, D), full)],
        out_specs=[pl.BlockSpec((tm, D), row), pl.BlockSpec((D, tm), col), pl.BlockSpec((tm, D), row),
                   pl.BlockSpec((tm, 1), row), pl.BlockSpec((tm, D), row), pl.BlockSpec((D, tm), col)],
        out_shape=[jax.ShapeDtypeStruct((T, D), F32), jax.ShapeDtypeStruct((D, T), BF16),
                   jax.ShapeDtypeStruct((T, D), F32), jax.ShapeDtypeStruct((T, 1), F32),
                   jax.ShapeDtypeStruct((T, D), BF16), jax.ShapeDtypeStruct((D, T), BF16)],
        compiler_params=_params(("parallel",)),
    )(ain, proj, proj, bp, x2, w_a, w_out, ps, g1, b1)


def _mlp_fwd(x1b, w_up, w_down, xhat1, tgt, g1, b1, g2, b2):
    T, D = xhat1.shape
    FF = w_up.shape[1]
    tm = min(256, T)

    def body(x_ref, wu_ref, wd_ref, xh_ref, t_ref, g1_ref, b1_ref, g2_ref, b2_ref,
             hp_ref, h_ref, dr_ref, drb_ref, drt_ref, vec_ref):
        @pl.when(pl.program_id(0) == 0)
        def _():
            vec_ref[...] = jnp.zeros_like(vec_ref)

        xb = x_ref[...]
        x1 = xh_ref[...] * g1_ref[...] + b1_ref[...]
        r2 = ALPHA * x1
        for f in range(FF // D):
            cols = slice(f * D, (f + 1) * D)
            hp = jnp.dot(xb, wu_ref[:, cols], preferred_element_type=F32)
            hp_ref[:, cols] = hp
            h = jnp.square(jnp.maximum(hp, 0.0)).astype(BF16)
            h_ref[:, cols] = h
            r2 = r2 + jnp.dot(h, wd_ref[cols, :], preferred_element_type=F32)
        xhat2, rs2 = _layer_norm_fwd(r2)
        err = xhat2 * g2_ref[...] + b2_ref[...] - t_ref[...]
        dy = err / D
        vec_ref[5:6, :] += jnp.sum(dy * xhat2, axis=0, keepdims=True)
        vec_ref[6:7, :] += jnp.sum(dy, axis=0, keepdims=True)
        vec_ref[7:8, :] += jnp.sum(0.5 * err * err / D, axis=0, keepdims=True)
        dr = _layer_norm_bwd(dy * g2_ref[...], xhat2, rs2)
        dr_ref[...] = dr
        drb_ref[...] = dr.astype(BF16)
        drt_ref[...] = dr.T.astype(BF16)

    row = lambda i: (i, 0)
    full = lambda i: (0, 0)
    return pl.pallas_call(
        body, name="mlp_fwd", grid=(T // tm,),
        in_specs=[pl.BlockSpec((tm, D), row), _resident((D, FF)), _resident((FF, D)),
                  pl.BlockSpec((tm, D), row), pl.BlockSpec((tm, D), row),
                  pl.BlockSpec((1, D), full), pl.BlockSpec((1, D), full),
                  pl.BlockSpec((1, D), full), pl.BlockSpec((1, D), full)],
        out_specs=[pl.BlockSpec((tm, FF), row), pl.BlockSpec((tm, FF), row), pl.BlockSpec((tm, D), row),
                   pl.BlockSpec((tm, D), row), pl.BlockSpec((D, tm), lambda i: (0, i)),
                   pl.BlockSpec((8, D), full)],
        out_shape=[jax.ShapeDtypeStruct((T, FF), F32), jax.ShapeDtypeStruct((T, FF), BF16),
                   jax.ShapeDtypeStruct((T, D), F32), jax.ShapeDtypeStruct((T, D), BF16),
                   jax.ShapeDtypeStruct((D, T), BF16), jax.ShapeDtypeStruct((8, D), F32)],
        compiler_params=_params(("arbitrary",)),
    )(x1b, w_up, w_down, xhat1, tgt, g1, b1, g2, b2)


def _mlp_bwd(drb, dr, hp, w_up, w_down, xhat1, rs1, g1):
    T, D = dr.shape
    FF = hp.shape[1]
    tm = min(256, T)

    def body(drb_ref, dr_ref, hp_ref, wu_ref, wd_ref, xh_ref, rs_ref, g1_ref,
             dhp_ref, d1_ref, d1b_ref, vec_ref):
        @pl.when(pl.program_id(0) == 0)
        def _():
            vec_ref[...] = jnp.zeros_like(vec_ref)

        drb = drb_ref[...]
        dx1 = ALPHA * dr_ref[...]
        for f in range(FF // D):
            cols = slice(f * D, (f + 1) * D)
            dh = lax.dot_general(drb, wd_ref[cols, :], NT_DIMS, preferred_element_type=F32)
            dhp = (dh * (2.0 * jnp.maximum(hp_ref[:, cols], 0.0))).astype(BF16)
            dhp_ref[:, cols] = dhp
            dx1 = dx1 + lax.dot_general(dhp, wu_ref[:, cols], NT_DIMS, preferred_element_type=F32)
        xhat = xh_ref[...]
        vec_ref[3:4, :] += jnp.sum(dx1 * xhat, axis=0, keepdims=True)
        vec_ref[4:5, :] += jnp.sum(dx1, axis=0, keepdims=True)
        d1 = _layer_norm_bwd(dx1 * g1_ref[...], xhat, rs_ref[...])
        d1_ref[...] = d1
        d1b_ref[...] = d1.astype(BF16)

    row = lambda i: (i, 0)
    full = lambda i: (0, 0)
    return pl.pallas_call(
        body, name="mlp_bwd", grid=(T // tm,),
        in_specs=[pl.BlockSpec((tm, D), row), pl.BlockSpec((tm, D), row), pl.BlockSpec((tm, FF), row),
                  _resident((D, FF)), _resident((FF, D)),
                  pl.BlockSpec((tm, D), row), pl.BlockSpec((tm, 1), row), pl.BlockSpec((1, D), full)],
        out_specs=[pl.BlockSpec((tm, FF), row), pl.BlockSpec((tm, D), row), pl.BlockSpec((tm, D), row),
                   pl.BlockSpec((8, D), full)],
        out_shape=[jax.ShapeDtypeStruct((T, FF), BF16), jax.ShapeDtypeStruct((T, D), F32),
                   jax.ShapeDtypeStruct((T, D), BF16), jax.ShapeDtypeStruct((8, D), F32)],
        compiler_params=_params(("arbitrary",)),
    )(drb, dr, hp, w_up, w_down, xhat1, rs1, g1)


def _dw(name, a_t, b, n_j, a_spec, b_spec, o_shape, o_block, o_map, transpose_out=False, dep=None,
        into=(None, None), ob_shape=None, ob_map=None):
    def body(*refs):
        a_ref, b_ref, o_ref, ob_ref = refs[0], refs[1], refs[-2], refs[-1]
        b_val = b_ref[0] if len(b_ref.shape) == 3 else b_ref[...]
        if len(a_ref.shape) == 3:
            seq = a_ref.shape[2]
            p = sum(jnp.dot(a_ref[i], b_val[i * seq:(i + 1) * seq], preferred_element_type=F32)
                    for i in range(a_ref.shape[0]))
        else:
            p = jnp.dot(a_ref[...], b_val, preferred_element_type=F32)
        if transpose_out:
            p = p.T
        p = p.reshape(o_ref.shape)
        o_ref[...] = p
        ob_ref[...] = p.astype(BF16)

    kw = dict(name=name, grid=(n_j,), in_specs=[a_spec, b_spec],
              out_specs=[pl.BlockSpec(o_block, o_map), pl.BlockSpec(o_block, ob_map or o_map)],
              out_shape=[jax.ShapeDtypeStruct(o_shape, F32), jax.ShapeDtypeStruct(ob_shape or o_shape, BF16)],
              compiler_params=_params(("parallel",)))
    args = (a_t, b)
    aliases = {}
    for out_index, arr in enumerate(into):
        if arr is not None:
            aliases[len(args)] = out_index
            args = args + (arr,)
            kw["in_specs"] = kw["in_specs"] + [pl.BlockSpec(memory_space=pl.ANY)]
    if aliases:
        kw["input_output_aliases"] = aliases
    if dep is None:
        return pl.pallas_call(body, **kw)(*args)
    return _call_after(dep, body, args, **kw)


def _mix_bwd(d1b, proj, a, bp, w_a, w_out, w_pool, ps, dep):
    T, D = a.shape
    tm = min(256, T)
    pg = D // POOL_GROUPS

    def body(d1b_ref, ga_ref, gb_ref, a_ref, bp_ref, wa_ref, wo_ref, wp_ref, ps_ref,
             da_ref, dbp_ref, dain_ref, dpl_ref, dg_ref, vec_ref):
        @pl.when(pl.program_id(0) == 0)
        def _():
            vec_ref[...] = jnp.zeros_like(vec_ref)

        dm = lax.dot_general(d1b_ref[...], wo_ref[...], NT_DIMS, preferred_element_type=F32)
        sa, sg = _sigmoid(ga_ref[0]), _sigmoid(gb_ref[0])
        bp_v, ps_v = bp_ref[...], ps_ref[...]
        da = (dm * sa).astype(BF16)
        db = dm * sg
        dg_ref[0] = (dm * a_ref[...] * sa * (1.0 - sa)).astype(BF16)
        dg_ref[1] = (dm * (bp_v * ps_v) * sg * (1.0 - sg)).astype(BF16)
        vec_ref[2:3, :] += jnp.sum(db * bp_v, axis=0, keepdims=True)
        dbp = (db * ps_v).astype(BF16)
        da_ref[...] = da
        dbp_ref[...] = dbp
        dain_ref[...] = lax.dot_general(da, wa_ref[...], NT_DIMS, preferred_element_type=F32)
        for g in range(POOL_GROUPS):
            cols = slice(g * pg, (g + 1) * pg)
            dpl_ref[:, cols] = lax.dot_general(dbp[:, cols], wp_ref[g], NT_DIMS,
                                               preferred_element_type=F32)

    row = lambda i: (i, 0)
    full = lambda i: (0, 0)
    return _call_after(
        dep, body, (d1b, proj, proj, a, bp, w_a, w_out, w_pool, ps), name="mix_bwd", grid=(T // tm,),
        in_specs=[pl.BlockSpec((tm, D), row),
                  pl.BlockSpec((1, tm, D), lambda i: (5, i, 0)),
                  pl.BlockSpec((1, tm, D), lambda i: (6, i, 0)),
                  pl.BlockSpec((tm, D), row), pl.BlockSpec((tm, D), row),
                  pl.BlockSpec((D, D), full), pl.BlockSpec((D, D), full),
                  pl.BlockSpec((POOL_GROUPS, pg, pg), lambda i: (0, 0, 0)),
                  pl.BlockSpec((1, D), full)],
        out_specs=[pl.BlockSpec((tm, D), row), pl.BlockSpec((tm, D), row),
                   pl.BlockSpec((tm, D), row), pl.BlockSpec((tm, D), row),
                   pl.BlockSpec((2, tm, D), lambda i: (0, i, 0)),
                   pl.BlockSpec((8, D), full)],
        out_shape=[jax.ShapeDtypeStruct((T, D), BF16), jax.ShapeDtypeStruct((T, D), BF16),
                   jax.ShapeDtypeStruct((T, D), F32), jax.ShapeDtypeStruct((T, D), F32),
                   jax.ShapeDtypeStruct((2, T, D), BF16), jax.ShapeDtypeStruct((8, D), F32)],
        compiler_params=_params(("arbitrary",)))


def _pool_bwd(dpooled3, dep):
    Bl, S, D = dpooled3.shape
    pg = D // POOL_GROUPS

    def body(dp_ref, dv_ref):
        g = pl.program_id(1)
        dp = dp_ref[0]
        pos, cnt = _window_count(dp.shape, g)
        cur, sums = dp / cnt, []
        for sh in (1, 2, 4, 8):
            cur = cur + jnp.where(pos < S - sh, pltpu.roll(cur, S - sh, 0), 0.0)
            sums.append(cur)
        dv_ref[0] = (_select_window(g, sums) - dp).astype(BF16)

    spec = pl.BlockSpec((1, S, pg), lambda b, g: (b, 0, g))
    return _call_after(
        dep, body, (dpooled3,), name="pool_bwd", grid=(Bl, POOL_GROUPS), in_specs=[spec], out_specs=spec,
        out_shape=jax.ShapeDtypeStruct((Bl, S, D), BF16),
        compiler_params=_params(("parallel", "parallel")))


def _hgrn_bwd(proj5, lb_logits, gn, dain3, st0_all, dep):
    _, Bl, S, D = proj5.shape
    H = D // HEAD
    sb = min(512, S)
    nsb = S // sb
    nc = sb // CHUNK
    streams = range(Bl)

    def body(p_ref, lbl_ref, gn_ref, dain_ref, st0_ref, d_ref, vec_ref,
             dcarry, o_scr, kv_scr, st_scr, dst_scr, dec_scr, dvi_scr, dke_scr, dqi_scr):
        s = pl.program_id(1)

        @pl.when(s == 0)
        def _():
            dcarry[...] = jnp.zeros_like(dcarry)
            vec_ref[...] = jnp.zeros_like(vec_ref)

        qs, vs, ogs = [p_ref[0, b] for b in streams], [p_ref[2, b] for b in streams], [p_ref[3, b] for b in streams]
        cs = [_hgrn_gates(qs[b], p_ref[1, b], lbl_ref[...]) for b in streams]
        _, bf = _hgrn_forward_blocks(cs, vs, [st0_ref[b, 0, 0] for b in streams], sb,
                                     o_scr, kv_scr, st_scr, dec_scr)
        mask = _intra_mask()
        gn_v = gn_ref[...]
        keep = []
        for b in streams:
            qd_b, ki_b, ke_b, v_b = bf[b]
            o = o_scr[b]
            rinv = lax.rsqrt(jnp.mean(o * o, axis=-1, keepdims=True) + RMS_EPS)
            on = o * rinv
            so = _sigmoid(ogs[b])
            dain = dain_ref[b]
            vec_ref[1:2, :] += jnp.sum(dain * on * so, axis=0, keepdims=True)
            d_og = dain * on * gn_v * so * (1.0 - so)
            d_on = dain * gn_v * so
            do = rinv * (d_on - on * jnp.mean(d_on * on, axis=-1, keepdims=True))
            do_b = do.astype(BF16)
            dv_parts, dqd_parts, dki_parts = [], [], []
            for g in range(sb // GROUP):
                sl = slice(g * GROUP, (g + 1) * GROUP)
                sc = lax.dot_general(qd_b[sl], ki_b[sl], NT_DIMS, preferred_element_type=F32)
                a = jnp.where(mask, sc, 0.0).astype(BF16)
                da = lax.dot_general(do_b[sl], v_b[sl], NT_DIMS, preferred_element_type=F32)
                da = jnp.where(mask, da, 0.0).astype(BF16)
                dv_parts.append(lax.dot_general(a, do_b[sl], TN_DIMS, preferred_element_type=F32))
                dqd_parts.append(jnp.dot(da, ki_b[sl], preferred_element_type=F32))
                dki_parts.append(lax.dot_general(da, qd_b[sl], TN_DIMS, preferred_element_type=F32))
            keep.append(dict(d_og=d_og, do_b=do_b, dv_intra=jnp.concatenate(dv_parts, axis=0),
                             dqd_intra=jnp.concatenate(dqd_parts, axis=0),
                             dki=jnp.concatenate(dki_parts, axis=0)))
            _chunk_outer(do, qd_b, kv_scr.at[b], sb)

        def rrec(i, dsts):
            n = nc - 1 - i
            row = pl.ds(pl.multiple_of(n * CHUNK, CHUNK), 1)
            out = []
            for b in streams:
                dst_scr[b, n] = dsts[b]
                out.append(dsts[b] * dec_scr[b, row, :] + kv_scr[b, n])
            return tuple(out)

        ends = lax.fori_loop(0, nc, rrec, tuple(dcarry[b] for b in streams))
        for b in streams:
            dcarry[b] = ends[b]
        for n in range(nc):
            rows = slice(n * CHUNK, (n + 1) * CHUNK)
            for b in streams:
                qd_b, ki_b, ke_b, v_b = bf[b]
                dst_b = dst_scr[b, n].astype(BF16)
                dvi_scr[b, rows, :] = lax.dot_general(ke_b[rows], dst_b, NT_DIMS, preferred_element_type=F32)
                dke_scr[b, rows, :] = jnp.dot(v_b[rows], dst_b, preferred_element_type=F32)
                dqi_scr[b, rows, :] = jnp.dot(keep[b]["do_b"][rows], st_scr[b, n].astype(BF16),
                                              preferred_element_type=F32)
        for b in streams:
            c, k = cs[b], keep[b]
            ddec = jnp.sum(dst_scr[b] * st_scr[b], axis=1)
            dgl = jnp.broadcast_to(ddec[:, None, :], (nc, CHUNK, HEAD)).reshape(sb, HEAD) * c["dec"]
            dqd = k["dqd_intra"] + dqi_scr[b]
            dke = dke_scr[b]
            dki = k["dki"]
            t_ke = dke * c["ke"]
            dG = dqd * c["qd"] - dki * c["ki"] - t_ke
            dgl = dgl + _chunk_cumsum(t_ke) + _chunk_cumsum(t_ke, reverse=True) - t_ke
            dlogf = _chunk_cumsum(dG, reverse=True) + dgl
            dk = dki * c["e_ng"] + dke * c["e_ge"]
            df = dlogf / c["f"] - dk
            sg, sq, lb, q = c["sg"], c["sq"], c["lb"], qs[b]
            vec_ref[0:1, :] += jnp.sum(df * (1.0 - sg), axis=0, keepdims=True)
            d_ref[0, b] = (dqd * c["e_g"] * Q_SCALE * (sq + q * sq * (1.0 - sq))).astype(BF16)
            d_ref[1, b] = (df * (1.0 - lb) * sg * (1.0 - sg)).astype(BF16)
            d_ref[2, b] = (k["dv_intra"] + dvi_scr[b]).astype(BF16)
            d_ref[3, b] = k["d_og"].astype(BF16)

    rev = lambda s: nsb - 1 - s
    big = pltpu.VMEM((Bl, nc, HEAD, HEAD), F32)
    rows_f32 = pltpu.VMEM((Bl, sb, HEAD), F32)
    return _call_after(
        dep, body, (proj5, lb_logits, gn, dain3, st0_all), name="hgrn_bwd", grid=(H, nsb),
        in_specs=[pl.BlockSpec((4, Bl, sb, HEAD), lambda h, s: (0, 0, rev(s), h)),
                  pl.BlockSpec((2, HEAD), lambda h, s: (0, h)),
                  pl.BlockSpec((1, HEAD), lambda h, s: (0, h)),
                  pl.BlockSpec((Bl, sb, HEAD), lambda h, s: (0, rev(s), h)),
                  pl.BlockSpec((Bl, 1, 1, HEAD, HEAD), lambda h, s: (0, h, rev(s), 0, 0))],
        out_specs=[pl.BlockSpec((4, Bl, sb, HEAD), lambda h, s: (0, 0, rev(s), h)),
                   pl.BlockSpec((8, HEAD), lambda h, s: (0, h))],
        out_shape=[jax.ShapeDtypeStruct((4, Bl, S, D), BF16), jax.ShapeDtypeStruct((8, D), F32)],
        scratch_shapes=[pltpu.VMEM((Bl, HEAD, HEAD), F32), rows_f32, big, big, big, rows_f32,
                        rows_f32, rows_f32, rows_f32],
        compiler_params=_params(("parallel", "arbitrary")))


def _dx(d1, dh4, dpv, dg2, w_in, dep):
    T, D = d1.shape
    tm = min(256, T)

    def body(d1_ref, dh_ref, dp_ref, dg_ref, w_ref, o_ref):
        blocks = [dh_ref[0], dh_ref[1], dh_ref[2], dh_ref[3], dp_ref[...], dg_ref[0], dg_ref[1]]
        acc = ALPHA * d1_ref[...]
        for j, blk in enumerate(blocks):
            acc = acc + lax.dot_general(blk, w_ref[:, j * D:(j + 1) * D], NT_DIMS, preferred_element_type=F32)
        o_ref[...] = acc

    row = lambda i: (i, 0)
    return _call_after(
        dep, body, (d1, dh4, dpv, dg2, w_in), name="dx", grid=(T // tm,),
        in_specs=[pl.BlockSpec((tm, D), row), pl.BlockSpec((4, tm, D), lambda i: (0, i, 0)),
                  pl.BlockSpec((tm, D), row), pl.BlockSpec((2, tm, D), lambda i: (0, i, 0)),
                  _resident((D, N_SEC * D))],
        out_specs=pl.BlockSpec((tm, D), row),
        out_shape=jax.ShapeDtypeStruct((T, D), F32),
        compiler_params=_params(("parallel",)))


EARLY_SEC = 4
DW_COLS = 512


def _dw_in_part(name, x_t, b, sections, first_sec, into, dep, ob_shape, ob_first):
    D, T = x_t.shape
    per = D // DW_COLS
    b_spec = (pl.BlockSpec((1, T, DW_COLS), lambda j: (j // per, 0, j % per)) if b.ndim == 3
              else pl.BlockSpec((T, DW_COLS), lambda j: (0, j)))
    return _dw(name, x_t, b, sections * per, _resident((D, T)), b_spec, (D, N_SEC * D), (D, DW_COLS),
               lambda j: (0, first_sec * per + j), dep=dep, into=into, ob_shape=ob_shape,
               ob_map=lambda j: (0, ob_first * per + j))


def _dw_in_early(x_t, dpv, dg2, dep):
    D = x_t.shape[0]
    early_shape = (D, (N_SEC - EARLY_SEC) * D)
    f32, bf = _dw_in_part("dw_in_gates", x_t, dg2, 2, 5, (None, None), dep, early_shape, 1)
    return _dw_in_part("dw_in_pool", x_t, dpv, 1, 4, (f32, bf), None, early_shape, 0)


def _dw_in_late(x_t, dh4, f32_early):
    D = x_t.shape[0]
    return _dw_in_part("dw_in_rec", x_t, dh4, EARLY_SEC, 0, (f32_early, None), None, (D, EARLY_SEC * D), 0)


def _adam_shard(name, me_arr, grad, land, layout, w, m, v):
    shape = layout.shape
    n_split = 4
    blk = (shape[0] // n_split,) + shape[1:]
    zeros = (0,) * (len(shape) - 1)

    def body(me_ref, g_ref, r_ref, w_ref, m_ref, v_ref, g_out, d_out, m_out, v_out):
        g = g_ref[...]
        for k in range(N_DEV - 1):
            g = g + r_ref[k].astype(F32)
        d, m2, v2 = _adamw(w_ref[...], g, m_ref[...], v_ref[...])
        g_out[...] = g
        d_out[...] = d
        m_out[...] = m2
        v_out[...] = v2

    def own(i, me_ref):
        bi = layout.block_index(me_ref[0])
        return (bi[0] * n_split + i,) + tuple(bi[1:]) if layout.kind == "row" else (i,) + tuple(bi[1:])

    plain = pl.BlockSpec(blk, lambda i, me_ref: (i,) + zeros)
    grid_spec = pltpu.PrefetchScalarGridSpec(
        num_scalar_prefetch=1, grid=(n_split,),
        in_specs=[pl.BlockSpec(blk, own),
                  pl.BlockSpec((N_DEV - 1,) + blk, lambda i, me_ref: (0, i) + zeros),
                  plain, plain, plain],
        out_specs=[plain] * 4)
    return pl.pallas_call(
        body, name=name, grid_spec=grid_spec,
        out_shape=[jax.ShapeDtypeStruct(shape, F32)] * 4,
        compiler_params=_params(("parallel",)),
    )(me_arr, grad, land, w, m, v)


def _vec_allreduce(vec):
    D = vec.shape[1]

    def body(vec_ref, tot_ref, gat, send_sems, recv_sems):
        x, y, c = _me()
        me = 4 * x + 2 * y + c
        gat[me] = vec_ref[...]
        copies = []
        for k in range(1, N_DEV):
            cp = pltpu.make_async_remote_copy(
                src_ref=vec_ref, dst_ref=gat.at[me], send_sem=send_sems.at[k - 1],
                recv_sem=recv_sems.at[k - 1], device_id=_peer(k, x, y, c), device_id_type=MESH)
            cp.start()
            copies.append(cp)
        for cp in copies:
            cp.wait()
        tot = gat[0]
        for d in range(1, N_DEV):
            tot = tot + gat[d]
        tot_ref[...] = tot

    vm = pl.BlockSpec(memory_space=pltpu.VMEM)
    return pl.pallas_call(
        body, name="vec_allreduce", out_shape=jax.ShapeDtypeStruct(vec.shape, F32),
        in_specs=[vm], out_specs=vm,
        scratch_shapes=[pltpu.VMEM((N_DEV, 8, D), F32), pltpu.SemaphoreType.DMA((N_DEV - 1,)),
                        pltpu.SemaphoreType.DMA((N_DEV - 1,))],
    )(vec)


def _vec_adam(tot, small_w, small_m, small_v):
    n = len(small_w)

    def body(*refs):
        tot = refs[0][...]
        ws, ms, vs = refs[1:1 + n], refs[1 + n:1 + 2 * n], refs[1 + 2 * n:1 + 3 * n]
        outs = refs[1 + 3 * n:]
        loss_ref, g_out, d_out = outs[0], outs[1:1 + n], outs[1 + n:1 + 2 * n]
        m_out, v_out = outs[1 + 2 * n:1 + 3 * n], outs[1 + 3 * n:1 + 4 * n]
        loss_ref[...] = jnp.broadcast_to(jnp.sum(tot[7:8, :], axis=1, keepdims=True), loss_ref.shape)
        lbl = ws[0][...]
        mx = jnp.maximum(lbl[0:1, :], lbl[1:2, :])
        e0, e1 = jnp.exp(lbl[0:1, :] - mx), jnp.exp(lbl[1:2, :] - mx)
        p0 = e0 / (e0 + e1)
        dl0 = tot[0:1, :] * p0 * (1.0 - p0)
        grads = [jnp.concatenate([dl0, -dl0], axis=0)] + [tot[r:r + 1, :] for r in range(1, n)]
        for i in range(n):
            d, m2, v2 = _adamw(ws[i][...], grads[i], ms[i][...], vs[i][...])
            g_out[i][...] = grads[i]
            d_out[i][...] = d
            m_out[i][...] = m2
            v_out[i][...] = v2

    vm = pl.BlockSpec(memory_space=pltpu.VMEM)
    shapes = [jax.ShapeDtypeStruct(w.shape, F32) for w in small_w]
    return pl.pallas_call(
        body, name="vec_adam",
        out_shape=[jax.ShapeDtypeStruct((1, 128), F32)] + shapes * 4,
        in_specs=[vm] * (1 + 3 * n), out_specs=[vm] * (1 + 4 * n),
    )(tot, *small_w, *small_m, *small_v)


def kernel(x, w_in, lb_logits, hgrn_norm_g, w_a, w_pool, pool_scale, w_out, ln1_g, ln1_b, w_up, w_down, ln2_g, ln2_b, loss_target, m_w_in, m_lb_logits, m_hgrn_norm_g, m_w_a, m_w_pool, m_pool_scale, m_w_out, m_ln1_g, m_ln1_b, m_w_up, m_w_down, m_ln2_g, m_ln2_b, v_w_in, v_lb_logits, v_hgrn_norm_g, v_w_a, v_w_pool, v_pool_scale, v_w_out, v_ln1_g, v_ln1_b, v_w_up, v_w_down, v_ln2_g, v_ln2_b):
    Bl, S, D = x.shape
    T = Bl * S
    pg = D // POOL_GROUPS
    x2 = x.reshape(T, D)
    tgt = loss_target.reshape(T, D)
    me = 4 * lax.axis_index("x") + 2 * lax.axis_index("y") + lax.axis_index("c")
    me_arr = jnp.reshape(me, (1,)).astype(jnp.int32)

    names = ["w_in", "w_a", "w_pool", "w_out", "w_up", "w_down"]
    big_w = dict(zip(names, [w_in[0], w_a[0], w_pool[0], w_out[0], w_up[0], w_down[0]]))
    big_m = dict(zip(names, [m_w_in[0], m_w_a[0], m_w_pool[0], m_w_out[0], m_w_up[0], m_w_down[0]]))
    big_v = dict(zip(names, [v_w_in[0], v_w_a[0], v_w_pool[0], v_w_out[0], v_w_up[0], v_w_down[0]]))
    kinds = dict(w_in="col", w_a="row", w_pool="pool", w_out="row", w_up="col", w_down="row")
    lay = {nm: _Sharded(kinds[nm], big_w[nm].shape) for nm in names}
    wb = {nm: big_w[nm].astype(BF16) for nm in names}

    (w_in_f,) = _all_gather("ag_w_in", [wb["w_in"]], [lay["w_in"]])
    rest = names[1:]
    ag_rest = _exchange_start("ag_rest", [wb[nm] for nm in rest],
                              [lax.empty(lay[nm].full_shape, BF16) for nm in rest],
                              src_at=lambda w, ref, peer: ref,
                              dst_at=lambda w, ref, mine, k: lay[rest[w]].at(ref, mine), after=w_in_f, own=True)

    proj, x_t = _proj(x2, w_in_f, ag_rest["token"])
    proj5 = proj.reshape(N_SEC, Bl, S, D)
    ain3, ain_t, st0_all = _hgrn_fwd(proj5, lb_logits, hgrn_norm_g)
    w_a_f, w_pool_f, w_out_f, w_up_f, w_down_f = _exchange_wait(ag_rest, ain3)
    pooled_t, bp3 = _pool_fwd(proj5, w_pool_f)
    ain, bp = ain3.reshape(T, D), bp3.reshape(T, D)
    a, merged_t, xhat1, rs1, x1b, x1_t = _mix_fwd(ain, proj, bp, x2, w_a_f, w_out_f, pool_scale, ln1_g, ln1_b)
    hp, h, dr2, dr2b, dr2_t, vec_mlp = _mlp_fwd(x1b, w_up_f, w_down_f, xhat1, tgt, ln1_g, ln1_b, ln2_g, ln2_b)

    def scatter_start(name, nms, grads_b, after):
        lands = [lax.empty((N_DEV - 1,) + lay[nm].shape, BF16) for nm in nms]
        return _exchange_start(name, grads_b, lands,
                               src_at=lambda w, ref, peer: lay[nms[w]].at(ref, peer),
                               dst_at=lambda w, ref, mine, k: ref.at[k - 1], after=after)

    dhp, dr1, dr1b, vec_ln1 = _mlp_bwd(dr2b, dr2, hp, w_up_f, w_down_f, xhat1, rs1, ln1_g)
    FF = 4 * D
    whole_t = _resident((D, T))
    cols_b = pl.BlockSpec((T, DW_COLS), lambda j: (0, j))
    cols_o = ((D, DW_COLS), lambda j: (0, j))
    gw, gwb = {}, {}
    gw["w_down"], gwb["w_down"] = _dw(
        "dw_down", dr2_t, h, FF // DW_COLS, whole_t, cols_b, (FF, D), (DW_COLS, D), lambda j: (j, 0),
        transpose_out=True)
    rs_down = scatter_start("rs_w_down", ["w_down"], [gwb["w_down"]], gw["w_down"])
    gw["w_up"], gwb["w_up"] = _dw("dw_up", x1_t, dhp, FF // DW_COLS, whole_t, cols_b, (D, FF), *cols_o,
                                  dep=rs_down["token"])
    rs_up = scatter_start("rs_w_up", ["w_up"], [gwb["w_up"]], gw["w_up"])
    da_b, dbp_b, dain, dpooled, dg2, vec_mix = _mix_bwd(dr1b, proj, a, bp, w_a_f, w_out_f, w_pool_f, pool_scale,
                                                        rs_up["token"])
    gw["w_out"], gwb["w_out"] = _dw("dw_out", merged_t, dr1b, D // DW_COLS, whole_t, cols_b, (D, D), *cols_o)
    gw["w_a"], gwb["w_a"] = _dw("dw_a", ain_t, da_b, D // DW_COLS, _resident((Bl, D, S)), cols_b, (D, D), *cols_o)
    gw["w_pool"], gwb["w_pool"] = _dw(
        "dw_pool", pooled_t, dbp_b, POOL_GROUPS, pl.BlockSpec((pg, T), lambda j: (j, 0)),
        pl.BlockSpec((T, pg), lambda j: (0, j)), (POOL_GROUPS, pg, pg), (1, pg, pg), lambda j: (j, 0, 0))
    mid = ["w_out", "w_a", "w_pool"]
    rs_mid = scatter_start("rs_w_mid", mid, [gwb[nm] for nm in mid], gw["w_pool"])
    dpv = _pool_bwd(dpooled.reshape(Bl, S, D), rs_mid["token"]).reshape(T, D)
    gw_in_early, gwb_in_early = _dw_in_early(x_t, dpv, dg2, rs_mid["token"])
    land_in = lax.empty((N_DEV - 1,) + lay["w_in"].shape, BF16)
    rs_in_early = _w_in_scatter_start("rs_w_in_early", gwb_in_early, land_in, True, gw_in_early)
    dh4, vec_hgrn = _hgrn_bwd(proj5, lb_logits, hgrn_norm_g, dain.reshape(Bl, S, D), st0_all,
                              rs_in_early["token"])
    dh4 = dh4.reshape(4, T, D)
    gw["w_in"], gwb_in_late = _dw_in_late(x_t, dh4, gw_in_early)
    rs_in_late = _w_in_scatter_start("rs_w_in_late", gwb_in_late, rs_in_early["land"], False, gw["w_in"])
    grad_x2 = _dx(dr1, dh4, dpv, dg2, w_in_f, rs_in_late["token"])
    grad_x = grad_x2.reshape(Bl, S, D)

    vec = vec_mlp + vec_ln1 + vec_mix + vec_hgrn
    small_names = ["lb_logits", "hgrn_norm_g", "pool_scale", "ln1_g", "ln1_b", "ln2_g", "ln2_b"]
    small_w = [lb_logits, hgrn_norm_g, pool_scale, ln1_g, ln1_b, ln2_g, ln2_b]
    small_m = [m_lb_logits, m_hgrn_norm_g, m_pool_scale, m_ln1_g, m_ln1_b, m_ln2_g, m_ln2_b]
    small_v = [v_lb_logits, v_hgrn_norm_g, v_pool_scale, v_ln1_g, v_ln1_b, v_ln2_g, v_ln2_b]
    res = _vec_adam(_vec_allreduce(vec), small_w, small_m, small_v)
    loss = res[0][0, 0]
    n = len(small_w)
    small = {nm: (res[1 + i], res[1 + n + i], res[1 + 2 * n + i], res[1 + 3 * n + i])
             for i, nm in enumerate(small_names)}

    big, last = {}, grad_x2

    def adam(nm, land):
        outs = _adam_shard("adam_" + nm, me_arr, gw[nm], land, lay[nm], big_w[nm], big_m[nm], big_v[nm])
        big[nm] = tuple(t[None] for t in outs)
        return outs[0]

    for pend, nms in ((rs_down, ["w_down"]), (rs_up, ["w_up"]), (rs_mid, mid)):
        for nm, land in zip(nms, _exchange_wait(pend, last)):
            last = adam(nm, land)
    land_in = _w_in_scatter_wait(rs_in_early, rs_in_late["land"], last)
    adam("w_in", _w_in_scatter_wait(rs_in_late, land_in, last))

    order = ["w_in", "lb_logits", "hgrn_norm_g", "w_a", "w_pool", "pool_scale", "w_out", "ln1_g", "ln1_b",
             "w_up", "w_down", "ln2_g", "ln2_b"]
    allp = {**big, **small}
    out = [loss, grad_x]
    for part in range(4):
        out += [allp[nm][part] for nm in order]
    return tuple(out)
```

```python
import jax
import jax.numpy as jnp
from jax import lax
from jax.experimental import pallas as pl
from jax.experimental.pallas import tpu as pltpu

F32 = jnp.float32
BF16 = jnp.bfloat16
MESH = pl.DeviceIdType.MESH

N_DEV = 8
HEAD = 128
CHUNK = 16
SUBLANES = 8
GROUP = 128
SUB_BLOCK = 1024
CH_PER_GROUP = GROUP // CHUNK
N_SEC = 7
POOL_GROUPS = 4
ALPHA = (2.0 * 1) ** 0.25
LN_EPS = 1e-5
RMS_EPS = 1e-6
Q_SCALE = HEAD ** -0.5
ADAM_LR = 0.001
ADAM_B1 = 0.9
ADAM_B2 = 0.999
ADAM_EPS = 1e-08
ADAM_WD = 0.01
ADAM_STEP = 10
VMEM_LIMIT = 60 << 20

NT_DIMS = (((1,), (1,)), ((), ()))
TN_DIMS = (((0,), (0,)), ((), ()))


def _params(sem=None):
    kw = dict(vmem_limit_bytes=VMEM_LIMIT)
    if sem is not None:
        kw["dimension_semantics"] = sem
    return pltpu.CompilerParams(**kw)


def _me():
    return lax.axis_index("x"), lax.axis_index("y"), lax.axis_index("c")


def _sigmoid(v):
    return jax.nn.sigmoid(v)


def _adamw(w, g, m, v):
    m = ADAM_B1 * m + (1.0 - ADAM_B1) * g
    v = ADAM_B2 * v + (1.0 - ADAM_B2) * jnp.square(g)
    m_hat = m / (1.0 - ADAM_B1 ** ADAM_STEP)
    v_hat = v / (1.0 - ADAM_B2 ** ADAM_STEP)
    delta = -ADAM_LR * (m_hat / (jnp.sqrt(v_hat) + ADAM_EPS) + ADAM_WD * w)
    return delta, m, v


class _Sharded:
    def __init__(self, kind, shard_shape):
        self.kind, self.shape = kind, tuple(shard_shape)

    @property
    def full_shape(self):
        r = self.shape
        if self.kind == "row":
            return (N_DEV * r[0],) + r[1:]
        return (r[0], N_DEV * r[1]) + r[2:]

    def at(self, ref, d):
        if self.kind == "col":
            n = self.shape[1]
            return ref.at[:, pl.ds(pl.multiple_of(d * n, 128), n)]
        if self.kind == "row":
            n = self.shape[0]
            return ref.at[pl.ds(pl.multiple_of(d * n, 16), n), :]
        n = self.shape[1]
        return ref.at[:, pl.ds(pl.multiple_of(d * n, 16), n), :]

    def block_index(self, d):
        return {"col": (0, d), "row": (d, 0), "pool": (0, d, 0)}[self.kind]


def _peer(k, x, y, c):
    return (1 - x if k & 4 else x, 1 - y if k & 2 else y, 1 - c if k & 1 else c)


def _all_gather(name, shards, layouts):
    nw = len(shards)

    def body(*refs):
        ins, outs = refs[:nw], refs[nw:2 * nw]
        send_sems, recv_sems, local_sems = refs[2 * nw:]
        x, y, c = _me()
        me = (x, y, c)
        sibling = (x, y, 1 - c)
        chips = [(1 - x, y), (x, 1 - y), (1 - x, 1 - y)]

        def copy(w, k, block, to, src=None):
            px, py, pc = block
            dst = layouts[w].at(outs[w], 4 * px + 2 * py + pc)
            return pltpu.make_async_remote_copy(
                src_ref=dst if src is None else src, dst_ref=dst,
                send_sem=send_sems.at[w, k], recv_sem=recv_sems.at[w, k],
                device_id=to, device_id_type=MESH)

        def place(w):
            mine = pltpu.make_async_copy(ins[w], layouts[w].at(outs[w], 4 * x + 2 * y + c), local_sems.at[w])
            mine.start()
            return mine

        first = []
        for w in range(nw):
            first.append(copy(w, 0, me, sibling, src=ins[w]))
            first += [copy(w, 1 + j, me, (*chip, c), src=ins[w]) for j, chip in enumerate(chips)]
        for cp in first:
            cp.start()
        local = [place(w) for w in range(nw)]
        passed = []
        for w in range(nw):
            for j, chip in enumerate(chips):
                copy(w, 1 + j, (*chip, c), me).wait_recv()
                fwd = copy(w, 4 + j, (*chip, c), sibling)
                fwd.start()
                passed.append(fwd)
        for w in range(nw):
            copy(w, 0, sibling, me).wait_recv()
            for j, chip in enumerate(chips):
                copy(w, 4 + j, (*chip, 1 - c), me).wait_recv()
        for cp in first + passed:
            cp.wait_send()
        for cp in local:
            cp.wait()

    any_spec = pl.BlockSpec(memory_space=pl.ANY)
    return pl.pallas_call(
        body, name=name,
        out_shape=[jax.ShapeDtypeStruct(l.full_shape, s.dtype) for s, l in zip(shards, layouts)],
        in_specs=[any_spec] * nw, out_specs=[any_spec] * nw,
        scratch_shapes=[pltpu.SemaphoreType.DMA((nw, 7)), pltpu.SemaphoreType.DMA((nw, 7)),
                        pltpu.SemaphoreType.DMA((nw,))],
    )(*shards)


HBM_SPEC = pl.BlockSpec(memory_space=pltpu.HBM)
SEM_SPEC = pl.BlockSpec(memory_space=pltpu.SEMAPHORE)
DATAFLOW = pltpu.SideEffectType.DATAFLOW_SIDE_EFFECTING


def _exchange_copies(srcs, lands, send_sems, recv_sems, src_at, dst_at):
    x, y, c = _me()
    me = 4 * x + 2 * y + c
    copies = []
    for w in range(len(srcs)):
        for k in range(1, N_DEV):
            px, py, pc = _peer(k, x, y, c)
            copies.append(pltpu.make_async_remote_copy(
                src_ref=src_at(w, srcs[w], 4 * px + 2 * py + pc), dst_ref=dst_at(w, lands[w], me, k),
                send_sem=send_sems.at[w * (N_DEV - 1) + k - 1], recv_sem=recv_sems.at[w * (N_DEV - 1) + k - 1],
                device_id=(px, py, pc), device_id_type=MESH))
    return copies


def _own_copies(srcs, lands, own_sems, src_at, dst_at):
    x, y, c = _me()
    me = 4 * x + 2 * y + c
    return [pltpu.make_async_copy(src_at(w, srcs[w], me), dst_at(w, lands[w], me, 0), own_sems.at[w])
            for w in range(len(srcs))]


def _exchange_start(name, srcs, lands, src_at, dst_at, after, own=False):
    nw = len(srcs)

    def body(*refs):
        src_refs, land_refs = refs[:nw], refs[nw:2 * nw]
        send_sems, recv_sems, own_sems = refs[2 * nw + 1], refs[2 * nw + 2], refs[2 * nw + 3]
        token = refs[-1]
        for cp in _exchange_copies(src_refs, land_refs, send_sems, recv_sems, src_at, dst_at):
            cp.start()
        if own:
            for cp in _own_copies(src_refs, land_refs, own_sems, src_at, dst_at):
                cp.start()
        token[...] = jnp.zeros_like(token)

    hbm = lambda a: pltpu.HBM(a.shape, a.dtype)
    outs = pl.pallas_call(
        body, name=name,
        out_shape=(pltpu.SemaphoreType.DMA((nw * (N_DEV - 1),)), pltpu.SemaphoreType.DMA((nw * (N_DEV - 1),)),
                   pltpu.SemaphoreType.DMA((nw,)), *[hbm(a) for a in srcs], *[hbm(a) for a in lands],
                   jax.ShapeDtypeStruct((8, 128), F32)),
        in_specs=[HBM_SPEC] * (2 * nw) + [pl.BlockSpec(memory_space=pl.ANY)],
        out_specs=(SEM_SPEC, SEM_SPEC, SEM_SPEC, *[HBM_SPEC] * (2 * nw), pl.BlockSpec(memory_space=pltpu.VMEM)),
        input_output_aliases={i: 3 + i for i in range(2 * nw)},
        compiler_params=pltpu.CompilerParams(has_side_effects=DATAFLOW),
    )(*[pltpu.with_memory_space_constraint(a, pltpu.HBM) for a in list(srcs) + list(lands)], after)
    return dict(send=outs[0], recv=outs[1], own_sems=outs[2], srcs=outs[3:3 + nw], lands=outs[3 + nw:3 + 2 * nw],
                token=outs[-1], src_at=src_at, dst_at=dst_at, name=name, own=own)


def _exchange_wait(pending, after):
    nw = len(pending["srcs"])

    def body(*refs):
        src_refs, land_refs = refs[:nw], refs[nw:2 * nw]
        send_sems, recv_sems, own_sems = refs[2 * nw], refs[2 * nw + 1], refs[2 * nw + 2]
        for cp in _exchange_copies(src_refs, land_refs, send_sems, recv_sems,
                                   pending["src_at"], pending["dst_at"]):
            cp.wait_send()
            cp.wait_recv()
        if pending["own"]:
            for cp in _own_copies(src_refs, land_refs, own_sems, pending["src_at"], pending["dst_at"]):
                cp.wait()

    hbm = lambda a: pltpu.HBM(a.shape, a.dtype)
    outs = pl.pallas_call(
        body, name=pending["name"] + "_wait",
        out_shape=(*[hbm(a) for a in pending["srcs"]], *[hbm(a) for a in pending["lands"]]),
        in_specs=[HBM_SPEC] * (2 * nw) + [SEM_SPEC, SEM_SPEC, SEM_SPEC, pl.BlockSpec(memory_space=pl.ANY)],
        out_specs=tuple([HBM_SPEC] * (2 * nw)),
        input_output_aliases={i: i for i in range(2 * nw)},
        compiler_params=pltpu.CompilerParams(has_side_effects=DATAFLOW),
    )(*pending["srcs"], *pending["lands"], pending["send"], pending["recv"], pending["own_sems"], after)
    return outs[nw:]


def _w_in_scatter_copies(src, land, send_sems, recv_sems, early):
    rows, cols = land.shape[1], land.shape[2]
    bound = EARLY_SEC * rows
    cut_dev = bound // cols
    cut = bound - cut_dev * cols
    x, y, c = _me()
    me = 4 * x + 2 * y + c

    def pieces(t):
        if early:
            return [(t > cut_dev, t * cols - bound, cols, 0), (t == cut_dev, 0, cols - cut, cut)]
        return [(t < cut_dev, t * cols, cols, 0), (t == cut_dev, cut_dev * cols, cut, 0)]

    out = []
    for k in range(1, N_DEV):
        px, py, pc = _peer(k, x, y, c)
        for (to_peer, s0, width, d0), (to_me, _, _, _) in zip(pieces(4 * px + 2 * py + pc), pieces(me)):
            s0 = s0 if isinstance(s0, int) else pl.multiple_of(jnp.maximum(s0, 0), 128)
            out.append((to_peer, to_me, pltpu.make_async_remote_copy(
                src_ref=src.at[:, pl.ds(s0, width)], dst_ref=land.at[k - 1, :, pl.ds(d0, width)],
                send_sem=send_sems.at[k - 1], recv_sem=recv_sems.at[k - 1],
                device_id=(px, py, pc), device_id_type=MESH)))
    return out


def _w_in_scatter_start(name, src, land, early, after):
    def body(src_ref, land_ref, after_ref, send_sems, recv_sems, src_thru, land_thru, token):
        for to_peer, _, cp in _w_in_scatter_copies(src_ref, land_ref, send_sems, recv_sems, early):
            pl.when(to_peer)(cp.start)
        token[...] = jnp.zeros_like(token)

    hbm = lambda a: pltpu.HBM(a.shape, a.dtype)
    outs = pl.pallas_call(
        body, name=name,
        out_shape=(pltpu.SemaphoreType.DMA((N_DEV - 1,)), pltpu.SemaphoreType.DMA((N_DEV - 1,)),
                   hbm(src), hbm(land), jax.ShapeDtypeStruct((8, 128), F32)),
        in_specs=[HBM_SPEC, HBM_SPEC, pl.BlockSpec(memory_space=pl.ANY)],
        out_specs=(SEM_SPEC, SEM_SPEC, HBM_SPEC, HBM_SPEC, pl.BlockSpec(memory_space=pltpu.VMEM)),
        input_output_aliases={0: 2, 1: 3},
        compiler_params=pltpu.CompilerParams(has_side_effects=DATAFLOW),
    )(pltpu.with_memory_space_constraint(src, pltpu.HBM), pltpu.with_memory_space_constraint(land, pltpu.HBM), after)
    return dict(send=outs[0], recv=outs[1], src=outs[2], land=outs[3], token=outs[4], early=early, name=name)


def _w_in_scatter_wait(pending, land, after):
    def body(src_ref, land_ref, send_sems, recv_sems, after_ref, src_dead, land_out):
        for to_peer, to_me, cp in _w_in_scatter_copies(src_ref, land_ref, send_sems, recv_sems, pending["early"]):
            pl.when(to_peer)(cp.wait_send)
            pl.when(to_me)(cp.wait_recv)

    hbm = lambda a: pltpu.HBM(a.shape, a.dtype)
    outs = pl.pallas_call(
        body, name=pending["name"] + "_wait", out_shape=(hbm(pending["src"]), hbm(land)),
        in_specs=[HBM_SPEC, HBM_SPEC, SEM_SPEC, SEM_SPEC, pl.BlockSpec(memory_space=pl.ANY)],
        out_specs=(HBM_SPEC, HBM_SPEC), input_output_aliases={0: 0, 1: 1},
        compiler_params=pltpu.CompilerParams(has_side_effects=DATAFLOW),
    )(pending["src"], land, pending["send"], pending["recv"], after)
    return outs[1]


def _call_after(dep, body, args, *, in_specs, **kw):
    n_in = len(args)

    def wrapped(*refs):
        body(*refs[:n_in], *refs[n_in + 1:])

    dep_spec = pl.BlockSpec(dep.shape, lambda *_: (0,) * dep.ndim)
    return pl.pallas_call(wrapped, in_specs=list(in_specs) + [dep_spec], **kw)(*args, dep)


def _resident(shape):
    return pl.BlockSpec(shape, lambda *_: (0,) * len(shape), pipeline_mode=pl.Buffered(1))


def _proj(x2, w_in, dep):
    T, D = x2.shape
    tm = min(256, T)

    def body(x_ref, w_ref, o_ref, xt_ref):
        x = x_ref[...]
        xt_ref[...] = x.T.astype(BF16)
        xb = x.astype(BF16)
        for j in range(N_SEC):
            o_ref[j] = jnp.dot(xb, w_ref[:, j * D:(j + 1) * D], preferred_element_type=F32)

    return _call_after(
        dep, body, (x2, w_in), name="proj", grid=(T // tm,),
        in_specs=[pl.BlockSpec((tm, D), lambda i: (i, 0)), _resident((D, N_SEC * D))],
        out_specs=[pl.BlockSpec((N_SEC, tm, D), lambda i: (0, i, 0)), pl.BlockSpec((D, tm), lambda i: (0, i))],
        out_shape=[jax.ShapeDtypeStruct((N_SEC, T, D), F32), jax.ShapeDtypeStruct((D, T), BF16)],
        compiler_params=_params(("parallel",)))


def _chunk_cumsum(v, reverse=False):
    rows, lanes = v.shape
    x = v.reshape(rows // SUBLANES, SUBLANES, lanes)
    pos = lax.broadcasted_iota(jnp.int32, x.shape, 1)
    for sh in (1, 2, 4):
        if reverse:
            x = x + jnp.where(pos < SUBLANES - sh, pltpu.roll(x, SUBLANES - sh, 1), 0.0)
        else:
            x = x + jnp.where(pos >= sh, pltpu.roll(x, sh, 1), 0.0)
    x = x.reshape(rows // CHUNK, CHUNK // SUBLANES, SUBLANES, lanes)
    half = lax.broadcasted_iota(jnp.int32, x.shape, 1)
    if reverse:
        x = x + jnp.where(half == 0, x[:, 1:2, 0:1, :], 0.0)
    else:
        x = x + jnp.where(half == 1, x[:, 0:1, SUBLANES - 1:SUBLANES, :], 0.0)
    return x.reshape(rows, lanes)


def _hgrn_gates(q, f_pre, lb_logits):
    l0, l1 = lb_logits[0:1, :], lb_logits[1:2, :]
    mx = jnp.maximum(l0, l1)
    e0, e1 = jnp.exp(l0 - mx), jnp.exp(l1 - mx)
    lb = e0 / (e0 + e1)
    sq = _sigmoid(q)
    qf = q * sq * Q_SCALE
    sg = _sigmoid(f_pre)
    f = lb + (1.0 - lb) * sg
    k = 1.0 - f
    log_f = jnp.log(f)
    G = _chunk_cumsum(log_f)
    g_to_end = _chunk_cumsum(log_f, reverse=True) - log_f
    e_g = jnp.exp(G)
    e_ng = jnp.exp(-G)
    e_ge = jnp.exp(g_to_end)
    return dict(lb=lb, sq=sq, qf=qf, sg=sg, f=f, k=k, G=G, e_g=e_g, e_ng=e_ng, e_ge=e_ge,
                qd=qf * e_g, ki=k * e_ng, ke=k * e_ge, dec=jnp.exp(G + g_to_end))


def _intra_mask():
    r = lax.broadcasted_iota(jnp.int32, (GROUP, GROUP), 0)
    c = lax.broadcasted_iota(jnp.int32, (GROUP, GROUP), 1)
    return (r // CHUNK == c // CHUNK) & (c <= r)


def _chunk_outer(lhs_rows, rhs_b, out_scr, sb):
    lane = lax.broadcasted_iota(jnp.int32, (GROUP, GROUP), 1) // CHUNK
    for g in range(sb // GROUP):
        sl = slice(g * GROUP, (g + 1) * GROUP)
        lhs_t = lhs_rows[sl].T
        for cc in range(CH_PER_GROUP):
            masked = jnp.where(lane == cc, lhs_t, 0.0).astype(BF16)
            out_scr[g * CH_PER_GROUP + cc] = jnp.dot(masked, rhs_b[sl], preferred_element_type=F32)


def _hgrn_forward_blocks(cs, vs, st0s, sb, o_scr, kv_scr, st_scr, dec_scr):
    nc = sb // CHUNK
    n_str = len(cs)
    mask = _intra_mask()
    bf = []
    for i, (c, v) in enumerate(zip(cs, vs)):
        qd_b, ki_b, ke_b, v_b = (c["qd"].astype(BF16), c["ki"].astype(BF16), c["ke"].astype(BF16),
                                 v.astype(BF16))
        bf.append((qd_b, ki_b, ke_b, v_b))
        for g in range(sb // GROUP):
            sl = slice(g * GROUP, (g + 1) * GROUP)
            sc = lax.dot_general(qd_b[sl], ki_b[sl], NT_DIMS, preferred_element_type=F32)
            a = jnp.where(mask, sc, 0.0).astype(BF16)
            o_scr[i, sl, :] = jnp.dot(a, v_b[sl], preferred_element_type=F32)
        _chunk_outer(v, ke_b, kv_scr.at[i], sb)
        dec_scr[i] = c["dec"]

    def rec(n, sts):
        row = pl.ds(pl.multiple_of(n * CHUNK, CHUNK), 1)
        out = []
        for i in range(n_str):
            st_scr[i, n] = sts[i]
            out.append(sts[i] * dec_scr[i, row, :] + kv_scr[i, n])
        return tuple(out)

    ends = lax.fori_loop(0, nc, rec, tuple(st0s))

    for n in range(nc):
        rows = slice(n * CHUNK, (n + 1) * CHUNK)
        for i in range(n_str):
            o_scr[i, rows, :] += lax.dot_general(bf[i][0][rows], st_scr[i, n].astype(BF16), NT_DIMS,
                                                 preferred_element_type=F32)
    return ends, bf


def _hgrn_fwd(proj5, lb_logits, gn):
    _, Bl, S, D = proj5.shape
    H = D // HEAD
    sb = min(SUB_BLOCK, S)
    nsb = S // sb
    nc = sb // CHUNK

    def body(p_ref, lbl_ref, gn_ref, ain_ref, aint_ref, st0_ref, carry, o_scr, kv_scr, st_scr, dec_scr):
        @pl.when(pl.program_id(1) == 0)
        def _():
            carry[...] = jnp.zeros_like(carry)

        st0s = [carry[b] for b in range(Bl)]
        for b in range(Bl):
            st0_ref[b, 0, 0] = st0s[b]
        cs = [_hgrn_gates(p_ref[0, b], p_ref[1, b], lbl_ref[...]) for b in range(Bl)]
        ends, _ = _hgrn_forward_blocks(cs, [p_ref[2, b] for b in range(Bl)], st0s, sb,
                                       o_scr, kv_scr, st_scr, dec_scr)
        for b in range(Bl):
            carry[b] = ends[b]
            o = o_scr[b]
            rinv = lax.rsqrt(jnp.mean(o * o, axis=-1, keepdims=True) + RMS_EPS)
            ain = o * rinv * gn_ref[...] * _sigmoid(p_ref[3, b])
            ain_ref[b] = ain.astype(BF16)
            aint_ref[b] = ain.T.astype(BF16)

    return pl.pallas_call(
        body, name="hgrn_fwd", grid=(H, nsb),
        in_specs=[pl.BlockSpec((4, Bl, sb, HEAD), lambda h, s: (0, 0, s, h)),
                  pl.BlockSpec((2, HEAD), lambda h, s: (0, h)),
                  pl.BlockSpec((1, HEAD), lambda h, s: (0, h))],
        out_specs=[pl.BlockSpec((Bl, sb, HEAD), lambda h, s: (0, s, h)),
                   pl.BlockSpec((Bl, HEAD, sb), lambda h, s: (0, h, s)),
                   pl.BlockSpec((Bl, 1, 1, HEAD, HEAD), lambda h, s: (0, h, s, 0, 0))],
        out_shape=[jax.ShapeDtypeStruct((Bl, S, D), BF16), jax.ShapeDtypeStruct((Bl, D, S), BF16),
                   jax.ShapeDtypeStruct((Bl, H, nsb, HEAD, HEAD), F32)],
        scratch_shapes=[pltpu.VMEM((Bl, HEAD, HEAD), F32), pltpu.VMEM((Bl, sb, HEAD), F32),
                        pltpu.VMEM((Bl, nc, HEAD, HEAD), F32), pltpu.VMEM((Bl, nc, HEAD, HEAD), F32),
                        pltpu.VMEM((Bl, sb, HEAD), F32)],
        compiler_params=_params(("parallel", "arbitrary")),
    )(proj5, lb_logits, gn)


def _window_count(shape, g):
    pos = lax.broadcasted_iota(jnp.int32, shape, 0)
    return pos, jnp.minimum(pos + 1, jnp.left_shift(2, g)).astype(F32)


def _select_window(g, sums):
    return jnp.where(g == 0, sums[0], jnp.where(g == 1, sums[1], jnp.where(g == 2, sums[2], sums[3])))


def _pool_fwd(proj5, w_pool):
    _, Bl, S, D = proj5.shape
    pg = D // POOL_GROUPS

    def body(v_ref, w_ref, pooled_t_ref, bp_ref):
        g = pl.program_id(1)
        v = v_ref[0, 0]
        pos, cnt = _window_count(v.shape, g)
        cur, sums = v, []
        for sh in (1, 2, 4, 8):
            cur = cur + jnp.where(pos >= sh, pltpu.roll(cur, sh, 0), 0.0)
            sums.append(cur)
        pooled = _select_window(g, sums) / cnt - v
        pooled_t_ref[...] = pooled.T.astype(BF16)
        bp_ref[0] = jnp.dot(pooled.astype(BF16), w_ref[0], preferred_element_type=F32)

    return pl.pallas_call(
        body, name="pool_fwd", grid=(Bl, POOL_GROUPS),
        in_specs=[pl.BlockSpec((1, 1, S, pg), lambda b, g: (4, b, 0, g)),
                  pl.BlockSpec((1, pg, pg), lambda b, g: (g, 0, 0))],
        out_specs=[pl.BlockSpec((pg, S), lambda b, g: (g, b)),
                   pl.BlockSpec((1, S, pg), lambda b, g: (b, 0, g))],
        out_shape=[jax.ShapeDtypeStruct((D, Bl * S), BF16), jax.ShapeDtypeStruct((Bl, S, D), F32)],
        compiler_params=_params(("parallel", "parallel")),
    )(proj5, w_pool)


def _layer_norm_fwd(r):
    mu = jnp.mean(r, axis=-1, keepdims=True)
    d = r - mu
    rs = lax.rsqrt(jnp.mean(d * d, axis=-1, keepdims=True) + LN_EPS)
    return d * rs, rs


def _layer_norm_bwd(dy_g, xhat, rs):
    return rs * (dy_g - jnp.mean(dy_g, axis=-1, keepdims=True)
                 - xhat * jnp.mean(dy_g * xhat, axis=-1, keepdims=True))


def _mix_fwd(ain, proj, bp, x2, w_a, w_out, ps, g1, b1):
    T, D = x2.shape
    tm = min(256, T)

    def body(ain_ref, ga_ref, gb_ref, bp_ref, x_ref, wa_ref, wo_ref, ps_ref, g1_ref, b1_ref,
             a_ref, mgt_ref, xh_ref, rs_ref, x1b_ref, x1t_ref):
        a = jnp.dot(ain_ref[...], wa_ref[...], preferred_element_type=F32)
        a_ref[...] = a
        merged = _sigmoid(ga_ref[0]) * a + _sigmoid(gb_ref[0]) * (bp_ref[...] * ps_ref[...])
        mgt_ref[...] = merged.T.astype(BF16)
        r1 = ALPHA * x_ref[...] + jnp.dot(merged.astype(BF16), wo_ref[...], preferred_element_type=F32)
        xhat, rs = _layer_norm_fwd(r1)
        xh_ref[...] = xhat
        rs_ref[...] = rs
        x1 = xhat * g1_ref[...] + b1_ref[...]
        x1b_ref[...] = x1.astype(BF16)
        x1t_ref[...] = x1.T.astype(BF16)

    row = lambda i: (i, 0)
    col = lambda i: (0, i)
    full = lambda i: (0, 0)
    return pl.pallas_call(
        body, name="mix_fwd", grid=(T // tm,),
        in_specs=[pl.BlockSpec((tm, D), row),
                  pl.BlockSpec((1, tm, D), lambda i: (5, i, 0)),
                  pl.BlockSpec((1, tm, D), lambda i: (6, i, 0)),
                  pl.BlockSpec((tm, D), row), pl.BlockSpec((tm, D), row),
                  pl.BlockSpec((D, D), full), pl.BlockSpec((D, D), full),
                  pl.BlockSpec((1, D), full), pl.BlockSpec((1, D), full), pl.BlockSpec((1, D), full)],
        out_specs=[pl.BlockSpec((tm, D), row), pl.BlockSpec((D, tm), col), pl.BlockSpec((tm, D), row),
                   pl.BlockSpec((tm, 1), row), pl.BlockSpec((tm, D), row), pl.BlockSpec((D, tm), col)],
        out_shape=[jax.ShapeDtypeStruct((T, D), F32), jax.ShapeDtypeStruct((D, T), BF16),
                   jax.ShapeDtypeStruct((T, D), F32), jax.ShapeDtypeStruct((T, 1), F32),
                   jax.ShapeDtypeStruct((T, D), BF16), jax.ShapeDtypeStruct((D, T), BF16)],
        compiler_params=_params(("parallel",)),
    )(ain, proj, proj, bp, x2, w_a, w_out, ps, g1, b1)


def _mlp_fwd(x1b, w_up, w_down, xhat1, tgt, g1, b1, g2, b2):
    T, D = xhat1.shape
    FF = w_up.shape[1]
    tm = min(256, T)

    def body(x_ref, wu_ref, wd_ref, xh_ref, t_ref, g1_ref, b1_ref, g2_ref, b2_ref,
             hp_ref, h_ref, dr_ref, drb_ref, drt_ref, vec_ref):
        @pl.when(pl.program_id(0) == 0)
        def _():
            vec_ref[...] = jnp.zeros_like(vec_ref)

        xb = x_ref[...]
        x1 = xh_ref[...] * g1_ref[...] + b1_ref[...]
        r2 = ALPHA * x1
        for f in range(FF // D):
            cols = slice(f * D, (f + 1) * D)
            hp = jnp.dot(xb, wu_ref[:, cols], preferred_element_type=F32)
            hp_ref[:, cols] = hp
            h = jnp.square(jnp.maximum(hp, 0.0)).astype(BF16)
            h_ref[:, cols] = h
            r2 = r2 + jnp.dot(h, wd_ref[cols, :], preferred_element_type=F32)
        xhat2, rs2 = _layer_norm_fwd(r2)
        err = xhat2 * g2_ref[...] + b2_ref[...] - t_ref[...]
        dy = err / D
        vec_ref[5:6, :] += jnp.sum(dy * xhat2, axis=0, keepdims=True)
        vec_ref[6:7, :] += jnp.sum(dy, axis=0, keepdims=True)
        vec_ref[7:8, :] += jnp.sum(0.5 * err * err / D, axis=0, keepdims=True)
        dr = _layer_norm_bwd(dy * g2_ref[...], xhat2, rs2)
        dr_ref[...] = dr
        drb_ref[...] = dr.astype(BF16)
        drt_ref[...] = dr.T.astype(BF16)

    row = lambda i: (i, 0)
    full = lambda i: (0, 0)
    return pl.pallas_call(
        body, name="mlp_fwd", grid=(T // tm,),
        in_specs=[pl.BlockSpec((tm, D), row), _resident((D, FF)), _resident((FF, D)),
                  pl.BlockSpec((tm, D), row), pl.BlockSpec((tm, D), row),
                  pl.BlockSpec((1, D), full), pl.BlockSpec((1, D), full),
                  pl.BlockSpec((1, D), full), pl.BlockSpec((1, D), full)],
        out_specs=[pl.BlockSpec((tm, FF), row), pl.BlockSpec((tm, FF), row), pl.BlockSpec((tm, D), row),
                   pl.BlockSpec((tm, D), row), pl.BlockSpec((D, tm), lambda i: (0, i)),
                   pl.BlockSpec((8, D), full)],
        out_shape=[jax.ShapeDtypeStruct((T, FF), F32), jax.ShapeDtypeStruct((T, FF), BF16),
                   jax.ShapeDtypeStruct((T, D), F32), jax.ShapeDtypeStruct((T, D), BF16),
                   jax.ShapeDtypeStruct((D, T), BF16), jax.ShapeDtypeStruct((8, D), F32)],
        compiler_params=_params(("arbitrary",)),
    )(x1b, w_up, w_down, xhat1, tgt, g1, b1, g2, b2)


def _mlp_bwd(drb, dr, hp, w_up, w_down, xhat1, rs1, g1):
    T, D = dr.shape
    FF = hp.shape[1]
    tm = min(256, T)

    def body(drb_ref, dr_ref, hp_ref, wu_ref, wd_ref, xh_ref, rs_ref, g1_ref,
             dhp_ref, d1_ref, d1b_ref, vec_ref):
        @pl.when(pl.program_id(0) == 0)
        def _():
            vec_ref[...] = jnp.zeros_like(vec_ref)

        drb = drb_ref[...]
        dx1 = ALPHA * dr_ref[...]
        for f in range(FF // D):
            cols = slice(f * D, (f + 1) * D)
            dh = lax.dot_general(drb, wd_ref[cols, :], NT_DIMS, preferred_element_type=F32)
            dhp = (dh * (2.0 * jnp.maximum(hp_ref[:, cols], 0.0))).astype(BF16)
            dhp_ref[:, cols] = dhp
            dx1 = dx1 + lax.dot_general(dhp, wu_ref[:, cols], NT_DIMS, preferred_element_type=F32)
        xhat = xh_ref[...]
        vec_ref[3:4, :] += jnp.sum(dx1 * xhat, axis=0, keepdims=True)
        vec_ref[4:5, :] += jnp.sum(dx1, axis=0, keepdims=True)
        d1 = _layer_norm_bwd(dx1 * g1_ref[...], xhat, rs_ref[...])
        d1_ref[...] = d1
        d1b_ref[...] = d1.astype(BF16)

    row = lambda i: (i, 0)
    full = lambda i: (0, 0)
    return pl.pallas_call(
        body, name="mlp_bwd", grid=(T // tm,),
        in_specs=[pl.BlockSpec((tm, D), row), pl.BlockSpec((tm, D), row), pl.BlockSpec((tm, FF), row),
                  _resident((D, FF)), _resident((FF, D)),
                  pl.BlockSpec((tm, D), row), pl.BlockSpec((tm, 1), row), pl.BlockSpec((1, D), full)],
        out_specs=[pl.BlockSpec((tm, FF), row), pl.BlockSpec((tm, D), row), pl.BlockSpec((tm, D), row),
                   pl.BlockSpec((8, D), full)],
        out_shape=[jax.ShapeDtypeStruct((T, FF), BF16), jax.ShapeDtypeStruct((T, D), F32),
                   jax.ShapeDtypeStruct((T, D), BF16), jax.ShapeDtypeStruct((8, D), F32)],
        compiler_params=_params(("arbitrary",)),
    )(drb, dr, hp, w_up, w_down, xhat1, rs1, g1)


def _dw(name, a_t, b, n_j, a_spec, b_spec, o_shape, o_block, o_map, transpose_out=False, dep=None,
        into=(None, None), ob_shape=None, ob_map=None):
    def body(*refs):
        a_ref, b_ref, o_ref, ob_ref = refs[0], refs[1], refs[-2], refs[-1]
        b_val = b_ref[0] if len(b_ref.shape) == 3 else b_ref[...]
        if len(a_ref.shape) == 3:
            seq = a_ref.shape[2]
            p = sum(jnp.dot(a_ref[i], b_val[i * seq:(i + 1) * seq], preferred_element_type=F32)
                    for i in range(a_ref.shape[0]))
        else:
            p = jnp.dot(a_ref[...], b_val, preferred_element_type=F32)
        if transpose_out:
            p = p.T
        p = p.reshape(o_ref.shape)
        o_ref[...] = p
        ob_ref[...] = p.astype(BF16)

    kw = dict(name=name, grid=(n_j,), in_specs=[a_spec, b_spec],
              out_specs=[pl.BlockSpec(o_block, o_map), pl.BlockSpec(o_block, ob_map or o_map)],
              out_shape=[jax.ShapeDtypeStruct(o_shape, F32), jax.ShapeDtypeStruct(ob_shape or o_shape, BF16)],
              compiler_params=_params(("parallel",)))
    args = (a_t, b)
    aliases = {}
    for out_index, arr in enumerate(into):
        if arr is not None:
            aliases[len(args)] = out_index
            args = args + (arr,)
            kw["in_specs"] = kw["in_specs"] + [pl.BlockSpec(memory_space=pl.ANY)]
    if aliases:
        kw["input_output_aliases"] = aliases
    if dep is None:
        return pl.pallas_call(body, **kw)(*args)
    return _call_after(dep, body, args, **kw)


def _mix_bwd(d1b, proj, a, bp, w_a, w_out, w_pool, ps, dep):
    T, D = a.shape
    tm = min(256, T)
    pg = D // POOL_GROUPS

    def body(d1b_ref, ga_ref, gb_ref, a_ref, bp_ref, wa_ref, wo_ref, wp_ref, ps_ref,
             da_ref, dbp_ref, dain_ref, dpl_ref, dg_ref, vec_ref):
        @pl.when(pl.program_id(0) == 0)
        def _():
            vec_ref[...] = jnp.zeros_like(vec_ref)

        dm = lax.dot_general(d1b_ref[...], wo_ref[...], NT_DIMS, preferred_element_type=F32)
        sa, sg = _sigmoid(ga_ref[0]), _sigmoid(gb_ref[0])
        bp_v, ps_v = bp_ref[...], ps_ref[...]
        da = (dm * sa).astype(BF16)
        db = dm * sg
        dg_ref[0] = (dm * a_ref[...] * sa * (1.0 - sa)).astype(BF16)
        dg_ref[1] = (dm * (bp_v * ps_v) * sg * (1.0 - sg)).astype(BF16)
        vec_ref[2:3, :] += jnp.sum(db * bp_v, axis=0, keepdims=True)
        dbp = (db * ps_v).astype(BF16)
        da_ref[...] = da
        dbp_ref[...] = dbp
        dain_ref[...] = lax.dot_general(da, wa_ref[...], NT_DIMS, preferred_element_type=F32)
        for g in range(POOL_GROUPS):
            cols = slice(g * pg, (g + 1) * pg)
            dpl_ref[:, cols] = lax.dot_general(dbp[:, cols], wp_ref[g], NT_DIMS,
                                               preferred_element_type=F32)

    row = lambda i: (i, 0)
    full = lambda i: (0, 0)
    return _call_after(
        dep, body, (d1b, proj, proj, a, bp, w_a, w_out, w_pool, ps), name="mix_bwd", grid=(T // tm,),
        in_specs=[pl.BlockSpec((tm, D), row),
                  pl.BlockSpec((1, tm, D), lambda i: (5, i, 0)),
                  pl.BlockSpec((1, tm, D), lambda i: (6, i, 0)),
                  pl.BlockSpec((tm, D), row), pl.BlockSpec((tm, D), row),
                  pl.BlockSpec((D, D), full), pl.BlockSpec((D, D), full),
                  pl.BlockSpec((POOL_GROUPS, pg, pg), lambda i: (0, 0, 0)),
                  pl.BlockSpec((1, D), full)],
        out_specs=[pl.BlockSpec((tm, D), row), pl.BlockSpec((tm, D), row),
                   pl.BlockSpec((tm, D), row), pl.BlockSpec((tm, D), row),
                   pl.BlockSpec((2, tm, D), lambda i: (0, i, 0)),
                   pl.BlockSpec((8, D), full)],
        out_shape=[jax.ShapeDtypeStruct((T, D), BF16), jax.ShapeDtypeStruct((T, D), BF16),
                   jax.ShapeDtypeStruct((T, D), F32), jax.ShapeDtypeStruct((T, D), F32),
                   jax.ShapeDtypeStruct((2, T, D), BF16), jax.ShapeDtypeStruct((8, D), F32)],
        compiler_params=_params(("arbitrary",)))


def _pool_bwd(dpooled3, dep):
    Bl, S, D = dpooled3.shape
    pg = D // POOL_GROUPS

    def body(dp_ref, dv_ref):
        g = pl.program_id(1)
        dp = dp_ref[0]
        pos, cnt = _window_count(dp.shape, g)
        cur, sums = dp / cnt, []
        for sh in (1, 2, 4, 8):
            cur = cur + jnp.where(pos < S - sh, pltpu.roll(cur, S - sh, 0), 0.0)
            sums.append(cur)
        dv_ref[0] = (_select_window(g, sums) - dp).astype(BF16)

    spec = pl.BlockSpec((1, S, pg), lambda b, g: (b, 0, g))
    return _call_after(
        dep, body, (dpooled3,), name="pool_bwd", grid=(Bl, POOL_GROUPS), in_specs=[spec], out_specs=spec,
        out_shape=jax.ShapeDtypeStruct((Bl, S, D), BF16),
        compiler_params=_params(("parallel", "parallel")))


def _hgrn_bwd(proj5, lb_logits, gn, dain3, st0_all, dep):
    _, Bl, S, D = proj5.shape
    H = D // HEAD
    sb = min(SUB_BLOCK, S)
    nsb = S // sb
    nc = sb // CHUNK
    streams = range(Bl)

    def body(p_ref, lbl_ref, gn_ref, dain_ref, st0_ref, d_ref, vec_ref,
             dcarry, o_scr, kv_scr, st_scr, dst_scr, dec_scr, dvi_scr, dke_scr, dqi_scr):
        s = pl.program_id(1)

        @pl.when(s == 0)
        def _():
            dcarry[...] = jnp.zeros_like(dcarry)
            vec_ref[...] = jnp.zeros_like(vec_ref)

        qs, vs, ogs = [p_ref[0, b] for b in streams], [p_ref[2, b] for b in streams], [p_ref[3, b] for b in streams]
        cs = [_hgrn_gates(qs[b], p_ref[1, b], lbl_ref[...]) for b in streams]
        _, bf = _hgrn_forward_blocks(cs, vs, [st0_ref[b, 0, 0] for b in streams], sb,
                                     o_scr, kv_scr, st_scr, dec_scr)
        mask = _intra_mask()
        gn_v = gn_ref[...]
        keep = []
        for b in streams:
            qd_b, ki_b, ke_b, v_b = bf[b]
            o = o_scr[b]
            rinv = lax.rsqrt(jnp.mean(o * o, axis=-1, keepdims=True) + RMS_EPS)
            on = o * rinv
            so = _sigmoid(ogs[b])
            dain = dain_ref[b]
            vec_ref[1:2, :] += jnp.sum(dain * on * so, axis=0, keepdims=True)
            d_og = dain * on * gn_v * so * (1.0 - so)
            d_on = dain * gn_v * so
            do = rinv * (d_on - on * jnp.mean(d_on * on, axis=-1, keepdims=True))
            do_b = do.astype(BF16)
            dv_parts, dqd_parts, dki_parts = [], [], []
            for g in range(sb // GROUP):
                sl = slice(g * GROUP, (g + 1) * GROUP)
                sc = lax.dot_general(qd_b[sl], ki_b[sl], NT_DIMS, preferred_element_type=F32)
                a = jnp.where(mask, sc, 0.0).astype(BF16)
                da = lax.dot_general(do_b[sl], v_b[sl], NT_DIMS, preferred_element_type=F32)
                da = jnp.where(mask, da, 0.0).astype(BF16)
                dv_parts.append(lax.dot_general(a, do_b[sl], TN_DIMS, preferred_element_type=F32))
                dqd_parts.append(jnp.dot(da, ki_b[sl], preferred_element_type=F32))
                dki_parts.append(lax.dot_general(da, qd_b[sl], TN_DIMS, preferred_element_type=F32))
            keep.append(dict(d_og=d_og, do_b=do_b, dv_intra=jnp.concatenate(dv_parts, axis=0),
                             dqd_intra=jnp.concatenate(dqd_parts, axis=0),
                             dki=jnp.concatenate(dki_parts, axis=0)))
            _chunk_outer(do, qd_b, kv_scr.at[b], sb)

        def rrec(i, dsts):
            n = nc - 1 - i
            row = pl.ds(pl.multiple_of(n * CHUNK, CHUNK), 1)
            out = []
            for b in streams:
                dst_scr[b, n] = dsts[b]
                out.append(dsts[b] * dec_scr[b, row, :] + kv_scr[b, n])
            return tuple(out)

        ends = lax.fori_loop(0, nc, rrec, tuple(dcarry[b] for b in streams))
        for b in streams:
            dcarry[b] = ends[b]
        for n in range(nc):
            rows = slice(n * CHUNK, (n + 1) * CHUNK)
            for b in streams:
                qd_b, ki_b, ke_b, v_b = bf[b]
                dst_b = dst_scr[b, n].astype(BF16)
                dvi_scr[b, rows, :] = lax.dot_general(ke_b[rows], dst_b, NT_DIMS, preferred_element_type=F32)
                dke_scr[b, rows, :] = jnp.dot(v_b[rows], dst_b, preferred_element_type=F32)
                dqi_scr[b, rows, :] = jnp.dot(keep[b]["do_b"][rows], st_scr[b, n].astype(BF16),
                                              preferred_element_type=F32)
        for b in streams:
            c, k = cs[b], keep[b]
            ddec = jnp.sum(dst_scr[b] * st_scr[b], axis=1)
            dgl = jnp.broadcast_to(ddec[:, None, :], (nc, CHUNK, HEAD)).reshape(sb, HEAD) * c["dec"]
            dqd = k["dqd_intra"] + dqi_scr[b]
            dke = dke_scr[b]
            dki = k["dki"]
            t_ke = dke * c["ke"]
            dG = dqd * c["qd"] - dki * c["ki"] - t_ke
            dgl = dgl + _chunk_cumsum(t_ke) + _chunk_cumsum(t_ke, reverse=True) - t_ke
            dlogf = _chunk_cumsum(dG, reverse=True) + dgl
            dk = dki * c["e_ng"] + dke * c["e_ge"]
            df = dlogf / c["f"] - dk
            sg, sq, lb, q = c["sg"], c["sq"], c["lb"], qs[b]
            vec_ref[0:1, :] += jnp.sum(df * (1.0 - sg), axis=0, keepdims=True)
            d_ref[0, b] = (dqd * c["e_g"] * Q_SCALE * (sq + q * sq * (1.0 - sq))).astype(BF16)
            d_ref[1, b] = (df * (1.0 - lb) * sg * (1.0 - sg)).astype(BF16)
            d_ref[2, b] = (k["dv_intra"] + dvi_scr[b]).astype(BF16)
            d_ref[3, b] = k["d_og"].astype(BF16)

    rev = lambda s: nsb - 1 - s
    big = pltpu.VMEM((Bl, nc, HEAD, HEAD), F32)
    rows_f32 = pltpu.VMEM((Bl, sb, HEAD), F32)
    return _call_after(
        dep, body, (proj5, lb_logits, gn, dain3, st0_all), name="hgrn_bwd", grid=(H, nsb),
        in_specs=[pl.BlockSpec((4, Bl, sb, HEAD), lambda h, s: (0, 0, rev(s), h)),
                  pl.BlockSpec((2, HEAD), lambda h, s: (0, h)),
                  pl.BlockSpec((1, HEAD), lambda h, s: (0, h)),
                  pl.BlockSpec((Bl, sb, HEAD), lambda h, s: (0, rev(s), h)),
                  pl.BlockSpec((Bl, 1, 1, HEAD, HEAD), lambda h, s: (0, h, rev(s), 0, 0))],
        out_specs=[pl.BlockSpec((4, Bl, sb, HEAD), lambda h, s: (0, 0, rev(s), h)),
                   pl.BlockSpec((8, HEAD), lambda h, s: (0, h))],
        out_shape=[jax.ShapeDtypeStruct((4, Bl, S, D), BF16), jax.ShapeDtypeStruct((8, D), F32)],
        scratch_shapes=[pltpu.VMEM((Bl, HEAD, HEAD), F32), rows_f32, big, big, big, rows_f32,
                        rows_f32, rows_f32, rows_f32],
        compiler_params=_params(("parallel", "arbitrary")))


def _dx(d1, dh4, dpv, dg2, w_in, dep):
    T, D = d1.shape
    tm = min(256, T)

    def body(d1_ref, dh_ref, dp_ref, dg_ref, w_ref, o_ref):
        blocks = [dh_ref[0], dh_ref[1], dh_ref[2], dh_ref[3], dp_ref[...], dg_ref[0], dg_ref[1]]
        acc = ALPHA * d1_ref[...]
        for j, blk in enumerate(blocks):
            acc = acc + lax.dot_general(blk, w_ref[:, j * D:(j + 1) * D], NT_DIMS, preferred_element_type=F32)
        o_ref[...] = acc

    row = lambda i: (i, 0)
    return _call_after(
        dep, body, (d1, dh4, dpv, dg2, w_in), name="dx", grid=(T // tm,),
        in_specs=[pl.BlockSpec((tm, D), row), pl.BlockSpec((4, tm, D), lambda i: (0, i, 0)),
                  pl.BlockSpec((tm, D), row), pl.BlockSpec((2, tm, D), lambda i: (0, i, 0)),
                  _resident((D, N_SEC * D))],
        out_specs=pl.BlockSpec((tm, D), row),
        out_shape=jax.ShapeDtypeStruct((T, D), F32),
        compiler_params=_params(("parallel",)))


EARLY_SEC = 4
DW_COLS = 512


def _dw_in_part(name, x_t, b, sections, first_sec, into, dep, ob_shape, ob_first):
    D, T = x_t.shape
    per = D // DW_COLS
    b_spec = (pl.BlockSpec((1, T, DW_COLS), lambda j: (j // per, 0, j % per)) if b.ndim == 3
              else pl.BlockSpec((T, DW_COLS), lambda j: (0, j)))
    return _dw(name, x_t, b, sections * per, _resident((D, T)), b_spec, (D, N_SEC * D), (D, DW_COLS),
               lambda j: (0, first_sec * per + j), dep=dep, into=into, ob_shape=ob_shape,
               ob_map=lambda j: (0, ob_first * per + j))


def _dw_in_early(x_t, dpv, dg2, dep):
    D = x_t.shape[0]
    early_shape = (D, (N_SEC - EARLY_SEC) * D)
    f32, bf = _dw_in_part("dw_in_gates", x_t, dg2, 2, 5, (None, None), dep, early_shape, 1)
    return _dw_in_part("dw_in_pool", x_t, dpv, 1, 4, (f32, bf), None, early_shape, 0)


def _dw_in_late(x_t, dh4, f32_early):
    D = x_t.shape[0]
    return _dw_in_part("dw_in_rec", x_t, dh4, EARLY_SEC, 0, (f32_early, None), None, (D, EARLY_SEC * D), 0)


def _adam_shard(name, me_arr, grad, land, layout, w, m, v):
    shape = layout.shape
    n_split = 4
    blk = (shape[0] // n_split,) + shape[1:]
    zeros = (0,) * (len(shape) - 1)

    def body(me_ref, g_ref, r_ref, w_ref, m_ref, v_ref, g_out, d_out, m_out, v_out):
        g = g_ref[...]
        for k in range(N_DEV - 1):
            g = g + r_ref[k].astype(F32)
        d, m2, v2 = _adamw(w_ref[...], g, m_ref[...], v_ref[...])
        g_out[...] = g
        d_out[...] = d
        m_out[...] = m2
        v_out[...] = v2

    def own(i, me_ref):
        bi = layout.block_index(me_ref[0])
        return (bi[0] * n_split + i,) + tuple(bi[1:]) if layout.kind == "row" else (i,) + tuple(bi[1:])

    plain = pl.BlockSpec(blk, lambda i, me_ref: (i,) + zeros)
    grid_spec = pltpu.PrefetchScalarGridSpec(
        num_scalar_prefetch=1, grid=(n_split,),
        in_specs=[pl.BlockSpec(blk, own),
                  pl.BlockSpec((N_DEV - 1,) + blk, lambda i, me_ref: (0, i) + zeros),
                  plain, plain, plain],
        out_specs=[plain] * 4)
    return pl.pallas_call(
        body, name=name, grid_spec=grid_spec,
        out_shape=[jax.ShapeDtypeStruct(shape, F32)] * 4,
        compiler_params=_params(("parallel",)),
    )(me_arr, grad, land, w, m, v)


def _vec_allreduce(vec):
    D = vec.shape[1]

    def body(vec_ref, tot_ref, gat, send_sems, recv_sems):
        x, y, c = _me()
        me = 4 * x + 2 * y + c
        gat[me] = vec_ref[...]
        copies = []
        for k in range(1, N_DEV):
            cp = pltpu.make_async_remote_copy(
                src_ref=vec_ref, dst_ref=gat.at[me], send_sem=send_sems.at[k - 1],
                recv_sem=recv_sems.at[k - 1], device_id=_peer(k, x, y, c), device_id_type=MESH)
            cp.start()
            copies.append(cp)
        for cp in copies:
            cp.wait()
        tot = gat[0]
        for d in range(1, N_DEV):
            tot = tot + gat[d]
        tot_ref[...] = tot

    vm = pl.BlockSpec(memory_space=pltpu.VMEM)
    return pl.pallas_call(
        body, name="vec_allreduce", out_shape=jax.ShapeDtypeStruct(vec.shape, F32),
        in_specs=[vm], out_specs=vm,
        scratch_shapes=[pltpu.VMEM((N_DEV, 8, D), F32), pltpu.SemaphoreType.DMA((N_DEV - 1,)),
                        pltpu.SemaphoreType.DMA((N_DEV - 1,))],
    )(vec)


def _vec_adam(tot, small_w, small_m, small_v):
    n = len(small_w)

    def body(*refs):
        tot = refs[0][...]
        ws, ms, vs = refs[1:1 + n], refs[1 + n:1 + 2 * n], refs[1 + 2 * n:1 + 3 * n]
        outs = refs[1 + 3 * n:]
        loss_ref, g_out, d_out = outs[0], outs[1:1 + n], outs[1 + n:1 + 2 * n]
        m_out, v_out = outs[1 + 2 * n:1 + 3 * n], outs[1 + 3 * n:1 + 4 * n]
        loss_ref[...] = jnp.broadcast_to(jnp.sum(tot[7:8, :], axis=1, keepdims=True), loss_ref.shape)
        lbl = ws[0][...]
        mx = jnp.maximum(lbl[0:1, :], lbl[1:2, :])
        e0, e1 = jnp.exp(lbl[0:1, :] - mx), jnp.exp(lbl[1:2, :] - mx)
        p0 = e0 / (e0 + e1)
        dl0 = tot[0:1, :] * p0 * (1.0 - p0)
        grads = [jnp.concatenate([dl0, -dl0], axis=0)] + [tot[r:r + 1, :] for r in range(1, n)]
        for i in range(n):
            d, m2, v2 = _adamw(ws[i][...], grads[i], ms[i][...], vs[i][...])
            g_out[i][...] = grads[i]
            d_out[i][...] = d
            m_out[i][...] = m2
            v_out[i][...] = v2

    vm = pl.BlockSpec(memory_space=pltpu.VMEM)
    shapes = [jax.ShapeDtypeStruct(w.shape, F32) for w in small_w]
    return pl.pallas_call(
        body, name="vec_adam",
        out_shape=[jax.ShapeDtypeStruct((1, 128), F32)] + shapes * 4,
        in_specs=[vm] * (1 + 3 * n), out_specs=[vm] * (1 + 4 * n),
    )(tot, *small_w, *small_m, *small_v)


def kernel(x, w_in, lb_logits, hgrn_norm_g, w_a, w_pool, pool_scale, w_out, ln1_g, ln1_b, w_up, w_down, ln2_g, ln2_b, loss_target, m_w_in, m_lb_logits, m_hgrn_norm_g, m_w_a, m_w_pool, m_pool_scale, m_w_out, m_ln1_g, m_ln1_b, m_w_up, m_w_down, m_ln2_g, m_ln2_b, v_w_in, v_lb_logits, v_hgrn_norm_g, v_w_a, v_w_pool, v_pool_scale, v_w_out, v_ln1_g, v_ln1_b, v_w_up, v_w_down, v_ln2_g, v_ln2_b):
    Bl, S, D = x.shape
    T = Bl * S
    pg = D // POOL_GROUPS
    x2 = x.reshape(T, D)
    tgt = loss_target.reshape(T, D)
    me = 4 * lax.axis_index("x") + 2 * lax.axis_index("y") + lax.axis_index("c")
    me_arr = jnp.reshape(me, (1,)).astype(jnp.int32)

    names = ["w_in", "w_a", "w_pool", "w_out", "w_up", "w_down"]
    big_w = dict(zip(names, [w_in[0], w_a[0], w_pool[0], w_out[0], w_up[0], w_down[0]]))
    big_m = dict(zip(names, [m_w_in[0], m_w_a[0], m_w_pool[0], m_w_out[0], m_w_up[0], m_w_down[0]]))
    big_v = dict(zip(names, [v_w_in[0], v_w_a[0], v_w_pool[0], v_w_out[0], v_w_up[0], v_w_down[0]]))
    kinds = dict(w_in="col", w_a="row", w_pool="pool", w_out="row", w_up="col", w_down="row")
    lay = {nm: _Sharded(kinds[nm], big_w[nm].shape) for nm in names}
    wb = {nm: big_w[nm].astype(BF16) for nm in names}

    (w_in_f,) = _all_gather("ag_w_in", [wb["w_in"]], [lay["w_in"]])
    def gather_start(name, nms, after):
        return _exchange_start(name, [wb[nm] for nm in nms], [lax.empty(lay[nm].full_shape, BF16) for nm in nms],
                               src_at=lambda w, ref, peer: ref,
                               dst_at=lambda w, ref, mine, k: lay[nms[w]].at(ref, mine), after=after, own=True)

    ag_mix = gather_start("ag_mix", ["w_a", "w_pool", "w_out"], w_in_f)
    ag_mlp = gather_start("ag_mlp", ["w_up", "w_down"], ag_mix["token"])

    proj, x_t = _proj(x2, w_in_f, ag_mlp["token"])
    proj5 = proj.reshape(N_SEC, Bl, S, D)
    ain3, ain_t, st0_all = _hgrn_fwd(proj5, lb_logits, hgrn_norm_g)
    w_a_f, w_pool_f, w_out_f = _exchange_wait(ag_mix, ain3)
    pooled_t, bp3 = _pool_fwd(proj5, w_pool_f)
    ain, bp = ain3.reshape(T, D), bp3.reshape(T, D)
    a, merged_t, xhat1, rs1, x1b, x1_t = _mix_fwd(ain, proj, bp, x2, w_a_f, w_out_f, pool_scale, ln1_g, ln1_b)
    w_up_f, w_down_f = _exchange_wait(ag_mlp, x1b)
    hp, h, dr2, dr2b, dr2_t, vec_mlp = _mlp_fwd(x1b, w_up_f, w_down_f, xhat1, tgt, ln1_g, ln1_b, ln2_g, ln2_b)

    def scatter_start(name, nms, grads_b, after):
        lands = [lax.empty((N_DEV - 1,) + lay[nm].shape, BF16) for nm in nms]
        return _exchange_start(name, grads_b, lands,
                               src_at=lambda w, ref, peer: lay[nms[w]].at(ref, peer),
                               dst_at=lambda w, ref, mine, k: ref.at[k - 1], after=after)

    dhp, dr1, dr1b, vec_ln1 = _mlp_bwd(dr2b, dr2, hp, w_up_f, w_down_f, xhat1, rs1, ln1_g)
    FF = 4 * D
    whole_t = _resident((D, T))
    cols_b = pl.BlockSpec((T, DW_COLS), lambda j: (0, j))
    cols_o = ((D, DW_COLS), lambda j: (0, j))
    gw, gwb = {}, {}
    gw["w_down"], gwb["w_down"] = _dw(
        "dw_down", dr2_t, h, FF // DW_COLS, whole_t, cols_b, (FF, D), (DW_COLS, D), lambda j: (j, 0),
        transpose_out=True)
    rs_down = scatter_start("rs_w_down", ["w_down"], [gwb["w_down"]], gw["w_down"])
    gw["w_up"], gwb["w_up"] = _dw("dw_up", x1_t, dhp, FF // DW_COLS, whole_t, cols_b, (D, FF), *cols_o,
                                  dep=rs_down["token"])
    rs_up = scatter_start("rs_w_up", ["w_up"], [gwb["w_up"]], gw["w_up"])
    da_b, dbp_b, dain, dpooled, dg2, vec_mix = _mix_bwd(dr1b, proj, a, bp, w_a_f, w_out_f, w_pool_f, pool_scale,
                                                        rs_up["token"])
    gw["w_out"], gwb["w_out"] = _dw("dw_out", merged_t, dr1b, D // DW_COLS, whole_t, cols_b, (D, D), *cols_o)
    gw["w_a"], gwb["w_a"] = _dw("dw_a", ain_t, da_b, D // DW_COLS, _resident((Bl, D, S)), cols_b, (D, D), *cols_o)
    gw["w_pool"], gwb["w_pool"] = _dw(
        "dw_pool", pooled_t, dbp_b, POOL_GROUPS, pl.BlockSpec((pg, T), lambda j: (j, 0)),
        pl.BlockSpec((T, pg), lambda j: (0, j)), (POOL_GROUPS, pg, pg), (1, pg, pg), lambda j: (j, 0, 0))
    mid = ["w_out", "w_a", "w_pool"]
    rs_mid = scatter_start("rs_w_mid", mid, [gwb[nm] for nm in mid], gw["w_pool"])
    dpv = _pool_bwd(dpooled.reshape(Bl, S, D), rs_mid["token"]).reshape(T, D)
    gw_in_early, gwb_in_early = _dw_in_early(x_t, dpv, dg2, rs_mid["token"])
    land_in = lax.empty((N_DEV - 1,) + lay["w_in"].shape, BF16)
    rs_in_early = _w_in_scatter_start("rs_w_in_early", gwb_in_early, land_in, True, gw_in_early)
    dh4, vec_hgrn = _hgrn_bwd(proj5, lb_logits, hgrn_norm_g, dain.reshape(Bl, S, D), st0_all,
                              rs_in_early["token"])
    dh4 = dh4.reshape(4, T, D)
    gw["w_in"], gwb_in_late = _dw_in_late(x_t, dh4, gw_in_early)
    rs_in_late = _w_in_scatter_start("rs_w_in_late", gwb_in_late, rs_in_early["land"], False, gw["w_in"])
    grad_x2 = _dx(dr1, dh4, dpv, dg2, w_in_f, rs_in_late["token"])
    grad_x = grad_x2.reshape(Bl, S, D)

    vec = vec_mlp + vec_ln1 + vec_mix + vec_hgrn
    small_names = ["lb_logits", "hgrn_norm_g", "pool_scale", "ln1_g", "ln1_b", "ln2_g", "ln2_b"]
    small_w = [lb_logits, hgrn_norm_g, pool_scale, ln1_g, ln1_b, ln2_g, ln2_b]
    small_m = [m_lb_logits, m_hgrn_norm_g, m_pool_scale, m_ln1_g, m_ln1_b, m_ln2_g, m_ln2_b]
    small_v = [v_lb_logits, v_hgrn_norm_g, v_pool_scale, v_ln1_g, v_ln1_b, v_ln2_g, v_ln2_b]
    res = _vec_adam(_vec_allreduce(vec), small_w, small_m, small_v)
    loss = res[0][0, 0]
    n = len(small_w)
    small = {nm: (res[1 + i], res[1 + n + i], res[1 + 2 * n + i], res[1 + 3 * n + i])
             for i, nm in enumerate(small_names)}

    big, last = {}, grad_x2

    def adam(nm, land):
        outs = _adam_shard("adam_" + nm, me_arr, gw[nm], land, lay[nm], big_w[nm], big_m[nm], big_v[nm])
        big[nm] = tuple(t[None] for t in outs)
        return outs[0]

    for pend, nms in ((rs_down, ["w_down"]), (rs_up, ["w_up"]), (rs_mid, mid)):
        for nm, land in zip(nms, _exchange_wait(pend, last)):
            last = adam(nm, land)
    land_in = _w_in_scatter_wait(rs_in_early, rs_in_late["land"], last)
    adam("w_in", _w_in_scatter_wait(rs_in_late, land_in, res[0]))

    order = ["w_in", "lb_logits", "hgrn_norm_g", "w_a", "w_pool", "pool_scale", "w_out", "ln1_g", "ln1_b",
             "w_up", "w_down", "ln2_g", "ln2_b"]
    allp = {**big, **small}
    out = [loss, grad_x]
    for part in range(4):
        out += [allp[nm][part] for nm in order]
    return tuple(out)
```

```python
import jax
import jax.numpy as jnp
from jax import lax
from jax.experimental import pallas as pl
from jax.experimental.pallas import tpu as pltpu

F32 = jnp.float32
BF16 = jnp.bfloat16
MESH = pl.DeviceIdType.MESH

N_DEV = 8
HEAD = 128
CHUNK = 16
SUBLANES = 8
GROUP = 128
SUB_BLOCK = 1024
CH_PER_GROUP = GROUP // CHUNK
N_SEC = 7
POOL_GROUPS = 4
ALPHA = (2.0 * 1) ** 0.25
LN_EPS = 1e-5
RMS_EPS = 1e-6
Q_SCALE = HEAD ** -0.5
ADAM_LR = 0.001
ADAM_B1 = 0.9
ADAM_B2 = 0.999
ADAM_EPS = 1e-08
ADAM_WD = 0.01
ADAM_STEP = 10
VMEM_LIMIT = 60 << 20

NT_DIMS = (((1,), (1,)), ((), ()))
TN_DIMS = (((0,), (0,)), ((), ()))


def _params(sem=None):
    kw = dict(vmem_limit_bytes=VMEM_LIMIT)
    if sem is not None:
        kw["dimension_semantics"] = sem
    return pltpu.CompilerParams(**kw)


def _me():
    return lax.axis_index("x"), lax.axis_index("y"), lax.axis_index("c")


def _sigmoid(v):
    return jax.nn.sigmoid(v)


def _adamw(w, g, m, v):
    m = ADAM_B1 * m + (1.0 - ADAM_B1) * g
    v = ADAM_B2 * v + (1.0 - ADAM_B2) * jnp.square(g)
    m_hat = m / (1.0 - ADAM_B1 ** ADAM_STEP)
    v_hat = v / (1.0 - ADAM_B2 ** ADAM_STEP)
    delta = -ADAM_LR * (m_hat / (jnp.sqrt(v_hat) + ADAM_EPS) + ADAM_WD * w)
    return delta, m, v


class _Sharded:
    def __init__(self, kind, shard_shape):
        self.kind, self.shape = kind, tuple(shard_shape)

    @property
    def full_shape(self):
        r = self.shape
        if self.kind == "row":
            return (N_DEV * r[0],) + r[1:]
        return (r[0], N_DEV * r[1]) + r[2:]

    def at(self, ref, d):
        if self.kind == "col":
            n = self.shape[1]
            return ref.at[:, pl.ds(pl.multiple_of(d * n, 128), n)]
        if self.kind == "row":
            n = self.shape[0]
            return ref.at[pl.ds(pl.multiple_of(d * n, 16), n), :]
        n = self.shape[1]
        return ref.at[:, pl.ds(pl.multiple_of(d * n, 16), n), :]

    def block_index(self, d):
        return {"col": (0, d), "row": (d, 0), "pool": (0, d, 0)}[self.kind]


def _peer(k, x, y, c):
    return (1 - x if k & 4 else x, 1 - y if k & 2 else y, 1 - c if k & 1 else c)


def _all_gather(name, shards, layouts):
    nw = len(shards)

    def body(*refs):
        ins, outs = refs[:nw], refs[nw:2 * nw]
        send_sems, recv_sems, local_sems = refs[2 * nw:]
        x, y, c = _me()
        me = (x, y, c)
        sibling = (x, y, 1 - c)
        chips = [(1 - x, y), (x, 1 - y), (1 - x, 1 - y)]

        def copy(w, k, block, to, src=None):
            px, py, pc = block
            dst = layouts[w].at(outs[w], 4 * px + 2 * py + pc)
            return pltpu.make_async_remote_copy(
                src_ref=dst if src is None else src, dst_ref=dst,
                send_sem=send_sems.at[w, k], recv_sem=recv_sems.at[w, k],
                device_id=to, device_id_type=MESH)

        def place(w):
            mine = pltpu.make_async_copy(ins[w], layouts[w].at(outs[w], 4 * x + 2 * y + c), local_sems.at[w])
            mine.start()
            return mine

        first = []
        for w in range(nw):
            first.append(copy(w, 0, me, sibling, src=ins[w]))
            first += [copy(w, 1 + j, me, (*chip, c), src=ins[w]) for j, chip in enumerate(chips)]
        for cp in first:
            cp.start()
        local = [place(w) for w in range(nw)]
        passed = []
        for w in range(nw):
            for j, chip in enumerate(chips):
                copy(w, 1 + j, (*chip, c), me).wait_recv()
                fwd = copy(w, 4 + j, (*chip, c), sibling)
                fwd.start()
                passed.append(fwd)
        for w in range(nw):
            copy(w, 0, sibling, me).wait_recv()
            for j, chip in enumerate(chips):
                copy(w, 4 + j, (*chip, 1 - c), me).wait_recv()
        for cp in first + passed:
            cp.wait_send()
        for cp in local:
            cp.wait()

    any_spec = pl.BlockSpec(memory_space=pl.ANY)
    return pl.pallas_call(
        body, name=name,
        out_shape=[jax.ShapeDtypeStruct(l.full_shape, s.dtype) for s, l in zip(shards, layouts)],
        in_specs=[any_spec] * nw, out_specs=[any_spec] * nw,
        scratch_shapes=[pltpu.SemaphoreType.DMA((nw, 7)), pltpu.SemaphoreType.DMA((nw, 7)),
                        pltpu.SemaphoreType.DMA((nw,))],
    )(*shards)


HBM_SPEC = pl.BlockSpec(memory_space=pltpu.HBM)
SEM_SPEC = pl.BlockSpec(memory_space=pltpu.SEMAPHORE)
DATAFLOW = pltpu.SideEffectType.DATAFLOW_SIDE_EFFECTING


def _exchange_copies(srcs, lands, send_sems, recv_sems, src_at, dst_at):
    x, y, c = _me()
    me = 4 * x + 2 * y + c
    copies = []
    for w in range(len(srcs)):
        for k in range(1, N_DEV):
            px, py, pc = _peer(k, x, y, c)
            copies.append(pltpu.make_async_remote_copy(
                src_ref=src_at(w, srcs[w], 4 * px + 2 * py + pc), dst_ref=dst_at(w, lands[w], me, k),
                send_sem=send_sems.at[w * (N_DEV - 1) + k - 1], recv_sem=recv_sems.at[w * (N_DEV - 1) + k - 1],
                device_id=(px, py, pc), device_id_type=MESH))
    return copies


def _own_copies(srcs, lands, own_sems, src_at, dst_at):
    x, y, c = _me()
    me = 4 * x + 2 * y + c
    return [pltpu.make_async_copy(src_at(w, srcs[w], me), dst_at(w, lands[w], me, 0), own_sems.at[w])
            for w in range(len(srcs))]


def _exchange_start(name, srcs, lands, src_at, dst_at, after, own=False):
    nw = len(srcs)

    def body(*refs):
        src_refs, land_refs = refs[:nw], refs[nw:2 * nw]
        send_sems, recv_sems, own_sems = refs[2 * nw + 1], refs[2 * nw + 2], refs[2 * nw + 3]
        token = refs[-1]
        for cp in _exchange_copies(src_refs, land_refs, send_sems, recv_sems, src_at, dst_at):
            cp.start()
        if own:
            for cp in _own_copies(src_refs, land_refs, own_sems, src_at, dst_at):
                cp.start()
        token[...] = jnp.zeros_like(token)

    hbm = lambda a: pltpu.HBM(a.shape, a.dtype)
    outs = pl.pallas_call(
        body, name=name,
        out_shape=(pltpu.SemaphoreType.DMA((nw * (N_DEV - 1),)), pltpu.SemaphoreType.DMA((nw * (N_DEV - 1),)),
                   pltpu.SemaphoreType.DMA((nw,)), *[hbm(a) for a in srcs], *[hbm(a) for a in lands],
                   jax.ShapeDtypeStruct((8, 128), F32)),
        in_specs=[HBM_SPEC] * (2 * nw) + [pl.BlockSpec(memory_space=pl.ANY)],
        out_specs=(SEM_SPEC, SEM_SPEC, SEM_SPEC, *[HBM_SPEC] * (2 * nw), pl.BlockSpec(memory_space=pltpu.VMEM)),
        input_output_aliases={i: 3 + i for i in range(2 * nw)},
        compiler_params=pltpu.CompilerParams(has_side_effects=DATAFLOW),
    )(*[pltpu.with_memory_space_constraint(a, pltpu.HBM) for a in list(srcs) + list(lands)], after)
    return dict(send=outs[0], recv=outs[1], own_sems=outs[2], srcs=outs[3:3 + nw], lands=outs[3 + nw:3 + 2 * nw],
                token=outs[-1], src_at=src_at, dst_at=dst_at, name=name, own=own)


def _exchange_wait(pending, after):
    nw = len(pending["srcs"])

    def body(*refs):
        src_refs, land_refs = refs[:nw], refs[nw:2 * nw]
        send_sems, recv_sems, own_sems = refs[2 * nw], refs[2 * nw + 1], refs[2 * nw + 2]
        for cp in _exchange_copies(src_refs, land_refs, send_sems, recv_sems,
                                   pending["src_at"], pending["dst_at"]):
            cp.wait_send()
            cp.wait_recv()
        if pending["own"]:
            for cp in _own_copies(src_refs, land_refs, own_sems, pending["src_at"], pending["dst_at"]):
                cp.wait()

    hbm = lambda a: pltpu.HBM(a.shape, a.dtype)
    outs = pl.pallas_call(
        body, name=pending["name"] + "_wait",
        out_shape=(*[hbm(a) for a in pending["srcs"]], *[hbm(a) for a in pending["lands"]]),
        in_specs=[HBM_SPEC] * (2 * nw) + [SEM_SPEC, SEM_SPEC, SEM_SPEC, pl.BlockSpec(memory_space=pl.ANY)],
        out_specs=tuple([HBM_SPEC] * (2 * nw)),
        input_output_aliases={i: i for i in range(2 * nw)},
        compiler_params=pltpu.CompilerParams(has_side_effects=DATAFLOW),
    )(*pending["srcs"], *pending["lands"], pending["send"], pending["recv"], pending["own_sems"], after)
    return outs[nw:]


def _w_in_scatter_copies(src, land, send_sems, recv_sems, early):
    rows, cols = land.shape[1], land.shape[2]
    bound = EARLY_SEC * rows
    cut_dev = bound // cols
    cut = bound - cut_dev * cols
    x, y, c = _me()
    me = 4 * x + 2 * y + c

    def pieces(t):
        if early:
            return [(t > cut_dev, t * cols - bound, cols, 0), (t == cut_dev, 0, cols - cut, cut)]
        return [(t < cut_dev, t * cols, cols, 0), (t == cut_dev, cut_dev * cols, cut, 0)]

    out = []
    for k in range(1, N_DEV):
        px, py, pc = _peer(k, x, y, c)
        for (to_peer, s0, width, d0), (to_me, _, _, _) in zip(pieces(4 * px + 2 * py + pc), pieces(me)):
            s0 = s0 if isinstance(s0, int) else pl.multiple_of(jnp.maximum(s0, 0), 128)
            out.append((to_peer, to_me, pltpu.make_async_remote_copy(
                src_ref=src.at[:, pl.ds(s0, width)], dst_ref=land.at[k - 1, :, pl.ds(d0, width)],
                send_sem=send_sems.at[k - 1], recv_sem=recv_sems.at[k - 1],
                device_id=(px, py, pc), device_id_type=MESH)))
    return out


def _w_in_scatter_start(name, src, land, early, after):
    def body(src_ref, land_ref, after_ref, send_sems, recv_sems, src_thru, land_thru, token):
        for to_peer, _, cp in _w_in_scatter_copies(src_ref, land_ref, send_sems, recv_sems, early):
            pl.when(to_peer)(cp.start)
        token[...] = jnp.zeros_like(token)

    hbm = lambda a: pltpu.HBM(a.shape, a.dtype)
    outs = pl.pallas_call(
        body, name=name,
        out_shape=(pltpu.SemaphoreType.DMA((N_DEV - 1,)), pltpu.SemaphoreType.DMA((N_DEV - 1,)),
                   hbm(src), hbm(land), jax.ShapeDtypeStruct((8, 128), F32)),
        in_specs=[HBM_SPEC, HBM_SPEC, pl.BlockSpec(memory_space=pl.ANY)],
        out_specs=(SEM_SPEC, SEM_SPEC, HBM_SPEC, HBM_SPEC, pl.BlockSpec(memory_space=pltpu.VMEM)),
        input_output_aliases={0: 2, 1: 3},
        compiler_params=pltpu.CompilerParams(has_side_effects=DATAFLOW),
    )(pltpu.with_memory_space_constraint(src, pltpu.HBM), pltpu.with_memory_space_constraint(land, pltpu.HBM), after)
    return dict(send=outs[0], recv=outs[1], src=outs[2], land=outs[3], token=outs[4], early=early, name=name)


def _w_in_scatter_wait(pending, land, after):
    def body(src_ref, land_ref, send_sems, recv_sems, after_ref, src_dead, land_out):
        for to_peer, to_me, cp in _w_in_scatter_copies(src_ref, land_ref, send_sems, recv_sems, pending["early"]):
            pl.when(to_peer)(cp.wait_send)
            pl.when(to_me)(cp.wait_recv)

    hbm = lambda a: pltpu.HBM(a.shape, a.dtype)
    outs = pl.pallas_call(
        body, name=pending["name"] + "_wait", out_shape=(hbm(pending["src"]), hbm(land)),
        in_specs=[HBM_SPEC, HBM_SPEC, SEM_SPEC, SEM_SPEC, pl.BlockSpec(memory_space=pl.ANY)],
        out_specs=(HBM_SPEC, HBM_SPEC), input_output_aliases={0: 0, 1: 1},
        compiler_params=pltpu.CompilerParams(has_side_effects=DATAFLOW),
    )(pending["src"], land, pending["send"], pending["recv"], after)
    return outs[1]


def _call_after(dep, body, args, *, in_specs, **kw):
    n_in = len(args)

    def wrapped(*refs):
        body(*refs[:n_in], *refs[n_in + 1:])

    dep_spec = pl.BlockSpec(dep.shape, lambda *_: (0,) * dep.ndim)
    return pl.pallas_call(wrapped, in_specs=list(in_specs) + [dep_spec], **kw)(*args, dep)


def _resident(shape):
    return pl.BlockSpec(shape, lambda *_: (0,) * len(shape), pipeline_mode=pl.Buffered(1))


def _proj(x2, w_in, dep):
    T, D = x2.shape
    tm = min(256, T)

    def body(x_ref, w_ref, o_ref, xt_ref):
        x = x_ref[...]
        xt_ref[...] = x.T.astype(BF16)
        xb = x.astype(BF16)
        for j in range(N_SEC):
            o_ref[j] = jnp.dot(xb, w_ref[:, j * D:(j + 1) * D], preferred_element_type=F32)

    return _call_after(
        dep, body, (x2, w_in), name="proj", grid=(T // tm,),
        in_specs=[pl.BlockSpec((tm, D), lambda i: (i, 0)), _resident((D, N_SEC * D))],
        out_specs=[pl.BlockSpec((N_SEC, tm, D), lambda i: (0, i, 0)), pl.BlockSpec((D, tm), lambda i: (0, i))],
        out_shape=[jax.ShapeDtypeStruct((N_SEC, T, D), F32), jax.ShapeDtypeStruct((D, T), BF16)],
        compiler_params=_params(("parallel",)))


def _chunk_cumsum(v, reverse=False):
    rows, lanes = v.shape
    x = v.reshape(rows // SUBLANES, SUBLANES, lanes)
    pos = lax.broadcasted_iota(jnp.int32, x.shape, 1)
    for sh in (1, 2, 4):
        if reverse:
            x = x + jnp.where(pos < SUBLANES - sh, pltpu.roll(x, SUBLANES - sh, 1), 0.0)
        else:
            x = x + jnp.where(pos >= sh, pltpu.roll(x, sh, 1), 0.0)
    x = x.reshape(rows // CHUNK, CHUNK // SUBLANES, SUBLANES, lanes)
    half = lax.broadcasted_iota(jnp.int32, x.shape, 1)
    if reverse:
        x = x + jnp.where(half == 0, x[:, 1:2, 0:1, :], 0.0)
    else:
        x = x + jnp.where(half == 1, x[:, 0:1, SUBLANES - 1:SUBLANES, :], 0.0)
    return x.reshape(rows, lanes)


def _hgrn_gates(q, f_pre, lb_logits):
    l0, l1 = lb_logits[0:1, :], lb_logits[1:2, :]
    mx = jnp.maximum(l0, l1)
    e0, e1 = jnp.exp(l0 - mx), jnp.exp(l1 - mx)
    lb = e0 / (e0 + e1)
    sq = _sigmoid(q)
    qf = q * sq * Q_SCALE
    sg = _sigmoid(f_pre)
    f = lb + (1.0 - lb) * sg
    k = 1.0 - f
    log_f = jnp.log(f)
    G = _chunk_cumsum(log_f)
    g_to_end = _chunk_cumsum(log_f, reverse=True) - log_f
    e_g = jnp.exp(G)
    e_ng = jnp.exp(-G)
    e_ge = jnp.exp(g_to_end)
    return dict(lb=lb, sq=sq, qf=qf, sg=sg, f=f, k=k, G=G, e_g=e_g, e_ng=e_ng, e_ge=e_ge,
                qd=qf * e_g, ki=k * e_ng, ke=k * e_ge, dec=jnp.exp(G + g_to_end))


def _intra_mask():
    r = lax.broadcasted_iota(jnp.int32, (GROUP, GROUP), 0)
    c = lax.broadcasted_iota(jnp.int32, (GROUP, GROUP), 1)
    return (r // CHUNK == c // CHUNK) & (c <= r)


def _chunk_outer(lhs_rows, rhs_b, out_scr, sb):
    lane = lax.broadcasted_iota(jnp.int32, (GROUP, GROUP), 1) // CHUNK
    for g in range(sb // GROUP):
        sl = slice(g * GROUP, (g + 1) * GROUP)
        lhs_t = lhs_rows[sl].T
        for cc in range(CH_PER_GROUP):
            masked = jnp.where(lane == cc, lhs_t, 0.0).astype(BF16)
            out_scr[g * CH_PER_GROUP + cc] = jnp.dot(masked, rhs_b[sl], preferred_element_type=F32)


def _hgrn_forward_blocks(cs, vs, st0s, sb, o_scr, kv_scr, st_scr, dec_scr):
    nc = sb // CHUNK
    n_str = len(cs)
    mask = _intra_mask()
    bf = []
    for i, (c, v) in enumerate(zip(cs, vs)):
        qd_b, ki_b, ke_b, v_b = (c["qd"].astype(BF16), c["ki"].astype(BF16), c["ke"].astype(BF16),
                                 v.astype(BF16))
        bf.append((qd_b, ki_b, ke_b, v_b))
        for g in range(sb // GROUP):
            sl = slice(g * GROUP, (g + 1) * GROUP)
            sc = lax.dot_general(qd_b[sl], ki_b[sl], NT_DIMS, preferred_element_type=F32)
            a = jnp.where(mask, sc, 0.0).astype(BF16)
            o_scr[i, sl, :] = jnp.dot(a, v_b[sl], preferred_element_type=F32)
        _chunk_outer(v, ke_b, kv_scr.at[i], sb)
        dec_scr[i] = c["dec"]

    def rec(n, sts):
        row = pl.ds(pl.multiple_of(n * CHUNK, CHUNK), 1)
        out = []
        for i in range(n_str):
            st_scr[i, n] = sts[i]
            out.append(sts[i] * dec_scr[i, row, :] + kv_scr[i, n])
        return tuple(out)

    ends = lax.fori_loop(0, nc, rec, tuple(st0s))

    for n in range(nc):
        rows = slice(n * CHUNK, (n + 1) * CHUNK)
        for i in range(n_str):
            o_scr[i, rows, :] += lax.dot_general(bf[i][0][rows], st_scr[i, n].astype(BF16), NT_DIMS,
                                                 preferred_element_type=F32)
    return ends, bf


def _hgrn_fwd(proj5, lb_logits, gn):
    _, Bl, S, D = proj5.shape
    H = D // HEAD
    sb = min(SUB_BLOCK, S)
    nsb = S // sb
    nc = sb // CHUNK

    def body(p_ref, lbl_ref, gn_ref, ain_ref, aint_ref, o_ref, st_ref, carry, o_scr, kv_scr, st_scr, dec_scr):
        @pl.when(pl.program_id(1) == 0)
        def _():
            carry[...] = jnp.zeros_like(carry)

        st0s = [carry[b] for b in range(Bl)]
        cs = [_hgrn_gates(p_ref[0, b], p_ref[1, b], lbl_ref[...]) for b in range(Bl)]
        ends, _ = _hgrn_forward_blocks(cs, [p_ref[2, b] for b in range(Bl)], st0s, sb,
                                       o_scr, kv_scr, st_scr, dec_scr)
        for b in range(Bl):
            carry[b] = ends[b]
            st_ref[b, 0] = st_scr[b].astype(BF16)
            o = o_scr[b]
            o_ref[b] = o
            rinv = lax.rsqrt(jnp.mean(o * o, axis=-1, keepdims=True) + RMS_EPS)
            ain = o * rinv * gn_ref[...] * _sigmoid(p_ref[3, b])
            ain_ref[b] = ain.astype(BF16)
            aint_ref[b] = ain.T.astype(BF16)

    return pl.pallas_call(
        body, name="hgrn_fwd", grid=(H, nsb),
        in_specs=[pl.BlockSpec((4, Bl, sb, HEAD), lambda h, s: (0, 0, s, h)),
                  pl.BlockSpec((2, HEAD), lambda h, s: (0, h)),
                  pl.BlockSpec((1, HEAD), lambda h, s: (0, h))],
        out_specs=[pl.BlockSpec((Bl, sb, HEAD), lambda h, s: (0, s, h)),
                   pl.BlockSpec((Bl, HEAD, sb), lambda h, s: (0, h, s)),
                   pl.BlockSpec((Bl, sb, HEAD), lambda h, s: (0, s, h)),
                   pl.BlockSpec((Bl, 1, nc, HEAD, HEAD), lambda h, s: (0, h, s, 0, 0))],
        out_shape=[jax.ShapeDtypeStruct((Bl, S, D), BF16), jax.ShapeDtypeStruct((Bl, D, S), BF16),
                   jax.ShapeDtypeStruct((Bl, S, D), F32),
                   jax.ShapeDtypeStruct((Bl, H, S // CHUNK, HEAD, HEAD), BF16)],
        scratch_shapes=[pltpu.VMEM((Bl, HEAD, HEAD), F32), pltpu.VMEM((Bl, sb, HEAD), F32),
                        pltpu.VMEM((Bl, nc, HEAD, HEAD), F32), pltpu.VMEM((Bl, nc, HEAD, HEAD), F32),
                        pltpu.VMEM((Bl, sb, HEAD), F32)],
        compiler_params=_params(("parallel", "arbitrary")),
    )(proj5, lb_logits, gn)


def _window_count(shape, g):
    pos = lax.broadcasted_iota(jnp.int32, shape, 0)
    return pos, jnp.minimum(pos + 1, jnp.left_shift(2, g)).astype(F32)


def _select_window(g, sums):
    return jnp.where(g == 0, sums[0], jnp.where(g == 1, sums[1], jnp.where(g == 2, sums[2], sums[3])))


def _pool_fwd(proj5, w_pool):
    _, Bl, S, D = proj5.shape
    pg = D // POOL_GROUPS

    def body(v_ref, w_ref, pooled_t_ref, bp_ref):
        g = pl.program_id(1)
        v = v_ref[0, 0]
        pos, cnt = _window_count(v.shape, g)
        cur, sums = v, []
        for sh in (1, 2, 4, 8):
            cur = cur + jnp.where(pos >= sh, pltpu.roll(cur, sh, 0), 0.0)
            sums.append(cur)
        pooled = _select_window(g, sums) / cnt - v
        pooled_t_ref[...] = pooled.T.astype(BF16)
        bp_ref[0] = jnp.dot(pooled.astype(BF16), w_ref[0], preferred_element_type=F32)

    return pl.pallas_call(
        body, name="pool_fwd", grid=(Bl, POOL_GROUPS),
        in_specs=[pl.BlockSpec((1, 1, S, pg), lambda b, g: (4, b, 0, g)),
                  pl.BlockSpec((1, pg, pg), lambda b, g: (g, 0, 0))],
        out_specs=[pl.BlockSpec((pg, S), lambda b, g: (g, b)),
                   pl.BlockSpec((1, S, pg), lambda b, g: (b, 0, g))],
        out_shape=[jax.ShapeDtypeStruct((D, Bl * S), BF16), jax.ShapeDtypeStruct((Bl, S, D), F32)],
        compiler_params=_params(("parallel", "parallel")),
    )(proj5, w_pool)


def _layer_norm_fwd(r):
    mu = jnp.mean(r, axis=-1, keepdims=True)
    d = r - mu
    rs = lax.rsqrt(jnp.mean(d * d, axis=-1, keepdims=True) + LN_EPS)
    return d * rs, rs


def _layer_norm_bwd(dy_g, xhat, rs):
    return rs * (dy_g - jnp.mean(dy_g, axis=-1, keepdims=True)
                 - xhat * jnp.mean(dy_g * xhat, axis=-1, keepdims=True))


def _mix_fwd(ain, proj, bp, x2, w_a, w_out, ps, g1, b1):
    T, D = x2.shape
    tm = min(256, T)

    def body(ain_ref, ga_ref, gb_ref, bp_ref, x_ref, wa_ref, wo_ref, ps_ref, g1_ref, b1_ref,
             a_ref, mgt_ref, xh_ref, rs_ref, x1b_ref, x1t_ref):
        a = jnp.dot(ain_ref[...], wa_ref[...], preferred_element_type=F32)
        a_ref[...] = a
        merged = _sigmoid(ga_ref[0]) * a + _sigmoid(gb_ref[0]) * (bp_ref[...] * ps_ref[...])
        mgt_ref[...] = merged.T.astype(BF16)
        r1 = ALPHA * x_ref[...] + jnp.dot(merged.astype(BF16), wo_ref[...], preferred_element_type=F32)
        xhat, rs = _layer_norm_fwd(r1)
        xh_ref[...] = xhat
        rs_ref[...] = rs
        x1 = xhat * g1_ref[...] + b1_ref[...]
        x1b_ref[...] = x1.astype(BF16)
        x1t_ref[...] = x1.T.astype(BF16)

    row = lambda i: (i, 0)
    col = lambda i: (0, i)
    full = lambda i: (0, 0)
    return pl.pallas_call(
        body, name="mix_fwd", grid=(T // tm,),
        in_specs=[pl.BlockSpec((tm, D), row),
                  pl.BlockSpec((1, tm, D), lambda i: (5, i, 0)),
                  pl.BlockSpec((1, tm, D), lambda i: (6, i, 0)),
                  pl.BlockSpec((tm, D), row), pl.BlockSpec((tm, D), row),
                  pl.BlockSpec((D, D), full), pl.BlockSpec((D, D), full),
                  pl.BlockSpec((1, D), full), pl.BlockSpec((1, D), full), pl.BlockSpec((1, D), full)],
        out_specs=[pl.BlockSpec((tm, D), row), pl.BlockSpec((D, tm), col), pl.BlockSpec((tm, D), row),
                   pl.BlockSpec((tm, 1), row), pl.BlockSpec((tm, D), row), pl.BlockSpec((D, tm), col)],
        out_shape=[jax.ShapeDtypeStruct((T, D), F32), jax.ShapeDtypeStruct((D, T), BF16),
                   jax.ShapeDtypeStruct((T, D), F32), jax.ShapeDtypeStruct((T, 1), F32),
                   jax.ShapeDtypeStruct((T, D), BF16), jax.ShapeDtypeStruct((D, T), BF16)],
        compiler_params=_params(("parallel",)),
    )(ain, proj, proj, bp, x2, w_a, w_out, ps, g1, b1)


def _mlp_fwd(x1b, w_up, w_down, xhat1, tgt, g1, b1, g2, b2):
    T, D = xhat1.shape
    FF = w_up.shape[1]
    tm = min(256, T)

    def body(x_ref, wu_ref, wd_ref, xh_ref, t_ref, g1_ref, b1_ref, g2_ref, b2_ref,
             hp_ref, h_ref, dr_ref, drb_ref, drt_ref, vec_ref):
        @pl.when(pl.program_id(0) == 0)
        def _():
            vec_ref[...] = jnp.zeros_like(vec_ref)

        xb = x_ref[...]
        x1 = xh_ref[...] * g1_ref[...] + b1_ref[...]
        r2 = ALPHA * x1
        for f in range(FF // D):
            cols = slice(f * D, (f + 1) * D)
            hp = jnp.dot(xb, wu_ref[:, cols], preferred_element_type=F32)
            hp_ref[:, cols] = hp
            h = jnp.square(jnp.maximum(hp, 0.0)).astype(BF16)
            h_ref[:, cols] = h
            r2 = r2 + jnp.dot(h, wd_ref[cols, :], preferred_element_type=F32)
        xhat2, rs2 = _layer_norm_fwd(r2)
        err = xhat2 * g2_ref[...] + b2_ref[...] - t_ref[...]
        dy = err / D
        vec_ref[5:6, :] += jnp.sum(dy * xhat2, axis=0, keepdims=True)
        vec_ref[6:7, :] += jnp.sum(dy, axis=0, keepdims=True)
        vec_ref[7:8, :] += jnp.sum(0.5 * err * err / D, axis=0, keepdims=True)
        dr = _layer_norm_bwd(dy * g2_ref[...], xhat2, rs2)
        dr_ref[...] = dr
        drb_ref[...] = dr.astype(BF16)
        drt_ref[...] = dr.T.astype(BF16)

    row = lambda i: (i, 0)
    full = lambda i: (0, 0)
    return pl.pallas_call(
        body, name="mlp_fwd", grid=(T // tm,),
        in_specs=[pl.BlockSpec((tm, D), row), _resident((D, FF)), _resident((FF, D)),
                  pl.BlockSpec((tm, D), row), pl.BlockSpec((tm, D), row),
                  pl.BlockSpec((1, D), full), pl.BlockSpec((1, D), full),
                  pl.BlockSpec((1, D), full), pl.BlockSpec((1, D), full)],
        out_specs=[pl.BlockSpec((tm, FF), row), pl.BlockSpec((tm, FF), row), pl.BlockSpec((tm, D), row),
                   pl.BlockSpec((tm, D), row), pl.BlockSpec((D, tm), lambda i: (0, i)),
                   pl.BlockSpec((8, D), full)],
        out_shape=[jax.ShapeDtypeStruct((T, FF), F32), jax.ShapeDtypeStruct((T, FF), BF16),
                   jax.ShapeDtypeStruct((T, D), F32), jax.ShapeDtypeStruct((T, D), BF16),
                   jax.ShapeDtypeStruct((D, T), BF16), jax.ShapeDtypeStruct((8, D), F32)],
        compiler_params=_params(("arbitrary",)),
    )(x1b, w_up, w_down, xhat1, tgt, g1, b1, g2, b2)


def _mlp_bwd(drb, dr, hp, w_up, w_down, xhat1, rs1, g1):
    T, D = dr.shape
    FF = hp.shape[1]
    tm = min(256, T)

    def body(drb_ref, dr_ref, hp_ref, wu_ref, wd_ref, xh_ref, rs_ref, g1_ref,
             dhp_ref, d1_ref, d1b_ref, vec_ref):
        @pl.when(pl.program_id(0) == 0)
        def _():
            vec_ref[...] = jnp.zeros_like(vec_ref)

        drb = drb_ref[...]
        dx1 = ALPHA * dr_ref[...]
        for f in range(FF // D):
            cols = slice(f * D, (f + 1) * D)
            dh = lax.dot_general(drb, wd_ref[cols, :], NT_DIMS, preferred_element_type=F32)
            dhp = (dh * (2.0 * jnp.maximum(hp_ref[:, cols], 0.0))).astype(BF16)
            dhp_ref[:, cols] = dhp
            dx1 = dx1 + lax.dot_general(dhp, wu_ref[:, cols], NT_DIMS, preferred_element_type=F32)
        xhat = xh_ref[...]
        vec_ref[3:4, :] += jnp.sum(dx1 * xhat, axis=0, keepdims=True)
        vec_ref[4:5, :] += jnp.sum(dx1, axis=0, keepdims=True)
        d1 = _layer_norm_bwd(dx1 * g1_ref[...], xhat, rs_ref[...])
        d1_ref[...] = d1
        d1b_ref[...] = d1.astype(BF16)

    row = lambda i: (i, 0)
    full = lambda i: (0, 0)
    return pl.pallas_call(
        body, name="mlp_bwd", grid=(T // tm,),
        in_specs=[pl.BlockSpec((tm, D), row), pl.BlockSpec((tm, D), row), pl.BlockSpec((tm, FF), row),
                  _resident((D, FF)), _resident((FF, D)),
                  pl.BlockSpec((tm, D), row), pl.BlockSpec((tm, 1), row), pl.BlockSpec((1, D), full)],
        out_specs=[pl.BlockSpec((tm, FF), row), pl.BlockSpec((tm, D), row), pl.BlockSpec((tm, D), row),
                   pl.BlockSpec((8, D), full)],
        out_shape=[jax.ShapeDtypeStruct((T, FF), BF16), jax.ShapeDtypeStruct((T, D), F32),
                   jax.ShapeDtypeStruct((T, D), BF16), jax.ShapeDtypeStruct((8, D), F32)],
        compiler_params=_params(("arbitrary",)),
    )(drb, dr, hp, w_up, w_down, xhat1, rs1, g1)


def _dw(name, a_t, b, n_j, a_spec, b_spec, o_shape, o_block, o_map, transpose_out=False, dep=None,
        into=(None, None), ob_shape=None, ob_map=None):
    def body(*refs):
        a_ref, b_ref, o_ref, ob_ref = refs[0], refs[1], refs[-2], refs[-1]
        b_val = b_ref[0] if len(b_ref.shape) == 3 else b_ref[...]
        if len(a_ref.shape) == 3:
            seq = a_ref.shape[2]
            p = sum(jnp.dot(a_ref[i], b_val[i * seq:(i + 1) * seq], preferred_element_type=F32)
                    for i in range(a_ref.shape[0]))
        else:
            p = jnp.dot(a_ref[...], b_val, preferred_element_type=F32)
        if transpose_out:
            p = p.T
        p = p.reshape(o_ref.shape)
        o_ref[...] = p
        ob_ref[...] = p.astype(BF16)

    kw = dict(name=name, grid=(n_j,), in_specs=[a_spec, b_spec],
              out_specs=[pl.BlockSpec(o_block, o_map), pl.BlockSpec(o_block, ob_map or o_map)],
              out_shape=[jax.ShapeDtypeStruct(o_shape, F32), jax.ShapeDtypeStruct(ob_shape or o_shape, BF16)],
              compiler_params=_params(("parallel",)))
    args = (a_t, b)
    aliases = {}
    for out_index, arr in enumerate(into):
        if arr is not None:
            aliases[len(args)] = out_index
            args = args + (arr,)
            kw["in_specs"] = kw["in_specs"] + [pl.BlockSpec(memory_space=pl.ANY)]
    if aliases:
        kw["input_output_aliases"] = aliases
    if dep is None:
        return pl.pallas_call(body, **kw)(*args)
    return _call_after(dep, body, args, **kw)


def _mix_bwd(d1b, proj, a, bp, w_a, w_out, w_pool, ps, dep):
    T, D = a.shape
    tm = min(256, T)
    pg = D // POOL_GROUPS

    def body(d1b_ref, ga_ref, gb_ref, a_ref, bp_ref, wa_ref, wo_ref, wp_ref, ps_ref,
             da_ref, dbp_ref, dain_ref, dpl_ref, dg_ref, vec_ref):
        @pl.when(pl.program_id(0) == 0)
        def _():
            vec_ref[...] = jnp.zeros_like(vec_ref)

        dm = lax.dot_general(d1b_ref[...], wo_ref[...], NT_DIMS, preferred_element_type=F32)
        sa, sg = _sigmoid(ga_ref[0]), _sigmoid(gb_ref[0])
        bp_v, ps_v = bp_ref[...], ps_ref[...]
        da = (dm * sa).astype(BF16)
        db = dm * sg
        dg_ref[0] = (dm * a_ref[...] * sa * (1.0 - sa)).astype(BF16)
        dg_ref[1] = (dm * (bp_v * ps_v) * sg * (1.0 - sg)).astype(BF16)
        vec_ref[2:3, :] += jnp.sum(db * bp_v, axis=0, keepdims=True)
        dbp = (db * ps_v).astype(BF16)
        da_ref[...] = da
        dbp_ref[...] = dbp
        dain_ref[...] = lax.dot_general(da, wa_ref[...], NT_DIMS, preferred_element_type=F32)
        for g in range(POOL_GROUPS):
            cols = slice(g * pg, (g + 1) * pg)
            dpl_ref[:, cols] = lax.dot_general(dbp[:, cols], wp_ref[g], NT_DIMS,
                                               preferred_element_type=F32)

    row = lambda i: (i, 0)
    full = lambda i: (0, 0)
    return _call_after(
        dep, body, (d1b, proj, proj, a, bp, w_a, w_out, w_pool, ps), name="mix_bwd", grid=(T // tm,),
        in_specs=[pl.BlockSpec((tm, D), row),
                  pl.BlockSpec((1, tm, D), lambda i: (5, i, 0)),
                  pl.BlockSpec((1, tm, D), lambda i: (6, i, 0)),
                  pl.BlockSpec((tm, D), row), pl.BlockSpec((tm, D), row),
                  pl.BlockSpec((D, D), full), pl.BlockSpec((D, D), full),
                  pl.BlockSpec((POOL_GROUPS, pg, pg), lambda i: (0, 0, 0)),
                  pl.BlockSpec((1, D), full)],
        out_specs=[pl.BlockSpec((tm, D), row), pl.BlockSpec((tm, D), row),
                   pl.BlockSpec((tm, D), row), pl.BlockSpec((tm, D), row),
                   pl.BlockSpec((2, tm, D), lambda i: (0, i, 0)),
                   pl.BlockSpec((8, D), full)],
        out_shape=[jax.ShapeDtypeStruct((T, D), BF16), jax.ShapeDtypeStruct((T, D), BF16),
                   jax.ShapeDtypeStruct((T, D), F32), jax.ShapeDtypeStruct((T, D), F32),
                   jax.ShapeDtypeStruct((2, T, D), BF16), jax.ShapeDtypeStruct((8, D), F32)],
        compiler_params=_params(("arbitrary",)))


def _pool_bwd(dpooled3, dep):
    Bl, S, D = dpooled3.shape
    pg = D // POOL_GROUPS

    def body(dp_ref, dv_ref):
        g = pl.program_id(1)
        dp = dp_ref[0]
        pos, cnt = _window_count(dp.shape, g)
        cur, sums = dp / cnt, []
        for sh in (1, 2, 4, 8):
            cur = cur + jnp.where(pos < S - sh, pltpu.roll(cur, S - sh, 0), 0.0)
            sums.append(cur)
        dv_ref[0] = (_select_window(g, sums) - dp).astype(BF16)

    spec = pl.BlockSpec((1, S, pg), lambda b, g: (b, 0, g))
    return _call_after(
        dep, body, (dpooled3,), name="pool_bwd", grid=(Bl, POOL_GROUPS), in_specs=[spec], out_specs=spec,
        out_shape=jax.ShapeDtypeStruct((Bl, S, D), BF16),
        compiler_params=_params(("parallel", "parallel")))


def _hgrn_bwd(proj5, lb_logits, gn, dain3, o3, st_all, dep):
    _, Bl, S, D = proj5.shape
    H = D // HEAD
    sb = min(SUB_BLOCK, S)
    nsb = S // sb
    nc = sb // CHUNK
    streams = range(Bl)

    def body(p_ref, lbl_ref, gn_ref, dain_ref, o_ref, st_ref, d_ref, vec_ref,
             dcarry, kv_scr, dst_scr, dec_scr, dvi_scr, dke_scr, dqi_scr):
        s = pl.program_id(1)

        @pl.when(s == 0)
        def _():
            dcarry[...] = jnp.zeros_like(dcarry)
            vec_ref[...] = jnp.zeros_like(vec_ref)

        qs, vs, ogs = [p_ref[0, b] for b in streams], [p_ref[2, b] for b in streams], [p_ref[3, b] for b in streams]
        cs = [_hgrn_gates(qs[b], p_ref[1, b], lbl_ref[...]) for b in streams]
        bf = [(cs[b]["qd"].astype(BF16), cs[b]["ki"].astype(BF16), cs[b]["ke"].astype(BF16),
               vs[b].astype(BF16)) for b in streams]
        mask = _intra_mask()
        gn_v = gn_ref[...]
        keep = []
        for b in streams:
            qd_b, ki_b, ke_b, v_b = bf[b]
            dec_scr[b] = cs[b]["dec"]
            o = o_ref[b]
            rinv = lax.rsqrt(jnp.mean(o * o, axis=-1, keepdims=True) + RMS_EPS)
            on = o * rinv
            so = _sigmoid(ogs[b])
            dain = dain_ref[b]
            vec_ref[1:2, :] += jnp.sum(dain * on * so, axis=0, keepdims=True)
            d_og = dain * on * gn_v * so * (1.0 - so)
            d_on = dain * gn_v * so
            do = rinv * (d_on - on * jnp.mean(d_on * on, axis=-1, keepdims=True))
            do_b = do.astype(BF16)
            dv_parts, dqd_parts, dki_parts = [], [], []
            for g in range(sb // GROUP):
                sl = slice(g * GROUP, (g + 1) * GROUP)
                sc = lax.dot_general(qd_b[sl], ki_b[sl], NT_DIMS, preferred_element_type=F32)
                a = jnp.where(mask, sc, 0.0).astype(BF16)
                da = lax.dot_general(do_b[sl], v_b[sl], NT_DIMS, preferred_element_type=F32)
                da = jnp.where(mask, da, 0.0).astype(BF16)
                dv_parts.append(lax.dot_general(a, do_b[sl], TN_DIMS, preferred_element_type=F32))
                dqd_parts.append(jnp.dot(da, ki_b[sl], preferred_element_type=F32))
                dki_parts.append(lax.dot_general(da, qd_b[sl], TN_DIMS, preferred_element_type=F32))
            keep.append(dict(d_og=d_og, do_b=do_b, dv_intra=jnp.concatenate(dv_parts, axis=0),
                             dqd_intra=jnp.concatenate(dqd_parts, axis=0),
                             dki=jnp.concatenate(dki_parts, axis=0)))
            _chunk_outer(do, qd_b, kv_scr.at[b], sb)

        def rrec(i, dsts):
            n = nc - 1 - i
            row = pl.ds(pl.multiple_of(n * CHUNK, CHUNK), 1)
            out = []
            for b in streams:
                dst_scr[b, n] = dsts[b]
                out.append(dsts[b] * dec_scr[b, row, :] + kv_scr[b, n])
            return tuple(out)

        ends = lax.fori_loop(0, nc, rrec, tuple(dcarry[b] for b in streams))
        for b in streams:
            dcarry[b] = ends[b]
        for n in range(nc):
            rows = slice(n * CHUNK, (n + 1) * CHUNK)
            for b in streams:
                qd_b, ki_b, ke_b, v_b = bf[b]
                dst_b = dst_scr[b, n].astype(BF16)
                dvi_scr[b, rows, :] = lax.dot_general(ke_b[rows], dst_b, NT_DIMS, preferred_element_type=F32)
                dke_scr[b, rows, :] = jnp.dot(v_b[rows], dst_b, preferred_element_type=F32)
                dqi_scr[b, rows, :] = jnp.dot(keep[b]["do_b"][rows], st_ref[b, 0, n],
                                              preferred_element_type=F32)
        for b in streams:
            c, k = cs[b], keep[b]
            ddec = jnp.sum(dst_scr[b] * st_ref[b, 0].astype(F32), axis=1)
            dgl = jnp.broadcast_to(ddec[:, None, :], (nc, CHUNK, HEAD)).reshape(sb, HEAD) * c["dec"]
            dqd = k["dqd_intra"] + dqi_scr[b]
            dke = dke_scr[b]
            dki = k["dki"]
            t_ke = dke * c["ke"]
            dG = dqd * c["qd"] - dki * c["ki"] - t_ke
            dgl = dgl + _chunk_cumsum(t_ke) + _chunk_cumsum(t_ke, reverse=True) - t_ke
            dlogf = _chunk_cumsum(dG, reverse=True) + dgl
            dk = dki * c["e_ng"] + dke * c["e_ge"]
            df = dlogf / c["f"] - dk
            sg, sq, lb, q = c["sg"], c["sq"], c["lb"], qs[b]
            vec_ref[0:1, :] += jnp.sum(df * (1.0 - sg), axis=0, keepdims=True)
            d_ref[0, b] = (dqd * c["e_g"] * Q_SCALE * (sq + q * sq * (1.0 - sq))).astype(BF16)
            d_ref[1, b] = (df * (1.0 - lb) * sg * (1.0 - sg)).astype(BF16)
            d_ref[2, b] = (k["dv_intra"] + dvi_scr[b]).astype(BF16)
            d_ref[3, b] = k["d_og"].astype(BF16)

    rev = lambda s: nsb - 1 - s
    big = pltpu.VMEM((Bl, nc, HEAD, HEAD), F32)
    rows_f32 = pltpu.VMEM((Bl, sb, HEAD), F32)
    return _call_after(
        dep, body, (proj5, lb_logits, gn, dain3, o3, st_all), name="hgrn_bwd", grid=(H, nsb),
        in_specs=[pl.BlockSpec((4, Bl, sb, HEAD), lambda h, s: (0, 0, rev(s), h)),
                  pl.BlockSpec((2, HEAD), lambda h, s: (0, h)),
                  pl.BlockSpec((1, HEAD), lambda h, s: (0, h)),
                  pl.BlockSpec((Bl, sb, HEAD), lambda h, s: (0, rev(s), h)),
                  pl.BlockSpec((Bl, sb, HEAD), lambda h, s: (0, rev(s), h)),
                  pl.BlockSpec((Bl, 1, nc, HEAD, HEAD), lambda h, s: (0, h, rev(s), 0, 0))],
        out_specs=[pl.BlockSpec((4, Bl, sb, HEAD), lambda h, s: (0, 0, rev(s), h)),
                   pl.BlockSpec((8, HEAD), lambda h, s: (0, h))],
        out_shape=[jax.ShapeDtypeStruct((4, Bl, S, D), BF16), jax.ShapeDtypeStruct((8, D), F32)],
        scratch_shapes=[pltpu.VMEM((Bl, HEAD, HEAD), F32), big, big, rows_f32, rows_f32, rows_f32, rows_f32],
        compiler_params=_params(("parallel", "arbitrary")))


def _dx(d1, dh4, dpv, dg2, w_in, dep):
    T, D = d1.shape
    tm = min(256, T)

    def body(d1_ref, dh_ref, dp_ref, dg_ref, w_ref, o_ref):
        blocks = [dh_ref[0], dh_ref[1], dh_ref[2], dh_ref[3], dp_ref[...], dg_ref[0], dg_ref[1]]
        acc = ALPHA * d1_ref[...]
        for j, blk in enumerate(blocks):
            acc = acc + lax.dot_general(blk, w_ref[:, j * D:(j + 1) * D], NT_DIMS, preferred_element_type=F32)
        o_ref[...] = acc

    row = lambda i: (i, 0)
    return _call_after(
        dep, body, (d1, dh4, dpv, dg2, w_in), name="dx", grid=(T // tm,),
        in_specs=[pl.BlockSpec((tm, D), row), pl.BlockSpec((4, tm, D), lambda i: (0, i, 0)),
                  pl.BlockSpec((tm, D), row), pl.BlockSpec((2, tm, D), lambda i: (0, i, 0)),
                  _resident((D, N_SEC * D))],
        out_specs=pl.BlockSpec((tm, D), row),
        out_shape=jax.ShapeDtypeStruct((T, D), F32),
        compiler_params=_params(("parallel",)))


EARLY_SEC = 4
DW_COLS = 512


def _dw_in_part(name, x_t, b, sections, first_sec, into, dep, ob_shape, ob_first):
    D, T = x_t.shape
    per = D // DW_COLS
    b_spec = (pl.BlockSpec((1, T, DW_COLS), lambda j: (j // per, 0, j % per)) if b.ndim == 3
              else pl.BlockSpec((T, DW_COLS), lambda j: (0, j)))
    return _dw(name, x_t, b, sections * per, _resident((D, T)), b_spec, (D, N_SEC * D), (D, DW_COLS),
               lambda j: (0, first_sec * per + j), dep=dep, into=into, ob_shape=ob_shape,
               ob_map=lambda j: (0, ob_first * per + j))


def _dw_in_early(x_t, dpv, dg2, dep):
    D = x_t.shape[0]
    early_shape = (D, (N_SEC - EARLY_SEC) * D)
    f32, bf = _dw_in_part("dw_in_gates", x_t, dg2, 2, 5, (None, None), dep, early_shape, 1)
    return _dw_in_part("dw_in_pool", x_t, dpv, 1, 4, (f32, bf), None, early_shape, 0)


def _dw_in_late(x_t, dh4, f32_early):
    D = x_t.shape[0]
    return _dw_in_part("dw_in_rec", x_t, dh4, EARLY_SEC, 0, (f32_early, None), None, (D, EARLY_SEC * D), 0)


def _adam_shard(name, me_arr, grad, land, layout, w, m, v):
    shape = layout.shape
    n_split = 4
    blk = (shape[0] // n_split,) + shape[1:]
    zeros = (0,) * (len(shape) - 1)

    def body(me_ref, g_ref, r_ref, w_ref, m_ref, v_ref, g_out, d_out, m_out, v_out):
        g = g_ref[...]
        for k in range(N_DEV - 1):
            g = g + r_ref[k].astype(F32)
        d, m2, v2 = _adamw(w_ref[...], g, m_ref[...], v_ref[...])
        g_out[...] = g
        d_out[...] = d
        m_out[...] = m2
        v_out[...] = v2

    def own(i, me_ref):
        bi = layout.block_index(me_ref[0])
        return (bi[0] * n_split + i,) + tuple(bi[1:]) if layout.kind == "row" else (i,) + tuple(bi[1:])

    plain = pl.BlockSpec(blk, lambda i, me_ref: (i,) + zeros)
    grid_spec = pltpu.PrefetchScalarGridSpec(
        num_scalar_prefetch=1, grid=(n_split,),
        in_specs=[pl.BlockSpec(blk, own),
                  pl.BlockSpec((N_DEV - 1,) + blk, lambda i, me_ref: (0, i) + zeros),
                  plain, plain, plain],
        out_specs=[plain] * 4)
    return pl.pallas_call(
        body, name=name, grid_spec=grid_spec,
        out_shape=[jax.ShapeDtypeStruct(shape, F32)] * 4,
        compiler_params=_params(("parallel",)),
    )(me_arr, grad, land, w, m, v)


def _vec_allreduce(vec):
    D = vec.shape[1]

    def body(vec_ref, tot_ref, gat, send_sems, recv_sems):
        x, y, c = _me()
        me = 4 * x + 2 * y + c
        gat[me] = vec_ref[...]
        copies = []
        for k in range(1, N_DEV):
            cp = pltpu.make_async_remote_copy(
                src_ref=vec_ref, dst_ref=gat.at[me], send_sem=send_sems.at[k - 1],
                recv_sem=recv_sems.at[k - 1], device_id=_peer(k, x, y, c), device_id_type=MESH)
            cp.start()
            copies.append(cp)
        for cp in copies:
            cp.wait()
        tot = gat[0]
        for d in range(1, N_DEV):
            tot = tot + gat[d]
        tot_ref[...] = tot

    vm = pl.BlockSpec(memory_space=pltpu.VMEM)
    return pl.pallas_call(
        body, name="vec_allreduce", out_shape=jax.ShapeDtypeStruct(vec.shape, F32),
        in_specs=[vm], out_specs=vm,
        scratch_shapes=[pltpu.VMEM((N_DEV, 8, D), F32), pltpu.SemaphoreType.DMA((N_DEV - 1,)),
                        pltpu.SemaphoreType.DMA((N_DEV - 1,))],
    )(vec)


def _vec_adam(tot, small_w, small_m, small_v):
    n = len(small_w)

    def body(*refs):
        tot = refs[0][...]
        ws, ms, vs = refs[1:1 + n], refs[1 + n:1 + 2 * n], refs[1 + 2 * n:1 + 3 * n]
        outs = refs[1 + 3 * n:]
        loss_ref, g_out, d_out = outs[0], outs[1:1 + n], outs[1 + n:1 + 2 * n]
        m_out, v_out = outs[1 + 2 * n:1 + 3 * n], outs[1 + 3 * n:1 + 4 * n]
        loss_ref[...] = jnp.broadcast_to(jnp.sum(tot[7:8, :], axis=1, keepdims=True), loss_ref.shape)
        lbl = ws[0][...]
        mx = jnp.maximum(lbl[0:1, :], lbl[1:2, :])
        e0, e1 = jnp.exp(lbl[0:1, :] - mx), jnp.exp(lbl[1:2, :] - mx)
        p0 = e0 / (e0 + e1)
        dl0 = tot[0:1, :] * p0 * (1.0 - p0)
        grads = [jnp.concatenate([dl0, -dl0], axis=0)] + [tot[r:r + 1, :] for r in range(1, n)]
        for i in range(n):
            d, m2, v2 = _adamw(ws[i][...], grads[i], ms[i][...], vs[i][...])
            g_out[i][...] = grads[i]
            d_out[i][...] = d
            m_out[i][...] = m2
            v_out[i][...] = v2

    vm = pl.BlockSpec(memory_space=pltpu.VMEM)
    shapes = [jax.ShapeDtypeStruct(w.shape, F32) for w in small_w]
    return pl.pallas_call(
        body, name="vec_adam",
        out_shape=[jax.ShapeDtypeStruct((1, 128), F32)] + shapes * 4,
        in_specs=[vm] * (1 + 3 * n), out_specs=[vm] * (1 + 4 * n),
    )(tot, *small_w, *small_m, *small_v)


def kernel(x, w_in, lb_logits, hgrn_norm_g, w_a, w_pool, pool_scale, w_out, ln1_g, ln1_b, w_up, w_down, ln2_g, ln2_b, loss_target, m_w_in, m_lb_logits, m_hgrn_norm_g, m_w_a, m_w_pool, m_pool_scale, m_w_out, m_ln1_g, m_ln1_b, m_w_up, m_w_down, m_ln2_g, m_ln2_b, v_w_in, v_lb_logits, v_hgrn_norm_g, v_w_a, v_w_pool, v_pool_scale, v_w_out, v_ln1_g, v_ln1_b, v_w_up, v_w_down, v_ln2_g, v_ln2_b):
    Bl, S, D = x.shape
    T = Bl * S
    pg = D // POOL_GROUPS
    x2 = x.reshape(T, D)
    tgt = loss_target.reshape(T, D)
    me = 4 * lax.axis_index("x") + 2 * lax.axis_index("y") + lax.axis_index("c")
    me_arr = jnp.reshape(me, (1,)).astype(jnp.int32)

    names = ["w_in", "w_a", "w_pool", "w_out", "w_up", "w_down"]
    big_w = dict(zip(names, [w_in[0], w_a[0], w_pool[0], w_out[0], w_up[0], w_down[0]]))
    big_m = dict(zip(names, [m_w_in[0], m_w_a[0], m_w_pool[0], m_w_out[0], m_w_up[0], m_w_down[0]]))
    big_v = dict(zip(names, [v_w_in[0], v_w_a[0], v_w_pool[0], v_w_out[0], v_w_up[0], v_w_down[0]]))
    kinds = dict(w_in="col", w_a="row", w_pool="pool", w_out="row", w_up="col", w_down="row")
    lay = {nm: _Sharded(kinds[nm], big_w[nm].shape) for nm in names}
    wb = {nm: big_w[nm].astype(BF16) for nm in names}

    (w_in_f,) = _all_gather("ag_w_in", [wb["w_in"]], [lay["w_in"]])
    def gather_start(name, nms, after):
        return _exchange_start(name, [wb[nm] for nm in nms], [lax.empty(lay[nm].full_shape, BF16) for nm in nms],
                               src_at=lambda w, ref, peer: ref,
                               dst_at=lambda w, ref, mine, k: lay[nms[w]].at(ref, mine), after=after, own=True)

    ag_mix = gather_start("ag_mix", ["w_a", "w_pool", "w_out"], w_in_f)
    ag_mlp = gather_start("ag_mlp", ["w_up", "w_down"], ag_mix["token"])

    proj, x_t = _proj(x2, w_in_f, ag_mlp["token"])
    proj5 = proj.reshape(N_SEC, Bl, S, D)
    ain3, ain_t, o3, st_all = _hgrn_fwd(proj5, lb_logits, hgrn_norm_g)
    w_a_f, w_pool_f, w_out_f = _exchange_wait(ag_mix, ain3)
    pooled_t, bp3 = _pool_fwd(proj5, w_pool_f)
    ain, bp = ain3.reshape(T, D), bp3.reshape(T, D)
    a, merged_t, xhat1, rs1, x1b, x1_t = _mix_fwd(ain, proj, bp, x2, w_a_f, w_out_f, pool_scale, ln1_g, ln1_b)
    w_up_f, w_down_f = _exchange_wait(ag_mlp, x1b)
    hp, h, dr2, dr2b, dr2_t, vec_mlp = _mlp_fwd(x1b, w_up_f, w_down_f, xhat1, tgt, ln1_g, ln1_b, ln2_g, ln2_b)

    def scatter_start(name, nms, grads_b, after):
        lands = [lax.empty((N_DEV - 1,) + lay[nm].shape, BF16) for nm in nms]
        return _exchange_start(name, grads_b, lands,
                               src_at=lambda w, ref, peer: lay[nms[w]].at(ref, peer),
                               dst_at=lambda w, ref, mine, k: ref.at[k - 1], after=after)

    dhp, dr1, dr1b, vec_ln1 = _mlp_bwd(dr2b, dr2, hp, w_up_f, w_down_f, xhat1, rs1, ln1_g)
    FF = 4 * D
    whole_t = _resident((D, T))
    cols_b = pl.BlockSpec((T, DW_COLS), lambda j: (0, j))
    cols_o = ((D, DW_COLS), lambda j: (0, j))
    gw, gwb = {}, {}
    gw["w_down"], gwb["w_down"] = _dw(
        "dw_down", dr2_t, h, FF // DW_COLS, whole_t, cols_b, (FF, D), (DW_COLS, D), lambda j: (j, 0),
        transpose_out=True)
    rs_down = scatter_start("rs_w_down", ["w_down"], [gwb["w_down"]], gw["w_down"])
    gw["w_up"], gwb["w_up"] = _dw("dw_up", x1_t, dhp, FF // DW_COLS, whole_t, cols_b, (D, FF), *cols_o,
                                  dep=rs_down["token"])
    rs_up = scatter_start("rs_w_up", ["w_up"], [gwb["w_up"]], gw["w_up"])
    da_b, dbp_b, dain, dpooled, dg2, vec_mix = _mix_bwd(dr1b, proj, a, bp, w_a_f, w_out_f, w_pool_f, pool_scale,
                                                        rs_up["token"])
    gw["w_out"], gwb["w_out"] = _dw("dw_out", merged_t, dr1b, D // DW_COLS, whole_t, cols_b, (D, D), *cols_o)
    gw["w_a"], gwb["w_a"] = _dw("dw_a", ain_t, da_b, D // DW_COLS, _resident((Bl, D, S)), cols_b, (D, D), *cols_o)
    gw["w_pool"], gwb["w_pool"] = _dw(
        "dw_pool", pooled_t, dbp_b, POOL_GROUPS, pl.BlockSpec((pg, T), lambda j: (j, 0)),
        pl.BlockSpec((T, pg), lambda j: (0, j)), (POOL_GROUPS, pg, pg), (1, pg, pg), lambda j: (j, 0, 0))
    mid = ["w_out", "w_a", "w_pool"]
    rs_mid = scatter_start("rs_w_mid", mid, [gwb[nm] for nm in mid], gw["w_pool"])
    dpv = _pool_bwd(dpooled.reshape(Bl, S, D), rs_mid["token"]).reshape(T, D)
    gw_in_early, gwb_in_early = _dw_in_early(x_t, dpv, dg2, rs_mid["token"])
    land_in = lax.empty((N_DEV - 1,) + lay["w_in"].shape, BF16)
    rs_in_early = _w_in_scatter_start("rs_w_in_early", gwb_in_early, land_in, True, gw_in_early)
    dh4, vec_hgrn = _hgrn_bwd(proj5, lb_logits, hgrn_norm_g, dain.reshape(Bl, S, D), o3, st_all,
                              rs_in_early["token"])
    dh4 = dh4.reshape(4, T, D)
    gw["w_in"], gwb_in_late = _dw_in_late(x_t, dh4, gw_in_early)
    rs_in_late = _w_in_scatter_start("rs_w_in_late", gwb_in_late, rs_in_early["land"], False, gw["w_in"])
    grad_x2 = _dx(dr1, dh4, dpv, dg2, w_in_f, rs_in_late["token"])
    grad_x = grad_x2.reshape(Bl, S, D)

    vec = vec_mlp + vec_ln1 + vec_mix + vec_hgrn
    small_names = ["lb_logits", "hgrn_norm_g", "pool_scale", "ln1_g", "ln1_b", "ln2_g", "ln2_b"]
    small_w = [lb_logits, hgrn_norm_g, pool_scale, ln1_g, ln1_b, ln2_g, ln2_b]
    small_m = [m_lb_logits, m_hgrn_norm_g, m_pool_scale, m_ln1_g, m_ln1_b, m_ln2_g, m_ln2_b]
    small_v = [v_lb_logits, v_hgrn_norm_g, v_pool_scale, v_ln1_g, v_ln1_b, v_ln2_g, v_ln2_b]
    res = _vec_adam(_vec_allreduce(vec), small_w, small_m, small_v)
    loss = res[0][0, 0]
    n = len(small_w)
    small = {nm: (res[1 + i], res[1 + n + i], res[1 + 2 * n + i], res[1 + 3 * n + i])
             for i, nm in enumerate(small_names)}

    big, last = {}, grad_x2

    def adam(nm, land):
        outs = _adam_shard("adam_" + nm, me_arr, gw[nm], land, lay[nm], big_w[nm], big_m[nm], big_v[nm])
        big[nm] = tuple(t[None] for t in outs)
        return outs[0]

    for pend, nms in ((rs_down, ["w_down"]), (rs_up, ["w_up"]), (rs_mid, mid)):
        for nm, land in zip(nms, _exchange_wait(pend, last)):
            last = adam(nm, land)
    land_in = _w_in_scatter_wait(rs_in_early, rs_in_late["land"], last)
    adam("w_in", _w_in_scatter_wait(rs_in_late, land_in, res[0]))

    order = ["w_in", "lb_logits", "hgrn_norm_g", "w_a", "w_pool", "pool_scale", "w_out", "ln1_g", "ln1_b",
             "w_up", "w_down", "ln2_g", "ln2_b"]
    allp = {**big, **small}
    out = [loss, grad_x]
    for part in range(4):
        out += [allp[nm][part] for nm in order]
    return tuple(out)
```

```python
import jax
import jax.numpy as jnp
from jax import lax
from jax.experimental import pallas as pl
from jax.experimental.pallas import tpu as pltpu

F32 = jnp.float32
BF16 = jnp.bfloat16
MESH = pl.DeviceIdType.MESH

N_DEV = 8
HEAD = 128
CHUNK = 16
SUBLANES = 8
GROUP = 128
SUB_BLOCK = 1024
CH_PER_GROUP = GROUP // CHUNK
N_SEC = 7
POOL_GROUPS = 4
ALPHA = (2.0 * 1) ** 0.25
LN_EPS = 1e-5
RMS_EPS = 1e-6
Q_SCALE = HEAD ** -0.5
ADAM_LR = 0.001
ADAM_B1 = 0.9
ADAM_B2 = 0.999
ADAM_EPS = 1e-08
ADAM_WD = 0.01
ADAM_STEP = 10
VMEM_LIMIT = 60 << 20

NT_DIMS = (((1,), (1,)), ((), ()))
TN_DIMS = (((0,), (0,)), ((), ()))


def _params(sem=None):
    kw = dict(vmem_limit_bytes=VMEM_LIMIT)
    if sem is not None:
        kw["dimension_semantics"] = sem
    return pltpu.CompilerParams(**kw)


def _me():
    return lax.axis_index("x"), lax.axis_index("y"), lax.axis_index("c")


def _sigmoid(v):
    return jax.nn.sigmoid(v)


def _adamw(w, g, m, v):
    m = ADAM_B1 * m + (1.0 - ADAM_B1) * g
    v = ADAM_B2 * v + (1.0 - ADAM_B2) * jnp.square(g)
    m_hat = m / (1.0 - ADAM_B1 ** ADAM_STEP)
    v_hat = v / (1.0 - ADAM_B2 ** ADAM_STEP)
    delta = -ADAM_LR * (m_hat / (jnp.sqrt(v_hat) + ADAM_EPS) + ADAM_WD * w)
    return delta, m, v


class _Sharded:
    def __init__(self, kind, shard_shape):
        self.kind, self.shape = kind, tuple(shard_shape)

    @property
    def full_shape(self):
        r = self.shape
        if self.kind == "row":
            return (N_DEV * r[0],) + r[1:]
        return (r[0], N_DEV * r[1]) + r[2:]

    def at(self, ref, d):
        if self.kind == "col":
            n = self.shape[1]
            return ref.at[:, pl.ds(pl.multiple_of(d * n, 128), n)]
        if self.kind == "row":
            n = self.shape[0]
            return ref.at[pl.ds(pl.multiple_of(d * n, 16), n), :]
        n = self.shape[1]
        return ref.at[:, pl.ds(pl.multiple_of(d * n, 16), n), :]

    def block_index(self, d):
        return {"col": (0, d), "row": (d, 0), "pool": (0, d, 0)}[self.kind]


def _peer(k, x, y, c):
    return (1 - x if k & 4 else x, 1 - y if k & 2 else y, 1 - c if k & 1 else c)


def _all_gather(name, shards, layouts):
    nw = len(shards)

    def body(*refs):
        ins, outs = refs[:nw], refs[nw:2 * nw]
        send_sems, recv_sems, local_sems = refs[2 * nw:]
        x, y, c = _me()
        me = (x, y, c)
        sibling = (x, y, 1 - c)
        chips = [(1 - x, y), (x, 1 - y), (1 - x, 1 - y)]

        def copy(w, k, block, to, src=None):
            px, py, pc = block
            dst = layouts[w].at(outs[w], 4 * px + 2 * py + pc)
            return pltpu.make_async_remote_copy(
                src_ref=dst if src is None else src, dst_ref=dst,
                send_sem=send_sems.at[w, k], recv_sem=recv_sems.at[w, k],
                device_id=to, device_id_type=MESH)

        def place(w):
            mine = pltpu.make_async_copy(ins[w], layouts[w].at(outs[w], 4 * x + 2 * y + c), local_sems.at[w])
            mine.start()
            return mine

        first = []
        for w in range(nw):
            first.append(copy(w, 0, me, sibling, src=ins[w]))
            first += [copy(w, 1 + j, me, (*chip, c), src=ins[w]) for j, chip in enumerate(chips)]
        for cp in first:
            cp.start()
        local = [place(w) for w in range(nw)]
        passed = []
        for w in range(nw):
            for j, chip in enumerate(chips):
                copy(w, 1 + j, (*chip, c), me).wait_recv()
                fwd = copy(w, 4 + j, (*chip, c), sibling)
                fwd.start()
                passed.append(fwd)
        for w in range(nw):
            copy(w, 0, sibling, me).wait_recv()
            for j, chip in enumerate(chips):
                copy(w, 4 + j, (*chip, 1 - c), me).wait_recv()
        for cp in first + passed:
            cp.wait_send()
        for cp in local:
            cp.wait()

    any_spec = pl.BlockSpec(memory_space=pl.ANY)
    return pl.pallas_call(
        body, name=name,
        out_shape=[jax.ShapeDtypeStruct(l.full_shape, s.dtype) for s, l in zip(shards, layouts)],
        in_specs=[any_spec] * nw, out_specs=[any_spec] * nw,
        scratch_shapes=[pltpu.SemaphoreType.DMA((nw, 7)), pltpu.SemaphoreType.DMA((nw, 7)),
                        pltpu.SemaphoreType.DMA((nw,))],
    )(*shards)


HBM_SPEC = pl.BlockSpec(memory_space=pltpu.HBM)
SEM_SPEC = pl.BlockSpec(memory_space=pltpu.SEMAPHORE)
DATAFLOW = pltpu.SideEffectType.DATAFLOW_SIDE_EFFECTING


def _exchange_copies(srcs, lands, send_sems, recv_sems, src_at, dst_at):
    x, y, c = _me()
    me = 4 * x + 2 * y + c
    copies = []
    for w in range(len(srcs)):
        for k in range(1, N_DEV):
            px, py, pc = _peer(k, x, y, c)
            copies.append(pltpu.make_async_remote_copy(
                src_ref=src_at(w, srcs[w], 4 * px + 2 * py + pc), dst_ref=dst_at(w, lands[w], me, k),
                send_sem=send_sems.at[w * (N_DEV - 1) + k - 1], recv_sem=recv_sems.at[w * (N_DEV - 1) + k - 1],
                device_id=(px, py, pc), device_id_type=MESH))
    return copies


def _own_copies(srcs, lands, own_sems, src_at, dst_at):
    x, y, c = _me()
    me = 4 * x + 2 * y + c
    return [pltpu.make_async_copy(src_at(w, srcs[w], me), dst_at(w, lands[w], me, 0), own_sems.at[w])
            for w in range(len(srcs))]


def _exchange_start(name, srcs, lands, src_at, dst_at, after, own=False):
    nw = len(srcs)

    def body(*refs):
        src_refs, land_refs = refs[:nw], refs[nw:2 * nw]
        send_sems, recv_sems, own_sems = refs[2 * nw + 1], refs[2 * nw + 2], refs[2 * nw + 3]
        token = refs[-1]
        for cp in _exchange_copies(src_refs, land_refs, send_sems, recv_sems, src_at, dst_at):
            cp.start()
        if own:
            for cp in _own_copies(src_refs, land_refs, own_sems, src_at, dst_at):
                cp.start()
        token[...] = jnp.zeros_like(token)

    hbm = lambda a: pltpu.HBM(a.shape, a.dtype)
    outs = pl.pallas_call(
        body, name=name,
        out_shape=(pltpu.SemaphoreType.DMA((nw * (N_DEV - 1),)), pltpu.SemaphoreType.DMA((nw * (N_DEV - 1),)),
                   pltpu.SemaphoreType.DMA((nw,)), *[hbm(a) for a in srcs], *[hbm(a) for a in lands],
                   jax.ShapeDtypeStruct((8, 128), F32)),
        in_specs=[HBM_SPEC] * (2 * nw) + [pl.BlockSpec(memory_space=pl.ANY)],
        out_specs=(SEM_SPEC, SEM_SPEC, SEM_SPEC, *[HBM_SPEC] * (2 * nw), pl.BlockSpec(memory_space=pltpu.VMEM)),
        input_output_aliases={i: 3 + i for i in range(2 * nw)},
        compiler_params=pltpu.CompilerParams(has_side_effects=DATAFLOW),
    )(*[pltpu.with_memory_space_constraint(a, pltpu.HBM) for a in list(srcs) + list(lands)], after)
    return dict(send=outs[0], recv=outs[1], own_sems=outs[2], srcs=outs[3:3 + nw], lands=outs[3 + nw:3 + 2 * nw],
                token=outs[-1], src_at=src_at, dst_at=dst_at, name=name, own=own)


def _exchange_wait(pending, after):
    nw = len(pending["srcs"])

    def body(*refs):
        src_refs, land_refs = refs[:nw], refs[nw:2 * nw]
        send_sems, recv_sems, own_sems = refs[2 * nw], refs[2 * nw + 1], refs[2 * nw + 2]
        for cp in _exchange_copies(src_refs, land_refs, send_sems, recv_sems,
                                   pending["src_at"], pending["dst_at"]):
            cp.wait_send()
            cp.wait_recv()
        if pending["own"]:
            for cp in _own_copies(src_refs, land_refs, own_sems, pending["src_at"], pending["dst_at"]):
                cp.wait()

    hbm = lambda a: pltpu.HBM(a.shape, a.dtype)
    outs = pl.pallas_call(
        body, name=pending["name"] + "_wait",
        out_shape=(*[hbm(a) for a in pending["srcs"]], *[hbm(a) for a in pending["lands"]]),
        in_specs=[HBM_SPEC] * (2 * nw) + [SEM_SPEC, SEM_SPEC, SEM_SPEC, pl.BlockSpec(memory_space=pl.ANY)],
        out_specs=tuple([HBM_SPEC] * (2 * nw)),
        input_output_aliases={i: i for i in range(2 * nw)},
        compiler_params=pltpu.CompilerParams(has_side_effects=DATAFLOW),
    )(*pending["srcs"], *pending["lands"], pending["send"], pending["recv"], pending["own_sems"], after)
    return outs[nw:]


def _w_in_scatter_copies(src, land, send_sems, recv_sems, early):
    rows, cols = land.shape[1], land.shape[2]
    bound = EARLY_SEC * rows
    cut_dev = bound // cols
    cut = bound - cut_dev * cols
    x, y, c = _me()
    me = 4 * x + 2 * y + c

    def pieces(t):
        if early:
            return [(t > cut_dev, t * cols - bound, cols, 0), (t == cut_dev, 0, cols - cut, cut)]
        return [(t < cut_dev, t * cols, cols, 0), (t == cut_dev, cut_dev * cols, cut, 0)]

    out = []
    for k in range(1, N_DEV):
        px, py, pc = _peer(k, x, y, c)
        for (to_peer, s0, width, d0), (to_me, _, _, _) in zip(pieces(4 * px + 2 * py + pc), pieces(me)):
            s0 = s0 if isinstance(s0, int) else pl.multiple_of(jnp.maximum(s0, 0), 128)
            out.append((to_peer, to_me, pltpu.make_async_remote_copy(
                src_ref=src.at[:, pl.ds(s0, width)], dst_ref=land.at[k - 1, :, pl.ds(d0, width)],
                send_sem=send_sems.at[k - 1], recv_sem=recv_sems.at[k - 1],
                device_id=(px, py, pc), device_id_type=MESH)))
    return out


def _w_in_scatter_start(name, src, land, early, after):
    def body(src_ref, land_ref, after_ref, send_sems, recv_sems, src_thru, land_thru, token):
        for to_peer, _, cp in _w_in_scatter_copies(src_ref, land_ref, send_sems, recv_sems, early):
            pl.when(to_peer)(cp.start)
        token[...] = jnp.zeros_like(token)

    hbm = lambda a: pltpu.HBM(a.shape, a.dtype)
    outs = pl.pallas_call(
        body, name=name,
        out_shape=(pltpu.SemaphoreType.DMA((N_DEV - 1,)), pltpu.SemaphoreType.DMA((N_DEV - 1,)),
                   hbm(src), hbm(land), jax.ShapeDtypeStruct((8, 128), F32)),
        in_specs=[HBM_SPEC, HBM_SPEC, pl.BlockSpec(memory_space=pl.ANY)],
        out_specs=(SEM_SPEC, SEM_SPEC, HBM_SPEC, HBM_SPEC, pl.BlockSpec(memory_space=pltpu.VMEM)),
        input_output_aliases={0: 2, 1: 3},
        compiler_params=pltpu.CompilerParams(has_side_effects=DATAFLOW),
    )(pltpu.with_memory_space_constraint(src, pltpu.HBM), pltpu.with_memory_space_constraint(land, pltpu.HBM), after)
    return dict(send=outs[0], recv=outs[1], src=outs[2], land=outs[3], token=outs[4], early=early, name=name)


def _w_in_scatter_wait(pending, land, after):
    def body(src_ref, land_ref, send_sems, recv_sems, after_ref, src_dead, land_out):
        for to_peer, to_me, cp in _w_in_scatter_copies(src_ref, land_ref, send_sems, recv_sems, pending["early"]):
            pl.when(to_peer)(cp.wait_send)
            pl.when(to_me)(cp.wait_recv)

    hbm = lambda a: pltpu.HBM(a.shape, a.dtype)
    outs = pl.pallas_call(
        body, name=pending["name"] + "_wait", out_shape=(hbm(pending["src"]), hbm(land)),
        in_specs=[HBM_SPEC, HBM_SPEC, SEM_SPEC, SEM_SPEC, pl.BlockSpec(memory_space=pl.ANY)],
        out_specs=(HBM_SPEC, HBM_SPEC), input_output_aliases={0: 0, 1: 1},
        compiler_params=pltpu.CompilerParams(has_side_effects=DATAFLOW),
    )(pending["src"], land, pending["send"], pending["recv"], after)
    return outs[1]


def _call_after(dep, body, args, *, in_specs, **kw):
    n_in = len(args)

    def wrapped(*refs):
        body(*refs[:n_in], *refs[n_in + 1:])

    dep_spec = pl.BlockSpec(dep.shape, lambda *_: (0,) * dep.ndim)
    return pl.pallas_call(wrapped, in_specs=list(in_specs) + [dep_spec], **kw)(*args, dep)


def _resident(shape):
    return pl.BlockSpec(shape, lambda *_: (0,) * len(shape), pipeline_mode=pl.Buffered(1))


def _proj(x2, w_in, dep):
    T, D = x2.shape
    tm = min(256, T)

    def body(x_ref, w_ref, o_ref, xt_ref):
        x = x_ref[...]
        xt_ref[...] = x.T.astype(BF16)
        xb = x.astype(BF16)
        for j in range(N_SEC):
            o_ref[j] = jnp.dot(xb, w_ref[:, j * D:(j + 1) * D], preferred_element_type=F32)

    return _call_after(
        dep, body, (x2, w_in), name="proj", grid=(T // tm,),
        in_specs=[pl.BlockSpec((tm, D), lambda i: (i, 0)), _resident((D, N_SEC * D))],
        out_specs=[pl.BlockSpec((N_SEC, tm, D), lambda i: (0, i, 0)), pl.BlockSpec((D, tm), lambda i: (0, i))],
        out_shape=[jax.ShapeDtypeStruct((N_SEC, T, D), F32), jax.ShapeDtypeStruct((D, T), BF16)],
        compiler_params=_params(("parallel",)))


def _chunk_cumsum(v, reverse=False):
    rows, lanes = v.shape
    x = v.reshape(rows // SUBLANES, SUBLANES, lanes)
    pos = lax.broadcasted_iota(jnp.int32, x.shape, 1)
    for sh in (1, 2, 4):
        if reverse:
            x = x + jnp.where(pos < SUBLANES - sh, pltpu.roll(x, SUBLANES - sh, 1), 0.0)
        else:
            x = x + jnp.where(pos >= sh, pltpu.roll(x, sh, 1), 0.0)
    x = x.reshape(rows // CHUNK, CHUNK // SUBLANES, SUBLANES, lanes)
    half = lax.broadcasted_iota(jnp.int32, x.shape, 1)
    if reverse:
        x = x + jnp.where(half == 0, x[:, 1:2, 0:1, :], 0.0)
    else:
        x = x + jnp.where(half == 1, x[:, 0:1, SUBLANES - 1:SUBLANES, :], 0.0)
    return x.reshape(rows, lanes)


def _hgrn_gates(q, f_pre, lb_logits):
    l0, l1 = lb_logits[0:1, :], lb_logits[1:2, :]
    mx = jnp.maximum(l0, l1)
    e0, e1 = jnp.exp(l0 - mx), jnp.exp(l1 - mx)
    lb = e0 / (e0 + e1)
    sq = _sigmoid(q)
    qf = q * sq * Q_SCALE
    sg = _sigmoid(f_pre)
    f = lb + (1.0 - lb) * sg
    k = 1.0 - f
    log_f = jnp.log(f)
    G = _chunk_cumsum(log_f)
    g_to_end = _chunk_cumsum(log_f, reverse=True) - log_f
    e_g = jnp.exp(G)
    e_ng = jnp.exp(-G)
    e_ge = jnp.exp(g_to_end)
    return dict(lb=lb, sq=sq, qf=qf, sg=sg, f=f, k=k, G=G, e_g=e_g, e_ng=e_ng, e_ge=e_ge,
                qd=qf * e_g, ki=k * e_ng, ke=k * e_ge, dec=e_g * e_ge)


def _intra_mask():
    r = lax.broadcasted_iota(jnp.int32, (GROUP, GROUP), 0)
    c = lax.broadcasted_iota(jnp.int32, (GROUP, GROUP), 1)
    return (r // CHUNK == c // CHUNK) & (c <= r)


def _chunk_outer(lhs_rows, rhs_b, out_scr, sb):
    lane = lax.broadcasted_iota(jnp.int32, (GROUP, GROUP), 1) // CHUNK
    for g in range(sb // GROUP):
        sl = slice(g * GROUP, (g + 1) * GROUP)
        lhs_t = lhs_rows[sl].T
        for cc in range(CH_PER_GROUP):
            masked = jnp.where(lane == cc, lhs_t, 0.0).astype(BF16)
            out_scr[g * CH_PER_GROUP + cc] = jnp.dot(masked, rhs_b[sl], preferred_element_type=F32)


def _hgrn_forward_blocks(cs, vs, st0s, sb, o_scr, kv_scr, st_scr, dec_scr):
    nc = sb // CHUNK
    n_str = len(cs)
    mask = _intra_mask()
    bf = []
    for i, (c, v) in enumerate(zip(cs, vs)):
        qd_b, ki_b, ke_b, v_b = (c["qd"].astype(BF16), c["ki"].astype(BF16), c["ke"].astype(BF16),
                                 v.astype(BF16))
        bf.append((qd_b, ki_b, ke_b, v_b))
        for g in range(sb // GROUP):
            sl = slice(g * GROUP, (g + 1) * GROUP)
            sc = lax.dot_general(qd_b[sl], ki_b[sl], NT_DIMS, preferred_element_type=F32)
            a = jnp.where(mask, sc, 0.0).astype(BF16)
            o_scr[i, sl, :] = jnp.dot(a, v_b[sl], preferred_element_type=F32)
        _chunk_outer(v, ke_b, kv_scr.at[i], sb)
        dec_scr[i] = c["dec"]

    def rec(n, sts):
        row = pl.ds(pl.multiple_of(n * CHUNK, CHUNK), 1)
        out = []
        for i in range(n_str):
            st_scr[i, n] = sts[i]
            out.append(sts[i] * dec_scr[i, row, :] + kv_scr[i, n])
        return tuple(out)

    ends = lax.fori_loop(0, nc, rec, tuple(st0s))

    for n in range(nc):
        rows = slice(n * CHUNK, (n + 1) * CHUNK)
        for i in range(n_str):
            o_scr[i, rows, :] += lax.dot_general(bf[i][0][rows], st_scr[i, n].astype(BF16), NT_DIMS,
                                                 preferred_element_type=F32)
    return ends, bf


def _hgrn_fwd(proj5, lb_logits, gn):
    _, Bl, S, D = proj5.shape
    H = D // HEAD
    sb = min(SUB_BLOCK, S)
    nsb = S // sb
    nc = sb // CHUNK

    def body(p_ref, lbl_ref, gn_ref, ain_ref, aint_ref, o_ref, st_ref, carry, o_scr, kv_scr, st_scr, dec_scr):
        @pl.when(pl.program_id(1) == 0)
        def _():
            carry[...] = jnp.zeros_like(carry)

        st0s = [carry[b] for b in range(Bl)]
        cs = [_hgrn_gates(p_ref[0, b], p_ref[1, b], lbl_ref[...]) for b in range(Bl)]
        ends, _ = _hgrn_forward_blocks(cs, [p_ref[2, b] for b in range(Bl)], st0s, sb,
                                       o_scr, kv_scr, st_scr, dec_scr)
        for b in range(Bl):
            carry[b] = ends[b]
            st_ref[b, 0] = st_scr[b].astype(BF16)
            o = o_scr[b]
            o_ref[b] = o
            rinv = lax.rsqrt(jnp.mean(o * o, axis=-1, keepdims=True) + RMS_EPS)
            ain = o * rinv * gn_ref[...] * _sigmoid(p_ref[3, b])
            ain_ref[b] = ain.astype(BF16)
            aint_ref[b] = ain.T.astype(BF16)

    return pl.pallas_call(
        body, name="hgrn_fwd", grid=(H, nsb),
        in_specs=[pl.BlockSpec((4, Bl, sb, HEAD), lambda h, s: (0, 0, s, h)),
                  pl.BlockSpec((2, HEAD), lambda h, s: (0, h)),
                  pl.BlockSpec((1, HEAD), lambda h, s: (0, h))],
        out_specs=[pl.BlockSpec((Bl, sb, HEAD), lambda h, s: (0, s, h)),
                   pl.BlockSpec((Bl, HEAD, sb), lambda h, s: (0, h, s)),
                   pl.BlockSpec((Bl, sb, HEAD), lambda h, s: (0, s, h)),
                   pl.BlockSpec((Bl, 1, nc, HEAD, HEAD), lambda h, s: (0, h, s, 0, 0))],
        out_shape=[jax.ShapeDtypeStruct((Bl, S, D), BF16), jax.ShapeDtypeStruct((Bl, D, S), BF16),
                   jax.ShapeDtypeStruct((Bl, S, D), F32),
                   jax.ShapeDtypeStruct((Bl, H, S // CHUNK, HEAD, HEAD), BF16)],
        scratch_shapes=[pltpu.VMEM((Bl, HEAD, HEAD), F32), pltpu.VMEM((Bl, sb, HEAD), F32),
                        pltpu.VMEM((Bl, nc, HEAD, HEAD), F32), pltpu.VMEM((Bl, nc, HEAD, HEAD), F32),
                        pltpu.VMEM((Bl, sb, HEAD), F32)],
        compiler_params=_params(("parallel", "arbitrary")),
    )(proj5, lb_logits, gn)


def _window_count(shape, g):
    pos = lax.broadcasted_iota(jnp.int32, shape, 0)
    return pos, jnp.minimum(pos + 1, jnp.left_shift(2, g)).astype(F32)


def _select_window(g, sums):
    return jnp.where(g == 0, sums[0], jnp.where(g == 1, sums[1], jnp.where(g == 2, sums[2], sums[3])))


def _pool_fwd(proj5, w_pool):
    _, Bl, S, D = proj5.shape
    pg = D // POOL_GROUPS

    def body(v_ref, w_ref, pooled_t_ref, bp_ref):
        g = pl.program_id(1)
        v = v_ref[0, 0]
        pos, cnt = _window_count(v.shape, g)
        cur, sums = v, []
        for sh in (1, 2, 4, 8):
            cur = cur + jnp.where(pos >= sh, pltpu.roll(cur, sh, 0), 0.0)
            sums.append(cur)
        pooled = _select_window(g, sums) / cnt - v
        pooled_t_ref[...] = pooled.T.astype(BF16)
        bp_ref[0] = jnp.dot(pooled.astype(BF16), w_ref[0], preferred_element_type=F32)

    return pl.pallas_call(
        body, name="pool_fwd", grid=(Bl, POOL_GROUPS),
        in_specs=[pl.BlockSpec((1, 1, S, pg), lambda b, g: (4, b, 0, g)),
                  pl.BlockSpec((1, pg, pg), lambda b, g: (g, 0, 0))],
        out_specs=[pl.BlockSpec((pg, S), lambda b, g: (g, b)),
                   pl.BlockSpec((1, S, pg), lambda b, g: (b, 0, g))],
        out_shape=[jax.ShapeDtypeStruct((D, Bl * S), BF16), jax.ShapeDtypeStruct((Bl, S, D), F32)],
        compiler_params=_params(("parallel", "parallel")),
    )(proj5, w_pool)


def _layer_norm_fwd(r):
    mu = jnp.mean(r, axis=-1, keepdims=True)
    d = r - mu
    rs = lax.rsqrt(jnp.mean(d * d, axis=-1, keepdims=True) + LN_EPS)
    return d * rs, rs


def _layer_norm_bwd(dy_g, xhat, rs):
    return rs * (dy_g - jnp.mean(dy_g, axis=-1, keepdims=True)
                 - xhat * jnp.mean(dy_g * xhat, axis=-1, keepdims=True))


def _mix_fwd(ain, proj, bp, x2, w_a, w_out, ps, g1, b1):
    T, D = x2.shape
    tm = min(256, T)

    def body(ain_ref, ga_ref, gb_ref, bp_ref, x_ref, wa_ref, wo_ref, ps_ref, g1_ref, b1_ref,
             a_ref, mgt_ref, xh_ref, rs_ref, x1b_ref, x1t_ref):
        a = jnp.dot(ain_ref[...], wa_ref[...], preferred_element_type=F32)
        a_ref[...] = a
        merged = _sigmoid(ga_ref[0]) * a + _sigmoid(gb_ref[0]) * (bp_ref[...] * ps_ref[...])
        mgt_ref[...] = merged.T.astype(BF16)
        r1 = ALPHA * x_ref[...] + jnp.dot(merged.astype(BF16), wo_ref[...], preferred_element_type=F32)
        xhat, rs = _layer_norm_fwd(r1)
        xh_ref[...] = xhat
        rs_ref[...] = rs
        x1 = xhat * g1_ref[...] + b1_ref[...]
        x1b_ref[...] = x1.astype(BF16)
        x1t_ref[...] = x1.T.astype(BF16)

    row = lambda i: (i, 0)
    col = lambda i: (0, i)
    full = lambda i: (0, 0)
    return pl.pallas_call(
        body, name="mix_fwd", grid=(T // tm,),
        in_specs=[pl.BlockSpec((tm, D), row),
                  pl.BlockSpec((1, tm, D), lambda i: (5, i, 0)),
                  pl.BlockSpec((1, tm, D), lambda i: (6, i, 0)),
                  pl.BlockSpec((tm, D), row), pl.BlockSpec((tm, D), row),
                  pl.BlockSpec((D, D), full), pl.BlockSpec((D, D), full),
                  pl.BlockSpec((1, D), full), pl.BlockSpec((1, D), full), pl.BlockSpec((1, D), full)],
        out_specs=[pl.BlockSpec((tm, D), row), pl.BlockSpec((D, tm), col), pl.BlockSpec((tm, D), row),
                   pl.BlockSpec((tm, 1), row), pl.BlockSpec((tm, D), row), pl.BlockSpec((D, tm), col)],
        out_shape=[jax.ShapeDtypeStruct((T, D), F32), jax.ShapeDtypeStruct((D, T), BF16),
                   jax.ShapeDtypeStruct((T, D), F32), jax.ShapeDtypeStruct((T, 1), F32),
                   jax.ShapeDtypeStruct((T, D), BF16), jax.ShapeDtypeStruct((D, T), BF16)],
        compiler_params=_params(("parallel",)),
    )(ain, proj, proj, bp, x2, w_a, w_out, ps, g1, b1)


def _mlp_fwd(x1b, w_up, w_down, xhat1, tgt, g1, b1, g2, b2):
    T, D = xhat1.shape
    FF = w_up.shape[1]
    tm = min(256, T)

    def body(x_ref, wu_ref, wd_ref, xh_ref, t_ref, g1_ref, b1_ref, g2_ref, b2_ref,
             hp_ref, h_ref, dr_ref, drb_ref, drt_ref, vec_ref):
        @pl.when(pl.program_id(0) == 0)
        def _():
            vec_ref[...] = jnp.zeros_like(vec_ref)

        xb = x_ref[...]
        x1 = xh_ref[...] * g1_ref[...] + b1_ref[...]
        r2 = ALPHA * x1
        for f in range(FF // D):
            cols = slice(f * D, (f + 1) * D)
            hp = jnp.dot(xb, wu_ref[:, cols], preferred_element_type=F32)
            hp_ref[:, cols] = hp
            h = jnp.square(jnp.maximum(hp, 0.0)).astype(BF16)
            h_ref[:, cols] = h
            r2 = r2 + jnp.dot(h, wd_ref[cols, :], preferred_element_type=F32)
        xhat2, rs2 = _layer_norm_fwd(r2)
        err = xhat2 * g2_ref[...] + b2_ref[...] - t_ref[...]
        dy = err / D
        vec_ref[5:6, :] += jnp.sum(dy * xhat2, axis=0, keepdims=True)
        vec_ref[6:7, :] += jnp.sum(dy, axis=0, keepdims=True)
        vec_ref[7:8, :] += jnp.sum(0.5 * err * err / D, axis=0, keepdims=True)
        dr = _layer_norm_bwd(dy * g2_ref[...], xhat2, rs2)
        dr_ref[...] = dr
        drb_ref[...] = dr.astype(BF16)
        drt_ref[...] = dr.T.astype(BF16)

    row = lambda i: (i, 0)
    full = lambda i: (0, 0)
    return pl.pallas_call(
        body, name="mlp_fwd", grid=(T // tm,),
        in_specs=[pl.BlockSpec((tm, D), row), _resident((D, FF)), _resident((FF, D)),
                  pl.BlockSpec((tm, D), row), pl.BlockSpec((tm, D), row),
                  pl.BlockSpec((1, D), full), pl.BlockSpec((1, D), full),
                  pl.BlockSpec((1, D), full), pl.BlockSpec((1, D), full)],
        out_specs=[pl.BlockSpec((tm, FF), row), pl.BlockSpec((tm, FF), row), pl.BlockSpec((tm, D), row),
                   pl.BlockSpec((tm, D), row), pl.BlockSpec((D, tm), lambda i: (0, i)),
                   pl.BlockSpec((8, D), full)],
        out_shape=[jax.ShapeDtypeStruct((T, FF), F32), jax.ShapeDtypeStruct((T, FF), BF16),
                   jax.ShapeDtypeStruct((T, D), F32), jax.ShapeDtypeStruct((T, D), BF16),
                   jax.ShapeDtypeStruct((D, T), BF16), jax.ShapeDtypeStruct((8, D), F32)],
        compiler_params=_params(("arbitrary",)),
    )(x1b, w_up, w_down, xhat1, tgt, g1, b1, g2, b2)


def _mlp_bwd(drb, dr, hp, w_up, w_down, xhat1, rs1, g1):
    T, D = dr.shape
    FF = hp.shape[1]
    tm = min(256, T)

    def body(drb_ref, dr_ref, hp_ref, wu_ref, wd_ref, xh_ref, rs_ref, g1_ref,
             dhp_ref, d1_ref, d1b_ref, vec_ref):
        @pl.when(pl.program_id(0) == 0)
        def _():
            vec_ref[...] = jnp.zeros_like(vec_ref)

        drb = drb_ref[...]
        dx1 = ALPHA * dr_ref[...]
        for f in range(FF // D):
            cols = slice(f * D, (f + 1) * D)
            dh = lax.dot_general(drb, wd_ref[cols, :], NT_DIMS, preferred_element_type=F32)
            dhp = (dh * (2.0 * jnp.maximum(hp_ref[:, cols], 0.0))).astype(BF16)
            dhp_ref[:, cols] = dhp
            dx1 = dx1 + lax.dot_general(dhp, wu_ref[:, cols], NT_DIMS, preferred_element_type=F32)
        xhat = xh_ref[...]
        vec_ref[3:4, :] += jnp.sum(dx1 * xhat, axis=0, keepdims=True)
        vec_ref[4:5, :] += jnp.sum(dx1, axis=0, keepdims=True)
        d1 = _layer_norm_bwd(dx1 * g1_ref[...], xhat, rs_ref[...])
        d1_ref[...] = d1
        d1b_ref[...] = d1.astype(BF16)

    row = lambda i: (i, 0)
    full = lambda i: (0, 0)
    return pl.pallas_call(
        body, name="mlp_bwd", grid=(T // tm,),
        in_specs=[pl.BlockSpec((tm, D), row), pl.BlockSpec((tm, D), row), pl.BlockSpec((tm, FF), row),
                  _resident((D, FF)), _resident((FF, D)),
                  pl.BlockSpec((tm, D), row), pl.BlockSpec((tm, 1), row), pl.BlockSpec((1, D), full)],
        out_specs=[pl.BlockSpec((tm, FF), row), pl.BlockSpec((tm, D), row), pl.BlockSpec((tm, D), row),
                   pl.BlockSpec((8, D), full)],
        out_shape=[jax.ShapeDtypeStruct((T, FF), BF16), jax.ShapeDtypeStruct((T, D), F32),
                   jax.ShapeDtypeStruct((T, D), BF16), jax.ShapeDtypeStruct((8, D), F32)],
        compiler_params=_params(("arbitrary",)),
    )(drb, dr, hp, w_up, w_down, xhat1, rs1, g1)


def _dw(name, a_t, b, n_j, a_spec, b_spec, o_shape, o_block, o_map, transpose_out=False, dep=None,
        into=(None, None), ob_shape=None, ob_map=None):
    def body(*refs):
        a_ref, b_ref, o_ref, ob_ref = refs[0], refs[1], refs[-2], refs[-1]
        b_val = b_ref[0] if len(b_ref.shape) == 3 else b_ref[...]
        if len(a_ref.shape) == 3:
            seq = a_ref.shape[2]
            p = sum(jnp.dot(a_ref[i], b_val[i * seq:(i + 1) * seq], preferred_element_type=F32)
                    for i in range(a_ref.shape[0]))
        else:
            p = jnp.dot(a_ref[...], b_val, preferred_element_type=F32)
        if transpose_out:
            p = p.T
        p = p.reshape(o_ref.shape)
        o_ref[...] = p
        ob_ref[...] = p.astype(BF16)

    kw = dict(name=name, grid=(n_j,), in_specs=[a_spec, b_spec],
              out_specs=[pl.BlockSpec(o_block, o_map), pl.BlockSpec(o_block, ob_map or o_map)],
              out_shape=[jax.ShapeDtypeStruct(o_shape, F32), jax.ShapeDtypeStruct(ob_shape or o_shape, BF16)],
              compiler_params=_params(("parallel",)))
    args = (a_t, b)
    aliases = {}
    for out_index, arr in enumerate(into):
        if arr is not None:
            aliases[len(args)] = out_index
            args = args + (arr,)
            kw["in_specs"] = kw["in_specs"] + [pl.BlockSpec(memory_space=pl.ANY)]
    if aliases:
        kw["input_output_aliases"] = aliases
    if dep is None:
        return pl.pallas_call(body, **kw)(*args)
    return _call_after(dep, body, args, **kw)


def _mix_bwd(d1b, proj, a, bp, w_a, w_out, w_pool, ps, dep):
    T, D = a.shape
    tm = min(256, T)
    pg = D // POOL_GROUPS

    def body(d1b_ref, ga_ref, gb_ref, a_ref, bp_ref, wa_ref, wo_ref, wp_ref, ps_ref,
             da_ref, dbp_ref, dain_ref, dpl_ref, dg_ref, vec_ref):
        @pl.when(pl.program_id(0) == 0)
        def _():
            vec_ref[...] = jnp.zeros_like(vec_ref)

        dm = lax.dot_general(d1b_ref[...], wo_ref[...], NT_DIMS, preferred_element_type=F32)
        sa, sg = _sigmoid(ga_ref[0]), _sigmoid(gb_ref[0])
        bp_v, ps_v = bp_ref[...], ps_ref[...]
        da = (dm * sa).astype(BF16)
        db = dm * sg
        dg_ref[0] = (dm * a_ref[...] * sa * (1.0 - sa)).astype(BF16)
        dg_ref[1] = (dm * (bp_v * ps_v) * sg * (1.0 - sg)).astype(BF16)
        vec_ref[2:3, :] += jnp.sum(db * bp_v, axis=0, keepdims=True)
        dbp = (db * ps_v).astype(BF16)
        da_ref[...] = da
        dbp_ref[...] = dbp
        dain_ref[...] = lax.dot_general(da, wa_ref[...], NT_DIMS, preferred_element_type=F32)
        for g in range(POOL_GROUPS):
            cols = slice(g * pg, (g + 1) * pg)
            dpl_ref[:, cols] = lax.dot_general(dbp[:, cols], wp_ref[g], NT_DIMS,
                                               preferred_element_type=F32)

    row = lambda i: (i, 0)
    full = lambda i: (0, 0)
    return _call_after(
        dep, body, (d1b, proj, proj, a, bp, w_a, w_out, w_pool, ps), name="mix_bwd", grid=(T // tm,),
        in_specs=[pl.BlockSpec((tm, D), row),
                  pl.BlockSpec((1, tm, D), lambda i: (5, i, 0)),
                  pl.BlockSpec((1, tm, D), lambda i: (6, i, 0)),
                  pl.BlockSpec((tm, D), row), pl.BlockSpec((tm, D), row),
                  pl.BlockSpec((D, D), full), pl.BlockSpec((D, D), full),
                  pl.BlockSpec((POOL_GROUPS, pg, pg), lambda i: (0, 0, 0)),
                  pl.BlockSpec((1, D), full)],
        out_specs=[pl.BlockSpec((tm, D), row), pl.BlockSpec((tm, D), row),
                   pl.BlockSpec((tm, D), row), pl.BlockSpec((tm, D), row),
                   pl.BlockSpec((2, tm, D), lambda i: (0, i, 0)),
                   pl.BlockSpec((8, D), full)],
        out_shape=[jax.ShapeDtypeStruct((T, D), BF16), jax.ShapeDtypeStruct((T, D), BF16),
                   jax.ShapeDtypeStruct((T, D), F32), jax.ShapeDtypeStruct((T, D), F32),
                   jax.ShapeDtypeStruct((2, T, D), BF16), jax.ShapeDtypeStruct((8, D), F32)],
        compiler_params=_params(("arbitrary",)))


def _pool_bwd(dpooled3, dep):
    Bl, S, D = dpooled3.shape
    pg = D // POOL_GROUPS

    def body(dp_ref, dv_ref):
        g = pl.program_id(1)
        dp = dp_ref[0]
        pos, cnt = _window_count(dp.shape, g)
        cur, sums = dp / cnt, []
        for sh in (1, 2, 4, 8):
            cur = cur + jnp.where(pos < S - sh, pltpu.roll(cur, S - sh, 0), 0.0)
            sums.append(cur)
        dv_ref[0] = (_select_window(g, sums) - dp).astype(BF16)

    spec = pl.BlockSpec((1, S, pg), lambda b, g: (b, 0, g))
    return _call_after(
        dep, body, (dpooled3,), name="pool_bwd", grid=(Bl, POOL_GROUPS), in_specs=[spec], out_specs=spec,
        out_shape=jax.ShapeDtypeStruct((Bl, S, D), BF16),
        compiler_params=_params(("parallel", "parallel")))


def _hgrn_bwd(proj5, lb_logits, gn, dain3, o3, st_all, dep):
    _, Bl, S, D = proj5.shape
    H = D // HEAD
    sb = min(SUB_BLOCK, S)
    nsb = S // sb
    nc = sb // CHUNK
    streams = range(Bl)

    def body(p_ref, lbl_ref, gn_ref, dain_ref, o_ref, st_ref, d_ref, vec_ref,
             dcarry, kv_scr, dst_scr, dec_scr, dvi_scr, dke_scr, dqi_scr):
        s = pl.program_id(1)

        @pl.when(s == 0)
        def _():
            dcarry[...] = jnp.zeros_like(dcarry)
            vec_ref[...] = jnp.zeros_like(vec_ref)

        qs, vs, ogs = [p_ref[0, b] for b in streams], [p_ref[2, b] for b in streams], [p_ref[3, b] for b in streams]
        cs = [_hgrn_gates(qs[b], p_ref[1, b], lbl_ref[...]) for b in streams]
        bf = [(cs[b]["qd"].astype(BF16), cs[b]["ki"].astype(BF16), cs[b]["ke"].astype(BF16),
               vs[b].astype(BF16)) for b in streams]
        mask = _intra_mask()
        gn_v = gn_ref[...]
        keep = []
        for b in streams:
            qd_b, ki_b, ke_b, v_b = bf[b]
            dec_scr[b] = cs[b]["dec"]
            o = o_ref[b]
            rinv = lax.rsqrt(jnp.mean(o * o, axis=-1, keepdims=True) + RMS_EPS)
            on = o * rinv
            so = _sigmoid(ogs[b])
            dain = dain_ref[b]
            vec_ref[1:2, :] += jnp.sum(dain * on * so, axis=0, keepdims=True)
            d_og = dain * on * gn_v * so * (1.0 - so)
            d_on = dain * gn_v * so
            do = rinv * (d_on - on * jnp.mean(d_on * on, axis=-1, keepdims=True))
            do_b = do.astype(BF16)
            dv_parts, dqd_parts, dki_parts = [], [], []
            for g in range(sb // GROUP):
                sl = slice(g * GROUP, (g + 1) * GROUP)
                sc = lax.dot_general(qd_b[sl], ki_b[sl], NT_DIMS, preferred_element_type=F32)
                a = jnp.where(mask, sc, 0.0).astype(BF16)
                da = lax.dot_general(do_b[sl], v_b[sl], NT_DIMS, preferred_element_type=F32)
                da = jnp.where(mask, da, 0.0).astype(BF16)
                dv_parts.append(lax.dot_general(a, do_b[sl], TN_DIMS, preferred_element_type=F32))
                dqd_parts.append(jnp.dot(da, ki_b[sl], preferred_element_type=F32))
                dki_parts.append(lax.dot_general(da, qd_b[sl], TN_DIMS, preferred_element_type=F32))
            keep.append(dict(d_og=d_og, do_b=do_b, dv_intra=jnp.concatenate(dv_parts, axis=0),
                             dqd_intra=jnp.concatenate(dqd_parts, axis=0),
                             dki=jnp.concatenate(dki_parts, axis=0)))
            _chunk_outer(do, qd_b, kv_scr.at[b], sb)

        def rrec(i, dsts):
            n = nc - 1 - i
            row = pl.ds(pl.multiple_of(n * CHUNK, CHUNK), 1)
            out = []
            for b in streams:
                dst_scr[b, n] = dsts[b]
                out.append(dsts[b] * dec_scr[b, row, :] + kv_scr[b, n])
            return tuple(out)

        ends = lax.fori_loop(0, nc, rrec, tuple(dcarry[b] for b in streams))
        for b in streams:
            dcarry[b] = ends[b]
        for n in range(nc):
            rows = slice(n * CHUNK, (n + 1) * CHUNK)
            for b in streams:
                qd_b, ki_b, ke_b, v_b = bf[b]
                dst_b = dst_scr[b, n].astype(BF16)
                dvi_scr[b, rows, :] = lax.dot_general(ke_b[rows], dst_b, NT_DIMS, preferred_element_type=F32)
                dke_scr[b, rows, :] = jnp.dot(v_b[rows], dst_b, preferred_element_type=F32)
                dqi_scr[b, rows, :] = jnp.dot(keep[b]["do_b"][rows], st_ref[b, 0, n],
                                              preferred_element_type=F32)
        for b in streams:
            c, k = cs[b], keep[b]
            ddec = jnp.sum(dst_scr[b] * st_ref[b, 0].astype(F32), axis=1)
            dgl = jnp.broadcast_to(ddec[:, None, :], (nc, CHUNK, HEAD)).reshape(sb, HEAD) * c["dec"]
            dqd = k["dqd_intra"] + dqi_scr[b]
            dke = dke_scr[b]
            dki = k["dki"]
            t_ke = dke * c["ke"]
            dG = dqd * c["qd"] - dki * c["ki"] - t_ke
            dgl = dgl + _chunk_cumsum(t_ke) + _chunk_cumsum(t_ke, reverse=True) - t_ke
            dlogf = _chunk_cumsum(dG, reverse=True) + dgl
            dk = dki * c["e_ng"] + dke * c["e_ge"]
            df = dlogf / c["f"] - dk
            sg, sq, lb, q = c["sg"], c["sq"], c["lb"], qs[b]
            vec_ref[0:1, :] += jnp.sum(df * (1.0 - sg), axis=0, keepdims=True)
            d_ref[0, b] = (dqd * c["e_g"] * Q_SCALE * (sq + q * sq * (1.0 - sq))).astype(BF16)
            d_ref[1, b] = (df * (1.0 - lb) * sg * (1.0 - sg)).astype(BF16)
            d_ref[2, b] = (k["dv_intra"] + dvi_scr[b]).astype(BF16)
            d_ref[3, b] = k["d_og"].astype(BF16)

    rev = lambda s: nsb - 1 - s
    big = pltpu.VMEM((Bl, nc, HEAD, HEAD), F32)
    rows_f32 = pltpu.VMEM((Bl, sb, HEAD), F32)
    return _call_after(
        dep, body, (proj5, lb_logits, gn, dain3, o3, st_all), name="hgrn_bwd", grid=(H, nsb),
        in_specs=[pl.BlockSpec((4, Bl, sb, HEAD), lambda h, s: (0, 0, rev(s), h)),
                  pl.BlockSpec((2, HEAD), lambda h, s: (0, h)),
                  pl.BlockSpec((1, HEAD), lambda h, s: (0, h)),
                  pl.BlockSpec((Bl, sb, HEAD), lambda h, s: (0, rev(s), h)),
                  pl.BlockSpec((Bl, sb, HEAD), lambda h, s: (0, rev(s), h)),
                  pl.BlockSpec((Bl, 1, nc, HEAD, HEAD), lambda h, s: (0, h, rev(s), 0, 0))],
        out_specs=[pl.BlockSpec((4, Bl, sb, HEAD), lambda h, s: (0, 0, rev(s), h)),
                   pl.BlockSpec((8, HEAD), lambda h, s: (0, h))],
        out_shape=[jax.ShapeDtypeStruct((4, Bl, S, D), BF16), jax.ShapeDtypeStruct((8, D), F32)],
        scratch_shapes=[pltpu.VMEM((Bl, HEAD, HEAD), F32), big, big, rows_f32, rows_f32, rows_f32, rows_f32],
        compiler_params=_params(("parallel", "arbitrary")))


def _dx(d1, dh4, dpv, dg2, w_in, dep):
    T, D = d1.shape
    tm = min(256, T)

    def body(d1_ref, dh_ref, dp_ref, dg_ref, w_ref, o_ref):
        blocks = [dh_ref[0], dh_ref[1], dh_ref[2], dh_ref[3], dp_ref[...], dg_ref[0], dg_ref[1]]
        acc = ALPHA * d1_ref[...]
        for j, blk in enumerate(blocks):
            acc = acc + lax.dot_general(blk, w_ref[:, j * D:(j + 1) * D], NT_DIMS, preferred_element_type=F32)
        o_ref[...] = acc

    row = lambda i: (i, 0)
    return _call_after(
        dep, body, (d1, dh4, dpv, dg2, w_in), name="dx", grid=(T // tm,),
        in_specs=[pl.BlockSpec((tm, D), row), pl.BlockSpec((4, tm, D), lambda i: (0, i, 0)),
                  pl.BlockSpec((tm, D), row), pl.BlockSpec((2, tm, D), lambda i: (0, i, 0)),
                  _resident((D, N_SEC * D))],
        out_specs=pl.BlockSpec((tm, D), row),
        out_shape=jax.ShapeDtypeStruct((T, D), F32),
        compiler_params=_params(("parallel",)))


EARLY_SEC = 4
DW_COLS = 512


def _dw_in_part(name, x_t, b, sections, first_sec, into, dep, ob_shape, ob_first):
    D, T = x_t.shape
    per = D // DW_COLS
    b_spec = (pl.BlockSpec((1, T, DW_COLS), lambda j: (j // per, 0, j % per)) if b.ndim == 3
              else pl.BlockSpec((T, DW_COLS), lambda j: (0, j)))
    return _dw(name, x_t, b, sections * per, _resident((D, T)), b_spec, (D, N_SEC * D), (D, DW_COLS),
               lambda j: (0, first_sec * per + j), dep=dep, into=into, ob_shape=ob_shape,
               ob_map=lambda j: (0, ob_first * per + j))


def _dw_in_early(x_t, dpv, dg2, dep):
    D = x_t.shape[0]
    early_shape = (D, (N_SEC - EARLY_SEC) * D)
    f32, bf = _dw_in_part("dw_in_gates", x_t, dg2, 2, 5, (None, None), dep, early_shape, 1)
    return _dw_in_part("dw_in_pool", x_t, dpv, 1, 4, (f32, bf), None, early_shape, 0)


def _dw_in_late(x_t, dh4, f32_early, dep):
    D = x_t.shape[0]
    return _dw_in_part("dw_in_rec", x_t, dh4, EARLY_SEC, 0, (f32_early, None), dep, (D, EARLY_SEC * D), 0)


def _adam_shard(name, me_arr, grad, land, layout, w, m, v):
    shape = layout.shape
    n_split = 4
    blk = (shape[0] // n_split,) + shape[1:]
    zeros = (0,) * (len(shape) - 1)

    def body(me_ref, g_ref, r_ref, w_ref, m_ref, v_ref, g_out, d_out, m_out, v_out):
        g = g_ref[...]
        for k in range(N_DEV - 1):
            g = g + r_ref[k].astype(F32)
        d, m2, v2 = _adamw(w_ref[...], g, m_ref[...], v_ref[...])
        g_out[...] = g
        d_out[...] = d
        m_out[...] = m2
        v_out[...] = v2

    def own(i, me_ref):
        bi = layout.block_index(me_ref[0])
        return (bi[0] * n_split + i,) + tuple(bi[1:]) if layout.kind == "row" else (i,) + tuple(bi[1:])

    plain = pl.BlockSpec(blk, lambda i, me_ref: (i,) + zeros)
    grid_spec = pltpu.PrefetchScalarGridSpec(
        num_scalar_prefetch=1, grid=(n_split,),
        in_specs=[pl.BlockSpec(blk, own),
                  pl.BlockSpec((N_DEV - 1,) + blk, lambda i, me_ref: (0, i) + zeros),
                  plain, plain, plain],
        out_specs=[plain] * 4)
    return pl.pallas_call(
        body, name=name, grid_spec=grid_spec,
        out_shape=[jax.ShapeDtypeStruct(shape, F32)] * 4,
        compiler_params=_params(("parallel",)),
    )(me_arr, grad, land, w, m, v)


def _vec_allreduce(vec):
    D = vec.shape[1]

    def body(vec_ref, tot_ref, gat, send_sems, recv_sems):
        x, y, c = _me()
        me = 4 * x + 2 * y + c
        gat[me] = vec_ref[...]
        copies = []
        for k in range(1, N_DEV):
            cp = pltpu.make_async_remote_copy(
                src_ref=vec_ref, dst_ref=gat.at[me], send_sem=send_sems.at[k - 1],
                recv_sem=recv_sems.at[k - 1], device_id=_peer(k, x, y, c), device_id_type=MESH)
            cp.start()
            copies.append(cp)
        for cp in copies:
            cp.wait()
        tot = gat[0]
        for d in range(1, N_DEV):
            tot = tot + gat[d]
        tot_ref[...] = tot

    vm = pl.BlockSpec(memory_space=pltpu.VMEM)
    return pl.pallas_call(
        body, name="vec_allreduce", out_shape=jax.ShapeDtypeStruct(vec.shape, F32),
        in_specs=[vm], out_specs=vm,
        scratch_shapes=[pltpu.VMEM((N_DEV, 8, D), F32), pltpu.SemaphoreType.DMA((N_DEV - 1,)),
                        pltpu.SemaphoreType.DMA((N_DEV - 1,))],
    )(vec)


def _vec_adam(tot, small_w, small_m, small_v):
    n = len(small_w)

    def body(*refs):
        tot = refs[0][...]
        ws, ms, vs = refs[1:1 + n], refs[1 + n:1 + 2 * n], refs[1 + 2 * n:1 + 3 * n]
        outs = refs[1 + 3 * n:]
        loss_ref, g_out, d_out = outs[0], outs[1:1 + n], outs[1 + n:1 + 2 * n]
        m_out, v_out = outs[1 + 2 * n:1 + 3 * n], outs[1 + 3 * n:1 + 4 * n]
        loss_ref[...] = jnp.broadcast_to(jnp.sum(tot[7:8, :], axis=1, keepdims=True), loss_ref.shape)
        lbl = ws[0][...]
        mx = jnp.maximum(lbl[0:1, :], lbl[1:2, :])
        e0, e1 = jnp.exp(lbl[0:1, :] - mx), jnp.exp(lbl[1:2, :] - mx)
        p0 = e0 / (e0 + e1)
        dl0 = tot[0:1, :] * p0 * (1.0 - p0)
        grads = [jnp.concatenate([dl0, -dl0], axis=0)] + [tot[r:r + 1, :] for r in range(1, n)]
        for i in range(n):
            d, m2, v2 = _adamw(ws[i][...], grads[i], ms[i][...], vs[i][...])
            g_out[i][...] = grads[i]
            d_out[i][...] = d
            m_out[i][...] = m2
            v_out[i][...] = v2

    vm = pl.BlockSpec(memory_space=pltpu.VMEM)
    shapes = [jax.ShapeDtypeStruct(w.shape, F32) for w in small_w]
    return pl.pallas_call(
        body, name="vec_adam",
        out_shape=[jax.ShapeDtypeStruct((1, 128), F32)] + shapes * 4,
        in_specs=[vm] * (1 + 3 * n), out_specs=[vm] * (1 + 4 * n),
    )(tot, *small_w, *small_m, *small_v)


def kernel(x, w_in, lb_logits, hgrn_norm_g, w_a, w_pool, pool_scale, w_out, ln1_g, ln1_b, w_up, w_down, ln2_g, ln2_b, loss_target, m_w_in, m_lb_logits, m_hgrn_norm_g, m_w_a, m_w_pool, m_pool_scale, m_w_out, m_ln1_g, m_ln1_b, m_w_up, m_w_down, m_ln2_g, m_ln2_b, v_w_in, v_lb_logits, v_hgrn_norm_g, v_w_a, v_w_pool, v_pool_scale, v_w_out, v_ln1_g, v_ln1_b, v_w_up, v_w_down, v_ln2_g, v_ln2_b):
    Bl, S, D = x.shape
    T = Bl * S
    pg = D // POOL_GROUPS
    x2 = x.reshape(T, D)
    tgt = loss_target.reshape(T, D)
    me = 4 * lax.axis_index("x") + 2 * lax.axis_index("y") + lax.axis_index("c")
    me_arr = jnp.reshape(me, (1,)).astype(jnp.int32)

    names = ["w_in", "w_a", "w_pool", "w_out", "w_up", "w_down"]
    big_w = dict(zip(names, [w_in[0], w_a[0], w_pool[0], w_out[0], w_up[0], w_down[0]]))
    big_m = dict(zip(names, [m_w_in[0], m_w_a[0], m_w_pool[0], m_w_out[0], m_w_up[0], m_w_down[0]]))
    big_v = dict(zip(names, [v_w_in[0], v_w_a[0], v_w_pool[0], v_w_out[0], v_w_up[0], v_w_down[0]]))
    kinds = dict(w_in="col", w_a="row", w_pool="pool", w_out="row", w_up="col", w_down="row")
    lay = {nm: _Sharded(kinds[nm], big_w[nm].shape) for nm in names}
    wb = {nm: big_w[nm].astype(BF16) for nm in names}

    (w_in_f,) = _all_gather("ag_w_in", [wb["w_in"]], [lay["w_in"]])
    def gather_start(name, nms, after):
        return _exchange_start(name, [wb[nm] for nm in nms], [lax.empty(lay[nm].full_shape, BF16) for nm in nms],
                               src_at=lambda w, ref, peer: ref,
                               dst_at=lambda w, ref, mine, k: lay[nms[w]].at(ref, mine), after=after, own=True)

    ag_mix = gather_start("ag_mix", ["w_a", "w_pool", "w_out"], w_in_f)
    ag_mlp = gather_start("ag_mlp", ["w_up", "w_down"], ag_mix["token"])

    proj, x_t = _proj(x2, w_in_f, ag_mlp["token"])
    proj5 = proj.reshape(N_SEC, Bl, S, D)
    ain3, ain_t, o3, st_all = _hgrn_fwd(proj5, lb_logits, hgrn_norm_g)
    w_a_f, w_pool_f, w_out_f = _exchange_wait(ag_mix, ain3)
    pooled_t, bp3 = _pool_fwd(proj5, w_pool_f)
    ain, bp = ain3.reshape(T, D), bp3.reshape(T, D)
    a, merged_t, xhat1, rs1, x1b, x1_t = _mix_fwd(ain, proj, bp, x2, w_a_f, w_out_f, pool_scale, ln1_g, ln1_b)
    w_up_f, w_down_f = _exchange_wait(ag_mlp, x1b)
    hp, h, dr2, dr2b, dr2_t, vec_mlp = _mlp_fwd(x1b, w_up_f, w_down_f, xhat1, tgt, ln1_g, ln1_b, ln2_g, ln2_b)

    def scatter_start(name, nms, grads_b, after):
        lands = [lax.empty((N_DEV - 1,) + lay[nm].shape, BF16) for nm in nms]
        return _exchange_start(name, grads_b, lands,
                               src_at=lambda w, ref, peer: lay[nms[w]].at(ref, peer),
                               dst_at=lambda w, ref, mine, k: ref.at[k - 1], after=after)

    dhp, dr1, dr1b, vec_ln1 = _mlp_bwd(dr2b, dr2, hp, w_up_f, w_down_f, xhat1, rs1, ln1_g)
    FF = 4 * D
    whole_t = _resident((D, T))
    cols_b = pl.BlockSpec((T, DW_COLS), lambda j: (0, j))
    cols_o = ((D, DW_COLS), lambda j: (0, j))
    gw, gwb = {}, {}
    gw["w_down"], gwb["w_down"] = _dw(
        "dw_down", dr2_t, h, FF // DW_COLS, whole_t, cols_b, (FF, D), (DW_COLS, D), lambda j: (j, 0),
        transpose_out=True)
    rs_down = scatter_start("rs_w_down", ["w_down"], [gwb["w_down"]], gw["w_down"])
    gw["w_up"], gwb["w_up"] = _dw("dw_up", x1_t, dhp, FF // DW_COLS, whole_t, cols_b, (D, FF), *cols_o,
                                  dep=rs_down["token"])
    rs_up = scatter_start("rs_w_up", ["w_up"], [gwb["w_up"]], gw["w_up"])
    da_b, dbp_b, dain, dpooled, dg2, vec_mix = _mix_bwd(dr1b, proj, a, bp, w_a_f, w_out_f, w_pool_f, pool_scale,
                                                        rs_up["token"])
    gw["w_out"], gwb["w_out"] = _dw("dw_out", merged_t, dr1b, D // DW_COLS, whole_t, cols_b, (D, D), *cols_o)
    gw["w_a"], gwb["w_a"] = _dw("dw_a", ain_t, da_b, D // DW_COLS, _resident((Bl, D, S)), cols_b, (D, D), *cols_o)
    gw["w_pool"], gwb["w_pool"] = _dw(
        "dw_pool", pooled_t, dbp_b, POOL_GROUPS, pl.BlockSpec((pg, T), lambda j: (j, 0)),
        pl.BlockSpec((T, pg), lambda j: (0, j)), (POOL_GROUPS, pg, pg), (1, pg, pg), lambda j: (j, 0, 0))
    mid = ["w_out", "w_a", "w_pool"]
    rs_mid = scatter_start("rs_w_mid", mid, [gwb[nm] for nm in mid], gw["w_pool"])
    dpv = _pool_bwd(dpooled.reshape(Bl, S, D), rs_mid["token"]).reshape(T, D)
    gw_in_early, gwb_in_early = _dw_in_early(x_t, dpv, dg2, rs_mid["token"])
    land_in = lax.empty((N_DEV - 1,) + lay["w_in"].shape, BF16)
    rs_in_early = _w_in_scatter_start("rs_w_in_early", gwb_in_early, land_in, True, gw_in_early)
    dh4, vec_hgrn = _hgrn_bwd(proj5, lb_logits, hgrn_norm_g, dain.reshape(Bl, S, D), o3, st_all,
                              rs_in_early["token"])
    dh4 = dh4.reshape(4, T, D)
    vec_tot = _vec_allreduce(vec_mlp + vec_ln1 + vec_mix + vec_hgrn)
    gw["w_in"], gwb_in_late = _dw_in_late(x_t, dh4, gw_in_early, vec_tot)
    rs_in_late = _w_in_scatter_start("rs_w_in_late", gwb_in_late, rs_in_early["land"], False, gw["w_in"])
    grad_x2 = _dx(dr1, dh4, dpv, dg2, w_in_f, rs_in_late["token"])
    grad_x = grad_x2.reshape(Bl, S, D)

    small_names =["lb_logits", "hgrn_norm_g", "pool_scale", "ln1_g", "ln1_b", "ln2_g", "ln2_b"]
    small_w = [lb_logits, hgrn_norm_g, pool_scale, ln1_g, ln1_b, ln2_g, ln2_b]
    small_m = [m_lb_logits, m_hgrn_norm_g, m_pool_scale, m_ln1_g, m_ln1_b, m_ln2_g, m_ln2_b]
    small_v = [v_lb_logits, v_hgrn_norm_g, v_pool_scale, v_ln1_g, v_ln1_b, v_ln2_g, v_ln2_b]
    res = _vec_adam(vec_tot, small_w, small_m, small_v)
    loss = res[0][0, 0]
    n = len(small_w)
    small = {nm: (res[1 + i], res[1 + n + i], res[1 + 2 * n + i], res[1 + 3 * n + i])
             for i, nm in enumerate(small_names)}

    big, last = {}, grad_x2

    def adam(nm, land):
        outs = _adam_shard("adam_" + nm, me_arr, gw[nm], land, lay[nm], big_w[nm], big_m[nm], big_v[nm])
        big[nm] = tuple(t[None] for t in outs)
        return outs[0]

    for pend, nms in ((rs_down, ["w_down"]), (rs_up, ["w_up"]), (rs_mid, mid)):
        for nm, land in zip(nms, _exchange_wait(pend, last)):
            last = adam(nm, land)
    land_in = _w_in_scatter_wait(rs_in_early, rs_in_late["land"], last)
    adam("w_in", _w_in_scatter_wait(rs_in_late, land_in, res[0]))

    order = ["w_in", "lb_logits", "hgrn_norm_g", "w_a", "w_pool", "pool_scale", "w_out", "ln1_g", "ln1_b",
             "w_up", "w_down", "ln2_g", "ln2_b"]
    allp = {**big, **small}
    out = [loss, grad_x]
    for part in range(4):
        out += [allp[nm][part] for nm in order]
    return tuple(out)
```

```python
import jax
import jax.numpy as jnp
from jax import lax
from jax.experimental import pallas as pl
from jax.experimental.pallas import tpu as pltpu

F32 = jnp.float32
BF16 = jnp.bfloat16
MESH = pl.DeviceIdType.MESH

N_DEV = 8
HEAD = 128
CHUNK = 16
SUBLANES = 8
GROUP = 128
SUB_BLOCK = 1024
ROW_TILE = 512
MLP_ROW_TILE = 256
DW_COLS = 512
EARLY_SEC = 4
CH_PER_GROUP = GROUP // CHUNK
N_SEC = 7
POOL_GROUPS = 4
ALPHA = (2.0 * 1) ** 0.25
LN_EPS = 1e-5
RMS_EPS = 1e-6
Q_SCALE = HEAD ** -0.5
ADAM_LR = 0.001
ADAM_B1 = 0.9
ADAM_B2 = 0.999
ADAM_EPS = 1e-08
ADAM_WD = 0.01
ADAM_STEP = 10
VMEM_LIMIT = 60 << 20

NT_DIMS = (((1,), (1,)), ((), ()))
TN_DIMS = (((0,), (0,)), ((), ()))


def _params(sem=None):
    kw = dict(vmem_limit_bytes=VMEM_LIMIT)
    if sem is not None:
        kw["dimension_semantics"] = sem
    return pltpu.CompilerParams(**kw)


def _me():
    return lax.axis_index("x"), lax.axis_index("y"), lax.axis_index("c")


def _sigmoid(v):
    return jax.nn.sigmoid(v)


def _adamw(w, g, m, v):
    m = ADAM_B1 * m + (1.0 - ADAM_B1) * g
    v = ADAM_B2 * v + (1.0 - ADAM_B2) * jnp.square(g)
    m_hat = m / (1.0 - ADAM_B1 ** ADAM_STEP)
    v_hat = v / (1.0 - ADAM_B2 ** ADAM_STEP)
    delta = -ADAM_LR * (m_hat / (jnp.sqrt(v_hat) + ADAM_EPS) + ADAM_WD * w)
    return delta, m, v


class _Sharded:
    def __init__(self, kind, shard_shape):
        self.kind, self.shape = kind, tuple(shard_shape)

    @property
    def full_shape(self):
        r = self.shape
        if self.kind == "row":
            return (N_DEV * r[0],) + r[1:]
        return (r[0], N_DEV * r[1]) + r[2:]

    def at(self, ref, d):
        if self.kind == "col":
            n = self.shape[1]
            return ref.at[:, pl.ds(pl.multiple_of(d * n, 128), n)]
        if self.kind == "row":
            n = self.shape[0]
            return ref.at[pl.ds(pl.multiple_of(d * n, 16), n), :]
        n = self.shape[1]
        return ref.at[:, pl.ds(pl.multiple_of(d * n, 16), n), :]

    def block_index(self, d):
        return {"col": (0, d), "row": (d, 0), "pool": (0, d, 0)}[self.kind]


def _peer(k, x, y, c):
    return (1 - x if k & 4 else x, 1 - y if k & 2 else y, 1 - c if k & 1 else c)


def _all_gather(name, shards, layouts):
    nw = len(shards)

    def body(*refs):
        ins, outs = refs[:nw], refs[nw:2 * nw]
        send_sems, recv_sems, local_sems = refs[2 * nw:]
        x, y, c = _me()
        me = (x, y, c)
        sibling = (x, y, 1 - c)
        chips = [(1 - x, y), (x, 1 - y), (1 - x, 1 - y)]

        def copy(w, k, block, to, src=None):
            px, py, pc = block
            dst = layouts[w].at(outs[w], 4 * px + 2 * py + pc)
            return pltpu.make_async_remote_copy(
                src_ref=dst if src is None else src, dst_ref=dst,
                send_sem=send_sems.at[w, k], recv_sem=recv_sems.at[w, k],
                device_id=to, device_id_type=MESH)

        def place(w):
            mine = pltpu.make_async_copy(ins[w], layouts[w].at(outs[w], 4 * x + 2 * y + c), local_sems.at[w])
            mine.start()
            return mine

        first = []
        for w in range(nw):
            first.append(copy(w, 0, me, sibling, src=ins[w]))
            first += [copy(w, 1 + j, me, (*chip, c), src=ins[w]) for j, chip in enumerate(chips)]
        for cp in first:
            cp.start()
        local = [place(w) for w in range(nw)]
        passed = []
        for w in range(nw):
            for j, chip in enumerate(chips):
                copy(w, 1 + j, (*chip, c), me).wait_recv()
                fwd = copy(w, 4 + j, (*chip, c), sibling)
                fwd.start()
                passed.append(fwd)
        for w in range(nw):
            copy(w, 0, sibling, me).wait_recv()
            for j, chip in enumerate(chips):
                copy(w, 4 + j, (*chip, 1 - c), me).wait_recv()
        for cp in first + passed:
            cp.wait_send()
        for cp in local:
            cp.wait()

    any_spec = pl.BlockSpec(memory_space=pl.ANY)
    return pl.pallas_call(
        body, name=name,
        out_shape=[jax.ShapeDtypeStruct(l.full_shape, s.dtype) for s, l in zip(shards, layouts)],
        in_specs=[any_spec] * nw, out_specs=[any_spec] * nw,
        scratch_shapes=[pltpu.SemaphoreType.DMA((nw, 7)), pltpu.SemaphoreType.DMA((nw, 7)),
                        pltpu.SemaphoreType.DMA((nw,))],
    )(*shards)


HBM_SPEC = pl.BlockSpec(memory_space=pltpu.HBM)
SEM_SPEC = pl.BlockSpec(memory_space=pltpu.SEMAPHORE)
DATAFLOW = pltpu.SideEffectType.DATAFLOW_SIDE_EFFECTING


def _exchange_copies(srcs, lands, send_sems, recv_sems, src_at, dst_at):
    x, y, c = _me()
    me = 4 * x + 2 * y + c
    copies = []
    for w in range(len(srcs)):
        for k in range(1, N_DEV):
            px, py, pc = _peer(k, x, y, c)
            copies.append(pltpu.make_async_remote_copy(
                src_ref=src_at(w, srcs[w], 4 * px + 2 * py + pc), dst_ref=dst_at(w, lands[w], me, k),
                send_sem=send_sems.at[w * (N_DEV - 1) + k - 1], recv_sem=recv_sems.at[w * (N_DEV - 1) + k - 1],
                device_id=(px, py, pc), device_id_type=MESH))
    return copies


def _own_copies(srcs, lands, own_sems, src_at, dst_at):
    x, y, c = _me()
    me = 4 * x + 2 * y + c
    return [pltpu.make_async_copy(src_at(w, srcs[w], me), dst_at(w, lands[w], me, 0), own_sems.at[w])
            for w in range(len(srcs))]


def _exchange_start(name, srcs, lands, src_at, dst_at, after, own=False):
    nw = len(srcs)

    def body(*refs):
        src_refs, land_refs = refs[:nw], refs[nw:2 * nw]
        send_sems, recv_sems, own_sems = refs[2 * nw + 1], refs[2 * nw + 2], refs[2 * nw + 3]
        token = refs[-1]
        for cp in _exchange_copies(src_refs, land_refs, send_sems, recv_sems, src_at, dst_at):
            cp.start()
        if own:
            for cp in _own_copies(src_refs, land_refs, own_sems, src_at, dst_at):
                cp.start()
        token[...] = jnp.zeros_like(token)

    hbm = lambda a: pltpu.HBM(a.shape, a.dtype)
    outs = pl.pallas_call(
        body, name=name,
        out_shape=(pltpu.SemaphoreType.DMA((nw * (N_DEV - 1),)), pltpu.SemaphoreType.DMA((nw * (N_DEV - 1),)),
                   pltpu.SemaphoreType.DMA((nw,)), *[hbm(a) for a in srcs], *[hbm(a) for a in lands],
                   jax.ShapeDtypeStruct((8, 128), F32)),
        in_specs=[HBM_SPEC] * (2 * nw) + [pl.BlockSpec(memory_space=pl.ANY)],
        out_specs=(SEM_SPEC, SEM_SPEC, SEM_SPEC, *[HBM_SPEC] * (2 * nw), pl.BlockSpec(memory_space=pltpu.VMEM)),
        input_output_aliases={i: 3 + i for i in range(2 * nw)},
        compiler_params=pltpu.CompilerParams(has_side_effects=DATAFLOW),
    )(*[pltpu.with_memory_space_constraint(a, pltpu.HBM) for a in list(srcs) + list(lands)], after)
    return dict(send=outs[0], recv=outs[1], own_sems=outs[2], srcs=outs[3:3 + nw], lands=outs[3 + nw:3 + 2 * nw],
                token=outs[-1], src_at=src_at, dst_at=dst_at, name=name, own=own)


def _exchange_wait(pending, after):
    nw = len(pending["srcs"])

    def body(*refs):
        src_refs, land_refs = refs[:nw], refs[nw:2 * nw]
        send_sems, recv_sems, own_sems = refs[2 * nw], refs[2 * nw + 1], refs[2 * nw + 2]
        for cp in _exchange_copies(src_refs, land_refs, send_sems, recv_sems,
                                   pending["src_at"], pending["dst_at"]):
            cp.wait_send()
            cp.wait_recv()
        if pending["own"]:
            for cp in _own_copies(src_refs, land_refs, own_sems, pending["src_at"], pending["dst_at"]):
                cp.wait()

    hbm = lambda a: pltpu.HBM(a.shape, a.dtype)
    outs = pl.pallas_call(
        body, name=pending["name"] + "_wait",
        out_shape=(*[hbm(a) for a in pending["srcs"]], *[hbm(a) for a in pending["lands"]]),
        in_specs=[HBM_SPEC] * (2 * nw) + [SEM_SPEC, SEM_SPEC, SEM_SPEC, pl.BlockSpec(memory_space=pl.ANY)],
        out_specs=tuple([HBM_SPEC] * (2 * nw)),
        input_output_aliases={i: i for i in range(2 * nw)},
        compiler_params=pltpu.CompilerParams(has_side_effects=DATAFLOW),
    )(*pending["srcs"], *pending["lands"], pending["send"], pending["recv"], pending["own_sems"], after)
    return outs[nw:]


def _w_in_scatter_copies(src, land, send_sems, recv_sems, early):
    rows, cols = land.shape[1], land.shape[2]
    bound = EARLY_SEC * rows
    cut_dev = bound // cols
    cut = bound - cut_dev * cols
    x, y, c = _me()
    me = 4 * x + 2 * y + c

    def pieces(t):
        if early:
            return [(t > cut_dev, t * cols - bound, cols, 0), (t == cut_dev, 0, cols - cut, cut)]
        return [(t < cut_dev, t * cols, cols, 0), (t == cut_dev, cut_dev * cols, cut, 0)]

    out = []
    for k in range(1, N_DEV):
        px, py, pc = _peer(k, x, y, c)
        for (to_peer, s0, width, d0), (to_me, _, _, _) in zip(pieces(4 * px + 2 * py + pc), pieces(me)):
            s0 = s0 if isinstance(s0, int) else pl.multiple_of(jnp.maximum(s0, 0), 128)
            out.append((to_peer, to_me, pltpu.make_async_remote_copy(
                src_ref=src.at[:, pl.ds(s0, width)], dst_ref=land.at[k - 1, :, pl.ds(d0, width)],
                send_sem=send_sems.at[k - 1], recv_sem=recv_sems.at[k - 1],
                device_id=(px, py, pc), device_id_type=MESH)))
    return out


def _w_in_scatter_start(name, src, land, early, after):
    def body(src_ref, land_ref, after_ref, send_sems, recv_sems, src_thru, land_thru, token):
        for to_peer, _, cp in _w_in_scatter_copies(src_ref, land_ref, send_sems, recv_sems, early):
            pl.when(to_peer)(cp.start)
        token[...] = jnp.zeros_like(token)

    hbm = lambda a: pltpu.HBM(a.shape, a.dtype)
    outs = pl.pallas_call(
        body, name=name,
        out_shape=(pltpu.SemaphoreType.DMA((N_DEV - 1,)), pltpu.SemaphoreType.DMA((N_DEV - 1,)),
                   hbm(src), hbm(land), jax.ShapeDtypeStruct((8, 128), F32)),
        in_specs=[HBM_SPEC, HBM_SPEC, pl.BlockSpec(memory_space=pl.ANY)],
        out_specs=(SEM_SPEC, SEM_SPEC, HBM_SPEC, HBM_SPEC, pl.BlockSpec(memory_space=pltpu.VMEM)),
        input_output_aliases={0: 2, 1: 3},
        compiler_params=pltpu.CompilerParams(has_side_effects=DATAFLOW),
    )(pltpu.with_memory_space_constraint(src, pltpu.HBM), pltpu.with_memory_space_constraint(land, pltpu.HBM), after)
    return dict(send=outs[0], recv=outs[1], src=outs[2], land=outs[3], token=outs[4], early=early, name=name)


def _w_in_scatter_wait(pending, land, after):
    def body(src_ref, land_ref, send_sems, recv_sems, after_ref, src_dead, land_out):
        for to_peer, to_me, cp in _w_in_scatter_copies(src_ref, land_ref, send_sems, recv_sems, pending["early"]):
            pl.when(to_peer)(cp.wait_send)
            pl.when(to_me)(cp.wait_recv)

    hbm = lambda a: pltpu.HBM(a.shape, a.dtype)
    outs = pl.pallas_call(
        body, name=pending["name"] + "_wait", out_shape=(hbm(pending["src"]), hbm(land)),
        in_specs=[HBM_SPEC, HBM_SPEC, SEM_SPEC, SEM_SPEC, pl.BlockSpec(memory_space=pl.ANY)],
        out_specs=(HBM_SPEC, HBM_SPEC), input_output_aliases={0: 0, 1: 1},
        compiler_params=pltpu.CompilerParams(has_side_effects=DATAFLOW),
    )(pending["src"], land, pending["send"], pending["recv"], after)
    return outs[1]


def _call_after(dep, body, args, *, in_specs, **kw):
    n_in = len(args)

    def wrapped(*refs):
        body(*refs[:n_in], *refs[n_in + 1:])

    dep_spec = pl.BlockSpec(dep.shape, lambda *_: (0,) * dep.ndim)
    return pl.pallas_call(wrapped, in_specs=list(in_specs) + [dep_spec], **kw)(*args, dep)


def _resident(shape):
    return pl.BlockSpec(shape, lambda *_: (0,) * len(shape), pipeline_mode=pl.Buffered(1))


def _proj(x2, w_in, dep):
    T, D = x2.shape
    tm = min(ROW_TILE, T)

    def body(x_ref, w_ref, o_ref, xt_ref):
        x = x_ref[...]
        xt_ref[...] = x.T.astype(BF16)
        xb = x.astype(BF16)
        for j in range(N_SEC):
            o_ref[j] = jnp.dot(xb, w_ref[:, j * D:(j + 1) * D], preferred_element_type=F32)

    return _call_after(
        dep, body, (x2, w_in), name="proj", grid=(T // tm,),
        in_specs=[pl.BlockSpec((tm, D), lambda i: (i, 0)), _resident((D, N_SEC * D))],
        out_specs=[pl.BlockSpec((N_SEC, tm, D), lambda i: (0, i, 0)), pl.BlockSpec((D, tm), lambda i: (0, i))],
        out_shape=[jax.ShapeDtypeStruct((N_SEC, T, D), F32), jax.ShapeDtypeStruct((D, T), BF16)],
        compiler_params=_params(("parallel",)))


def _chunk_cumsum(v, reverse=False):
    rows, lanes = v.shape
    x = v.reshape(rows // SUBLANES, SUBLANES, lanes)
    pos = lax.broadcasted_iota(jnp.int32, x.shape, 1)
    for sh in (1, 2, 4):
        if reverse:
            x = x + jnp.where(pos < SUBLANES - sh, pltpu.roll(x, SUBLANES - sh, 1), 0.0)
        else:
            x = x + jnp.where(pos >= sh, pltpu.roll(x, sh, 1), 0.0)
    x = x.reshape(rows // CHUNK, CHUNK // SUBLANES, SUBLANES, lanes)
    half = lax.broadcasted_iota(jnp.int32, x.shape, 1)
    if reverse:
        x = x + jnp.where(half == 0, x[:, 1:2, 0:1, :], 0.0)
    else:
        x = x + jnp.where(half == 1, x[:, 0:1, SUBLANES - 1:SUBLANES, :], 0.0)
    return x.reshape(rows, lanes)


def _hgrn_gates(q, f_pre, lb_logits):
    l0, l1 = lb_logits[0:1, :], lb_logits[1:2, :]
    mx = jnp.maximum(l0, l1)
    e0, e1 = jnp.exp(l0 - mx), jnp.exp(l1 - mx)
    lb = e0 / (e0 + e1)
    sq = _sigmoid(q)
    qf = q * sq * Q_SCALE
    sg = _sigmoid(f_pre)
    f = lb + (1.0 - lb) * sg
    k = 1.0 - f
    log_f = jnp.log(f)
    G = _chunk_cumsum(log_f)
    g_to_end = _chunk_cumsum(log_f, reverse=True) - log_f
    e_g = jnp.exp(G)
    e_ng = jnp.exp(-G)
    e_ge = jnp.exp(g_to_end)
    return dict(lb=lb, sq=sq, qf=qf, sg=sg, f=f, k=k, G=G, e_g=e_g, e_ng=e_ng, e_ge=e_ge,
                qd=qf * e_g, ki=k * e_ng, ke=k * e_ge, dec=e_g * e_ge)


def _intra_mask():
    r = lax.broadcasted_iota(jnp.int32, (GROUP, GROUP), 0)
    c = lax.broadcasted_iota(jnp.int32, (GROUP, GROUP), 1)
    return (r // CHUNK == c // CHUNK) & (c <= r)


def _chunk_outer(lhs_rows, rhs_b, out_scr, sb):
    lane = lax.broadcasted_iota(jnp.int32, (GROUP, GROUP), 1) // CHUNK
    for g in range(sb // GROUP):
        sl = slice(g * GROUP, (g + 1) * GROUP)
        lhs_t = lhs_rows[sl].T
        for cc in range(CH_PER_GROUP):
            masked = jnp.where(lane == cc, lhs_t, 0.0).astype(BF16)
            out_scr[g * CH_PER_GROUP + cc] = jnp.dot(masked, rhs_b[sl], preferred_element_type=F32)


def _hgrn_forward_blocks(cs, vs, st0s, sb, o_scr, kv_scr, st_scr, dec_scr):
    nc = sb // CHUNK
    n_str = len(cs)
    mask = _intra_mask()
    bf = []
    for i, (c, v) in enumerate(zip(cs, vs)):
        qd_b, ki_b, ke_b, v_b = (c["qd"].astype(BF16), c["ki"].astype(BF16), c["ke"].astype(BF16),
                                 v.astype(BF16))
        bf.append((qd_b, ki_b, ke_b, v_b))
        for g in range(sb // GROUP):
            sl = slice(g * GROUP, (g + 1) * GROUP)
            sc = lax.dot_general(qd_b[sl], ki_b[sl], NT_DIMS, preferred_element_type=F32)
            a = jnp.where(mask, sc, 0.0).astype(BF16)
            o_scr[i, sl, :] = jnp.dot(a, v_b[sl], preferred_element_type=F32)
        _chunk_outer(v, ke_b, kv_scr.at[i], sb)
        dec_scr[i] = c["dec"]

    def rec(n, sts):
        row = pl.ds(pl.multiple_of(n * CHUNK, CHUNK), 1)
        out = []
        for i in range(n_str):
            st_scr[i, n] = sts[i]
            out.append(sts[i] * dec_scr[i, row, :] + kv_scr[i, n])
        return tuple(out)

    ends = lax.fori_loop(0, nc, rec, tuple(st0s))

    for n in range(nc):
        rows = slice(n * CHUNK, (n + 1) * CHUNK)
        for i in range(n_str):
            o_scr[i, rows, :] += lax.dot_general(bf[i][0][rows], st_scr[i, n].astype(BF16), NT_DIMS,
                                                 preferred_element_type=F32)
    return ends, bf


def _hgrn_fwd(proj5, lb_logits, gn):
    _, Bl, S, D = proj5.shape
    H = D // HEAD
    sb = min(SUB_BLOCK, S)
    nsb = S // sb
    nc = sb // CHUNK

    def body(p_ref, lbl_ref, gn_ref, ain_ref, aint_ref, o_ref, st_ref, carry, o_scr, kv_scr, st_scr, dec_scr):
        @pl.when(pl.program_id(1) == 0)
        def _():
            carry[...] = jnp.zeros_like(carry)

        st0s = [carry[b] for b in range(Bl)]
        cs = [_hgrn_gates(p_ref[0, b], p_ref[1, b], lbl_ref[...]) for b in range(Bl)]
        ends, _ = _hgrn_forward_blocks(cs, [p_ref[2, b] for b in range(Bl)], st0s, sb,
                                       o_scr, kv_scr, st_scr, dec_scr)
        for b in range(Bl):
            carry[b] = ends[b]
            st_ref[b, 0] = st_scr[b].astype(BF16)
            o = o_scr[b]
            o_ref[b] = o
            rinv = lax.rsqrt(jnp.mean(o * o, axis=-1, keepdims=True) + RMS_EPS)
            ain = o * rinv * gn_ref[...] * _sigmoid(p_ref[3, b])
            ain_ref[b] = ain.astype(BF16)
            aint_ref[b] = ain.T.astype(BF16)

    return pl.pallas_call(
        body, name="hgrn_fwd", grid=(H, nsb),
        in_specs=[pl.BlockSpec((4, Bl, sb, HEAD), lambda h, s: (0, 0, s, h)),
                  pl.BlockSpec((2, HEAD), lambda h, s: (0, h)),
                  pl.BlockSpec((1, HEAD), lambda h, s: (0, h))],
        out_specs=[pl.BlockSpec((Bl, sb, HEAD), lambda h, s: (0, s, h)),
                   pl.BlockSpec((Bl, HEAD, sb), lambda h, s: (0, h, s)),
                   pl.BlockSpec((Bl, sb, HEAD), lambda h, s: (0, s, h)),
                   pl.BlockSpec((Bl, 1, nc, HEAD, HEAD), lambda h, s: (0, h, s, 0, 0))],
        out_shape=[jax.ShapeDtypeStruct((Bl, S, D), BF16), jax.ShapeDtypeStruct((Bl, D, S), BF16),
                   jax.ShapeDtypeStruct((Bl, S, D), F32),
                   jax.ShapeDtypeStruct((Bl, H, S // CHUNK, HEAD, HEAD), BF16)],
        scratch_shapes=[pltpu.VMEM((Bl, HEAD, HEAD), F32), pltpu.VMEM((Bl, sb, HEAD), F32),
                        pltpu.VMEM((Bl, nc, HEAD, HEAD), F32), pltpu.VMEM((Bl, nc, HEAD, HEAD), F32),
                        pltpu.VMEM((Bl, sb, HEAD), F32)],
        compiler_params=_params(("parallel", "arbitrary")),
    )(proj5, lb_logits, gn)


def _window_count(shape, g):
    pos = lax.broadcasted_iota(jnp.int32, shape, 0)
    return pos, jnp.minimum(pos + 1, jnp.left_shift(2, g)).astype(F32)


def _select_window(g, sums):
    return jnp.where(g == 0, sums[0], jnp.where(g == 1, sums[1], jnp.where(g == 2, sums[2], sums[3])))


def _pool_fwd(proj5, w_pool):
    _, Bl, S, D = proj5.shape
    pg = D // POOL_GROUPS

    def body(v_ref, w_ref, pooled_t_ref, bp_ref):
        g = pl.program_id(1)
        v = v_ref[0, 0]
        pos, cnt = _window_count(v.shape, g)
        cur, sums = v, []
        for sh in (1, 2, 4, 8):
            cur = cur + jnp.where(pos >= sh, pltpu.roll(cur, sh, 0), 0.0)
            sums.append(cur)
        pooled = _select_window(g, sums) / cnt - v
        pooled_t_ref[...] = pooled.T.astype(BF16)
        bp_ref[0] = jnp.dot(pooled.astype(BF16), w_ref[0], preferred_element_type=F32)

    return pl.pallas_call(
        body, name="pool_fwd", grid=(Bl, POOL_GROUPS),
        in_specs=[pl.BlockSpec((1, 1, S, pg), lambda b, g: (4, b, 0, g)),
                  pl.BlockSpec((1, pg, pg), lambda b, g: (g, 0, 0))],
        out_specs=[pl.BlockSpec((pg, S), lambda b, g: (g, b)),
                   pl.BlockSpec((1, S, pg), lambda b, g: (b, 0, g))],
        out_shape=[jax.ShapeDtypeStruct((D, Bl * S), BF16), jax.ShapeDtypeStruct((Bl, S, D), F32)],
        compiler_params=_params(("parallel", "parallel")),
    )(proj5, w_pool)


def _layer_norm_fwd(r):
    mu = jnp.mean(r, axis=-1, keepdims=True)
    d = r - mu
    rs = lax.rsqrt(jnp.mean(d * d, axis=-1, keepdims=True) + LN_EPS)
    return d * rs, rs


def _layer_norm_bwd(dy_g, xhat, rs):
    return rs * (dy_g - jnp.mean(dy_g, axis=-1, keepdims=True)
                 - xhat * jnp.mean(dy_g * xhat, axis=-1, keepdims=True))


def _mix_fwd(ain, proj, bp, x2, w_a, w_out, ps, g1, b1):
    T, D = x2.shape
    tm = min(ROW_TILE, T)

    def body(ain_ref, ga_ref, gb_ref, bp_ref, x_ref, wa_ref, wo_ref, ps_ref, g1_ref, b1_ref,
             a_ref, mgt_ref, xh_ref, rs_ref, x1b_ref, x1t_ref):
        a = jnp.dot(ain_ref[...], wa_ref[...], preferred_element_type=F32)
        a_ref[...] = a
        merged = _sigmoid(ga_ref[0]) * a + _sigmoid(gb_ref[0]) * (bp_ref[...] * ps_ref[...])
        mgt_ref[...] = merged.T.astype(BF16)
        r1 = ALPHA * x_ref[...] + jnp.dot(merged.astype(BF16), wo_ref[...], preferred_element_type=F32)
        xhat, rs = _layer_norm_fwd(r1)
        xh_ref[...] = xhat
        rs_ref[...] = rs
        x1 = xhat * g1_ref[...] + b1_ref[...]
        x1b_ref[...] = x1.astype(BF16)
        x1t_ref[...] = x1.T.astype(BF16)

    row = lambda i: (i, 0)
    col = lambda i: (0, i)
    full = lambda i: (0, 0)
    return pl.pallas_call(
        body, name="mix_fwd", grid=(T // tm,),
        in_specs=[pl.BlockSpec((tm, D), row),
                  pl.BlockSpec((1, tm, D), lambda i: (5, i, 0)),
                  pl.BlockSpec((1, tm, D), lambda i: (6, i, 0)),
                  pl.BlockSpec((tm, D), row), pl.BlockSpec((tm, D), row),
                  pl.BlockSpec((D, D), full), pl.BlockSpec((D, D), full),
                  pl.BlockSpec((1, D), full), pl.BlockSpec((1, D), full), pl.BlockSpec((1, D), full)],
        out_specs=[pl.BlockSpec((tm, D), row), pl.BlockSpec((D, tm), col), pl.BlockSpec((tm, D), row),
                   pl.BlockSpec((tm, 1), row), pl.BlockSpec((tm, D), row), pl.BlockSpec((D, tm), col)],
        out_shape=[jax.ShapeDtypeStruct((T, D), F32), jax.ShapeDtypeStruct((D, T), BF16),
                   jax.ShapeDtypeStruct((T, D), F32), jax.ShapeDtypeStruct((T, 1), F32),
                   jax.ShapeDtypeStruct((T, D), BF16), jax.ShapeDtypeStruct((D, T), BF16)],
        compiler_params=_params(("parallel",)),
    )(ain, proj, proj, bp, x2, w_a, w_out, ps, g1, b1)


def _mlp_fwd(x1b, w_up, w_down, xhat1, tgt, g1, b1, g2, b2):
    T, D = xhat1.shape
    FF = w_up.shape[1]
    tm = min(MLP_ROW_TILE, T)

    def body(x_ref, wu_ref, wd_ref, xh_ref, t_ref, g1_ref, b1_ref, g2_ref, b2_ref,
             hp_ref, h_ref, dr_ref, drb_ref, drt_ref, vec_ref):
        @pl.when(pl.program_id(0) == 0)
        def _():
            vec_ref[...] = jnp.zeros_like(vec_ref)

        xb = x_ref[...]
        x1 = xh_ref[...] * g1_ref[...] + b1_ref[...]
        r2 = ALPHA * x1
        for f in range(FF // D):
            cols = slice(f * D, (f + 1) * D)
            hp = jnp.dot(xb, wu_ref[:, cols], preferred_element_type=F32)
            hp_ref[:, cols] = hp
            h = jnp.square(jnp.maximum(hp, 0.0)).astype(BF16)
            h_ref[:, cols] = h
            r2 = r2 + jnp.dot(h, wd_ref[cols, :], preferred_element_type=F32)
        xhat2, rs2 = _layer_norm_fwd(r2)
        err = xhat2 * g2_ref[...] + b2_ref[...] - t_ref[...]
        dy = err / D
        vec_ref[5:6, :] += jnp.sum(dy * xhat2, axis=0, keepdims=True)
        vec_ref[6:7, :] += jnp.sum(dy, axis=0, keepdims=True)
        vec_ref[7:8, :] += jnp.sum(0.5 * err * err / D, axis=0, keepdims=True)
        dr = _layer_norm_bwd(dy * g2_ref[...], xhat2, rs2)
        dr_ref[...] = dr
        drb_ref[...] = dr.astype(BF16)
        drt_ref[...] = dr.T.astype(BF16)

    row = lambda i: (i, 0)
    full = lambda i: (0, 0)
    return pl.pallas_call(
        body, name="mlp_fwd", grid=(T // tm,),
        in_specs=[pl.BlockSpec((tm, D), row), _resident((D, FF)), _resident((FF, D)),
                  pl.BlockSpec((tm, D), row), pl.BlockSpec((tm, D), row),
                  pl.BlockSpec((1, D), full), pl.BlockSpec((1, D), full),
                  pl.BlockSpec((1, D), full), pl.BlockSpec((1, D), full)],
        out_specs=[pl.BlockSpec((tm, FF), row), pl.BlockSpec((tm, FF), row), pl.BlockSpec((tm, D), row),
                   pl.BlockSpec((tm, D), row), pl.BlockSpec((D, tm), lambda i: (0, i)),
                   pl.BlockSpec((8, D), full)],
        out_shape=[jax.ShapeDtypeStruct((T, FF), F32), jax.ShapeDtypeStruct((T, FF), BF16),
                   jax.ShapeDtypeStruct((T, D), F32), jax.ShapeDtypeStruct((T, D), BF16),
                   jax.ShapeDtypeStruct((D, T), BF16), jax.ShapeDtypeStruct((8, D), F32)],
        compiler_params=_params(("arbitrary",)),
    )(x1b, w_up, w_down, xhat1, tgt, g1, b1, g2, b2)


def _mlp_bwd(drb, dr, hp, w_up, w_down, xhat1, rs1, g1):
    T, D = dr.shape
    FF = hp.shape[1]
    tm = min(MLP_ROW_TILE, T)

    def body(drb_ref, dr_ref, hp_ref, wu_ref, wd_ref, xh_ref, rs_ref, g1_ref,
             dhp_ref, d1_ref, d1b_ref, vec_ref):
        @pl.when(pl.program_id(0) == 0)
        def _():
            vec_ref[...] = jnp.zeros_like(vec_ref)

        drb = drb_ref[...]
        dx1 = ALPHA * dr_ref[...]
        for f in range(FF // D):
            cols = slice(f * D, (f + 1) * D)
            dh = lax.dot_general(drb, wd_ref[cols, :], NT_DIMS, preferred_element_type=F32)
            dhp = (dh * (2.0 * jnp.maximum(hp_ref[:, cols], 0.0))).astype(BF16)
            dhp_ref[:, cols] = dhp
            dx1 = dx1 + lax.dot_general(dhp, wu_ref[:, cols], NT_DIMS, preferred_element_type=F32)
        xhat = xh_ref[...]
        vec_ref[3:4, :] += jnp.sum(dx1 * xhat, axis=0, keepdims=True)
        vec_ref[4:5, :] += jnp.sum(dx1, axis=0, keepdims=True)
        d1 = _layer_norm_bwd(dx1 * g1_ref[...], xhat, rs_ref[...])
        d1_ref[...] = d1
        d1b_ref[...] = d1.astype(BF16)

    row = lambda i: (i, 0)
    full = lambda i: (0, 0)
    return pl.pallas_call(
        body, name="mlp_bwd", grid=(T // tm,),
        in_specs=[pl.BlockSpec((tm, D), row), pl.BlockSpec((tm, D), row), pl.BlockSpec((tm, FF), row),
                  _resident((D, FF)), _resident((FF, D)),
                  pl.BlockSpec((tm, D), row), pl.BlockSpec((tm, 1), row), pl.BlockSpec((1, D), full)],
        out_specs=[pl.BlockSpec((tm, FF), row), pl.BlockSpec((tm, D), row), pl.BlockSpec((tm, D), row),
                   pl.BlockSpec((8, D), full)],
        out_shape=[jax.ShapeDtypeStruct((T, FF), BF16), jax.ShapeDtypeStruct((T, D), F32),
                   jax.ShapeDtypeStruct((T, D), BF16), jax.ShapeDtypeStruct((8, D), F32)],
        compiler_params=_params(("arbitrary",)),
    )(drb, dr, hp, w_up, w_down, xhat1, rs1, g1)


def _dw(name, a_t, b, n_j, a_spec, b_spec, o_shape, o_block, o_map, transpose_out=False, dep=None,
        into=(None, None), ob_shape=None, ob_map=None):
    def body(*refs):
        a_ref, b_ref, o_ref, ob_ref = refs[0], refs[1], refs[-2], refs[-1]
        b_val = b_ref[0] if len(b_ref.shape) == 3 else b_ref[...]
        if len(a_ref.shape) == 3:
            seq = a_ref.shape[2]
            p = sum(jnp.dot(a_ref[i], b_val[i * seq:(i + 1) * seq], preferred_element_type=F32)
                    for i in range(a_ref.shape[0]))
        else:
            p = jnp.dot(a_ref[...], b_val, preferred_element_type=F32)
        if transpose_out:
            p = p.T
        p = p.reshape(o_ref.shape)
        o_ref[...] = p
        ob_ref[...] = p.astype(BF16)

    kw = dict(name=name, grid=(n_j,), in_specs=[a_spec, b_spec],
              out_specs=[pl.BlockSpec(o_block, o_map), pl.BlockSpec(o_block, ob_map or o_map)],
              out_shape=[jax.ShapeDtypeStruct(o_shape, F32), jax.ShapeDtypeStruct(ob_shape or o_shape, BF16)],
              compiler_params=_params(("parallel",)))
    args = (a_t, b)
    aliases = {}
    for out_index, arr in enumerate(into):
        if arr is not None:
            aliases[len(args)] = out_index
            args = args + (arr,)
            kw["in_specs"] = kw["in_specs"] + [pl.BlockSpec(memory_space=pl.ANY)]
    if aliases:
        kw["input_output_aliases"] = aliases
    if dep is None:
        return pl.pallas_call(body, **kw)(*args)
    return _call_after(dep, body, args, **kw)


def _mix_bwd(d1b, proj, a, bp, w_a, w_out, w_pool, ps, dep):
    T, D = a.shape
    tm = min(ROW_TILE, T)
    pg = D // POOL_GROUPS

    def body(d1b_ref, ga_ref, gb_ref, a_ref, bp_ref, wa_ref, wo_ref, wp_ref, ps_ref,
             da_ref, dbp_ref, dain_ref, dpl_ref, dg_ref, vec_ref):
        @pl.when(pl.program_id(0) == 0)
        def _():
            vec_ref[...] = jnp.zeros_like(vec_ref)

        dm = lax.dot_general(d1b_ref[...], wo_ref[...], NT_DIMS, preferred_element_type=F32)
        sa, sg = _sigmoid(ga_ref[0]), _sigmoid(gb_ref[0])
        bp_v, ps_v = bp_ref[...], ps_ref[...]
        da = (dm * sa).astype(BF16)
        db = dm * sg
        dg_ref[0] = (dm * a_ref[...] * sa * (1.0 - sa)).astype(BF16)
        dg_ref[1] = (dm * (bp_v * ps_v) * sg * (1.0 - sg)).astype(BF16)
        vec_ref[2:3, :] += jnp.sum(db * bp_v, axis=0, keepdims=True)
        dbp = (db * ps_v).astype(BF16)
        da_ref[...] = da
        dbp_ref[...] = dbp
        dain_ref[...] = lax.dot_general(da, wa_ref[...], NT_DIMS, preferred_element_type=F32)
        for g in range(POOL_GROUPS):
            cols = slice(g * pg, (g + 1) * pg)
            dpl_ref[:, cols] = lax.dot_general(dbp[:, cols], wp_ref[g], NT_DIMS,
                                               preferred_element_type=F32)

    row = lambda i: (i, 0)
    full = lambda i: (0, 0)
    return _call_after(
        dep, body, (d1b, proj, proj, a, bp, w_a, w_out, w_pool, ps), name="mix_bwd", grid=(T // tm,),
        in_specs=[pl.BlockSpec((tm, D), row),
                  pl.BlockSpec((1, tm, D), lambda i: (5, i, 0)),
                  pl.BlockSpec((1, tm, D), lambda i: (6, i, 0)),
                  pl.BlockSpec((tm, D), row), pl.BlockSpec((tm, D), row),
                  pl.BlockSpec((D, D), full), pl.BlockSpec((D, D), full),
                  pl.BlockSpec((POOL_GROUPS, pg, pg), lambda i: (0, 0, 0)),
                  pl.BlockSpec((1, D), full)],
        out_specs=[pl.BlockSpec((tm, D), row), pl.BlockSpec((tm, D), row),
                   pl.BlockSpec((tm, D), row), pl.BlockSpec((tm, D), row),
                   pl.BlockSpec((2, tm, D), lambda i: (0, i, 0)),
                   pl.BlockSpec((8, D), full)],
        out_shape=[jax.ShapeDtypeStruct((T, D), BF16), jax.ShapeDtypeStruct((T, D), BF16),
                   jax.ShapeDtypeStruct((T, D), F32), jax.ShapeDtypeStruct((T, D), F32),
                   jax.ShapeDtypeStruct((2, T, D), BF16), jax.ShapeDtypeStruct((8, D), F32)],
        compiler_params=_params(("arbitrary",)))


def _pool_bwd(dpooled3, dep):
    Bl, S, D = dpooled3.shape
    pg = D // POOL_GROUPS

    def body(dp_ref, dv_ref):
        g = pl.program_id(1)
        dp = dp_ref[0]
        pos, cnt = _window_count(dp.shape, g)
        cur, sums = dp / cnt, []
        for sh in (1, 2, 4, 8):
            cur = cur + jnp.where(pos < S - sh, pltpu.roll(cur, S - sh, 0), 0.0)
            sums.append(cur)
        dv_ref[0] = (_select_window(g, sums) - dp).astype(BF16)

    spec = pl.BlockSpec((1, S, pg), lambda b, g: (b, 0, g))
    return _call_after(
        dep, body, (dpooled3,), name="pool_bwd", grid=(Bl, POOL_GROUPS), in_specs=[spec], out_specs=spec,
        out_shape=jax.ShapeDtypeStruct((Bl, S, D), BF16),
        compiler_params=_params(("parallel", "parallel")))


def _hgrn_bwd(proj5, lb_logits, gn, dain3, o3, st_all, dep):
    _, Bl, S, D = proj5.shape
    H = D // HEAD
    sb = min(SUB_BLOCK, S)
    nsb = S // sb
    nc = sb // CHUNK
    streams = range(Bl)

    def body(p_ref, lbl_ref, gn_ref, dain_ref, o_ref, st_ref, d_ref, vec_ref,
             dcarry, kv_scr, dst_scr, dec_scr, dvi_scr, dke_scr, dqi_scr):
        s = pl.program_id(1)

        @pl.when(s == 0)
        def _():
            dcarry[...] = jnp.zeros_like(dcarry)
            vec_ref[...] = jnp.zeros_like(vec_ref)

        qs, vs, ogs = [p_ref[0, b] for b in streams], [p_ref[2, b] for b in streams], [p_ref[3, b] for b in streams]
        cs = [_hgrn_gates(qs[b], p_ref[1, b], lbl_ref[...]) for b in streams]
        bf = [(cs[b]["qd"].astype(BF16), cs[b]["ki"].astype(BF16), cs[b]["ke"].astype(BF16),
               vs[b].astype(BF16)) for b in streams]
        mask = _intra_mask()
        gn_v = gn_ref[...]
        keep = []
        for b in streams:
            qd_b, ki_b, ke_b, v_b = bf[b]
            dec_scr[b] = cs[b]["dec"]
            o = o_ref[b]
            rinv = lax.rsqrt(jnp.mean(o * o, axis=-1, keepdims=True) + RMS_EPS)
            on = o * rinv
            so = _sigmoid(ogs[b])
            dain = dain_ref[b]
            vec_ref[1:2, :] += jnp.sum(dain * on * so, axis=0, keepdims=True)
            d_og = dain * on * gn_v * so * (1.0 - so)
            d_on = dain * gn_v * so
            do = rinv * (d_on - on * jnp.mean(d_on * on, axis=-1, keepdims=True))
            do_b = do.astype(BF16)
            dv_parts, dqd_parts, dki_parts = [], [], []
            for g in range(sb // GROUP):
                sl = slice(g * GROUP, (g + 1) * GROUP)
                sc = lax.dot_general(qd_b[sl], ki_b[sl], NT_DIMS, preferred_element_type=F32)
                a = jnp.where(mask, sc, 0.0).astype(BF16)
                da = lax.dot_general(do_b[sl], v_b[sl], NT_DIMS, preferred_element_type=F32)
                da = jnp.where(mask, da, 0.0).astype(BF16)
                dv_parts.append(lax.dot_general(a, do_b[sl], TN_DIMS, preferred_element_type=F32))
                dqd_parts.append(jnp.dot(da, ki_b[sl], preferred_element_type=F32))
                dki_parts.append(lax.dot_general(da, qd_b[sl], TN_DIMS, preferred_element_type=F32))
            keep.append(dict(d_og=d_og, do_b=do_b, dv_intra=jnp.concatenate(dv_parts, axis=0),
                             dqd_intra=jnp.concatenate(dqd_parts, axis=0),
                             dki=jnp.concatenate(dki_parts, axis=0)))
            _chunk_outer(do, qd_b, kv_scr.at[b], sb)

        def rrec(i, dsts):
            n = nc - 1 - i
            row = pl.ds(pl.multiple_of(n * CHUNK, CHUNK), 1)
            out = []
            for b in streams:
                dst_scr[b, n] = dsts[b]
                out.append(dsts[b] * dec_scr[b, row, :] + kv_scr[b, n])
            return tuple(out)

        ends = lax.fori_loop(0, nc, rrec, tuple(dcarry[b] for b in streams))
        for b in streams:
            dcarry[b] = ends[b]
        for n in range(nc):
            rows = slice(n * CHUNK, (n + 1) * CHUNK)
            for b in streams:
                qd_b, ki_b, ke_b, v_b = bf[b]
                dst_b = dst_scr[b, n].astype(BF16)
                dvi_scr[b, rows, :] = lax.dot_general(ke_b[rows], dst_b, NT_DIMS, preferred_element_type=F32)
                dke_scr[b, rows, :] = jnp.dot(v_b[rows], dst_b, preferred_element_type=F32)
                dqi_scr[b, rows, :] = jnp.dot(keep[b]["do_b"][rows], st_ref[b, 0, n],
                                              preferred_element_type=F32)
        for b in streams:
            c, k = cs[b], keep[b]
            ddec = jnp.sum(dst_scr[b] * st_ref[b, 0].astype(F32), axis=1)
            dgl = jnp.broadcast_to(ddec[:, None, :], (nc, CHUNK, HEAD)).reshape(sb, HEAD) * c["dec"]
            dqd = k["dqd_intra"] + dqi_scr[b]
            dke = dke_scr[b]
            dki = k["dki"]
            t_ke = dke * c["ke"]
            dG = dqd * c["qd"] - dki * c["ki"] - t_ke
            dgl = dgl + _chunk_cumsum(t_ke) + _chunk_cumsum(t_ke, reverse=True) - t_ke
            dlogf = _chunk_cumsum(dG, reverse=True) + dgl
            dk = dki * c["e_ng"] + dke * c["e_ge"]
            df = dlogf / c["f"] - dk
            sg, sq, lb, q = c["sg"], c["sq"], c["lb"], qs[b]
            vec_ref[0:1, :] += jnp.sum(df * (1.0 - sg), axis=0, keepdims=True)
            d_ref[0, b] = (dqd * c["e_g"] * Q_SCALE * (sq + q * sq * (1.0 - sq))).astype(BF16)
            d_ref[1, b] = (df * (1.0 - lb) * sg * (1.0 - sg)).astype(BF16)
            d_ref[2, b] = (k["dv_intra"] + dvi_scr[b]).astype(BF16)
            d_ref[3, b] = k["d_og"].astype(BF16)

    rev = lambda s: nsb - 1 - s
    big = pltpu.VMEM((Bl, nc, HEAD, HEAD), F32)
    rows_f32 = pltpu.VMEM((Bl, sb, HEAD), F32)
    return _call_after(
        dep, body, (proj5, lb_logits, gn, dain3, o3, st_all), name="hgrn_bwd", grid=(H, nsb),
        in_specs=[pl.BlockSpec((4, Bl, sb, HEAD), lambda h, s: (0, 0, rev(s), h)),
                  pl.BlockSpec((2, HEAD), lambda h, s: (0, h)),
                  pl.BlockSpec((1, HEAD), lambda h, s: (0, h)),
                  pl.BlockSpec((Bl, sb, HEAD), lambda h, s: (0, rev(s), h)),
                  pl.BlockSpec((Bl, sb, HEAD), lambda h, s: (0, rev(s), h)),
                  pl.BlockSpec((Bl, 1, nc, HEAD, HEAD), lambda h, s: (0, h, rev(s), 0, 0))],
        out_specs=[pl.BlockSpec((4, Bl, sb, HEAD), lambda h, s: (0, 0, rev(s), h)),
                   pl.BlockSpec((8, HEAD), lambda h, s: (0, h))],
        out_shape=[jax.ShapeDtypeStruct((4, Bl, S, D), BF16), jax.ShapeDtypeStruct((8, D), F32)],
        scratch_shapes=[pltpu.VMEM((Bl, HEAD, HEAD), F32), big, big, rows_f32, rows_f32, rows_f32, rows_f32],
        compiler_params=_params(("parallel", "arbitrary")))


def _dx(d1, dh4, dpv, dg2, w_in, dep):
    T, D = d1.shape
    tm = min(ROW_TILE, T)

    def body(d1_ref, dh_ref, dp_ref, dg_ref, w_ref, o_ref):
        blocks = [dh_ref[0], dh_ref[1], dh_ref[2], dh_ref[3], dp_ref[...], dg_ref[0], dg_ref[1]]
        acc = ALPHA * d1_ref[...]
        for j, blk in enumerate(blocks):
            acc = acc + lax.dot_general(blk, w_ref[:, j * D:(j + 1) * D], NT_DIMS, preferred_element_type=F32)
        o_ref[...] = acc

    row = lambda i: (i, 0)
    return _call_after(
        dep, body, (d1, dh4, dpv, dg2, w_in), name="dx", grid=(T // tm,),
        in_specs=[pl.BlockSpec((tm, D), row), pl.BlockSpec((4, tm, D), lambda i: (0, i, 0)),
                  pl.BlockSpec((tm, D), row), pl.BlockSpec((2, tm, D), lambda i: (0, i, 0)),
                  _resident((D, N_SEC * D))],
        out_specs=pl.BlockSpec((tm, D), row),
        out_shape=jax.ShapeDtypeStruct((T, D), F32),
        compiler_params=_params(("parallel",)))


def _dw_in_part(name, x_t, b, sections, first_sec, into, dep, ob_shape, ob_first):
    D, T = x_t.shape
    per = D // DW_COLS
    b_spec = (pl.BlockSpec((1, T, DW_COLS), lambda j: (j // per, 0, j % per)) if b.ndim == 3
              else pl.BlockSpec((T, DW_COLS), lambda j: (0, j)))
    return _dw(name, x_t, b, sections * per, _resident((D, T)), b_spec, (D, N_SEC * D), (D, DW_COLS),
               lambda j: (0, first_sec * per + j), dep=dep, into=into, ob_shape=ob_shape,
               ob_map=lambda j: (0, ob_first * per + j))


def _dw_in_early(x_t, dpv, dg2, dep):
    D = x_t.shape[0]
    early_shape = (D, (N_SEC - EARLY_SEC) * D)
    f32, bf = _dw_in_part("dw_in_gates", x_t, dg2, 2, 5, (None, None), dep, early_shape, 1)
    return _dw_in_part("dw_in_pool", x_t, dpv, 1, 4, (f32, bf), None, early_shape, 0)


def _dw_in_late(x_t, dh4, f32_early, dep):
    D = x_t.shape[0]
    return _dw_in_part("dw_in_rec", x_t, dh4, EARLY_SEC, 0, (f32_early, None), dep, (D, EARLY_SEC * D), 0)


def _adam_shard(name, me_arr, grad, land, layout, w, m, v):
    shape = layout.shape
    n_split = 4
    blk = (shape[0] // n_split,) + shape[1:]
    zeros = (0,) * (len(shape) - 1)

    def body(me_ref, g_ref, r_ref, w_ref, m_ref, v_ref, g_out, d_out, m_out, v_out):
        g = g_ref[...]
        for k in range(N_DEV - 1):
            g = g + r_ref[k].astype(F32)
        d, m2, v2 = _adamw(w_ref[...], g, m_ref[...], v_ref[...])
        g_out[...] = g
        d_out[...] = d
        m_out[...] = m2
        v_out[...] = v2

    def own(i, me_ref):
        bi = layout.block_index(me_ref[0])
        return (bi[0] * n_split + i,) + tuple(bi[1:]) if layout.kind == "row" else (i,) + tuple(bi[1:])

    plain = pl.BlockSpec(blk, lambda i, me_ref: (i,) + zeros)
    grid_spec = pltpu.PrefetchScalarGridSpec(
        num_scalar_prefetch=1, grid=(n_split,),
        in_specs=[pl.BlockSpec(blk, own),
                  pl.BlockSpec((N_DEV - 1,) + blk, lambda i, me_ref: (0, i) + zeros),
                  plain, plain, plain],
        out_specs=[plain] * 4)
    return pl.pallas_call(
        body, name=name, grid_spec=grid_spec,
        out_shape=[jax.ShapeDtypeStruct(shape, F32)] * 4,
        compiler_params=_params(("parallel",)),
    )(me_arr, grad, land, w, m, v)


def _vec_allreduce(vec):
    D = vec.shape[1]

    def body(vec_ref, tot_ref, gat, send_sems, recv_sems):
        x, y, c = _me()
        me = 4 * x + 2 * y + c
        gat[me] = vec_ref[...]
        copies = []
        for k in range(1, N_DEV):
            cp = pltpu.make_async_remote_copy(
                src_ref=vec_ref, dst_ref=gat.at[me], send_sem=send_sems.at[k - 1],
                recv_sem=recv_sems.at[k - 1], device_id=_peer(k, x, y, c), device_id_type=MESH)
            cp.start()
            copies.append(cp)
        for cp in copies:
            cp.wait()
        tot = gat[0]
        for d in range(1, N_DEV):
            tot = tot + gat[d]
        tot_ref[...] = tot

    vm = pl.BlockSpec(memory_space=pltpu.VMEM)
    return pl.pallas_call(
        body, name="vec_allreduce", out_shape=jax.ShapeDtypeStruct(vec.shape, F32),
        in_specs=[vm], out_specs=vm,
        scratch_shapes=[pltpu.VMEM((N_DEV, 8, D), F32), pltpu.SemaphoreType.DMA((N_DEV - 1,)),
                        pltpu.SemaphoreType.DMA((N_DEV - 1,))],
    )(vec)


def _vec_adam(tot, small_w, small_m, small_v):
    n = len(small_w)

    def body(*refs):
        tot = refs[0][...]
        ws, ms, vs = refs[1:1 + n], refs[1 + n:1 + 2 * n], refs[1 + 2 * n:1 + 3 * n]
        outs = refs[1 + 3 * n:]
        loss_ref, g_out, d_out = outs[0], outs[1:1 + n], outs[1 + n:1 + 2 * n]
        m_out, v_out = outs[1 + 2 * n:1 + 3 * n], outs[1 + 3 * n:1 + 4 * n]
        loss_ref[...] = jnp.broadcast_to(jnp.sum(tot[7:8, :], axis=1, keepdims=True), loss_ref.shape)
        lbl = ws[0][...]
        mx = jnp.maximum(lbl[0:1, :], lbl[1:2, :])
        e0, e1 = jnp.exp(lbl[0:1, :] - mx), jnp.exp(lbl[1:2, :] - mx)
        p0 = e0 / (e0 + e1)
        dl0 = tot[0:1, :] * p0 * (1.0 - p0)
        grads = [jnp.concatenate([dl0, -dl0], axis=0)] + [tot[r:r + 1, :] for r in range(1, n)]
        for i in range(n):
            d, m2, v2 = _adamw(ws[i][...], grads[i], ms[i][...], vs[i][...])
            g_out[i][...] = grads[i]
            d_out[i][...] = d
            m_out[i][...] = m2
            v_out[i][...] = v2

    vm = pl.BlockSpec(memory_space=pltpu.VMEM)
    shapes = [jax.ShapeDtypeStruct(w.shape, F32) for w in small_w]
    return pl.pallas_call(
        body, name="vec_adam",
        out_shape=[jax.ShapeDtypeStruct((1, 128), F32)] + shapes * 4,
        in_specs=[vm] * (1 + 3 * n), out_specs=[vm] * (1 + 4 * n),
    )(tot, *small_w, *small_m, *small_v)


def kernel(x, w_in, lb_logits, hgrn_norm_g, w_a, w_pool, pool_scale, w_out, ln1_g, ln1_b, w_up, w_down, ln2_g, ln2_b, loss_target, m_w_in, m_lb_logits, m_hgrn_norm_g, m_w_a, m_w_pool, m_pool_scale, m_w_out, m_ln1_g, m_ln1_b, m_w_up, m_w_down, m_ln2_g, m_ln2_b, v_w_in, v_lb_logits, v_hgrn_norm_g, v_w_a, v_w_pool, v_pool_scale, v_w_out, v_ln1_g, v_ln1_b, v_w_up, v_w_down, v_ln2_g, v_ln2_b):
    Bl, S, D = x.shape
    T = Bl * S
    pg = D // POOL_GROUPS
    x2 = x.reshape(T, D)
    tgt = loss_target.reshape(T, D)
    me = 4 * lax.axis_index("x") + 2 * lax.axis_index("y") + lax.axis_index("c")
    me_arr = jnp.reshape(me, (1,)).astype(jnp.int32)

    names = ["w_in", "w_a", "w_pool", "w_out", "w_up", "w_down"]
    big_w = dict(zip(names, [w_in[0], w_a[0], w_pool[0], w_out[0], w_up[0], w_down[0]]))
    big_m = dict(zip(names, [m_w_in[0], m_w_a[0], m_w_pool[0], m_w_out[0], m_w_up[0], m_w_down[0]]))
    big_v = dict(zip(names, [v_w_in[0], v_w_a[0], v_w_pool[0], v_w_out[0], v_w_up[0], v_w_down[0]]))
    kinds = dict(w_in="col", w_a="row", w_pool="pool", w_out="row", w_up="col", w_down="row")
    lay = {nm: _Sharded(kinds[nm], big_w[nm].shape) for nm in names}
    wb = {nm: big_w[nm].astype(BF16) for nm in names}

    (w_in_f,) = _all_gather("ag_w_in", [wb["w_in"]], [lay["w_in"]])
    def gather_start(name, nms, after):
        return _exchange_start(name, [wb[nm] for nm in nms], [lax.empty(lay[nm].full_shape, BF16) for nm in nms],
                               src_at=lambda w, ref, peer: ref,
                               dst_at=lambda w, ref, mine, k: lay[nms[w]].at(ref, mine), after=after, own=True)

    ag_mix = gather_start("ag_mix", ["w_a", "w_pool", "w_out"], w_in_f)
    ag_mlp = gather_start("ag_mlp", ["w_up", "w_down"], ag_mix["token"])

    proj, x_t = _proj(x2, w_in_f, ag_mlp["token"])
    proj5 = proj.reshape(N_SEC, Bl, S, D)
    ain3, ain_t, o3, st_all = _hgrn_fwd(proj5, lb_logits, hgrn_norm_g)
    w_a_f, w_pool_f, w_out_f = _exchange_wait(ag_mix, ain3)
    pooled_t, bp3 = _pool_fwd(proj5, w_pool_f)
    ain, bp = ain3.reshape(T, D), bp3.reshape(T, D)
    a, merged_t, xhat1, rs1, x1b, x1_t = _mix_fwd(ain, proj, bp, x2, w_a_f, w_out_f, pool_scale, ln1_g, ln1_b)
    w_up_f, w_down_f = _exchange_wait(ag_mlp, x1b)
    hp, h, dr2, dr2b, dr2_t, vec_mlp = _mlp_fwd(x1b, w_up_f, w_down_f, xhat1, tgt, ln1_g, ln1_b, ln2_g, ln2_b)

    def scatter_start(name, nms, grads_b, after):
        lands = [lax.empty((N_DEV - 1,) + lay[nm].shape, BF16) for nm in nms]
        return _exchange_start(name, grads_b, lands,
                               src_at=lambda w, ref, peer: lay[nms[w]].at(ref, peer),
                               dst_at=lambda w, ref, mine, k: ref.at[k - 1], after=after)

    dhp, dr1, dr1b, vec_ln1 = _mlp_bwd(dr2b, dr2, hp, w_up_f, w_down_f, xhat1, rs1, ln1_g)
    FF = 4 * D
    whole_t = _resident((D, T))
    cols_b = pl.BlockSpec((T, DW_COLS), lambda j: (0, j))
    cols_o = ((D, DW_COLS), lambda j: (0, j))
    gw, gwb = {}, {}
    gw["w_down"], gwb["w_down"] = _dw(
        "dw_down", dr2_t, h, FF // DW_COLS, whole_t, cols_b, (FF, D), (DW_COLS, D), lambda j: (j, 0),
        transpose_out=True)
    rs_down = scatter_start("rs_w_down", ["w_down"], [gwb["w_down"]], gw["w_down"])
    gw["w_up"], gwb["w_up"] = _dw("dw_up", x1_t, dhp, FF // DW_COLS, whole_t, cols_b, (D, FF), *cols_o,
                                  dep=rs_down["token"])
    rs_up = scatter_start("rs_w_up", ["w_up"], [gwb["w_up"]], gw["w_up"])
    da_b, dbp_b, dain, dpooled, dg2, vec_mix = _mix_bwd(dr1b, proj, a, bp, w_a_f, w_out_f, w_pool_f, pool_scale,
                                                        rs_up["token"])
    gw["w_out"], gwb["w_out"] = _dw("dw_out", merged_t, dr1b, D // DW_COLS, whole_t, cols_b, (D, D), *cols_o)
    gw["w_a"], gwb["w_a"] = _dw("dw_a", ain_t, da_b, D // DW_COLS, _resident((Bl, D, S)), cols_b, (D, D), *cols_o)
    gw["w_pool"], gwb["w_pool"] = _dw(
        "dw_pool", pooled_t, dbp_b, POOL_GROUPS, pl.BlockSpec((pg, T), lambda j: (j, 0)),
        pl.BlockSpec((T, pg), lambda j: (0, j)), (POOL_GROUPS, pg, pg), (1, pg, pg), lambda j: (j, 0, 0))
    mid = ["w_out", "w_a", "w_pool"]
    rs_mid = scatter_start("rs_w_mid", mid, [gwb[nm] for nm in mid], gw["w_pool"])
    dpv = _pool_bwd(dpooled.reshape(Bl, S, D), rs_mid["token"]).reshape(T, D)
    gw_in_early, gwb_in_early = _dw_in_early(x_t, dpv, dg2, rs_mid["token"])
    land_in = lax.empty((N_DEV - 1,) + lay["w_in"].shape, BF16)
    rs_in_early = _w_in_scatter_start("rs_w_in_early", gwb_in_early, land_in, True, gw_in_early)
    dh4, vec_hgrn = _hgrn_bwd(proj5, lb_logits, hgrn_norm_g, dain.reshape(Bl, S, D), o3, st_all,
                              rs_in_early["token"])
    dh4 = dh4.reshape(4, T, D)
    vec_tot = _vec_allreduce(vec_mlp + vec_ln1 + vec_mix + vec_hgrn)
    gw["w_in"], gwb_in_late = _dw_in_late(x_t, dh4, gw_in_early, vec_tot)
    rs_in_late = _w_in_scatter_start("rs_w_in_late", gwb_in_late, rs_in_early["land"], False, gw["w_in"])
    grad_x2 = _dx(dr1, dh4, dpv, dg2, w_in_f, rs_in_late["token"])
    grad_x = grad_x2.reshape(Bl, S, D)

    small_names =["lb_logits", "hgrn_norm_g", "pool_scale", "ln1_g", "ln1_b", "ln2_g", "ln2_b"]
    small_w = [lb_logits, hgrn_norm_g, pool_scale, ln1_g, ln1_b, ln2_g, ln2_b]
    small_m = [m_lb_logits, m_hgrn_norm_g, m_pool_scale, m_ln1_g, m_ln1_b, m_ln2_g, m_ln2_b]
    small_v = [v_lb_logits, v_hgrn_norm_g, v_pool_scale, v_ln1_g, v_ln1_b, v_ln2_g, v_ln2_b]
    res = _vec_adam(vec_tot, small_w, small_m, small_v)
    loss = res[0][0, 0]
    n = len(small_w)
    small = {nm: (res[1 + i], res[1 + n + i], res[1 + 2 * n + i], res[1 + 3 * n + i])
             for i, nm in enumerate(small_names)}

    big, last = {}, grad_x2

    def adam(nm, land):
        outs = _adam_shard("adam_" + nm, me_arr, gw[nm], land, lay[nm], big_w[nm], big_m[nm], big_v[nm])
        big[nm] = tuple(t[None] for t in outs)
        return outs[0]

    for pend, nms in ((rs_down, ["w_down"]), (rs_up, ["w_up"]), (rs_mid, mid)):
        for nm, land in zip(nms, _exchange_wait(pend, last)):
            last = adam(nm, land)
    land_in = _w_in_scatter_wait(rs_in_early, rs_in_late["land"], last)
    adam("w_in", _w_in_scatter_wait(rs_in_late, land_in, res[0]))

    order = ["w_in", "lb_logits", "hgrn_norm_g", "w_a", "w_pool", "pool_scale", "w_out", "ln1_g", "ln1_b",
             "w_up", "w_down", "ln2_g", "ln2_b"]
    allp = {**big, **small}
    out = [loss, grad_x]
    for part in range(4):
        out += [allp[nm][part] for nm in order]
    return tuple(out)
```

```python
import jax
import jax.numpy as jnp
from jax import lax
from jax.experimental import pallas as pl
from jax.experimental.pallas import tpu as pltpu

F32 = jnp.float32
BF16 = jnp.bfloat16
MESH = pl.DeviceIdType.MESH

N_DEV = 8
HEAD = 128
CHUNK = 16
SUBLANES = 8
GROUP = 128
SUB_BLOCK = 1024
ROW_TILE = 512
MLP_ROW_TILE = 256
DW_COLS = 512
EARLY_SEC = 4
CH_PER_GROUP = GROUP // CHUNK
N_SEC = 7
POOL_GROUPS = 4
ALPHA = (2.0 * 1) ** 0.25
LN_EPS = 1e-5
RMS_EPS = 1e-6
Q_SCALE = HEAD ** -0.5
ADAM_LR = 0.001
ADAM_B1 = 0.9
ADAM_B2 = 0.999
ADAM_EPS = 1e-08
ADAM_WD = 0.01
ADAM_STEP = 10
VMEM_LIMIT = 60 << 20

NT_DIMS = (((1,), (1,)), ((), ()))
TN_DIMS = (((0,), (0,)), ((), ()))


def _params(sem=None):
    kw = dict(vmem_limit_bytes=VMEM_LIMIT)
    if sem is not None:
        kw["dimension_semantics"] = sem
    return pltpu.CompilerParams(**kw)


def _me():
    return lax.axis_index("x"), lax.axis_index("y"), lax.axis_index("c")


def _sigmoid(v):
    return jax.nn.sigmoid(v)


def _adamw(w, g, m, v):
    m = ADAM_B1 * m + (1.0 - ADAM_B1) * g
    v = ADAM_B2 * v + (1.0 - ADAM_B2) * jnp.square(g)
    m_hat = m / (1.0 - ADAM_B1 ** ADAM_STEP)
    v_hat = v / (1.0 - ADAM_B2 ** ADAM_STEP)
    delta = -ADAM_LR * (m_hat / (jnp.sqrt(v_hat) + ADAM_EPS) + ADAM_WD * w)
    return delta, m, v


class _Sharded:
    def __init__(self, kind, shard_shape):
        self.kind, self.shape = kind, tuple(shard_shape)

    @property
    def full_shape(self):
        r = self.shape
        if self.kind == "row":
            return (N_DEV * r[0],) + r[1:]
        return (r[0], N_DEV * r[1]) + r[2:]

    def at(self, ref, d):
        if self.kind == "col":
            n = self.shape[1]
            return ref.at[:, pl.ds(pl.multiple_of(d * n, 128), n)]
        if self.kind == "row":
            n = self.shape[0]
            return ref.at[pl.ds(pl.multiple_of(d * n, 16), n), :]
        n = self.shape[1]
        return ref.at[:, pl.ds(pl.multiple_of(d * n, 16), n), :]

    def block_index(self, d):
        return {"col": (0, d), "row": (d, 0), "pool": (0, d, 0)}[self.kind]


def _peer(k, x, y, c):
    return (1 - x if k & 4 else x, 1 - y if k & 2 else y, 1 - c if k & 1 else c)


def _all_gather(name, shards, layouts):
    nw = len(shards)

    def body(*refs):
        ins, outs = refs[:nw], refs[nw:2 * nw]
        send_sems, recv_sems, local_sems = refs[2 * nw:]
        x, y, c = _me()
        me = (x, y, c)
        sibling = (x, y, 1 - c)
        chips = [(1 - x, y), (x, 1 - y), (1 - x, 1 - y)]

        def copy(w, k, block, to, src=None):
            px, py, pc = block
            dst = layouts[w].at(outs[w], 4 * px + 2 * py + pc)
            return pltpu.make_async_remote_copy(
                src_ref=dst if src is None else src, dst_ref=dst,
                send_sem=send_sems.at[w, k], recv_sem=recv_sems.at[w, k],
                device_id=to, device_id_type=MESH)

        def place(w):
            mine = pltpu.make_async_copy(ins[w], layouts[w].at(outs[w], 4 * x + 2 * y + c), local_sems.at[w])
            mine.start()
            return mine

        north, south = c == 1, c == 0
        first, diagonal = [], []
        for w in range(nw):
            first.append(copy(w, 0, me, sibling, src=ins[w]))
            first += [copy(w, 1 + j, me, (*chips[j], c), src=ins[w]) for j in range(2)]
            diagonal.append(copy(w, 3, me, (*chips[2], c), src=ins[w]))
        for cp in first:
            cp.start()
        for cp in diagonal:
            pl.when(north)(cp.start)
        local = [place(w) for w in range(nw)]
        passed, relayed = [], []
        for w in range(nw):
            for j in (1, 0, 2):
                copy(w, 1 + j, (*chips[j], c), me).wait_recv()
                fwd = copy(w, 4 + j, (*chips[j], c), sibling)
                fwd.start()
                passed.append(fwd)
                if j == 1:
                    on = copy(w, 3, (*chips[1], c), (*chips[0], c))
                    pl.when(south)(on.start)
                    relayed.append(on)
        for w in range(nw):
            copy(w, 0, sibling, me).wait_recv()
            for j, chip in enumerate(chips):
                copy(w, 4 + j, (*chip, 1 - c), me).wait_recv()
        for cp in first + passed:
            cp.wait_send()
        for cp in diagonal:
            pl.when(north)(cp.wait_send)
        for cp in relayed:
            pl.when(south)(cp.wait_send)
        for cp in local:
            cp.wait()

    any_spec = pl.BlockSpec(memory_space=pl.ANY)
    return pl.pallas_call(
        body, name=name,
        out_shape=[jax.ShapeDtypeStruct(l.full_shape, s.dtype) for s, l in zip(shards, layouts)],
        in_specs=[any_spec] * nw, out_specs=[any_spec] * nw,
        scratch_shapes=[pltpu.SemaphoreType.DMA((nw, 7)), pltpu.SemaphoreType.DMA((nw, 7)),
                        pltpu.SemaphoreType.DMA((nw,))],
    )(*shards)


HBM_SPEC = pl.BlockSpec(memory_space=pltpu.HBM)
SEM_SPEC = pl.BlockSpec(memory_space=pltpu.SEMAPHORE)
DATAFLOW = pltpu.SideEffectType.DATAFLOW_SIDE_EFFECTING


def _exchange_copies(srcs, lands, send_sems, recv_sems, src_at, dst_at):
    x, y, c = _me()
    me = 4 * x + 2 * y + c
    copies = []
    for w in range(len(srcs)):
        for k in range(1, N_DEV):
            px, py, pc = _peer(k, x, y, c)
            copies.append(pltpu.make_async_remote_copy(
                src_ref=src_at(w, srcs[w], 4 * px + 2 * py + pc), dst_ref=dst_at(w, lands[w], me, k),
                send_sem=send_sems.at[w * (N_DEV - 1) + k - 1], recv_sem=recv_sems.at[w * (N_DEV - 1) + k - 1],
                device_id=(px, py, pc), device_id_type=MESH))
    return copies


def _own_copies(srcs, lands, own_sems, src_at, dst_at):
    x, y, c = _me()
    me = 4 * x + 2 * y + c
    return [pltpu.make_async_copy(src_at(w, srcs[w], me), dst_at(w, lands[w], me, 0), own_sems.at[w])
            for w in range(len(srcs))]


def _exchange_start(name, srcs, lands, src_at, dst_at, after, own=False):
    nw = len(srcs)

    def body(*refs):
        src_refs, land_refs = refs[:nw], refs[nw:2 * nw]
        send_sems, recv_sems, own_sems = refs[2 * nw + 1], refs[2 * nw + 2], refs[2 * nw + 3]
        token = refs[-1]
        for cp in _exchange_copies(src_refs, land_refs, send_sems, recv_sems, src_at, dst_at):
            cp.start()
        if own:
            for cp in _own_copies(src_refs, land_refs, own_sems, src_at, dst_at):
                cp.start()
        token[...] = jnp.zeros_like(token)

    hbm = lambda a: pltpu.HBM(a.shape, a.dtype)
    outs = pl.pallas_call(
        body, name=name,
        out_shape=(pltpu.SemaphoreType.DMA((nw * (N_DEV - 1),)), pltpu.SemaphoreType.DMA((nw * (N_DEV - 1),)),
                   pltpu.SemaphoreType.DMA((nw,)), *[hbm(a) for a in srcs], *[hbm(a) for a in lands],
                   jax.ShapeDtypeStruct((8, 128), F32)),
        in_specs=[HBM_SPEC] * (2 * nw) + [pl.BlockSpec(memory_space=pl.ANY)],
        out_specs=(SEM_SPEC, SEM_SPEC, SEM_SPEC, *[HBM_SPEC] * (2 * nw), pl.BlockSpec(memory_space=pltpu.VMEM)),
        input_output_aliases={i: 3 + i for i in range(2 * nw)},
        compiler_params=pltpu.CompilerParams(has_side_effects=DATAFLOW),
    )(*[pltpu.with_memory_space_constraint(a, pltpu.HBM) for a in list(srcs) + list(lands)], after)
    return dict(send=outs[0], recv=outs[1], own_sems=outs[2], srcs=outs[3:3 + nw], lands=outs[3 + nw:3 + 2 * nw],
                token=outs[-1], src_at=src_at, dst_at=dst_at, name=name, own=own)


def _exchange_wait(pending, after):
    nw = len(pending["srcs"])

    def body(*refs):
        src_refs, land_refs = refs[:nw], refs[nw:2 * nw]
        send_sems, recv_sems, own_sems = refs[2 * nw], refs[2 * nw + 1], refs[2 * nw + 2]
        for cp in _exchange_copies(src_refs, land_refs, send_sems, recv_sems,
                                   pending["src_at"], pending["dst_at"]):
            cp.wait_send()
            cp.wait_recv()
        if pending["own"]:
            for cp in _own_copies(src_refs, land_refs, own_sems, pending["src_at"], pending["dst_at"]):
                cp.wait()

    hbm = lambda a: pltpu.HBM(a.shape, a.dtype)
    outs = pl.pallas_call(
        body, name=pending["name"] + "_wait",
        out_shape=(*[hbm(a) for a in pending["srcs"]], *[hbm(a) for a in pending["lands"]]),
        in_specs=[HBM_SPEC] * (2 * nw) + [SEM_SPEC, SEM_SPEC, SEM_SPEC, pl.BlockSpec(memory_space=pl.ANY)],
        out_specs=tuple([HBM_SPEC] * (2 * nw)),
        input_output_aliases={i: i for i in range(2 * nw)},
        compiler_params=pltpu.CompilerParams(has_side_effects=DATAFLOW),
    )(*pending["srcs"], *pending["lands"], pending["send"], pending["recv"], pending["own_sems"], after)
    return outs[nw:]


def _w_in_scatter_copies(src, land, send_sems, recv_sems, early):
    rows, cols = land.shape[1], land.shape[2]
    bound = EARLY_SEC * rows
    cut_dev = bound // cols
    cut = bound - cut_dev * cols
    x, y, c = _me()
    me = 4 * x + 2 * y + c

    def pieces(t):
        if early:
            return [(t > cut_dev, t * cols - bound, cols, 0), (t == cut_dev, 0, cols - cut, cut)]
        return [(t < cut_dev, t * cols, cols, 0), (t == cut_dev, cut_dev * cols, cut, 0)]

    out = []
    for k in range(1, N_DEV):
        px, py, pc = _peer(k, x, y, c)
        for (to_peer, s0, width, d0), (to_me, _, _, _) in zip(pieces(4 * px + 2 * py + pc), pieces(me)):
            s0 = s0 if isinstance(s0, int) else pl.multiple_of(jnp.maximum(s0, 0), 128)
            out.append((to_peer, to_me, pltpu.make_async_remote_copy(
                src_ref=src.at[:, pl.ds(s0, width)], dst_ref=land.at[k - 1, :, pl.ds(d0, width)],
                send_sem=send_sems.at[k - 1], recv_sem=recv_sems.at[k - 1],
                device_id=(px, py, pc), device_id_type=MESH)))
    return out


def _w_in_scatter_start(name, src, land, early, after):
    def body(src_ref, land_ref, after_ref, send_sems, recv_sems, src_thru, land_thru, token):
        for to_peer, _, cp in _w_in_scatter_copies(src_ref, land_ref, send_sems, recv_sems, early):
            pl.when(to_peer)(cp.start)
        token[...] = jnp.zeros_like(token)

    hbm = lambda a: pltpu.HBM(a.shape, a.dtype)
    outs = pl.pallas_call(
        body, name=name,
        out_shape=(pltpu.SemaphoreType.DMA((N_DEV - 1,)), pltpu.SemaphoreType.DMA((N_DEV - 1,)),
                   hbm(src), hbm(land), jax.ShapeDtypeStruct((8, 128), F32)),
        in_specs=[HBM_SPEC, HBM_SPEC, pl.BlockSpec(memory_space=pl.ANY)],
        out_specs=(SEM_SPEC, SEM_SPEC, HBM_SPEC, HBM_SPEC, pl.BlockSpec(memory_space=pltpu.VMEM)),
        input_output_aliases={0: 2, 1: 3},
        compiler_params=pltpu.CompilerParams(has_side_effects=DATAFLOW),
    )(pltpu.with_memory_space_constraint(src, pltpu.HBM), pltpu.with_memory_space_constraint(land, pltpu.HBM), after)
    return dict(send=outs[0], recv=outs[1], src=outs[2], land=outs[3], token=outs[4], early=early, name=name)


def _w_in_scatter_wait(pending, land, after):
    def body(src_ref, land_ref, send_sems, recv_sems, after_ref, src_dead, land_out):
        for to_peer, to_me, cp in _w_in_scatter_copies(src_ref, land_ref, send_sems, recv_sems, pending["early"]):
            pl.when(to_peer)(cp.wait_send)
            pl.when(to_me)(cp.wait_recv)

    hbm = lambda a: pltpu.HBM(a.shape, a.dtype)
    outs = pl.pallas_call(
        body, name=pending["name"] + "_wait", out_shape=(hbm(pending["src"]), hbm(land)),
        in_specs=[HBM_SPEC, HBM_SPEC, SEM_SPEC, SEM_SPEC, pl.BlockSpec(memory_space=pl.ANY)],
        out_specs=(HBM_SPEC, HBM_SPEC), input_output_aliases={0: 0, 1: 1},
        compiler_params=pltpu.CompilerParams(has_side_effects=DATAFLOW),
    )(pending["src"], land, pending["send"], pending["recv"], after)
    return outs[1]


def _call_after(dep, body, args, *, in_specs, **kw):
    n_in = len(args)

    def wrapped(*refs):
        body(*refs[:n_in], *refs[n_in + 1:])

    dep_spec = pl.BlockSpec(dep.shape, lambda *_: (0,) * dep.ndim)
    return pl.pallas_call(wrapped, in_specs=list(in_specs) + [dep_spec], **kw)(*args, dep)


def _resident(shape):
    return pl.BlockSpec(shape, lambda *_: (0,) * len(shape), pipeline_mode=pl.Buffered(1))


def _proj(x2, w_in, dep):
    T, D = x2.shape
    tm = min(ROW_TILE, T)

    def body(x_ref, w_ref, o_ref, xt_ref):
        x = x_ref[...]
        xt_ref[...] = x.T.astype(BF16)
        xb = x.astype(BF16)
        for j in range(N_SEC):
            o_ref[j] = jnp.dot(xb, w_ref[:, j * D:(j + 1) * D], preferred_element_type=F32)

    return _call_after(
        dep, body, (x2, w_in), name="proj", grid=(T // tm,),
        in_specs=[pl.BlockSpec((tm, D), lambda i: (i, 0)), _resident((D, N_SEC * D))],
        out_specs=[pl.BlockSpec((N_SEC, tm, D), lambda i: (0, i, 0)), pl.BlockSpec((D, tm), lambda i: (0, i))],
        out_shape=[jax.ShapeDtypeStruct((N_SEC, T, D), F32), jax.ShapeDtypeStruct((D, T), BF16)],
        compiler_params=_params(("parallel",)))


def _chunk_cumsum(v, reverse=False):
    rows, lanes = v.shape
    x = v.reshape(rows // SUBLANES, SUBLANES, lanes)
    pos = lax.broadcasted_iota(jnp.int32, x.shape, 1)
    for sh in (1, 2, 4):
        if reverse:
            x = x + jnp.where(pos < SUBLANES - sh, pltpu.roll(x, SUBLANES - sh, 1), 0.0)
        else:
            x = x + jnp.where(pos >= sh, pltpu.roll(x, sh, 1), 0.0)
    x = x.reshape(rows // CHUNK, CHUNK // SUBLANES, SUBLANES, lanes)
    half = lax.broadcasted_iota(jnp.int32, x.shape, 1)
    if reverse:
        x = x + jnp.where(half == 0, x[:, 1:2, 0:1, :], 0.0)
    else:
        x = x + jnp.where(half == 1, x[:, 0:1, SUBLANES - 1:SUBLANES, :], 0.0)
    return x.reshape(rows, lanes)


def _hgrn_gates(q, f_pre, lb_logits):
    l0, l1 = lb_logits[0:1, :], lb_logits[1:2, :]
    mx = jnp.maximum(l0, l1)
    e0, e1 = jnp.exp(l0 - mx), jnp.exp(l1 - mx)
    lb = e0 / (e0 + e1)
    sq = _sigmoid(q)
    qf = q * sq * Q_SCALE
    sg = _sigmoid(f_pre)
    f = lb + (1.0 - lb) * sg
    k = 1.0 - f
    log_f = jnp.log(f)
    G = _chunk_cumsum(log_f)
    g_to_end = _chunk_cumsum(log_f, reverse=True) - log_f
    e_g = jnp.exp(G)
    e_ng = jnp.exp(-G)
    e_ge = jnp.exp(g_to_end)
    return dict(lb=lb, sq=sq, qf=qf, sg=sg, f=f, k=k, G=G, e_g=e_g, e_ng=e_ng, e_ge=e_ge,
                qd=qf * e_g, ki=k * e_ng, ke=k * e_ge, dec=e_g * e_ge)


def _intra_mask():
    r = lax.broadcasted_iota(jnp.int32, (GROUP, GROUP), 0)
    c = lax.broadcasted_iota(jnp.int32, (GROUP, GROUP), 1)
    return (r // CHUNK == c // CHUNK) & (c <= r)


def _chunk_outer(lhs_rows, rhs_b, out_scr, sb):
    lane = lax.broadcasted_iota(jnp.int32, (GROUP, GROUP), 1) // CHUNK
    for g in range(sb // GROUP):
        sl = slice(g * GROUP, (g + 1) * GROUP)
        lhs_t = lhs_rows[sl].T
        for cc in range(CH_PER_GROUP):
            masked = jnp.where(lane == cc, lhs_t, 0.0).astype(BF16)
            out_scr[g * CH_PER_GROUP + cc] = jnp.dot(masked, rhs_b[sl], preferred_element_type=F32)


def _hgrn_forward_blocks(cs, vs, st0s, sb, o_scr, kv_scr, st_scr, dec_scr):
    nc = sb // CHUNK
    n_str = len(cs)
    mask = _intra_mask()
    bf = []
    for i, (c, v) in enumerate(zip(cs, vs)):
        qd_b, ki_b, ke_b, v_b = (c["qd"].astype(BF16), c["ki"].astype(BF16), c["ke"].astype(BF16),
                                 v.astype(BF16))
        bf.append((qd_b, ki_b, ke_b, v_b))
        for g in range(sb // GROUP):
            sl = slice(g * GROUP, (g + 1) * GROUP)
            sc = lax.dot_general(qd_b[sl], ki_b[sl], NT_DIMS, preferred_element_type=F32)
            a = jnp.where(mask, sc, 0.0).astype(BF16)
            o_scr[i, sl, :] = jnp.dot(a, v_b[sl], preferred_element_type=F32)
        _chunk_outer(v, ke_b, kv_scr.at[i], sb)
        dec_scr[i] = c["dec"]

    def rec(n, sts):
        row = pl.ds(pl.multiple_of(n * CHUNK, CHUNK), 1)
        out = []
        for i in range(n_str):
            st_scr[i, n] = sts[i]
            out.append(sts[i] * dec_scr[i, row, :] + kv_scr[i, n])
        return tuple(out)

    ends = lax.fori_loop(0, nc, rec, tuple(st0s))

    for n in range(nc):
        rows = slice(n * CHUNK, (n + 1) * CHUNK)
        for i in range(n_str):
            o_scr[i, rows, :] += lax.dot_general(bf[i][0][rows], st_scr[i, n].astype(BF16), NT_DIMS,
                                                 preferred_element_type=F32)
    return ends, bf


def _hgrn_fwd(proj5, lb_logits, gn):
    _, Bl, S, D = proj5.shape
    H = D // HEAD
    sb = min(SUB_BLOCK, S)
    nsb = S // sb
    nc = sb // CHUNK

    def body(p_ref, lbl_ref, gn_ref, ain_ref, aint_ref, o_ref, st_ref, carry, o_scr, kv_scr, st_scr, dec_scr):
        @pl.when(pl.program_id(1) == 0)
        def _():
            carry[...] = jnp.zeros_like(carry)

        st0s = [carry[b] for b in range(Bl)]
        cs = [_hgrn_gates(p_ref[0, b], p_ref[1, b], lbl_ref[...]) for b in range(Bl)]
        ends, _ = _hgrn_forward_blocks(cs, [p_ref[2, b] for b in range(Bl)], st0s, sb,
                                       o_scr, kv_scr, st_scr, dec_scr)
        for b in range(Bl):
            carry[b] = ends[b]
            st_ref[b, 0] = st_scr[b].astype(BF16)
            o = o_scr[b]
            o_ref[b] = o
            rinv = lax.rsqrt(jnp.mean(o * o, axis=-1, keepdims=True) + RMS_EPS)
            ain = o * rinv * gn_ref[...] * _sigmoid(p_ref[3, b])
            ain_ref[b] = ain.astype(BF16)
            aint_ref[b] = ain.T.astype(BF16)

    return pl.pallas_call(
        body, name="hgrn_fwd", grid=(H, nsb),
        in_specs=[pl.BlockSpec((4, Bl, sb, HEAD), lambda h, s: (0, 0, s, h)),
                  pl.BlockSpec((2, HEAD), lambda h, s: (0, h)),
                  pl.BlockSpec((1, HEAD), lambda h, s: (0, h))],
        out_specs=[pl.BlockSpec((Bl, sb, HEAD), lambda h, s: (0, s, h)),
                   pl.BlockSpec((Bl, HEAD, sb), lambda h, s: (0, h, s)),
                   pl.BlockSpec((Bl, sb, HEAD), lambda h, s: (0, s, h)),
                   pl.BlockSpec((Bl, 1, nc, HEAD, HEAD), lambda h, s: (0, h, s, 0, 0))],
        out_shape=[jax.ShapeDtypeStruct((Bl, S, D), BF16), jax.ShapeDtypeStruct((Bl, D, S), BF16),
                   jax.ShapeDtypeStruct((Bl, S, D), F32),
                   jax.ShapeDtypeStruct((Bl, H, S // CHUNK, HEAD, HEAD), BF16)],
        scratch_shapes=[pltpu.VMEM((Bl, HEAD, HEAD), F32), pltpu.VMEM((Bl, sb, HEAD), F32),
                        pltpu.VMEM((Bl, nc, HEAD, HEAD), F32), pltpu.VMEM((Bl, nc, HEAD, HEAD), F32),
                        pltpu.VMEM((Bl, sb, HEAD), F32)],
        compiler_params=_params(("parallel", "arbitrary")),
    )(proj5, lb_logits, gn)


def _window_count(shape, g):
    pos = lax.broadcasted_iota(jnp.int32, shape, 0)
    return pos, jnp.minimum(pos + 1, jnp.left_shift(2, g)).astype(F32)


def _select_window(g, sums):
    return jnp.where(g == 0, sums[0], jnp.where(g == 1, sums[1], jnp.where(g == 2, sums[2], sums[3])))


def _pool_fwd(proj5, w_pool):
    _, Bl, S, D = proj5.shape
    pg = D // POOL_GROUPS

    def body(v_ref, w_ref, pooled_t_ref, bp_ref):
        g = pl.program_id(1)
        v = v_ref[0, 0]
        pos, cnt = _window_count(v.shape, g)
        cur, sums = v, []
        for sh in (1, 2, 4, 8):
            cur = cur + jnp.where(pos >= sh, pltpu.roll(cur, sh, 0), 0.0)
            sums.append(cur)
        pooled = _select_window(g, sums) / cnt - v
        pooled_t_ref[...] = pooled.T.astype(BF16)
        bp_ref[0] = jnp.dot(pooled.astype(BF16), w_ref[0], preferred_element_type=F32)

    return pl.pallas_call(
        body, name="pool_fwd", grid=(Bl, POOL_GROUPS),
        in_specs=[pl.BlockSpec((1, 1, S, pg), lambda b, g: (4, b, 0, g)),
                  pl.BlockSpec((1, pg, pg), lambda b, g: (g, 0, 0))],
        out_specs=[pl.BlockSpec((pg, S), lambda b, g: (g, b)),
                   pl.BlockSpec((1, S, pg), lambda b, g: (b, 0, g))],
        out_shape=[jax.ShapeDtypeStruct((D, Bl * S), BF16), jax.ShapeDtypeStruct((Bl, S, D), F32)],
        compiler_params=_params(("parallel", "parallel")),
    )(proj5, w_pool)


def _layer_norm_fwd(r):
    mu = jnp.mean(r, axis=-1, keepdims=True)
    d = r - mu
    rs = lax.rsqrt(jnp.mean(d * d, axis=-1, keepdims=True) + LN_EPS)
    return d * rs, rs


def _layer_norm_bwd(dy_g, xhat, rs):
    return rs * (dy_g - jnp.mean(dy_g, axis=-1, keepdims=True)
                 - xhat * jnp.mean(dy_g * xhat, axis=-1, keepdims=True))


def _mix_fwd(ain, proj, bp, x2, w_a, w_out, ps, g1, b1):
    T, D = x2.shape
    tm = min(ROW_TILE, T)

    def body(ain_ref, ga_ref, gb_ref, bp_ref, x_ref, wa_ref, wo_ref, ps_ref, g1_ref, b1_ref,
             a_ref, mgt_ref, xh_ref, rs_ref, x1b_ref, x1t_ref):
        a = jnp.dot(ain_ref[...], wa_ref[...], preferred_element_type=F32)
        a_ref[...] = a
        merged = _sigmoid(ga_ref[0]) * a + _sigmoid(gb_ref[0]) * (bp_ref[...] * ps_ref[...])
        mgt_ref[...] = merged.T.astype(BF16)
        r1 = ALPHA * x_ref[...] + jnp.dot(merged.astype(BF16), wo_ref[...], preferred_element_type=F32)
        xhat, rs = _layer_norm_fwd(r1)
        xh_ref[...] = xhat
        rs_ref[...] = rs
        x1 = xhat * g1_ref[...] + b1_ref[...]
        x1b_ref[...] = x1.astype(BF16)
        x1t_ref[...] = x1.T.astype(BF16)

    row = lambda i: (i, 0)
    col = lambda i: (0, i)
    full = lambda i: (0, 0)
    return pl.pallas_call(
        body, name="mix_fwd", grid=(T // tm,),
        in_specs=[pl.BlockSpec((tm, D), row),
                  pl.BlockSpec((1, tm, D), lambda i: (5, i, 0)),
                  pl.BlockSpec((1, tm, D), lambda i: (6, i, 0)),
                  pl.BlockSpec((tm, D), row), pl.BlockSpec((tm, D), row),
                  pl.BlockSpec((D, D), full), pl.BlockSpec((D, D), full),
                  pl.BlockSpec((1, D), full), pl.BlockSpec((1, D), full), pl.BlockSpec((1, D), full)],
        out_specs=[pl.BlockSpec((tm, D), row), pl.BlockSpec((D, tm), col), pl.BlockSpec((tm, D), row),
                   pl.BlockSpec((tm, 1), row), pl.BlockSpec((tm, D), row), pl.BlockSpec((D, tm), col)],
        out_shape=[jax.ShapeDtypeStruct((T, D), F32), jax.ShapeDtypeStruct((D, T), BF16),
                   jax.ShapeDtypeStruct((T, D), F32), jax.ShapeDtypeStruct((T, 1), F32),
                   jax.ShapeDtypeStruct((T, D), BF16), jax.ShapeDtypeStruct((D, T), BF16)],
        compiler_params=_params(("parallel",)),
    )(ain, proj, proj, bp, x2, w_a, w_out, ps, g1, b1)


def _mlp_fwd(x1b, w_up, w_down, xhat1, tgt, g1, b1, g2, b2):
    T, D = xhat1.shape
    FF = w_up.shape[1]
    tm = min(MLP_ROW_TILE, T)

    def body(x_ref, wu_ref, wd_ref, xh_ref, t_ref, g1_ref, b1_ref, g2_ref, b2_ref,
             hp_ref, h_ref, dr_ref, drb_ref, drt_ref, vec_ref):
        @pl.when(pl.program_id(0) == 0)
        def _():
            vec_ref[...] = jnp.zeros_like(vec_ref)

        xb = x_ref[...]
        x1 = xh_ref[...] * g1_ref[...] + b1_ref[...]
        r2 = ALPHA * x1
        for f in range(FF // D):
            cols = slice(f * D, (f + 1) * D)
            hp = jnp.dot(xb, wu_ref[:, cols], preferred_element_type=F32)
            hp_ref[:, cols] = hp
            h = jnp.square(jnp.maximum(hp, 0.0)).astype(BF16)
            h_ref[:, cols] = h
            r2 = r2 + jnp.dot(h, wd_ref[cols, :], preferred_element_type=F32)
        xhat2, rs2 = _layer_norm_fwd(r2)
        err = xhat2 * g2_ref[...] + b2_ref[...] - t_ref[...]
        dy = err / D
        vec_ref[5:6, :] += jnp.sum(dy * xhat2, axis=0, keepdims=True)
        vec_ref[6:7, :] += jnp.sum(dy, axis=0, keepdims=True)
        vec_ref[7:8, :] += jnp.sum(0.5 * err * err / D, axis=0, keepdims=True)
        dr = _layer_norm_bwd(dy * g2_ref[...], xhat2, rs2)
        dr_ref[...] = dr
        drb_ref[...] = dr.astype(BF16)
        drt_ref[...] = dr.T.astype(BF16)

    row = lambda i: (i, 0)
    full = lambda i: (0, 0)
    return pl.pallas_call(
        body, name="mlp_fwd", grid=(T // tm,),
        in_specs=[pl.BlockSpec((tm, D), row), _resident((D, FF)), _resident((FF, D)),
                  pl.BlockSpec((tm, D), row), pl.BlockSpec((tm, D), row),
                  pl.BlockSpec((1, D), full), pl.BlockSpec((1, D), full),
                  pl.BlockSpec((1, D), full), pl.BlockSpec((1, D), full)],
        out_specs=[pl.BlockSpec((tm, FF), row), pl.BlockSpec((tm, FF), row), pl.BlockSpec((tm, D), row),
                   pl.BlockSpec((tm, D), row), pl.BlockSpec((D, tm), lambda i: (0, i)),
                   pl.BlockSpec((8, D), full)],
        out_shape=[jax.ShapeDtypeStruct((T, FF), F32), jax.ShapeDtypeStruct((T, FF), BF16),
                   jax.ShapeDtypeStruct((T, D), F32), jax.ShapeDtypeStruct((T, D), BF16),
                   jax.ShapeDtypeStruct((D, T), BF16), jax.ShapeDtypeStruct((8, D), F32)],
        compiler_params=_params(("arbitrary",)),
    )(x1b, w_up, w_down, xhat1, tgt, g1, b1, g2, b2)


def _mlp_bwd(drb, dr, hp, w_up, w_down, xhat1, rs1, g1):
    T, D = dr.shape
    FF = hp.shape[1]
    tm = min(MLP_ROW_TILE, T)

    def body(drb_ref, dr_ref, hp_ref, wu_ref, wd_ref, xh_ref, rs_ref, g1_ref,
             dhp_ref, d1_ref, d1b_ref, vec_ref):
        @pl.when(pl.program_id(0) == 0)
        def _():
            vec_ref[...] = jnp.zeros_like(vec_ref)

        drb = drb_ref[...]
        dx1 = ALPHA * dr_ref[...]
        for f in range(FF // D):
            cols = slice(f * D, (f + 1) * D)
            dh = lax.dot_general(drb, wd_ref[cols, :], NT_DIMS, preferred_element_type=F32)
            dhp = (dh * (2.0 * jnp.maximum(hp_ref[:, cols], 0.0))).astype(BF16)
            dhp_ref[:, cols] = dhp
            dx1 = dx1 + lax.dot_general(dhp, wu_ref[:, cols], NT_DIMS, preferred_element_type=F32)
        xhat = xh_ref[...]
        vec_ref[3:4, :] += jnp.sum(dx1 * xhat, axis=0, keepdims=True)
        vec_ref[4:5, :] += jnp.sum(dx1, axis=0, keepdims=True)
        d1 = _layer_norm_bwd(dx1 * g1_ref[...], xhat, rs_ref[...])
        d1_ref[...] = d1
        d1b_ref[...] = d1.astype(BF16)

    row = lambda i: (i, 0)
    full = lambda i: (0, 0)
    return pl.pallas_call(
        body, name="mlp_bwd", grid=(T // tm,),
        in_specs=[pl.BlockSpec((tm, D), row), pl.BlockSpec((tm, D), row), pl.BlockSpec((tm, FF), row),
                  _resident((D, FF)), _resident((FF, D)),
                  pl.BlockSpec((tm, D), row), pl.BlockSpec((tm, 1), row), pl.BlockSpec((1, D), full)],
        out_specs=[pl.BlockSpec((tm, FF), row), pl.BlockSpec((tm, D), row), pl.BlockSpec((tm, D), row),
                   pl.BlockSpec((8, D), full)],
        out_shape=[jax.ShapeDtypeStruct((T, FF), BF16), jax.ShapeDtypeStruct((T, D), F32),
                   jax.ShapeDtypeStruct((T, D), BF16), jax.ShapeDtypeStruct((8, D), F32)],
        compiler_params=_params(("arbitrary",)),
    )(drb, dr, hp, w_up, w_down, xhat1, rs1, g1)


def _dw(name, a_t, b, n_j, a_spec, b_spec, o_shape, o_block, o_map, transpose_out=False, dep=None,
        into=(None, None), ob_shape=None, ob_map=None):
    def body(*refs):
        a_ref, b_ref, o_ref, ob_ref = refs[0], refs[1], refs[-2], refs[-1]
        b_val = b_ref[0] if len(b_ref.shape) == 3 else b_ref[...]
        if len(a_ref.shape) == 3:
            seq = a_ref.shape[2]
            p = sum(jnp.dot(a_ref[i], b_val[i * seq:(i + 1) * seq], preferred_element_type=F32)
                    for i in range(a_ref.shape[0]))
        else:
            p = jnp.dot(a_ref[...], b_val, preferred_element_type=F32)
        if transpose_out:
            p = p.T
        p = p.reshape(o_ref.shape)
        o_ref[...] = p
        ob_ref[...] = p.astype(BF16)

    kw = dict(name=name, grid=(n_j,), in_specs=[a_spec, b_spec],
              out_specs=[pl.BlockSpec(o_block, o_map), pl.BlockSpec(o_block, ob_map or o_map)],
              out_shape=[jax.ShapeDtypeStruct(o_shape, F32), jax.ShapeDtypeStruct(ob_shape or o_shape, BF16)],
              compiler_params=_params(("parallel",)))
    args = (a_t, b)
    aliases = {}
    for out_index, arr in enumerate(into):
        if arr is not None:
            aliases[len(args)] = out_index
            args = args + (arr,)
            kw["in_specs"] = kw["in_specs"] + [pl.BlockSpec(memory_space=pl.ANY)]
    if aliases:
        kw["input_output_aliases"] = aliases
    if dep is None:
        return pl.pallas_call(body, **kw)(*args)
    return _call_after(dep, body, args, **kw)


def _mix_bwd(d1b, proj, a, bp, w_a, w_out, w_pool, ps, dep):
    T, D = a.shape
    tm = min(ROW_TILE, T)
    pg = D // POOL_GROUPS

    def body(d1b_ref, ga_ref, gb_ref, a_ref, bp_ref, wa_ref, wo_ref, wp_ref, ps_ref,
             da_ref, dbp_ref, dain_ref, dpl_ref, dg_ref, vec_ref):
        @pl.when(pl.program_id(0) == 0)
        def _():
            vec_ref[...] = jnp.zeros_like(vec_ref)

        dm = lax.dot_general(d1b_ref[...], wo_ref[...], NT_DIMS, preferred_element_type=F32)
        sa, sg = _sigmoid(ga_ref[0]), _sigmoid(gb_ref[0])
        bp_v, ps_v = bp_ref[...], ps_ref[...]
        da = (dm * sa).astype(BF16)
        db = dm * sg
        dg_ref[0] = (dm * a_ref[...] * sa * (1.0 - sa)).astype(BF16)
        dg_ref[1] = (dm * (bp_v * ps_v) * sg * (1.0 - sg)).astype(BF16)
        vec_ref[2:3, :] += jnp.sum(db * bp_v, axis=0, keepdims=True)
        dbp = (db * ps_v).astype(BF16)
        da_ref[...] = da
        dbp_ref[...] = dbp
        dain_ref[...] = lax.dot_general(da, wa_ref[...], NT_DIMS, preferred_element_type=F32)
        for g in range(POOL_GROUPS):
            cols = slice(g * pg, (g + 1) * pg)
            dpl_ref[:, cols] = lax.dot_general(dbp[:, cols], wp_ref[g], NT_DIMS,
                                               preferred_element_type=F32)

    row = lambda i: (i, 0)
    full = lambda i: (0, 0)
    return _call_after(
        dep, body, (d1b, proj, proj, a, bp, w_a, w_out, w_pool, ps), name="mix_bwd", grid=(T // tm,),
        in_specs=[pl.BlockSpec((tm, D), row),
                  pl.BlockSpec((1, tm, D), lambda i: (5, i, 0)),
                  pl.BlockSpec((1, tm, D), lambda i: (6, i, 0)),
                  pl.BlockSpec((tm, D), row), pl.BlockSpec((tm, D), row),
                  pl.BlockSpec((D, D), full), pl.BlockSpec((D, D), full),
                  pl.BlockSpec((POOL_GROUPS, pg, pg), lambda i: (0, 0, 0)),
                  pl.BlockSpec((1, D), full)],
        out_specs=[pl.BlockSpec((tm, D), row), pl.BlockSpec((tm, D), row),
                   pl.BlockSpec((tm, D), row), pl.BlockSpec((tm, D), row),
                   pl.BlockSpec((2, tm, D), lambda i: (0, i, 0)),
                   pl.BlockSpec((8, D), full)],
        out_shape=[jax.ShapeDtypeStruct((T, D), BF16), jax.ShapeDtypeStruct((T, D), BF16),
                   jax.ShapeDtypeStruct((T, D), F32), jax.ShapeDtypeStruct((T, D), F32),
                   jax.ShapeDtypeStruct((2, T, D), BF16), jax.ShapeDtypeStruct((8, D), F32)],
        compiler_params=_params(("arbitrary",)))


def _pool_bwd(dpooled3, dep):
    Bl, S, D = dpooled3.shape
    pg = D // POOL_GROUPS

    def body(dp_ref, dv_ref):
        g = pl.program_id(1)
        dp = dp_ref[0]
        pos, cnt = _window_count(dp.shape, g)
        cur, sums = dp / cnt, []
        for sh in (1, 2, 4, 8):
            cur = cur + jnp.where(pos < S - sh, pltpu.roll(cur, S - sh, 0), 0.0)
            sums.append(cur)
        dv_ref[0] = (_select_window(g, sums) - dp).astype(BF16)

    spec = pl.BlockSpec((1, S, pg), lambda b, g: (b, 0, g))
    return _call_after(
        dep, body, (dpooled3,), name="pool_bwd", grid=(Bl, POOL_GROUPS), in_specs=[spec], out_specs=spec,
        out_shape=jax.ShapeDtypeStruct((Bl, S, D), BF16),
        compiler_params=_params(("parallel", "parallel")))


def _hgrn_bwd(proj5, lb_logits, gn, dain3, o3, st_all, dep):
    _, Bl, S, D = proj5.shape
    H = D // HEAD
    sb = min(SUB_BLOCK, S)
    nsb = S // sb
    nc = sb // CHUNK
    streams = range(Bl)

    def body(p_ref, lbl_ref, gn_ref, dain_ref, o_ref, st_ref, d_ref, vec_ref,
             dcarry, kv_scr, dst_scr, dec_scr, dvi_scr, dke_scr, dqi_scr):
        s = pl.program_id(1)

        @pl.when(s == 0)
        def _():
            dcarry[...] = jnp.zeros_like(dcarry)
            vec_ref[...] = jnp.zeros_like(vec_ref)

        qs, vs, ogs = [p_ref[0, b] for b in streams], [p_ref[2, b] for b in streams], [p_ref[3, b] for b in streams]
        cs = [_hgrn_gates(qs[b], p_ref[1, b], lbl_ref[...]) for b in streams]
        bf = [(cs[b]["qd"].astype(BF16), cs[b]["ki"].astype(BF16), cs[b]["ke"].astype(BF16),
               vs[b].astype(BF16)) for b in streams]
        mask = _intra_mask()
        gn_v = gn_ref[...]
        keep = []
        for b in streams:
            qd_b, ki_b, ke_b, v_b = bf[b]
            dec_scr[b] = cs[b]["dec"]
            o = o_ref[b]
            rinv = lax.rsqrt(jnp.mean(o * o, axis=-1, keepdims=True) + RMS_EPS)
            on = o * rinv
            so = _sigmoid(ogs[b])
            dain = dain_ref[b]
            vec_ref[1:2, :] += jnp.sum(dain * on * so, axis=0, keepdims=True)
            d_og = dain * on * gn_v * so * (1.0 - so)
            d_on = dain * gn_v * so
            do = rinv * (d_on - on * jnp.mean(d_on * on, axis=-1, keepdims=True))
            do_b = do.astype(BF16)
            dv_parts, dqd_parts, dki_parts = [], [], []
            for g in range(sb // GROUP):
                sl = slice(g * GROUP, (g + 1) * GROUP)
                sc = lax.dot_general(qd_b[sl], ki_b[sl], NT_DIMS, preferred_element_type=F32)
                a = jnp.where(mask, sc, 0.0).astype(BF16)
                da = lax.dot_general(do_b[sl], v_b[sl], NT_DIMS, preferred_element_type=F32)
                da = jnp.where(mask, da, 0.0).astype(BF16)
                dv_parts.append(lax.dot_general(a, do_b[sl], TN_DIMS, preferred_element_type=F32))
                dqd_parts.append(jnp.dot(da, ki_b[sl], preferred_element_type=F32))
                dki_parts.append(lax.dot_general(da, qd_b[sl], TN_DIMS, preferred_element_type=F32))
            keep.append(dict(d_og=d_og, do_b=do_b, dv_intra=jnp.concatenate(dv_parts, axis=0),
                             dqd_intra=jnp.concatenate(dqd_parts, axis=0),
                             dki=jnp.concatenate(dki_parts, axis=0)))
            _chunk_outer(do, qd_b, kv_scr.at[b], sb)

        def rrec(i, dsts):
            n = nc - 1 - i
            row = pl.ds(pl.multiple_of(n * CHUNK, CHUNK), 1)
            out = []
            for b in streams:
                dst_scr[b, n] = dsts[b]
                out.append(dsts[b] * dec_scr[b, row, :] + kv_scr[b, n])
            return tuple(out)

        ends = lax.fori_loop(0, nc, rrec, tuple(dcarry[b] for b in streams))
        for b in streams:
            dcarry[b] = ends[b]
        for n in range(nc):
            rows = slice(n * CHUNK, (n + 1) * CHUNK)
            for b in streams:
                qd_b, ki_b, ke_b, v_b = bf[b]
                dst_b = dst_scr[b, n].astype(BF16)
                dvi_scr[b, rows, :] = lax.dot_general(ke_b[rows], dst_b, NT_DIMS, preferred_element_type=F32)
                dke_scr[b, rows, :] = jnp.dot(v_b[rows], dst_b, preferred_element_type=F32)
                dqi_scr[b, rows, :] = jnp.dot(keep[b]["do_b"][rows], st_ref[b, 0, n],
                                              preferred_element_type=F32)
        for b in streams:
            c, k = cs[b], keep[b]
            ddec = jnp.sum(dst_scr[b] * st_ref[b, 0].astype(F32), axis=1)
            dgl = jnp.broadcast_to(ddec[:, None, :], (nc, CHUNK, HEAD)).reshape(sb, HEAD) * c["dec"]
            dqd = k["dqd_intra"] + dqi_scr[b]
            dke = dke_scr[b]
            dki = k["dki"]
            t_ke = dke * c["ke"]
            dG = dqd * c["qd"] - dki * c["ki"] - t_ke
            dgl = dgl + _chunk_cumsum(t_ke) + _chunk_cumsum(t_ke, reverse=True) - t_ke
            dlogf = _chunk_cumsum(dG, reverse=True) + dgl
            dk = dki * c["e_ng"] + dke * c["e_ge"]
            df = dlogf / c["f"] - dk
            sg, sq, lb, q = c["sg"], c["sq"], c["lb"], qs[b]
            vec_ref[0:1, :] += jnp.sum(df * (1.0 - sg), axis=0, keepdims=True)
            d_ref[0, b] = (dqd * c["e_g"] * Q_SCALE * (sq + q * sq * (1.0 - sq))).astype(BF16)
            d_ref[1, b] = (df * (1.0 - lb) * sg * (1.0 - sg)).astype(BF16)
            d_ref[2, b] = (k["dv_intra"] + dvi_scr[b]).astype(BF16)
            d_ref[3, b] = k["d_og"].astype(BF16)

    rev = lambda s: nsb - 1 - s
    big = pltpu.VMEM((Bl, nc, HEAD, HEAD), F32)
    rows_f32 = pltpu.VMEM((Bl, sb, HEAD), F32)
    return _call_after(
        dep, body, (proj5, lb_logits, gn, dain3, o3, st_all), name="hgrn_bwd", grid=(H, nsb),
        in_specs=[pl.BlockSpec((4, Bl, sb, HEAD), lambda h, s: (0, 0, rev(s), h)),
                  pl.BlockSpec((2, HEAD), lambda h, s: (0, h)),
                  pl.BlockSpec((1, HEAD), lambda h, s: (0, h)),
                  pl.BlockSpec((Bl, sb, HEAD), lambda h, s: (0, rev(s), h)),
                  pl.BlockSpec((Bl, sb, HEAD), lambda h, s: (0, rev(s), h)),
                  pl.BlockSpec((Bl, 1, nc, HEAD, HEAD), lambda h, s: (0, h, rev(s), 0, 0))],
        out_specs=[pl.BlockSpec((4, Bl, sb, HEAD), lambda h, s: (0, 0, rev(s), h)),
                   pl.BlockSpec((8, HEAD), lambda h, s: (0, h))],
        out_shape=[jax.ShapeDtypeStruct((4, Bl, S, D), BF16), jax.ShapeDtypeStruct((8, D), F32)],
        scratch_shapes=[pltpu.VMEM((Bl, HEAD, HEAD), F32), big, big, rows_f32, rows_f32, rows_f32, rows_f32],
        compiler_params=_params(("parallel", "arbitrary")))


def _dx(d1, dh4, dpv, dg2, w_in, dep):
    T, D = d1.shape
    tm = min(ROW_TILE, T)

    def body(d1_ref, dh_ref, dp_ref, dg_ref, w_ref, o_ref):
        blocks = [dh_ref[0], dh_ref[1], dh_ref[2], dh_ref[3], dp_ref[...], dg_ref[0], dg_ref[1]]
        acc = ALPHA * d1_ref[...]
        for j, blk in enumerate(blocks):
            acc = acc + lax.dot_general(blk, w_ref[:, j * D:(j + 1) * D], NT_DIMS, preferred_element_type=F32)
        o_ref[...] = acc

    row = lambda i: (i, 0)
    return _call_after(
        dep, body, (d1, dh4, dpv, dg2, w_in), name="dx", grid=(T // tm,),
        in_specs=[pl.BlockSpec((tm, D), row), pl.BlockSpec((4, tm, D), lambda i: (0, i, 0)),
                  pl.BlockSpec((tm, D), row), pl.BlockSpec((2, tm, D), lambda i: (0, i, 0)),
                  _resident((D, N_SEC * D))],
        out_specs=pl.BlockSpec((tm, D), row),
        out_shape=jax.ShapeDtypeStruct((T, D), F32),
        compiler_params=_params(("parallel",)))


def _dw_in_part(name, x_t, b, sections, first_sec, into, dep, ob_shape, ob_first):
    D, T = x_t.shape
    per = D // DW_COLS
    b_spec = (pl.BlockSpec((1, T, DW_COLS), lambda j: (j // per, 0, j % per)) if b.ndim == 3
              else pl.BlockSpec((T, DW_COLS), lambda j: (0, j)))
    return _dw(name, x_t, b, sections * per, _resident((D, T)), b_spec, (D, N_SEC * D), (D, DW_COLS),
               lambda j: (0, first_sec * per + j), dep=dep, into=into, ob_shape=ob_shape,
               ob_map=lambda j: (0, ob_first * per + j))


def _dw_in_early(x_t, dpv, dg2, dep):
    D = x_t.shape[0]
    early_shape = (D, (N_SEC - EARLY_SEC) * D)
    f32, bf = _dw_in_part("dw_in_gates", x_t, dg2, 2, 5, (None, None), dep, early_shape, 1)
    return _dw_in_part("dw_in_pool", x_t, dpv, 1, 4, (f32, bf), None, early_shape, 0)


def _dw_in_late(x_t, dh4, f32_early, dep):
    D = x_t.shape[0]
    return _dw_in_part("dw_in_rec", x_t, dh4, EARLY_SEC, 0, (f32_early, None), dep, (D, EARLY_SEC * D), 0)


def _adam_shard(name, me_arr, grad, land, layout, w, m, v):
    shape = layout.shape
    n_split = 4
    blk = (shape[0] // n_split,) + shape[1:]
    zeros = (0,) * (len(shape) - 1)

    def body(me_ref, g_ref, r_ref, w_ref, m_ref, v_ref, g_out, d_out, m_out, v_out):
        g = g_ref[...]
        for k in range(N_DEV - 1):
            g = g + r_ref[k].astype(F32)
        d, m2, v2 = _adamw(w_ref[...], g, m_ref[...], v_ref[...])
        g_out[...] = g
        d_out[...] = d
        m_out[...] = m2
        v_out[...] = v2

    def own(i, me_ref):
        bi = layout.block_index(me_ref[0])
        return (bi[0] * n_split + i,) + tuple(bi[1:]) if layout.kind == "row" else (i,) + tuple(bi[1:])

    plain = pl.BlockSpec(blk, lambda i, me_ref: (i,) + zeros)
    grid_spec = pltpu.PrefetchScalarGridSpec(
        num_scalar_prefetch=1, grid=(n_split,),
        in_specs=[pl.BlockSpec(blk, own),
                  pl.BlockSpec((N_DEV - 1,) + blk, lambda i, me_ref: (0, i) + zeros),
                  plain, plain, plain],
        out_specs=[plain] * 4)
    return pl.pallas_call(
        body, name=name, grid_spec=grid_spec,
        out_shape=[jax.ShapeDtypeStruct(shape, F32)] * 4,
        compiler_params=_params(("parallel",)),
    )(me_arr, grad, land, w, m, v)


def _vec_allreduce(vec):
    D = vec.shape[1]

    def body(vec_ref, tot_ref, gat, send_sems, recv_sems):
        x, y, c = _me()
        me = 4 * x + 2 * y + c
        gat[me] = vec_ref[...]
        copies = []
        for k in range(1, N_DEV):
            cp = pltpu.make_async_remote_copy(
                src_ref=vec_ref, dst_ref=gat.at[me], send_sem=send_sems.at[k - 1],
                recv_sem=recv_sems.at[k - 1], device_id=_peer(k, x, y, c), device_id_type=MESH)
            cp.start()
            copies.append(cp)
        for cp in copies:
            cp.wait()
        tot = gat[0]
        for d in range(1, N_DEV):
            tot = tot + gat[d]
        tot_ref[...] = tot

    vm = pl.BlockSpec(memory_space=pltpu.VMEM)
    return pl.pallas_call(
        body, name="vec_allreduce", out_shape=jax.ShapeDtypeStruct(vec.shape, F32),
        in_specs=[vm], out_specs=vm,
        scratch_shapes=[pltpu.VMEM((N_DEV, 8, D), F32), pltpu.SemaphoreType.DMA((N_DEV - 1,)),
                        pltpu.SemaphoreType.DMA((N_DEV - 1,))],
    )(vec)


def _vec_adam(tot, small_w, small_m, small_v):
    n = len(small_w)

    def body(*refs):
        tot = refs[0][...]
        ws, ms, vs = refs[1:1 + n], refs[1 + n:1 + 2 * n], refs[1 + 2 * n:1 + 3 * n]
        outs = refs[1 + 3 * n:]
        loss_ref, g_out, d_out = outs[0], outs[1:1 + n], outs[1 + n:1 + 2 * n]
        m_out, v_out = outs[1 + 2 * n:1 + 3 * n], outs[1 + 3 * n:1 + 4 * n]
        loss_ref[...] = jnp.broadcast_to(jnp.sum(tot[7:8, :], axis=1, keepdims=True), loss_ref.shape)
        lbl = ws[0][...]
        mx = jnp.maximum(lbl[0:1, :], lbl[1:2, :])
        e0, e1 = jnp.exp(lbl[0:1, :] - mx), jnp.exp(lbl[1:2, :] - mx)
        p0 = e0 / (e0 + e1)
        dl0 = tot[0:1, :] * p0 * (1.0 - p0)
        grads = [jnp.concatenate([dl0, -dl0], axis=0)] + [tot[r:r + 1, :] for r in range(1, n)]
        for i in range(n):
            d, m2, v2 = _adamw(ws[i][...], grads[i], ms[i][...], vs[i][...])
            g_out[i][...] = grads[i]
            d_out[i][...] = d
            m_out[i][...] = m2
            v_out[i][...] = v2

    vm = pl.BlockSpec(memory_space=pltpu.VMEM)
    shapes = [jax.ShapeDtypeStruct(w.shape, F32) for w in small_w]
    return pl.pallas_call(
        body, name="vec_adam",
        out_shape=[jax.ShapeDtypeStruct((1, 128), F32)] + shapes * 4,
        in_specs=[vm] * (1 + 3 * n), out_specs=[vm] * (1 + 4 * n),
    )(tot, *small_w, *small_m, *small_v)


def kernel(x, w_in, lb_logits, hgrn_norm_g, w_a, w_pool, pool_scale, w_out, ln1_g, ln1_b, w_up, w_down, ln2_g, ln2_b, loss_target, m_w_in, m_lb_logits, m_hgrn_norm_g, m_w_a, m_w_pool, m_pool_scale, m_w_out, m_ln1_g, m_ln1_b, m_w_up, m_w_down, m_ln2_g, m_ln2_b, v_w_in, v_lb_logits, v_hgrn_norm_g, v_w_a, v_w_pool, v_pool_scale, v_w_out, v_ln1_g, v_ln1_b, v_w_up, v_w_down, v_ln2_g, v_ln2_b):
    Bl, S, D = x.shape
    T = Bl * S
    pg = D // POOL_GROUPS
    x2 = x.reshape(T, D)
    tgt = loss_target.reshape(T, D)
    me = 4 * lax.axis_index("x") + 2 * lax.axis_index("y") + lax.axis_index("c")
    me_arr = jnp.reshape(me, (1,)).astype(jnp.int32)

    names = ["w_in", "w_a", "w_pool", "w_out", "w_up", "w_down"]
    big_w = dict(zip(names, [w_in[0], w_a[0], w_pool[0], w_out[0], w_up[0], w_down[0]]))
    big_m = dict(zip(names, [m_w_in[0], m_w_a[0], m_w_pool[0], m_w_out[0], m_w_up[0], m_w_down[0]]))
    big_v = dict(zip(names, [v_w_in[0], v_w_a[0], v_w_pool[0], v_w_out[0], v_w_up[0], v_w_down[0]]))
    kinds = dict(w_in="col", w_a="row", w_pool="pool", w_out="row", w_up="col", w_down="row")
    lay = {nm: _Sharded(kinds[nm], big_w[nm].shape) for nm in names}
    wb = {nm: big_w[nm].astype(BF16) for nm in names}

    (w_in_f,) = _all_gather("ag_w_in", [wb["w_in"]], [lay["w_in"]])
    def gather_start(name, nms, after):
        return _exchange_start(name, [wb[nm] for nm in nms], [lax.empty(lay[nm].full_shape, BF16) for nm in nms],
                               src_at=lambda w, ref, peer: ref,
                               dst_at=lambda w, ref, mine, k: lay[nms[w]].at(ref, mine), after=after, own=True)

    ag_mix = gather_start("ag_mix", ["w_a", "w_pool", "w_out"], w_in_f)
    ag_mlp = gather_start("ag_mlp", ["w_up", "w_down"], ag_mix["token"])

    proj, x_t = _proj(x2, w_in_f, ag_mlp["token"])
    proj5 = proj.reshape(N_SEC, Bl, S, D)
    ain3, ain_t, o3, st_all = _hgrn_fwd(proj5, lb_logits, hgrn_norm_g)
    w_a_f, w_pool_f, w_out_f = _exchange_wait(ag_mix, ain3)
    pooled_t, bp3 = _pool_fwd(proj5, w_pool_f)
    ain, bp = ain3.reshape(T, D), bp3.reshape(T, D)
    a, merged_t, xhat1, rs1, x1b, x1_t = _mix_fwd(ain, proj, bp, x2, w_a_f, w_out_f, pool_scale, ln1_g, ln1_b)
    w_up_f, w_down_f = _exchange_wait(ag_mlp, x1b)
    hp, h, dr2, dr2b, dr2_t, vec_mlp = _mlp_fwd(x1b, w_up_f, w_down_f, xhat1, tgt, ln1_g, ln1_b, ln2_g, ln2_b)

    def scatter_start(name, nms, grads_b, after):
        lands = [lax.empty((N_DEV - 1,) + lay[nm].shape, BF16) for nm in nms]
        return _exchange_start(name, grads_b, lands,
                               src_at=lambda w, ref, peer: lay[nms[w]].at(ref, peer),
                               dst_at=lambda w, ref, mine, k: ref.at[k - 1], after=after)

    dhp, dr1, dr1b, vec_ln1 = _mlp_bwd(dr2b, dr2, hp, w_up_f, w_down_f, xhat1, rs1, ln1_g)
    FF = 4 * D
    whole_t = _resident((D, T))
    cols_b = pl.BlockSpec((T, DW_COLS), lambda j: (0, j))
    cols_o = ((D, DW_COLS), lambda j: (0, j))
    gw, gwb = {}, {}
    gw["w_down"], gwb["w_down"] = _dw(
        "dw_down", dr2_t, h, FF // DW_COLS, whole_t, cols_b, (FF, D), (DW_COLS, D), lambda j: (j, 0),
        transpose_out=True)
    rs_down = scatter_start("rs_w_down", ["w_down"], [gwb["w_down"]], gw["w_down"])
    gw["w_up"], gwb["w_up"] = _dw("dw_up", x1_t, dhp, FF // DW_COLS, whole_t, cols_b, (D, FF), *cols_o,
                                  dep=rs_down["token"])
    rs_up = scatter_start("rs_w_up", ["w_up"], [gwb["w_up"]], gw["w_up"])
    da_b, dbp_b, dain, dpooled, dg2, vec_mix = _mix_bwd(dr1b, proj, a, bp, w_a_f, w_out_f, w_pool_f, pool_scale,
                                                        rs_up["token"])
    gw["w_out"], gwb["w_out"] = _dw("dw_out", merged_t, dr1b, D // DW_COLS, whole_t, cols_b, (D, D), *cols_o)
    gw["w_a"], gwb["w_a"] = _dw("dw_a", ain_t, da_b, D // DW_COLS, _resident((Bl, D, S)), cols_b, (D, D), *cols_o)
    gw["w_pool"], gwb["w_pool"] = _dw(
        "dw_pool", pooled_t, dbp_b, POOL_GROUPS, pl.BlockSpec((pg, T), lambda j: (j, 0)),
        pl.BlockSpec((T, pg), lambda j: (0, j)), (POOL_GROUPS, pg, pg), (1, pg, pg), lambda j: (j, 0, 0))
    mid = ["w_out", "w_a", "w_pool"]
    rs_mid = scatter_start("rs_w_mid", mid, [gwb[nm] for nm in mid], gw["w_pool"])
    dpv = _pool_bwd(dpooled.reshape(Bl, S, D), rs_mid["token"]).reshape(T, D)
    gw_in_early, gwb_in_early = _dw_in_early(x_t, dpv, dg2, rs_mid["token"])
    land_in = lax.empty((N_DEV - 1,) + lay["w_in"].shape, BF16)
    rs_in_early = _w_in_scatter_start("rs_w_in_early", gwb_in_early, land_in, True, gw_in_early)
    dh4, vec_hgrn = _hgrn_bwd(proj5, lb_logits, hgrn_norm_g, dain.reshape(Bl, S, D), o3, st_all,
                              rs_in_early["token"])
    dh4 = dh4.reshape(4, T, D)
    vec_tot = _vec_allreduce(vec_mlp + vec_ln1 + vec_mix + vec_hgrn)
    gw["w_in"], gwb_in_late = _dw_in_late(x_t, dh4, gw_in_early, vec_tot)
    rs_in_late = _w_in_scatter_start("rs_w_in_late", gwb_in_late, rs_in_early["land"], False, gw["w_in"])
    grad_x2 = _dx(dr1, dh4, dpv, dg2, w_in_f, rs_in_late["token"])
    grad_x = grad_x2.reshape(Bl, S, D)

    small_names =["lb_logits", "hgrn_norm_g", "pool_scale", "ln1_g", "ln1_b", "ln2_g", "ln2_b"]
    small_w = [lb_logits, hgrn_norm_g, pool_scale, ln1_g, ln1_b, ln2_g, ln2_b]
    small_m = [m_lb_logits, m_hgrn_norm_g, m_pool_scale, m_ln1_g, m_ln1_b, m_ln2_g, m_ln2_b]
    small_v = [v_lb_logits, v_hgrn_norm_g, v_pool_scale, v_ln1_g, v_ln1_b, v_ln2_g, v_ln2_b]
    res = _vec_adam(vec_tot, small_w, small_m, small_v)
    loss = res[0][0, 0]
    n = len(small_w)
    small = {nm: (res[1 + i], res[1 + n + i], res[1 + 2 * n + i], res[1 + 3 * n + i])
             for i, nm in enumerate(small_names)}

    big, last = {}, grad_x2

    def adam(nm, land):
        outs = _adam_shard("adam_" + nm, me_arr, gw[nm], land, lay[nm], big_w[nm], big_m[nm], big_v[nm])
        big[nm] = tuple(t[None] for t in outs)
        return outs[0]

    for pend, nms in ((rs_down, ["w_down"]), (rs_up, ["w_up"]), (rs_mid, mid)):
        for nm, land in zip(nms, _exchange_wait(pend, last)):
            last = adam(nm, land)
    land_in = _w_in_scatter_wait(rs_in_early, rs_in_late["land"], last)
    adam("w_in", _w_in_scatter_wait(rs_in_late, land_in, res[0]))

    order = ["w_in", "lb_logits", "hgrn_norm_g", "w_a", "w_pool", "pool_scale", "w_out", "ln1_g", "ln1_b",
             "w_up", "w_down", "ln2_g", "ln2_b"]
    allp = {**big, **small}
    out = [loss, grad_x]
    for part in range(4):
        out += [allp[nm][part] for nm in order]
    return tuple(out)
```

```python
import jax
import jax.numpy as jnp
from jax import lax
from jax.experimental import pallas as pl
from jax.experimental.pallas import tpu as pltpu

F32 = jnp.float32
BF16 = jnp.bfloat16
MESH = pl.DeviceIdType.MESH

N_DEV = 8
HEAD = 128
CHUNK = 16
SUBLANES = 8
GROUP = 128
SUB_BLOCK = 1024
ROW_TILE = 512
MLP_ROW_TILE = 256
DW_COLS = 512
EARLY_SEC = 4
CH_PER_GROUP = GROUP // CHUNK
N_SEC = 7
POOL_GROUPS = 4
ALPHA = (2.0 * 1) ** 0.25
LN_EPS = 1e-5
RMS_EPS = 1e-6
Q_SCALE = HEAD ** -0.5
ADAM_LR = 0.001
ADAM_B1 = 0.9
ADAM_B2 = 0.999
ADAM_EPS = 1e-08
ADAM_WD = 0.01
ADAM_STEP = 10
VMEM_LIMIT = 60 << 20

NT_DIMS = (((1,), (1,)), ((), ()))
TN_DIMS = (((0,), (0,)), ((), ()))


def _params(sem=None):
    kw = dict(vmem_limit_bytes=VMEM_LIMIT)
    if sem is not None:
        kw["dimension_semantics"] = sem
    return pltpu.CompilerParams(**kw)


def _me():
    return lax.axis_index("x"), lax.axis_index("y"), lax.axis_index("c")


def _sigmoid(v):
    return jax.nn.sigmoid(v)


def _adamw(w, g, m, v):
    m = ADAM_B1 * m + (1.0 - ADAM_B1) * g
    v = ADAM_B2 * v + (1.0 - ADAM_B2) * jnp.square(g)
    m_hat = m / (1.0 - ADAM_B1 ** ADAM_STEP)
    v_hat = v / (1.0 - ADAM_B2 ** ADAM_STEP)
    delta = -ADAM_LR * (m_hat / (jnp.sqrt(v_hat) + ADAM_EPS) + ADAM_WD * w)
    return delta, m, v


class _Sharded:
    def __init__(self, kind, shard_shape):
        self.kind, self.shape = kind, tuple(shard_shape)

    @property
    def full_shape(self):
        r = self.shape
        if self.kind == "row":
            return (N_DEV * r[0],) + r[1:]
        return (r[0], N_DEV * r[1]) + r[2:]

    def at(self, ref, d):
        if self.kind == "col":
            n = self.shape[1]
            return ref.at[:, pl.ds(pl.multiple_of(d * n, 128), n)]
        if self.kind == "row":
            n = self.shape[0]
            return ref.at[pl.ds(pl.multiple_of(d * n, 16), n), :]
        n = self.shape[1]
        return ref.at[:, pl.ds(pl.multiple_of(d * n, 16), n), :]

    def block_index(self, d):
        return {"col": (0, d), "row": (d, 0), "pool": (0, d, 0)}[self.kind]


def _peer(k, x, y, c):
    return (1 - x if k & 4 else x, 1 - y if k & 2 else y, 1 - c if k & 1 else c)


def _all_gather(name, shards, layouts):
    nw = len(shards)

    def body(*refs):
        ins, outs = refs[:nw], refs[nw:2 * nw]
        send_sems, recv_sems, local_sems = refs[2 * nw:]
        x, y, c = _me()
        me = (x, y, c)
        sibling = (x, y, 1 - c)
        chips = [(1 - x, y), (x, 1 - y), (1 - x, 1 - y)]

        def copy(w, k, block, to, src=None):
            px, py, pc = block
            dst = layouts[w].at(outs[w], 4 * px + 2 * py + pc)
            return pltpu.make_async_remote_copy(
                src_ref=dst if src is None else src, dst_ref=dst,
                send_sem=send_sems.at[w, k], recv_sem=recv_sems.at[w, k],
                device_id=to, device_id_type=MESH)

        def place(w):
            mine = pltpu.make_async_copy(ins[w], layouts[w].at(outs[w], 4 * x + 2 * y + c), local_sems.at[w])
            mine.start()
            return mine

        north, south = c == 1, c == 0
        first, diagonal = [], []
        for w in range(nw):
            first.append(copy(w, 0, me, sibling, src=ins[w]))
            first += [copy(w, 1 + j, me, (*chips[j], c), src=ins[w]) for j in range(2)]
            diagonal.append(copy(w, 3, me, (*chips[2], c), src=ins[w]))
        for cp in first:
            cp.start()
        for cp in diagonal:
            pl.when(north)(cp.start)
        local = [place(w) for w in range(nw)]
        passed, relayed = [], []
        for w in range(nw):
            for j in (0, 1, 2):
                copy(w, 1 + j, (*chips[j], c), me).wait_recv()
                fwd = copy(w, 4 + j, (*chips[j], c), sibling)
                fwd.start()
                passed.append(fwd)
                if j == 0:
                    on = copy(w, 3, (*chips[0], c), (*chips[1], c))
                    pl.when(south)(on.start)
                    relayed.append(on)
        for w in range(nw):
            copy(w, 0, sibling, me).wait_recv()
            for j, chip in enumerate(chips):
                copy(w, 4 + j, (*chip, 1 - c), me).wait_recv()
        for cp in first + passed:
            cp.wait_send()
        for cp in diagonal:
            pl.when(north)(cp.wait_send)
        for cp in relayed:
            pl.when(south)(cp.wait_send)
        for cp in local:
            cp.wait()

    any_spec = pl.BlockSpec(memory_space=pl.ANY)
    return pl.pallas_call(
        body, name=name,
        out_shape=[jax.ShapeDtypeStruct(l.full_shape, s.dtype) for s, l in zip(shards, layouts)],
        in_specs=[any_spec] * nw, out_specs=[any_spec] * nw,
        scratch_shapes=[pltpu.SemaphoreType.DMA((nw, 7)), pltpu.SemaphoreType.DMA((nw, 7)),
                        pltpu.SemaphoreType.DMA((nw,))],
    )(*shards)


HBM_SPEC = pl.BlockSpec(memory_space=pltpu.HBM)
SEM_SPEC = pl.BlockSpec(memory_space=pltpu.SEMAPHORE)
DATAFLOW = pltpu.SideEffectType.DATAFLOW_SIDE_EFFECTING


def _exchange_copies(srcs, lands, send_sems, recv_sems, src_at, dst_at):
    x, y, c = _me()
    me = 4 * x + 2 * y + c
    copies = []
    for w in range(len(srcs)):
        for k in range(1, N_DEV):
            px, py, pc = _peer(k, x, y, c)
            copies.append(pltpu.make_async_remote_copy(
                src_ref=src_at(w, srcs[w], 4 * px + 2 * py + pc), dst_ref=dst_at(w, lands[w], me, k),
                send_sem=send_sems.at[w * (N_DEV - 1) + k - 1], recv_sem=recv_sems.at[w * (N_DEV - 1) + k - 1],
                device_id=(px, py, pc), device_id_type=MESH))
    return copies


def _own_copies(srcs, lands, own_sems, src_at, dst_at):
    x, y, c = _me()
    me = 4 * x + 2 * y + c
    return [pltpu.make_async_copy(src_at(w, srcs[w], me), dst_at(w, lands[w], me, 0), own_sems.at[w])
            for w in range(len(srcs))]


def _exchange_start(name, srcs, lands, src_at, dst_at, after, own=False):
    nw = len(srcs)

    def body(*refs):
        src_refs, land_refs = refs[:nw], refs[nw:2 * nw]
        send_sems, recv_sems, own_sems = refs[2 * nw + 1], refs[2 * nw + 2], refs[2 * nw + 3]
        token = refs[-1]
        for cp in _exchange_copies(src_refs, land_refs, send_sems, recv_sems, src_at, dst_at):
            cp.start()
        if own:
            for cp in _own_copies(src_refs, land_refs, own_sems, src_at, dst_at):
                cp.start()
        token[...] = jnp.zeros_like(token)

    hbm = lambda a: pltpu.HBM(a.shape, a.dtype)
    outs = pl.pallas_call(
        body, name=name,
        out_shape=(pltpu.SemaphoreType.DMA((nw * (N_DEV - 1),)), pltpu.SemaphoreType.DMA((nw * (N_DEV - 1),)),
                   pltpu.SemaphoreType.DMA((nw,)), *[hbm(a) for a in srcs], *[hbm(a) for a in lands],
                   jax.ShapeDtypeStruct((8, 128), F32)),
        in_specs=[HBM_SPEC] * (2 * nw) + [pl.BlockSpec(memory_space=pl.ANY)],
        out_specs=(SEM_SPEC, SEM_SPEC, SEM_SPEC, *[HBM_SPEC] * (2 * nw), pl.BlockSpec(memory_space=pltpu.VMEM)),
        input_output_aliases={i: 3 + i for i in range(2 * nw)},
        compiler_params=pltpu.CompilerParams(has_side_effects=DATAFLOW),
    )(*[pltpu.with_memory_space_constraint(a, pltpu.HBM) for a in list(srcs) + list(lands)], after)
    return dict(send=outs[0], recv=outs[1], own_sems=outs[2], srcs=outs[3:3 + nw], lands=outs[3 + nw:3 + 2 * nw],
                token=outs[-1], src_at=src_at, dst_at=dst_at, name=name, own=own)


def _exchange_wait(pending, after):
    nw = len(pending["srcs"])

    def body(*refs):
        src_refs, land_refs = refs[:nw], refs[nw:2 * nw]
        send_sems, recv_sems, own_sems = refs[2 * nw], refs[2 * nw + 1], refs[2 * nw + 2]
        for cp in _exchange_copies(src_refs, land_refs, send_sems, recv_sems,
                                   pending["src_at"], pending["dst_at"]):
            cp.wait_send()
            cp.wait_recv()
        if pending["own"]:
            for cp in _own_copies(src_refs, land_refs, own_sems, pending["src_at"], pending["dst_at"]):
                cp.wait()

    hbm = lambda a: pltpu.HBM(a.shape, a.dtype)
    outs = pl.pallas_call(
        body, name=pending["name"] + "_wait",
        out_shape=(*[hbm(a) for a in pending["srcs"]], *[hbm(a) for a in pending["lands"]]),
        in_specs=[HBM_SPEC] * (2 * nw) + [SEM_SPEC, SEM_SPEC, SEM_SPEC, pl.BlockSpec(memory_space=pl.ANY)],
        out_specs=tuple([HBM_SPEC] * (2 * nw)),
        input_output_aliases={i: i for i in range(2 * nw)},
        compiler_params=pltpu.CompilerParams(has_side_effects=DATAFLOW),
    )(*pending["srcs"], *pending["lands"], pending["send"], pending["recv"], pending["own_sems"], after)
    return outs[nw:]


def _w_in_scatter_copies(src, land, send_sems, recv_sems, early):
    rows, cols = land.shape[1], land.shape[2]
    bound = EARLY_SEC * rows
    cut_dev = bound // cols
    cut = bound - cut_dev * cols
    x, y, c = _me()
    me = 4 * x + 2 * y + c

    def pieces(t):
        if early:
            return [(t > cut_dev, t * cols - bound, cols, 0), (t == cut_dev, 0, cols - cut, cut)]
        return [(t < cut_dev, t * cols, cols, 0), (t == cut_dev, cut_dev * cols, cut, 0)]

    out = []
    for k in range(1, N_DEV):
        px, py, pc = _peer(k, x, y, c)
        for (to_peer, s0, width, d0), (to_me, _, _, _) in zip(pieces(4 * px + 2 * py + pc), pieces(me)):
            s0 = s0 if isinstance(s0, int) else pl.multiple_of(jnp.maximum(s0, 0), 128)
            out.append((to_peer, to_me, pltpu.make_async_remote_copy(
                src_ref=src.at[:, pl.ds(s0, width)], dst_ref=land.at[k - 1, :, pl.ds(d0, width)],
                send_sem=send_sems.at[k - 1], recv_sem=recv_sems.at[k - 1],
                device_id=(px, py, pc), device_id_type=MESH)))
    return out


def _w_in_scatter_start(name, src, land, early, after):
    def body(src_ref, land_ref, after_ref, send_sems, recv_sems, src_thru, land_thru, token):
        for to_peer, _, cp in _w_in_scatter_copies(src_ref, land_ref, send_sems, recv_sems, early):
            pl.when(to_peer)(cp.start)
        token[...] = jnp.zeros_like(token)

    hbm = lambda a: pltpu.HBM(a.shape, a.dtype)
    outs = pl.pallas_call(
        body, name=name,
        out_shape=(pltpu.SemaphoreType.DMA((N_DEV - 1,)), pltpu.SemaphoreType.DMA((N_DEV - 1,)),
                   hbm(src), hbm(land), jax.ShapeDtypeStruct((8, 128), F32)),
        in_specs=[HBM_SPEC, HBM_SPEC, pl.BlockSpec(memory_space=pl.ANY)],
        out_specs=(SEM_SPEC, SEM_SPEC, HBM_SPEC, HBM_SPEC, pl.BlockSpec(memory_space=pltpu.VMEM)),
        input_output_aliases={0: 2, 1: 3},
        compiler_params=pltpu.CompilerParams(has_side_effects=DATAFLOW),
    )(pltpu.with_memory_space_constraint(src, pltpu.HBM), pltpu.with_memory_space_constraint(land, pltpu.HBM), after)
    return dict(send=outs[0], recv=outs[1], src=outs[2], land=outs[3], token=outs[4], early=early, name=name)


def _w_in_scatter_wait(pending, land, after):
    def body(src_ref, land_ref, send_sems, recv_sems, after_ref, src_dead, land_out):
        for to_peer, to_me, cp in _w_in_scatter_copies(src_ref, land_ref, send_sems, recv_sems, pending["early"]):
            pl.when(to_peer)(cp.wait_send)
            pl.when(to_me)(cp.wait_recv)

    hbm = lambda a: pltpu.HBM(a.shape, a.dtype)
    outs = pl.pallas_call(
        body, name=pending["name"] + "_wait", out_shape=(hbm(pending["src"]), hbm(land)),
        in_specs=[HBM_SPEC, HBM_SPEC, SEM_SPEC, SEM_SPEC, pl.BlockSpec(memory_space=pl.ANY)],
        out_specs=(HBM_SPEC, HBM_SPEC), input_output_aliases={0: 0, 1: 1},
        compiler_params=pltpu.CompilerParams(has_side_effects=DATAFLOW),
    )(pending["src"], land, pending["send"], pending["recv"], after)
    return outs[1]


def _call_after(dep, body, args, *, in_specs, **kw):
    n_in = len(args)

    def wrapped(*refs):
        body(*refs[:n_in], *refs[n_in + 1:])

    dep_spec = pl.BlockSpec(dep.shape, lambda *_: (0,) * dep.ndim)
    return pl.pallas_call(wrapped, in_specs=list(in_specs) + [dep_spec], **kw)(*args, dep)


def _resident(shape):
    return pl.BlockSpec(shape, lambda *_: (0,) * len(shape), pipeline_mode=pl.Buffered(1))


def _proj(x2, w_in, dep):
    T, D = x2.shape
    tm = min(ROW_TILE, T)

    def body(x_ref, w_ref, o_ref, xt_ref):
        x = x_ref[...]
        xt_ref[...] = x.T.astype(BF16)
        xb = x.astype(BF16)
        for j in range(N_SEC):
            o_ref[j] = jnp.dot(xb, w_ref[:, j * D:(j + 1) * D], preferred_element_type=F32)

    return _call_after(
        dep, body, (x2, w_in), name="proj", grid=(T // tm,),
        in_specs=[pl.BlockSpec((tm, D), lambda i: (i, 0)), _resident((D, N_SEC * D))],
        out_specs=[pl.BlockSpec((N_SEC, tm, D), lambda i: (0, i, 0)), pl.BlockSpec((D, tm), lambda i: (0, i))],
        out_shape=[jax.ShapeDtypeStruct((N_SEC, T, D), F32), jax.ShapeDtypeStruct((D, T), BF16)],
        compiler_params=_params(("parallel",)))


def _chunk_cumsum(v, reverse=False):
    rows, lanes = v.shape
    x = v.reshape(rows // SUBLANES, SUBLANES, lanes)
    pos = lax.broadcasted_iota(jnp.int32, x.shape, 1)
    for sh in (1, 2, 4):
        if reverse:
            x = x + jnp.where(pos < SUBLANES - sh, pltpu.roll(x, SUBLANES - sh, 1), 0.0)
        else:
            x = x + jnp.where(pos >= sh, pltpu.roll(x, sh, 1), 0.0)
    x = x.reshape(rows // CHUNK, CHUNK // SUBLANES, SUBLANES, lanes)
    half = lax.broadcasted_iota(jnp.int32, x.shape, 1)
    if reverse:
        x = x + jnp.where(half == 0, x[:, 1:2, 0:1, :], 0.0)
    else:
        x = x + jnp.where(half == 1, x[:, 0:1, SUBLANES - 1:SUBLANES, :], 0.0)
    return x.reshape(rows, lanes)


def _hgrn_gates(q, f_pre, lb_logits):
    l0, l1 = lb_logits[0:1, :], lb_logits[1:2, :]
    mx = jnp.maximum(l0, l1)
    e0, e1 = jnp.exp(l0 - mx), jnp.exp(l1 - mx)
    lb = e0 / (e0 + e1)
    sq = _sigmoid(q)
    qf = q * sq * Q_SCALE
    sg = _sigmoid(f_pre)
    f = lb + (1.0 - lb) * sg
    k = 1.0 - f
    log_f = jnp.log(f)
    G = _chunk_cumsum(log_f)
    g_to_end = _chunk_cumsum(log_f, reverse=True) - log_f
    e_g = jnp.exp(G)
    e_ng = jnp.exp(-G)
    e_ge = jnp.exp(g_to_end)
    return dict(lb=lb, sq=sq, qf=qf, sg=sg, f=f, k=k, G=G, e_g=e_g, e_ng=e_ng, e_ge=e_ge,
                qd=qf * e_g, ki=k * e_ng, ke=k * e_ge, dec=e_g * e_ge)


def _intra_mask():
    r = lax.broadcasted_iota(jnp.int32, (GROUP, GROUP), 0)
    c = lax.broadcasted_iota(jnp.int32, (GROUP, GROUP), 1)
    return (r // CHUNK == c // CHUNK) & (c <= r)


def _chunk_outer(lhs_rows, rhs_b, out_scr, sb):
    lane = lax.broadcasted_iota(jnp.int32, (GROUP, GROUP), 1) // CHUNK
    for g in range(sb // GROUP):
        sl = slice(g * GROUP, (g + 1) * GROUP)
        lhs_t = lhs_rows[sl].T
        for cc in range(CH_PER_GROUP):
            masked = jnp.where(lane == cc, lhs_t, 0.0).astype(BF16)
            out_scr[g * CH_PER_GROUP + cc] = jnp.dot(masked, rhs_b[sl], preferred_element_type=F32)


def _hgrn_forward_blocks(cs, vs, st0s, sb, o_scr, kv_scr, st_scr, dec_scr):
    nc = sb // CHUNK
    n_str = len(cs)
    mask = _intra_mask()
    bf = []
    for i, (c, v) in enumerate(zip(cs, vs)):
        qd_b, ki_b, ke_b, v_b = (c["qd"].astype(BF16), c["ki"].astype(BF16), c["ke"].astype(BF16),
                                 v.astype(BF16))
        bf.append((qd_b, ki_b, ke_b, v_b))
        for g in range(sb // GROUP):
            sl = slice(g * GROUP, (g + 1) * GROUP)
            sc = lax.dot_general(qd_b[sl], ki_b[sl], NT_DIMS, preferred_element_type=F32)
            a = jnp.where(mask, sc, 0.0).astype(BF16)
            o_scr[i, sl, :] = jnp.dot(a, v_b[sl], preferred_element_type=F32)
        _chunk_outer(v, ke_b, kv_scr.at[i], sb)
        dec_scr[i] = c["dec"]

    def rec(n, sts):
        row = pl.ds(pl.multiple_of(n * CHUNK, CHUNK), 1)
        out = []
        for i in range(n_str):
            st_scr[i, n] = sts[i]
            out.append(sts[i] * dec_scr[i, row, :] + kv_scr[i, n])
        return tuple(out)

    ends = lax.fori_loop(0, nc, rec, tuple(st0s))

    for n in range(nc):
        rows = slice(n * CHUNK, (n + 1) * CHUNK)
        for i in range(n_str):
            o_scr[i, rows, :] += lax.dot_general(bf[i][0][rows], st_scr[i, n].astype(BF16), NT_DIMS,
                                                 preferred_element_type=F32)
    return ends, bf


def _hgrn_fwd(proj5, lb_logits, gn):
    _, Bl, S, D = proj5.shape
    H = D // HEAD
    sb = min(SUB_BLOCK, S)
    nsb = S // sb
    nc = sb // CHUNK

    def body(p_ref, lbl_ref, gn_ref, ain_ref, aint_ref, o_ref, st_ref, carry, o_scr, kv_scr, st_scr, dec_scr):
        @pl.when(pl.program_id(1) == 0)
        def _():
            carry[...] = jnp.zeros_like(carry)

        st0s = [carry[b] for b in range(Bl)]
        cs = [_hgrn_gates(p_ref[0, b], p_ref[1, b], lbl_ref[...]) for b in range(Bl)]
        ends, _ = _hgrn_forward_blocks(cs, [p_ref[2, b] for b in range(Bl)], st0s, sb,
                                       o_scr, kv_scr, st_scr, dec_scr)
        for b in range(Bl):
            carry[b] = ends[b]
            st_ref[b, 0] = st_scr[b].astype(BF16)
            o = o_scr[b]
            o_ref[b] = o
            rinv = lax.rsqrt(jnp.mean(o * o, axis=-1, keepdims=True) + RMS_EPS)
            ain = o * rinv * gn_ref[...] * _sigmoid(p_ref[3, b])
            ain_ref[b] = ain.astype(BF16)
            aint_ref[b] = ain.T.astype(BF16)

    return pl.pallas_call(
        body, name="hgrn_fwd", grid=(H, nsb),
        in_specs=[pl.BlockSpec((4, Bl, sb, HEAD), lambda h, s: (0, 0, s, h)),
                  pl.BlockSpec((2, HEAD), lambda h, s: (0, h)),
                  pl.BlockSpec((1, HEAD), lambda h, s: (0, h))],
        out_specs=[pl.BlockSpec((Bl, sb, HEAD), lambda h, s: (0, s, h)),
                   pl.BlockSpec((Bl, HEAD, sb), lambda h, s: (0, h, s)),
                   pl.BlockSpec((Bl, sb, HEAD), lambda h, s: (0, s, h)),
                   pl.BlockSpec((Bl, 1, nc, HEAD, HEAD), lambda h, s: (0, h, s, 0, 0))],
        out_shape=[jax.ShapeDtypeStruct((Bl, S, D), BF16), jax.ShapeDtypeStruct((Bl, D, S), BF16),
                   jax.ShapeDtypeStruct((Bl, S, D), F32),
                   jax.ShapeDtypeStruct((Bl, H, S // CHUNK, HEAD, HEAD), BF16)],
        scratch_shapes=[pltpu.VMEM((Bl, HEAD, HEAD), F32), pltpu.VMEM((Bl, sb, HEAD), F32),
                        pltpu.VMEM((Bl, nc, HEAD, HEAD), F32), pltpu.VMEM((Bl, nc, HEAD, HEAD), F32),
                        pltpu.VMEM((Bl, sb, HEAD), F32)],
        compiler_params=_params(("parallel", "arbitrary")),
    )(proj5, lb_logits, gn)


def _window_count(shape, g):
    pos = lax.broadcasted_iota(jnp.int32, shape, 0)
    return pos, jnp.minimum(pos + 1, jnp.left_shift(2, g)).astype(F32)


def _select_window(g, sums):
    return jnp.where(g == 0, sums[0], jnp.where(g == 1, sums[1], jnp.where(g == 2, sums[2], sums[3])))


def _pool_fwd(proj5, w_pool):
    _, Bl, S, D = proj5.shape
    pg = D // POOL_GROUPS

    def body(v_ref, w_ref, pooled_t_ref, bp_ref):
        g = pl.program_id(1)
        v = v_ref[0, 0]
        pos, cnt = _window_count(v.shape, g)
        cur, sums = v, []
        for sh in (1, 2, 4, 8):
            cur = cur + jnp.where(pos >= sh, pltpu.roll(cur, sh, 0), 0.0)
            sums.append(cur)
        pooled = _select_window(g, sums) / cnt - v
        pooled_t_ref[...] = pooled.T.astype(BF16)
        bp_ref[0] = jnp.dot(pooled.astype(BF16), w_ref[0], preferred_element_type=F32)

    return pl.pallas_call(
        body, name="pool_fwd", grid=(Bl, POOL_GROUPS),
        in_specs=[pl.BlockSpec((1, 1, S, pg), lambda b, g: (4, b, 0, g)),
                  pl.BlockSpec((1, pg, pg), lambda b, g: (g, 0, 0))],
        out_specs=[pl.BlockSpec((pg, S), lambda b, g: (g, b)),
                   pl.BlockSpec((1, S, pg), lambda b, g: (b, 0, g))],
        out_shape=[jax.ShapeDtypeStruct((D, Bl * S), BF16), jax.ShapeDtypeStruct((Bl, S, D), F32)],
        compiler_params=_params(("parallel", "parallel")),
    )(proj5, w_pool)


def _layer_norm_fwd(r):
    mu = jnp.mean(r, axis=-1, keepdims=True)
    d = r - mu
    rs = lax.rsqrt(jnp.mean(d * d, axis=-1, keepdims=True) + LN_EPS)
    return d * rs, rs


def _layer_norm_bwd(dy_g, xhat, rs):
    return rs * (dy_g - jnp.mean(dy_g, axis=-1, keepdims=True)
                 - xhat * jnp.mean(dy_g * xhat, axis=-1, keepdims=True))


def _mix_fwd(ain, proj, bp, x2, w_a, w_out, ps, g1, b1):
    T, D = x2.shape
    tm = min(ROW_TILE, T)

    def body(ain_ref, ga_ref, gb_ref, bp_ref, x_ref, wa_ref, wo_ref, ps_ref, g1_ref, b1_ref,
             a_ref, mgt_ref, xh_ref, rs_ref, x1b_ref, x1t_ref):
        a = jnp.dot(ain_ref[...], wa_ref[...], preferred_element_type=F32)
        a_ref[...] = a
        merged = _sigmoid(ga_ref[0]) * a + _sigmoid(gb_ref[0]) * (bp_ref[...] * ps_ref[...])
        mgt_ref[...] = merged.T.astype(BF16)
        r1 = ALPHA * x_ref[...] + jnp.dot(merged.astype(BF16), wo_ref[...], preferred_element_type=F32)
        xhat, rs = _layer_norm_fwd(r1)
        xh_ref[...] = xhat
        rs_ref[...] = rs
        x1 = xhat * g1_ref[...] + b1_ref[...]
        x1b_ref[...] = x1.astype(BF16)
        x1t_ref[...] = x1.T.astype(BF16)

    row = lambda i: (i, 0)
    col = lambda i: (0, i)
    full = lambda i: (0, 0)
    return pl.pallas_call(
        body, name="mix_fwd", grid=(T // tm,),
        in_specs=[pl.BlockSpec((tm, D), row),
                  pl.BlockSpec((1, tm, D), lambda i: (5, i, 0)),
                  pl.BlockSpec((1, tm, D), lambda i: (6, i, 0)),
                  pl.BlockSpec((tm, D), row), pl.BlockSpec((tm, D), row),
                  pl.BlockSpec((D, D), full), pl.BlockSpec((D, D), full),
                  pl.BlockSpec((1, D), full), pl.BlockSpec((1, D), full), pl.BlockSpec((1, D), full)],
        out_specs=[pl.BlockSpec((tm, D), row), pl.BlockSpec((D, tm), col), pl.BlockSpec((tm, D), row),
                   pl.BlockSpec((tm, 1), row), pl.BlockSpec((tm, D), row), pl.BlockSpec((D, tm), col)],
        out_shape=[jax.ShapeDtypeStruct((T, D), F32), jax.ShapeDtypeStruct((D, T), BF16),
                   jax.ShapeDtypeStruct((T, D), F32), jax.ShapeDtypeStruct((T, 1), F32),
                   jax.ShapeDtypeStruct((T, D), BF16), jax.ShapeDtypeStruct((D, T), BF16)],
        compiler_params=_params(("parallel",)),
    )(ain, proj, proj, bp, x2, w_a, w_out, ps, g1, b1)


def _mlp_fwd(x1b, w_up, w_down, xhat1, tgt, g1, b1, g2, b2):
    T, D = xhat1.shape
    FF = w_up.shape[1]
    tm = min(MLP_ROW_TILE, T)

    def body(x_ref, wu_ref, wd_ref, xh_ref, t_ref, g1_ref, b1_ref, g2_ref, b2_ref,
             hp_ref, h_ref, dr_ref, drb_ref, drt_ref, vec_ref):
        @pl.when(pl.program_id(0) == 0)
        def _():
            vec_ref[...] = jnp.zeros_like(vec_ref)

        xb = x_ref[...]
        x1 = xh_ref[...] * g1_ref[...] + b1_ref[...]
        r2 = ALPHA * x1
        for f in range(FF // D):
            cols = slice(f * D, (f + 1) * D)
            hp = jnp.dot(xb, wu_ref[:, cols], preferred_element_type=F32)
            hp_ref[:, cols] = hp
            h = jnp.square(jnp.maximum(hp, 0.0)).astype(BF16)
            h_ref[:, cols] = h
            r2 = r2 + jnp.dot(h, wd_ref[cols, :], preferred_element_type=F32)
        xhat2, rs2 = _layer_norm_fwd(r2)
        err = xhat2 * g2_ref[...] + b2_ref[...] - t_ref[...]
        dy = err / D
        vec_ref[5:6, :] += jnp.sum(dy * xhat2, axis=0, keepdims=True)
        vec_ref[6:7, :] += jnp.sum(dy, axis=0, keepdims=True)
        vec_ref[7:8, :] += jnp.sum(0.5 * err * err / D, axis=0, keepdims=True)
        dr = _layer_norm_bwd(dy * g2_ref[...], xhat2, rs2)
        dr_ref[...] = dr
        drb_ref[...] = dr.astype(BF16)
        drt_ref[...] = dr.T.astype(BF16)

    row = lambda i: (i, 0)
    full = lambda i: (0, 0)
    return pl.pallas_call(
        body, name="mlp_fwd", grid=(T // tm,),
        in_specs=[pl.BlockSpec((tm, D), row), _resident((D, FF)), _resident((FF, D)),
                  pl.BlockSpec((tm, D), row), pl.BlockSpec((tm, D), row),
                  pl.BlockSpec((1, D), full), pl.BlockSpec((1, D), full),
                  pl.BlockSpec((1, D), full), pl.BlockSpec((1, D), full)],
        out_specs=[pl.BlockSpec((tm, FF), row), pl.BlockSpec((tm, FF), row), pl.BlockSpec((tm, D), row),
                   pl.BlockSpec((tm, D), row), pl.BlockSpec((D, tm), lambda i: (0, i)),
                   pl.BlockSpec((8, D), full)],
        out_shape=[jax.ShapeDtypeStruct((T, FF), F32), jax.ShapeDtypeStruct((T, FF), BF16),
                   jax.ShapeDtypeStruct((T, D), F32), jax.ShapeDtypeStruct((T, D), BF16),
                   jax.ShapeDtypeStruct((D, T), BF16), jax.ShapeDtypeStruct((8, D), F32)],
        compiler_params=_params(("arbitrary",)),
    )(x1b, w_up, w_down, xhat1, tgt, g1, b1, g2, b2)


def _mlp_bwd(drb, dr, hp, w_up, w_down, xhat1, rs1, g1):
    T, D = dr.shape
    FF = hp.shape[1]
    tm = min(MLP_ROW_TILE, T)

    def body(drb_ref, dr_ref, hp_ref, wu_ref, wd_ref, xh_ref, rs_ref, g1_ref,
             dhp_ref, d1_ref, d1b_ref, vec_ref):
        @pl.when(pl.program_id(0) == 0)
        def _():
            vec_ref[...] = jnp.zeros_like(vec_ref)

        drb = drb_ref[...]
        dx1 = ALPHA * dr_ref[...]
        for f in range(FF // D):
            cols = slice(f * D, (f + 1) * D)
            dh = lax.dot_general(drb, wd_ref[cols, :], NT_DIMS, preferred_element_type=F32)
            dhp = (dh * (2.0 * jnp.maximum(hp_ref[:, cols], 0.0))).astype(BF16)
            dhp_ref[:, cols] = dhp
            dx1 = dx1 + lax.dot_general(dhp, wu_ref[:, cols], NT_DIMS, preferred_element_type=F32)
        xhat = xh_ref[...]
        vec_ref[3:4, :] += jnp.sum(dx1 * xhat, axis=0, keepdims=True)
        vec_ref[4:5, :] += jnp.sum(dx1, axis=0, keepdims=True)
        d1 = _layer_norm_bwd(dx1 * g1_ref[...], xhat, rs_ref[...])
        d1_ref[...] = d1
        d1b_ref[...] = d1.astype(BF16)

    row = lambda i: (i, 0)
    full = lambda i: (0, 0)
    return pl.pallas_call(
        body, name="mlp_bwd", grid=(T // tm,),
        in_specs=[pl.BlockSpec((tm, D), row), pl.BlockSpec((tm, D), row), pl.BlockSpec((tm, FF), row),
                  _resident((D, FF)), _resident((FF, D)),
                  pl.BlockSpec((tm, D), row), pl.BlockSpec((tm, 1), row), pl.BlockSpec((1, D), full)],
        out_specs=[pl.BlockSpec((tm, FF), row), pl.BlockSpec((tm, D), row), pl.BlockSpec((tm, D), row),
                   pl.BlockSpec((8, D), full)],
        out_shape=[jax.ShapeDtypeStruct((T, FF), BF16), jax.ShapeDtypeStruct((T, D), F32),
                   jax.ShapeDtypeStruct((T, D), BF16), jax.ShapeDtypeStruct((8, D), F32)],
        compiler_params=_params(("arbitrary",)),
    )(drb, dr, hp, w_up, w_down, xhat1, rs1, g1)


def _dw(name, a_t, b, n_j, a_spec, b_spec, o_shape, o_block, o_map, transpose_out=False, dep=None,
        into=(None, None), ob_shape=None, ob_map=None):
    def body(*refs):
        a_ref, b_ref, o_ref, ob_ref = refs[0], refs[1], refs[-2], refs[-1]
        b_val = b_ref[0] if len(b_ref.shape) == 3 else b_ref[...]
        if len(a_ref.shape) == 3:
            seq = a_ref.shape[2]
            p = sum(jnp.dot(a_ref[i], b_val[i * seq:(i + 1) * seq], preferred_element_type=F32)
                    for i in range(a_ref.shape[0]))
        else:
            p = jnp.dot(a_ref[...], b_val, preferred_element_type=F32)
        if transpose_out:
            p = p.T
        p = p.reshape(o_ref.shape)
        o_ref[...] = p
        ob_ref[...] = p.astype(BF16)

    kw = dict(name=name, grid=(n_j,), in_specs=[a_spec, b_spec],
              out_specs=[pl.BlockSpec(o_block, o_map), pl.BlockSpec(o_block, ob_map or o_map)],
              out_shape=[jax.ShapeDtypeStruct(o_shape, F32), jax.ShapeDtypeStruct(ob_shape or o_shape, BF16)],
              compiler_params=_params(("parallel",)))
    args = (a_t, b)
    aliases = {}
    for out_index, arr in enumerate(into):
        if arr is not None:
            aliases[len(args)] = out_index
            args = args + (arr,)
            kw["in_specs"] = kw["in_specs"] + [pl.BlockSpec(memory_space=pl.ANY)]
    if aliases:
        kw["input_output_aliases"] = aliases
    if dep is None:
        return pl.pallas_call(body, **kw)(*args)
    return _call_after(dep, body, args, **kw)


def _mix_bwd(d1b, proj, a, bp, w_a, w_out, w_pool, ps, dep):
    T, D = a.shape
    tm = min(ROW_TILE, T)
    pg = D // POOL_GROUPS

    def body(d1b_ref, ga_ref, gb_ref, a_ref, bp_ref, wa_ref, wo_ref, wp_ref, ps_ref,
             da_ref, dbp_ref, dain_ref, dpl_ref, dg_ref, vec_ref):
        @pl.when(pl.program_id(0) == 0)
        def _():
            vec_ref[...] = jnp.zeros_like(vec_ref)

        dm = lax.dot_general(d1b_ref[...], wo_ref[...], NT_DIMS, preferred_element_type=F32)
        sa, sg = _sigmoid(ga_ref[0]), _sigmoid(gb_ref[0])
        bp_v, ps_v = bp_ref[...], ps_ref[...]
        da = (dm * sa).astype(BF16)
        db = dm * sg
        dg_ref[0] = (dm * a_ref[...] * sa * (1.0 - sa)).astype(BF16)
        dg_ref[1] = (dm * (bp_v * ps_v) * sg * (1.0 - sg)).astype(BF16)
        vec_ref[2:3, :] += jnp.sum(db * bp_v, axis=0, keepdims=True)
        dbp = (db * ps_v).astype(BF16)
        da_ref[...] = da
        dbp_ref[...] = dbp
        dain_ref[...] = lax.dot_general(da, wa_ref[...], NT_DIMS, preferred_element_type=F32)
        for g in range(POOL_GROUPS):
            cols = slice(g * pg, (g + 1) * pg)
            dpl_ref[:, cols] = lax.dot_general(dbp[:, cols], wp_ref[g], NT_DIMS,
                                               preferred_element_type=F32)

    row = lambda i: (i, 0)
    full = lambda i: (0, 0)
    return _call_after(
        dep, body, (d1b, proj, proj, a, bp, w_a, w_out, w_pool, ps), name="mix_bwd", grid=(T // tm,),
        in_specs=[pl.BlockSpec((tm, D), row),
                  pl.BlockSpec((1, tm, D), lambda i: (5, i, 0)),
                  pl.BlockSpec((1, tm, D), lambda i: (6, i, 0)),
                  pl.BlockSpec((tm, D), row), pl.BlockSpec((tm, D), row),
                  pl.BlockSpec((D, D), full), pl.BlockSpec((D, D), full),
                  pl.BlockSpec((POOL_GROUPS, pg, pg), lambda i: (0, 0, 0)),
                  pl.BlockSpec((1, D), full)],
        out_specs=[pl.BlockSpec((tm, D), row), pl.BlockSpec((tm, D), row),
                   pl.BlockSpec((tm, D), row), pl.BlockSpec((tm, D), row),
                   pl.BlockSpec((2, tm, D), lambda i: (0, i, 0)),
                   pl.BlockSpec((8, D), full)],
        out_shape=[jax.ShapeDtypeStruct((T, D), BF16), jax.ShapeDtypeStruct((T, D), BF16),
                   jax.ShapeDtypeStruct((T, D), F32), jax.ShapeDtypeStruct((T, D), F32),
                   jax.ShapeDtypeStruct((2, T, D), BF16), jax.ShapeDtypeStruct((8, D), F32)],
        compiler_params=_params(("arbitrary",)))


def _pool_bwd(dpooled3, dep):
    Bl, S, D = dpooled3.shape
    pg = D // POOL_GROUPS

    def body(dp_ref, dv_ref):
        g = pl.program_id(1)
        dp = dp_ref[0]
        pos, cnt = _window_count(dp.shape, g)
        cur, sums = dp / cnt, []
        for sh in (1, 2, 4, 8):
            cur = cur + jnp.where(pos < S - sh, pltpu.roll(cur, S - sh, 0), 0.0)
            sums.append(cur)
        dv_ref[0] = (_select_window(g, sums) - dp).astype(BF16)

    spec = pl.BlockSpec((1, S, pg), lambda b, g: (b, 0, g))
    return _call_after(
        dep, body, (dpooled3,), name="pool_bwd", grid=(Bl, POOL_GROUPS), in_specs=[spec], out_specs=spec,
        out_shape=jax.ShapeDtypeStruct((Bl, S, D), BF16),
        compiler_params=_params(("parallel", "parallel")))


def _hgrn_bwd(proj5, lb_logits, gn, dain3, o3, st_all, dep):
    _, Bl, S, D = proj5.shape
    H = D // HEAD
    sb = min(SUB_BLOCK, S)
    nsb = S // sb
    nc = sb // CHUNK
    streams = range(Bl)

    def body(p_ref, lbl_ref, gn_ref, dain_ref, o_ref, st_ref, d_ref, vec_ref,
             dcarry, kv_scr, dst_scr, dec_scr, dvi_scr, dke_scr, dqi_scr):
        s = pl.program_id(1)

        @pl.when(s == 0)
        def _():
            dcarry[...] = jnp.zeros_like(dcarry)
            vec_ref[...] = jnp.zeros_like(vec_ref)

        qs, vs, ogs = [p_ref[0, b] for b in streams], [p_ref[2, b] for b in streams], [p_ref[3, b] for b in streams]
        cs = [_hgrn_gates(qs[b], p_ref[1, b], lbl_ref[...]) for b in streams]
        bf = [(cs[b]["qd"].astype(BF16), cs[b]["ki"].astype(BF16), cs[b]["ke"].astype(BF16),
               vs[b].astype(BF16)) for b in streams]
        mask = _intra_mask()
        gn_v = gn_ref[...]
        keep = []
        for b in streams:
            qd_b, ki_b, ke_b, v_b = bf[b]
            dec_scr[b] = cs[b]["dec"]
            o = o_ref[b]
            rinv = lax.rsqrt(jnp.mean(o * o, axis=-1, keepdims=True) + RMS_EPS)
            on = o * rinv
            so = _sigmoid(ogs[b])
            dain = dain_ref[b]
            vec_ref[1:2, :] += jnp.sum(dain * on * so, axis=0, keepdims=True)
            d_og = dain * on * gn_v * so * (1.0 - so)
            d_on = dain * gn_v * so
            do = rinv * (d_on - on * jnp.mean(d_on * on, axis=-1, keepdims=True))
            do_b = do.astype(BF16)
            dv_parts, dqd_parts, dki_parts = [], [], []
            for g in range(sb // GROUP):
                sl = slice(g * GROUP, (g + 1) * GROUP)
                sc = lax.dot_general(qd_b[sl], ki_b[sl], NT_DIMS, preferred_element_type=F32)
                a = jnp.where(mask, sc, 0.0).astype(BF16)
                da = lax.dot_general(do_b[sl], v_b[sl], NT_DIMS, preferred_element_type=F32)
                da = jnp.where(mask, da, 0.0).astype(BF16)
                dv_parts.append(lax.dot_general(a, do_b[sl], TN_DIMS, preferred_element_type=F32))
                dqd_parts.append(jnp.dot(da, ki_b[sl], preferred_element_type=F32))
                dki_parts.append(lax.dot_general(da, qd_b[sl], TN_DIMS, preferred_element_type=F32))
            keep.append(dict(d_og=d_og, do_b=do_b, dv_intra=jnp.concatenate(dv_parts, axis=0),
                             dqd_intra=jnp.concatenate(dqd_parts, axis=0),
                             dki=jnp.concatenate(dki_parts, axis=0)))
            _chunk_outer(do, qd_b, kv_scr.at[b], sb)

        def rrec(i, dsts):
            n = nc - 1 - i
            row = pl.ds(pl.multiple_of(n * CHUNK, CHUNK), 1)
            out = []
            for b in streams:
                dst_scr[b, n] = dsts[b]
                out.append(dsts[b] * dec_scr[b, row, :] + kv_scr[b, n])
            return tuple(out)

        ends = lax.fori_loop(0, nc, rrec, tuple(dcarry[b] for b in streams))
        for b in streams:
            dcarry[b] = ends[b]
        for n in range(nc):
            rows = slice(n * CHUNK, (n + 1) * CHUNK)
            for b in streams:
                qd_b, ki_b, ke_b, v_b = bf[b]
                dst_b = dst_scr[b, n].astype(BF16)
                dvi_scr[b, rows, :] = lax.dot_general(ke_b[rows], dst_b, NT_DIMS, preferred_element_type=F32)
                dke_scr[b, rows, :] = jnp.dot(v_b[rows], dst_b, preferred_element_type=F32)
                dqi_scr[b, rows, :] = jnp.dot(keep[b]["do_b"][rows], st_ref[b, 0, n],
                                              preferred_element_type=F32)
        for b in streams:
            c, k = cs[b], keep[b]
            ddec = jnp.sum(dst_scr[b] * st_ref[b, 0].astype(F32), axis=1)
            dgl = jnp.broadcast_to(ddec[:, None, :], (nc, CHUNK, HEAD)).reshape(sb, HEAD) * c["dec"]
            dqd = k["dqd_intra"] + dqi_scr[b]
            dke = dke_scr[b]
            dki = k["dki"]
            t_ke = dke * c["ke"]
            dG = dqd * c["qd"] - dki * c["ki"] - t_ke
            dgl = dgl + _chunk_cumsum(t_ke) + _chunk_cumsum(t_ke, reverse=True) - t_ke
            dlogf = _chunk_cumsum(dG, reverse=True) + dgl
            dk = dki * c["e_ng"] + dke * c["e_ge"]
            df = dlogf / c["f"] - dk
            sg, sq, lb, q = c["sg"], c["sq"], c["lb"], qs[b]
            vec_ref[0:1, :] += jnp.sum(df * (1.0 - sg), axis=0, keepdims=True)
            d_ref[0, b] = (dqd * c["e_g"] * Q_SCALE * (sq + q * sq * (1.0 - sq))).astype(BF16)
            d_ref[1, b] = (df * (1.0 - lb) * sg * (1.0 - sg)).astype(BF16)
            d_ref[2, b] = (k["dv_intra"] + dvi_scr[b]).astype(BF16)
            d_ref[3, b] = k["d_og"].astype(BF16)

    rev = lambda s: nsb - 1 - s
    big = pltpu.VMEM((Bl, nc, HEAD, HEAD), F32)
    rows_f32 = pltpu.VMEM((Bl, sb, HEAD), F32)
    return _call_after(
        dep, body, (proj5, lb_logits, gn, dain3, o3, st_all), name="hgrn_bwd", grid=(H, nsb),
        in_specs=[pl.BlockSpec((4, Bl, sb, HEAD), lambda h, s: (0, 0, rev(s), h)),
                  pl.BlockSpec((2, HEAD), lambda h, s: (0, h)),
                  pl.BlockSpec((1, HEAD), lambda h, s: (0, h)),
                  pl.BlockSpec((Bl, sb, HEAD), lambda h, s: (0, rev(s), h)),
                  pl.BlockSpec((Bl, sb, HEAD), lambda h, s: (0, rev(s), h)),
                  pl.BlockSpec((Bl, 1, nc, HEAD, HEAD), lambda h, s: (0, h, rev(s), 0, 0))],
        out_specs=[pl.BlockSpec((4, Bl, sb, HEAD), lambda h, s: (0, 0, rev(s), h)),
                   pl.BlockSpec((8, HEAD), lambda h, s: (0, h))],
        out_shape=[jax.ShapeDtypeStruct((4, Bl, S, D), BF16), jax.ShapeDtypeStruct((8, D), F32)],
        scratch_shapes=[pltpu.VMEM((Bl, HEAD, HEAD), F32), big, big, rows_f32, rows_f32, rows_f32, rows_f32],
        compiler_params=_params(("parallel", "arbitrary")))


def _dx(d1, dh4, dpv, dg2, w_in, dep):
    T, D = d1.shape
    tm = min(ROW_TILE, T)

    def body(d1_ref, dh_ref, dp_ref, dg_ref, w_ref, o_ref):
        blocks = [dh_ref[0], dh_ref[1], dh_ref[2], dh_ref[3], dp_ref[...], dg_ref[0], dg_ref[1]]
        acc = ALPHA * d1_ref[...]
        for j, blk in enumerate(blocks):
            acc = acc + lax.dot_general(blk, w_ref[:, j * D:(j + 1) * D], NT_DIMS, preferred_element_type=F32)
        o_ref[...] = acc

    row = lambda i: (i, 0)
    return _call_after(
        dep, body, (d1, dh4, dpv, dg2, w_in), name="dx", grid=(T // tm,),
        in_specs=[pl.BlockSpec((tm, D), row), pl.BlockSpec((4, tm, D), lambda i: (0, i, 0)),
                  pl.BlockSpec((tm, D), row), pl.BlockSpec((2, tm, D), lambda i: (0, i, 0)),
                  _resident((D, N_SEC * D))],
        out_specs=pl.BlockSpec((tm, D), row),
        out_shape=jax.ShapeDtypeStruct((T, D), F32),
        compiler_params=_params(("parallel",)))


def _dw_in_part(name, x_t, b, sections, first_sec, into, dep, ob_shape, ob_first):
    D, T = x_t.shape
    per = D // DW_COLS
    b_spec = (pl.BlockSpec((1, T, DW_COLS), lambda j: (j // per, 0, j % per)) if b.ndim == 3
              else pl.BlockSpec((T, DW_COLS), lambda j: (0, j)))
    return _dw(name, x_t, b, sections * per, _resident((D, T)), b_spec, (D, N_SEC * D), (D, DW_COLS),
               lambda j: (0, first_sec * per + j), dep=dep, into=into, ob_shape=ob_shape,
               ob_map=lambda j: (0, ob_first * per + j))


def _dw_in_early(x_t, dpv, dg2, dep):
    D = x_t.shape[0]
    early_shape = (D, (N_SEC - EARLY_SEC) * D)
    f32, bf = _dw_in_part("dw_in_gates", x_t, dg2, 2, 5, (None, None), dep, early_shape, 1)
    return _dw_in_part("dw_in_pool", x_t, dpv, 1, 4, (f32, bf), None, early_shape, 0)


def _dw_in_late(x_t, dh4, f32_early, dep):
    D = x_t.shape[0]
    return _dw_in_part("dw_in_rec", x_t, dh4, EARLY_SEC, 0, (f32_early, None), dep, (D, EARLY_SEC * D), 0)


def _adam_shard(name, me_arr, grad, land, layout, w, m, v):
    shape = layout.shape
    n_split = 4
    blk = (shape[0] // n_split,) + shape[1:]
    zeros = (0,) * (len(shape) - 1)

    def body(me_ref, g_ref, r_ref, w_ref, m_ref, v_ref, g_out, d_out, m_out, v_out):
        g = g_ref[...]
        for k in range(N_DEV - 1):
            g = g + r_ref[k].astype(F32)
        d, m2, v2 = _adamw(w_ref[...], g, m_ref[...], v_ref[...])
        g_out[...] = g
        d_out[...] = d
        m_out[...] = m2
        v_out[...] = v2

    def own(i, me_ref):
        bi = layout.block_index(me_ref[0])
        return (bi[0] * n_split + i,) + tuple(bi[1:]) if layout.kind == "row" else (i,) + tuple(bi[1:])

    plain = pl.BlockSpec(blk, lambda i, me_ref: (i,) + zeros)
    grid_spec = pltpu.PrefetchScalarGridSpec(
        num_scalar_prefetch=1, grid=(n_split,),
        in_specs=[pl.BlockSpec(blk, own),
                  pl.BlockSpec((N_DEV - 1,) + blk, lambda i, me_ref: (0, i) + zeros),
                  plain, plain, plain],
        out_specs=[plain] * 4)
    return pl.pallas_call(
        body, name=name, grid_spec=grid_spec,
        out_shape=[jax.ShapeDtypeStruct(shape, F32)] * 4,
        compiler_params=_params(("parallel",)),
    )(me_arr, grad, land, w, m, v)


def _vec_allreduce(vec):
    D = vec.shape[1]

    def body(vec_ref, tot_ref, gat, send_sems, recv_sems):
        x, y, c = _me()
        me = 4 * x + 2 * y + c
        gat[me] = vec_ref[...]
        copies = []
        for k in range(1, N_DEV):
            cp = pltpu.make_async_remote_copy(
                src_ref=vec_ref, dst_ref=gat.at[me], send_sem=send_sems.at[k - 1],
                recv_sem=recv_sems.at[k - 1], device_id=_peer(k, x, y, c), device_id_type=MESH)
            cp.start()
            copies.append(cp)
        for cp in copies:
            cp.wait()
        tot = gat[0]
        for d in range(1, N_DEV):
            tot = tot + gat[d]
        tot_ref[...] = tot

    vm = pl.BlockSpec(memory_space=pltpu.VMEM)
    return pl.pallas_call(
        body, name="vec_allreduce", out_shape=jax.ShapeDtypeStruct(vec.shape, F32),
        in_specs=[vm], out_specs=vm,
        scratch_shapes=[pltpu.VMEM((N_DEV, 8, D), F32), pltpu.SemaphoreType.DMA((N_DEV - 1,)),
                        pltpu.SemaphoreType.DMA((N_DEV - 1,))],
    )(vec)


def _vec_adam(tot, small_w, small_m, small_v):
    n = len(small_w)

    def body(*refs):
        tot = refs[0][...]
        ws, ms, vs = refs[1:1 + n], refs[1 + n:1 + 2 * n], refs[1 + 2 * n:1 + 3 * n]
        outs = refs[1 + 3 * n:]
        loss_ref, g_out, d_out = outs[0], outs[1:1 + n], outs[1 + n:1 + 2 * n]
        m_out, v_out = outs[1 + 2 * n:1 + 3 * n], outs[1 + 3 * n:1 + 4 * n]
        loss_ref[...] = jnp.broadcast_to(jnp.sum(tot[7:8, :], axis=1, keepdims=True), loss_ref.shape)
        lbl = ws[0][...]
        mx = jnp.maximum(lbl[0:1, :], lbl[1:2, :])
        e0, e1 = jnp.exp(lbl[0:1, :] - mx), jnp.exp(lbl[1:2, :] - mx)
        p0 = e0 / (e0 + e1)
        dl0 = tot[0:1, :] * p0 * (1.0 - p0)
        grads = [jnp.concatenate([dl0, -dl0], axis=0)] + [tot[r:r + 1, :] for r in range(1, n)]
        for i in range(n):
            d, m2, v2 = _adamw(ws[i][...], grads[i], ms[i][...], vs[i][...])
            g_out[i][...] = grads[i]
            d_out[i][...] = d
            m_out[i][...] = m2
            v_out[i][...] = v2

    vm = pl.BlockSpec(memory_space=pltpu.VMEM)
    shapes = [jax.ShapeDtypeStruct(w.shape, F32) for w in small_w]
    return pl.pallas_call(
        body, name="vec_adam",
        out_shape=[jax.ShapeDtypeStruct((1, 128), F32)] + shapes * 4,
        in_specs=[vm] * (1 + 3 * n), out_specs=[vm] * (1 + 4 * n),
    )(tot, *small_w, *small_m, *small_v)


def kernel(x, w_in, lb_logits, hgrn_norm_g, w_a, w_pool, pool_scale, w_out, ln1_g, ln1_b, w_up, w_down, ln2_g, ln2_b, loss_target, m_w_in, m_lb_logits, m_hgrn_norm_g, m_w_a, m_w_pool, m_pool_scale, m_w_out, m_ln1_g, m_ln1_b, m_w_up, m_w_down, m_ln2_g, m_ln2_b, v_w_in, v_lb_logits, v_hgrn_norm_g, v_w_a, v_w_pool, v_pool_scale, v_w_out, v_ln1_g, v_ln1_b, v_w_up, v_w_down, v_ln2_g, v_ln2_b):
    Bl, S, D = x.shape
    T = Bl * S
    pg = D // POOL_GROUPS
    x2 = x.reshape(T, D)
    tgt = loss_target.reshape(T, D)
    me = 4 * lax.axis_index("x") + 2 * lax.axis_index("y") + lax.axis_index("c")
    me_arr = jnp.reshape(me, (1,)).astype(jnp.int32)

    names = ["w_in", "w_a", "w_pool", "w_out", "w_up", "w_down"]
    big_w = dict(zip(names, [w_in[0], w_a[0], w_pool[0], w_out[0], w_up[0], w_down[0]]))
    big_m = dict(zip(names, [m_w_in[0], m_w_a[0], m_w_pool[0], m_w_out[0], m_w_up[0], m_w_down[0]]))
    big_v = dict(zip(names, [v_w_in[0], v_w_a[0], v_w_pool[0], v_w_out[0], v_w_up[0], v_w_down[0]]))
    kinds = dict(w_in="col", w_a="row", w_pool="pool", w_out="row", w_up="col", w_down="row")
    lay = {nm: _Sharded(kinds[nm], big_w[nm].shape) for nm in names}
    wb = {nm: big_w[nm].astype(BF16) for nm in names}

    (w_in_f,) = _all_gather("ag_w_in", [wb["w_in"]], [lay["w_in"]])
    def gather_start(name, nms, after):
        return _exchange_start(name, [wb[nm] for nm in nms], [lax.empty(lay[nm].full_shape, BF16) for nm in nms],
                               src_at=lambda w, ref, peer: ref,
                               dst_at=lambda w, ref, mine, k: lay[nms[w]].at(ref, mine), after=after, own=True)

    ag_mix = gather_start("ag_mix", ["w_a", "w_pool", "w_out"], w_in_f)
    ag_mlp = gather_start("ag_mlp", ["w_up", "w_down"], ag_mix["token"])

    proj, x_t = _proj(x2, w_in_f, ag_mlp["token"])
    proj5 = proj.reshape(N_SEC, Bl, S, D)
    ain3, ain_t, o3, st_all = _hgrn_fwd(proj5, lb_logits, hgrn_norm_g)
    w_a_f, w_pool_f, w_out_f = _exchange_wait(ag_mix, ain3)
    pooled_t, bp3 = _pool_fwd(proj5, w_pool_f)
    ain, bp = ain3.reshape(T, D), bp3.reshape(T, D)
    a, merged_t, xhat1, rs1, x1b, x1_t = _mix_fwd(ain, proj, bp, x2, w_a_f, w_out_f, pool_scale, ln1_g, ln1_b)
    w_up_f, w_down_f = _exchange_wait(ag_mlp, x1b)
    hp, h, dr2, dr2b, dr2_t, vec_mlp = _mlp_fwd(x1b, w_up_f, w_down_f, xhat1, tgt, ln1_g, ln1_b, ln2_g, ln2_b)

    def scatter_start(name, nms, grads_b, after):
        lands = [lax.empty((N_DEV - 1,) + lay[nm].shape, BF16) for nm in nms]
        return _exchange_start(name, grads_b, lands,
                               src_at=lambda w, ref, peer: lay[nms[w]].at(ref, peer),
                               dst_at=lambda w, ref, mine, k: ref.at[k - 1], after=after)

    dhp, dr1, dr1b, vec_ln1 = _mlp_bwd(dr2b, dr2, hp, w_up_f, w_down_f, xhat1, rs1, ln1_g)
    FF = 4 * D
    whole_t = _resident((D, T))
    cols_b = pl.BlockSpec((T, DW_COLS), lambda j: (0, j))
    cols_o = ((D, DW_COLS), lambda j: (0, j))
    gw, gwb = {}, {}
    gw["w_down"], gwb["w_down"] = _dw(
        "dw_down", dr2_t, h, FF // DW_COLS, whole_t, cols_b, (FF, D), (DW_COLS, D), lambda j: (j, 0),
        transpose_out=True)
    rs_down = scatter_start("rs_w_down", ["w_down"], [gwb["w_down"]], gw["w_down"])
    gw["w_up"], gwb["w_up"] = _dw("dw_up", x1_t, dhp, FF // DW_COLS, whole_t, cols_b, (D, FF), *cols_o,
                                  dep=rs_down["token"])
    rs_up = scatter_start("rs_w_up", ["w_up"], [gwb["w_up"]], gw["w_up"])
    da_b, dbp_b, dain, dpooled, dg2, vec_mix = _mix_bwd(dr1b, proj, a, bp, w_a_f, w_out_f, w_pool_f, pool_scale,
                                                        rs_up["token"])
    gw["w_out"], gwb["w_out"] = _dw("dw_out", merged_t, dr1b, D // DW_COLS, whole_t, cols_b, (D, D), *cols_o)
    gw["w_a"], gwb["w_a"] = _dw("dw_a", ain_t, da_b, D // DW_COLS, _resident((Bl, D, S)), cols_b, (D, D), *cols_o)
    gw["w_pool"], gwb["w_pool"] = _dw(
        "dw_pool", pooled_t, dbp_b, POOL_GROUPS, pl.BlockSpec((pg, T), lambda j: (j, 0)),
        pl.BlockSpec((T, pg), lambda j: (0, j)), (POOL_GROUPS, pg, pg), (1, pg, pg), lambda j: (j, 0, 0))
    mid = ["w_out", "w_a", "w_pool"]
    rs_mid = scatter_start("rs_w_mid", mid, [gwb[nm] for nm in mid], gw["w_pool"])
    dpv = _pool_bwd(dpooled.reshape(Bl, S, D), rs_mid["token"]).reshape(T, D)
    gw_in_early, gwb_in_early = _dw_in_early(x_t, dpv, dg2, rs_mid["token"])
    land_in = lax.empty((N_DEV - 1,) + lay["w_in"].shape, BF16)
    rs_in_early = _w_in_scatter_start("rs_w_in_early", gwb_in_early, land_in, True, gw_in_early)
    dh4, vec_hgrn = _hgrn_bwd(proj5, lb_logits, hgrn_norm_g, dain.reshape(Bl, S, D), o3, st_all,
                              rs_in_early["token"])
    dh4 = dh4.reshape(4, T, D)
    vec_tot = _vec_allreduce(vec_mlp + vec_ln1 + vec_mix + vec_hgrn)
    gw["w_in"], gwb_in_late = _dw_in_late(x_t, dh4, gw_in_early, vec_tot)
    rs_in_late = _w_in_scatter_start("rs_w_in_late", gwb_in_late, rs_in_early["land"], False, gw["w_in"])
    grad_x2 = _dx(dr1, dh4, dpv, dg2, w_in_f, rs_in_late["token"])
    grad_x = grad_x2.reshape(Bl, S, D)

    small_names =["lb_logits", "hgrn_norm_g", "pool_scale", "ln1_g", "ln1_b", "ln2_g", "ln2_b"]
    small_w = [lb_logits, hgrn_norm_g, pool_scale, ln1_g, ln1_b, ln2_g, ln2_b]
    small_m = [m_lb_logits, m_hgrn_norm_g, m_pool_scale, m_ln1_g, m_ln1_b, m_ln2_g, m_ln2_b]
    small_v = [v_lb_logits, v_hgrn_norm_g, v_pool_scale, v_ln1_g, v_ln1_b, v_ln2_g, v_ln2_b]
    res = _vec_adam(vec_tot, small_w, small_m, small_v)
    loss = res[0][0, 0]
    n = len(small_w)
    small = {nm: (res[1 + i], res[1 + n + i], res[1 + 2 * n + i], res[1 + 3 * n + i])
             for i, nm in enumerate(small_names)}

    big, last = {}, grad_x2

    def adam(nm, land):
        outs = _adam_shard("adam_" + nm, me_arr, gw[nm], land, lay[nm], big_w[nm], big_m[nm], big_v[nm])
        big[nm] = tuple(t[None] for t in outs)
        return outs[0]

    for pend, nms in ((rs_down, ["w_down"]), (rs_up, ["w_up"]), (rs_mid, mid)):
        for nm, land in zip(nms, _exchange_wait(pend, last)):
            last = adam(nm, land)
    land_in = _w_in_scatter_wait(rs_in_early, rs_in_late["land"], last)
    adam("w_in", _w_in_scatter_wait(rs_in_late, land_in, res[0]))

    order = ["w_in", "lb_logits", "hgrn_norm_g", "w_a", "w_pool", "pool_scale", "w_out", "ln1_g", "ln1_b",
             "w_up", "w_down", "ln2_g", "ln2_b"]
    allp = {**big, **small}
    out = [loss, grad_x]
    for part in range(4):
        out += [allp[nm][part] for nm in order]
    return tuple(out)
```

```python
import jax
import jax.numpy as jnp
from jax import lax
from jax.experimental import pallas as pl
from jax.experimental.pallas import tpu as pltpu

F32 = jnp.float32
BF16 = jnp.bfloat16
MESH = pl.DeviceIdType.MESH

N_DEV = 8
HEAD = 128
CHUNK = 16
SUBLANES = 8
GROUP = 128
SUB_BLOCK = 1024
ROW_TILE = 512
MLP_ROW_TILE = 256
DW_COLS = 512
EARLY_SEC = 4
CH_PER_GROUP = GROUP // CHUNK
N_SEC = 7
POOL_GROUPS = 4
ALPHA = (2.0 * 1) ** 0.25
LN_EPS = 1e-5
RMS_EPS = 1e-6
Q_SCALE = HEAD ** -0.5
ADAM_LR = 0.001
ADAM_B1 = 0.9
ADAM_B2 = 0.999
ADAM_EPS = 1e-08
ADAM_WD = 0.01
ADAM_STEP = 10
VMEM_LIMIT = 60 << 20

NT_DIMS = (((1,), (1,)), ((), ()))
TN_DIMS = (((0,), (0,)), ((), ()))


def _params(sem=None):
    kw = dict(vmem_limit_bytes=VMEM_LIMIT)
    if sem is not None:
        kw["dimension_semantics"] = sem
    return pltpu.CompilerParams(**kw)


def _me():
    return lax.axis_index("x"), lax.axis_index("y"), lax.axis_index("c")


def _sigmoid(v):
    return jax.nn.sigmoid(v)


def _adamw(w, g, m, v):
    m = ADAM_B1 * m + (1.0 - ADAM_B1) * g
    v = ADAM_B2 * v + (1.0 - ADAM_B2) * jnp.square(g)
    m_hat = m / (1.0 - ADAM_B1 ** ADAM_STEP)
    v_hat = v / (1.0 - ADAM_B2 ** ADAM_STEP)
    delta = -ADAM_LR * (m_hat / (jnp.sqrt(v_hat) + ADAM_EPS) + ADAM_WD * w)
    return delta, m, v


class _Sharded:
    def __init__(self, kind, shard_shape):
        self.kind, self.shape = kind, tuple(shard_shape)

    @property
    def full_shape(self):
        r = self.shape
        if self.kind == "row":
            return (N_DEV * r[0],) + r[1:]
        return (r[0], N_DEV * r[1]) + r[2:]

    def at(self, ref, d):
        if self.kind == "col":
            n = self.shape[1]
            return ref.at[:, pl.ds(pl.multiple_of(d * n, 128), n)]
        if self.kind == "row":
            n = self.shape[0]
            return ref.at[pl.ds(pl.multiple_of(d * n, 16), n), :]
        n = self.shape[1]
        return ref.at[:, pl.ds(pl.multiple_of(d * n, 16), n), :]

    def block_index(self, d):
        return {"col": (0, d), "row": (d, 0), "pool": (0, d, 0)}[self.kind]


def _peer(k, x, y, c):
    return (1 - x if k & 4 else x, 1 - y if k & 2 else y, 1 - c if k & 1 else c)


def _all_gather(name, shards, layouts):
    nw = len(shards)

    def body(*refs):
        ins, outs = refs[:nw], refs[nw:2 * nw]
        send_sems, recv_sems, local_sems = refs[2 * nw:]
        x, y, c = _me()
        me = (x, y, c)
        sibling = (x, y, 1 - c)
        chips = [(1 - x, y), (x, 1 - y), (1 - x, 1 - y)]

        def copy(w, k, block, to, src=None):
            px, py, pc = block
            dst = layouts[w].at(outs[w], 4 * px + 2 * py + pc)
            return pltpu.make_async_remote_copy(
                src_ref=dst if src is None else src, dst_ref=dst,
                send_sem=send_sems.at[w, k], recv_sem=recv_sems.at[w, k],
                device_id=to, device_id_type=MESH)

        def place(w):
            mine = pltpu.make_async_copy(ins[w], layouts[w].at(outs[w], 4 * x + 2 * y + c), local_sems.at[w])
            mine.start()
            return mine

        north, south = c == 1, c == 0
        first, diagonal = [], []
        for w in range(nw):
            first.append(copy(w, 0, me, sibling, src=ins[w]))
            first += [copy(w, 1 + j, me, (*chips[j], c), src=ins[w]) for j in range(2)]
            diagonal.append(copy(w, 3, me, (*chips[2], c), src=ins[w]))
        for cp in first:
            cp.start()
        for cp in diagonal:
            pl.when(north)(cp.start)
        local = [place(w) for w in range(nw)]
        passed, relayed = [], []
        for w in range(nw):
            for j in (1, 0, 2):
                copy(w, 1 + j, (*chips[j], c), me).wait_recv()
                fwd = copy(w, 4 + j, (*chips[j], c), sibling)
                fwd.start()
                passed.append(fwd)
                if j == 1:
                    on = copy(w, 3, (*chips[1], c), (*chips[0], c))
                    pl.when(south)(on.start)
                    relayed.append(on)
        for w in range(nw):
            copy(w, 0, sibling, me).wait_recv()
            for j, chip in enumerate(chips):
                copy(w, 4 + j, (*chip, 1 - c), me).wait_recv()
        for cp in first + passed:
            cp.wait_send()
        for cp in diagonal:
            pl.when(north)(cp.wait_send)
        for cp in relayed:
            pl.when(south)(cp.wait_send)
        for cp in local:
            cp.wait()

    any_spec = pl.BlockSpec(memory_space=pl.ANY)
    return pl.pallas_call(
        body, name=name,
        out_shape=[jax.ShapeDtypeStruct(l.full_shape, s.dtype) for s, l in zip(shards, layouts)],
        in_specs=[any_spec] * nw, out_specs=[any_spec] * nw,
        scratch_shapes=[pltpu.SemaphoreType.DMA((nw, 7)), pltpu.SemaphoreType.DMA((nw, 7)),
                        pltpu.SemaphoreType.DMA((nw,))],
    )(*shards)


HBM_SPEC = pl.BlockSpec(memory_space=pltpu.HBM)
SEM_SPEC = pl.BlockSpec(memory_space=pltpu.SEMAPHORE)
DATAFLOW = pltpu.SideEffectType.DATAFLOW_SIDE_EFFECTING


def _exchange_copies(srcs, lands, send_sems, recv_sems, src_at, dst_at):
    x, y, c = _me()
    me = 4 * x + 2 * y + c
    copies = []
    for w in range(len(srcs)):
        for k in range(1, N_DEV):
            px, py, pc = _peer(k, x, y, c)
            copies.append(pltpu.make_async_remote_copy(
                src_ref=src_at(w, srcs[w], 4 * px + 2 * py + pc), dst_ref=dst_at(w, lands[w], me, k),
                send_sem=send_sems.at[w * (N_DEV - 1) + k - 1], recv_sem=recv_sems.at[w * (N_DEV - 1) + k - 1],
                device_id=(px, py, pc), device_id_type=MESH))
    return copies


def _own_copies(srcs, lands, own_sems, src_at, dst_at):
    x, y, c = _me()
    me = 4 * x + 2 * y + c
    return [pltpu.make_async_copy(src_at(w, srcs[w], me), dst_at(w, lands[w], me, 0), own_sems.at[w])
            for w in range(len(srcs))]


def _exchange_start(name, srcs, lands, src_at, dst_at, after, own=False):
    nw = len(srcs)

    def body(*refs):
        src_refs, land_refs = refs[:nw], refs[nw:2 * nw]
        send_sems, recv_sems, own_sems = refs[2 * nw + 1], refs[2 * nw + 2], refs[2 * nw + 3]
        token = refs[-1]
        for cp in _exchange_copies(src_refs, land_refs, send_sems, recv_sems, src_at, dst_at):
            cp.start()
        if own:
            for cp in _own_copies(src_refs, land_refs, own_sems, src_at, dst_at):
                cp.start()
        token[...] = jnp.zeros_like(token)

    hbm = lambda a: pltpu.HBM(a.shape, a.dtype)
    outs = pl.pallas_call(
        body, name=name,
        out_shape=(pltpu.SemaphoreType.DMA((nw * (N_DEV - 1),)), pltpu.SemaphoreType.DMA((nw * (N_DEV - 1),)),
                   pltpu.SemaphoreType.DMA((nw,)), *[hbm(a) for a in srcs], *[hbm(a) for a in lands],
                   jax.ShapeDtypeStruct((8, 128), F32)),
        in_specs=[HBM_SPEC] * (2 * nw) + [pl.BlockSpec(memory_space=pl.ANY)],
        out_specs=(SEM_SPEC, SEM_SPEC, SEM_SPEC, *[HBM_SPEC] * (2 * nw), pl.BlockSpec(memory_space=pltpu.VMEM)),
        input_output_aliases={i: 3 + i for i in range(2 * nw)},
        compiler_params=pltpu.CompilerParams(has_side_effects=DATAFLOW),
    )(*[pltpu.with_memory_space_constraint(a, pltpu.HBM) for a in list(srcs) + list(lands)], after)
    return dict(send=outs[0], recv=outs[1], own_sems=outs[2], srcs=outs[3:3 + nw], lands=outs[3 + nw:3 + 2 * nw],
                token=outs[-1], src_at=src_at, dst_at=dst_at, name=name, own=own)


def _exchange_wait(pending, after):
    nw = len(pending["srcs"])

    def body(*refs):
        src_refs, land_refs = refs[:nw], refs[nw:2 * nw]
        send_sems, recv_sems, own_sems = refs[2 * nw], refs[2 * nw + 1], refs[2 * nw + 2]
        for cp in _exchange_copies(src_refs, land_refs, send_sems, recv_sems,
                                   pending["src_at"], pending["dst_at"]):
            cp.wait_send()
            cp.wait_recv()
        if pending["own"]:
            for cp in _own_copies(src_refs, land_refs, own_sems, pending["src_at"], pending["dst_at"]):
                cp.wait()

    hbm = lambda a: pltpu.HBM(a.shape, a.dtype)
    outs = pl.pallas_call(
        body, name=pending["name"] + "_wait",
        out_shape=(*[hbm(a) for a in pending["srcs"]], *[hbm(a) for a in pending["lands"]]),
        in_specs=[HBM_SPEC] * (2 * nw) + [SEM_SPEC, SEM_SPEC, SEM_SPEC, pl.BlockSpec(memory_space=pl.ANY)],
        out_specs=tuple([HBM_SPEC] * (2 * nw)),
        input_output_aliases={i: i for i in range(2 * nw)},
        compiler_params=pltpu.CompilerParams(has_side_effects=DATAFLOW),
    )(*pending["srcs"], *pending["lands"], pending["send"], pending["recv"], pending["own_sems"], after)
    return outs[nw:]


def _w_in_scatter_copies(src, land, send_sems, recv_sems, early):
    rows, cols = land.shape[1], land.shape[2]
    bound = EARLY_SEC * rows
    cut_dev = bound // cols
    cut = bound - cut_dev * cols
    x, y, c = _me()
    me = 4 * x + 2 * y + c

    def pieces(t):
        if early:
            return [(t > cut_dev, t * cols - bound, cols, 0), (t == cut_dev, 0, cols - cut, cut)]
        return [(t < cut_dev, t * cols, cols, 0), (t == cut_dev, cut_dev * cols, cut, 0)]

    out = []
    for k in range(1, N_DEV):
        px, py, pc = _peer(k, x, y, c)
        for (to_peer, s0, width, d0), (to_me, _, _, _) in zip(pieces(4 * px + 2 * py + pc), pieces(me)):
            s0 = s0 if isinstance(s0, int) else pl.multiple_of(jnp.maximum(s0, 0), 128)
            out.append((to_peer, to_me, pltpu.make_async_remote_copy(
                src_ref=src.at[:, pl.ds(s0, width)], dst_ref=land.at[k - 1, :, pl.ds(d0, width)],
                send_sem=send_sems.at[k - 1], recv_sem=recv_sems.at[k - 1],
                device_id=(px, py, pc), device_id_type=MESH)))
    return out


def _w_in_scatter_start(name, src, land, early, after):
    def body(src_ref, land_ref, after_ref, send_sems, recv_sems, src_thru, land_thru, token):
        for to_peer, _, cp in _w_in_scatter_copies(src_ref, land_ref, send_sems, recv_sems, early):
            pl.when(to_peer)(cp.start)
        token[...] = jnp.zeros_like(token)

    hbm = lambda a: pltpu.HBM(a.shape, a.dtype)
    outs = pl.pallas_call(
        body, name=name,
        out_shape=(pltpu.SemaphoreType.DMA((N_DEV - 1,)), pltpu.SemaphoreType.DMA((N_DEV - 1,)),
                   hbm(src), hbm(land), jax.ShapeDtypeStruct((8, 128), F32)),
        in_specs=[HBM_SPEC, HBM_SPEC, pl.BlockSpec(memory_space=pl.ANY)],
        out_specs=(SEM_SPEC, SEM_SPEC, HBM_SPEC, HBM_SPEC, pl.BlockSpec(memory_space=pltpu.VMEM)),
        input_output_aliases={0: 2, 1: 3},
        compiler_params=pltpu.CompilerParams(has_side_effects=DATAFLOW),
    )(pltpu.with_memory_space_constraint(src, pltpu.HBM), pltpu.with_memory_space_constraint(land, pltpu.HBM), after)
    return dict(send=outs[0], recv=outs[1], src=outs[2], land=outs[3], token=outs[4], early=early, name=name)


def _w_in_scatter_wait(pending, land, after):
    def body(src_ref, land_ref, send_sems, recv_sems, after_ref, src_dead, land_out):
        for to_peer, to_me, cp in _w_in_scatter_copies(src_ref, land_ref, send_sems, recv_sems, pending["early"]):
            pl.when(to_peer)(cp.wait_send)
            pl.when(to_me)(cp.wait_recv)

    hbm = lambda a: pltpu.HBM(a.shape, a.dtype)
    outs = pl.pallas_call(
        body, name=pending["name"] + "_wait", out_shape=(hbm(pending["src"]), hbm(land)),
        in_specs=[HBM_SPEC, HBM_SPEC, SEM_SPEC, SEM_SPEC, pl.BlockSpec(memory_space=pl.ANY)],
        out_specs=(HBM_SPEC, HBM_SPEC), input_output_aliases={0: 0, 1: 1},
        compiler_params=pltpu.CompilerParams(has_side_effects=DATAFLOW),
    )(pending["src"], land, pending["send"], pending["recv"], after)
    return outs[1]


def _call_after(dep, body, args, *, in_specs, **kw):
    n_in = len(args)

    def wrapped(*refs):
        body(*refs[:n_in], *refs[n_in + 1:])

    dep_spec = pl.BlockSpec(dep.shape, lambda *_: (0,) * dep.ndim)
    return pl.pallas_call(wrapped, in_specs=list(in_specs) + [dep_spec], **kw)(*args, dep)


def _resident(shape):
    return pl.BlockSpec(shape, lambda *_: (0,) * len(shape), pipeline_mode=pl.Buffered(1))


def _proj(x2, w_in, dep):
    T, D = x2.shape
    tm = min(ROW_TILE, T)

    def body(x_ref, w_ref, o_ref, xt_ref):
        x = x_ref[...]
        xt_ref[...] = x.T.astype(BF16)
        xb = x.astype(BF16)
        for j in range(N_SEC):
            o_ref[j] = jnp.dot(xb, w_ref[:, j * D:(j + 1) * D], preferred_element_type=F32)

    return _call_after(
        dep, body, (x2, w_in), name="proj", grid=(T // tm,),
        in_specs=[pl.BlockSpec((tm, D), lambda i: (i, 0)), _resident((D, N_SEC * D))],
        out_specs=[pl.BlockSpec((N_SEC, tm, D), lambda i: (0, i, 0)), pl.BlockSpec((D, tm), lambda i: (0, i))],
        out_shape=[jax.ShapeDtypeStruct((N_SEC, T, D), F32), jax.ShapeDtypeStruct((D, T), BF16)],
        compiler_params=_params(("parallel",)))


def _chunk_cumsum(v, reverse=False):
    rows, lanes = v.shape
    x = v.reshape(rows // SUBLANES, SUBLANES, lanes)
    pos = lax.broadcasted_iota(jnp.int32, x.shape, 1)
    for sh in (1, 2, 4):
        if reverse:
            x = x + jnp.where(pos < SUBLANES - sh, pltpu.roll(x, SUBLANES - sh, 1), 0.0)
        else:
            x = x + jnp.where(pos >= sh, pltpu.roll(x, sh, 1), 0.0)
    x = x.reshape(rows // CHUNK, CHUNK // SUBLANES, SUBLANES, lanes)
    half = lax.broadcasted_iota(jnp.int32, x.shape, 1)
    if reverse:
        x = x + jnp.where(half == 0, x[:, 1:2, 0:1, :], 0.0)
    else:
        x = x + jnp.where(half == 1, x[:, 0:1, SUBLANES - 1:SUBLANES, :], 0.0)
    return x.reshape(rows, lanes)


def _hgrn_gates(q, f_pre, lb_logits):
    l0, l1 = lb_logits[0:1, :], lb_logits[1:2, :]
    mx = jnp.maximum(l0, l1)
    e0, e1 = jnp.exp(l0 - mx), jnp.exp(l1 - mx)
    lb = e0 / (e0 + e1)
    sq = _sigmoid(q)
    qf = q * sq * Q_SCALE
    sg = _sigmoid(f_pre)
    f = lb + (1.0 - lb) * sg
    k = 1.0 - f
    log_f = jnp.log(f)
    G = _chunk_cumsum(log_f)
    g_to_end = _chunk_cumsum(log_f, reverse=True) - log_f
    e_g = jnp.exp(G)
    e_ng = jnp.exp(-G)
    e_ge = jnp.exp(g_to_end)
    return dict(lb=lb, sq=sq, qf=qf, sg=sg, f=f, k=k, G=G, e_g=e_g, e_ng=e_ng, e_ge=e_ge,
                qd=qf * e_g, ki=k * e_ng, ke=k * e_ge, dec=e_g * e_ge)


def _intra_mask():
    r = lax.broadcasted_iota(jnp.int32, (GROUP, GROUP), 0)
    c = lax.broadcasted_iota(jnp.int32, (GROUP, GROUP), 1)
    return (r // CHUNK == c // CHUNK) & (c <= r)


def _chunk_outer(lhs_rows, rhs_b, out_scr, sb):
    lane = lax.broadcasted_iota(jnp.int32, (GROUP, GROUP), 1) // CHUNK
    for g in range(sb // GROUP):
        sl = slice(g * GROUP, (g + 1) * GROUP)
        lhs_t = lhs_rows[sl].T
        for cc in range(CH_PER_GROUP):
            masked = jnp.where(lane == cc, lhs_t, 0.0).astype(BF16)
            out_scr[g * CH_PER_GROUP + cc] = jnp.dot(masked, rhs_b[sl], preferred_element_type=F32)


def _hgrn_forward_blocks(cs, vs, st0s, sb, o_scr, kv_scr, st_scr, dec_scr):
    nc = sb // CHUNK
    n_str = len(cs)
    mask = _intra_mask()
    bf = []
    for i, (c, v) in enumerate(zip(cs, vs)):
        qd_b, ki_b, ke_b, v_b = (c["qd"].astype(BF16), c["ki"].astype(BF16), c["ke"].astype(BF16),
                                 v.astype(BF16))
        bf.append((qd_b, ki_b, ke_b, v_b))
        for g in range(sb // GROUP):
            sl = slice(g * GROUP, (g + 1) * GROUP)
            sc = lax.dot_general(qd_b[sl], ki_b[sl], NT_DIMS, preferred_element_type=F32)
            a = jnp.where(mask, sc, 0.0).astype(BF16)
            o_scr[i, sl, :] = jnp.dot(a, v_b[sl], preferred_element_type=F32)
        _chunk_outer(v, ke_b, kv_scr.at[i], sb)
        dec_scr[i] = c["dec"]

    def rec(n, sts):
        row = pl.ds(pl.multiple_of(n * CHUNK, CHUNK), 1)
        out = []
        for i in range(n_str):
            st_scr[i, n] = sts[i]
            out.append(sts[i] * dec_scr[i, row, :] + kv_scr[i, n])
        return tuple(out)

    ends = lax.fori_loop(0, nc, rec, tuple(st0s))

    for n in range(nc):
        rows = slice(n * CHUNK, (n + 1) * CHUNK)
        for i in range(n_str):
            o_scr[i, rows, :] += lax.dot_general(bf[i][0][rows], st_scr[i, n].astype(BF16), NT_DIMS,
                                                 preferred_element_type=F32)
    return ends, bf


def _hgrn_fwd(proj5, lb_logits, gn):
    _, Bl, S, D = proj5.shape
    H = D // HEAD
    sb = min(SUB_BLOCK, S)
    nsb = S // sb
    nc = sb // CHUNK

    def body(p_ref, lbl_ref, gn_ref, ain_ref, aint_ref, o_ref, st_ref, carry, o_scr, kv_scr, st_scr, dec_scr):
        @pl.when(pl.program_id(1) == 0)
        def _():
            carry[...] = jnp.zeros_like(carry)

        st0s = [carry[b] for b in range(Bl)]
        cs = [_hgrn_gates(p_ref[0, b], p_ref[1, b], lbl_ref[...]) for b in range(Bl)]
        ends, _ = _hgrn_forward_blocks(cs, [p_ref[2, b] for b in range(Bl)], st0s, sb,
                                       o_scr, kv_scr, st_scr, dec_scr)
        for b in range(Bl):
            carry[b] = ends[b]
            st_ref[b, 0] = st_scr[b].astype(BF16)
            o = o_scr[b]
            o_ref[b] = o
            rinv = lax.rsqrt(jnp.mean(o * o, axis=-1, keepdims=True) + RMS_EPS)
            ain = o * rinv * gn_ref[...] * _sigmoid(p_ref[3, b])
            ain_ref[b] = ain.astype(BF16)
            aint_ref[b] = ain.T.astype(BF16)

    return pl.pallas_call(
        body, name="hgrn_fwd", grid=(H, nsb),
        in_specs=[pl.BlockSpec((4, Bl, sb, HEAD), lambda h, s: (0, 0, s, h)),
                  pl.BlockSpec((2, HEAD), lambda h, s: (0, h)),
                  pl.BlockSpec((1, HEAD), lambda h, s: (0, h))],
        out_specs=[pl.BlockSpec((Bl, sb, HEAD), lambda h, s: (0, s, h)),
                   pl.BlockSpec((Bl, HEAD, sb), lambda h, s: (0, h, s)),
                   pl.BlockSpec((Bl, sb, HEAD), lambda h, s: (0, s, h)),
                   pl.BlockSpec((Bl, 1, nc, HEAD, HEAD), lambda h, s: (0, h, s, 0, 0))],
        out_shape=[jax.ShapeDtypeStruct((Bl, S, D), BF16), jax.ShapeDtypeStruct((Bl, D, S), BF16),
                   jax.ShapeDtypeStruct((Bl, S, D), F32),
                   jax.ShapeDtypeStruct((Bl, H, S // CHUNK, HEAD, HEAD), BF16)],
        scratch_shapes=[pltpu.VMEM((Bl, HEAD, HEAD), F32), pltpu.VMEM((Bl, sb, HEAD), F32),
                        pltpu.VMEM((Bl, nc, HEAD, HEAD), F32), pltpu.VMEM((Bl, nc, HEAD, HEAD), F32),
                        pltpu.VMEM((Bl, sb, HEAD), F32)],
        compiler_params=_params(("parallel", "arbitrary")),
    )(proj5, lb_logits, gn)


def _window_count(shape, g):
    pos = lax.broadcasted_iota(jnp.int32, shape, 0)
    return pos, jnp.minimum(pos + 1, jnp.left_shift(2, g)).astype(F32)


def _select_window(g, sums):
    return jnp.where(g == 0, sums[0], jnp.where(g == 1, sums[1], jnp.where(g == 2, sums[2], sums[3])))


def _pool_fwd(proj5, w_pool):
    _, Bl, S, D = proj5.shape
    pg = D // POOL_GROUPS

    def body(v_ref, w_ref, pooled_t_ref, bp_ref):
        g = pl.program_id(1)
        v = v_ref[0, 0]
        pos, cnt = _window_count(v.shape, g)
        cur, sums = v, []
        for sh in (1, 2, 4, 8):
            cur = cur + jnp.where(pos >= sh, pltpu.roll(cur, sh, 0), 0.0)
            sums.append(cur)
        pooled = _select_window(g, sums) / cnt - v
        pooled_t_ref[...] = pooled.T.astype(BF16)
        bp_ref[0] = jnp.dot(pooled.astype(BF16), w_ref[0], preferred_element_type=F32)

    return pl.pallas_call(
        body, name="pool_fwd", grid=(Bl, POOL_GROUPS),
        in_specs=[pl.BlockSpec((1, 1, S, pg), lambda b, g: (4, b, 0, g)),
                  pl.BlockSpec((1, pg, pg), lambda b, g: (g, 0, 0))],
        out_specs=[pl.BlockSpec((pg, S), lambda b, g: (g, b)),
                   pl.BlockSpec((1, S, pg), lambda b, g: (b, 0, g))],
        out_shape=[jax.ShapeDtypeStruct((D, Bl * S), BF16), jax.ShapeDtypeStruct((Bl, S, D), F32)],
        compiler_params=_params(("parallel", "parallel")),
    )(proj5, w_pool)


def _layer_norm_fwd(r):
    mu = jnp.mean(r, axis=-1, keepdims=True)
    d = r - mu
    rs = lax.rsqrt(jnp.mean(d * d, axis=-1, keepdims=True) + LN_EPS)
    return d * rs, rs


def _layer_norm_bwd(dy_g, xhat, rs):
    return rs * (dy_g - jnp.mean(dy_g, axis=-1, keepdims=True)
                 - xhat * jnp.mean(dy_g * xhat, axis=-1, keepdims=True))


def _mix_fwd(ain, proj, bp, x2, w_a, w_out, ps, g1, b1):
    T, D = x2.shape
    tm = min(ROW_TILE, T)

    def body(ain_ref, ga_ref, gb_ref, bp_ref, x_ref, wa_ref, wo_ref, ps_ref, g1_ref, b1_ref,
             a_ref, mgt_ref, xh_ref, rs_ref, x1b_ref, x1t_ref):
        a = jnp.dot(ain_ref[...], wa_ref[...], preferred_element_type=F32)
        a_ref[...] = a
        merged = _sigmoid(ga_ref[0]) * a + _sigmoid(gb_ref[0]) * (bp_ref[...] * ps_ref[...])
        mgt_ref[...] = merged.T.astype(BF16)
        r1 = ALPHA * x_ref[...] + jnp.dot(merged.astype(BF16), wo_ref[...], preferred_element_type=F32)
        xhat, rs = _layer_norm_fwd(r1)
        xh_ref[...] = xhat
        rs_ref[...] = rs
        x1 = xhat * g1_ref[...] + b1_ref[...]
        x1b_ref[...] = x1.astype(BF16)
        x1t_ref[...] = x1.T.astype(BF16)

    row = lambda i: (i, 0)
    col = lambda i: (0, i)
    full = lambda i: (0, 0)
    return pl.pallas_call(
        body, name="mix_fwd", grid=(T // tm,),
        in_specs=[pl.BlockSpec((tm, D), row),
                  pl.BlockSpec((1, tm, D), lambda i: (5, i, 0)),
                  pl.BlockSpec((1, tm, D), lambda i: (6, i, 0)),
                  pl.BlockSpec((tm, D), row), pl.BlockSpec((tm, D), row),
                  pl.BlockSpec((D, D), full), pl.BlockSpec((D, D), full),
                  pl.BlockSpec((1, D), full), pl.BlockSpec((1, D), full), pl.BlockSpec((1, D), full)],
        out_specs=[pl.BlockSpec((tm, D), row), pl.BlockSpec((D, tm), col), pl.BlockSpec((tm, D), row),
                   pl.BlockSpec((tm, 1), row), pl.BlockSpec((tm, D), row), pl.BlockSpec((D, tm), col)],
        out_shape=[jax.ShapeDtypeStruct((T, D), F32), jax.ShapeDtypeStruct((D, T), BF16),
                   jax.ShapeDtypeStruct((T, D), F32), jax.ShapeDtypeStruct((T, 1), F32),
                   jax.ShapeDtypeStruct((T, D), BF16), jax.ShapeDtypeStruct((D, T), BF16)],
        compiler_params=_params(("parallel",)),
    )(ain, proj, proj, bp, x2, w_a, w_out, ps, g1, b1)


def _mlp_fwd(x1b, w_up, w_down, xhat1, tgt, g1, b1, g2, b2):
    T, D = xhat1.shape
    FF = w_up.shape[1]
    tm = min(MLP_ROW_TILE, T)

    def body(x_ref, wu_ref, wd_ref, xh_ref, t_ref, g1_ref, b1_ref, g2_ref, b2_ref,
             hp_ref, h_ref, dr_ref, drb_ref, drt_ref, vec_ref):
        @pl.when(pl.program_id(0) == 0)
        def _():
            vec_ref[...] = jnp.zeros_like(vec_ref)

        xb = x_ref[...]
        x1 = xh_ref[...] * g1_ref[...] + b1_ref[...]
        r2 = ALPHA * x1
        for f in range(FF // D):
            cols = slice(f * D, (f + 1) * D)
            hp = jnp.dot(xb, wu_ref[:, cols], preferred_element_type=F32)
            hp_ref[:, cols] = hp
            h = jnp.square(jnp.maximum(hp, 0.0)).astype(BF16)
            h_ref[:, cols] = h
            r2 = r2 + jnp.dot(h, wd_ref[cols, :], preferred_element_type=F32)
        xhat2, rs2 = _layer_norm_fwd(r2)
        err = xhat2 * g2_ref[...] + b2_ref[...] - t_ref[...]
        dy = err / D
        vec_ref[5:6, :] += jnp.sum(dy * xhat2, axis=0, keepdims=True)
        vec_ref[6:7, :] += jnp.sum(dy, axis=0, keepdims=True)
        vec_ref[7:8, :] += jnp.sum(0.5 * err * err / D, axis=0, keepdims=True)
        dr = _layer_norm_bwd(dy * g2_ref[...], xhat2, rs2)
        dr_ref[...] = dr
        drb_ref[...] = dr.astype(BF16)
        drt_ref[...] = dr.T.astype(BF16)

    row = lambda i: (i, 0)
    full = lambda i: (0, 0)
    return pl.pallas_call(
        body, name="mlp_fwd", grid=(T // tm,),
        in_specs=[pl.BlockSpec((tm, D), row), _resident((D, FF)), _resident((FF, D)),
                  pl.BlockSpec((tm, D), row), pl.BlockSpec((tm, D), row),
                  pl.BlockSpec((1, D), full), pl.BlockSpec((1, D), full),
                  pl.BlockSpec((1, D), full), pl.BlockSpec((1, D), full)],
        out_specs=[pl.BlockSpec((tm, FF), row), pl.BlockSpec((tm, FF), row), pl.BlockSpec((tm, D), row),
                   pl.BlockSpec((tm, D), row), pl.BlockSpec((D, tm), lambda i: (0, i)),
                   pl.BlockSpec((8, D), full)],
        out_shape=[jax.ShapeDtypeStruct((T, FF), F32), jax.ShapeDtypeStruct((T, FF), BF16),
                   jax.ShapeDtypeStruct((T, D), F32), jax.ShapeDtypeStruct((T, D), BF16),
                   jax.ShapeDtypeStruct((D, T), BF16), jax.ShapeDtypeStruct((8, D), F32)],
        compiler_params=_params(("arbitrary",)),
    )(x1b, w_up, w_down, xhat1, tgt, g1, b1, g2, b2)


def _mlp_bwd(drb, dr, hp, w_up, w_down, xhat1, rs1, g1):
    T, D = dr.shape
    FF = hp.shape[1]
    tm = min(MLP_ROW_TILE, T)

    def body(drb_ref, dr_ref, hp_ref, wu_ref, wd_ref, xh_ref, rs_ref, g1_ref,
             dhp_ref, d1_ref, d1b_ref, vec_ref):
        @pl.when(pl.program_id(0) == 0)
        def _():
            vec_ref[...] = jnp.zeros_like(vec_ref)

        drb = drb_ref[...]
        dx1 = ALPHA * dr_ref[...]
        for f in range(FF // D):
            cols = slice(f * D, (f + 1) * D)
            dh = lax.dot_general(drb, wd_ref[cols, :], NT_DIMS, preferred_element_type=F32)
            dhp = (dh * (2.0 * jnp.maximum(hp_ref[:, cols], 0.0))).astype(BF16)
            dhp_ref[:, cols] = dhp
            dx1 = dx1 + lax.dot_general(dhp, wu_ref[:, cols], NT_DIMS, preferred_element_type=F32)
        xhat = xh_ref[...]
        vec_ref[3:4, :] += jnp.sum(dx1 * xhat, axis=0, keepdims=True)
        vec_ref[4:5, :] += jnp.sum(dx1, axis=0, keepdims=True)
        d1 = _layer_norm_bwd(dx1 * g1_ref[...], xhat, rs_ref[...])
        d1_ref[...] = d1
        d1b_ref[...] = d1.astype(BF16)

    row = lambda i: (i, 0)
    full = lambda i: (0, 0)
    return pl.pallas_call(
        body, name="mlp_bwd", grid=(T // tm,),
        in_specs=[pl.BlockSpec((tm, D), row), pl.BlockSpec((tm, D), row), pl.BlockSpec((tm, FF), row),
                  _resident((D, FF)), _resident((FF, D)),
                  pl.BlockSpec((tm, D), row), pl.BlockSpec((tm, 1), row), pl.BlockSpec((1, D), full)],
        out_specs=[pl.BlockSpec((tm, FF), row), pl.BlockSpec((tm, D), row), pl.BlockSpec((tm, D), row),
                   pl.BlockSpec((8, D), full)],
        out_shape=[jax.ShapeDtypeStruct((T, FF), BF16), jax.ShapeDtypeStruct((T, D), F32),
                   jax.ShapeDtypeStruct((T, D), BF16), jax.ShapeDtypeStruct((8, D), F32)],
        compiler_params=_params(("arbitrary",)),
    )(drb, dr, hp, w_up, w_down, xhat1, rs1, g1)


def _dw(name, a_t, b, n_j, a_spec, b_spec, o_shape, o_block, o_map, transpose_out=False, dep=None,
        into=(None, None), ob_shape=None, ob_map=None):
    def body(*refs):
        a_ref, b_ref, o_ref, ob_ref = refs[0], refs[1], refs[-2], refs[-1]
        b_val = b_ref[0] if len(b_ref.shape) == 3 else b_ref[...]
        if len(a_ref.shape) == 3:
            seq = a_ref.shape[2]
            p = sum(jnp.dot(a_ref[i], b_val[i * seq:(i + 1) * seq], preferred_element_type=F32)
                    for i in range(a_ref.shape[0]))
        else:
            p = jnp.dot(a_ref[...], b_val, preferred_element_type=F32)
        if transpose_out:
            p = p.T
        p = p.reshape(o_ref.shape)
        o_ref[...] = p
        ob_ref[...] = p.astype(BF16)

    kw = dict(name=name, grid=(n_j,), in_specs=[a_spec, b_spec],
              out_specs=[pl.BlockSpec(o_block, o_map), pl.BlockSpec(o_block, ob_map or o_map)],
              out_shape=[jax.ShapeDtypeStruct(o_shape, F32), jax.ShapeDtypeStruct(ob_shape or o_shape, BF16)],
              compiler_params=_params(("parallel",)))
    args = (a_t, b)
    aliases = {}
    for out_index, arr in enumerate(into):
        if arr is not None:
            aliases[len(args)] = out_index
            args = args + (arr,)
            kw["in_specs"] = kw["in_specs"] + [pl.BlockSpec(memory_space=pl.ANY)]
    if aliases:
        kw["input_output_aliases"] = aliases
    if dep is None:
        return pl.pallas_call(body, **kw)(*args)
    return _call_after(dep, body, args, **kw)


def _mix_bwd(d1b, proj, a, bp, w_a, w_out, w_pool, ps, dep):
    T, D = a.shape
    tm = min(ROW_TILE, T)
    pg = D // POOL_GROUPS

    def body(d1b_ref, ga_ref, gb_ref, a_ref, bp_ref, wa_ref, wo_ref, wp_ref, ps_ref,
             da_ref, dbp_ref, dain_ref, dpl_ref, dg_ref, vec_ref):
        @pl.when(pl.program_id(0) == 0)
        def _():
            vec_ref[...] = jnp.zeros_like(vec_ref)

        dm = lax.dot_general(d1b_ref[...], wo_ref[...], NT_DIMS, preferred_element_type=F32)
        sa, sg = _sigmoid(ga_ref[0]), _sigmoid(gb_ref[0])
        bp_v, ps_v = bp_ref[...], ps_ref[...]
        da = (dm * sa).astype(BF16)
        db = dm * sg
        dg_ref[0] = (dm * a_ref[...] * sa * (1.0 - sa)).astype(BF16)
        dg_ref[1] = (dm * (bp_v * ps_v) * sg * (1.0 - sg)).astype(BF16)
        vec_ref[2:3, :] += jnp.sum(db * bp_v, axis=0, keepdims=True)
        dbp = (db * ps_v).astype(BF16)
        da_ref[...] = da
        dbp_ref[...] = dbp
        dain_ref[...] = lax.dot_general(da, wa_ref[...], NT_DIMS, preferred_element_type=F32)
        for g in range(POOL_GROUPS):
            cols = slice(g * pg, (g + 1) * pg)
            dpl_ref[:, cols] = lax.dot_general(dbp[:, cols], wp_ref[g], NT_DIMS,
                                               preferred_element_type=F32)

    row = lambda i: (i, 0)
    full = lambda i: (0, 0)
    return _call_after(
        dep, body, (d1b, proj, proj, a, bp, w_a, w_out, w_pool, ps), name="mix_bwd", grid=(T // tm,),
        in_specs=[pl.BlockSpec((tm, D), row),
                  pl.BlockSpec((1, tm, D), lambda i: (5, i, 0)),
                  pl.BlockSpec((1, tm, D), lambda i: (6, i, 0)),
                  pl.BlockSpec((tm, D), row), pl.BlockSpec((tm, D), row),
                  pl.BlockSpec((D, D), full), pl.BlockSpec((D, D), full),
                  pl.BlockSpec((POOL_GROUPS, pg, pg), lambda i: (0, 0, 0)),
                  pl.BlockSpec((1, D), full)],
        out_specs=[pl.BlockSpec((tm, D), row), pl.BlockSpec((tm, D), row),
                   pl.BlockSpec((tm, D), row), pl.BlockSpec((tm, D), row),
                   pl.BlockSpec((2, tm, D), lambda i: (0, i, 0)),
                   pl.BlockSpec((8, D), full)],
        out_shape=[jax.ShapeDtypeStruct((T, D), BF16), jax.ShapeDtypeStruct((T, D), BF16),
                   jax.ShapeDtypeStruct((T, D), F32), jax.ShapeDtypeStruct((T, D), F32),
                   jax.ShapeDtypeStruct((2, T, D), BF16), jax.ShapeDtypeStruct((8, D), F32)],
        compiler_params=_params(("arbitrary",)))


def _pool_bwd(dpooled3, dep):
    Bl, S, D = dpooled3.shape
    pg = D // POOL_GROUPS

    def body(dp_ref, dv_ref):
        g = pl.program_id(1)
        dp = dp_ref[0]
        pos, cnt = _window_count(dp.shape, g)
        cur, sums = dp / cnt, []
        for sh in (1, 2, 4, 8):
            cur = cur + jnp.where(pos < S - sh, pltpu.roll(cur, S - sh, 0), 0.0)
            sums.append(cur)
        dv_ref[0] = (_select_window(g, sums) - dp).astype(BF16)

    spec = pl.BlockSpec((1, S, pg), lambda b, g: (b, 0, g))
    return _call_after(
        dep, body, (dpooled3,), name="pool_bwd", grid=(Bl, POOL_GROUPS), in_specs=[spec], out_specs=spec,
        out_shape=jax.ShapeDtypeStruct((Bl, S, D), BF16),
        compiler_params=_params(("parallel", "parallel")))


def _hgrn_bwd(proj5, lb_logits, gn, dain3, o3, st_all, dep):
    _, Bl, S, D = proj5.shape
    H = D // HEAD
    sb = min(SUB_BLOCK, S)
    nsb = S // sb
    nc = sb // CHUNK
    streams = range(Bl)

    def body(p_ref, lbl_ref, gn_ref, dain_ref, o_ref, st_ref, d_ref, vec_ref,
             dcarry, kv_scr, dst_scr, dec_scr, dvi_scr, dke_scr, dqi_scr):
        s = pl.program_id(1)

        @pl.when(s == 0)
        def _():
            dcarry[...] = jnp.zeros_like(dcarry)
            vec_ref[...] = jnp.zeros_like(vec_ref)

        qs, vs, ogs = [p_ref[0, b] for b in streams], [p_ref[2, b] for b in streams], [p_ref[3, b] for b in streams]
        cs = [_hgrn_gates(qs[b], p_ref[1, b], lbl_ref[...]) for b in streams]
        bf = [(cs[b]["qd"].astype(BF16), cs[b]["ki"].astype(BF16), cs[b]["ke"].astype(BF16),
               vs[b].astype(BF16)) for b in streams]
        mask = _intra_mask()
        gn_v = gn_ref[...]
        keep = []
        for b in streams:
            qd_b, ki_b, ke_b, v_b = bf[b]
            dec_scr[b] = cs[b]["dec"]
            o = o_ref[b]
            rinv = lax.rsqrt(jnp.mean(o * o, axis=-1, keepdims=True) + RMS_EPS)
            on = o * rinv
            so = _sigmoid(ogs[b])
            dain = dain_ref[b]
            vec_ref[1:2, :] += jnp.sum(dain * on * so, axis=0, keepdims=True)
            d_og = dain * on * gn_v * so * (1.0 - so)
            d_on = dain * gn_v * so
            do = rinv * (d_on - on * jnp.mean(d_on * on, axis=-1, keepdims=True))
            do_b = do.astype(BF16)
            dv_parts, dqd_parts, dki_parts = [], [], []
            for g in range(sb // GROUP):
                sl = slice(g * GROUP, (g + 1) * GROUP)
                sc = lax.dot_general(qd_b[sl], ki_b[sl], NT_DIMS, preferred_element_type=F32)
                a = jnp.where(mask, sc, 0.0).astype(BF16)
                da = lax.dot_general(do_b[sl], v_b[sl], NT_DIMS, preferred_element_type=F32)
                da = jnp.where(mask, da, 0.0).astype(BF16)
                dv_parts.append(lax.dot_general(a, do_b[sl], TN_DIMS, preferred_element_type=F32))
                dqd_parts.append(jnp.dot(da, ki_b[sl], preferred_element_type=F32))
                dki_parts.append(lax.dot_general(da, qd_b[sl], TN_DIMS, preferred_element_type=F32))
            keep.append(dict(d_og=d_og, do_b=do_b, dv_intra=jnp.concatenate(dv_parts, axis=0),
                             dqd_intra=jnp.concatenate(dqd_parts, axis=0),
                             dki=jnp.concatenate(dki_parts, axis=0)))
            _chunk_outer(do, qd_b, kv_scr.at[b], sb)

        def rrec(i, dsts):
            n = nc - 1 - i
            row = pl.ds(pl.multiple_of(n * CHUNK, CHUNK), 1)
            out = []
            for b in streams:
                dst_scr[b, n] = dsts[b]
                out.append(dsts[b] * dec_scr[b, row, :] + kv_scr[b, n])
            return tuple(out)

        ends = lax.fori_loop(0, nc, rrec, tuple(dcarry[b] for b in streams))
        for b in streams:
            dcarry[b] = ends[b]
        for n in range(nc):
            rows = slice(n * CHUNK, (n + 1) * CHUNK)
            for b in streams:
                qd_b, ki_b, ke_b, v_b = bf[b]
                dst_b = dst_scr[b, n].astype(BF16)
                dvi_scr[b, rows, :] = lax.dot_general(ke_b[rows], dst_b, NT_DIMS, preferred_element_type=F32)
                dke_scr[b, rows, :] = jnp.dot(v_b[rows], dst_b, preferred_element_type=F32)
                dqi_scr[b, rows, :] = jnp.dot(keep[b]["do_b"][rows], st_ref[b, 0, n],
                                              preferred_element_type=F32)
        for b in streams:
            c, k = cs[b], keep[b]
            ddec = jnp.sum(dst_scr[b] * st_ref[b, 0].astype(F32), axis=1)
            dgl = jnp.broadcast_to(ddec[:, None, :], (nc, CHUNK, HEAD)).reshape(sb, HEAD) * c["dec"]
            dqd = k["dqd_intra"] + dqi_scr[b]
            dke = dke_scr[b]
            dki = k["dki"]
            t_ke = dke * c["ke"]
            dG = dqd * c["qd"] - dki * c["ki"] - t_ke
            dgl = dgl + _chunk_cumsum(t_ke) + _chunk_cumsum(t_ke, reverse=True) - t_ke
            dlogf = _chunk_cumsum(dG, reverse=True) + dgl
            dk = dki * c["e_ng"] + dke * c["e_ge"]
            df = dlogf / c["f"] - dk
            sg, sq, lb, q = c["sg"], c["sq"], c["lb"], qs[b]
            vec_ref[0:1, :] += jnp.sum(df * (1.0 - sg), axis=0, keepdims=True)
            d_ref[0, b] = (dqd * c["e_g"] * Q_SCALE * (sq + q * sq * (1.0 - sq))).astype(BF16)
            d_ref[1, b] = (df * (1.0 - lb) * sg * (1.0 - sg)).astype(BF16)
            d_ref[2, b] = (k["dv_intra"] + dvi_scr[b]).astype(BF16)
            d_ref[3, b] = k["d_og"].astype(BF16)

    rev = lambda s: nsb - 1 - s
    big = pltpu.VMEM((Bl, nc, HEAD, HEAD), F32)
    rows_f32 = pltpu.VMEM((Bl, sb, HEAD), F32)
    return _call_after(
        dep, body, (proj5, lb_logits, gn, dain3, o3, st_all), name="hgrn_bwd", grid=(H, nsb),
        in_specs=[pl.BlockSpec((4, Bl, sb, HEAD), lambda h, s: (0, 0, rev(s), h)),
                  pl.BlockSpec((2, HEAD), lambda h, s: (0, h)),
                  pl.BlockSpec((1, HEAD), lambda h, s: (0, h)),
                  pl.BlockSpec((Bl, sb, HEAD), lambda h, s: (0, rev(s), h)),
                  pl.BlockSpec((Bl, sb, HEAD), lambda h, s: (0, rev(s), h)),
                  pl.BlockSpec((Bl, 1, nc, HEAD, HEAD), lambda h, s: (0, h, rev(s), 0, 0))],
        out_specs=[pl.BlockSpec((4, Bl, sb, HEAD), lambda h, s: (0, 0, rev(s), h)),
                   pl.BlockSpec((8, HEAD), lambda h, s: (0, h))],
        out_shape=[jax.ShapeDtypeStruct((4, Bl, S, D), BF16), jax.ShapeDtypeStruct((8, D), F32)],
        scratch_shapes=[pltpu.VMEM((Bl, HEAD, HEAD), F32), big, big, rows_f32, rows_f32, rows_f32, rows_f32],
        compiler_params=_params(("parallel", "arbitrary")))


def _dx(d1, dh4, dpv, dg2, w_in, dep):
    T, D = d1.shape
    tm = min(ROW_TILE, T)

    def body(d1_ref, dh_ref, dp_ref, dg_ref, w_ref, o_ref):
        blocks = [dh_ref[0], dh_ref[1], dh_ref[2], dh_ref[3], dp_ref[...], dg_ref[0], dg_ref[1]]
        acc = ALPHA * d1_ref[...]
        for j, blk in enumerate(blocks):
            acc = acc + lax.dot_general(blk, w_ref[:, j * D:(j + 1) * D], NT_DIMS, preferred_element_type=F32)
        o_ref[...] = acc

    row = lambda i: (i, 0)
    return _call_after(
        dep, body, (d1, dh4, dpv, dg2, w_in), name="dx", grid=(T // tm,),
        in_specs=[pl.BlockSpec((tm, D), row), pl.BlockSpec((4, tm, D), lambda i: (0, i, 0)),
                  pl.BlockSpec((tm, D), row), pl.BlockSpec((2, tm, D), lambda i: (0, i, 0)),
                  _resident((D, N_SEC * D))],
        out_specs=pl.BlockSpec((tm, D), row),
        out_shape=jax.ShapeDtypeStruct((T, D), F32),
        compiler_params=_params(("parallel",)))


def _dw_in_part(name, x_t, b, sections, first_sec, into, dep, ob_shape, ob_first):
    D, T = x_t.shape
    per = D // DW_COLS
    b_spec = (pl.BlockSpec((1, T, DW_COLS), lambda j: (j // per, 0, j % per)) if b.ndim == 3
              else pl.BlockSpec((T, DW_COLS), lambda j: (0, j)))
    return _dw(name, x_t, b, sections * per, _resident((D, T)), b_spec, (D, N_SEC * D), (D, DW_COLS),
               lambda j: (0, first_sec * per + j), dep=dep, into=into, ob_shape=ob_shape,
               ob_map=lambda j: (0, ob_first * per + j))


def _dw_in_rec(x_t, dh4, dep):
    D = x_t.shape[0]
    return _dw_in_part("dw_in_rec", x_t, dh4, EARLY_SEC, 0, (None, None), dep, (D, EARLY_SEC * D), 0)


def _dw_in_rest(x_t, dpv, dg2, f32_rec, dep):
    D = x_t.shape[0]
    rest_shape = (D, (N_SEC - EARLY_SEC) * D)
    f32, bf = _dw_in_part("dw_in_gates", x_t, dg2, 2, 5, (f32_rec, None), dep, rest_shape, 1)
    return _dw_in_part("dw_in_pool", x_t, dpv, 1, 4, (f32, bf), None, rest_shape, 0)


def _adam_shard(name, me_arr, grad, land, layout, w, m, v):
    shape = layout.shape
    n_split = 4
    blk = (shape[0] // n_split,) + shape[1:]
    zeros = (0,) * (len(shape) - 1)

    def body(me_ref, g_ref, r_ref, w_ref, m_ref, v_ref, g_out, d_out, m_out, v_out):
        g = g_ref[...]
        for k in range(N_DEV - 1):
            g = g + r_ref[k].astype(F32)
        d, m2, v2 = _adamw(w_ref[...], g, m_ref[...], v_ref[...])
        g_out[...] = g
        d_out[...] = d
        m_out[...] = m2
        v_out[...] = v2

    def own(i, me_ref):
        bi = layout.block_index(me_ref[0])
        return (bi[0] * n_split + i,) + tuple(bi[1:]) if layout.kind == "row" else (i,) + tuple(bi[1:])

    plain = pl.BlockSpec(blk, lambda i, me_ref: (i,) + zeros)
    grid_spec = pltpu.PrefetchScalarGridSpec(
        num_scalar_prefetch=1, grid=(n_split,),
        in_specs=[pl.BlockSpec(blk, own),
                  pl.BlockSpec((N_DEV - 1,) + blk, lambda i, me_ref: (0, i) + zeros),
                  plain, plain, plain],
        out_specs=[plain] * 4)
    return pl.pallas_call(
        body, name=name, grid_spec=grid_spec,
        out_shape=[jax.ShapeDtypeStruct(shape, F32)] * 4,
        compiler_params=_params(("parallel",)),
    )(me_arr, grad, land, w, m, v)


def _vec_allreduce(vec):
    D = vec.shape[1]

    def body(vec_ref, tot_ref, gat, send_sems, recv_sems):
        x, y, c = _me()
        me = 4 * x + 2 * y + c
        gat[me] = vec_ref[...]
        copies = []
        for k in range(1, N_DEV):
            cp = pltpu.make_async_remote_copy(
                src_ref=vec_ref, dst_ref=gat.at[me], send_sem=send_sems.at[k - 1],
                recv_sem=recv_sems.at[k - 1], device_id=_peer(k, x, y, c), device_id_type=MESH)
            cp.start()
            copies.append(cp)
        for cp in copies:
            cp.wait()
        tot = gat[0]
        for d in range(1, N_DEV):
            tot = tot + gat[d]
        tot_ref[...] = tot

    vm = pl.BlockSpec(memory_space=pltpu.VMEM)
    return pl.pallas_call(
        body, name="vec_allreduce", out_shape=jax.ShapeDtypeStruct(vec.shape, F32),
        in_specs=[vm], out_specs=vm,
        scratch_shapes=[pltpu.VMEM((N_DEV, 8, D), F32), pltpu.SemaphoreType.DMA((N_DEV - 1,)),
                        pltpu.SemaphoreType.DMA((N_DEV - 1,))],
    )(vec)


def _vec_adam(tot, small_w, small_m, small_v):
    n = len(small_w)

    def body(*refs):
        tot = refs[0][...]
        ws, ms, vs = refs[1:1 + n], refs[1 + n:1 + 2 * n], refs[1 + 2 * n:1 + 3 * n]
        outs = refs[1 + 3 * n:]
        loss_ref, g_out, d_out = outs[0], outs[1:1 + n], outs[1 + n:1 + 2 * n]
        m_out, v_out = outs[1 + 2 * n:1 + 3 * n], outs[1 + 3 * n:1 + 4 * n]
        loss_ref[...] = jnp.broadcast_to(jnp.sum(tot[7:8, :], axis=1, keepdims=True), loss_ref.shape)
        lbl = ws[0][...]
        mx = jnp.maximum(lbl[0:1, :], lbl[1:2, :])
        e0, e1 = jnp.exp(lbl[0:1, :] - mx), jnp.exp(lbl[1:2, :] - mx)
        p0 = e0 / (e0 + e1)
        dl0 = tot[0:1, :] * p0 * (1.0 - p0)
        grads = [jnp.concatenate([dl0, -dl0], axis=0)] + [tot[r:r + 1, :] for r in range(1, n)]
        for i in range(n):
            d, m2, v2 = _adamw(ws[i][...], grads[i], ms[i][...], vs[i][...])
            g_out[i][...] = grads[i]
            d_out[i][...] = d
            m_out[i][...] = m2
            v_out[i][...] = v2

    vm = pl.BlockSpec(memory_space=pltpu.VMEM)
    shapes = [jax.ShapeDtypeStruct(w.shape, F32) for w in small_w]
    return pl.pallas_call(
        body, name="vec_adam",
        out_shape=[jax.ShapeDtypeStruct((1, 128), F32)] + shapes * 4,
        in_specs=[vm] * (1 + 3 * n), out_specs=[vm] * (1 + 4 * n),
    )(tot, *small_w, *small_m, *small_v)


def kernel(x, w_in, lb_logits, hgrn_norm_g, w_a, w_pool, pool_scale, w_out, ln1_g, ln1_b, w_up, w_down, ln2_g, ln2_b, loss_target, m_w_in, m_lb_logits, m_hgrn_norm_g, m_w_a, m_w_pool, m_pool_scale, m_w_out, m_ln1_g, m_ln1_b, m_w_up, m_w_down, m_ln2_g, m_ln2_b, v_w_in, v_lb_logits, v_hgrn_norm_g, v_w_a, v_w_pool, v_pool_scale, v_w_out, v_ln1_g, v_ln1_b, v_w_up, v_w_down, v_ln2_g, v_ln2_b):
    Bl, S, D = x.shape
    T = Bl * S
    pg = D // POOL_GROUPS
    x2 = x.reshape(T, D)
    tgt = loss_target.reshape(T, D)
    me = 4 * lax.axis_index("x") + 2 * lax.axis_index("y") + lax.axis_index("c")
    me_arr = jnp.reshape(me, (1,)).astype(jnp.int32)

    names = ["w_in", "w_a", "w_pool", "w_out", "w_up", "w_down"]
    big_w = dict(zip(names, [w_in[0], w_a[0], w_pool[0], w_out[0], w_up[0], w_down[0]]))
    big_m = dict(zip(names, [m_w_in[0], m_w_a[0], m_w_pool[0], m_w_out[0], m_w_up[0], m_w_down[0]]))
    big_v = dict(zip(names, [v_w_in[0], v_w_a[0], v_w_pool[0], v_w_out[0], v_w_up[0], v_w_down[0]]))
    kinds = dict(w_in="col", w_a="row", w_pool="pool", w_out="row", w_up="col", w_down="row")
    lay = {nm: _Sharded(kinds[nm], big_w[nm].shape) for nm in names}
    wb = {nm: big_w[nm].astype(BF16) for nm in names}

    (w_in_f,) = _all_gather("ag_w_in", [wb["w_in"]], [lay["w_in"]])
    def gather_start(name, nms, after):
        return _exchange_start(name, [wb[nm] for nm in nms], [lax.empty(lay[nm].full_shape, BF16) for nm in nms],
                               src_at=lambda w, ref, peer: ref,
                               dst_at=lambda w, ref, mine, k: lay[nms[w]].at(ref, mine), after=after, own=True)

    ag_mix = gather_start("ag_mix", ["w_a", "w_pool", "w_out"], w_in_f)
    ag_mlp = gather_start("ag_mlp", ["w_up", "w_down"], ag_mix["token"])

    proj, x_t = _proj(x2, w_in_f, ag_mlp["token"])
    proj5 = proj.reshape(N_SEC, Bl, S, D)
    ain3, ain_t, o3, st_all = _hgrn_fwd(proj5, lb_logits, hgrn_norm_g)
    w_a_f, w_pool_f, w_out_f = _exchange_wait(ag_mix, ain3)
    pooled_t, bp3 = _pool_fwd(proj5, w_pool_f)
    ain, bp = ain3.reshape(T, D), bp3.reshape(T, D)
    a, merged_t, xhat1, rs1, x1b, x1_t = _mix_fwd(ain, proj, bp, x2, w_a_f, w_out_f, pool_scale, ln1_g, ln1_b)
    w_up_f, w_down_f = _exchange_wait(ag_mlp, x1b)
    hp, h, dr2, dr2b, dr2_t, vec_mlp = _mlp_fwd(x1b, w_up_f, w_down_f, xhat1, tgt, ln1_g, ln1_b, ln2_g, ln2_b)

    def scatter_start(name, nms, grads_b, after):
        lands = [lax.empty((N_DEV - 1,) + lay[nm].shape, BF16) for nm in nms]
        return _exchange_start(name, grads_b, lands,
                               src_at=lambda w, ref, peer: lay[nms[w]].at(ref, peer),
                               dst_at=lambda w, ref, mine, k: ref.at[k - 1], after=after)

    dhp, dr1, dr1b, vec_ln1 = _mlp_bwd(dr2b, dr2, hp, w_up_f, w_down_f, xhat1, rs1, ln1_g)
    FF = 4 * D
    whole_t = _resident((D, T))
    cols_b = pl.BlockSpec((T, DW_COLS), lambda j: (0, j))
    cols_o = ((D, DW_COLS), lambda j: (0, j))
    gw, gwb = {}, {}
    gw["w_down"], gwb["w_down"] = _dw(
        "dw_down", dr2_t, h, FF // DW_COLS, whole_t, cols_b, (FF, D), (DW_COLS, D), lambda j: (j, 0),
        transpose_out=True)
    rs_down = scatter_start("rs_w_down", ["w_down"], [gwb["w_down"]], gw["w_down"])
    gw["w_up"], gwb["w_up"] = _dw("dw_up", x1_t, dhp, FF // DW_COLS, whole_t, cols_b, (D, FF), *cols_o,
                                  dep=rs_down["token"])
    rs_up = scatter_start("rs_w_up", ["w_up"], [gwb["w_up"]], gw["w_up"])
    da_b, dbp_b, dain, dpooled, dg2, vec_mix = _mix_bwd(dr1b, proj, a, bp, w_a_f, w_out_f, w_pool_f, pool_scale,
                                                        rs_up["token"])
    dh4, vec_hgrn = _hgrn_bwd(proj5, lb_logits, hgrn_norm_g, dain.reshape(Bl, S, D), o3, st_all, rs_up["token"])
    dh4 = dh4.reshape(4, T, D)
    vec_tot = _vec_allreduce(vec_mlp + vec_ln1 + vec_mix + vec_hgrn)
    gw_in_rec, gwb_in_rec = _dw_in_rec(x_t, dh4, vec_tot)
    land_in = lax.empty((N_DEV - 1,) + lay["w_in"].shape, BF16)
    rs_in_rec = _w_in_scatter_start("rs_w_in_rec", gwb_in_rec, land_in, False, gw_in_rec)
    gw["w_out"], gwb["w_out"] = _dw("dw_out", merged_t, dr1b, D // DW_COLS, whole_t, cols_b, (D, D), *cols_o,
                                    dep=rs_in_rec["token"])
    gw["w_a"], gwb["w_a"] = _dw("dw_a", ain_t, da_b, D // DW_COLS, _resident((Bl, D, S)), cols_b, (D, D), *cols_o,
                                dep=rs_in_rec["token"])
    gw["w_pool"], gwb["w_pool"] = _dw(
        "dw_pool", pooled_t, dbp_b, POOL_GROUPS, pl.BlockSpec((pg, T), lambda j: (j, 0)),
        pl.BlockSpec((T, pg), lambda j: (0, j)), (POOL_GROUPS, pg, pg), (1, pg, pg), lambda j: (j, 0, 0),
        dep=rs_in_rec["token"])
    mid = ["w_out", "w_a", "w_pool"]
    rs_mid = scatter_start("rs_w_mid", mid, [gwb[nm] for nm in mid], gw["w_pool"])
    dpv = _pool_bwd(dpooled.reshape(Bl, S, D), rs_mid["token"]).reshape(T, D)
    gw["w_in"], gwb_in_rest = _dw_in_rest(x_t, dpv, dg2, gw_in_rec, rs_mid["token"])
    rs_in_rest = _w_in_scatter_start("rs_w_in_rest", gwb_in_rest, rs_in_rec["land"], True, gw["w_in"])
    grad_x2 = _dx(dr1, dh4, dpv, dg2, w_in_f, rs_in_rest["token"])
    grad_x = grad_x2.reshape(Bl, S, D)

    small_names =["lb_logits", "hgrn_norm_g", "pool_scale", "ln1_g", "ln1_b", "ln2_g", "ln2_b"]
    small_w = [lb_logits, hgrn_norm_g, pool_scale, ln1_g, ln1_b, ln2_g, ln2_b]
    small_m = [m_lb_logits, m_hgrn_norm_g, m_pool_scale, m_ln1_g, m_ln1_b, m_ln2_g, m_ln2_b]
    small_v = [v_lb_logits, v_hgrn_norm_g, v_pool_scale, v_ln1_g, v_ln1_b, v_ln2_g, v_ln2_b]
    res = _vec_adam(vec_tot, small_w, small_m, small_v)
    loss = res[0][0, 0]
    n = len(small_w)
    small = {nm: (res[1 + i], res[1 + n + i], res[1 + 2 * n + i], res[1 + 3 * n + i])
             for i, nm in enumerate(small_names)}

    big, last = {}, grad_x2

    def adam(nm, land):
        outs = _adam_shard("adam_" + nm, me_arr, gw[nm], land, lay[nm], big_w[nm], big_m[nm], big_v[nm])
        big[nm] = tuple(t[None] for t in outs)
        return outs[0]

    for pend, nms in ((rs_down, ["w_down"]), (rs_up, ["w_up"]), (rs_mid, mid)):
        for nm, land in zip(nms, _exchange_wait(pend, last)):
            last = adam(nm, land)
    land_in = _w_in_scatter_wait(rs_in_rec, rs_in_rest["land"], last)
    adam("w_in", _w_in_scatter_wait(rs_in_rest, land_in, res[0]))

    order = ["w_in", "lb_logits", "hgrn_norm_g", "w_a", "w_pool", "pool_scale", "w_out", "ln1_g", "ln1_b",
             "w_up", "w_down", "ln2_g", "ln2_b"]
    allp = {**big, **small}
    out = [loss, grad_x]
    for part in range(4):
        out += [allp[nm][part] for nm in order]
    return tuple(out)
```

```python
import jax
import jax.numpy as jnp
from jax import lax
from jax.experimental import pallas as pl
from jax.experimental.pallas import tpu as pltpu

F32 = jnp.float32
BF16 = jnp.bfloat16
MESH = pl.DeviceIdType.MESH

N_DEV = 8
HEAD = 128
CHUNK = 16
SUBLANES = 8
GROUP = 128
SUB_BLOCK = 1024
ROW_TILE = 512
MLP_ROW_TILE = 256
DW_COLS = 512
EARLY_SEC = 4
CH_PER_GROUP = GROUP // CHUNK
N_SEC = 7
POOL_GROUPS = 4
ALPHA = (2.0 * 1) ** 0.25
LN_EPS = 1e-5
RMS_EPS = 1e-6
Q_SCALE = HEAD ** -0.5
ADAM_LR = 0.001
ADAM_B1 = 0.9
ADAM_B2 = 0.999
ADAM_EPS = 1e-08
ADAM_WD = 0.01
ADAM_STEP = 10
VMEM_LIMIT = 60 << 20

NT_DIMS = (((1,), (1,)), ((), ()))
TN_DIMS = (((0,), (0,)), ((), ()))


def _params(sem=None):
    kw = dict(vmem_limit_bytes=VMEM_LIMIT)
    if sem is not None:
        kw["dimension_semantics"] = sem
    return pltpu.CompilerParams(**kw)


def _me():
    return lax.axis_index("x"), lax.axis_index("y"), lax.axis_index("c")


def _sigmoid(v):
    return jax.nn.sigmoid(v)


def _adamw(w, g, m, v):
    m = ADAM_B1 * m + (1.0 - ADAM_B1) * g
    v = ADAM_B2 * v + (1.0 - ADAM_B2) * jnp.square(g)
    m_hat = m / (1.0 - ADAM_B1 ** ADAM_STEP)
    v_hat = v / (1.0 - ADAM_B2 ** ADAM_STEP)
    delta = -ADAM_LR * (m_hat / (jnp.sqrt(v_hat) + ADAM_EPS) + ADAM_WD * w)
    return delta, m, v


class _Sharded:
    def __init__(self, kind, shard_shape):
        self.kind, self.shape = kind, tuple(shard_shape)

    @property
    def full_shape(self):
        r = self.shape
        if self.kind == "row":
            return (N_DEV * r[0],) + r[1:]
        return (r[0], N_DEV * r[1]) + r[2:]

    def at(self, ref, d):
        if self.kind == "col":
            n = self.shape[1]
            return ref.at[:, pl.ds(pl.multiple_of(d * n, 128), n)]
        if self.kind == "row":
            n = self.shape[0]
            return ref.at[pl.ds(pl.multiple_of(d * n, 16), n), :]
        n = self.shape[1]
        return ref.at[:, pl.ds(pl.multiple_of(d * n, 16), n), :]

    def block_index(self, d):
        return {"col": (0, d), "row": (d, 0), "pool": (0, d, 0)}[self.kind]


def _peer(k, x, y, c):
    return (1 - x if k & 4 else x, 1 - y if k & 2 else y, 1 - c if k & 1 else c)


def _all_gather(name, shards, layouts):
    nw = len(shards)

    def body(*refs):
        ins, outs = refs[:nw], refs[nw:2 * nw]
        send_sems, recv_sems, local_sems = refs[2 * nw:]
        x, y, c = _me()
        me = (x, y, c)
        sibling = (x, y, 1 - c)
        chips = [(1 - x, y), (x, 1 - y), (1 - x, 1 - y)]

        def copy(w, k, block, to, src=None):
            px, py, pc = block
            dst = layouts[w].at(outs[w], 4 * px + 2 * py + pc)
            return pltpu.make_async_remote_copy(
                src_ref=dst if src is None else src, dst_ref=dst,
                send_sem=send_sems.at[w, k], recv_sem=recv_sems.at[w, k],
                device_id=to, device_id_type=MESH)

        def place(w):
            mine = pltpu.make_async_copy(ins[w], layouts[w].at(outs[w], 4 * x + 2 * y + c), local_sems.at[w])
            mine.start()
            return mine

        north, south = c == 1, c == 0
        first, diagonal = [], []
        for w in range(nw):
            first.append(copy(w, 0, me, sibling, src=ins[w]))
            first += [copy(w, 1 + j, me, (*chips[j], c), src=ins[w]) for j in range(2)]
            diagonal.append(copy(w, 3, me, (*chips[2], c), src=ins[w]))
        for cp in first:
            cp.start()
        for cp in diagonal:
            pl.when(north)(cp.start)
        local = [place(w) for w in range(nw)]
        passed, relayed = [], []
        for w in range(nw):
            for j in (1, 0, 2):
                copy(w, 1 + j, (*chips[j], c), me).wait_recv()
                fwd = copy(w, 4 + j, (*chips[j], c), sibling)
                fwd.start()
                passed.append(fwd)
                if j == 1:
                    on = copy(w, 3, (*chips[1], c), (*chips[0], c))
                    pl.when(south)(on.start)
                    relayed.append(on)
        for w in range(nw):
            copy(w, 0, sibling, me).wait_recv()
            for j, chip in enumerate(chips):
                copy(w, 4 + j, (*chip, 1 - c), me).wait_recv()
        for cp in first + passed:
            cp.wait_send()
        for cp in diagonal:
            pl.when(north)(cp.wait_send)
        for cp in relayed:
            pl.when(south)(cp.wait_send)
        for cp in local:
            cp.wait()

    any_spec = pl.BlockSpec(memory_space=pl.ANY)
    return pl.pallas_call(
        body, name=name,
        out_shape=[jax.ShapeDtypeStruct(l.full_shape, s.dtype) for s, l in zip(shards, layouts)],
        in_specs=[any_spec] * nw, out_specs=[any_spec] * nw,
        scratch_shapes=[pltpu.SemaphoreType.DMA((nw, 7)), pltpu.SemaphoreType.DMA((nw, 7)),
                        pltpu.SemaphoreType.DMA((nw,))],
    )(*shards)


HBM_SPEC = pl.BlockSpec(memory_space=pltpu.HBM)
SEM_SPEC = pl.BlockSpec(memory_space=pltpu.SEMAPHORE)
DATAFLOW = pltpu.SideEffectType.DATAFLOW_SIDE_EFFECTING


def _exchange_copies(srcs, lands, send_sems, recv_sems, src_at, dst_at):
    x, y, c = _me()
    me = 4 * x + 2 * y + c
    copies = []
    for w in range(len(srcs)):
        for k in range(1, N_DEV):
            px, py, pc = _peer(k, x, y, c)
            copies.append(pltpu.make_async_remote_copy(
                src_ref=src_at(w, srcs[w], 4 * px + 2 * py + pc), dst_ref=dst_at(w, lands[w], me, k),
                send_sem=send_sems.at[w * (N_DEV - 1) + k - 1], recv_sem=recv_sems.at[w * (N_DEV - 1) + k - 1],
                device_id=(px, py, pc), device_id_type=MESH))
    return copies


def _own_copies(srcs, lands, own_sems, src_at, dst_at):
    x, y, c = _me()
    me = 4 * x + 2 * y + c
    return [pltpu.make_async_copy(src_at(w, srcs[w], me), dst_at(w, lands[w], me, 0), own_sems.at[w])
            for w in range(len(srcs))]


def _exchange_start(name, srcs, lands, src_at, dst_at, after, own=False):
    nw = len(srcs)

    def body(*refs):
        src_refs, land_refs = refs[:nw], refs[nw:2 * nw]
        send_sems, recv_sems, own_sems = refs[2 * nw + 1], refs[2 * nw + 2], refs[2 * nw + 3]
        token = refs[-1]
        for cp in _exchange_copies(src_refs, land_refs, send_sems, recv_sems, src_at, dst_at):
            cp.start()
        if own:
            for cp in _own_copies(src_refs, land_refs, own_sems, src_at, dst_at):
                cp.start()
        token[...] = jnp.zeros_like(token)

    hbm = lambda a: pltpu.HBM(a.shape, a.dtype)
    outs = pl.pallas_call(
        body, name=name,
        out_shape=(pltpu.SemaphoreType.DMA((nw * (N_DEV - 1),)), pltpu.SemaphoreType.DMA((nw * (N_DEV - 1),)),
                   pltpu.SemaphoreType.DMA((nw,)), *[hbm(a) for a in srcs], *[hbm(a) for a in lands],
                   jax.ShapeDtypeStruct((8, 128), F32)),
        in_specs=[HBM_SPEC] * (2 * nw) + [pl.BlockSpec(memory_space=pl.ANY)],
        out_specs=(SEM_SPEC, SEM_SPEC, SEM_SPEC, *[HBM_SPEC] * (2 * nw), pl.BlockSpec(memory_space=pltpu.VMEM)),
        input_output_aliases={i: 3 + i for i in range(2 * nw)},
        compiler_params=pltpu.CompilerParams(has_side_effects=DATAFLOW),
    )(*[pltpu.with_memory_space_constraint(a, pltpu.HBM) for a in list(srcs) + list(lands)], after)
    return dict(send=outs[0], recv=outs[1], own_sems=outs[2], srcs=outs[3:3 + nw], lands=outs[3 + nw:3 + 2 * nw],
                token=outs[-1], src_at=src_at, dst_at=dst_at, name=name, own=own)


def _exchange_wait(pending, after):
    nw = len(pending["srcs"])

    def body(*refs):
        src_refs, land_refs = refs[:nw], refs[nw:2 * nw]
        send_sems, recv_sems, own_sems = refs[2 * nw], refs[2 * nw + 1], refs[2 * nw + 2]
        for cp in _exchange_copies(src_refs, land_refs, send_sems, recv_sems,
                                   pending["src_at"], pending["dst_at"]):
            cp.wait_send()
            cp.wait_recv()
        if pending["own"]:
            for cp in _own_copies(src_refs, land_refs, own_sems, pending["src_at"], pending["dst_at"]):
                cp.wait()

    hbm = lambda a: pltpu.HBM(a.shape, a.dtype)
    outs = pl.pallas_call(
        body, name=pending["name"] + "_wait",
        out_shape=(*[hbm(a) for a in pending["srcs"]], *[hbm(a) for a in pending["lands"]]),
        in_specs=[HBM_SPEC] * (2 * nw) + [SEM_SPEC, SEM_SPEC, SEM_SPEC, pl.BlockSpec(memory_space=pl.ANY)],
        out_specs=tuple([HBM_SPEC] * (2 * nw)),
        input_output_aliases={i: i for i in range(2 * nw)},
        compiler_params=pltpu.CompilerParams(has_side_effects=DATAFLOW),
    )(*pending["srcs"], *pending["lands"], pending["send"], pending["recv"], pending["own_sems"], after)
    return outs[nw:]


def _w_in_scatter_copies(src, land, send_sems, recv_sems, early):
    rows, cols = land.shape[1], land.shape[2]
    bound = EARLY_SEC * rows
    cut_dev = bound // cols
    cut = bound - cut_dev * cols
    x, y, c = _me()
    me = 4 * x + 2 * y + c

    def pieces(t):
        if early:
            return [(t > cut_dev, t * cols - bound, cols, 0), (t == cut_dev, 0, cols - cut, cut)]
        return [(t < cut_dev, t * cols, cols, 0), (t == cut_dev, cut_dev * cols, cut, 0)]

    out = []
    for k in range(1, N_DEV):
        px, py, pc = _peer(k, x, y, c)
        for (to_peer, s0, width, d0), (to_me, _, _, _) in zip(pieces(4 * px + 2 * py + pc), pieces(me)):
            s0 = s0 if isinstance(s0, int) else pl.multiple_of(jnp.maximum(s0, 0), 128)
            out.append((to_peer, to_me, pltpu.make_async_remote_copy(
                src_ref=src.at[:, pl.ds(s0, width)], dst_ref=land.at[k - 1, :, pl.ds(d0, width)],
                send_sem=send_sems.at[k - 1], recv_sem=recv_sems.at[k - 1],
                device_id=(px, py, pc), device_id_type=MESH)))
    return out


def _w_in_scatter_start(name, src, land, early, after):
    def body(src_ref, land_ref, after_ref, send_sems, recv_sems, src_thru, land_thru, token):
        for to_peer, _, cp in _w_in_scatter_copies(src_ref, land_ref, send_sems, recv_sems, early):
            pl.when(to_peer)(cp.start)
        token[...] = jnp.zeros_like(token)

    hbm = lambda a: pltpu.HBM(a.shape, a.dtype)
    outs = pl.pallas_call(
        body, name=name,
        out_shape=(pltpu.SemaphoreType.DMA((N_DEV - 1,)), pltpu.SemaphoreType.DMA((N_DEV - 1,)),
                   hbm(src), hbm(land), jax.ShapeDtypeStruct((8, 128), F32)),
        in_specs=[HBM_SPEC, HBM_SPEC, pl.BlockSpec(memory_space=pl.ANY)],
        out_specs=(SEM_SPEC, SEM_SPEC, HBM_SPEC, HBM_SPEC, pl.BlockSpec(memory_space=pltpu.VMEM)),
        input_output_aliases={0: 2, 1: 3},
        compiler_params=pltpu.CompilerParams(has_side_effects=DATAFLOW),
    )(pltpu.with_memory_space_constraint(src, pltpu.HBM), pltpu.with_memory_space_constraint(land, pltpu.HBM), after)
    return dict(send=outs[0], recv=outs[1], src=outs[2], land=outs[3], token=outs[4], early=early, name=name)


def _w_in_scatter_wait(pending, land, after):
    def body(src_ref, land_ref, send_sems, recv_sems, after_ref, src_dead, land_out):
        for to_peer, to_me, cp in _w_in_scatter_copies(src_ref, land_ref, send_sems, recv_sems, pending["early"]):
            pl.when(to_peer)(cp.wait_send)
            pl.when(to_me)(cp.wait_recv)

    hbm = lambda a: pltpu.HBM(a.shape, a.dtype)
    outs = pl.pallas_call(
        body, name=pending["name"] + "_wait", out_shape=(hbm(pending["src"]), hbm(land)),
        in_specs=[HBM_SPEC, HBM_SPEC, SEM_SPEC, SEM_SPEC, pl.BlockSpec(memory_space=pl.ANY)],
        out_specs=(HBM_SPEC, HBM_SPEC), input_output_aliases={0: 0, 1: 1},
        compiler_params=pltpu.CompilerParams(has_side_effects=DATAFLOW),
    )(pending["src"], land, pending["send"], pending["recv"], after)
    return outs[1]


def _call_after(dep, body, args, *, in_specs, **kw):
    n_in = len(args)

    def wrapped(*refs):
        body(*refs[:n_in], *refs[n_in + 1:])

    dep_spec = pl.BlockSpec(dep.shape, lambda *_: (0,) * dep.ndim)
    return pl.pallas_call(wrapped, in_specs=list(in_specs) + [dep_spec], **kw)(*args, dep)


def _resident(shape):
    return pl.BlockSpec(shape, lambda *_: (0,) * len(shape), pipeline_mode=pl.Buffered(1))


def _proj(x2, w_in, dep):
    T, D = x2.shape
    tm = min(ROW_TILE, T)

    def body(x_ref, w_ref, o_ref, xt_ref):
        x = x_ref[...]
        xt_ref[...] = x.T.astype(BF16)
        xb = x.astype(BF16)
        for j in range(N_SEC):
            o_ref[j] = jnp.dot(xb, w_ref[:, j * D:(j + 1) * D], preferred_element_type=F32)

    return _call_after(
        dep, body, (x2, w_in), name="proj", grid=(T // tm,),
        in_specs=[pl.BlockSpec((tm, D), lambda i: (i, 0)), _resident((D, N_SEC * D))],
        out_specs=[pl.BlockSpec((N_SEC, tm, D), lambda i: (0, i, 0)), pl.BlockSpec((D, tm), lambda i: (0, i))],
        out_shape=[jax.ShapeDtypeStruct((N_SEC, T, D), F32), jax.ShapeDtypeStruct((D, T), BF16)],
        compiler_params=_params(("parallel",)))


def _chunk_cumsum(v, reverse=False):
    rows, lanes = v.shape
    x = v.reshape(rows // SUBLANES, SUBLANES, lanes)
    pos = lax.broadcasted_iota(jnp.int32, x.shape, 1)
    for sh in (1, 2, 4):
        if reverse:
            x = x + jnp.where(pos < SUBLANES - sh, pltpu.roll(x, SUBLANES - sh, 1), 0.0)
        else:
            x = x + jnp.where(pos >= sh, pltpu.roll(x, sh, 1), 0.0)
    x = x.reshape(rows // CHUNK, CHUNK // SUBLANES, SUBLANES, lanes)
    half = lax.broadcasted_iota(jnp.int32, x.shape, 1)
    if reverse:
        x = x + jnp.where(half == 0, x[:, 1:2, 0:1, :], 0.0)
    else:
        x = x + jnp.where(half == 1, x[:, 0:1, SUBLANES - 1:SUBLANES, :], 0.0)
    return x.reshape(rows, lanes)


def _hgrn_gates(q, f_pre, lb_logits):
    l0, l1 = lb_logits[0:1, :], lb_logits[1:2, :]
    mx = jnp.maximum(l0, l1)
    e0, e1 = jnp.exp(l0 - mx), jnp.exp(l1 - mx)
    lb = e0 / (e0 + e1)
    sq = _sigmoid(q)
    qf = q * sq * Q_SCALE
    sg = _sigmoid(f_pre)
    f = lb + (1.0 - lb) * sg
    k = 1.0 - f
    log_f = jnp.log(f)
    G = _chunk_cumsum(log_f)
    g_to_end = _chunk_cumsum(log_f, reverse=True) - log_f
    e_g = jnp.exp(G)
    e_ng = jnp.exp(-G)
    e_ge = jnp.exp(g_to_end)
    return dict(lb=lb, sq=sq, qf=qf, sg=sg, f=f, k=k, G=G, e_g=e_g, e_ng=e_ng, e_ge=e_ge,
                qd=qf * e_g, ki=k * e_ng, ke=k * e_ge, dec=e_g * e_ge)


def _intra_mask():
    r = lax.broadcasted_iota(jnp.int32, (GROUP, GROUP), 0)
    c = lax.broadcasted_iota(jnp.int32, (GROUP, GROUP), 1)
    return (r // CHUNK == c // CHUNK) & (c <= r)


def _chunk_outer(lhs_rows, rhs_b, out_scr, sb):
    lane = lax.broadcasted_iota(jnp.int32, (GROUP, GROUP), 1) // CHUNK
    for g in range(sb // GROUP):
        sl = slice(g * GROUP, (g + 1) * GROUP)
        lhs_t = lhs_rows[sl].T
        for cc in range(CH_PER_GROUP):
            masked = jnp.where(lane == cc, lhs_t, 0.0).astype(BF16)
            out_scr[g * CH_PER_GROUP + cc] = jnp.dot(masked, rhs_b[sl], preferred_element_type=F32)


def _hgrn_forward_blocks(cs, vs, st0s, sb, o_scr, kv_scr, st_scr, dec_scr):
    nc = sb // CHUNK
    n_str = len(cs)
    mask = _intra_mask()
    bf = []
    for i, (c, v) in enumerate(zip(cs, vs)):
        qd_b, ki_b, ke_b, v_b = (c["qd"].astype(BF16), c["ki"].astype(BF16), c["ke"].astype(BF16),
                                 v.astype(BF16))
        bf.append((qd_b, ki_b, ke_b, v_b))
        for g in range(sb // GROUP):
            sl = slice(g * GROUP, (g + 1) * GROUP)
            sc = lax.dot_general(qd_b[sl], ki_b[sl], NT_DIMS, preferred_element_type=F32)
            a = jnp.where(mask, sc, 0.0).astype(BF16)
            o_scr[i, sl, :] = jnp.dot(a, v_b[sl], preferred_element_type=F32)
        _chunk_outer(v, ke_b, kv_scr.at[i], sb)
        dec_scr[i] = c["dec"]

    def rec(n, sts):
        row = pl.ds(pl.multiple_of(n * CHUNK, CHUNK), 1)
        out = []
        for i in range(n_str):
            st_scr[i, n] = sts[i]
            out.append(sts[i] * dec_scr[i, row, :] + kv_scr[i, n])
        return tuple(out)

    ends = lax.fori_loop(0, nc, rec, tuple(st0s))

    for n in range(nc):
        rows = slice(n * CHUNK, (n + 1) * CHUNK)
        for i in range(n_str):
            o_scr[i, rows, :] += lax.dot_general(bf[i][0][rows], st_scr[i, n].astype(BF16), NT_DIMS,
                                                 preferred_element_type=F32)
    return ends, bf


def _hgrn_fwd(proj5, lb_logits, gn):
    _, Bl, S, D = proj5.shape
    H = D // HEAD
    sb = min(SUB_BLOCK, S)
    nsb = S // sb
    nc = sb // CHUNK

    def body(p_ref, lbl_ref, gn_ref, ain_ref, aint_ref, o_ref, st_ref, carry, o_scr, kv_scr, st_scr, dec_scr):
        @pl.when(pl.program_id(1) == 0)
        def _():
            carry[...] = jnp.zeros_like(carry)

        st0s = [carry[b] for b in range(Bl)]
        cs = [_hgrn_gates(p_ref[0, b], p_ref[1, b], lbl_ref[...]) for b in range(Bl)]
        ends, _ = _hgrn_forward_blocks(cs, [p_ref[2, b] for b in range(Bl)], st0s, sb,
                                       o_scr, kv_scr, st_scr, dec_scr)
        for b in range(Bl):
            carry[b] = ends[b]
            st_ref[b, 0] = st_scr[b].astype(BF16)
            o = o_scr[b]
            o_ref[b] = o
            rinv = lax.rsqrt(jnp.mean(o * o, axis=-1, keepdims=True) + RMS_EPS)
            ain = o * rinv * gn_ref[...] * _sigmoid(p_ref[3, b])
            ain_ref[b] = ain.astype(BF16)
            aint_ref[b] = ain.T.astype(BF16)

    return pl.pallas_call(
        body, name="hgrn_fwd", grid=(H, nsb),
        in_specs=[pl.BlockSpec((4, Bl, sb, HEAD), lambda h, s: (0, 0, s, h)),
                  pl.BlockSpec((2, HEAD), lambda h, s: (0, h)),
                  pl.BlockSpec((1, HEAD), lambda h, s: (0, h))],
        out_specs=[pl.BlockSpec((Bl, sb, HEAD), lambda h, s: (0, s, h)),
                   pl.BlockSpec((Bl, HEAD, sb), lambda h, s: (0, h, s)),
                   pl.BlockSpec((Bl, sb, HEAD), lambda h, s: (0, s, h)),
                   pl.BlockSpec((Bl, 1, nc, HEAD, HEAD), lambda h, s: (0, h, s, 0, 0))],
        out_shape=[jax.ShapeDtypeStruct((Bl, S, D), BF16), jax.ShapeDtypeStruct((Bl, D, S), BF16),
                   jax.ShapeDtypeStruct((Bl, S, D), F32),
                   jax.ShapeDtypeStruct((Bl, H, S // CHUNK, HEAD, HEAD), BF16)],
        scratch_shapes=[pltpu.VMEM((Bl, HEAD, HEAD), F32), pltpu.VMEM((Bl, sb, HEAD), F32),
                        pltpu.VMEM((Bl, nc, HEAD, HEAD), F32), pltpu.VMEM((Bl, nc, HEAD, HEAD), F32),
                        pltpu.VMEM((Bl, sb, HEAD), F32)],
        compiler_params=_params(("parallel", "arbitrary")),
    )(proj5, lb_logits, gn)


def _window_count(shape, g):
    pos = lax.broadcasted_iota(jnp.int32, shape, 0)
    return pos, jnp.minimum(pos + 1, jnp.left_shift(2, g)).astype(F32)


def _select_window(g, sums):
    return jnp.where(g == 0, sums[0], jnp.where(g == 1, sums[1], jnp.where(g == 2, sums[2], sums[3])))


def _pool_fwd(proj5, w_pool):
    _, Bl, S, D = proj5.shape
    pg = D // POOL_GROUPS

    def body(v_ref, w_ref, pooled_t_ref, bp_ref):
        g = pl.program_id(1)
        v = v_ref[0, 0]
        pos, cnt = _window_count(v.shape, g)
        cur, sums = v, []
        for sh in (1, 2, 4, 8):
            cur = cur + jnp.where(pos >= sh, pltpu.roll(cur, sh, 0), 0.0)
            sums.append(cur)
        pooled = _select_window(g, sums) / cnt - v
        pooled_t_ref[...] = pooled.T.astype(BF16)
        bp_ref[0] = jnp.dot(pooled.astype(BF16), w_ref[0], preferred_element_type=F32)

    return pl.pallas_call(
        body, name="pool_fwd", grid=(Bl, POOL_GROUPS),
        in_specs=[pl.BlockSpec((1, 1, S, pg), lambda b, g: (4, b, 0, g)),
                  pl.BlockSpec((1, pg, pg), lambda b, g: (g, 0, 0))],
        out_specs=[pl.BlockSpec((pg, S), lambda b, g: (g, b)),
                   pl.BlockSpec((1, S, pg), lambda b, g: (b, 0, g))],
        out_shape=[jax.ShapeDtypeStruct((D, Bl * S), BF16), jax.ShapeDtypeStruct((Bl, S, D), F32)],
        compiler_params=_params(("parallel", "parallel")),
    )(proj5, w_pool)


def _layer_norm_fwd(r):
    mu = jnp.mean(r, axis=-1, keepdims=True)
    d = r - mu
    rs = lax.rsqrt(jnp.mean(d * d, axis=-1, keepdims=True) + LN_EPS)
    return d * rs, rs


def _layer_norm_bwd(dy_g, xhat, rs):
    return rs * (dy_g - jnp.mean(dy_g, axis=-1, keepdims=True)
                 - xhat * jnp.mean(dy_g * xhat, axis=-1, keepdims=True))


def _mix_fwd(ain, proj, bp, x2, w_a, w_out, ps, g1, b1):
    T, D = x2.shape
    tm = min(ROW_TILE, T)

    def body(ain_ref, ga_ref, gb_ref, bp_ref, x_ref, wa_ref, wo_ref, ps_ref, g1_ref, b1_ref,
             a_ref, mgt_ref, xh_ref, rs_ref, x1b_ref, x1t_ref):
        a = jnp.dot(ain_ref[...], wa_ref[...], preferred_element_type=F32)
        a_ref[...] = a
        merged = _sigmoid(ga_ref[0]) * a + _sigmoid(gb_ref[0]) * (bp_ref[...] * ps_ref[...])
        mgt_ref[...] = merged.T.astype(BF16)
        r1 = ALPHA * x_ref[...] + jnp.dot(merged.astype(BF16), wo_ref[...], preferred_element_type=F32)
        xhat, rs = _layer_norm_fwd(r1)
        xh_ref[...] = xhat
        rs_ref[...] = rs
        x1 = xhat * g1_ref[...] + b1_ref[...]
        x1b_ref[...] = x1.astype(BF16)
        x1t_ref[...] = x1.T.astype(BF16)

    row = lambda i: (i, 0)
    col = lambda i: (0, i)
    full = lambda i: (0, 0)
    return pl.pallas_call(
        body, name="mix_fwd", grid=(T // tm,),
        in_specs=[pl.BlockSpec((tm, D), row),
                  pl.BlockSpec((1, tm, D), lambda i: (5, i, 0)),
                  pl.BlockSpec((1, tm, D), lambda i: (6, i, 0)),
                  pl.BlockSpec((tm, D), row), pl.BlockSpec((tm, D), row),
                  pl.BlockSpec((D, D), full), pl.BlockSpec((D, D), full),
                  pl.BlockSpec((1, D), full), pl.BlockSpec((1, D), full), pl.BlockSpec((1, D), full)],
        out_specs=[pl.BlockSpec((tm, D), row), pl.BlockSpec((D, tm), col), pl.BlockSpec((tm, D), row),
                   pl.BlockSpec((tm, 1), row), pl.BlockSpec((tm, D), row), pl.BlockSpec((D, tm), col)],
        out_shape=[jax.ShapeDtypeStruct((T, D), F32), jax.ShapeDtypeStruct((D, T), BF16),
                   jax.ShapeDtypeStruct((T, D), F32), jax.ShapeDtypeStruct((T, 1), F32),
                   jax.ShapeDtypeStruct((T, D), BF16), jax.ShapeDtypeStruct((D, T), BF16)],
        compiler_params=_params(("parallel",)),
    )(ain, proj, proj, bp, x2, w_a, w_out, ps, g1, b1)


def _mlp_fwd(x1b, w_up, w_down, xhat1, tgt, g1, b1, g2, b2):
    T, D = xhat1.shape
    FF = w_up.shape[1]
    tm = min(MLP_ROW_TILE, T)

    def body(x_ref, wu_ref, wd_ref, xh_ref, t_ref, g1_ref, b1_ref, g2_ref, b2_ref,
             hp_ref, h_ref, dr_ref, drb_ref, drt_ref, vec_ref):
        @pl.when(pl.program_id(0) == 0)
        def _():
            vec_ref[...] = jnp.zeros_like(vec_ref)

        xb = x_ref[...]
        x1 = xh_ref[...] * g1_ref[...] + b1_ref[...]
        r2 = ALPHA * x1
        for f in range(FF // D):
            cols = slice(f * D, (f + 1) * D)
            hp = jnp.dot(xb, wu_ref[:, cols], preferred_element_type=F32)
            hp_ref[:, cols] = hp
            h = jnp.square(jnp.maximum(hp, 0.0)).astype(BF16)
            h_ref[:, cols] = h
            r2 = r2 + jnp.dot(h, wd_ref[cols, :], preferred_element_type=F32)
        xhat2, rs2 = _layer_norm_fwd(r2)
        err = xhat2 * g2_ref[...] + b2_ref[...] - t_ref[...]
        dy = err / D
        vec_ref[5:6, :] += jnp.sum(dy * xhat2, axis=0, keepdims=True)
        vec_ref[6:7, :] += jnp.sum(dy, axis=0, keepdims=True)
        vec_ref[7:8, :] += jnp.sum(0.5 * err * err / D, axis=0, keepdims=True)
        dr = _layer_norm_bwd(dy * g2_ref[...], xhat2, rs2)
        dr_ref[...] = dr
        drb_ref[...] = dr.astype(BF16)
        drt_ref[...] = dr.T.astype(BF16)

    row = lambda i: (i, 0)
    full = lambda i: (0, 0)
    return pl.pallas_call(
        body, name="mlp_fwd", grid=(T // tm,),
        in_specs=[pl.BlockSpec((tm, D), row), _resident((D, FF)), _resident((FF, D)),
                  pl.BlockSpec((tm, D), row), pl.BlockSpec((tm, D), row),
                  pl.BlockSpec((1, D), full), pl.BlockSpec((1, D), full),
                  pl.BlockSpec((1, D), full), pl.BlockSpec((1, D), full)],
        out_specs=[pl.BlockSpec((tm, FF), row), pl.BlockSpec((tm, FF), row), pl.BlockSpec((tm, D), row),
                   pl.BlockSpec((tm, D), row), pl.BlockSpec((D, tm), lambda i: (0, i)),
                   pl.BlockSpec((8, D), full)],
        out_shape=[jax.ShapeDtypeStruct((T, FF), F32), jax.ShapeDtypeStruct((T, FF), BF16),
                   jax.ShapeDtypeStruct((T, D), F32), jax.ShapeDtypeStruct((T, D), BF16),
                   jax.ShapeDtypeStruct((D, T), BF16), jax.ShapeDtypeStruct((8, D), F32)],
        compiler_params=_params(("arbitrary",)),
    )(x1b, w_up, w_down, xhat1, tgt, g1, b1, g2, b2)


def _mlp_bwd(drb, dr, hp, w_up, w_down, xhat1, rs1, g1):
    T, D = dr.shape
    FF = hp.shape[1]
    tm = min(MLP_ROW_TILE, T)

    def body(drb_ref, dr_ref, hp_ref, wu_ref, wd_ref, xh_ref, rs_ref, g1_ref,
             dhp_ref, d1_ref, d1b_ref, vec_ref):
        @pl.when(pl.program_id(0) == 0)
        def _():
            vec_ref[...] = jnp.zeros_like(vec_ref)

        drb = drb_ref[...]
        dx1 = ALPHA * dr_ref[...]
        for f in range(FF // D):
            cols = slice(f * D, (f + 1) * D)
            dh = lax.dot_general(drb, wd_ref[cols, :], NT_DIMS, preferred_element_type=F32)
            dhp = (dh * (2.0 * jnp.maximum(hp_ref[:, cols], 0.0))).astype(BF16)
            dhp_ref[:, cols] = dhp
            dx1 = dx1 + lax.dot_general(dhp, wu_ref[:, cols], NT_DIMS, preferred_element_type=F32)
        xhat = xh_ref[...]
        vec_ref[3:4, :] += jnp.sum(dx1 * xhat, axis=0, keepdims=True)
        vec_ref[4:5, :] += jnp.sum(dx1, axis=0, keepdims=True)
        d1 = _layer_norm_bwd(dx1 * g1_ref[...], xhat, rs_ref[...])
        d1_ref[...] = d1
        d1b_ref[...] = d1.astype(BF16)

    row = lambda i: (i, 0)
    full = lambda i: (0, 0)
    return pl.pallas_call(
        body, name="mlp_bwd", grid=(T // tm,),
        in_specs=[pl.BlockSpec((tm, D), row), pl.BlockSpec((tm, D), row), pl.BlockSpec((tm, FF), row),
                  _resident((D, FF)), _resident((FF, D)),
                  pl.BlockSpec((tm, D), row), pl.BlockSpec((tm, 1), row), pl.BlockSpec((1, D), full)],
        out_specs=[pl.BlockSpec((tm, FF), row), pl.BlockSpec((tm, D), row), pl.BlockSpec((tm, D), row),
                   pl.BlockSpec((8, D), full)],
        out_shape=[jax.ShapeDtypeStruct((T, FF), BF16), jax.ShapeDtypeStruct((T, D), F32),
                   jax.ShapeDtypeStruct((T, D), BF16), jax.ShapeDtypeStruct((8, D), F32)],
        compiler_params=_params(("arbitrary",)),
    )(drb, dr, hp, w_up, w_down, xhat1, rs1, g1)


def _dw(name, a_t, b, n_j, a_spec, b_spec, o_shape, o_block, o_map, transpose_out=False, dep=None,
        into=(None, None), ob_shape=None, ob_map=None):
    def body(*refs):
        a_ref, b_ref, o_ref, ob_ref = refs[0], refs[1], refs[-2], refs[-1]
        b_val = b_ref[0] if len(b_ref.shape) == 3 else b_ref[...]
        if len(a_ref.shape) == 3:
            seq = a_ref.shape[2]
            p = sum(jnp.dot(a_ref[i], b_val[i * seq:(i + 1) * seq], preferred_element_type=F32)
                    for i in range(a_ref.shape[0]))
        else:
            p = jnp.dot(a_ref[...], b_val, preferred_element_type=F32)
        if transpose_out:
            p = p.T
        p = p.reshape(o_ref.shape)
        o_ref[...] = p
        ob_ref[...] = p.astype(BF16)

    kw = dict(name=name, grid=(n_j,), in_specs=[a_spec, b_spec],
              out_specs=[pl.BlockSpec(o_block, o_map), pl.BlockSpec(o_block, ob_map or o_map)],
              out_shape=[jax.ShapeDtypeStruct(o_shape, F32), jax.ShapeDtypeStruct(ob_shape or o_shape, BF16)],
              compiler_params=_params(("parallel",)))
    args = (a_t, b)
    aliases = {}
    for out_index, arr in enumerate(into):
        if arr is not None:
            aliases[len(args)] = out_index
            args = args + (arr,)
            kw["in_specs"] = kw["in_specs"] + [pl.BlockSpec(memory_space=pl.ANY)]
    if aliases:
        kw["input_output_aliases"] = aliases
    if dep is None:
        return pl.pallas_call(body, **kw)(*args)
    return _call_after(dep, body, args, **kw)


def _mix_bwd(d1b, proj, a, bp, w_a, w_out, w_pool, ps, dep):
    T, D = a.shape
    tm = min(ROW_TILE, T)
    pg = D // POOL_GROUPS

    def body(d1b_ref, ga_ref, gb_ref, a_ref, bp_ref, wa_ref, wo_ref, wp_ref, ps_ref,
             da_ref, dbp_ref, dain_ref, dpl_ref, dg_ref, vec_ref):
        @pl.when(pl.program_id(0) == 0)
        def _():
            vec_ref[...] = jnp.zeros_like(vec_ref)

        dm = lax.dot_general(d1b_ref[...], wo_ref[...], NT_DIMS, preferred_element_type=F32)
        sa, sg = _sigmoid(ga_ref[0]), _sigmoid(gb_ref[0])
        bp_v, ps_v = bp_ref[...], ps_ref[...]
        da = (dm * sa).astype(BF16)
        db = dm * sg
        dg_ref[0] = (dm * a_ref[...] * sa * (1.0 - sa)).astype(BF16)
        dg_ref[1] = (dm * (bp_v * ps_v) * sg * (1.0 - sg)).astype(BF16)
        vec_ref[2:3, :] += jnp.sum(db * bp_v, axis=0, keepdims=True)
        dbp = (db * ps_v).astype(BF16)
        da_ref[...] = da
        dbp_ref[...] = dbp
        dain_ref[...] = lax.dot_general(da, wa_ref[...], NT_DIMS, preferred_element_type=F32)
        for g in range(POOL_GROUPS):
            cols = slice(g * pg, (g + 1) * pg)
            dpl_ref[:, cols] = lax.dot_general(dbp[:, cols], wp_ref[g], NT_DIMS,
                                               preferred_element_type=F32)

    row = lambda i: (i, 0)
    full = lambda i: (0, 0)
    return _call_after(
        dep, body, (d1b, proj, proj, a, bp, w_a, w_out, w_pool, ps), name="mix_bwd", grid=(T // tm,),
        in_specs=[pl.BlockSpec((tm, D), row),
                  pl.BlockSpec((1, tm, D), lambda i: (5, i, 0)),
                  pl.BlockSpec((1, tm, D), lambda i: (6, i, 0)),
                  pl.BlockSpec((tm, D), row), pl.BlockSpec((tm, D), row),
                  pl.BlockSpec((D, D), full), pl.BlockSpec((D, D), full),
                  pl.BlockSpec((POOL_GROUPS, pg, pg), lambda i: (0, 0, 0)),
                  pl.BlockSpec((1, D), full)],
        out_specs=[pl.BlockSpec((tm, D), row), pl.BlockSpec((tm, D), row),
                   pl.BlockSpec((tm, D), row), pl.BlockSpec((tm, D), row),
                   pl.BlockSpec((2, tm, D), lambda i: (0, i, 0)),
                   pl.BlockSpec((8, D), full)],
        out_shape=[jax.ShapeDtypeStruct((T, D), BF16), jax.ShapeDtypeStruct((T, D), BF16),
                   jax.ShapeDtypeStruct((T, D), F32), jax.ShapeDtypeStruct((T, D), F32),
                   jax.ShapeDtypeStruct((2, T, D), BF16), jax.ShapeDtypeStruct((8, D), F32)],
        compiler_params=_params(("arbitrary",)))


def _pool_bwd(dpooled3, dep):
    Bl, S, D = dpooled3.shape
    pg = D // POOL_GROUPS

    def body(dp_ref, dv_ref):
        g = pl.program_id(1)
        dp = dp_ref[0]
        pos, cnt = _window_count(dp.shape, g)
        cur, sums = dp / cnt, []
        for sh in (1, 2, 4, 8):
            cur = cur + jnp.where(pos < S - sh, pltpu.roll(cur, S - sh, 0), 0.0)
            sums.append(cur)
        dv_ref[0] = (_select_window(g, sums) - dp).astype(BF16)

    spec = pl.BlockSpec((1, S, pg), lambda b, g: (b, 0, g))
    return _call_after(
        dep, body, (dpooled3,), name="pool_bwd", grid=(Bl, POOL_GROUPS), in_specs=[spec], out_specs=spec,
        out_shape=jax.ShapeDtypeStruct((Bl, S, D), BF16),
        compiler_params=_params(("parallel", "parallel")))


def _hgrn_bwd(proj5, lb_logits, gn, dain3, o3, st_all, dep):
    _, Bl, S, D = proj5.shape
    H = D // HEAD
    sb = min(SUB_BLOCK, S)
    nsb = S // sb
    nc = sb // CHUNK
    streams = range(Bl)

    def body(p_ref, lbl_ref, gn_ref, dain_ref, o_ref, st_ref, d_ref, vec_ref,
             dcarry, kv_scr, dst_scr, dec_scr, dvi_scr, dke_scr, dqi_scr):
        s = pl.program_id(1)

        @pl.when(s == 0)
        def _():
            dcarry[...] = jnp.zeros_like(dcarry)
            vec_ref[...] = jnp.zeros_like(vec_ref)

        qs, vs, ogs = [p_ref[0, b] for b in streams], [p_ref[2, b] for b in streams], [p_ref[3, b] for b in streams]
        cs = [_hgrn_gates(qs[b], p_ref[1, b], lbl_ref[...]) for b in streams]
        bf = [(cs[b]["qd"].astype(BF16), cs[b]["ki"].astype(BF16), cs[b]["ke"].astype(BF16),
               vs[b].astype(BF16)) for b in streams]
        mask = _intra_mask()
        gn_v = gn_ref[...]
        keep = []
        for b in streams:
            qd_b, ki_b, ke_b, v_b = bf[b]
            dec_scr[b] = cs[b]["dec"]
            o = o_ref[b]
            rinv = lax.rsqrt(jnp.mean(o * o, axis=-1, keepdims=True) + RMS_EPS)
            on = o * rinv
            so = _sigmoid(ogs[b])
            dain = dain_ref[b]
            vec_ref[1:2, :] += jnp.sum(dain * on * so, axis=0, keepdims=True)
            d_og = dain * on * gn_v * so * (1.0 - so)
            d_on = dain * gn_v * so
            do = rinv * (d_on - on * jnp.mean(d_on * on, axis=-1, keepdims=True))
            do_b = do.astype(BF16)
            dv_parts, dqd_parts, dki_parts = [], [], []
            for g in range(sb // GROUP):
                sl = slice(g * GROUP, (g + 1) * GROUP)
                sc = lax.dot_general(qd_b[sl], ki_b[sl], NT_DIMS, preferred_element_type=F32)
                a = jnp.where(mask, sc, 0.0).astype(BF16)
                da = lax.dot_general(do_b[sl], v_b[sl], NT_DIMS, preferred_element_type=F32)
                da = jnp.where(mask, da, 0.0).astype(BF16)
                dv_parts.append(lax.dot_general(a, do_b[sl], TN_DIMS, preferred_element_type=F32))
                dqd_parts.append(jnp.dot(da, ki_b[sl], preferred_element_type=F32))
                dki_parts.append(lax.dot_general(da, qd_b[sl], TN_DIMS, preferred_element_type=F32))
            keep.append(dict(d_og=d_og, do_b=do_b, dv_intra=jnp.concatenate(dv_parts, axis=0),
                             dqd_intra=jnp.concatenate(dqd_parts, axis=0),
                             dki=jnp.concatenate(dki_parts, axis=0)))
            _chunk_outer(do, qd_b, kv_scr.at[b], sb)

        def rrec(i, dsts):
            n = nc - 1 - i
            row = pl.ds(pl.multiple_of(n * CHUNK, CHUNK), 1)
            out = []
            for b in streams:
                dst_scr[b, n] = dsts[b]
                out.append(dsts[b] * dec_scr[b, row, :] + kv_scr[b, n])
            return tuple(out)

        ends = lax.fori_loop(0, nc, rrec, tuple(dcarry[b] for b in streams))
        for b in streams:
            dcarry[b] = ends[b]
        for n in range(nc):
            rows = slice(n * CHUNK, (n + 1) * CHUNK)
            for b in streams:
                qd_b, ki_b, ke_b, v_b = bf[b]
                dst_b = dst_scr[b, n].astype(BF16)
                dvi_scr[b, rows, :] = lax.dot_general(ke_b[rows], dst_b, NT_DIMS, preferred_element_type=F32)
                dke_scr[b, rows, :] = jnp.dot(v_b[rows], dst_b, preferred_element_type=F32)
                dqi_scr[b, rows, :] = jnp.dot(keep[b]["do_b"][rows], st_ref[b, 0, n],
                                              preferred_element_type=F32)
        for b in streams:
            c, k = cs[b], keep[b]
            ddec = jnp.sum(dst_scr[b] * st_ref[b, 0].astype(F32), axis=1)
            dgl = jnp.broadcast_to(ddec[:, None, :], (nc, CHUNK, HEAD)).reshape(sb, HEAD) * c["dec"]
            dqd = k["dqd_intra"] + dqi_scr[b]
            dke = dke_scr[b]
            dki = k["dki"]
            t_ke = dke * c["ke"]
            dG = dqd * c["qd"] - dki * c["ki"] - t_ke
            dgl = dgl + _chunk_cumsum(t_ke) + _chunk_cumsum(t_ke, reverse=True) - t_ke
            dlogf = _chunk_cumsum(dG, reverse=True) + dgl
            dk = dki * c["e_ng"] + dke * c["e_ge"]
            df = dlogf / c["f"] - dk
            sg, sq, lb, q = c["sg"], c["sq"], c["lb"], qs[b]
            vec_ref[0:1, :] += jnp.sum(df * (1.0 - sg), axis=0, keepdims=True)
            d_ref[0, b] = (dqd * c["e_g"] * Q_SCALE * (sq + q * sq * (1.0 - sq))).astype(BF16)
            d_ref[1, b] = (df * (1.0 - lb) * sg * (1.0 - sg)).astype(BF16)
            d_ref[2, b] = (k["dv_intra"] + dvi_scr[b]).astype(BF16)
            d_ref[3, b] = k["d_og"].astype(BF16)

    rev = lambda s: nsb - 1 - s
    big = pltpu.VMEM((Bl, nc, HEAD, HEAD), F32)
    rows_f32 = pltpu.VMEM((Bl, sb, HEAD), F32)
    return _call_after(
        dep, body, (proj5, lb_logits, gn, dain3, o3, st_all), name="hgrn_bwd", grid=(H, nsb),
        in_specs=[pl.BlockSpec((4, Bl, sb, HEAD), lambda h, s: (0, 0, rev(s), h)),
                  pl.BlockSpec((2, HEAD), lambda h, s: (0, h)),
                  pl.BlockSpec((1, HEAD), lambda h, s: (0, h)),
                  pl.BlockSpec((Bl, sb, HEAD), lambda h, s: (0, rev(s), h)),
                  pl.BlockSpec((Bl, sb, HEAD), lambda h, s: (0, rev(s), h)),
                  pl.BlockSpec((Bl, 1, nc, HEAD, HEAD), lambda h, s: (0, h, rev(s), 0, 0))],
        out_specs=[pl.BlockSpec((4, Bl, sb, HEAD), lambda h, s: (0, 0, rev(s), h)),
                   pl.BlockSpec((8, HEAD), lambda h, s: (0, h))],
        out_shape=[jax.ShapeDtypeStruct((4, Bl, S, D), BF16), jax.ShapeDtypeStruct((8, D), F32)],
        scratch_shapes=[pltpu.VMEM((Bl, HEAD, HEAD), F32), big, big, rows_f32, rows_f32, rows_f32, rows_f32],
        compiler_params=_params(("parallel", "arbitrary")))


def _dx(d1, dh4, dpv, dg2, w_in, dep):
    T, D = d1.shape
    tm = min(ROW_TILE, T)

    def body(d1_ref, dh_ref, dp_ref, dg_ref, w_ref, o_ref):
        blocks = [dh_ref[0], dh_ref[1], dh_ref[2], dh_ref[3], dp_ref[...], dg_ref[0], dg_ref[1]]
        acc = ALPHA * d1_ref[...]
        for j, blk in enumerate(blocks):
            acc = acc + lax.dot_general(blk, w_ref[:, j * D:(j + 1) * D], NT_DIMS, preferred_element_type=F32)
        o_ref[...] = acc

    row = lambda i: (i, 0)
    return _call_after(
        dep, body, (d1, dh4, dpv, dg2, w_in), name="dx", grid=(T // tm,),
        in_specs=[pl.BlockSpec((tm, D), row), pl.BlockSpec((4, tm, D), lambda i: (0, i, 0)),
                  pl.BlockSpec((tm, D), row), pl.BlockSpec((2, tm, D), lambda i: (0, i, 0)),
                  _resident((D, N_SEC * D))],
        out_specs=pl.BlockSpec((tm, D), row),
        out_shape=jax.ShapeDtypeStruct((T, D), F32),
        compiler_params=_params(("parallel",)))


def _dw_in_part(name, x_t, b, sections, first_sec, into, dep, ob_shape, ob_first):
    D, T = x_t.shape
    per = D // DW_COLS
    b_spec = (pl.BlockSpec((1, T, DW_COLS), lambda j: (j // per, 0, j % per)) if b.ndim == 3
              else pl.BlockSpec((T, DW_COLS), lambda j: (0, j)))
    return _dw(name, x_t, b, sections * per, _resident((D, T)), b_spec, (D, N_SEC * D), (D, DW_COLS),
               lambda j: (0, first_sec * per + j), dep=dep, into=into, ob_shape=ob_shape,
               ob_map=lambda j: (0, ob_first * per + j))


def _dw_in_rec(x_t, dh4, dep):
    D = x_t.shape[0]
    return _dw_in_part("dw_in_rec", x_t, dh4, EARLY_SEC, 0, (None, None), dep, (D, EARLY_SEC * D), 0)


def _dw_in_rest(x_t, dpv, dg2, f32_rec, dep):
    D = x_t.shape[0]
    rest_shape = (D, (N_SEC - EARLY_SEC) * D)
    f32, bf = _dw_in_part("dw_in_gates", x_t, dg2, 2, 5, (f32_rec, None), dep, rest_shape, 1)
    return _dw_in_part("dw_in_pool", x_t, dpv, 1, 4, (f32, bf), None, rest_shape, 0)


def _adam_shard(name, me_arr, grad, land, layout, w, m, v):
    shape = layout.shape
    n_split = 4
    blk = (shape[0] // n_split,) + shape[1:]
    zeros = (0,) * (len(shape) - 1)

    def body(me_ref, g_ref, r_ref, w_ref, m_ref, v_ref, g_out, d_out, m_out, v_out):
        g = g_ref[...]
        for k in range(N_DEV - 1):
            g = g + r_ref[k].astype(F32)
        d, m2, v2 = _adamw(w_ref[...], g, m_ref[...], v_ref[...])
        g_out[...] = g
        d_out[...] = d
        m_out[...] = m2
        v_out[...] = v2

    def own(i, me_ref):
        bi = layout.block_index(me_ref[0])
        return (bi[0] * n_split + i,) + tuple(bi[1:]) if layout.kind == "row" else (i,) + tuple(bi[1:])

    plain = pl.BlockSpec(blk, lambda i, me_ref: (i,) + zeros)
    grid_spec = pltpu.PrefetchScalarGridSpec(
        num_scalar_prefetch=1, grid=(n_split,),
        in_specs=[pl.BlockSpec(blk, own),
                  pl.BlockSpec((N_DEV - 1,) + blk, lambda i, me_ref: (0, i) + zeros),
                  plain, plain, plain],
        out_specs=[plain] * 4)
    return pl.pallas_call(
        body, name=name, grid_spec=grid_spec,
        out_shape=[jax.ShapeDtypeStruct(shape, F32)] * 4,
        compiler_params=_params(("parallel",)),
    )(me_arr, grad, land, w, m, v)


def _vec_allreduce(vec):
    D = vec.shape[1]

    def body(vec_ref, tot_ref, gat, send_sems, recv_sems):
        x, y, c = _me()
        me = 4 * x + 2 * y + c
        gat[me] = vec_ref[...]
        copies = []
        for k in range(1, N_DEV):
            cp = pltpu.make_async_remote_copy(
                src_ref=vec_ref, dst_ref=gat.at[me], send_sem=send_sems.at[k - 1],
                recv_sem=recv_sems.at[k - 1], device_id=_peer(k, x, y, c), device_id_type=MESH)
            cp.start()
            copies.append(cp)
        for cp in copies:
            cp.wait()
        tot = gat[0]
        for d in range(1, N_DEV):
            tot = tot + gat[d]
        tot_ref[...] = tot

    vm = pl.BlockSpec(memory_space=pltpu.VMEM)
    return pl.pallas_call(
        body, name="vec_allreduce", out_shape=jax.ShapeDtypeStruct(vec.shape, F32),
        in_specs=[vm], out_specs=vm,
        scratch_shapes=[pltpu.VMEM((N_DEV, 8, D), F32), pltpu.SemaphoreType.DMA((N_DEV - 1,)),
                        pltpu.SemaphoreType.DMA((N_DEV - 1,))],
    )(vec)


def _vec_adam(tot, small_w, small_m, small_v):
    n = len(small_w)

    def body(*refs):
        tot = refs[0][...]
        ws, ms, vs = refs[1:1 + n], refs[1 + n:1 + 2 * n], refs[1 + 2 * n:1 + 3 * n]
        outs = refs[1 + 3 * n:]
        loss_ref, g_out, d_out = outs[0], outs[1:1 + n], outs[1 + n:1 + 2 * n]
        m_out, v_out = outs[1 + 2 * n:1 + 3 * n], outs[1 + 3 * n:1 + 4 * n]
        loss_ref[...] = jnp.broadcast_to(jnp.sum(tot[7:8, :], axis=1, keepdims=True), loss_ref.shape)
        lbl = ws[0][...]
        mx = jnp.maximum(lbl[0:1, :], lbl[1:2, :])
        e0, e1 = jnp.exp(lbl[0:1, :] - mx), jnp.exp(lbl[1:2, :] - mx)
        p0 = e0 / (e0 + e1)
        dl0 = tot[0:1, :] * p0 * (1.0 - p0)
        grads = [jnp.concatenate([dl0, -dl0], axis=0)] + [tot[r:r + 1, :] for r in range(1, n)]
        for i in range(n):
            d, m2, v2 = _adamw(ws[i][...], grads[i], ms[i][...], vs[i][...])
            g_out[i][...] = grads[i]
            d_out[i][...] = d
            m_out[i][...] = m2
            v_out[i][...] = v2

    vm = pl.BlockSpec(memory_space=pltpu.VMEM)
    shapes = [jax.ShapeDtypeStruct(w.shape, F32) for w in small_w]
    return pl.pallas_call(
        body, name="vec_adam",
        out_shape=[jax.ShapeDtypeStruct((1, 128), F32)] + shapes * 4,
        in_specs=[vm] * (1 + 3 * n), out_specs=[vm] * (1 + 4 * n),
    )(tot, *small_w, *small_m, *small_v)


def kernel(x, w_in, lb_logits, hgrn_norm_g, w_a, w_pool, pool_scale, w_out, ln1_g, ln1_b, w_up, w_down, ln2_g, ln2_b, loss_target, m_w_in, m_lb_logits, m_hgrn_norm_g, m_w_a, m_w_pool, m_pool_scale, m_w_out, m_ln1_g, m_ln1_b, m_w_up, m_w_down, m_ln2_g, m_ln2_b, v_w_in, v_lb_logits, v_hgrn_norm_g, v_w_a, v_w_pool, v_pool_scale, v_w_out, v_ln1_g, v_ln1_b, v_w_up, v_w_down, v_ln2_g, v_ln2_b):
    Bl, S, D = x.shape
    T = Bl * S
    pg = D // POOL_GROUPS
    x2 = x.reshape(T, D)
    tgt = loss_target.reshape(T, D)
    me = 4 * lax.axis_index("x") + 2 * lax.axis_index("y") + lax.axis_index("c")
    me_arr = jnp.reshape(me, (1,)).astype(jnp.int32)

    names = ["w_in", "w_a", "w_pool", "w_out", "w_up", "w_down"]
    big_w = dict(zip(names, [w_in[0], w_a[0], w_pool[0], w_out[0], w_up[0], w_down[0]]))
    big_m = dict(zip(names, [m_w_in[0], m_w_a[0], m_w_pool[0], m_w_out[0], m_w_up[0], m_w_down[0]]))
    big_v = dict(zip(names, [v_w_in[0], v_w_a[0], v_w_pool[0], v_w_out[0], v_w_up[0], v_w_down[0]]))
    kinds = dict(w_in="col", w_a="row", w_pool="pool", w_out="row", w_up="col", w_down="row")
    lay = {nm: _Sharded(kinds[nm], big_w[nm].shape) for nm in names}
    wb = {nm: big_w[nm].astype(BF16) for nm in names}

    (w_in_f,) = _all_gather("ag_w_in", [wb["w_in"]], [lay["w_in"]])
    def gather_start(name, nms, after):
        return _exchange_start(name, [wb[nm] for nm in nms], [lax.empty(lay[nm].full_shape, BF16) for nm in nms],
                               src_at=lambda w, ref, peer: ref,
                               dst_at=lambda w, ref, mine, k: lay[nms[w]].at(ref, mine), after=after, own=True)

    ag_mix = gather_start("ag_mix", ["w_a", "w_pool", "w_out"], w_in_f)
    ag_mlp = gather_start("ag_mlp", ["w_up", "w_down"], ag_mix["token"])

    proj, x_t = _proj(x2, w_in_f, ag_mlp["token"])
    proj5 = proj.reshape(N_SEC, Bl, S, D)
    ain3, ain_t, o3, st_all = _hgrn_fwd(proj5, lb_logits, hgrn_norm_g)
    w_a_f, w_pool_f, w_out_f = _exchange_wait(ag_mix, ain3)
    pooled_t, bp3 = _pool_fwd(proj5, w_pool_f)
    ain, bp = ain3.reshape(T, D), bp3.reshape(T, D)
    a, merged_t, xhat1, rs1, x1b, x1_t = _mix_fwd(ain, proj, bp, x2, w_a_f, w_out_f, pool_scale, ln1_g, ln1_b)
    w_up_f, w_down_f = _exchange_wait(ag_mlp, x1b)
    hp, h, dr2, dr2b, dr2_t, vec_mlp = _mlp_fwd(x1b, w_up_f, w_down_f, xhat1, tgt, ln1_g, ln1_b, ln2_g, ln2_b)

    def scatter_start(name, nms, grads_b, after):
        lands = [lax.empty((N_DEV - 1,) + lay[nm].shape, BF16) for nm in nms]
        return _exchange_start(name, grads_b, lands,
                               src_at=lambda w, ref, peer: lay[nms[w]].at(ref, peer),
                               dst_at=lambda w, ref, mine, k: ref.at[k - 1], after=after)

    dhp, dr1, dr1b, vec_ln1 = _mlp_bwd(dr2b, dr2, hp, w_up_f, w_down_f, xhat1, rs1, ln1_g)
    FF = 4 * D
    whole_t = _resident((D, T))
    cols_b = pl.BlockSpec((T, DW_COLS), lambda j: (0, j))
    cols_o = ((D, DW_COLS), lambda j: (0, j))
    gw, gwb = {}, {}
    gw["w_down"], gwb["w_down"] = _dw(
        "dw_down", dr2_t, h, FF // DW_COLS, whole_t, cols_b, (FF, D), (DW_COLS, D), lambda j: (j, 0),
        transpose_out=True)
    rs_down = scatter_start("rs_w_down", ["w_down"], [gwb["w_down"]], gw["w_down"])
    gw["w_up"], gwb["w_up"] = _dw("dw_up", x1_t, dhp, FF // DW_COLS, whole_t, cols_b, (D, FF), *cols_o,
                                  dep=rs_down["token"])
    rs_up = scatter_start("rs_w_up", ["w_up"], [gwb["w_up"]], gw["w_up"])
    da_b, dbp_b, dain, dpooled, dg2, vec_mix = _mix_bwd(dr1b, proj, a, bp, w_a_f, w_out_f, w_pool_f, pool_scale,
                                                        rs_up["token"])
    dh4, vec_hgrn = _hgrn_bwd(proj5, lb_logits, hgrn_norm_g, dain.reshape(Bl, S, D), o3, st_all, rs_up["token"])
    dh4 = dh4.reshape(4, T, D)
    vec_tot = _vec_allreduce(vec_mlp + vec_ln1 + vec_mix + vec_hgrn)
    gw_in_rec, gwb_in_rec = _dw_in_rec(x_t, dh4, vec_tot)
    land_in = lax.empty((N_DEV - 1,) + lay["w_in"].shape, BF16)
    rs_in_rec = _w_in_scatter_start("rs_w_in_rec", gwb_in_rec, land_in, False, gw_in_rec)
    dpv = _pool_bwd(dpooled.reshape(Bl, S, D), rs_in_rec["token"]).reshape(T, D)
    gw["w_in"], gwb_in_rest = _dw_in_rest(x_t, dpv, dg2, gw_in_rec, rs_in_rec["token"])
    rs_in_rest = _w_in_scatter_start("rs_w_in_rest", gwb_in_rest, rs_in_rec["land"], True, gw["w_in"])
    gw["w_out"], gwb["w_out"] = _dw("dw_out", merged_t, dr1b, D // DW_COLS, whole_t, cols_b, (D, D), *cols_o,
                                    dep=rs_in_rest["token"])
    gw["w_a"], gwb["w_a"] = _dw("dw_a", ain_t, da_b, D // DW_COLS, _resident((Bl, D, S)), cols_b, (D, D), *cols_o,
                                dep=rs_in_rest["token"])
    gw["w_pool"], gwb["w_pool"] = _dw(
        "dw_pool", pooled_t, dbp_b, POOL_GROUPS, pl.BlockSpec((pg, T), lambda j: (j, 0)),
        pl.BlockSpec((T, pg), lambda j: (0, j)), (POOL_GROUPS, pg, pg), (1, pg, pg), lambda j: (j, 0, 0),
        dep=rs_in_rest["token"])
    mid = ["w_out", "w_a", "w_pool"]
    rs_mid = scatter_start("rs_w_mid", mid, [gwb[nm] for nm in mid], gw["w_pool"])
    grad_x2 = _dx(dr1, dh4, dpv, dg2, w_in_f, rs_mid["token"])
    grad_x = grad_x2.reshape(Bl, S, D)

    small_names =["lb_logits", "hgrn_norm_g", "pool_scale", "ln1_g", "ln1_b", "ln2_g", "ln2_b"]
    small_w = [lb_logits, hgrn_norm_g, pool_scale, ln1_g, ln1_b, ln2_g, ln2_b]
    small_m = [m_lb_logits, m_hgrn_norm_g, m_pool_scale, m_ln1_g, m_ln1_b, m_ln2_g, m_ln2_b]
    small_v = [v_lb_logits, v_hgrn_norm_g, v_pool_scale, v_ln1_g, v_ln1_b, v_ln2_g, v_ln2_b]
    res = _vec_adam(vec_tot, small_w, small_m, small_v)
    loss = res[0][0, 0]
    n = len(small_w)
    small = {nm: (res[1 + i], res[1 + n + i], res[1 + 2 * n + i], res[1 + 3 * n + i])
             for i, nm in enumerate(small_names)}

    big, last = {}, grad_x2

    def adam(nm, land):
        outs = _adam_shard("adam_" + nm, me_arr, gw[nm], land, lay[nm], big_w[nm], big_m[nm], big_v[nm])
        big[nm] = tuple(t[None] for t in outs)
        return outs[0]

    for pend, nms in ((rs_down, ["w_down"]), (rs_up, ["w_up"]), (rs_mid, mid)):
        for nm, land in zip(nms, _exchange_wait(pend, last)):
            last = adam(nm, land)
    land_in = _w_in_scatter_wait(rs_in_rec, rs_in_rest["land"], last)
    adam("w_in", _w_in_scatter_wait(rs_in_rest, land_in, res[0]))

    order = ["w_in", "lb_logits", "hgrn_norm_g", "w_a", "w_pool", "pool_scale", "w_out", "ln1_g", "ln1_b",
             "w_up", "w_down", "ln2_g", "ln2_b"]
    allp = {**big, **small}
    out = [loss, grad_x]
    for part in range(4):
        out += [allp[nm][part] for nm in order]
    return tuple(out)
```

```python
import jax
import jax.numpy as jnp
from jax import lax
from jax.experimental import pallas as pl
from jax.experimental.pallas import tpu as pltpu

F32 = jnp.float32
BF16 = jnp.bfloat16
MESH = pl.DeviceIdType.MESH

N_DEV = 8
HEAD = 128
CHUNK = 16
SUBLANES = 8
GROUP = 128
SUB_BLOCK = 1024
ROW_TILE = 512
MLP_ROW_TILE = 256
DW_COLS = 512
EARLY_SEC = 4
CH_PER_GROUP = GROUP // CHUNK
N_SEC = 7
POOL_GROUPS = 4
ALPHA = (2.0 * 1) ** 0.25
LN_EPS = 1e-5
RMS_EPS = 1e-6
Q_SCALE = HEAD ** -0.5
ADAM_LR = 0.001
ADAM_B1 = 0.9
ADAM_B2 = 0.999
ADAM_EPS = 1e-08
ADAM_WD = 0.01
ADAM_STEP = 10
VMEM_LIMIT = 60 << 20

NT_DIMS = (((1,), (1,)), ((), ()))
TN_DIMS = (((0,), (0,)), ((), ()))


def _params(sem=None):
    kw = dict(vmem_limit_bytes=VMEM_LIMIT)
    if sem is not None:
        kw["dimension_semantics"] = sem
    return pltpu.CompilerParams(**kw)


def _me():
    return lax.axis_index("x"), lax.axis_index("y"), lax.axis_index("c")


def _sigmoid(v):
    return jax.nn.sigmoid(v)


def _adamw(w, g, m, v):
    m = ADAM_B1 * m + (1.0 - ADAM_B1) * g
    v = ADAM_B2 * v + (1.0 - ADAM_B2) * jnp.square(g)
    m_hat = m / (1.0 - ADAM_B1 ** ADAM_STEP)
    v_hat = v / (1.0 - ADAM_B2 ** ADAM_STEP)
    delta = -ADAM_LR * (m_hat / (jnp.sqrt(v_hat) + ADAM_EPS) + ADAM_WD * w)
    return delta, m, v


class _Sharded:
    def __init__(self, kind, shard_shape):
        self.kind, self.shape = kind, tuple(shard_shape)

    @property
    def full_shape(self):
        r = self.shape
        if self.kind == "row":
            return (N_DEV * r[0],) + r[1:]
        return (r[0], N_DEV * r[1]) + r[2:]

    def at(self, ref, d):
        if self.kind == "col":
            n = self.shape[1]
            return ref.at[:, pl.ds(pl.multiple_of(d * n, 128), n)]
        if self.kind == "row":
            n = self.shape[0]
            return ref.at[pl.ds(pl.multiple_of(d * n, 16), n), :]
        n = self.shape[1]
        return ref.at[:, pl.ds(pl.multiple_of(d * n, 16), n), :]

    def block_index(self, d):
        return {"col": (0, d), "row": (d, 0), "pool": (0, d, 0)}[self.kind]


def _peer(k, x, y, c):
    return (1 - x if k & 4 else x, 1 - y if k & 2 else y, 1 - c if k & 1 else c)


def _all_gather(name, shards, layouts):
    nw = len(shards)

    def body(*refs):
        ins, outs = refs[:nw], refs[nw:2 * nw]
        send_sems, recv_sems, local_sems = refs[2 * nw:]
        x, y, c = _me()
        me = (x, y, c)
        sibling = (x, y, 1 - c)
        chips = [(1 - x, y), (x, 1 - y), (1 - x, 1 - y)]

        def copy(w, k, block, to, src=None):
            px, py, pc = block
            dst = layouts[w].at(outs[w], 4 * px + 2 * py + pc)
            return pltpu.make_async_remote_copy(
                src_ref=dst if src is None else src, dst_ref=dst,
                send_sem=send_sems.at[w, k], recv_sem=recv_sems.at[w, k],
                device_id=to, device_id_type=MESH)

        def place(w):
            mine = pltpu.make_async_copy(ins[w], layouts[w].at(outs[w], 4 * x + 2 * y + c), local_sems.at[w])
            mine.start()
            return mine

        first = []
        for w in range(nw):
            first.append(copy(w, 0, me, sibling, src=ins[w]))
            first += [copy(w, 1 + j, me, (*chip, c), src=ins[w]) for j, chip in enumerate(chips)]
        for cp in first:
            cp.start()
        local = [place(w) for w in range(nw)]
        passed = []
        for w in range(nw):
            for j, chip in enumerate(chips):
                copy(w, 1 + j, (*chip, c), me).wait_recv()
                fwd = copy(w, 4 + j, (*chip, c), sibling)
                fwd.start()
                passed.append(fwd)
        for w in range(nw):
            copy(w, 0, sibling, me).wait_recv()
            for j, chip in enumerate(chips):
                copy(w, 4 + j, (*chip, 1 - c), me).wait_recv()
        for cp in first + passed:
            cp.wait_send()
        for cp in local:
            cp.wait()

    any_spec = pl.BlockSpec(memory_space=pl.ANY)
    return pl.pallas_call(
        body, name=name,
        out_shape=[jax.ShapeDtypeStruct(l.full_shape, s.dtype) for s, l in zip(shards, layouts)],
        in_specs=[any_spec] * nw, out_specs=[any_spec] * nw,
        scratch_shapes=[pltpu.SemaphoreType.DMA((nw, 7)), pltpu.SemaphoreType.DMA((nw, 7)),
                        pltpu.SemaphoreType.DMA((nw,))],
    )(*shards)


HBM_SPEC = pl.BlockSpec(memory_space=pltpu.HBM)
SEM_SPEC = pl.BlockSpec(memory_space=pltpu.SEMAPHORE)
DATAFLOW = pltpu.SideEffectType.DATAFLOW_SIDE_EFFECTING


def _exchange_copies(srcs, lands, send_sems, recv_sems, src_at, dst_at):
    x, y, c = _me()
    me = 4 * x + 2 * y + c
    copies = []
    for w in range(len(srcs)):
        for k in range(1, N_DEV):
            px, py, pc = _peer(k, x, y, c)
            copies.append(pltpu.make_async_remote_copy(
                src_ref=src_at(w, srcs[w], 4 * px + 2 * py + pc), dst_ref=dst_at(w, lands[w], me, k),
                send_sem=send_sems.at[w * (N_DEV - 1) + k - 1], recv_sem=recv_sems.at[w * (N_DEV - 1) + k - 1],
                device_id=(px, py, pc), device_id_type=MESH))
    return copies


def _own_copies(srcs, lands, own_sems, src_at, dst_at):
    x, y, c = _me()
    me = 4 * x + 2 * y + c
    return [pltpu.make_async_copy(src_at(w, srcs[w], me), dst_at(w, lands[w], me, 0), own_sems.at[w])
            for w in range(len(srcs))]


def _exchange_start(name, srcs, lands, src_at, dst_at, after, own=False):
    nw = len(srcs)

    def body(*refs):
        src_refs, land_refs = refs[:nw], refs[nw:2 * nw]
        send_sems, recv_sems, own_sems = refs[2 * nw + 1], refs[2 * nw + 2], refs[2 * nw + 3]
        token = refs[-1]
        for cp in _exchange_copies(src_refs, land_refs, send_sems, recv_sems, src_at, dst_at):
            cp.start()
        if own:
            for cp in _own_copies(src_refs, land_refs, own_sems, src_at, dst_at):
                cp.start()
        token[...] = jnp.zeros_like(token)

    hbm = lambda a: pltpu.HBM(a.shape, a.dtype)
    outs = pl.pallas_call(
        body, name=name,
        out_shape=(pltpu.SemaphoreType.DMA((nw * (N_DEV - 1),)), pltpu.SemaphoreType.DMA((nw * (N_DEV - 1),)),
                   pltpu.SemaphoreType.DMA((nw,)), *[hbm(a) for a in srcs], *[hbm(a) for a in lands],
                   jax.ShapeDtypeStruct((8, 128), F32)),
        in_specs=[HBM_SPEC] * (2 * nw) + [pl.BlockSpec(memory_space=pl.ANY)],
        out_specs=(SEM_SPEC, SEM_SPEC, SEM_SPEC, *[HBM_SPEC] * (2 * nw), pl.BlockSpec(memory_space=pltpu.VMEM)),
        input_output_aliases={i: 3 + i for i in range(2 * nw)},
        compiler_params=pltpu.CompilerParams(has_side_effects=DATAFLOW),
    )(*[pltpu.with_memory_space_constraint(a, pltpu.HBM) for a in list(srcs) + list(lands)], after)
    return dict(send=outs[0], recv=outs[1], own_sems=outs[2], srcs=outs[3:3 + nw], lands=outs[3 + nw:3 + 2 * nw],
                token=outs[-1], src_at=src_at, dst_at=dst_at, name=name, own=own)


def _exchange_wait(pending, after):
    nw = len(pending["srcs"])

    def body(*refs):
        src_refs, land_refs = refs[:nw], refs[nw:2 * nw]
        send_sems, recv_sems, own_sems = refs[2 * nw], refs[2 * nw + 1], refs[2 * nw + 2]
        for cp in _exchange_copies(src_refs, land_refs, send_sems, recv_sems,
                                   pending["src_at"], pending["dst_at"]):
            cp.wait_send()
            cp.wait_recv()
        if pending["own"]:
            for cp in _own_copies(src_refs, land_refs, own_sems, pending["src_at"], pending["dst_at"]):
                cp.wait()

    hbm = lambda a: pltpu.HBM(a.shape, a.dtype)
    outs = pl.pallas_call(
        body, name=pending["name"] + "_wait",
        out_shape=(*[hbm(a) for a in pending["srcs"]], *[hbm(a) for a in pending["lands"]]),
        in_specs=[HBM_SPEC] * (2 * nw) + [SEM_SPEC, SEM_SPEC, SEM_SPEC, pl.BlockSpec(memory_space=pl.ANY)],
        out_specs=tuple([HBM_SPEC] * (2 * nw)),
        input_output_aliases={i: i for i in range(2 * nw)},
        compiler_params=pltpu.CompilerParams(has_side_effects=DATAFLOW),
    )(*pending["srcs"], *pending["lands"], pending["send"], pending["recv"], pending["own_sems"], after)
    return outs[nw:]


def _w_in_scatter_copies(src, land, send_sems, recv_sems, early):
    rows, cols = land.shape[1], land.shape[2]
    bound = EARLY_SEC * rows
    cut_dev = bound // cols
    cut = bound - cut_dev * cols
    x, y, c = _me()
    me = 4 * x + 2 * y + c

    def pieces(t):
        if early:
            return [(t > cut_dev, t * cols - bound, cols, 0), (t == cut_dev, 0, cols - cut, cut)]
        return [(t < cut_dev, t * cols, cols, 0), (t == cut_dev, cut_dev * cols, cut, 0)]

    out = []
    for k in range(1, N_DEV):
        px, py, pc = _peer(k, x, y, c)
        for (to_peer, s0, width, d0), (to_me, _, _, _) in zip(pieces(4 * px + 2 * py + pc), pieces(me)):
            s0 = s0 if isinstance(s0, int) else pl.multiple_of(jnp.maximum(s0, 0), 128)
            out.append((to_peer, to_me, pltpu.make_async_remote_copy(
                src_ref=src.at[:, pl.ds(s0, width)], dst_ref=land.at[k - 1, :, pl.ds(d0, width)],
                send_sem=send_sems.at[k - 1], recv_sem=recv_sems.at[k - 1],
                device_id=(px, py, pc), device_id_type=MESH)))
    return out


def _w_in_scatter_start(name, src, land, early, after):
    def body(src_ref, land_ref, after_ref, send_sems, recv_sems, src_thru, land_thru, token):
        for to_peer, _, cp in _w_in_scatter_copies(src_ref, land_ref, send_sems, recv_sems, early):
            pl.when(to_peer)(cp.start)
        token[...] = jnp.zeros_like(token)

    hbm = lambda a: pltpu.HBM(a.shape, a.dtype)
    outs = pl.pallas_call(
        body, name=name,
        out_shape=(pltpu.SemaphoreType.DMA((N_DEV - 1,)), pltpu.SemaphoreType.DMA((N_DEV - 1,)),
                   hbm(src), hbm(land), jax.ShapeDtypeStruct((8, 128), F32)),
        in_specs=[HBM_SPEC, HBM_SPEC, pl.BlockSpec(memory_space=pl.ANY)],
        out_specs=(SEM_SPEC, SEM_SPEC, HBM_SPEC, HBM_SPEC, pl.BlockSpec(memory_space=pltpu.VMEM)),
        input_output_aliases={0: 2, 1: 3},
        compiler_params=pltpu.CompilerParams(has_side_effects=DATAFLOW),
    )(pltpu.with_memory_space_constraint(src, pltpu.HBM), pltpu.with_memory_space_constraint(land, pltpu.HBM), after)
    return dict(send=outs[0], recv=outs[1], src=outs[2], land=outs[3], token=outs[4], early=early, name=name)


def _w_in_scatter_wait(pending, land, after):
    def body(src_ref, land_ref, send_sems, recv_sems, after_ref, src_dead, land_out):
        for to_peer, to_me, cp in _w_in_scatter_copies(src_ref, land_ref, send_sems, recv_sems, pending["early"]):
            pl.when(to_peer)(cp.wait_send)
            pl.when(to_me)(cp.wait_recv)

    hbm = lambda a: pltpu.HBM(a.shape, a.dtype)
    outs = pl.pallas_call(
        body, name=pending["name"] + "_wait", out_shape=(hbm(pending["src"]), hbm(land)),
        in_specs=[HBM_SPEC, HBM_SPEC, SEM_SPEC, SEM_SPEC, pl.BlockSpec(memory_space=pl.ANY)],
        out_specs=(HBM_SPEC, HBM_SPEC), input_output_aliases={0: 0, 1: 1},
        compiler_params=pltpu.CompilerParams(has_side_effects=DATAFLOW),
    )(pending["src"], land, pending["send"], pending["recv"], after)
    return outs[1]


def _call_after(dep, body, args, *, in_specs, **kw):
    n_in = len(args)

    def wrapped(*refs):
        body(*refs[:n_in], *refs[n_in + 1:])

    dep_spec = pl.BlockSpec(dep.shape, lambda *_: (0,) * dep.ndim)
    return pl.pallas_call(wrapped, in_specs=list(in_specs) + [dep_spec], **kw)(*args, dep)


def _resident(shape):
    return pl.BlockSpec(shape, lambda *_: (0,) * len(shape), pipeline_mode=pl.Buffered(1))


def _proj(x2, w_in, dep):
    T, D = x2.shape
    tm = min(ROW_TILE, T)

    def body(x_ref, w_ref, o_ref, xt_ref):
        x = x_ref[...]
        xt_ref[...] = x.T.astype(BF16)
        xb = x.astype(BF16)
        for j in range(N_SEC):
            o_ref[j] = jnp.dot(xb, w_ref[:, j * D:(j + 1) * D], preferred_element_type=F32)

    return _call_after(
        dep, body, (x2, w_in), name="proj", grid=(T // tm,),
        in_specs=[pl.BlockSpec((tm, D), lambda i: (i, 0)), _resident((D, N_SEC * D))],
        out_specs=[pl.BlockSpec((N_SEC, tm, D), lambda i: (0, i, 0)), pl.BlockSpec((D, tm), lambda i: (0, i))],
        out_shape=[jax.ShapeDtypeStruct((N_SEC, T, D), F32), jax.ShapeDtypeStruct((D, T), BF16)],
        compiler_params=_params(("parallel",)))


def _chunk_cumsum(v, reverse=False):
    rows, lanes = v.shape
    x = v.reshape(rows // SUBLANES, SUBLANES, lanes)
    pos = lax.broadcasted_iota(jnp.int32, x.shape, 1)
    for sh in (1, 2, 4):
        if reverse:
            x = x + jnp.where(pos < SUBLANES - sh, pltpu.roll(x, SUBLANES - sh, 1), 0.0)
        else:
            x = x + jnp.where(pos >= sh, pltpu.roll(x, sh, 1), 0.0)
    x = x.reshape(rows // CHUNK, CHUNK // SUBLANES, SUBLANES, lanes)
    half = lax.broadcasted_iota(jnp.int32, x.shape, 1)
    if reverse:
        x = x + jnp.where(half == 0, x[:, 1:2, 0:1, :], 0.0)
    else:
        x = x + jnp.where(half == 1, x[:, 0:1, SUBLANES - 1:SUBLANES, :], 0.0)
    return x.reshape(rows, lanes)


def _hgrn_gates(q, f_pre, lb_logits):
    l0, l1 = lb_logits[0:1, :], lb_logits[1:2, :]
    mx = jnp.maximum(l0, l1)
    e0, e1 = jnp.exp(l0 - mx), jnp.exp(l1 - mx)
    lb = e0 / (e0 + e1)
    sq = _sigmoid(q)
    qf = q * sq * Q_SCALE
    sg = _sigmoid(f_pre)
    f = lb + (1.0 - lb) * sg
    k = 1.0 - f
    log_f = jnp.log(f)
    G = _chunk_cumsum(log_f)
    g_to_end = _chunk_cumsum(log_f, reverse=True) - log_f
    e_g = jnp.exp(G)
    e_ng = jnp.exp(-G)
    e_ge = jnp.exp(g_to_end)
    return dict(lb=lb, sq=sq, qf=qf, sg=sg, f=f, k=k, G=G, e_g=e_g, e_ng=e_ng, e_ge=e_ge,
                qd=qf * e_g, ki=k * e_ng, ke=k * e_ge, dec=e_g * e_ge)


def _intra_mask():
    r = lax.broadcasted_iota(jnp.int32, (GROUP, GROUP), 0)
    c = lax.broadcasted_iota(jnp.int32, (GROUP, GROUP), 1)
    return (r // CHUNK == c // CHUNK) & (c <= r)


def _chunk_outer(lhs_rows, rhs_b, out_scr, sb):
    lane = lax.broadcasted_iota(jnp.int32, (GROUP, GROUP), 1) // CHUNK
    for g in range(sb // GROUP):
        sl = slice(g * GROUP, (g + 1) * GROUP)
        lhs_t = lhs_rows[sl].T
        for cc in range(CH_PER_GROUP):
            masked = jnp.where(lane == cc, lhs_t, 0.0).astype(BF16)
            out_scr[g * CH_PER_GROUP + cc] = jnp.dot(masked, rhs_b[sl], preferred_element_type=F32)


def _hgrn_forward_blocks(cs, vs, st0s, sb, o_scr, kv_scr, st_out, dec_scr):
    nc = sb // CHUNK
    n_str = len(cs)
    mask = _intra_mask()
    bf = []
    for i, (c, v) in enumerate(zip(cs, vs)):
        qd_b, ki_b, ke_b, v_b = (c["qd"].astype(BF16), c["ki"].astype(BF16), c["ke"].astype(BF16),
                                 v.astype(BF16))
        bf.append((qd_b, ki_b, ke_b, v_b))
        for g in range(sb // GROUP):
            sl = slice(g * GROUP, (g + 1) * GROUP)
            sc = lax.dot_general(qd_b[sl], ki_b[sl], NT_DIMS, preferred_element_type=F32)
            a = jnp.where(mask, sc, 0.0).astype(BF16)
            o_scr[i, sl, :] = jnp.dot(a, v_b[sl], preferred_element_type=F32)
        _chunk_outer(v, ke_b, kv_scr.at[i], sb)
        dec_scr[i] = c["dec"]

    def rec(n, sts):
        row = pl.ds(pl.multiple_of(n * CHUNK, CHUNK), 1)
        out = []
        for i in range(n_str):
            st_out[i, 0, n] = sts[i].astype(BF16)
            out.append(sts[i] * dec_scr[i, row, :] + kv_scr[i, n])
        return tuple(out)

    ends = lax.fori_loop(0, nc, rec, tuple(st0s))

    for n in range(nc):
        rows = slice(n * CHUNK, (n + 1) * CHUNK)
        for i in range(n_str):
            o_scr[i, rows, :] += lax.dot_general(bf[i][0][rows], st_out[i, 0, n], NT_DIMS,
                                                 preferred_element_type=F32)
    return ends, bf


def _hgrn_fwd(proj5, lb_logits, gn):
    _, Bl, S, D = proj5.shape
    H = D // HEAD
    sb = min(SUB_BLOCK, S)
    nsb = S // sb
    nc = sb // CHUNK

    def body(p_ref, lbl_ref, gn_ref, ain_ref, aint_ref, o_ref, st_ref, carry, o_scr, kv_scr, dec_scr):
        @pl.when(pl.program_id(1) == 0)
        def _():
            carry[...] = jnp.zeros_like(carry)

        st0s = [carry[b] for b in range(Bl)]
        cs = [_hgrn_gates(p_ref[0, b], p_ref[1, b], lbl_ref[...]) for b in range(Bl)]
        ends, _ = _hgrn_forward_blocks(cs, [p_ref[2, b] for b in range(Bl)], st0s, sb,
                                       o_scr, kv_scr, st_ref, dec_scr)
        for b in range(Bl):
            carry[b] = ends[b]
            o = o_scr[b]
            o_ref[b] = o
            rinv = lax.rsqrt(jnp.mean(o * o, axis=-1, keepdims=True) + RMS_EPS)
            ain = o * rinv * gn_ref[...] * _sigmoid(p_ref[3, b])
            ain_ref[b] = ain.astype(BF16)
            aint_ref[b] = ain.T.astype(BF16)

    return pl.pallas_call(
        body, name="hgrn_fwd", grid=(H, nsb),
        in_specs=[pl.BlockSpec((4, Bl, sb, HEAD), lambda h, s: (0, 0, s, h)),
                  pl.BlockSpec((2, HEAD), lambda h, s: (0, h)),
                  pl.BlockSpec((1, HEAD), lambda h, s: (0, h))],
        out_specs=[pl.BlockSpec((Bl, sb, HEAD), lambda h, s: (0, s, h)),
                   pl.BlockSpec((Bl, HEAD, sb), lambda h, s: (0, h, s)),
                   pl.BlockSpec((Bl, sb, HEAD), lambda h, s: (0, s, h)),
                   pl.BlockSpec((Bl, 1, nc, HEAD, HEAD), lambda h, s: (0, h, s, 0, 0))],
        out_shape=[jax.ShapeDtypeStruct((Bl, S, D), BF16), jax.ShapeDtypeStruct((Bl, D, S), BF16),
                   jax.ShapeDtypeStruct((Bl, S, D), F32),
                   jax.ShapeDtypeStruct((Bl, H, S // CHUNK, HEAD, HEAD), BF16)],
        scratch_shapes=[pltpu.VMEM((Bl, HEAD, HEAD), F32), pltpu.VMEM((Bl, sb, HEAD), F32),
                        pltpu.VMEM((Bl, nc, HEAD, HEAD), F32), pltpu.VMEM((Bl, sb, HEAD), F32)],
        compiler_params=_params(("parallel", "arbitrary")),
    )(proj5, lb_logits, gn)


def _window_count(shape, g):
    pos = lax.broadcasted_iota(jnp.int32, shape, 0)
    return pos, jnp.minimum(pos + 1, jnp.left_shift(2, g)).astype(F32)


def _select_window(g, sums):
    return jnp.where(g == 0, sums[0], jnp.where(g == 1, sums[1], jnp.where(g == 2, sums[2], sums[3])))


def _pool_fwd(proj5, w_pool):
    _, Bl, S, D = proj5.shape
    pg = D // POOL_GROUPS

    def body(v_ref, w_ref, pooled_t_ref, bp_ref):
        g = pl.program_id(1)
        v = v_ref[0, 0]
        pos, cnt = _window_count(v.shape, g)
        cur, sums = v, []
        for sh in (1, 2, 4, 8):
            cur = cur + jnp.where(pos >= sh, pltpu.roll(cur, sh, 0), 0.0)
            sums.append(cur)
        pooled = _select_window(g, sums) / cnt - v
        pooled_t_ref[...] = pooled.T.astype(BF16)
        bp_ref[0] = jnp.dot(pooled.astype(BF16), w_ref[0], preferred_element_type=F32)

    return pl.pallas_call(
        body, name="pool_fwd", grid=(Bl, POOL_GROUPS),
        in_specs=[pl.BlockSpec((1, 1, S, pg), lambda b, g: (4, b, 0, g)),
                  pl.BlockSpec((1, pg, pg), lambda b, g: (g, 0, 0))],
        out_specs=[pl.BlockSpec((pg, S), lambda b, g: (g, b)),
                   pl.BlockSpec((1, S, pg), lambda b, g: (b, 0, g))],
        out_shape=[jax.ShapeDtypeStruct((D, Bl * S), BF16), jax.ShapeDtypeStruct((Bl, S, D), F32)],
        compiler_params=_params(("parallel", "parallel")),
    )(proj5, w_pool)


def _layer_norm_fwd(r):
    mu = jnp.mean(r, axis=-1, keepdims=True)
    d = r - mu
    rs = lax.rsqrt(jnp.mean(d * d, axis=-1, keepdims=True) + LN_EPS)
    return d * rs, rs


def _layer_norm_bwd(dy_g, xhat, rs):
    return rs * (dy_g - jnp.mean(dy_g, axis=-1, keepdims=True)
                 - xhat * jnp.mean(dy_g * xhat, axis=-1, keepdims=True))


def _mix_fwd(ain, proj, bp, x2, w_a, w_out, ps, g1, b1):
    T, D = x2.shape
    tm = min(ROW_TILE, T)

    def body(ain_ref, ga_ref, gb_ref, bp_ref, x_ref, wa_ref, wo_ref, ps_ref, g1_ref, b1_ref,
             a_ref, mgt_ref, xh_ref, rs_ref, x1b_ref, x1t_ref):
        a = jnp.dot(ain_ref[...], wa_ref[...], preferred_element_type=F32)
        a_ref[...] = a
        merged = _sigmoid(ga_ref[0]) * a + _sigmoid(gb_ref[0]) * (bp_ref[...] * ps_ref[...])
        mgt_ref[...] = merged.T.astype(BF16)
        r1 = ALPHA * x_ref[...] + jnp.dot(merged.astype(BF16), wo_ref[...], preferred_element_type=F32)
        xhat, rs = _layer_norm_fwd(r1)
        xh_ref[...] = xhat
        rs_ref[...] = rs
        x1 = xhat * g1_ref[...] + b1_ref[...]
        x1b_ref[...] = x1.astype(BF16)
        x1t_ref[...] = x1.T.astype(BF16)

    row = lambda i: (i, 0)
    col = lambda i: (0, i)
    full = lambda i: (0, 0)
    return pl.pallas_call(
        body, name="mix_fwd", grid=(T // tm,),
        in_specs=[pl.BlockSpec((tm, D), row),
                  pl.BlockSpec((1, tm, D), lambda i: (5, i, 0)),
                  pl.BlockSpec((1, tm, D), lambda i: (6, i, 0)),
                  pl.BlockSpec((tm, D), row), pl.BlockSpec((tm, D), row),
                  pl.BlockSpec((D, D), full), pl.BlockSpec((D, D), full),
                  pl.BlockSpec((1, D), full), pl.BlockSpec((1, D), full), pl.BlockSpec((1, D), full)],
        out_specs=[pl.BlockSpec((tm, D), row), pl.BlockSpec((D, tm), col), pl.BlockSpec((tm, D), row),
                   pl.BlockSpec((tm, 1), row), pl.BlockSpec((tm, D), row), pl.BlockSpec((D, tm), col)],
        out_shape=[jax.ShapeDtypeStruct((T, D), F32), jax.ShapeDtypeStruct((D, T), BF16),
                   jax.ShapeDtypeStruct((T, D), F32), jax.ShapeDtypeStruct((T, 1), F32),
                   jax.ShapeDtypeStruct((T, D), BF16), jax.ShapeDtypeStruct((D, T), BF16)],
        compiler_params=_params(("parallel",)),
    )(ain, proj, proj, bp, x2, w_a, w_out, ps, g1, b1)


def _mlp_fwd(x1b, w_up, w_down, xhat1, tgt, g1, b1, g2, b2):
    T, D = xhat1.shape
    FF = w_up.shape[1]
    tm = min(MLP_ROW_TILE, T)

    def body(x_ref, wu_ref, wd_ref, xh_ref, t_ref, g1_ref, b1_ref, g2_ref, b2_ref,
             hp_ref, h_ref, dr_ref, drb_ref, drt_ref, vec_ref):
        @pl.when(pl.program_id(0) == 0)
        def _():
            vec_ref[...] = jnp.zeros_like(vec_ref)

        xb = x_ref[...]
        x1 = xh_ref[...] * g1_ref[...] + b1_ref[...]
        r2 = ALPHA * x1
        for f in range(FF // D):
            cols = slice(f * D, (f + 1) * D)
            hp = jnp.dot(xb, wu_ref[:, cols], preferred_element_type=F32)
            hp_ref[:, cols] = hp
            h = jnp.square(jnp.maximum(hp, 0.0)).astype(BF16)
            h_ref[:, cols] = h
            r2 = r2 + jnp.dot(h, wd_ref[cols, :], preferred_element_type=F32)
        xhat2, rs2 = _layer_norm_fwd(r2)
        err = xhat2 * g2_ref[...] + b2_ref[...] - t_ref[...]
        dy = err / D
        vec_ref[5:6, :] += jnp.sum(dy * xhat2, axis=0, keepdims=True)
        vec_ref[6:7, :] += jnp.sum(dy, axis=0, keepdims=True)
        vec_ref[7:8, :] += jnp.sum(0.5 * err * err / D, axis=0, keepdims=True)
        dr = _layer_norm_bwd(dy * g2_ref[...], xhat2, rs2)
        dr_ref[...] = dr
        drb_ref[...] = dr.astype(BF16)
        drt_ref[...] = dr.T.astype(BF16)

    row = lambda i: (i, 0)
    full = lambda i: (0, 0)
    return pl.pallas_call(
        body, name="mlp_fwd", grid=(T // tm,),
        in_specs=[pl.BlockSpec((tm, D), row), _resident((D, FF)), _resident((FF, D)),
                  pl.BlockSpec((tm, D), row), pl.BlockSpec((tm, D), row),
                  pl.BlockSpec((1, D), full), pl.BlockSpec((1, D), full),
                  pl.BlockSpec((1, D), full), pl.BlockSpec((1, D), full)],
        out_specs=[pl.BlockSpec((tm, FF), row), pl.BlockSpec((tm, FF), row), pl.BlockSpec((tm, D), row),
                   pl.BlockSpec((tm, D), row), pl.BlockSpec((D, tm), lambda i: (0, i)),
                   pl.BlockSpec((8, D), full)],
        out_shape=[jax.ShapeDtypeStruct((T, FF), F32), jax.ShapeDtypeStruct((T, FF), BF16),
                   jax.ShapeDtypeStruct((T, D), F32), jax.ShapeDtypeStruct((T, D), BF16),
                   jax.ShapeDtypeStruct((D, T), BF16), jax.ShapeDtypeStruct((8, D), F32)],
        compiler_params=_params(("arbitrary",)),
    )(x1b, w_up, w_down, xhat1, tgt, g1, b1, g2, b2)


def _mlp_bwd(drb, dr, hp, w_up, w_down, xhat1, rs1, g1):
    T, D = dr.shape
    FF = hp.shape[1]
    tm = min(MLP_ROW_TILE, T)

    def body(drb_ref, dr_ref, hp_ref, wu_ref, wd_ref, xh_ref, rs_ref, g1_ref,
             dhp_ref, d1_ref, d1b_ref, vec_ref):
        @pl.when(pl.program_id(0) == 0)
        def _():
            vec_ref[...] = jnp.zeros_like(vec_ref)

        drb = drb_ref[...]
        dx1 = ALPHA * dr_ref[...]
        for f in range(FF // D):
            cols = slice(f * D, (f + 1) * D)
            dh = lax.dot_general(drb, wd_ref[cols, :], NT_DIMS, preferred_element_type=F32)
            dhp = (dh * (2.0 * jnp.maximum(hp_ref[:, cols], 0.0))).astype(BF16)
            dhp_ref[:, cols] = dhp
            dx1 = dx1 + lax.dot_general(dhp, wu_ref[:, cols], NT_DIMS, preferred_element_type=F32)
        xhat = xh_ref[...]
        vec_ref[3:4, :] += jnp.sum(dx1 * xhat, axis=0, keepdims=True)
        vec_ref[4:5, :] += jnp.sum(dx1, axis=0, keepdims=True)
        d1 = _layer_norm_bwd(dx1 * g1_ref[...], xhat, rs_ref[...])
        d1_ref[...] = d1
        d1b_ref[...] = d1.astype(BF16)

    row = lambda i: (i, 0)
    full = lambda i: (0, 0)
    return pl.pallas_call(
        body, name="mlp_bwd", grid=(T // tm,),
        in_specs=[pl.BlockSpec((tm, D), row), pl.BlockSpec((tm, D), row), pl.BlockSpec((tm, FF), row),
                  _resident((D, FF)), _resident((FF, D)),
                  pl.BlockSpec((tm, D), row), pl.BlockSpec((tm, 1), row), pl.BlockSpec((1, D), full)],
        out_specs=[pl.BlockSpec((tm, FF), row), pl.BlockSpec((tm, D), row), pl.BlockSpec((tm, D), row),
                   pl.BlockSpec((8, D), full)],
        out_shape=[jax.ShapeDtypeStruct((T, FF), BF16), jax.ShapeDtypeStruct((T, D), F32),
                   jax.ShapeDtypeStruct((T, D), BF16), jax.ShapeDtypeStruct((8, D), F32)],
        compiler_params=_params(("arbitrary",)),
    )(drb, dr, hp, w_up, w_down, xhat1, rs1, g1)


def _dw(name, a_t, b, n_j, a_spec, b_spec, o_shape, o_block, o_map, transpose_out=False, dep=None,
        into=(None, None), ob_shape=None, ob_map=None):
    def body(*refs):
        a_ref, b_ref, o_ref, ob_ref = refs[0], refs[1], refs[-2], refs[-1]
        b_val = b_ref[0] if len(b_ref.shape) == 3 else b_ref[...]
        if len(a_ref.shape) == 3:
            seq = a_ref.shape[2]
            p = sum(jnp.dot(a_ref[i], b_val[i * seq:(i + 1) * seq], preferred_element_type=F32)
                    for i in range(a_ref.shape[0]))
        else:
            p = jnp.dot(a_ref[...], b_val, preferred_element_type=F32)
        if transpose_out:
            p = p.T
        p = p.reshape(o_ref.shape)
        o_ref[...] = p
        ob_ref[...] = p.astype(BF16)

    kw = dict(name=name, grid=(n_j,), in_specs=[a_spec, b_spec],
              out_specs=[pl.BlockSpec(o_block, o_map), pl.BlockSpec(o_block, ob_map or o_map)],
              out_shape=[jax.ShapeDtypeStruct(o_shape, F32), jax.ShapeDtypeStruct(ob_shape or o_shape, BF16)],
              compiler_params=_params(("parallel",)))
    args = (a_t, b)
    aliases = {}
    for out_index, arr in enumerate(into):
        if arr is not None:
            aliases[len(args)] = out_index
            args = args + (arr,)
            kw["in_specs"] = kw["in_specs"] + [pl.BlockSpec(memory_space=pl.ANY)]
    if aliases:
        kw["input_output_aliases"] = aliases
    if dep is None:
        return pl.pallas_call(body, **kw)(*args)
    return _call_after(dep, body, args, **kw)


def _mix_bwd(d1b, proj, a, bp, w_a, w_out, w_pool, ps, dep):
    T, D = a.shape
    tm = min(ROW_TILE, T)
    pg = D // POOL_GROUPS

    def body(d1b_ref, ga_ref, gb_ref, a_ref, bp_ref, wa_ref, wo_ref, wp_ref, ps_ref,
             da_ref, dbp_ref, dain_ref, dpl_ref, dg_ref, vec_ref):
        @pl.when(pl.program_id(0) == 0)
        def _():
            vec_ref[...] = jnp.zeros_like(vec_ref)

        dm = lax.dot_general(d1b_ref[...], wo_ref[...], NT_DIMS, preferred_element_type=F32)
        sa, sg = _sigmoid(ga_ref[0]), _sigmoid(gb_ref[0])
        bp_v, ps_v = bp_ref[...], ps_ref[...]
        da = (dm * sa).astype(BF16)
        db = dm * sg
        dg_ref[0] = (dm * a_ref[...] * sa * (1.0 - sa)).astype(BF16)
        dg_ref[1] = (dm * (bp_v * ps_v) * sg * (1.0 - sg)).astype(BF16)
        vec_ref[2:3, :] += jnp.sum(db * bp_v, axis=0, keepdims=True)
        dbp = (db * ps_v).astype(BF16)
        da_ref[...] = da
        dbp_ref[...] = dbp
        dain_ref[...] = lax.dot_general(da, wa_ref[...], NT_DIMS, preferred_element_type=F32)
        for g in range(POOL_GROUPS):
            cols = slice(g * pg, (g + 1) * pg)
            dpl_ref[:, cols] = lax.dot_general(dbp[:, cols], wp_ref[g], NT_DIMS,
                                               preferred_element_type=F32)

    row = lambda i: (i, 0)
    full = lambda i: (0, 0)
    return _call_after(
        dep, body, (d1b, proj, proj, a, bp, w_a, w_out, w_pool, ps), name="mix_bwd", grid=(T // tm,),
        in_specs=[pl.BlockSpec((tm, D), row),
                  pl.BlockSpec((1, tm, D), lambda i: (5, i, 0)),
                  pl.BlockSpec((1, tm, D), lambda i: (6, i, 0)),
                  pl.BlockSpec((tm, D), row), pl.BlockSpec((tm, D), row),
                  pl.BlockSpec((D, D), full), pl.BlockSpec((D, D), full),
                  pl.BlockSpec((POOL_GROUPS, pg, pg), lambda i: (0, 0, 0)),
                  pl.BlockSpec((1, D), full)],
        out_specs=[pl.BlockSpec((tm, D), row), pl.BlockSpec((tm, D), row),
                   pl.BlockSpec((tm, D), row), pl.BlockSpec((tm, D), row),
                   pl.BlockSpec((2, tm, D), lambda i: (0, i, 0)),
                   pl.BlockSpec((8, D), full)],
        out_shape=[jax.ShapeDtypeStruct((T, D), BF16), jax.ShapeDtypeStruct((T, D), BF16),
                   jax.ShapeDtypeStruct((T, D), F32), jax.ShapeDtypeStruct((T, D), F32),
                   jax.ShapeDtypeStruct((2, T, D), BF16), jax.ShapeDtypeStruct((8, D), F32)],
        compiler_params=_params(("arbitrary",)))


def _pool_bwd(dpooled3, dep):
    Bl, S, D = dpooled3.shape
    pg = D // POOL_GROUPS

    def body(dp_ref, dv_ref):
        g = pl.program_id(1)
        dp = dp_ref[0]
        pos, cnt = _window_count(dp.shape, g)
        cur, sums = dp / cnt, []
        for sh in (1, 2, 4, 8):
            cur = cur + jnp.where(pos < S - sh, pltpu.roll(cur, S - sh, 0), 0.0)
            sums.append(cur)
        dv_ref[0] = (_select_window(g, sums) - dp).astype(BF16)

    spec = pl.BlockSpec((1, S, pg), lambda b, g: (b, 0, g))
    return _call_after(
        dep, body, (dpooled3,), name="pool_bwd", grid=(Bl, POOL_GROUPS), in_specs=[spec], out_specs=spec,
        out_shape=jax.ShapeDtypeStruct((Bl, S, D), BF16),
        compiler_params=_params(("parallel", "parallel")))


def _hgrn_bwd(proj5, lb_logits, gn, dain3, o3, st_all, dep):
    _, Bl, S, D = proj5.shape
    H = D // HEAD
    sb = min(SUB_BLOCK, S)
    nsb = S // sb
    nc = sb // CHUNK
    streams = range(Bl)

    def body(p_ref, lbl_ref, gn_ref, dain_ref, o_ref, st_ref, d_ref, vec_ref,
             dcarry, kv_scr, dst_scr, dec_scr, dvi_scr, dke_scr, dqi_scr):
        s = pl.program_id(1)

        @pl.when(s == 0)
        def _():
            dcarry[...] = jnp.zeros_like(dcarry)
            vec_ref[...] = jnp.zeros_like(vec_ref)

        qs, vs, ogs = [p_ref[0, b] for b in streams], [p_ref[2, b] for b in streams], [p_ref[3, b] for b in streams]
        cs = [_hgrn_gates(qs[b], p_ref[1, b], lbl_ref[...]) for b in streams]
        bf = [(cs[b]["qd"].astype(BF16), cs[b]["ki"].astype(BF16), cs[b]["ke"].astype(BF16),
               vs[b].astype(BF16)) for b in streams]
        mask = _intra_mask()
        gn_v = gn_ref[...]
        keep = []
        for b in streams:
            qd_b, ki_b, ke_b, v_b = bf[b]
            dec_scr[b] = cs[b]["dec"]
            o = o_ref[b]
            rinv = lax.rsqrt(jnp.mean(o * o, axis=-1, keepdims=True) + RMS_EPS)
            on = o * rinv
            so = _sigmoid(ogs[b])
            dain = dain_ref[b]
            vec_ref[1:2, :] += jnp.sum(dain * on * so, axis=0, keepdims=True)
            d_og = dain * on * gn_v * so * (1.0 - so)
            d_on = dain * gn_v * so
            do = rinv * (d_on - on * jnp.mean(d_on * on, axis=-1, keepdims=True))
            do_b = do.astype(BF16)
            dv_parts, dqd_parts, dki_parts = [], [], []
            for g in range(sb // GROUP):
                sl = slice(g * GROUP, (g + 1) * GROUP)
                sc = lax.dot_general(qd_b[sl], ki_b[sl], NT_DIMS, preferred_element_type=F32)
                a = jnp.where(mask, sc, 0.0).astype(BF16)
                da = lax.dot_general(do_b[sl], v_b[sl], NT_DIMS, preferred_element_type=F32)
                da = jnp.where(mask, da, 0.0).astype(BF16)
                dv_parts.append(lax.dot_general(a, do_b[sl], TN_DIMS, preferred_element_type=F32))
                dqd_parts.append(jnp.dot(da, ki_b[sl], preferred_element_type=F32))
                dki_parts.append(lax.dot_general(da, qd_b[sl], TN_DIMS, preferred_element_type=F32))
            keep.append(dict(d_og=d_og, do_b=do_b, dv_intra=jnp.concatenate(dv_parts, axis=0),
                             dqd_intra=jnp.concatenate(dqd_parts, axis=0),
                             dki=jnp.concatenate(dki_parts, axis=0)))
            _chunk_outer(do, qd_b, kv_scr.at[b], sb)

        def rrec(i, dsts):
            n = nc - 1 - i
            row = pl.ds(pl.multiple_of(n * CHUNK, CHUNK), 1)
            out = []
            for b in streams:
                dst_scr[b, n] = dsts[b]
                out.append(dsts[b] * dec_scr[b, row, :] + kv_scr[b, n])
            return tuple(out)

        ends = lax.fori_loop(0, nc, rrec, tuple(dcarry[b] for b in streams))
        for b in streams:
            dcarry[b] = ends[b]
        for n in range(nc):
            rows = slice(n * CHUNK, (n + 1) * CHUNK)
            for b in streams:
                qd_b, ki_b, ke_b, v_b = bf[b]
                dst_b = dst_scr[b, n].astype(BF16)
                dvi_scr[b, rows, :] = lax.dot_general(ke_b[rows], dst_b, NT_DIMS, preferred_element_type=F32)
                dke_scr[b, rows, :] = jnp.dot(v_b[rows], dst_b, preferred_element_type=F32)
                dqi_scr[b, rows, :] = jnp.dot(keep[b]["do_b"][rows], st_ref[b, 0, n],
                                              preferred_element_type=F32)
        for b in streams:
            c, k = cs[b], keep[b]
            ddec = jnp.sum(dst_scr[b] * st_ref[b, 0].astype(F32), axis=1)
            dgl = jnp.broadcast_to(ddec[:, None, :], (nc, CHUNK, HEAD)).reshape(sb, HEAD) * c["dec"]
            dqd = k["dqd_intra"] + dqi_scr[b]
            dke = dke_scr[b]
            dki = k["dki"]
            t_ke = dke * c["ke"]
            dG = dqd * c["qd"] - dki * c["ki"] - t_ke
            dgl = dgl + _chunk_cumsum(t_ke) + _chunk_cumsum(t_ke, reverse=True) - t_ke
            dlogf = _chunk_cumsum(dG, reverse=True) + dgl
            dk = dki * c["e_ng"] + dke * c["e_ge"]
            df = dlogf / c["f"] - dk
            sg, sq, lb, q = c["sg"], c["sq"], c["lb"], qs[b]
            vec_ref[0:1, :] += jnp.sum(df * (1.0 - sg), axis=0, keepdims=True)
            d_ref[0, b] = (dqd * c["e_g"] * Q_SCALE * (sq + q * sq * (1.0 - sq))).astype(BF16)
            d_ref[1, b] = (df * (1.0 - lb) * sg * (1.0 - sg)).astype(BF16)
            d_ref[2, b] = (k["dv_intra"] + dvi_scr[b]).astype(BF16)
            d_ref[3, b] = k["d_og"].astype(BF16)

    rev = lambda s: nsb - 1 - s
    big = pltpu.VMEM((Bl, nc, HEAD, HEAD), F32)
    rows_f32 = pltpu.VMEM((Bl, sb, HEAD), F32)
    return _call_after(
        dep, body, (proj5, lb_logits, gn, dain3, o3, st_all), name="hgrn_bwd", grid=(H, nsb),
        in_specs=[pl.BlockSpec((4, Bl, sb, HEAD), lambda h, s: (0, 0, rev(s), h)),
                  pl.BlockSpec((2, HEAD), lambda h, s: (0, h)),
                  pl.BlockSpec((1, HEAD), lambda h, s: (0, h)),
                  pl.BlockSpec((Bl, sb, HEAD), lambda h, s: (0, rev(s), h)),
                  pl.BlockSpec((Bl, sb, HEAD), lambda h, s: (0, rev(s), h)),
                  pl.BlockSpec((Bl, 1, nc, HEAD, HEAD), lambda h, s: (0, h, rev(s), 0, 0))],
        out_specs=[pl.BlockSpec((4, Bl, sb, HEAD), lambda h, s: (0, 0, rev(s), h)),
                   pl.BlockSpec((8, HEAD), lambda h, s: (0, h))],
        out_shape=[jax.ShapeDtypeStruct((4, Bl, S, D), BF16), jax.ShapeDtypeStruct((8, D), F32)],
        scratch_shapes=[pltpu.VMEM((Bl, HEAD, HEAD), F32), big, big, rows_f32, rows_f32, rows_f32, rows_f32],
        compiler_params=_params(("parallel", "arbitrary")))


def _dx(d1, dh4, dpv, dg2, w_in, dep):
    T, D = d1.shape
    tm = min(ROW_TILE, T)

    def body(d1_ref, dh_ref, dp_ref, dg_ref, w_ref, o_ref):
        blocks = [dh_ref[0], dh_ref[1], dh_ref[2], dh_ref[3], dp_ref[...], dg_ref[0], dg_ref[1]]
        acc = ALPHA * d1_ref[...]
        for j, blk in enumerate(blocks):
            acc = acc + lax.dot_general(blk, w_ref[:, j * D:(j + 1) * D], NT_DIMS, preferred_element_type=F32)
        o_ref[...] = acc

    row = lambda i: (i, 0)
    return _call_after(
        dep, body, (d1, dh4, dpv, dg2, w_in), name="dx", grid=(T // tm,),
        in_specs=[pl.BlockSpec((tm, D), row), pl.BlockSpec((4, tm, D), lambda i: (0, i, 0)),
                  pl.BlockSpec((tm, D), row), pl.BlockSpec((2, tm, D), lambda i: (0, i, 0)),
                  _resident((D, N_SEC * D))],
        out_specs=pl.BlockSpec((tm, D), row),
        out_shape=jax.ShapeDtypeStruct((T, D), F32),
        compiler_params=_params(("parallel",)))


def _dw_in_part(name, x_t, b, sections, first_sec, into, dep, ob_shape, ob_first):
    D, T = x_t.shape
    per = D // DW_COLS
    b_spec = (pl.BlockSpec((1, T, DW_COLS), lambda j: (j // per, 0, j % per)) if b.ndim == 3
              else pl.BlockSpec((T, DW_COLS), lambda j: (0, j)))
    return _dw(name, x_t, b, sections * per, _resident((D, T)), b_spec, (D, N_SEC * D), (D, DW_COLS),
               lambda j: (0, first_sec * per + j), dep=dep, into=into, ob_shape=ob_shape,
               ob_map=lambda j: (0, ob_first * per + j))


def _dw_in_rec(x_t, dh4, dep):
    D = x_t.shape[0]
    return _dw_in_part("dw_in_rec", x_t, dh4, EARLY_SEC, 0, (None, None), dep, (D, EARLY_SEC * D), 0)


def _dw_in_rest(x_t, dpv, dg2, f32_rec, dep):
    D = x_t.shape[0]
    rest_shape = (D, (N_SEC - EARLY_SEC) * D)
    f32, bf = _dw_in_part("dw_in_gates", x_t, dg2, 2, 5, (f32_rec, None), dep, rest_shape, 1)
    return _dw_in_part("dw_in_pool", x_t, dpv, 1, 4, (f32, bf), None, rest_shape, 0)


def _adam_shard(name, me_arr, grad, land, layout, w, m, v):
    shape = layout.shape
    n_split = 4
    blk = (shape[0] // n_split,) + shape[1:]
    zeros = (0,) * (len(shape) - 1)

    def body(me_ref, g_ref, r_ref, w_ref, m_ref, v_ref, g_out, d_out, m_out, v_out):
        g = g_ref[...]
        for k in range(N_DEV - 1):
            g = g + r_ref[k].astype(F32)
        d, m2, v2 = _adamw(w_ref[...], g, m_ref[...], v_ref[...])
        g_out[...] = g
        d_out[...] = d
        m_out[...] = m2
        v_out[...] = v2

    def own(i, me_ref):
        bi = layout.block_index(me_ref[0])
        return (bi[0] * n_split + i,) + tuple(bi[1:]) if layout.kind == "row" else (i,) + tuple(bi[1:])

    plain = pl.BlockSpec(blk, lambda i, me_ref: (i,) + zeros)
    grid_spec = pltpu.PrefetchScalarGridSpec(
        num_scalar_prefetch=1, grid=(n_split,),
        in_specs=[pl.BlockSpec(blk, own),
                  pl.BlockSpec((N_DEV - 1,) + blk, lambda i, me_ref: (0, i) + zeros),
                  plain, plain, plain],
        out_specs=[plain] * 4)
    return pl.pallas_call(
        body, name=name, grid_spec=grid_spec,
        out_shape=[jax.ShapeDtypeStruct(shape, F32)] * 4,
        compiler_params=_params(("parallel",)),
    )(me_arr, grad, land, w, m, v)


def _vec_allreduce(vec):
    D = vec.shape[1]

    def body(vec_ref, tot_ref, gat, send_sems, recv_sems):
        x, y, c = _me()
        me = 4 * x + 2 * y + c
        gat[me] = vec_ref[...]
        copies = []
        for k in range(1, N_DEV):
            cp = pltpu.make_async_remote_copy(
                src_ref=vec_ref, dst_ref=gat.at[me], send_sem=send_sems.at[k - 1],
                recv_sem=recv_sems.at[k - 1], device_id=_peer(k, x, y, c), device_id_type=MESH)
            cp.start()
            copies.append(cp)
        for cp in copies:
            cp.wait()
        tot = gat[0]
        for d in range(1, N_DEV):
            tot = tot + gat[d]
        tot_ref[...] = tot

    vm = pl.BlockSpec(memory_space=pltpu.VMEM)
    return pl.pallas_call(
        body, name="vec_allreduce", out_shape=jax.ShapeDtypeStruct(vec.shape, F32),
        in_specs=[vm], out_specs=vm,
        scratch_shapes=[pltpu.VMEM((N_DEV, 8, D), F32), pltpu.SemaphoreType.DMA((N_DEV - 1,)),
                        pltpu.SemaphoreType.DMA((N_DEV - 1,))],
    )(vec)


def _vec_adam(tot, small_w, small_m, small_v):
    n = len(small_w)

    def body(*refs):
        tot = refs[0][...]
        ws, ms, vs = refs[1:1 + n], refs[1 + n:1 + 2 * n], refs[1 + 2 * n:1 + 3 * n]
        outs = refs[1 + 3 * n:]
        loss_ref, g_out, d_out = outs[0], outs[1:1 + n], outs[1 + n:1 + 2 * n]
        m_out, v_out = outs[1 + 2 * n:1 + 3 * n], outs[1 + 3 * n:1 + 4 * n]
        loss_ref[...] = jnp.broadcast_to(jnp.sum(tot[7:8, :], axis=1, keepdims=True), loss_ref.shape)
        lbl = ws[0][...]
        mx = jnp.maximum(lbl[0:1, :], lbl[1:2, :])
        e0, e1 = jnp.exp(lbl[0:1, :] - mx), jnp.exp(lbl[1:2, :] - mx)
        p0 = e0 / (e0 + e1)
        dl0 = tot[0:1, :] * p0 * (1.0 - p0)
        grads = [jnp.concatenate([dl0, -dl0], axis=0)] + [tot[r:r + 1, :] for r in range(1, n)]
        for i in range(n):
            d, m2, v2 = _adamw(ws[i][...], grads[i], ms[i][...], vs[i][...])
            g_out[i][...] = grads[i]
            d_out[i][...] = d
            m_out[i][...] = m2
            v_out[i][...] = v2

    vm = pl.BlockSpec(memory_space=pltpu.VMEM)
    shapes = [jax.ShapeDtypeStruct(w.shape, F32) for w in small_w]
    return pl.pallas_call(
        body, name="vec_adam",
        out_shape=[jax.ShapeDtypeStruct((1, 128), F32)] + shapes * 4,
        in_specs=[vm] * (1 + 3 * n), out_specs=[vm] * (1 + 4 * n),
    )(tot, *small_w, *small_m, *small_v)


def kernel(x, w_in, lb_logits, hgrn_norm_g, w_a, w_pool, pool_scale, w_out, ln1_g, ln1_b, w_up, w_down, ln2_g, ln2_b, loss_target, m_w_in, m_lb_logits, m_hgrn_norm_g, m_w_a, m_w_pool, m_pool_scale, m_w_out, m_ln1_g, m_ln1_b, m_w_up, m_w_down, m_ln2_g, m_ln2_b, v_w_in, v_lb_logits, v_hgrn_norm_g, v_w_a, v_w_pool, v_pool_scale, v_w_out, v_ln1_g, v_ln1_b, v_w_up, v_w_down, v_ln2_g, v_ln2_b):
    Bl, S, D = x.shape
    T = Bl * S
    pg = D // POOL_GROUPS
    x2 = x.reshape(T, D)
    tgt = loss_target.reshape(T, D)
    me = 4 * lax.axis_index("x") + 2 * lax.axis_index("y") + lax.axis_index("c")
    me_arr = jnp.reshape(me, (1,)).astype(jnp.int32)

    names = ["w_in", "w_a", "w_pool", "w_out", "w_up", "w_down"]
    big_w = dict(zip(names, [w_in[0], w_a[0], w_pool[0], w_out[0], w_up[0], w_down[0]]))
    big_m = dict(zip(names, [m_w_in[0], m_w_a[0], m_w_pool[0], m_w_out[0], m_w_up[0], m_w_down[0]]))
    big_v = dict(zip(names, [v_w_in[0], v_w_a[0], v_w_pool[0], v_w_out[0], v_w_up[0], v_w_down[0]]))
    kinds = dict(w_in="col", w_a="row", w_pool="pool", w_out="row", w_up="col", w_down="row")
    lay = {nm: _Sharded(kinds[nm], big_w[nm].shape) for nm in names}
    wb = {nm: big_w[nm].astype(BF16) for nm in names}

    (w_in_f,) = _all_gather("ag_w_in", [wb["w_in"]], [lay["w_in"]])
    def gather_start(name, nms, after):
        return _exchange_start(name, [wb[nm] for nm in nms], [lax.empty(lay[nm].full_shape, BF16) for nm in nms],
                               src_at=lambda w, ref, peer: ref,
                               dst_at=lambda w, ref, mine, k: lay[nms[w]].at(ref, mine), after=after, own=True)

    ag_mix = gather_start("ag_mix", ["w_a", "w_pool", "w_out"], w_in_f)
    ag_mlp = gather_start("ag_mlp", ["w_up", "w_down"], ag_mix["token"])

    proj, x_t = _proj(x2, w_in_f, ag_mlp["token"])
    proj5 = proj.reshape(N_SEC, Bl, S, D)
    ain3, ain_t, o3, st_all = _hgrn_fwd(proj5, lb_logits, hgrn_norm_g)
    w_a_f, w_pool_f, w_out_f = _exchange_wait(ag_mix, ain3)
    pooled_t, bp3 = _pool_fwd(proj5, w_pool_f)
    ain, bp = ain3.reshape(T, D), bp3.reshape(T, D)
    a, merged_t, xhat1, rs1, x1b, x1_t = _mix_fwd(ain, proj, bp, x2, w_a_f, w_out_f, pool_scale, ln1_g, ln1_b)
    w_up_f, w_down_f = _exchange_wait(ag_mlp, x1b)
    hp, h, dr2, dr2b, dr2_t, vec_mlp = _mlp_fwd(x1b, w_up_f, w_down_f, xhat1, tgt, ln1_g, ln1_b, ln2_g, ln2_b)

    def scatter_start(name, nms, grads_b, after):
        lands = [lax.empty((N_DEV - 1,) + lay[nm].shape, BF16) for nm in nms]
        return _exchange_start(name, grads_b, lands,
                               src_at=lambda w, ref, peer: lay[nms[w]].at(ref, peer),
                               dst_at=lambda w, ref, mine, k: ref.at[k - 1], after=after)

    dhp, dr1, dr1b, vec_ln1 = _mlp_bwd(dr2b, dr2, hp, w_up_f, w_down_f, xhat1, rs1, ln1_g)
    FF = 4 * D
    whole_t = _resident((D, T))
    cols_b = pl.BlockSpec((T, DW_COLS), lambda j: (0, j))
    cols_o = ((D, DW_COLS), lambda j: (0, j))
    gw, gwb = {}, {}
    gw["w_down"], gwb["w_down"] = _dw(
        "dw_down", dr2_t, h, FF // DW_COLS, whole_t, cols_b, (FF, D), (DW_COLS, D), lambda j: (j, 0),
        transpose_out=True)
    rs_down = scatter_start("rs_w_down", ["w_down"], [gwb["w_down"]], gw["w_down"])
    gw["w_up"], gwb["w_up"] = _dw("dw_up", x1_t, dhp, FF // DW_COLS, whole_t, cols_b, (D, FF), *cols_o,
                                  dep=rs_down["token"])
    rs_up = scatter_start("rs_w_up", ["w_up"], [gwb["w_up"]], gw["w_up"])
    da_b, dbp_b, dain, dpooled, dg2, vec_mix = _mix_bwd(dr1b, proj, a, bp, w_a_f, w_out_f, w_pool_f, pool_scale,
                                                        rs_up["token"])
    dh4, vec_hgrn = _hgrn_bwd(proj5, lb_logits, hgrn_norm_g, dain.reshape(Bl, S, D), o3, st_all, rs_up["token"])
    dh4 = dh4.reshape(4, T, D)
    vec_tot = _vec_allreduce(vec_mlp + vec_ln1 + vec_mix + vec_hgrn)
    gw_in_rec, gwb_in_rec = _dw_in_rec(x_t, dh4, vec_tot)
    land_in = lax.empty((N_DEV - 1,) + lay["w_in"].shape, BF16)
    rs_in_rec = _w_in_scatter_start("rs_w_in_rec", gwb_in_rec, land_in, False, gw_in_rec)
    gw["w_out"], gwb["w_out"] = _dw("dw_out", merged_t, dr1b, D // DW_COLS, whole_t, cols_b, (D, D), *cols_o,
                                    dep=rs_in_rec["token"])
    gw["w_a"], gwb["w_a"] = _dw("dw_a", ain_t, da_b, D // DW_COLS, _resident((Bl, D, S)), cols_b, (D, D), *cols_o,
                                dep=rs_in_rec["token"])
    gw["w_pool"], gwb["w_pool"] = _dw(
        "dw_pool", pooled_t, dbp_b, POOL_GROUPS, pl.BlockSpec((pg, T), lambda j: (j, 0)),
        pl.BlockSpec((T, pg), lambda j: (0, j)), (POOL_GROUPS, pg, pg), (1, pg, pg), lambda j: (j, 0, 0),
        dep=rs_in_rec["token"])
    mid = ["w_out", "w_a", "w_pool"]
    rs_mid = scatter_start("rs_w_mid", mid, [gwb[nm] for nm in mid], gw["w_pool"])
    dpv = _pool_bwd(dpooled.reshape(Bl, S, D), rs_mid["token"]).reshape(T, D)
    gw["w_in"], gwb_in_rest = _dw_in_rest(x_t, dpv, dg2, gw_in_rec, rs_mid["token"])
    rs_in_rest = _w_in_scatter_start("rs_w_in_rest", gwb_in_rest, rs_in_rec["land"], True, gw["w_in"])
    grad_x2 = _dx(dr1, dh4, dpv, dg2, w_in_f, rs_in_rest["token"])
    grad_x = grad_x2.reshape(Bl, S, D)

    small_names =["lb_logits", "hgrn_norm_g", "pool_scale", "ln1_g", "ln1_b", "ln2_g", "ln2_b"]
    small_w = [lb_logits, hgrn_norm_g, pool_scale, ln1_g, ln1_b, ln2_g, ln2_b]
    small_m = [m_lb_logits, m_hgrn_norm_g, m_pool_scale, m_ln1_g, m_ln1_b, m_ln2_g, m_ln2_b]
    small_v = [v_lb_logits, v_hgrn_norm_g, v_pool_scale, v_ln1_g, v_ln1_b, v_ln2_g, v_ln2_b]
    res = _vec_adam(vec_tot, small_w, small_m, small_v)
    loss = res[0][0, 0]
    n = len(small_w)
    small = {nm: (res[1 + i], res[1 + n + i], res[1 + 2 * n + i], res[1 + 3 * n + i])
             for i, nm in enumerate(small_names)}

    big, last = {}, grad_x2

    def adam(nm, land):
        outs = _adam_shard("adam_" + nm, me_arr, gw[nm], land, lay[nm], big_w[nm], big_m[nm], big_v[nm])
        big[nm] = tuple(t[None] for t in outs)
        return outs[0]

    for pend, nms in ((rs_down, ["w_down"]), (rs_up, ["w_up"]), (rs_mid, mid)):
        for nm, land in zip(nms, _exchange_wait(pend, last)):
            last = adam(nm, land)
    land_in = _w_in_scatter_wait(rs_in_rec, rs_in_rest["land"], last)
    adam("w_in", _w_in_scatter_wait(rs_in_rest, land_in, res[0]))

    order = ["w_in", "lb_logits", "hgrn_norm_g", "w_a", "w_pool", "pool_scale", "w_out", "ln1_g", "ln1_b",
             "w_up", "w_down", "ln2_g", "ln2_b"]
    allp = {**big, **small}
    out = [loss, grad_x]
    for part in range(4):
        out += [allp[nm][part] for nm in order]
    return tuple(out)
```

```python
import jax
import jax.numpy as jnp
from jax import lax
from jax.experimental import pallas as pl
from jax.experimental.pallas import tpu as pltpu

F32 = jnp.float32
BF16 = jnp.bfloat16
MESH = pl.DeviceIdType.MESH

N_DEV = 8
HEAD = 128
CHUNK = 16
SUBLANES = 8
GROUP = 128
SUB_BLOCK = 1024
ROW_TILE = 512
MLP_ROW_TILE = 256
DW_COLS = 512
EARLY_SEC = 4
CH_PER_GROUP = GROUP // CHUNK
N_SEC = 7
POOL_GROUPS = 4
ALPHA = (2.0 * 1) ** 0.25
LN_EPS = 1e-5
RMS_EPS = 1e-6
Q_SCALE = HEAD ** -0.5
ADAM_LR = 0.001
ADAM_B1 = 0.9
ADAM_B2 = 0.999
ADAM_EPS = 1e-08
ADAM_WD = 0.01
ADAM_STEP = 10
VMEM_LIMIT = 60 << 20

NT_DIMS = (((1,), (1,)), ((), ()))
TN_DIMS = (((0,), (0,)), ((), ()))


def _params(sem=None):
    kw = dict(vmem_limit_bytes=VMEM_LIMIT)
    if sem is not None:
        kw["dimension_semantics"] = sem
    return pltpu.CompilerParams(**kw)


def _me():
    return lax.axis_index("x"), lax.axis_index("y"), lax.axis_index("c")


def _sigmoid(v):
    return jax.nn.sigmoid(v)


def _adamw(w, g, m, v):
    m = ADAM_B1 * m + (1.0 - ADAM_B1) * g
    v = ADAM_B2 * v + (1.0 - ADAM_B2) * jnp.square(g)
    m_hat = m / (1.0 - ADAM_B1 ** ADAM_STEP)
    v_hat = v / (1.0 - ADAM_B2 ** ADAM_STEP)
    delta = -ADAM_LR * (m_hat / (jnp.sqrt(v_hat) + ADAM_EPS) + ADAM_WD * w)
    return delta, m, v


class _Sharded:
    def __init__(self, kind, shard_shape):
        self.kind, self.shape = kind, tuple(shard_shape)

    @property
    def full_shape(self):
        r = self.shape
        if self.kind == "row":
            return (N_DEV * r[0],) + r[1:]
        return (r[0], N_DEV * r[1]) + r[2:]

    def at(self, ref, d):
        if self.kind == "col":
            n = self.shape[1]
            return ref.at[:, pl.ds(pl.multiple_of(d * n, 128), n)]
        if self.kind == "row":
            n = self.shape[0]
            return ref.at[pl.ds(pl.multiple_of(d * n, 16), n), :]
        n = self.shape[1]
        return ref.at[:, pl.ds(pl.multiple_of(d * n, 16), n), :]

    def block_index(self, d):
        return {"col": (0, d), "row": (d, 0), "pool": (0, d, 0)}[self.kind]


def _peer(k, x, y, c):
    return (1 - x if k & 4 else x, 1 - y if k & 2 else y, 1 - c if k & 1 else c)


def _all_gather(name, shards, layouts):
    nw = len(shards)

    def body(*refs):
        ins, outs = refs[:nw], refs[nw:2 * nw]
        send_sems, recv_sems, local_sems = refs[2 * nw:]
        x, y, c = _me()
        me = (x, y, c)
        sibling = (x, y, 1 - c)
        chips = [(1 - x, y), (x, 1 - y), (1 - x, 1 - y)]

        def copy(w, k, block, to, src=None):
            px, py, pc = block
            dst = layouts[w].at(outs[w], 4 * px + 2 * py + pc)
            return pltpu.make_async_remote_copy(
                src_ref=dst if src is None else src, dst_ref=dst,
                send_sem=send_sems.at[w, k], recv_sem=recv_sems.at[w, k],
                device_id=to, device_id_type=MESH)

        def place(w):
            mine = pltpu.make_async_copy(ins[w], layouts[w].at(outs[w], 4 * x + 2 * y + c), local_sems.at[w])
            mine.start()
            return mine

        first = []
        for w in range(nw):
            first.append(copy(w, 0, me, sibling, src=ins[w]))
            first += [copy(w, 1 + j, me, (*chip, c), src=ins[w]) for j, chip in enumerate(chips)]
        for cp in first:
            cp.start()
        local = [place(w) for w in range(nw)]
        passed = []
        for w in range(nw):
            for j, chip in enumerate(chips):
                copy(w, 1 + j, (*chip, c), me).wait_recv()
                fwd = copy(w, 4 + j, (*chip, c), sibling)
                fwd.start()
                passed.append(fwd)
        for w in range(nw):
            copy(w, 0, sibling, me).wait_recv()
            for j, chip in enumerate(chips):
                copy(w, 4 + j, (*chip, 1 - c), me).wait_recv()
        for cp in first + passed:
            cp.wait_send()
        for cp in local:
            cp.wait()

    any_spec = pl.BlockSpec(memory_space=pl.ANY)
    return pl.pallas_call(
        body, name=name,
        out_shape=[jax.ShapeDtypeStruct(l.full_shape, s.dtype) for s, l in zip(shards, layouts)],
        in_specs=[any_spec] * nw, out_specs=[any_spec] * nw,
        scratch_shapes=[pltpu.SemaphoreType.DMA((nw, 7)), pltpu.SemaphoreType.DMA((nw, 7)),
                        pltpu.SemaphoreType.DMA((nw,))],
    )(*shards)


HBM_SPEC = pl.BlockSpec(memory_space=pltpu.HBM)
SEM_SPEC = pl.BlockSpec(memory_space=pltpu.SEMAPHORE)
DATAFLOW = pltpu.SideEffectType.DATAFLOW_SIDE_EFFECTING


def _exchange_copies(srcs, lands, send_sems, recv_sems, src_at, dst_at):
    x, y, c = _me()
    me = 4 * x + 2 * y + c
    copies = []
    for w in range(len(srcs)):
        for k in range(1, N_DEV):
            px, py, pc = _peer(k, x, y, c)
            copies.append(pltpu.make_async_remote_copy(
                src_ref=src_at(w, srcs[w], 4 * px + 2 * py + pc), dst_ref=dst_at(w, lands[w], me, k),
                send_sem=send_sems.at[w * (N_DEV - 1) + k - 1], recv_sem=recv_sems.at[w * (N_DEV - 1) + k - 1],
                device_id=(px, py, pc), device_id_type=MESH))
    return copies


def _own_copies(srcs, lands, own_sems, src_at, dst_at):
    x, y, c = _me()
    me = 4 * x + 2 * y + c
    return [pltpu.make_async_copy(src_at(w, srcs[w], me), dst_at(w, lands[w], me, 0), own_sems.at[w])
            for w in range(len(srcs))]


def _exchange_start(name, srcs, lands, src_at, dst_at, after, own=False):
    nw = len(srcs)

    def body(*refs):
        src_refs, land_refs = refs[:nw], refs[nw:2 * nw]
        send_sems, recv_sems, own_sems = refs[2 * nw + 1], refs[2 * nw + 2], refs[2 * nw + 3]
        token = refs[-1]
        for cp in _exchange_copies(src_refs, land_refs, send_sems, recv_sems, src_at, dst_at):
            cp.start()
        if own:
            for cp in _own_copies(src_refs, land_refs, own_sems, src_at, dst_at):
                cp.start()
        token[...] = jnp.zeros_like(token)

    hbm = lambda a: pltpu.HBM(a.shape, a.dtype)
    outs = pl.pallas_call(
        body, name=name,
        out_shape=(pltpu.SemaphoreType.DMA((nw * (N_DEV - 1),)), pltpu.SemaphoreType.DMA((nw * (N_DEV - 1),)),
                   pltpu.SemaphoreType.DMA((nw,)), *[hbm(a) for a in srcs], *[hbm(a) for a in lands],
                   jax.ShapeDtypeStruct((8, 128), F32)),
        in_specs=[HBM_SPEC] * (2 * nw) + [pl.BlockSpec(memory_space=pl.ANY)],
        out_specs=(SEM_SPEC, SEM_SPEC, SEM_SPEC, *[HBM_SPEC] * (2 * nw), pl.BlockSpec(memory_space=pltpu.VMEM)),
        input_output_aliases={i: 3 + i for i in range(2 * nw)},
        compiler_params=pltpu.CompilerParams(has_side_effects=DATAFLOW),
    )(*[pltpu.with_memory_space_constraint(a, pltpu.HBM) for a in list(srcs) + list(lands)], after)
    return dict(send=outs[0], recv=outs[1], own_sems=outs[2], srcs=outs[3:3 + nw], lands=outs[3 + nw:3 + 2 * nw],
                token=outs[-1], src_at=src_at, dst_at=dst_at, name=name, own=own)


def _exchange_wait(pending, after):
    nw = len(pending["srcs"])

    def body(*refs):
        src_refs, land_refs = refs[:nw], refs[nw:2 * nw]
        send_sems, recv_sems, own_sems = refs[2 * nw], refs[2 * nw + 1], refs[2 * nw + 2]
        for cp in _exchange_copies(src_refs, land_refs, send_sems, recv_sems,
                                   pending["src_at"], pending["dst_at"]):
            cp.wait_send()
            cp.wait_recv()
        if pending["own"]:
            for cp in _own_copies(src_refs, land_refs, own_sems, pending["src_at"], pending["dst_at"]):
                cp.wait()

    hbm = lambda a: pltpu.HBM(a.shape, a.dtype)
    outs = pl.pallas_call(
        body, name=pending["name"] + "_wait",
        out_shape=(*[hbm(a) for a in pending["srcs"]], *[hbm(a) for a in pending["lands"]]),
        in_specs=[HBM_SPEC] * (2 * nw) + [SEM_SPEC, SEM_SPEC, SEM_SPEC, pl.BlockSpec(memory_space=pl.ANY)],
        out_specs=tuple([HBM_SPEC] * (2 * nw)),
        input_output_aliases={i: i for i in range(2 * nw)},
        compiler_params=pltpu.CompilerParams(has_side_effects=DATAFLOW),
    )(*pending["srcs"], *pending["lands"], pending["send"], pending["recv"], pending["own_sems"], after)
    return outs[nw:]


def _w_in_scatter_copies(src, land, send_sems, recv_sems, early):
    rows, cols = land.shape[1], land.shape[2]
    bound = EARLY_SEC * rows
    cut_dev = bound // cols
    cut = bound - cut_dev * cols
    x, y, c = _me()
    me = 4 * x + 2 * y + c

    def pieces(t):
        if early:
            return [(t > cut_dev, t * cols - bound, cols, 0), (t == cut_dev, 0, cols - cut, cut)]
        return [(t < cut_dev, t * cols, cols, 0), (t == cut_dev, cut_dev * cols, cut, 0)]

    out = []
    for k in range(1, N_DEV):
        px, py, pc = _peer(k, x, y, c)
        for (to_peer, s0, width, d0), (to_me, _, _, _) in zip(pieces(4 * px + 2 * py + pc), pieces(me)):
            s0 = s0 if isinstance(s0, int) else pl.multiple_of(jnp.maximum(s0, 0), 128)
            out.append((to_peer, to_me, pltpu.make_async_remote_copy(
                src_ref=src.at[:, pl.ds(s0, width)], dst_ref=land.at[k - 1, :, pl.ds(d0, width)],
                send_sem=send_sems.at[k - 1], recv_sem=recv_sems.at[k - 1],
                device_id=(px, py, pc), device_id_type=MESH)))
    return out


def _w_in_scatter_start(name, src, land, early, after):
    def body(src_ref, land_ref, after_ref, send_sems, recv_sems, src_thru, land_thru, token):
        for to_peer, _, cp in _w_in_scatter_copies(src_ref, land_ref, send_sems, recv_sems, early):
            pl.when(to_peer)(cp.start)
        token[...] = jnp.zeros_like(token)

    hbm = lambda a: pltpu.HBM(a.shape, a.dtype)
    outs = pl.pallas_call(
        body, name=name,
        out_shape=(pltpu.SemaphoreType.DMA((N_DEV - 1,)), pltpu.SemaphoreType.DMA((N_DEV - 1,)),
                   hbm(src), hbm(land), jax.ShapeDtypeStruct((8, 128), F32)),
        in_specs=[HBM_SPEC, HBM_SPEC, pl.BlockSpec(memory_space=pl.ANY)],
        out_specs=(SEM_SPEC, SEM_SPEC, HBM_SPEC, HBM_SPEC, pl.BlockSpec(memory_space=pltpu.VMEM)),
        input_output_aliases={0: 2, 1: 3},
        compiler_params=pltpu.CompilerParams(has_side_effects=DATAFLOW),
    )(pltpu.with_memory_space_constraint(src, pltpu.HBM), pltpu.with_memory_space_constraint(land, pltpu.HBM), after)
    return dict(send=outs[0], recv=outs[1], src=outs[2], land=outs[3], token=outs[4], early=early, name=name)


def _w_in_scatter_wait(pending, land, after):
    def body(src_ref, land_ref, send_sems, recv_sems, after_ref, src_dead, land_out):
        for to_peer, to_me, cp in _w_in_scatter_copies(src_ref, land_ref, send_sems, recv_sems, pending["early"]):
            pl.when(to_peer)(cp.wait_send)
            pl.when(to_me)(cp.wait_recv)

    hbm = lambda a: pltpu.HBM(a.shape, a.dtype)
    outs = pl.pallas_call(
        body, name=pending["name"] + "_wait", out_shape=(hbm(pending["src"]), hbm(land)),
        in_specs=[HBM_SPEC, HBM_SPEC, SEM_SPEC, SEM_SPEC, pl.BlockSpec(memory_space=pl.ANY)],
        out_specs=(HBM_SPEC, HBM_SPEC), input_output_aliases={0: 0, 1: 1},
        compiler_params=pltpu.CompilerParams(has_side_effects=DATAFLOW),
    )(pending["src"], land, pending["send"], pending["recv"], after)
    return outs[1]


def _call_after(dep, body, args, *, in_specs, **kw):
    n_in = len(args)

    def wrapped(*refs):
        body(*refs[:n_in], *refs[n_in + 1:])

    dep_spec = pl.BlockSpec(dep.shape, lambda *_: (0,) * dep.ndim)
    return pl.pallas_call(wrapped, in_specs=list(in_specs) + [dep_spec], **kw)(*args, dep)


def _resident(shape):
    return pl.BlockSpec(shape, lambda *_: (0,) * len(shape), pipeline_mode=pl.Buffered(1))


def _proj(x2, w_in, dep):
    T, D = x2.shape
    tm = min(ROW_TILE, T)

    def body(x_ref, w_ref, o_ref, xt_ref):
        x = x_ref[...]
        xt_ref[...] = x.T.astype(BF16)
        xb = x.astype(BF16)
        for j in range(N_SEC):
            o_ref[j] = jnp.dot(xb, w_ref[:, j * D:(j + 1) * D], preferred_element_type=F32)

    return _call_after(
        dep, body, (x2, w_in), name="proj", grid=(T // tm,),
        in_specs=[pl.BlockSpec((tm, D), lambda i: (i, 0)), _resident((D, N_SEC * D))],
        out_specs=[pl.BlockSpec((N_SEC, tm, D), lambda i: (0, i, 0)), pl.BlockSpec((D, tm), lambda i: (0, i))],
        out_shape=[jax.ShapeDtypeStruct((N_SEC, T, D), F32), jax.ShapeDtypeStruct((D, T), BF16)],
        compiler_params=_params(("parallel",)))


def _chunk_cumsum(v, reverse=False):
    rows, lanes = v.shape
    x = v.reshape(rows // SUBLANES, SUBLANES, lanes)
    pos = lax.broadcasted_iota(jnp.int32, x.shape, 1)
    for sh in (1, 2, 4):
        if reverse:
            x = x + jnp.where(pos < SUBLANES - sh, pltpu.roll(x, SUBLANES - sh, 1), 0.0)
        else:
            x = x + jnp.where(pos >= sh, pltpu.roll(x, sh, 1), 0.0)
    x = x.reshape(rows // CHUNK, CHUNK // SUBLANES, SUBLANES, lanes)
    half = lax.broadcasted_iota(jnp.int32, x.shape, 1)
    if reverse:
        x = x + jnp.where(half == 0, x[:, 1:2, 0:1, :], 0.0)
    else:
        x = x + jnp.where(half == 1, x[:, 0:1, SUBLANES - 1:SUBLANES, :], 0.0)
    return x.reshape(rows, lanes)


def _hgrn_gates(q, f_pre, lb_logits):
    l0, l1 = lb_logits[0:1, :], lb_logits[1:2, :]
    mx = jnp.maximum(l0, l1)
    e0, e1 = jnp.exp(l0 - mx), jnp.exp(l1 - mx)
    lb = e0 / (e0 + e1)
    sq = _sigmoid(q)
    qf = q * sq * Q_SCALE
    sg = _sigmoid(f_pre)
    f = lb + (1.0 - lb) * sg
    k = 1.0 - f
    log_f = jnp.log(f)
    G = _chunk_cumsum(log_f)
    g_to_end = _chunk_cumsum(log_f, reverse=True) - log_f
    e_g = jnp.exp(G)
    e_ng = jnp.exp(-G)
    e_ge = jnp.exp(g_to_end)
    return dict(lb=lb, sq=sq, qf=qf, sg=sg, f=f, k=k, G=G, e_g=e_g, e_ng=e_ng, e_ge=e_ge,
                qd=qf * e_g, ki=k * e_ng, ke=k * e_ge, dec=e_g * e_ge)


def _intra_mask():
    r = lax.broadcasted_iota(jnp.int32, (GROUP, GROUP), 0)
    c = lax.broadcasted_iota(jnp.int32, (GROUP, GROUP), 1)
    return (r // CHUNK == c // CHUNK) & (c <= r)


def _chunk_outer(lhs_rows, rhs_b, out_scr, sb):
    lane = lax.broadcasted_iota(jnp.int32, (GROUP, GROUP), 1) // CHUNK
    for g in range(sb // GROUP):
        sl = slice(g * GROUP, (g + 1) * GROUP)
        lhs_t = lhs_rows[sl].T
        for cc in range(CH_PER_GROUP):
            masked = jnp.where(lane == cc, lhs_t, 0.0).astype(BF16)
            out_scr[g * CH_PER_GROUP + cc] = jnp.dot(masked, rhs_b[sl], preferred_element_type=F32)


def _hgrn_forward_blocks(cs, vs, st0s, sb, o_scr, kv_scr, st_out, dec_scr):
    nc = sb // CHUNK
    n_str = len(cs)
    mask = _intra_mask()
    bf = []
    for i, (c, v) in enumerate(zip(cs, vs)):
        qd_b, ki_b, ke_b, v_b = (c["qd"].astype(BF16), c["ki"].astype(BF16), c["ke"].astype(BF16),
                                 v.astype(BF16))
        bf.append((qd_b, ki_b, ke_b, v_b))
        for g in range(sb // GROUP):
            sl = slice(g * GROUP, (g + 1) * GROUP)
            sc = lax.dot_general(qd_b[sl], ki_b[sl], NT_DIMS, preferred_element_type=F32)
            a = jnp.where(mask, sc, 0.0).astype(BF16)
            o_scr[i, sl, :] = jnp.dot(a, v_b[sl], preferred_element_type=F32)
        _chunk_outer(v, ke_b, kv_scr.at[i], sb)
        dec_scr[i] = c["dec"]

    def rec(n, sts):
        row = pl.ds(pl.multiple_of(n * CHUNK, CHUNK), 1)
        out = []
        for i in range(n_str):
            st_out[i, 0, n] = sts[i].astype(BF16)
            out.append(sts[i] * dec_scr[i, row, :] + kv_scr[i, n])
        return tuple(out)

    ends = lax.fori_loop(0, nc, rec, tuple(st0s))

    for n in range(nc):
        rows = slice(n * CHUNK, (n + 1) * CHUNK)
        for i in range(n_str):
            o_scr[i, rows, :] += lax.dot_general(bf[i][0][rows], st_out[i, 0, n], NT_DIMS,
                                                 preferred_element_type=F32)
    return ends, bf


def _hgrn_fwd(proj5, lb_logits, gn):
    _, Bl, S, D = proj5.shape
    H = D // HEAD
    sb = min(SUB_BLOCK, S)
    nsb = S // sb
    nc = sb // CHUNK

    def body(p_ref, lbl_ref, gn_ref, ain_ref, aint_ref, o_ref, st_ref, carry, o_scr, kv_scr, dec_scr):
        @pl.when(pl.program_id(1) == 0)
        def _():
            carry[...] = jnp.zeros_like(carry)

        st0s = [carry[b] for b in range(Bl)]
        cs = [_hgrn_gates(p_ref[0, b], p_ref[1, b], lbl_ref[...]) for b in range(Bl)]
        ends, _ = _hgrn_forward_blocks(cs, [p_ref[2, b] for b in range(Bl)], st0s, sb,
                                       o_scr, kv_scr, st_ref, dec_scr)
        for b in range(Bl):
            carry[b] = ends[b]
            o = o_scr[b]
            o_ref[b] = o
            rinv = lax.rsqrt(jnp.mean(o * o, axis=-1, keepdims=True) + RMS_EPS)
            ain = o * rinv * gn_ref[...] * _sigmoid(p_ref[3, b])
            ain_ref[b] = ain.astype(BF16)
            aint_ref[b] = ain.T.astype(BF16)

    return pl.pallas_call(
        body, name="hgrn_fwd", grid=(H, nsb),
        in_specs=[pl.BlockSpec((4, Bl, sb, HEAD), lambda h, s: (0, 0, s, h)),
                  pl.BlockSpec((2, HEAD), lambda h, s: (0, h)),
                  pl.BlockSpec((1, HEAD), lambda h, s: (0, h))],
        out_specs=[pl.BlockSpec((Bl, sb, HEAD), lambda h, s: (0, s, h)),
                   pl.BlockSpec((Bl, HEAD, sb), lambda h, s: (0, h, s)),
                   pl.BlockSpec((Bl, sb, HEAD), lambda h, s: (0, s, h)),
                   pl.BlockSpec((Bl, 1, nc, HEAD, HEAD), lambda h, s: (0, h, s, 0, 0))],
        out_shape=[jax.ShapeDtypeStruct((Bl, S, D), BF16), jax.ShapeDtypeStruct((Bl, D, S), BF16),
                   jax.ShapeDtypeStruct((Bl, S, D), F32),
                   jax.ShapeDtypeStruct((Bl, H, S // CHUNK, HEAD, HEAD), BF16)],
        scratch_shapes=[pltpu.VMEM((Bl, HEAD, HEAD), F32), pltpu.VMEM((Bl, sb, HEAD), F32),
                        pltpu.VMEM((Bl, nc, HEAD, HEAD), F32), pltpu.VMEM((Bl, sb, HEAD), F32)],
        compiler_params=_params(("parallel", "arbitrary")),
    )(proj5, lb_logits, gn)


def _window_count(shape, g):
    pos = lax.broadcasted_iota(jnp.int32, shape, 0)
    return pos, jnp.minimum(pos + 1, jnp.left_shift(2, g)).astype(F32)


def _select_window(g, sums):
    return jnp.where(g == 0, sums[0], jnp.where(g == 1, sums[1], jnp.where(g == 2, sums[2], sums[3])))


def _pool_fwd(proj5, w_pool):
    _, Bl, S, D = proj5.shape
    pg = D // POOL_GROUPS

    def body(v_ref, w_ref, pooled_t_ref, bp_ref):
        g = pl.program_id(1)
        v = v_ref[0, 0]
        pos, cnt = _window_count(v.shape, g)
        cur, sums = v, []
        for sh in (1, 2, 4, 8):
            cur = cur + jnp.where(pos >= sh, pltpu.roll(cur, sh, 0), 0.0)
            sums.append(cur)
        pooled = _select_window(g, sums) / cnt - v
        pooled_t_ref[...] = pooled.T.astype(BF16)
        bp_ref[0] = jnp.dot(pooled.astype(BF16), w_ref[0], preferred_element_type=F32)

    return pl.pallas_call(
        body, name="pool_fwd", grid=(Bl, POOL_GROUPS),
        in_specs=[pl.BlockSpec((1, 1, S, pg), lambda b, g: (4, b, 0, g)),
                  pl.BlockSpec((1, pg, pg), lambda b, g: (g, 0, 0))],
        out_specs=[pl.BlockSpec((pg, S), lambda b, g: (g, b)),
                   pl.BlockSpec((1, S, pg), lambda b, g: (b, 0, g))],
        out_shape=[jax.ShapeDtypeStruct((D, Bl * S), BF16), jax.ShapeDtypeStruct((Bl, S, D), F32)],
        compiler_params=_params(("parallel", "parallel")),
    )(proj5, w_pool)


def _layer_norm_fwd(r):
    mu = jnp.mean(r, axis=-1, keepdims=True)
    d = r - mu
    rs = lax.rsqrt(jnp.mean(d * d, axis=-1, keepdims=True) + LN_EPS)
    return d * rs, rs


def _layer_norm_bwd(dy_g, xhat, rs):
    return rs * (dy_g - jnp.mean(dy_g, axis=-1, keepdims=True)
                 - xhat * jnp.mean(dy_g * xhat, axis=-1, keepdims=True))


def _mix_fwd(ain, proj, bp, x2, w_a, w_out, ps, g1, b1):
    T, D = x2.shape
    tm = min(ROW_TILE, T)

    def body(ain_ref, ga_ref, gb_ref, bp_ref, x_ref, wa_ref, wo_ref, ps_ref, g1_ref, b1_ref,
             a_ref, mgt_ref, xh_ref, rs_ref, x1b_ref, x1t_ref):
        a = jnp.dot(ain_ref[...], wa_ref[...], preferred_element_type=F32)
        a_ref[...] = a
        merged = _sigmoid(ga_ref[0]) * a + _sigmoid(gb_ref[0]) * (bp_ref[...] * ps_ref[...])
        mgt_ref[...] = merged.T.astype(BF16)
        r1 = ALPHA * x_ref[...] + jnp.dot(merged.astype(BF16), wo_ref[...], preferred_element_type=F32)
        xhat, rs = _layer_norm_fwd(r1)
        xh_ref[...] = xhat
        rs_ref[...] = rs
        x1 = xhat * g1_ref[...] + b1_ref[...]
        x1b_ref[...] = x1.astype(BF16)
        x1t_ref[...] = x1.T.astype(BF16)

    row = lambda i: (i, 0)
    col = lambda i: (0, i)
    full = lambda i: (0, 0)
    return pl.pallas_call(
        body, name="mix_fwd", grid=(T // tm,),
        in_specs=[pl.BlockSpec((tm, D), row),
                  pl.BlockSpec((1, tm, D), lambda i: (5, i, 0)),
                  pl.BlockSpec((1, tm, D), lambda i: (6, i, 0)),
                  pl.BlockSpec((tm, D), row), pl.BlockSpec((tm, D), row),
                  pl.BlockSpec((D, D), full), pl.BlockSpec((D, D), full),
                  pl.BlockSpec((1, D), full), pl.BlockSpec((1, D), full), pl.BlockSpec((1, D), full)],
        out_specs=[pl.BlockSpec((tm, D), row), pl.BlockSpec((D, tm), col), pl.BlockSpec((tm, D), row),
                   pl.BlockSpec((tm, 1), row), pl.BlockSpec((tm, D), row), pl.BlockSpec((D, tm), col)],
        out_shape=[jax.ShapeDtypeStruct((T, D), F32), jax.ShapeDtypeStruct((D, T), BF16),
                   jax.ShapeDtypeStruct((T, D), F32), jax.ShapeDtypeStruct((T, 1), F32),
                   jax.ShapeDtypeStruct((T, D), BF16), jax.ShapeDtypeStruct((D, T), BF16)],
        compiler_params=_params(("parallel",)),
    )(ain, proj, proj, bp, x2, w_a, w_out, ps, g1, b1)


def _mlp_fwd(x1b, w_up, w_down, xhat1, tgt, g1, b1, g2, b2):
    T, D = xhat1.shape
    FF = w_up.shape[1]
    tm = min(MLP_ROW_TILE, T)

    def body(x_ref, wu_ref, wd_ref, xh_ref, t_ref, g1_ref, b1_ref, g2_ref, b2_ref,
             hp_ref, h_ref, dr_ref, drb_ref, drt_ref, vec_ref):
        @pl.when(pl.program_id(0) == 0)
        def _():
            vec_ref[...] = jnp.zeros_like(vec_ref)

        xb = x_ref[...]
        x1 = xh_ref[...] * g1_ref[...] + b1_ref[...]
        r2 = ALPHA * x1
        for f in range(FF // D):
            cols = slice(f * D, (f + 1) * D)
            hp = jnp.dot(xb, wu_ref[:, cols], preferred_element_type=F32)
            hp_ref[:, cols] = hp
            h = jnp.square(jnp.maximum(hp, 0.0)).astype(BF16)
            h_ref[:, cols] = h
            r2 = r2 + jnp.dot(h, wd_ref[cols, :], preferred_element_type=F32)
        xhat2, rs2 = _layer_norm_fwd(r2)
        err = xhat2 * g2_ref[...] + b2_ref[...] - t_ref[...]
        dy = err / D
        vec_ref[5:6, :] += jnp.sum(dy * xhat2, axis=0, keepdims=True)
        vec_ref[6:7, :] += jnp.sum(dy, axis=0, keepdims=True)
        vec_ref[7:8, :] += jnp.sum(0.5 * err * err / D, axis=0, keepdims=True)
        dr = _layer_norm_bwd(dy * g2_ref[...], xhat2, rs2)
        dr_ref[...] = dr
        drb_ref[...] = dr.astype(BF16)
        drt_ref[...] = dr.T.astype(BF16)

    row = lambda i: (i, 0)
    full = lambda i: (0, 0)
    return pl.pallas_call(
        body, name="mlp_fwd", grid=(T // tm,),
        in_specs=[pl.BlockSpec((tm, D), row), _resident((D, FF)), _resident((FF, D)),
                  pl.BlockSpec((tm, D), row), pl.BlockSpec((tm, D), row),
                  pl.BlockSpec((1, D), full), pl.BlockSpec((1, D), full),
                  pl.BlockSpec((1, D), full), pl.BlockSpec((1, D), full)],
        out_specs=[pl.BlockSpec((tm, FF), row), pl.BlockSpec((tm, FF), row), pl.BlockSpec((tm, D), row),
                   pl.BlockSpec((tm, D), row), pl.BlockSpec((D, tm), lambda i: (0, i)),
                   pl.BlockSpec((8, D), full)],
        out_shape=[jax.ShapeDtypeStruct((T, FF), F32), jax.ShapeDtypeStruct((T, FF), BF16),
                   jax.ShapeDtypeStruct((T, D), F32), jax.ShapeDtypeStruct((T, D), BF16),
                   jax.ShapeDtypeStruct((D, T), BF16), jax.ShapeDtypeStruct((8, D), F32)],
        compiler_params=_params(("arbitrary",)),
    )(x1b, w_up, w_down, xhat1, tgt, g1, b1, g2, b2)


def _mlp_bwd(drb, dr, hp, w_up, w_down, xhat1, rs1, g1):
    T, D = dr.shape
    FF = hp.shape[1]
    tm = min(MLP_ROW_TILE, T)

    def body(drb_ref, dr_ref, hp_ref, wu_ref, wd_ref, xh_ref, rs_ref, g1_ref,
             dhp_ref, d1_ref, d1b_ref, vec_ref):
        @pl.when(pl.program_id(0) == 0)
        def _():
            vec_ref[...] = jnp.zeros_like(vec_ref)

        drb = drb_ref[...]
        dx1 = ALPHA * dr_ref[...]
        for f in range(FF // D):
            cols = slice(f * D, (f + 1) * D)
            dh = lax.dot_general(drb, wd_ref[cols, :], NT_DIMS, preferred_element_type=F32)
            dhp = (dh * (2.0 * jnp.maximum(hp_ref[:, cols], 0.0))).astype(BF16)
            dhp_ref[:, cols] = dhp
            dx1 = dx1 + lax.dot_general(dhp, wu_ref[:, cols], NT_DIMS, preferred_element_type=F32)
        xhat = xh_ref[...]
        vec_ref[3:4, :] += jnp.sum(dx1 * xhat, axis=0, keepdims=True)
        vec_ref[4:5, :] += jnp.sum(dx1, axis=0, keepdims=True)
        d1 = _layer_norm_bwd(dx1 * g1_ref[...], xhat, rs_ref[...])
        d1_ref[...] = d1
        d1b_ref[...] = d1.astype(BF16)

    row = lambda i: (i, 0)
    full = lambda i: (0, 0)
    return pl.pallas_call(
        body, name="mlp_bwd", grid=(T // tm,),
        in_specs=[pl.BlockSpec((tm, D), row), pl.BlockSpec((tm, D), row), pl.BlockSpec((tm, FF), row),
                  _resident((D, FF)), _resident((FF, D)),
                  pl.BlockSpec((tm, D), row), pl.BlockSpec((tm, 1), row), pl.BlockSpec((1, D), full)],
        out_specs=[pl.BlockSpec((tm, FF), row), pl.BlockSpec((tm, D), row), pl.BlockSpec((tm, D), row),
                   pl.BlockSpec((8, D), full)],
        out_shape=[jax.ShapeDtypeStruct((T, FF), BF16), jax.ShapeDtypeStruct((T, D), F32),
                   jax.ShapeDtypeStruct((T, D), BF16), jax.ShapeDtypeStruct((8, D), F32)],
        compiler_params=_params(("arbitrary",)),
    )(drb, dr, hp, w_up, w_down, xhat1, rs1, g1)


def _dw(name, a_t, b, n_j, a_spec, b_spec, o_shape, o_block, o_map, transpose_out=False, dep=None,
        into=(None, None), ob_shape=None, ob_map=None):
    def body(*refs):
        a_ref, b_ref, o_ref, ob_ref = refs[0], refs[1], refs[-2], refs[-1]
        b_val = b_ref[0] if len(b_ref.shape) == 3 else b_ref[...]
        if len(a_ref.shape) == 3:
            seq = a_ref.shape[2]
            p = sum(jnp.dot(a_ref[i], b_val[i * seq:(i + 1) * seq], preferred_element_type=F32)
                    for i in range(a_ref.shape[0]))
        else:
            p = jnp.dot(a_ref[...], b_val, preferred_element_type=F32)
        if transpose_out:
            p = p.T
        p = p.reshape(o_ref.shape)
        o_ref[...] = p
        ob_ref[...] = p.astype(BF16)

    kw = dict(name=name, grid=(n_j,), in_specs=[a_spec, b_spec],
              out_specs=[pl.BlockSpec(o_block, o_map), pl.BlockSpec(o_block, ob_map or o_map)],
              out_shape=[jax.ShapeDtypeStruct(o_shape, F32), jax.ShapeDtypeStruct(ob_shape or o_shape, BF16)],
              compiler_params=_params(("parallel",)))
    args = (a_t, b)
    aliases = {}
    for out_index, arr in enumerate(into):
        if arr is not None:
            aliases[len(args)] = out_index
            args = args + (arr,)
            kw["in_specs"] = kw["in_specs"] + [pl.BlockSpec(memory_space=pl.ANY)]
    if aliases:
        kw["input_output_aliases"] = aliases
    if dep is None:
        return pl.pallas_call(body, **kw)(*args)
    return _call_after(dep, body, args, **kw)


def _mix_bwd(d1b, proj, a, bp, w_a, w_out, w_pool, ps, dep):
    T, D = a.shape
    tm = min(ROW_TILE, T)
    pg = D // POOL_GROUPS

    def body(d1b_ref, ga_ref, gb_ref, a_ref, bp_ref, wa_ref, wo_ref, wp_ref, ps_ref,
             da_ref, dbp_ref, dain_ref, dpl_ref, dg_ref, vec_ref):
        @pl.when(pl.program_id(0) == 0)
        def _():
            vec_ref[...] = jnp.zeros_like(vec_ref)

        dm = lax.dot_general(d1b_ref[...], wo_ref[...], NT_DIMS, preferred_element_type=F32)
        sa, sg = _sigmoid(ga_ref[0]), _sigmoid(gb_ref[0])
        bp_v, ps_v = bp_ref[...], ps_ref[...]
        da = (dm * sa).astype(BF16)
        db = dm * sg
        dg_ref[0] = (dm * a_ref[...] * sa * (1.0 - sa)).astype(BF16)
        dg_ref[1] = (dm * (bp_v * ps_v) * sg * (1.0 - sg)).astype(BF16)
        vec_ref[2:3, :] += jnp.sum(db * bp_v, axis=0, keepdims=True)
        dbp = (db * ps_v).astype(BF16)
        da_ref[...] = da
        dbp_ref[...] = dbp
        dain_ref[...] = lax.dot_general(da, wa_ref[...], NT_DIMS, preferred_element_type=F32)
        for g in range(POOL_GROUPS):
            cols = slice(g * pg, (g + 1) * pg)
            dpl_ref[:, cols] = lax.dot_general(dbp[:, cols], wp_ref[g], NT_DIMS,
                                               preferred_element_type=F32)

    row = lambda i: (i, 0)
    full = lambda i: (0, 0)
    return _call_after(
        dep, body, (d1b, proj, proj, a, bp, w_a, w_out, w_pool, ps), name="mix_bwd", grid=(T // tm,),
        in_specs=[pl.BlockSpec((tm, D), row),
                  pl.BlockSpec((1, tm, D), lambda i: (5, i, 0)),
                  pl.BlockSpec((1, tm, D), lambda i: (6, i, 0)),
                  pl.BlockSpec((tm, D), row), pl.BlockSpec((tm, D), row),
                  pl.BlockSpec((D, D), full), pl.BlockSpec((D, D), full),
                  pl.BlockSpec((POOL_GROUPS, pg, pg), lambda i: (0, 0, 0)),
                  pl.BlockSpec((1, D), full)],
        out_specs=[pl.BlockSpec((tm, D), row), pl.BlockSpec((tm, D), row),
                   pl.BlockSpec((tm, D), row), pl.BlockSpec((tm, D), row),
                   pl.BlockSpec((2, tm, D), lambda i: (0, i, 0)),
                   pl.BlockSpec((8, D), full)],
        out_shape=[jax.ShapeDtypeStruct((T, D), BF16), jax.ShapeDtypeStruct((T, D), BF16),
                   jax.ShapeDtypeStruct((T, D), F32), jax.ShapeDtypeStruct((T, D), F32),
                   jax.ShapeDtypeStruct((2, T, D), BF16), jax.ShapeDtypeStruct((8, D), F32)],
        compiler_params=_params(("arbitrary",)))


def _pool_bwd(dpooled3, dep):
    Bl, S, D = dpooled3.shape
    pg = D // POOL_GROUPS

    def body(dp_ref, dv_ref):
        g = pl.program_id(1)
        dp = dp_ref[0]
        pos, cnt = _window_count(dp.shape, g)
        cur, sums = dp / cnt, []
        for sh in (1, 2, 4, 8):
            cur = cur + jnp.where(pos < S - sh, pltpu.roll(cur, S - sh, 0), 0.0)
            sums.append(cur)
        dv_ref[0] = (_select_window(g, sums) - dp).astype(BF16)

    spec = pl.BlockSpec((1, S, pg), lambda b, g: (b, 0, g))
    return _call_after(
        dep, body, (dpooled3,), name="pool_bwd", grid=(Bl, POOL_GROUPS), in_specs=[spec], out_specs=spec,
        out_shape=jax.ShapeDtypeStruct((Bl, S, D), BF16),
        compiler_params=_params(("parallel", "parallel")))


def _hgrn_bwd(proj5, lb_logits, gn, dain3, o3, st_all, dep):
    _, Bl, S, D = proj5.shape
    H = D // HEAD
    sb = min(SUB_BLOCK, S)
    nsb = S // sb
    nc = sb // CHUNK
    streams = range(Bl)

    def body(p_ref, lbl_ref, gn_ref, dain_ref, o_ref, st_ref, d_ref, vec_ref,
             dcarry, kv_scr, dst_scr, dec_scr, dvi_scr, dke_scr, dqi_scr):
        s = pl.program_id(1)

        @pl.when(s == 0)
        def _():
            dcarry[...] = jnp.zeros_like(dcarry)
            vec_ref[...] = jnp.zeros_like(vec_ref)

        qs, vs, ogs = [p_ref[0, b] for b in streams], [p_ref[2, b] for b in streams], [p_ref[3, b] for b in streams]
        cs = [_hgrn_gates(qs[b], p_ref[1, b], lbl_ref[...]) for b in streams]
        bf = [(cs[b]["qd"].astype(BF16), cs[b]["ki"].astype(BF16), cs[b]["ke"].astype(BF16),
               vs[b].astype(BF16)) for b in streams]
        mask = _intra_mask()
        gn_v = gn_ref[...]
        keep = []
        for b in streams:
            qd_b, ki_b, ke_b, v_b = bf[b]
            dec_scr[b] = cs[b]["dec"]
            o = o_ref[b]
            rinv = lax.rsqrt(jnp.mean(o * o, axis=-1, keepdims=True) + RMS_EPS)
            on = o * rinv
            so = _sigmoid(ogs[b])
            dain = dain_ref[b]
            vec_ref[1:2, :] += jnp.sum(dain * on * so, axis=0, keepdims=True)
            d_og = dain * on * gn_v * so * (1.0 - so)
            d_on = dain * gn_v * so
            do = rinv * (d_on - on * jnp.mean(d_on * on, axis=-1, keepdims=True))
            do_b = do.astype(BF16)
            dv_parts, dqd_parts, dki_parts = [], [], []
            for g in range(sb // GROUP):
                sl = slice(g * GROUP, (g + 1) * GROUP)
                sc = lax.dot_general(qd_b[sl], ki_b[sl], NT_DIMS, preferred_element_type=F32)
                a = jnp.where(mask, sc, 0.0).astype(BF16)
                da = lax.dot_general(do_b[sl], v_b[sl], NT_DIMS, preferred_element_type=F32)
                da = jnp.where(mask, da, 0.0).astype(BF16)
                dv_parts.append(lax.dot_general(a, do_b[sl], TN_DIMS, preferred_element_type=F32))
                dqd_parts.append(jnp.dot(da, ki_b[sl], preferred_element_type=F32))
                dki_parts.append(lax.dot_general(da, qd_b[sl], TN_DIMS, preferred_element_type=F32))
            keep.append(dict(d_og=d_og, do_b=do_b, dv_intra=jnp.concatenate(dv_parts, axis=0),
                             dqd_intra=jnp.concatenate(dqd_parts, axis=0),
                             dki=jnp.concatenate(dki_parts, axis=0)))
            _chunk_outer(do, qd_b, kv_scr.at[b], sb)

        def rrec(i, dsts):
            n = nc - 1 - i
            row = pl.ds(pl.multiple_of(n * CHUNK, CHUNK), 1)
            out = []
            for b in streams:
                dst_scr[b, n] = dsts[b]
                out.append(dsts[b] * dec_scr[b, row, :] + kv_scr[b, n])
            return tuple(out)

        ends = lax.fori_loop(0, nc, rrec, tuple(dcarry[b] for b in streams))
        for b in streams:
            dcarry[b] = ends[b]
        for n in range(nc):
            rows = slice(n * CHUNK, (n + 1) * CHUNK)
            for b in streams:
                qd_b, ki_b, ke_b, v_b = bf[b]
                dst_b = dst_scr[b, n].astype(BF16)
                dvi_scr[b, rows, :] = lax.dot_general(ke_b[rows], dst_b, NT_DIMS, preferred_element_type=F32)
                dke_scr[b, rows, :] = jnp.dot(v_b[rows], dst_b, preferred_element_type=F32)
                dqi_scr[b, rows, :] = jnp.dot(keep[b]["do_b"][rows], st_ref[b, 0, n],
                                              preferred_element_type=F32)
        for b in streams:
            c, k = cs[b], keep[b]
            ddec = jnp.sum(dst_scr[b] * st_ref[b, 0].astype(F32), axis=1)
            dgl = jnp.broadcast_to(ddec[:, None, :], (nc, CHUNK, HEAD)).reshape(sb, HEAD) * c["dec"]
            dqd = k["dqd_intra"] + dqi_scr[b]
            dke = dke_scr[b]
            dki = k["dki"]
            t_ke = dke * c["ke"]
            dG = dqd * c["qd"] - dki * c["ki"] - t_ke
            dgl = dgl + _chunk_cumsum(t_ke) + _chunk_cumsum(t_ke, reverse=True) - t_ke
            dlogf = _chunk_cumsum(dG, reverse=True) + dgl
            dk = dki * c["e_ng"] + dke * c["e_ge"]
            df = dlogf / c["f"] - dk
            sg, sq, lb, q = c["sg"], c["sq"], c["lb"], qs[b]
            vec_ref[0:1, :] += jnp.sum(df * (1.0 - sg), axis=0, keepdims=True)
            d_ref[0, b] = (dqd * c["e_g"] * Q_SCALE * (sq + q * sq * (1.0 - sq))).astype(BF16)
            d_ref[1, b] = (df * (1.0 - lb) * sg * (1.0 - sg)).astype(BF16)
            d_ref[2, b] = (k["dv_intra"] + dvi_scr[b]).astype(BF16)
            d_ref[3, b] = k["d_og"].astype(BF16)

    rev = lambda s: nsb - 1 - s
    big = pltpu.VMEM((Bl, nc, HEAD, HEAD), F32)
    rows_f32 = pltpu.VMEM((Bl, sb, HEAD), F32)
    return _call_after(
        dep, body, (proj5, lb_logits, gn, dain3, o3, st_all), name="hgrn_bwd", grid=(H, nsb),
        in_specs=[pl.BlockSpec((4, Bl, sb, HEAD), lambda h, s: (0, 0, rev(s), h)),
                  pl.BlockSpec((2, HEAD), lambda h, s: (0, h)),
                  pl.BlockSpec((1, HEAD), lambda h, s: (0, h)),
                  pl.BlockSpec((Bl, sb, HEAD), lambda h, s: (0, rev(s), h)),
                  pl.BlockSpec((Bl, sb, HEAD), lambda h, s: (0, rev(s), h)),
                  pl.BlockSpec((Bl, 1, nc, HEAD, HEAD), lambda h, s: (0, h, rev(s), 0, 0))],
        out_specs=[pl.BlockSpec((4, Bl, sb, HEAD), lambda h, s: (0, 0, rev(s), h)),
                   pl.BlockSpec((8, HEAD), lambda h, s: (0, h))],
        out_shape=[jax.ShapeDtypeStruct((4, Bl, S, D), BF16), jax.ShapeDtypeStruct((8, D), F32)],
        scratch_shapes=[pltpu.VMEM((Bl, HEAD, HEAD), F32), big, big, rows_f32, rows_f32, rows_f32, rows_f32],
        compiler_params=_params(("parallel", "arbitrary")))


def _dx(d1, dh4, dpv, dg2, w_in, dep):
    T, D = d1.shape
    tm = min(ROW_TILE, T)

    def body(d1_ref, dh_ref, dp_ref, dg_ref, w_ref, o_ref):
        blocks = [dh_ref[0], dh_ref[1], dh_ref[2], dh_ref[3], dp_ref[...], dg_ref[0], dg_ref[1]]
        acc = ALPHA * d1_ref[...]
        for j, blk in enumerate(blocks):
            acc = acc + lax.dot_general(blk, w_ref[:, j * D:(j + 1) * D], NT_DIMS, preferred_element_type=F32)
        o_ref[...] = acc

    row = lambda i: (i, 0)
    return _call_after(
        dep, body, (d1, dh4, dpv, dg2, w_in), name="dx", grid=(T // tm,),
        in_specs=[pl.BlockSpec((tm, D), row), pl.BlockSpec((4, tm, D), lambda i: (0, i, 0)),
                  pl.BlockSpec((tm, D), row), pl.BlockSpec((2, tm, D), lambda i: (0, i, 0)),
                  _resident((D, N_SEC * D))],
        out_specs=pl.BlockSpec((tm, D), row),
        out_shape=jax.ShapeDtypeStruct((T, D), F32),
        compiler_params=_params(("parallel",)))


def _dw_in_part(name, x_t, b, sections, first_sec, into, dep, ob_shape, ob_first):
    D, T = x_t.shape
    per = D // DW_COLS
    b_spec = (pl.BlockSpec((1, T, DW_COLS), lambda j: (j // per, 0, j % per)) if b.ndim == 3
              else pl.BlockSpec((T, DW_COLS), lambda j: (0, j)))
    return _dw(name, x_t, b, sections * per, _resident((D, T)), b_spec, (D, N_SEC * D), (D, DW_COLS),
               lambda j: (0, first_sec * per + j), dep=dep, into=into, ob_shape=ob_shape,
               ob_map=lambda j: (0, ob_first * per + j))


def _dw_in_rec(x_t, dh4, dep):
    D = x_t.shape[0]
    return _dw_in_part("dw_in_rec", x_t, dh4, EARLY_SEC, 0, (None, None), dep, (D, EARLY_SEC * D), 0)


def _dw_in_rest(x_t, dpv, dg2, f32_rec, dep):
    D = x_t.shape[0]
    rest_shape = (D, (N_SEC - EARLY_SEC) * D)
    f32, bf = _dw_in_part("dw_in_gates", x_t, dg2, 2, 5, (f32_rec, None), dep, rest_shape, 1)
    return _dw_in_part("dw_in_pool", x_t, dpv, 1, 4, (f32, bf), None, rest_shape, 0)


def _adam_shard(name, me_arr, grad, land, layout, w, m, v):
    shape = layout.shape
    n_split = 4
    blk = (shape[0] // n_split,) + shape[1:]
    zeros = (0,) * (len(shape) - 1)

    def body(me_ref, g_ref, r_ref, w_ref, m_ref, v_ref, g_out, d_out, m_out, v_out):
        g = g_ref[...]
        for k in range(N_DEV - 1):
            g = g + r_ref[k].astype(F32)
        d, m2, v2 = _adamw(w_ref[...], g, m_ref[...], v_ref[...])
        g_out[...] = g
        d_out[...] = d
        m_out[...] = m2
        v_out[...] = v2

    def own(i, me_ref):
        bi = layout.block_index(me_ref[0])
        return (bi[0] * n_split + i,) + tuple(bi[1:]) if layout.kind == "row" else (i,) + tuple(bi[1:])

    plain = pl.BlockSpec(blk, lambda i, me_ref: (i,) + zeros)
    grid_spec = pltpu.PrefetchScalarGridSpec(
        num_scalar_prefetch=1, grid=(n_split,),
        in_specs=[pl.BlockSpec(blk, own),
                  pl.BlockSpec((N_DEV - 1,) + blk, lambda i, me_ref: (0, i) + zeros),
                  plain, plain, plain],
        out_specs=[plain] * 4)
    return pl.pallas_call(
        body, name=name, grid_spec=grid_spec,
        out_shape=[jax.ShapeDtypeStruct(shape, F32)] * 4,
        compiler_params=_params(("parallel",)),
    )(me_arr, grad, land, w, m, v)


def _vec_allreduce(vec):
    D = vec.shape[1]

    def body(vec_ref, tot_ref, gat, send_sems, recv_sems):
        x, y, c = _me()
        me = 4 * x + 2 * y + c
        gat[me] = vec_ref[...]
        copies = []
        for k in range(1, N_DEV):
            cp = pltpu.make_async_remote_copy(
                src_ref=vec_ref, dst_ref=gat.at[me], send_sem=send_sems.at[k - 1],
                recv_sem=recv_sems.at[k - 1], device_id=_peer(k, x, y, c), device_id_type=MESH)
            cp.start()
            copies.append(cp)
        for cp in copies:
            cp.wait()
        tot = gat[0]
        for d in range(1, N_DEV):
            tot = tot + gat[d]
        tot_ref[...] = tot

    vm = pl.BlockSpec(memory_space=pltpu.VMEM)
    return pl.pallas_call(
        body, name="vec_allreduce", out_shape=jax.ShapeDtypeStruct(vec.shape, F32),
        in_specs=[vm], out_specs=vm,
        scratch_shapes=[pltpu.VMEM((N_DEV, 8, D), F32), pltpu.SemaphoreType.DMA((N_DEV - 1,)),
                        pltpu.SemaphoreType.DMA((N_DEV - 1,))],
    )(vec)


def _vec_adam(tot, small_w, small_m, small_v):
    n = len(small_w)

    def body(*refs):
        tot = refs[0][...]
        ws, ms, vs = refs[1:1 + n], refs[1 + n:1 + 2 * n], refs[1 + 2 * n:1 + 3 * n]
        outs = refs[1 + 3 * n:]
        loss_ref, g_out, d_out = outs[0], outs[1:1 + n], outs[1 + n:1 + 2 * n]
        m_out, v_out = outs[1 + 2 * n:1 + 3 * n], outs[1 + 3 * n:1 + 4 * n]
        loss_ref[...] = jnp.broadcast_to(jnp.sum(tot[7:8, :], axis=1, keepdims=True), loss_ref.shape)
        lbl = ws[0][...]
        mx = jnp.maximum(lbl[0:1, :], lbl[1:2, :])
        e0, e1 = jnp.exp(lbl[0:1, :] - mx), jnp.exp(lbl[1:2, :] - mx)
        p0 = e0 / (e0 + e1)
        dl0 = tot[0:1, :] * p0 * (1.0 - p0)
        grads = [jnp.concatenate([dl0, -dl0], axis=0)] + [tot[r:r + 1, :] for r in range(1, n)]
        for i in range(n):
            d, m2, v2 = _adamw(ws[i][...], grads[i], ms[i][...], vs[i][...])
            g_out[i][...] = grads[i]
            d_out[i][...] = d
            m_out[i][...] = m2
            v_out[i][...] = v2

    vm = pl.BlockSpec(memory_space=pltpu.VMEM)
    shapes = [jax.ShapeDtypeStruct(w.shape, F32) for w in small_w]
    return pl.pallas_call(
        body, name="vec_adam",
        out_shape=[jax.ShapeDtypeStruct((1, 128), F32)] + shapes * 4,
        in_specs=[vm] * (1 + 3 * n), out_specs=[vm] * (1 + 4 * n),
    )(tot, *small_w, *small_m, *small_v)


def kernel(x, w_in, lb_logits, hgrn_norm_g, w_a, w_pool, pool_scale, w_out, ln1_g, ln1_b, w_up, w_down, ln2_g, ln2_b, loss_target, m_w_in, m_lb_logits, m_hgrn_norm_g, m_w_a, m_w_pool, m_pool_scale, m_w_out, m_ln1_g, m_ln1_b, m_w_up, m_w_down, m_ln2_g, m_ln2_b, v_w_in, v_lb_logits, v_hgrn_norm_g, v_w_a, v_w_pool, v_pool_scale, v_w_out, v_ln1_g, v_ln1_b, v_w_up, v_w_down, v_ln2_g, v_ln2_b):
    Bl, S, D = x.shape
    T = Bl * S
    pg = D // POOL_GROUPS
    x2 = x.reshape(T, D)
    tgt = loss_target.reshape(T, D)
    me = 4 * lax.axis_index("x") + 2 * lax.axis_index("y") + lax.axis_index("c")
    me_arr = jnp.reshape(me, (1,)).astype(jnp.int32)

    names = ["w_in", "w_a", "w_pool", "w_out", "w_up", "w_down"]
    big_w = dict(zip(names, [w_in[0], w_a[0], w_pool[0], w_out[0], w_up[0], w_down[0]]))
    big_m = dict(zip(names, [m_w_in[0], m_w_a[0], m_w_pool[0], m_w_out[0], m_w_up[0], m_w_down[0]]))
    big_v = dict(zip(names, [v_w_in[0], v_w_a[0], v_w_pool[0], v_w_out[0], v_w_up[0], v_w_down[0]]))
    kinds = dict(w_in="col", w_a="row", w_pool="pool", w_out="row", w_up="col", w_down="row")
    lay = {nm: _Sharded(kinds[nm], big_w[nm].shape) for nm in names}
    wb = {nm: big_w[nm].astype(BF16) for nm in names}

    (w_in_f,) = _all_gather("ag_w_in", [wb["w_in"]], [lay["w_in"]])
    def gather_start(name, nms, after):
        return _exchange_start(name, [wb[nm] for nm in nms], [lax.empty(lay[nm].full_shape, BF16) for nm in nms],
                               src_at=lambda w, ref, peer: ref,
                               dst_at=lambda w, ref, mine, k: lay[nms[w]].at(ref, mine), after=after, own=True)

    ag_mix = gather_start("ag_mix", ["w_a", "w_pool", "w_out"], w_in_f)
    ag_mlp = gather_start("ag_mlp", ["w_up", "w_down"], ag_mix["token"])

    proj, x_t = _proj(x2, w_in_f, ag_mlp["token"])
    proj5 = proj.reshape(N_SEC, Bl, S, D)
    ain3, ain_t, o3, st_all = _hgrn_fwd(proj5, lb_logits, hgrn_norm_g)
    w_a_f, w_pool_f, w_out_f = _exchange_wait(ag_mix, ain3)
    pooled_t, bp3 = _pool_fwd(proj5, w_pool_f)
    ain, bp = ain3.reshape(T, D), bp3.reshape(T, D)
    a, merged_t, xhat1, rs1, x1b, x1_t = _mix_fwd(ain, proj, bp, x2, w_a_f, w_out_f, pool_scale, ln1_g, ln1_b)
    w_up_f, w_down_f = _exchange_wait(ag_mlp, x1b)
    hp, h, dr2, dr2b, dr2_t, vec_mlp = _mlp_fwd(x1b, w_up_f, w_down_f, xhat1, tgt, ln1_g, ln1_b, ln2_g, ln2_b)

    def scatter_start(name, nms, grads_b, after):
        lands = [lax.empty((N_DEV - 1,) + lay[nm].shape, BF16) for nm in nms]
        return _exchange_start(name, grads_b, lands,
                               src_at=lambda w, ref, peer: lay[nms[w]].at(ref, peer),
                               dst_at=lambda w, ref, mine, k: ref.at[k - 1], after=after)

    dhp, dr1, dr1b, vec_ln1 = _mlp_bwd(dr2b, dr2, hp, w_up_f, w_down_f, xhat1, rs1, ln1_g)
    FF = 4 * D
    whole_t = _resident((D, T))
    cols_b = pl.BlockSpec((T, DW_COLS), lambda j: (0, j))
    cols_o = ((D, DW_COLS), lambda j: (0, j))
    gw, gwb = {}, {}
    gw["w_down"], gwb["w_down"] = _dw(
        "dw_down", dr2_t, h, FF // DW_COLS, whole_t, cols_b, (FF, D), (DW_COLS, D), lambda j: (j, 0),
        transpose_out=True)
    rs_down = scatter_start("rs_w_down", ["w_down"], [gwb["w_down"]], gw["w_down"])
    gw["w_up"], gwb["w_up"] = _dw("dw_up", x1_t, dhp, FF // DW_COLS, whole_t, cols_b, (D, FF), *cols_o,
                                  dep=rs_down["token"])
    rs_up = scatter_start("rs_w_up", ["w_up"], [gwb["w_up"]], gw["w_up"])
    da_b, dbp_b, dain, dpooled, dg2, vec_mix = _mix_bwd(dr1b, proj, a, bp, w_a_f, w_out_f, w_pool_f, pool_scale,
                                                        rs_up["token"])
    gw["w_out"], gwb["w_out"] = _dw("dw_out", merged_t, dr1b, D // DW_COLS, whole_t, cols_b, (D, D), *cols_o)
    gw["w_a"], gwb["w_a"] = _dw("dw_a", ain_t, da_b, D // DW_COLS, _resident((Bl, D, S)), cols_b, (D, D), *cols_o)
    gw["w_pool"], gwb["w_pool"] = _dw(
        "dw_pool", pooled_t, dbp_b, POOL_GROUPS, pl.BlockSpec((pg, T), lambda j: (j, 0)),
        pl.BlockSpec((T, pg), lambda j: (0, j)), (POOL_GROUPS, pg, pg), (1, pg, pg), lambda j: (j, 0, 0))
    mid = ["w_out", "w_a", "w_pool"]
    rs_mid = scatter_start("rs_w_mid", mid, [gwb[nm] for nm in mid], gw["w_pool"])
    dh4, vec_hgrn = _hgrn_bwd(proj5, lb_logits, hgrn_norm_g, dain.reshape(Bl, S, D), o3, st_all, rs_mid["token"])
    dh4 = dh4.reshape(4, T, D)
    vec_tot = _vec_allreduce(vec_mlp + vec_ln1 + vec_mix + vec_hgrn)
    gw_in_rec, gwb_in_rec = _dw_in_rec(x_t, dh4, vec_tot)
    land_in = lax.empty((N_DEV - 1,) + lay["w_in"].shape, BF16)
    rs_in_rec = _w_in_scatter_start("rs_w_in_rec", gwb_in_rec, land_in, False, gw_in_rec)
    dpv = _pool_bwd(dpooled.reshape(Bl, S, D), rs_in_rec["token"]).reshape(T, D)
    gw["w_in"], gwb_in_rest = _dw_in_rest(x_t, dpv, dg2, gw_in_rec, rs_in_rec["token"])
    rs_in_rest = _w_in_scatter_start("rs_w_in_rest", gwb_in_rest, rs_in_rec["land"], True, gw["w_in"])
    grad_x2 = _dx(dr1, dh4, dpv, dg2, w_in_f, rs_in_rest["token"])
    grad_x = grad_x2.reshape(Bl, S, D)

    small_names =["lb_logits", "hgrn_norm_g", "pool_scale", "ln1_g", "ln1_b", "ln2_g", "ln2_b"]
    small_w = [lb_logits, hgrn_norm_g, pool_scale, ln1_g, ln1_b, ln2_g, ln2_b]
    small_m = [m_lb_logits, m_hgrn_norm_g, m_pool_scale, m_ln1_g, m_ln1_b, m_ln2_g, m_ln2_b]
    small_v = [v_lb_logits, v_hgrn_norm_g, v_pool_scale, v_ln1_g, v_ln1_b, v_ln2_g, v_ln2_b]
    res = _vec_adam(vec_tot, small_w, small_m, small_v)
    loss = res[0][0, 0]
    n = len(small_w)
    small = {nm: (res[1 + i], res[1 + n + i], res[1 + 2 * n + i], res[1 + 3 * n + i])
             for i, nm in enumerate(small_names)}

    big, last = {}, grad_x2

    def adam(nm, land):
        outs = _adam_shard("adam_" + nm, me_arr, gw[nm], land, lay[nm], big_w[nm], big_m[nm], big_v[nm])
        big[nm] = tuple(t[None] for t in outs)
        return outs[0]

    for pend, nms in ((rs_down, ["w_down"]), (rs_up, ["w_up"]), (rs_mid, mid)):
        for nm, land in zip(nms, _exchange_wait(pend, last)):
            last = adam(nm, land)
    land_in = _w_in_scatter_wait(rs_in_rec, rs_in_rest["land"], last)
    adam("w_in", _w_in_scatter_wait(rs_in_rest, land_in, res[0]))

    order = ["w_in", "lb_logits", "hgrn_norm_g", "w_a", "w_pool", "pool_scale", "w_out", "ln1_g", "ln1_b",
             "w_up", "w_down", "ln2_g", "ln2_b"]
    allp = {**big, **small}
    out = [loss, grad_x]
    for part in range(4):
        out += [allp[nm][part] for nm in order]
    return tuple(out)
```

```python
import jax
import jax.numpy as jnp
from jax import lax
from jax.experimental import pallas as pl
from jax.experimental.pallas import tpu as pltpu

F32 = jnp.float32
BF16 = jnp.bfloat16
MESH = pl.DeviceIdType.MESH

N_DEV = 8
HEAD = 128
CHUNK = 16
SUBLANES = 8
GROUP = 128
SUB_BLOCK = 1024
ROW_TILE = 512
MLP_ROW_TILE = 256
DW_COLS = 512
EARLY_SEC = 4
CH_PER_GROUP = GROUP // CHUNK
N_SEC = 7
POOL_GROUPS = 4
ALPHA = (2.0 * 1) ** 0.25
LN_EPS = 1e-5
RMS_EPS = 1e-6
Q_SCALE = HEAD ** -0.5
ADAM_LR = 0.001
ADAM_B1 = 0.9
ADAM_B2 = 0.999
ADAM_EPS = 1e-08
ADAM_WD = 0.01
ADAM_STEP = 10
VMEM_LIMIT = 60 << 20

NT_DIMS = (((1,), (1,)), ((), ()))
TN_DIMS = (((0,), (0,)), ((), ()))


def _params(sem=None):
    kw = dict(vmem_limit_bytes=VMEM_LIMIT)
    if sem is not None:
        kw["dimension_semantics"] = sem
    return pltpu.CompilerParams(**kw)


def _me():
    return lax.axis_index("x"), lax.axis_index("y"), lax.axis_index("c")


def _sigmoid(v):
    return jax.nn.sigmoid(v)


def _adamw(w, g, m, v):
    m = ADAM_B1 * m + (1.0 - ADAM_B1) * g
    v = ADAM_B2 * v + (1.0 - ADAM_B2) * jnp.square(g)
    m_hat = m / (1.0 - ADAM_B1 ** ADAM_STEP)
    v_hat = v / (1.0 - ADAM_B2 ** ADAM_STEP)
    delta = -ADAM_LR * (m_hat / (jnp.sqrt(v_hat) + ADAM_EPS) + ADAM_WD * w)
    return delta, m, v


class _Sharded:
    def __init__(self, kind, shard_shape):
        self.kind, self.shape = kind, tuple(shard_shape)

    @property
    def full_shape(self):
        r = self.shape
        if self.kind == "row":
            return (N_DEV * r[0],) + r[1:]
        return (r[0], N_DEV * r[1]) + r[2:]

    def at(self, ref, d):
        if self.kind == "col":
            n = self.shape[1]
            return ref.at[:, pl.ds(pl.multiple_of(d * n, 128), n)]
        if self.kind == "row":
            n = self.shape[0]
            return ref.at[pl.ds(pl.multiple_of(d * n, 16), n), :]
        n = self.shape[1]
        return ref.at[:, pl.ds(pl.multiple_of(d * n, 16), n), :]

    def block_index(self, d):
        return {"col": (0, d), "row": (d, 0), "pool": (0, d, 0)}[self.kind]


def _peer(k, x, y, c):
    return (1 - x if k & 4 else x, 1 - y if k & 2 else y, 1 - c if k & 1 else c)


def _all_gather(name, shards, layouts):
    nw = len(shards)

    def body(*refs):
        ins, outs = refs[:nw], refs[nw:2 * nw]
        send_sems, recv_sems, local_sems = refs[2 * nw:]
        x, y, c = _me()
        me = (x, y, c)
        sibling = (x, y, 1 - c)
        chips = [(1 - x, y), (x, 1 - y), (1 - x, 1 - y)]

        def copy(w, k, block, to, src=None):
            px, py, pc = block
            dst = layouts[w].at(outs[w], 4 * px + 2 * py + pc)
            return pltpu.make_async_remote_copy(
                src_ref=dst if src is None else src, dst_ref=dst,
                send_sem=send_sems.at[w, k], recv_sem=recv_sems.at[w, k],
                device_id=to, device_id_type=MESH)

        def place(w):
            mine = pltpu.make_async_copy(ins[w], layouts[w].at(outs[w], 4 * x + 2 * y + c), local_sems.at[w])
            mine.start()
            return mine

        first = []
        for w in range(nw):
            first.append(copy(w, 0, me, sibling, src=ins[w]))
            first += [copy(w, 1 + j, me, (*chip, c), src=ins[w]) for j, chip in enumerate(chips)]
        for cp in first:
            cp.start()
        local = [place(w) for w in range(nw)]
        passed = []
        for w in range(nw):
            for j, chip in enumerate(chips):
                copy(w, 1 + j, (*chip, c), me).wait_recv()
                fwd = copy(w, 4 + j, (*chip, c), sibling)
                fwd.start()
                passed.append(fwd)
        for w in range(nw):
            copy(w, 0, sibling, me).wait_recv()
            for j, chip in enumerate(chips):
                copy(w, 4 + j, (*chip, 1 - c), me).wait_recv()
        for cp in first + passed:
            cp.wait_send()
        for cp in local:
            cp.wait()

    any_spec = pl.BlockSpec(memory_space=pl.ANY)
    return pl.pallas_call(
        body, name=name,
        out_shape=[jax.ShapeDtypeStruct(l.full_shape, s.dtype) for s, l in zip(shards, layouts)],
        in_specs=[any_spec] * nw, out_specs=[any_spec] * nw,
        scratch_shapes=[pltpu.SemaphoreType.DMA((nw, 7)), pltpu.SemaphoreType.DMA((nw, 7)),
                        pltpu.SemaphoreType.DMA((nw,))],
    )(*shards)


HBM_SPEC = pl.BlockSpec(memory_space=pltpu.HBM)
SEM_SPEC = pl.BlockSpec(memory_space=pltpu.SEMAPHORE)
DATAFLOW = pltpu.SideEffectType.DATAFLOW_SIDE_EFFECTING


def _exchange_copies(srcs, lands, send_sems, recv_sems, src_at, dst_at):
    x, y, c = _me()
    me = 4 * x + 2 * y + c
    copies = []
    for w in range(len(srcs)):
        for k in range(1, N_DEV):
            px, py, pc = _peer(k, x, y, c)
            copies.append(pltpu.make_async_remote_copy(
                src_ref=src_at(w, srcs[w], 4 * px + 2 * py + pc), dst_ref=dst_at(w, lands[w], me, k),
                send_sem=send_sems.at[w * (N_DEV - 1) + k - 1], recv_sem=recv_sems.at[w * (N_DEV - 1) + k - 1],
                device_id=(px, py, pc), device_id_type=MESH))
    return copies


def _own_copies(srcs, lands, own_sems, src_at, dst_at):
    x, y, c = _me()
    me = 4 * x + 2 * y + c
    return [pltpu.make_async_copy(src_at(w, srcs[w], me), dst_at(w, lands[w], me, 0), own_sems.at[w])
            for w in range(len(srcs))]


def _exchange_start(name, srcs, lands, src_at, dst_at, after, own=False):
    nw = len(srcs)

    def body(*refs):
        src_refs, land_refs = refs[:nw], refs[nw:2 * nw]
        send_sems, recv_sems, own_sems = refs[2 * nw + 1], refs[2 * nw + 2], refs[2 * nw + 3]
        token = refs[-1]
        for cp in _exchange_copies(src_refs, land_refs, send_sems, recv_sems, src_at, dst_at):
            cp.start()
        if own:
            for cp in _own_copies(src_refs, land_refs, own_sems, src_at, dst_at):
                cp.start()
        token[...] = jnp.zeros_like(token)

    hbm = lambda a: pltpu.HBM(a.shape, a.dtype)
    outs = pl.pallas_call(
        body, name=name,
        out_shape=(pltpu.SemaphoreType.DMA((nw * (N_DEV - 1),)), pltpu.SemaphoreType.DMA((nw * (N_DEV - 1),)),
                   pltpu.SemaphoreType.DMA((nw,)), *[hbm(a) for a in srcs], *[hbm(a) for a in lands],
                   jax.ShapeDtypeStruct((8, 128), F32)),
        in_specs=[HBM_SPEC] * (2 * nw) + [pl.BlockSpec(memory_space=pl.ANY)],
        out_specs=(SEM_SPEC, SEM_SPEC, SEM_SPEC, *[HBM_SPEC] * (2 * nw), pl.BlockSpec(memory_space=pltpu.VMEM)),
        input_output_aliases={i: 3 + i for i in range(2 * nw)},
        compiler_params=pltpu.CompilerParams(has_side_effects=DATAFLOW),
    )(*[pltpu.with_memory_space_constraint(a, pltpu.HBM) for a in list(srcs) + list(lands)], after)
    return dict(send=outs[0], recv=outs[1], own_sems=outs[2], srcs=outs[3:3 + nw], lands=outs[3 + nw:3 + 2 * nw],
                token=outs[-1], src_at=src_at, dst_at=dst_at, name=name, own=own)


def _exchange_wait(pending, after):
    nw = len(pending["srcs"])

    def body(*refs):
        src_refs, land_refs = refs[:nw], refs[nw:2 * nw]
        send_sems, recv_sems, own_sems = refs[2 * nw], refs[2 * nw + 1], refs[2 * nw + 2]
        for cp in _exchange_copies(src_refs, land_refs, send_sems, recv_sems,
                                   pending["src_at"], pending["dst_at"]):
            cp.wait_send()
            cp.wait_recv()
        if pending["own"]:
            for cp in _own_copies(src_refs, land_refs, own_sems, pending["src_at"], pending["dst_at"]):
                cp.wait()

    hbm = lambda a: pltpu.HBM(a.shape, a.dtype)
    outs = pl.pallas_call(
        body, name=pending["name"] + "_wait",
        out_shape=(*[hbm(a) for a in pending["srcs"]], *[hbm(a) for a in pending["lands"]]),
        in_specs=[HBM_SPEC] * (2 * nw) + [SEM_SPEC, SEM_SPEC, SEM_SPEC, pl.BlockSpec(memory_space=pl.ANY)],
        out_specs=tuple([HBM_SPEC] * (2 * nw)),
        input_output_aliases={i: i for i in range(2 * nw)},
        compiler_params=pltpu.CompilerParams(has_side_effects=DATAFLOW),
    )(*pending["srcs"], *pending["lands"], pending["send"], pending["recv"], pending["own_sems"], after)
    return outs[nw:]


def _w_in_scatter_copies(src, land, send_sems, recv_sems, early):
    rows, cols = land.shape[1], land.shape[2]
    bound = EARLY_SEC * rows
    cut_dev = bound // cols
    cut = bound - cut_dev * cols
    x, y, c = _me()
    me = 4 * x + 2 * y + c

    def pieces(t):
        if early:
            return [(t > cut_dev, t * cols - bound, cols, 0), (t == cut_dev, 0, cols - cut, cut)]
        return [(t < cut_dev, t * cols, cols, 0), (t == cut_dev, cut_dev * cols, cut, 0)]

    out = []
    for k in range(1, N_DEV):
        px, py, pc = _peer(k, x, y, c)
        for (to_peer, s0, width, d0), (to_me, _, _, _) in zip(pieces(4 * px + 2 * py + pc), pieces(me)):
            s0 = s0 if isinstance(s0, int) else pl.multiple_of(jnp.maximum(s0, 0), 128)
            out.append((to_peer, to_me, pltpu.make_async_remote_copy(
                src_ref=src.at[:, pl.ds(s0, width)], dst_ref=land.at[k - 1, :, pl.ds(d0, width)],
                send_sem=send_sems.at[k - 1], recv_sem=recv_sems.at[k - 1],
                device_id=(px, py, pc), device_id_type=MESH)))
    return out


def _w_in_scatter_start(name, src, land, early, after):
    def body(src_ref, land_ref, after_ref, send_sems, recv_sems, src_thru, land_thru, token):
        for to_peer, _, cp in _w_in_scatter_copies(src_ref, land_ref, send_sems, recv_sems, early):
            pl.when(to_peer)(cp.start)
        token[...] = jnp.zeros_like(token)

    hbm = lambda a: pltpu.HBM(a.shape, a.dtype)
    outs = pl.pallas_call(
        body, name=name,
        out_shape=(pltpu.SemaphoreType.DMA((N_DEV - 1,)), pltpu.SemaphoreType.DMA((N_DEV - 1,)),
                   hbm(src), hbm(land), jax.ShapeDtypeStruct((8, 128), F32)),
        in_specs=[HBM_SPEC, HBM_SPEC, pl.BlockSpec(memory_space=pl.ANY)],
        out_specs=(SEM_SPEC, SEM_SPEC, HBM_SPEC, HBM_SPEC, pl.BlockSpec(memory_space=pltpu.VMEM)),
        input_output_aliases={0: 2, 1: 3},
        compiler_params=pltpu.CompilerParams(has_side_effects=DATAFLOW),
    )(pltpu.with_memory_space_constraint(src, pltpu.HBM), pltpu.with_memory_space_constraint(land, pltpu.HBM), after)
    return dict(send=outs[0], recv=outs[1], src=outs[2], land=outs[3], token=outs[4], early=early, name=name)


def _w_in_scatter_wait(pending, land, after):
    def body(src_ref, land_ref, send_sems, recv_sems, after_ref, src_dead, land_out):
        for to_peer, to_me, cp in _w_in_scatter_copies(src_ref, land_ref, send_sems, recv_sems, pending["early"]):
            pl.when(to_peer)(cp.wait_send)
            pl.when(to_me)(cp.wait_recv)

    hbm = lambda a: pltpu.HBM(a.shape, a.dtype)
    outs = pl.pallas_call(
        body, name=pending["name"] + "_wait", out_shape=(hbm(pending["src"]), hbm(land)),
        in_specs=[HBM_SPEC, HBM_SPEC, SEM_SPEC, SEM_SPEC, pl.BlockSpec(memory_space=pl.ANY)],
        out_specs=(HBM_SPEC, HBM_SPEC), input_output_aliases={0: 0, 1: 1},
        compiler_params=pltpu.CompilerParams(has_side_effects=DATAFLOW),
    )(pending["src"], land, pending["send"], pending["recv"], after)
    return outs[1]


def _call_after(dep, body, args, *, in_specs, **kw):
    n_in = len(args)

    def wrapped(*refs):
        body(*refs[:n_in], *refs[n_in + 1:])

    dep_spec = pl.BlockSpec(dep.shape, lambda *_: (0,) * dep.ndim)
    return pl.pallas_call(wrapped, in_specs=list(in_specs) + [dep_spec], **kw)(*args, dep)


def _resident(shape):
    return pl.BlockSpec(shape, lambda *_: (0,) * len(shape), pipeline_mode=pl.Buffered(1))


def _proj(x2, w_in, dep):
    T, D = x2.shape
    tm = min(ROW_TILE, T)

    def body(x_ref, w_ref, o_ref, xt_ref):
        x = x_ref[...]
        xt_ref[...] = x.T.astype(BF16)
        xb = x.astype(BF16)
        for j in range(N_SEC):
            o_ref[j] = jnp.dot(xb, w_ref[:, j * D:(j + 1) * D], preferred_element_type=F32)

    return _call_after(
        dep, body, (x2, w_in), name="proj", grid=(T // tm,),
        in_specs=[pl.BlockSpec((tm, D), lambda i: (i, 0)), _resident((D, N_SEC * D))],
        out_specs=[pl.BlockSpec((N_SEC, tm, D), lambda i: (0, i, 0)), pl.BlockSpec((D, tm), lambda i: (0, i))],
        out_shape=[jax.ShapeDtypeStruct((N_SEC, T, D), F32), jax.ShapeDtypeStruct((D, T), BF16)],
        compiler_params=_params(("parallel",)))


def _chunk_cumsum(v, reverse=False):
    rows, lanes = v.shape
    x = v.reshape(rows // SUBLANES, SUBLANES, lanes)
    pos = lax.broadcasted_iota(jnp.int32, x.shape, 1)
    for sh in (1, 2, 4):
        if reverse:
            x = x + jnp.where(pos < SUBLANES - sh, pltpu.roll(x, SUBLANES - sh, 1), 0.0)
        else:
            x = x + jnp.where(pos >= sh, pltpu.roll(x, sh, 1), 0.0)
    x = x.reshape(rows // CHUNK, CHUNK // SUBLANES, SUBLANES, lanes)
    half = lax.broadcasted_iota(jnp.int32, x.shape, 1)
    if reverse:
        x = x + jnp.where(half == 0, x[:, 1:2, 0:1, :], 0.0)
    else:
        x = x + jnp.where(half == 1, x[:, 0:1, SUBLANES - 1:SUBLANES, :], 0.0)
    return x.reshape(rows, lanes)


def _hgrn_gates(q, f_pre, lb_logits):
    l0, l1 = lb_logits[0:1, :], lb_logits[1:2, :]
    mx = jnp.maximum(l0, l1)
    e0, e1 = jnp.exp(l0 - mx), jnp.exp(l1 - mx)
    lb = e0 / (e0 + e1)
    sq = _sigmoid(q)
    qf = q * sq * Q_SCALE
    sg = _sigmoid(f_pre)
    f = lb + (1.0 - lb) * sg
    k = 1.0 - f
    log_f = jnp.log(f)
    G = _chunk_cumsum(log_f)
    g_to_end = _chunk_cumsum(log_f, reverse=True) - log_f
    e_g = jnp.exp(G)
    e_ng = jnp.exp(-G)
    e_ge = jnp.exp(g_to_end)
    return dict(lb=lb, sq=sq, qf=qf, sg=sg, f=f, k=k, G=G, e_g=e_g, e_ng=e_ng, e_ge=e_ge,
                qd=qf * e_g, ki=k * e_ng, ke=k * e_ge, dec=e_g * e_ge)


def _intra_mask():
    r = lax.broadcasted_iota(jnp.int32, (GROUP, GROUP), 0)
    c = lax.broadcasted_iota(jnp.int32, (GROUP, GROUP), 1)
    return (r // CHUNK == c // CHUNK) & (c <= r)


def _chunk_outer(lhs_rows, rhs_b, out_scr, sb):
    lane = lax.broadcasted_iota(jnp.int32, (GROUP, GROUP), 1) // CHUNK
    for g in range(sb // GROUP):
        sl = slice(g * GROUP, (g + 1) * GROUP)
        lhs_t = lhs_rows[sl].T
        for cc in range(CH_PER_GROUP):
            masked = jnp.where(lane == cc, lhs_t, 0.0).astype(BF16)
            out_scr[g * CH_PER_GROUP + cc] = jnp.dot(masked, rhs_b[sl], preferred_element_type=F32)


def _hgrn_forward_blocks(cs, vs, st0s, sb, o_scr, kv_scr, st_out, dec_scr):
    nc = sb // CHUNK
    n_str = len(cs)
    mask = _intra_mask()
    bf = []
    for i, (c, v) in enumerate(zip(cs, vs)):
        qd_b, ki_b, ke_b, v_b = (c["qd"].astype(BF16), c["ki"].astype(BF16), c["ke"].astype(BF16),
                                 v.astype(BF16))
        bf.append((qd_b, ki_b, ke_b, v_b))
        for g in range(sb // GROUP):
            sl = slice(g * GROUP, (g + 1) * GROUP)
            sc = lax.dot_general(qd_b[sl], ki_b[sl], NT_DIMS, preferred_element_type=F32)
            a = jnp.where(mask, sc, 0.0).astype(BF16)
            o_scr[i, sl, :] = jnp.dot(a, v_b[sl], preferred_element_type=F32)
        _chunk_outer(v, ke_b, kv_scr.at[i], sb)
        dec_scr[i] = c["dec"]

    def rec(n, sts):
        row = pl.ds(pl.multiple_of(n * CHUNK, CHUNK), 1)
        out = []
        for i in range(n_str):
            st_out[i, 0, n] = sts[i].astype(BF16)
            out.append(sts[i] * dec_scr[i, row, :] + kv_scr[i, n])
        return tuple(out)

    ends = lax.fori_loop(0, nc, rec, tuple(st0s))

    for n in range(nc):
        rows = slice(n * CHUNK, (n + 1) * CHUNK)
        for i in range(n_str):
            o_scr[i, rows, :] += lax.dot_general(bf[i][0][rows], st_out[i, 0, n], NT_DIMS,
                                                 preferred_element_type=F32)
    return ends, bf


def _hgrn_fwd(proj5, lb_logits, gn):
    _, Bl, S, D = proj5.shape
    H = D // HEAD
    sb = min(SUB_BLOCK, S)
    nsb = S // sb
    nc = sb // CHUNK

    def body(p_ref, lbl_ref, gn_ref, ain_ref, aint_ref, o_ref, st_ref, carry, o_scr, kv_scr, dec_scr):
        @pl.when(pl.program_id(1) == 0)
        def _():
            carry[...] = jnp.zeros_like(carry)

        st0s = [carry[b] for b in range(Bl)]
        cs = [_hgrn_gates(p_ref[0, b], p_ref[1, b], lbl_ref[...]) for b in range(Bl)]
        ends, _ = _hgrn_forward_blocks(cs, [p_ref[2, b] for b in range(Bl)], st0s, sb,
                                       o_scr, kv_scr, st_ref, dec_scr)
        for b in range(Bl):
            carry[b] = ends[b]
            o = o_scr[b]
            o_ref[b] = o
            rinv = lax.rsqrt(jnp.mean(o * o, axis=-1, keepdims=True) + RMS_EPS)
            ain = o * rinv * gn_ref[...] * _sigmoid(p_ref[3, b])
            ain_ref[b] = ain.astype(BF16)
            aint_ref[b] = ain.T.astype(BF16)

    return pl.pallas_call(
        body, name="hgrn_fwd", grid=(H, nsb),
        in_specs=[pl.BlockSpec((4, Bl, sb, HEAD), lambda h, s: (0, 0, s, h)),
                  pl.BlockSpec((2, HEAD), lambda h, s: (0, h)),
                  pl.BlockSpec((1, HEAD), lambda h, s: (0, h))],
        out_specs=[pl.BlockSpec((Bl, sb, HEAD), lambda h, s: (0, s, h)),
                   pl.BlockSpec((Bl, HEAD, sb), lambda h, s: (0, h, s)),
                   pl.BlockSpec((Bl, sb, HEAD), lambda h, s: (0, s, h)),
                   pl.BlockSpec((Bl, 1, nc, HEAD, HEAD), lambda h, s: (0, h, s, 0, 0))],
        out_shape=[jax.ShapeDtypeStruct((Bl, S, D), BF16), jax.ShapeDtypeStruct((Bl, D, S), BF16),
                   jax.ShapeDtypeStruct((Bl, S, D), F32),
                   jax.ShapeDtypeStruct((Bl, H, S // CHUNK, HEAD, HEAD), BF16)],
        scratch_shapes=[pltpu.VMEM((Bl, HEAD, HEAD), F32), pltpu.VMEM((Bl, sb, HEAD), F32),
                        pltpu.VMEM((Bl, nc, HEAD, HEAD), F32), pltpu.VMEM((Bl, sb, HEAD), F32)],
        compiler_params=_params(("parallel", "arbitrary")),
    )(proj5, lb_logits, gn)


def _window_count(shape, g):
    pos = lax.broadcasted_iota(jnp.int32, shape, 0)
    return pos, jnp.minimum(pos + 1, jnp.left_shift(2, g)).astype(F32)


def _pool_fwd(proj5, w_pool):
    _, Bl, S, D = proj5.shape
    pg = D // POOL_GROUPS

    def body(v_ref, w_ref, pooled_t_ref, bp_ref):
        g = pl.program_id(1)
        v = v_ref[0, 0]
        pos, cnt = _window_count(v.shape, g)
        cur = v
        for k, sh in enumerate((1, 2, 4, 8)):
            step = lambda c, sh=sh: c + jnp.where(pos >= sh, pltpu.roll(c, sh, 0), 0.0)
            cur = step(cur) if k == 0 else lax.cond(g >= k, step, lambda c: c, cur)
        pooled = cur / cnt - v
        pooled_t_ref[...] = pooled.T.astype(BF16)
        bp_ref[0] = jnp.dot(pooled.astype(BF16), w_ref[0], preferred_element_type=F32)

    return pl.pallas_call(
        body, name="pool_fwd", grid=(Bl, POOL_GROUPS),
        in_specs=[pl.BlockSpec((1, 1, S, pg), lambda b, g: (4, b, 0, g)),
                  pl.BlockSpec((1, pg, pg), lambda b, g: (g, 0, 0))],
        out_specs=[pl.BlockSpec((pg, S), lambda b, g: (g, b)),
                   pl.BlockSpec((1, S, pg), lambda b, g: (b, 0, g))],
        out_shape=[jax.ShapeDtypeStruct((D, Bl * S), BF16), jax.ShapeDtypeStruct((Bl, S, D), F32)],
        compiler_params=_params(("parallel", "parallel")),
    )(proj5, w_pool)


def _layer_norm_fwd(r):
    mu = jnp.mean(r, axis=-1, keepdims=True)
    d = r - mu
    rs = lax.rsqrt(jnp.mean(d * d, axis=-1, keepdims=True) + LN_EPS)
    return d * rs, rs


def _layer_norm_bwd(dy_g, xhat, rs):
    return rs * (dy_g - jnp.mean(dy_g, axis=-1, keepdims=True)
                 - xhat * jnp.mean(dy_g * xhat, axis=-1, keepdims=True))


def _mix_fwd(ain, proj, bp, x2, w_a, w_out, ps, g1, b1):
    T, D = x2.shape
    tm = min(ROW_TILE, T)

    def body(ain_ref, ga_ref, gb_ref, bp_ref, x_ref, wa_ref, wo_ref, ps_ref, g1_ref, b1_ref,
             a_ref, mgt_ref, xh_ref, rs_ref, x1b_ref, x1t_ref):
        a = jnp.dot(ain_ref[...], wa_ref[...], preferred_element_type=F32)
        a_ref[...] = a
        merged = _sigmoid(ga_ref[0]) * a + _sigmoid(gb_ref[0]) * (bp_ref[...] * ps_ref[...])
        mgt_ref[...] = merged.T.astype(BF16)
        r1 = ALPHA * x_ref[...] + jnp.dot(merged.astype(BF16), wo_ref[...], preferred_element_type=F32)
        xhat, rs = _layer_norm_fwd(r1)
        xh_ref[...] = xhat
        rs_ref[...] = rs
        x1 = xhat * g1_ref[...] + b1_ref[...]
        x1b_ref[...] = x1.astype(BF16)
        x1t_ref[...] = x1.T.astype(BF16)

    row = lambda i: (i, 0)
    col = lambda i: (0, i)
    full = lambda i: (0, 0)
    return pl.pallas_call(
        body, name="mix_fwd", grid=(T // tm,),
        in_specs=[pl.BlockSpec((tm, D), row),
                  pl.BlockSpec((1, tm, D), lambda i: (5, i, 0)),
                  pl.BlockSpec((1, tm, D), lambda i: (6, i, 0)),
                  pl.BlockSpec((tm, D), row), pl.BlockSpec((tm, D), row),
                  pl.BlockSpec((D, D), full), pl.BlockSpec((D, D), full),
                  pl.BlockSpec((1, D), full), pl.BlockSpec((1, D), full), pl.BlockSpec((1, D), full)],
        out_specs=[pl.BlockSpec((tm, D), row), pl.BlockSpec((D, tm), col), pl.BlockSpec((tm, D), row),
                   pl.BlockSpec((tm, 1), row), pl.BlockSpec((tm, D), row), pl.BlockSpec((D, tm), col)],
        out_shape=[jax.ShapeDtypeStruct((T, D), F32), jax.ShapeDtypeStruct((D, T), BF16),
                   jax.ShapeDtypeStruct((T, D), F32), jax.ShapeDtypeStruct((T, 1), F32),
                   jax.ShapeDtypeStruct((T, D), BF16), jax.ShapeDtypeStruct((D, T), BF16)],
        compiler_params=_params(("parallel",)),
    )(ain, proj, proj, bp, x2, w_a, w_out, ps, g1, b1)


def _mlp_fwd(x1b, w_up, w_down, xhat1, tgt, g1, b1, g2, b2):
    T, D = xhat1.shape
    FF = w_up.shape[1]
    tm = min(MLP_ROW_TILE, T)

    def body(x_ref, wu_ref, wd_ref, xh_ref, t_ref, g1_ref, b1_ref, g2_ref, b2_ref,
             hp_ref, h_ref, dr_ref, drb_ref, drt_ref, vec_ref):
        @pl.when(pl.program_id(0) == 0)
        def _():
            vec_ref[...] = jnp.zeros_like(vec_ref)

        xb = x_ref[...]
        x1 = xh_ref[...] * g1_ref[...] + b1_ref[...]
        r2 = ALPHA * x1
        for f in range(FF // D):
            cols = slice(f * D, (f + 1) * D)
            hp = jnp.dot(xb, wu_ref[:, cols], preferred_element_type=F32)
            hp_ref[:, cols] = hp
            h = jnp.square(jnp.maximum(hp, 0.0)).astype(BF16)
            h_ref[:, cols] = h
            r2 = r2 + jnp.dot(h, wd_ref[cols, :], preferred_element_type=F32)
        xhat2, rs2 = _layer_norm_fwd(r2)
        err = xhat2 * g2_ref[...] + b2_ref[...] - t_ref[...]
        dy = err / D
        vec_ref[5:6, :] += jnp.sum(dy * xhat2, axis=0, keepdims=True)
        vec_ref[6:7, :] += jnp.sum(dy, axis=0, keepdims=True)
        vec_ref[7:8, :] += jnp.sum(0.5 * err * err / D, axis=0, keepdims=True)
        dr = _layer_norm_bwd(dy * g2_ref[...], xhat2, rs2)
        dr_ref[...] = dr
        drb_ref[...] = dr.astype(BF16)
        drt_ref[...] = dr.T.astype(BF16)

    row = lambda i: (i, 0)
    full = lambda i: (0, 0)
    return pl.pallas_call(
        body, name="mlp_fwd", grid=(T // tm,),
        in_specs=[pl.BlockSpec((tm, D), row), _resident((D, FF)), _resident((FF, D)),
                  pl.BlockSpec((tm, D), row), pl.BlockSpec((tm, D), row),
                  pl.BlockSpec((1, D), full), pl.BlockSpec((1, D), full),
                  pl.BlockSpec((1, D), full), pl.BlockSpec((1, D), full)],
        out_specs=[pl.BlockSpec((tm, FF), row), pl.BlockSpec((tm, FF), row), pl.BlockSpec((tm, D), row),
                   pl.BlockSpec((tm, D), row), pl.BlockSpec((D, tm), lambda i: (0, i)),
                   pl.BlockSpec((8, D), full)],
        out_shape=[jax.ShapeDtypeStruct((T, FF), F32), jax.ShapeDtypeStruct((T, FF), BF16),
                   jax.ShapeDtypeStruct((T, D), F32), jax.ShapeDtypeStruct((T, D), BF16),
                   jax.ShapeDtypeStruct((D, T), BF16), jax.ShapeDtypeStruct((8, D), F32)],
        compiler_params=_params(("arbitrary",)),
    )(x1b, w_up, w_down, xhat1, tgt, g1, b1, g2, b2)


def _mlp_bwd(drb, dr, hp, w_up, w_down, xhat1, rs1, g1):
    T, D = dr.shape
    FF = hp.shape[1]
    tm = min(MLP_ROW_TILE, T)

    def body(drb_ref, dr_ref, hp_ref, wu_ref, wd_ref, xh_ref, rs_ref, g1_ref,
             dhp_ref, d1_ref, d1b_ref, vec_ref):
        @pl.when(pl.program_id(0) == 0)
        def _():
            vec_ref[...] = jnp.zeros_like(vec_ref)

        drb = drb_ref[...]
        dx1 = ALPHA * dr_ref[...]
        for f in range(FF // D):
            cols = slice(f * D, (f + 1) * D)
            dh = lax.dot_general(drb, wd_ref[cols, :], NT_DIMS, preferred_element_type=F32)
            dhp = (dh * (2.0 * jnp.maximum(hp_ref[:, cols], 0.0))).astype(BF16)
            dhp_ref[:, cols] = dhp
            dx1 = dx1 + lax.dot_general(dhp, wu_ref[:, cols], NT_DIMS, preferred_element_type=F32)
        xhat = xh_ref[...]
        vec_ref[3:4, :] += jnp.sum(dx1 * xhat, axis=0, keepdims=True)
        vec_ref[4:5, :] += jnp.sum(dx1, axis=0, keepdims=True)
        d1 = _layer_norm_bwd(dx1 * g1_ref[...], xhat, rs_ref[...])
        d1_ref[...] = d1
        d1b_ref[...] = d1.astype(BF16)

    row = lambda i: (i, 0)
    full = lambda i: (0, 0)
    return pl.pallas_call(
        body, name="mlp_bwd", grid=(T // tm,),
        in_specs=[pl.BlockSpec((tm, D), row), pl.BlockSpec((tm, D), row), pl.BlockSpec((tm, FF), row),
                  _resident((D, FF)), _resident((FF, D)),
                  pl.BlockSpec((tm, D), row), pl.BlockSpec((tm, 1), row), pl.BlockSpec((1, D), full)],
        out_specs=[pl.BlockSpec((tm, FF), row), pl.BlockSpec((tm, D), row), pl.BlockSpec((tm, D), row),
                   pl.BlockSpec((8, D), full)],
        out_shape=[jax.ShapeDtypeStruct((T, FF), BF16), jax.ShapeDtypeStruct((T, D), F32),
                   jax.ShapeDtypeStruct((T, D), BF16), jax.ShapeDtypeStruct((8, D), F32)],
        compiler_params=_params(("arbitrary",)),
    )(drb, dr, hp, w_up, w_down, xhat1, rs1, g1)


def _dw(name, a_t, b, n_j, a_spec, b_spec, o_shape, o_block, o_map, transpose_out=False, dep=None,
        into=(None, None), ob_shape=None, ob_map=None):
    def body(*refs):
        a_ref, b_ref, o_ref, ob_ref = refs[0], refs[1], refs[-2], refs[-1]
        b_val = b_ref[0] if len(b_ref.shape) == 3 else b_ref[...]
        if len(a_ref.shape) == 3:
            seq = a_ref.shape[2]
            p = sum(jnp.dot(a_ref[i], b_val[i * seq:(i + 1) * seq], preferred_element_type=F32)
                    for i in range(a_ref.shape[0]))
        else:
            p = jnp.dot(a_ref[...], b_val, preferred_element_type=F32)
        if transpose_out:
            p = p.T
        p = p.reshape(o_ref.shape)
        o_ref[...] = p
        ob_ref[...] = p.astype(BF16)

    kw = dict(name=name, grid=(n_j,), in_specs=[a_spec, b_spec],
              out_specs=[pl.BlockSpec(o_block, o_map), pl.BlockSpec(o_block, ob_map or o_map)],
              out_shape=[jax.ShapeDtypeStruct(o_shape, F32), jax.ShapeDtypeStruct(ob_shape or o_shape, BF16)],
              compiler_params=_params(("parallel",)))
    args = (a_t, b)
    aliases = {}
    for out_index, arr in enumerate(into):
        if arr is not None:
            aliases[len(args)] = out_index
            args = args + (arr,)
            kw["in_specs"] = kw["in_specs"] + [pl.BlockSpec(memory_space=pl.ANY)]
    if aliases:
        kw["input_output_aliases"] = aliases
    if dep is None:
        return pl.pallas_call(body, **kw)(*args)
    return _call_after(dep, body, args, **kw)


def _mix_bwd(d1b, proj, a, bp, w_a, w_out, w_pool, ps, dep):
    T, D = a.shape
    tm = min(ROW_TILE, T)
    pg = D // POOL_GROUPS

    def body(d1b_ref, ga_ref, gb_ref, a_ref, bp_ref, wa_ref, wo_ref, wp_ref, ps_ref,
             da_ref, dbp_ref, dain_ref, dpl_ref, dg_ref, vec_ref):
        @pl.when(pl.program_id(0) == 0)
        def _():
            vec_ref[...] = jnp.zeros_like(vec_ref)

        dm = lax.dot_general(d1b_ref[...], wo_ref[...], NT_DIMS, preferred_element_type=F32)
        sa, sg = _sigmoid(ga_ref[0]), _sigmoid(gb_ref[0])
        bp_v, ps_v = bp_ref[...], ps_ref[...]
        da = (dm * sa).astype(BF16)
        db = dm * sg
        dg_ref[0] = (dm * a_ref[...] * sa * (1.0 - sa)).astype(BF16)
        dg_ref[1] = (dm * (bp_v * ps_v) * sg * (1.0 - sg)).astype(BF16)
        vec_ref[2:3, :] += jnp.sum(db * bp_v, axis=0, keepdims=True)
        dbp = (db * ps_v).astype(BF16)
        da_ref[...] = da
        dbp_ref[...] = dbp
        dain_ref[...] = lax.dot_general(da, wa_ref[...], NT_DIMS, preferred_element_type=F32)
        for g in range(POOL_GROUPS):
            cols = slice(g * pg, (g + 1) * pg)
            dpl_ref[:, cols] = lax.dot_general(dbp[:, cols], wp_ref[g], NT_DIMS,
                                               preferred_element_type=F32)

    row = lambda i: (i, 0)
    full = lambda i: (0, 0)
    return _call_after(
        dep, body, (d1b, proj, proj, a, bp, w_a, w_out, w_pool, ps), name="mix_bwd", grid=(T // tm,),
        in_specs=[pl.BlockSpec((tm, D), row),
                  pl.BlockSpec((1, tm, D), lambda i: (5, i, 0)),
                  pl.BlockSpec((1, tm, D), lambda i: (6, i, 0)),
                  pl.BlockSpec((tm, D), row), pl.BlockSpec((tm, D), row),
                  pl.BlockSpec((D, D), full), pl.BlockSpec((D, D), full),
                  pl.BlockSpec((POOL_GROUPS, pg, pg), lambda i: (0, 0, 0)),
                  pl.BlockSpec((1, D), full)],
        out_specs=[pl.BlockSpec((tm, D), row), pl.BlockSpec((tm, D), row),
                   pl.BlockSpec((tm, D), row), pl.BlockSpec((tm, D), row),
                   pl.BlockSpec((2, tm, D), lambda i: (0, i, 0)),
                   pl.BlockSpec((8, D), full)],
        out_shape=[jax.ShapeDtypeStruct((T, D), BF16), jax.ShapeDtypeStruct((T, D), BF16),
                   jax.ShapeDtypeStruct((T, D), F32), jax.ShapeDtypeStruct((T, D), F32),
                   jax.ShapeDtypeStruct((2, T, D), BF16), jax.ShapeDtypeStruct((8, D), F32)],
        compiler_params=_params(("arbitrary",)))


def _pool_bwd(dpooled3, dep):
    Bl, S, D = dpooled3.shape
    pg = D // POOL_GROUPS

    def body(dp_ref, dv_ref):
        g = pl.program_id(1)
        dp = dp_ref[0]
        pos, cnt = _window_count(dp.shape, g)
        cur = dp / cnt
        for k, sh in enumerate((1, 2, 4, 8)):
            step = lambda c, sh=sh: c + jnp.where(pos < S - sh, pltpu.roll(c, S - sh, 0), 0.0)
            cur = step(cur) if k == 0 else lax.cond(g >= k, step, lambda c: c, cur)
        dv_ref[0] = (cur - dp).astype(BF16)

    spec = pl.BlockSpec((1, S, pg), lambda b, g: (b, 0, g))
    return _call_after(
        dep, body, (dpooled3,), name="pool_bwd", grid=(Bl, POOL_GROUPS), in_specs=[spec], out_specs=spec,
        out_shape=jax.ShapeDtypeStruct((Bl, S, D), BF16),
        compiler_params=_params(("parallel", "parallel")))


def _hgrn_bwd(proj5, lb_logits, gn, dain3, o3, st_all, dep):
    _, Bl, S, D = proj5.shape
    H = D // HEAD
    sb = min(SUB_BLOCK, S)
    nsb = S // sb
    nc = sb // CHUNK
    streams = range(Bl)

    def body(p_ref, lbl_ref, gn_ref, dain_ref, o_ref, st_ref, d_ref, vec_ref,
             dcarry, kv_scr, dst_scr, dec_scr, dvi_scr, dke_scr, dqi_scr):
        s = pl.program_id(1)

        @pl.when(s == 0)
        def _():
            dcarry[...] = jnp.zeros_like(dcarry)
            vec_ref[...] = jnp.zeros_like(vec_ref)

        qs, vs, ogs = [p_ref[0, b] for b in streams], [p_ref[2, b] for b in streams], [p_ref[3, b] for b in streams]
        cs = [_hgrn_gates(qs[b], p_ref[1, b], lbl_ref[...]) for b in streams]
        bf = [(cs[b]["qd"].astype(BF16), cs[b]["ki"].astype(BF16), cs[b]["ke"].astype(BF16),
               vs[b].astype(BF16)) for b in streams]
        mask = _intra_mask()
        gn_v = gn_ref[...]
        keep = []
        for b in streams:
            qd_b, ki_b, ke_b, v_b = bf[b]
            dec_scr[b] = cs[b]["dec"]
            o = o_ref[b]
            rinv = lax.rsqrt(jnp.mean(o * o, axis=-1, keepdims=True) + RMS_EPS)
            on = o * rinv
            so = _sigmoid(ogs[b])
            dain = dain_ref[b]
            vec_ref[1:2, :] += jnp.sum(dain * on * so, axis=0, keepdims=True)
            d_og = dain * on * gn_v * so * (1.0 - so)
            d_on = dain * gn_v * so
            do = rinv * (d_on - on * jnp.mean(d_on * on, axis=-1, keepdims=True))
            do_b = do.astype(BF16)
            dv_parts, dqd_parts, dki_parts = [], [], []
            for g in range(sb // GROUP):
                sl = slice(g * GROUP, (g + 1) * GROUP)
                sc = lax.dot_general(qd_b[sl], ki_b[sl], NT_DIMS, preferred_element_type=F32)
                a = jnp.where(mask, sc, 0.0).astype(BF16)
                da = lax.dot_general(do_b[sl], v_b[sl], NT_DIMS, preferred_element_type=F32)
                da = jnp.where(mask, da, 0.0).astype(BF16)
                dv_parts.append(lax.dot_general(a, do_b[sl], TN_DIMS, preferred_element_type=F32))
                dqd_parts.append(jnp.dot(da, ki_b[sl], preferred_element_type=F32))
                dki_parts.append(lax.dot_general(da, qd_b[sl], TN_DIMS, preferred_element_type=F32))
            keep.append(dict(d_og=d_og, do_b=do_b, dv_intra=jnp.concatenate(dv_parts, axis=0),
                             dqd_intra=jnp.concatenate(dqd_parts, axis=0),
                             dki=jnp.concatenate(dki_parts, axis=0)))
            _chunk_outer(do, qd_b, kv_scr.at[b], sb)

        def rrec(i, dsts):
            n = nc - 1 - i
            row = pl.ds(pl.multiple_of(n * CHUNK, CHUNK), 1)
            out = []
            for b in streams:
                dst_scr[b, n] = dsts[b]
                out.append(dsts[b] * dec_scr[b, row, :] + kv_scr[b, n])
            return tuple(out)

        ends = lax.fori_loop(0, nc, rrec, tuple(dcarry[b] for b in streams))
        for b in streams:
            dcarry[b] = ends[b]
        for n in range(nc):
            rows = slice(n * CHUNK, (n + 1) * CHUNK)
            for b in streams:
                qd_b, ki_b, ke_b, v_b = bf[b]
                dst_b = dst_scr[b, n].astype(BF16)
                dvi_scr[b, rows, :] = lax.dot_general(ke_b[rows], dst_b, NT_DIMS, preferred_element_type=F32)
                dke_scr[b, rows, :] = jnp.dot(v_b[rows], dst_b, preferred_element_type=F32)
                dqi_scr[b, rows, :] = jnp.dot(keep[b]["do_b"][rows], st_ref[b, 0, n],
                                              preferred_element_type=F32)
        for b in streams:
            c, k = cs[b], keep[b]
            ddec = jnp.sum(dst_scr[b] * st_ref[b, 0].astype(F32), axis=1)
            dgl = jnp.broadcast_to(ddec[:, None, :], (nc, CHUNK, HEAD)).reshape(sb, HEAD) * c["dec"]
            dqd = k["dqd_intra"] + dqi_scr[b]
            dke = dke_scr[b]
            dki = k["dki"]
            t_ke = dke * c["ke"]
            dG = dqd * c["qd"] - dki * c["ki"] - t_ke
            dgl = dgl + _chunk_cumsum(t_ke) + _chunk_cumsum(t_ke, reverse=True) - t_ke
            dlogf = _chunk_cumsum(dG, reverse=True) + dgl
            dk = dki * c["e_ng"] + dke * c["e_ge"]
            df = dlogf / c["f"] - dk
            sg, sq, lb, q = c["sg"], c["sq"], c["lb"], qs[b]
            vec_ref[0:1, :] += jnp.sum(df * (1.0 - sg), axis=0, keepdims=True)
            d_ref[0, b] = (dqd * c["e_g"] * Q_SCALE * (sq + q * sq * (1.0 - sq))).astype(BF16)
            d_ref[1, b] = (df * (1.0 - lb) * sg * (1.0 - sg)).astype(BF16)
            d_ref[2, b] = (k["dv_intra"] + dvi_scr[b]).astype(BF16)
            d_ref[3, b] = k["d_og"].astype(BF16)

    rev = lambda s: nsb - 1 - s
    big = pltpu.VMEM((Bl, nc, HEAD, HEAD), F32)
    rows_f32 = pltpu.VMEM((Bl, sb, HEAD), F32)
    return _call_after(
        dep, body, (proj5, lb_logits, gn, dain3, o3, st_all), name="hgrn_bwd", grid=(H, nsb),
        in_specs=[pl.BlockSpec((4, Bl, sb, HEAD), lambda h, s: (0, 0, rev(s), h)),
                  pl.BlockSpec((2, HEAD), lambda h, s: (0, h)),
                  pl.BlockSpec((1, HEAD), lambda h, s: (0, h)),
                  pl.BlockSpec((Bl, sb, HEAD), lambda h, s: (0, rev(s), h)),
                  pl.BlockSpec((Bl, sb, HEAD), lambda h, s: (0, rev(s), h)),
                  pl.BlockSpec((Bl, 1, nc, HEAD, HEAD), lambda h, s: (0, h, rev(s), 0, 0))],
        out_specs=[pl.BlockSpec((4, Bl, sb, HEAD), lambda h, s: (0, 0, rev(s), h)),
                   pl.BlockSpec((8, HEAD), lambda h, s: (0, h))],
        out_shape=[jax.ShapeDtypeStruct((4, Bl, S, D), BF16), jax.ShapeDtypeStruct((8, D), F32)],
        scratch_shapes=[pltpu.VMEM((Bl, HEAD, HEAD), F32), big, big, rows_f32, rows_f32, rows_f32, rows_f32],
        compiler_params=_params(("parallel", "arbitrary")))


def _dx(d1, dh4, dpv, dg2, w_in, dep):
    T, D = d1.shape
    tm = min(ROW_TILE, T)

    def body(d1_ref, dh_ref, dp_ref, dg_ref, w_ref, o_ref):
        blocks = [dh_ref[0], dh_ref[1], dh_ref[2], dh_ref[3], dp_ref[...], dg_ref[0], dg_ref[1]]
        acc = ALPHA * d1_ref[...]
        for j, blk in enumerate(blocks):
            acc = acc + lax.dot_general(blk, w_ref[:, j * D:(j + 1) * D], NT_DIMS, preferred_element_type=F32)
        o_ref[...] = acc

    row = lambda i: (i, 0)
    return _call_after(
        dep, body, (d1, dh4, dpv, dg2, w_in), name="dx", grid=(T // tm,),
        in_specs=[pl.BlockSpec((tm, D), row), pl.BlockSpec((4, tm, D), lambda i: (0, i, 0)),
                  pl.BlockSpec((tm, D), row), pl.BlockSpec((2, tm, D), lambda i: (0, i, 0)),
                  _resident((D, N_SEC * D))],
        out_specs=pl.BlockSpec((tm, D), row),
        out_shape=jax.ShapeDtypeStruct((T, D), F32),
        compiler_params=_params(("parallel",)))


def _dw_in_part(name, x_t, b, sections, first_sec, into, dep, ob_shape, ob_first):
    D, T = x_t.shape
    per = D // DW_COLS
    b_spec = (pl.BlockSpec((1, T, DW_COLS), lambda j: (j // per, 0, j % per)) if b.ndim == 3
              else pl.BlockSpec((T, DW_COLS), lambda j: (0, j)))
    return _dw(name, x_t, b, sections * per, _resident((D, T)), b_spec, (D, N_SEC * D), (D, DW_COLS),
               lambda j: (0, first_sec * per + j), dep=dep, into=into, ob_shape=ob_shape,
               ob_map=lambda j: (0, ob_first * per + j))


def _dw_in_rec(x_t, dh4, dep):
    D = x_t.shape[0]
    return _dw_in_part("dw_in_rec", x_t, dh4, EARLY_SEC, 0, (None, None), dep, (D, EARLY_SEC * D), 0)


def _dw_in_rest(x_t, dpv, dg2, f32_rec, dep):
    D = x_t.shape[0]
    rest_shape = (D, (N_SEC - EARLY_SEC) * D)
    f32, bf = _dw_in_part("dw_in_gates", x_t, dg2, 2, 5, (f32_rec, None), dep, rest_shape, 1)
    return _dw_in_part("dw_in_pool", x_t, dpv, 1, 4, (f32, bf), None, rest_shape, 0)


def _adam_shard(name, me_arr, grad, land, layout, w, m, v):
    shape = layout.shape
    n_split = 4
    blk = (shape[0] // n_split,) + shape[1:]
    zeros = (0,) * (len(shape) - 1)

    def body(me_ref, g_ref, r_ref, w_ref, m_ref, v_ref, g_out, d_out, m_out, v_out):
        g = g_ref[...]
        for k in range(N_DEV - 1):
            g = g + r_ref[k].astype(F32)
        d, m2, v2 = _adamw(w_ref[...], g, m_ref[...], v_ref[...])
        g_out[...] = g
        d_out[...] = d
        m_out[...] = m2
        v_out[...] = v2

    def own(i, me_ref):
        bi = layout.block_index(me_ref[0])
        return (bi[0] * n_split + i,) + tuple(bi[1:]) if layout.kind == "row" else (i,) + tuple(bi[1:])

    plain = pl.BlockSpec(blk, lambda i, me_ref: (i,) + zeros)
    grid_spec = pltpu.PrefetchScalarGridSpec(
        num_scalar_prefetch=1, grid=(n_split,),
        in_specs=[pl.BlockSpec(blk, own),
                  pl.BlockSpec((N_DEV - 1,) + blk, lambda i, me_ref: (0, i) + zeros),
                  plain, plain, plain],
        out_specs=[plain] * 4)
    return pl.pallas_call(
        body, name=name, grid_spec=grid_spec,
        out_shape=[jax.ShapeDtypeStruct(shape, F32)] * 4,
        compiler_params=_params(("parallel",)),
    )(me_arr, grad, land, w, m, v)


def _vec_allreduce(vec):
    D = vec.shape[1]

    def body(vec_ref, tot_ref, gat, send_sems, recv_sems):
        x, y, c = _me()
        me = 4 * x + 2 * y + c
        gat[me] = vec_ref[...]
        copies = []
        for k in range(1, N_DEV):
            cp = pltpu.make_async_remote_copy(
                src_ref=vec_ref, dst_ref=gat.at[me], send_sem=send_sems.at[k - 1],
                recv_sem=recv_sems.at[k - 1], device_id=_peer(k, x, y, c), device_id_type=MESH)
            cp.start()
            copies.append(cp)
        for cp in copies:
            cp.wait()
        tot = gat[0]
        for d in range(1, N_DEV):
            tot = tot + gat[d]
        tot_ref[...] = tot

    vm = pl.BlockSpec(memory_space=pltpu.VMEM)
    return pl.pallas_call(
        body, name="vec_allreduce", out_shape=jax.ShapeDtypeStruct(vec.shape, F32),
        in_specs=[vm], out_specs=vm,
        scratch_shapes=[pltpu.VMEM((N_DEV, 8, D), F32), pltpu.SemaphoreType.DMA((N_DEV - 1,)),
                        pltpu.SemaphoreType.DMA((N_DEV - 1,))],
    )(vec)


def _vec_adam(tot, small_w, small_m, small_v):
    n = len(small_w)

    def body(*refs):
        tot = refs[0][...]
        ws, ms, vs = refs[1:1 + n], refs[1 + n:1 + 2 * n], refs[1 + 2 * n:1 + 3 * n]
        outs = refs[1 + 3 * n:]
        loss_ref, g_out, d_out = outs[0], outs[1:1 + n], outs[1 + n:1 + 2 * n]
        m_out, v_out = outs[1 + 2 * n:1 + 3 * n], outs[1 + 3 * n:1 + 4 * n]
        loss_ref[...] = jnp.broadcast_to(jnp.sum(tot[7:8, :], axis=1, keepdims=True), loss_ref.shape)
        lbl = ws[0][...]
        mx = jnp.maximum(lbl[0:1, :], lbl[1:2, :])
        e0, e1 = jnp.exp(lbl[0:1, :] - mx), jnp.exp(lbl[1:2, :] - mx)
        p0 = e0 / (e0 + e1)
        dl0 = tot[0:1, :] * p0 * (1.0 - p0)
        grads = [jnp.concatenate([dl0, -dl0], axis=0)] + [tot[r:r + 1, :] for r in range(1, n)]
        for i in range(n):
            d, m2, v2 = _adamw(ws[i][...], grads[i], ms[i][...], vs[i][...])
            g_out[i][...] = grads[i]
            d_out[i][...] = d
            m_out[i][...] = m2
            v_out[i][...] = v2

    vm = pl.BlockSpec(memory_space=pltpu.VMEM)
    shapes = [jax.ShapeDtypeStruct(w.shape, F32) for w in small_w]
    return pl.pallas_call(
        body, name="vec_adam",
        out_shape=[jax.ShapeDtypeStruct((1, 128), F32)] + shapes * 4,
        in_specs=[vm] * (1 + 3 * n), out_specs=[vm] * (1 + 4 * n),
    )(tot, *small_w, *small_m, *small_v)


def kernel(x, w_in, lb_logits, hgrn_norm_g, w_a, w_pool, pool_scale, w_out, ln1_g, ln1_b, w_up, w_down, ln2_g, ln2_b, loss_target, m_w_in, m_lb_logits, m_hgrn_norm_g, m_w_a, m_w_pool, m_pool_scale, m_w_out, m_ln1_g, m_ln1_b, m_w_up, m_w_down, m_ln2_g, m_ln2_b, v_w_in, v_lb_logits, v_hgrn_norm_g, v_w_a, v_w_pool, v_pool_scale, v_w_out, v_ln1_g, v_ln1_b, v_w_up, v_w_down, v_ln2_g, v_ln2_b):
    Bl, S, D = x.shape
    T = Bl * S
    pg = D // POOL_GROUPS
    x2 = x.reshape(T, D)
    tgt = loss_target.reshape(T, D)
    me = 4 * lax.axis_index("x") + 2 * lax.axis_index("y") + lax.axis_index("c")
    me_arr = jnp.reshape(me, (1,)).astype(jnp.int32)

    names = ["w_in", "w_a", "w_pool", "w_out", "w_up", "w_down"]
    big_w = dict(zip(names, [w_in[0], w_a[0], w_pool[0], w_out[0], w_up[0], w_down[0]]))
    big_m = dict(zip(names, [m_w_in[0], m_w_a[0], m_w_pool[0], m_w_out[0], m_w_up[0], m_w_down[0]]))
    big_v = dict(zip(names, [v_w_in[0], v_w_a[0], v_w_pool[0], v_w_out[0], v_w_up[0], v_w_down[0]]))
    kinds = dict(w_in="col", w_a="row", w_pool="pool", w_out="row", w_up="col", w_down="row")
    lay = {nm: _Sharded(kinds[nm], big_w[nm].shape) for nm in names}
    wb = {nm: big_w[nm].astype(BF16) for nm in names}

    (w_in_f,) = _all_gather("ag_w_in", [wb["w_in"]], [lay["w_in"]])
    def gather_start(name, nms, after):
        return _exchange_start(name, [wb[nm] for nm in nms], [lax.empty(lay[nm].full_shape, BF16) for nm in nms],
                               src_at=lambda w, ref, peer: ref,
                               dst_at=lambda w, ref, mine, k: lay[nms[w]].at(ref, mine), after=after, own=True)

    ag_mix = gather_start("ag_mix", ["w_a", "w_pool", "w_out"], w_in_f)
    ag_mlp = gather_start("ag_mlp", ["w_up", "w_down"], ag_mix["token"])

    proj, x_t = _proj(x2, w_in_f, ag_mlp["token"])
    proj5 = proj.reshape(N_SEC, Bl, S, D)
    ain3, ain_t, o3, st_all = _hgrn_fwd(proj5, lb_logits, hgrn_norm_g)
    w_a_f, w_pool_f, w_out_f = _exchange_wait(ag_mix, ain3)
    pooled_t, bp3 = _pool_fwd(proj5, w_pool_f)
    ain, bp = ain3.reshape(T, D), bp3.reshape(T, D)
    a, merged_t, xhat1, rs1, x1b, x1_t = _mix_fwd(ain, proj, bp, x2, w_a_f, w_out_f, pool_scale, ln1_g, ln1_b)
    w_up_f, w_down_f = _exchange_wait(ag_mlp, x1b)
    hp, h, dr2, dr2b, dr2_t, vec_mlp = _mlp_fwd(x1b, w_up_f, w_down_f, xhat1, tgt, ln1_g, ln1_b, ln2_g, ln2_b)

    def scatter_start(name, nms, grads_b, after):
        lands = [lax.empty((N_DEV - 1,) + lay[nm].shape, BF16) for nm in nms]
        return _exchange_start(name, grads_b, lands,
                               src_at=lambda w, ref, peer: lay[nms[w]].at(ref, peer),
                               dst_at=lambda w, ref, mine, k: ref.at[k - 1], after=after)

    dhp, dr1, dr1b, vec_ln1 = _mlp_bwd(dr2b, dr2, hp, w_up_f, w_down_f, xhat1, rs1, ln1_g)
    FF = 4 * D
    whole_t = _resident((D, T))
    cols_b = pl.BlockSpec((T, DW_COLS), lambda j: (0, j))
    cols_o = ((D, DW_COLS), lambda j: (0, j))
    gw, gwb = {}, {}
    gw["w_down"], gwb["w_down"] = _dw(
        "dw_down", dr2_t, h, FF // DW_COLS, whole_t, cols_b, (FF, D), (DW_COLS, D), lambda j: (j, 0),
        transpose_out=True)
    rs_down = scatter_start("rs_w_down", ["w_down"], [gwb["w_down"]], gw["w_down"])
    gw["w_up"], gwb["w_up"] = _dw("dw_up", x1_t, dhp, FF // DW_COLS, whole_t, cols_b, (D, FF), *cols_o,
                                  dep=rs_down["token"])
    rs_up = scatter_start("rs_w_up", ["w_up"], [gwb["w_up"]], gw["w_up"])
    da_b, dbp_b, dain, dpooled, dg2, vec_mix = _mix_bwd(dr1b, proj, a, bp, w_a_f, w_out_f, w_pool_f, pool_scale,
                                                        rs_up["token"])
    gw["w_out"], gwb["w_out"] = _dw("dw_out", merged_t, dr1b, D // DW_COLS, whole_t, cols_b, (D, D), *cols_o)
    gw["w_a"], gwb["w_a"] = _dw("dw_a", ain_t, da_b, D // DW_COLS, _resident((Bl, D, S)), cols_b, (D, D), *cols_o)
    gw["w_pool"], gwb["w_pool"] = _dw(
        "dw_pool", pooled_t, dbp_b, POOL_GROUPS, pl.BlockSpec((pg, T), lambda j: (j, 0)),
        pl.BlockSpec((T, pg), lambda j: (0, j)), (POOL_GROUPS, pg, pg), (1, pg, pg), lambda j: (j, 0, 0))
    mid = ["w_out", "w_a", "w_pool"]
    rs_mid = scatter_start("rs_w_mid", mid, [gwb[nm] for nm in mid], gw["w_pool"])
    dh4, vec_hgrn = _hgrn_bwd(proj5, lb_logits, hgrn_norm_g, dain.reshape(Bl, S, D), o3, st_all, rs_mid["token"])
    dh4 = dh4.reshape(4, T, D)
    vec_tot = _vec_allreduce(vec_mlp + vec_ln1 + vec_mix + vec_hgrn)
    gw_in_rec, gwb_in_rec = _dw_in_rec(x_t, dh4, vec_tot)
    land_in = lax.empty((N_DEV - 1,) + lay["w_in"].shape, BF16)
    rs_in_rec = _w_in_scatter_start("rs_w_in_rec", gwb_in_rec, land_in, False, gw_in_rec)
    dpv = _pool_bwd(dpooled.reshape(Bl, S, D), rs_in_rec["token"]).reshape(T, D)
    gw["w_in"], gwb_in_rest = _dw_in_rest(x_t, dpv, dg2, gw_in_rec, rs_in_rec["token"])
    rs_in_rest = _w_in_scatter_start("rs_w_in_rest", gwb_in_rest, rs_in_rec["land"], True, gw["w_in"])
    grad_x2 = _dx(dr1, dh4, dpv, dg2, w_in_f, rs_in_rest["token"])
    grad_x = grad_x2.reshape(Bl, S, D)

    small_names =["lb_logits", "hgrn_norm_g", "pool_scale", "ln1_g", "ln1_b", "ln2_g", "ln2_b"]
    small_w = [lb_logits, hgrn_norm_g, pool_scale, ln1_g, ln1_b, ln2_g, ln2_b]
    small_m = [m_lb_logits, m_hgrn_norm_g, m_pool_scale, m_ln1_g, m_ln1_b, m_ln2_g, m_ln2_b]
    small_v = [v_lb_logits, v_hgrn_norm_g, v_pool_scale, v_ln1_g, v_ln1_b, v_ln2_g, v_ln2_b]
    res = _vec_adam(vec_tot, small_w, small_m, small_v)
    loss = res[0][0, 0]
    n = len(small_w)
    small = {nm: (res[1 + i], res[1 + n + i], res[1 + 2 * n + i], res[1 + 3 * n + i])
             for i, nm in enumerate(small_names)}

    big, last = {}, grad_x2

    def adam(nm, land):
        outs = _adam_shard("adam_" + nm, me_arr, gw[nm], land, lay[nm], big_w[nm], big_m[nm], big_v[nm])
        big[nm] = tuple(t[None] for t in outs)
        return outs[0]

    for pend, nms in ((rs_down, ["w_down"]), (rs_up, ["w_up"]), (rs_mid, mid)):
        for nm, land in zip(nms, _exchange_wait(pend, last)):
            last = adam(nm, land)
    land_in = _w_in_scatter_wait(rs_in_rec, rs_in_rest["land"], last)
    adam("w_in", _w_in_scatter_wait(rs_in_rest, land_in, res[0]))

    order = ["w_in", "lb_logits", "hgrn_norm_g", "w_a", "w_pool", "pool_scale", "w_out", "ln1_g", "ln1_b",
             "w_up", "w_down", "ln2_g", "ln2_b"]
    allp = {**big, **small}
    out = [loss, grad_x]
    for part in range(4):
        out += [allp[nm][part] for nm in order]
    return tuple(out)
```
